```python
import math
import jax, jax.numpy as jnp
from jax import lax
import numpy as np

D_MODEL = 1024
BATCH = 32
SEQ = 256
DEPTH = 2
DEC_BATCH = 2
DEC_SEQ = 1024
PAST_LEN = 512

GRID_W = 64
HEAD_DIM = 64
D_RWKV = 384
H_RWKV = D_RWKV // HEAD_DIM
D_DIFF = 384
H_DIFF = D_DIFF // HEAD_DIM
D_QK = HEAD_DIM // 2
D_FNET = D_MODEL - D_RWKV - D_DIFF
FNET_GROUPS = 4
FNET_GROUP_DIM = D_FNET // FNET_GROUPS
LORA_W = 32
LORA_A = 32
LORA_G = 64
N_RWKV_IN = 3 * D_RWKV + 2 * LORA_W + 2 * LORA_A + LORA_G
N_DIFF_IN = 3 * D_DIFF
D_IN = N_RWKV_IN + N_DIFF_IN + D_FNET
RWKV_SPLITS = (D_RWKV, 2 * D_RWKV, 3 * D_RWKV, 3 * D_RWKV + 2 * LORA_W, 3 * D_RWKV + 2 * LORA_W + 2 * LORA_A)
D_FF = ((8 * D_MODEL + 3 * 256 - 1) // (3 * 256)) * 256
ROPE_PAIRS = D_QK // 4
ROPE_BASE = 10000.0
Q_BLOCK = 128
RMS_EPS = 1e-6
GN_EPS = 64e-5
SUBLN_EPS = 1e-5
DECAY_SCALE = math.exp(-0.5)

kernel_name = 'hybrid_rwkv7_diffattn_fnet_dit_step'

F32 = jnp.float32


def rmsnorm(x, g, eps=RMS_EPS):
    xf = x.astype(F32)
    y = xf * lax.rsqrt(jnp.mean(xf * xf, axis=-1, keepdims=True) + eps)
    return (y * g.astype(F32)).astype(x.dtype)


def modulation(cond, w_mod, b_mod):
    m = jax.nn.silu(cond) @ w_mod + b_mod
    return jnp.split(m[:, None, :], 6, axis=-1)


def token_shift(x, mu):
    prev = jnp.pad(x[:, :-1], ((0, 0), (1, 0), (0, 0)))
    nxt = jnp.pad(x[:, 1:], ((0, 0), (0, 1), (0, 0)))
    return x + mu[0] * (prev - x) + mu[1] * (nxt - x)


def wkv_scan(s0, r, w, k, v, kk, a, reverse):
    def step(S, inp):
        r_t, w_t, k_t, v_t, kk_t, a_t = inp
        sa = jnp.einsum('bhvk,bhk->bhv', S, -kk_t)
        S = S * w_t[:, :, None, :] + sa[..., None] * (kk_t * a_t)[:, :, None, :] + v_t[..., None] * k_t[:, :, None, :]
        return S, jnp.einsum('bhvk,bhk->bhv', S, r_t)
    xs = tuple(jnp.moveaxis(t, 1, 0) for t in (r, w, k, v, kk, a))
    S, ys = lax.scan(step, s0, xs, reverse=reverse)
    return S, jnp.moveaxis(ys, 0, 1)


def rwkv_time_mix(u, s0, p, l):
    B, T, _ = u.shape
    u = token_shift(u.astype(F32), p['shift_mu'][l])
    r, k, v, wd, ad, gd = jnp.split(u, RWKV_SPLITS, axis=-1)
    wd = wd.reshape(B, T, 2, LORA_W)
    ad = ad.reshape(B, T, 2, LORA_A)
    w = jnp.exp(-DECAY_SCALE * jax.nn.sigmoid(p['decay_w0'][l] + jnp.einsum('btdr,drc->btdc', jnp.tanh(wd), p['decay_up'][l])))
    a = jax.nn.sigmoid(p['iclr_a0'][l] + jnp.einsum('btdr,drc->btdc', ad, p['iclr_up'][l]))
    g = jax.nn.sigmoid(gd) @ p['gate_up'][l]
    heads = lambda t: t.reshape(B, T, H_RWKV, HEAD_DIM)
    kk = heads(k * p['k_k'][l])
    kk = kk / jnp.maximum(jnp.sqrt(jnp.sum(kk * kk, axis=-1, keepdims=True)), 1e-12)
    k_dir = k[:, :, None, :] * (1.0 + (a - 1.0) * p['k_a'][l])
    rh, vh = heads(r), heads(v)
    k_f, k_b = heads(k_dir[:, :, 0]), heads(k_dir[:, :, 1])
    S_f, y_f = wkv_scan(s0[:, 0], rh, heads(w[:, :, 0]), k_f, vh, kk, heads(a[:, :, 0]), False)
    S_b, y_b = wkv_scan(s0[:, 1], rh, heads(w[:, :, 1]), k_b, vh, kk, heads(a[:, :, 1]), True)
    y = y_f + y_b
    mu = jnp.mean(y, axis=-1, keepdims=True)
    var = jnp.mean(jnp.square(y - mu), axis=-1, keepdims=True)
    y = ((y - mu) * lax.rsqrt(var + GN_EPS)).reshape(B, T, D_RWKV) * p['lnx_g'][l] + p['lnx_b'][l]
    bonus = jnp.sum(rh * (k_f + k_b) * p['r_k'][l], axis=-1, keepdims=True) * vh
    out = (y + bonus.reshape(B, T, D_RWKV)) * g
    return out, jnp.stack([S_f, S_b], axis=1)


def axial_rope(T):
    rows = T // GRID_W
    row = jnp.repeat(jnp.arange(rows), GRID_W).astype(F32)
    col = jnp.tile(jnp.arange(GRID_W), rows).astype(F32)
    inv = 1.0 / (ROPE_BASE ** (jnp.arange(ROPE_PAIRS, dtype=F32) / ROPE_PAIRS))
    ang = jnp.stack([row[:, None] * inv, col[:, None] * inv], axis=1)
    return jnp.cos(ang), jnp.sin(ang)


def apply_rope(x, cos, sin):
    xs = x.astype(F32).reshape(*x.shape[:-1], 2, 2, ROPE_PAIRS)
    x1, x2 = xs[..., 0, :], xs[..., 1, :]
    c, s = cos[:, None, None], sin[:, None, None]
    out = jnp.stack([x1 * c - x2 * s, x2 * c + x1 * s], axis=-2)
    return out.reshape(x.shape).astype(x.dtype)


def diff_attention(q, k, v, lam_params, lam_init, subln_g):
    B, Tq, H = q.shape[:3]
    lp = lam_params.astype(F32)
    lam = jnp.exp(jnp.sum(lp[0] * lp[1])) - jnp.exp(jnp.sum(lp[2] * lp[3])) + lam_init
    kf, vf = k.astype(F32), v.astype(F32)
    qb = jnp.moveaxis((q.astype(F32) * D_QK ** -0.5).reshape(B, Tq // Q_BLOCK, Q_BLOCK, H, 2, D_QK), 1, 0)

    def block(q_blk):
        pr = jax.nn.softmax(jnp.einsum('bqhmd,bkhmd->bhmqk', q_blk, kf), axis=-1)
        return jnp.einsum('bhqk,bkhd->bqhd', pr[:, :, 0] - lam * pr[:, :, 1], vf)

    o = jnp.moveaxis(lax.map(block, qb), 0, 1).reshape(B, Tq, H, HEAD_DIM)
    o = o * lax.rsqrt(jnp.mean(o * o, axis=-1, keepdims=True) + SUBLN_EPS) * subln_g * (1.0 - lam_init)
    return o.reshape(B, Tq, H * HEAD_DIM)


def fourier_mix(u):
    B, T, _ = u.shape
    f = jnp.fft.fft2(u.astype(F32).reshape(B, T, FNET_GROUPS, FNET_GROUP_DIM), axes=(1, 3), norm='ortho').real
    return f.reshape(B, T, D_FNET)


def mixer_sublayer(h, p, l, s0, ctx_k, ctx_v, rope):
    B, T, _ = h.shape
    u_rwkv, u_diff, u_fnet = jnp.split(h @ p['w_in'][l], [N_RWKV_IN, N_RWKV_IN + N_DIFF_IN], axis=-1)
    y_rwkv, s_new = rwkv_time_mix(u_rwkv, s0, p, l)
    q, k, v = jnp.split(u_diff, 3, axis=-1)
    q = q.reshape(B, T, H_DIFF, 2, D_QK)
    k = k.reshape(B, T, H_DIFF, 2, D_QK)
    v = v.reshape(B, T, H_DIFF, HEAD_DIM)
    if rope is None:
        keys, vals = k, v
    else:
        q = apply_rope(q, *rope)
        k = apply_rope(k, *rope)
        keys = jnp.concatenate([ctx_k.astype(k.dtype), k], axis=1)
        vals = jnp.concatenate([ctx_v.astype(v.dtype), v], axis=1)
    lam_init = 0.8 - 0.6 * math.exp(-0.3 * l)
    y_diff = diff_attention(q, keys, vals, p['diff_lambda'][l], lam_init, p['subln_g'][l])
    y_fnet = fourier_mix(u_fnet)
    y = jnp.concatenate([y_rwkv, y_diff, y_fnet], axis=-1).astype(h.dtype) @ p['w_out'][l]
    return y, s_new, k, v


def swiglu(h, w_in, w_out):
    gt, up = jnp.split(h @ w_in, 2, axis=-1)
    return (jax.nn.silu(gt) * up) @ w_out


def trunk_layer(x, cond, p, l, s0, ctx_k, ctx_v, rope):
    sh1, sc1, g1, sh2, sc2, g2 = modulation(cond, p['w_mod'][l], p['b_mod'][l])
    h = rmsnorm(x, p['norm1_g'][l]) * (1 + sc1) + sh1
    y, s_new, k, v = mixer_sublayer(h, p, l, s0, ctx_k, ctx_v, rope)
    x = x + g1 * y
    h = rmsnorm(x, p['norm2_g'][l]) * (1 + sc2) + sh2
    x = x + g2 * swiglu(h, p['w_ffn_in'][l], p['w_ffn_out'][l])
    return x, s_new, k, v


def setup_inputs(seed: int = 0) -> dict:
    key = jax.random.key(seed)
    ks = iter(jax.random.split(key, 40))
    nrm = lambda shape, s: jax.random.normal(next(ks), shape, F32) * s
    L = DEPTH
    return {
        'x_prompt': nrm((BATCH, SEQ, D_MODEL), 1.0),
        'x_sample': nrm((DEC_BATCH, DEC_SEQ, D_MODEL), 1.0),
        'c': nrm((DEC_BATCH, D_MODEL), 1.0),
        'state_rwkv': nrm((DEC_BATCH, L, 2, H_RWKV, HEAD_DIM, HEAD_DIM), 0.5),
        'cache_diff_k': nrm((DEC_BATCH, L, PAST_LEN, H_DIFF, 2, D_QK), 1.0),
        'cache_diff_v': nrm((DEC_BATCH, L, PAST_LEN, H_DIFF, HEAD_DIM), 1.0),
        'c_ctx': nrm((D_MODEL,), 1.0),
        'norm1_g': 1.0 + nrm((L, D_MODEL), 0.02),
        'norm2_g': 1.0 + nrm((L, D_MODEL), 0.02),
        'final_norm_g': 1.0 + nrm((D_MODEL,), 0.02),
        'w_mod': nrm((L, D_MODEL, 6 * D_MODEL), 0.5 * D_MODEL ** -0.5),
        'b_mod': nrm((L, 6 * D_MODEL), 0.02),
        'w_in': nrm((L, D_MODEL, D_IN), D_MODEL ** -0.5),
        'w_out': nrm((L, D_MODEL, D_MODEL), D_MODEL ** -0.5),
        'shift_mu': jax.random.uniform(next(ks), (L, 2, N_RWKV_IN), F32, 0.0, 0.4),
        'decay_w0': nrm((L, 2, D_RWKV), 1.0),
        'decay_up': nrm((L, 2, LORA_W, D_RWKV), 0.1 * LORA_W ** -0.5),
        'iclr_a0': nrm((L, 2, D_RWKV), 0.5),
        'iclr_up': nrm((L, 2, LORA_A, D_RWKV), 0.1 * LORA_A ** -0.5),
        'gate_up': nrm((L, LORA_G, D_RWKV), LORA_G ** -0.5),
        'k_k': 1.0 + nrm((L, D_RWKV), 0.1),
        'k_a': 1.0 + nrm((L, D_RWKV), 0.1),
        'r_k': nrm((L, H_RWKV, HEAD_DIM), 0.1),
        'lnx_g': 1.0 + nrm((L, D_RWKV), 0.02),
        'lnx_b': nrm((L, D_RWKV), 0.02),
        'diff_lambda': nrm((L, 4, D_QK), 0.1),
        'subln_g': 1.0 + nrm((L, HEAD_DIM), 0.02),
        'w_ffn_in': nrm((L, D_MODEL, 2 * D_FF), D_MODEL ** -0.5),
        'w_ffn_out': nrm((L, D_FF, D_MODEL), D_FF ** -0.5),
    }


def reference(x_prompt, x_sample, c, state_rwkv, cache_diff_k, cache_diff_v, c_ctx, norm1_g, norm2_g, final_norm_g,
              w_mod, b_mod, w_in, w_out, shift_mu, decay_w0, decay_up, iclr_a0, iclr_up, gate_up, k_k, k_a, r_k,
              lnx_g, lnx_b, diff_lambda, subln_g, w_ffn_in, w_ffn_out):
    p = dict(norm1_g=norm1_g, norm2_g=norm2_g, w_mod=w_mod, b_mod=b_mod, w_in=w_in, w_out=w_out,
             shift_mu=shift_mu, decay_w0=decay_w0, decay_up=decay_up, iclr_a0=iclr_a0, iclr_up=iclr_up,
             gate_up=gate_up, k_k=k_k, k_a=k_a, r_k=r_k, lnx_g=lnx_g, lnx_b=lnx_b, diff_lambda=diff_lambda,
             subln_g=subln_g, w_ffn_in=w_ffn_in, w_ffn_out=w_ffn_out)

    xp = x_prompt
    s_zero = jnp.zeros((x_prompt.shape[0], 2, H_RWKV, HEAD_DIM, HEAD_DIM), F32)
    states, ks, vs = [], [], []
    for l in range(DEPTH):
        xp, s_new, k, v = trunk_layer(xp, c_ctx[None, :], p, l, s_zero, None, None, None)
        states.append(s_new)
        ks.append(k)
        vs.append(v)
    y_prompt = rmsnorm(xp, final_norm_g)
    new_state_rwkv = jnp.stack(states, axis=1)
    new_cache_diff_k = jnp.stack(ks, axis=1)
    new_cache_diff_v = jnp.stack(vs, axis=1)

    rope = axial_rope(x_sample.shape[1])
    xs = x_sample
    for l in range(DEPTH):
        xs, _, _, _ = trunk_layer(xs, c, p, l, state_rwkv[:, l].astype(F32), cache_diff_k[:, l], cache_diff_v[:, l], rope)
    y_sample = rmsnorm(xs, final_norm_g)
    return (y_prompt, y_sample, new_state_rwkv, new_cache_diff_k, new_cache_diff_v)
```

```python
import functools
import math

import numpy as np
import jax
import jax.numpy as jnp
from jax import lax
from jax.experimental import pallas as pl
from jax.experimental.pallas import tpu as pltpu

F32 = jnp.float32
BF16 = jnp.bfloat16

D_MODEL = 1024
DEPTH = 2
GRID_W = 64
HEAD_DIM = 64
D_RWKV = 384
H_RWKV = D_RWKV // HEAD_DIM
D_DIFF = 384
H_DIFF = D_DIFF // HEAD_DIM
D_QK = HEAD_DIM // 2
D_FNET = D_MODEL - D_RWKV - D_DIFF
FNET_GROUPS = 4
FNET_GROUP_DIM = D_FNET // FNET_GROUPS
LORA_W = 32
LORA_A = 32
LORA_G = 64
N_RWKV_IN = 3 * D_RWKV + 2 * LORA_W + 2 * LORA_A + LORA_G
N_DIFF_IN = 3 * D_DIFF
D_FF = ((8 * D_MODEL + 3 * 256 - 1) // (3 * 256)) * 256
ROPE_PAIRS = D_QK // 4
ROPE_BASE = 10000.0
RMS_EPS = 1e-6
GN_EPS = 64e-5
SUBLN_EPS = 1e-5
DECAY_SCALE = math.exp(-0.5)

LANES = 128
N_RWKV_PAD = 11 * LANES
D_IN_PAD = N_RWKV_PAD + N_DIFF_IN + D_FNET
N_PAIR = H_RWKV // 2
CHUNK = 64
MOD_ROWS = 8
VMEM_LIMIT = 56 * 1024 * 1024


def _cp(n_axes=1):
    return pltpu.CompilerParams(dimension_semantics=("arbitrary",) * n_axes,
                                vmem_limit_bytes=VMEM_LIMIT)


def _bdot(a, b):
    return jnp.dot(a.astype(BF16), b.astype(BF16), preferred_element_type=F32)


def _bdot_nt(a, b):
    return lax.dot_general(a.astype(BF16), b.astype(BF16), (((1,), (1,)), ((), ())),
                           preferred_element_type=F32)


def _split2(x):
    hi = x.astype(BF16)
    lo = (x - hi.astype(F32)).astype(BF16)
    return hi, lo


def _split3(x):
    hi = x.astype(BF16)
    r1 = x - hi.astype(F32)
    mid = r1.astype(BF16)
    lo = (r1 - mid.astype(F32)).astype(BF16)
    return hi, mid, lo


def _dot_x3(a, b):
    a_hi, a_lo = _split2(a)
    b_hi, b_lo = _split2(b)
    d = functools.partial(jnp.dot, preferred_element_type=F32)
    return d(a_hi, b_hi) + d(a_lo, b_hi) + d(a_hi, b_lo)


def _dot_exact_rhs(c_bf16, x):
    d = functools.partial(jnp.dot, preferred_element_type=F32)
    hi, mid, lo = _split3(x)
    return d(c_bf16, hi) + d(c_bf16, mid) + d(c_bf16, lo)


def _dot_exact_lhs(x, c_bf16):
    d = functools.partial(jnp.dot, preferred_element_type=F32)
    hi, mid, lo = _split3(x)
    return d(hi, c_bf16) + d(mid, c_bf16) + d(lo, c_bf16)


def _rms(x, g):
    return x * lax.rsqrt(jnp.mean(x * x, axis=-1, keepdims=True) + RMS_EPS) * g


def _mod_body(c_ref, w_ref, b_ref, o_ref):
    c = c_ref[...]
    a = c * jax.nn.sigmoid(c)
    o_ref[0] = _dot_x3(a, w_ref[0]) + b_ref[0]


def _modulation(cond, w_mod, b_mod):
    n_layers, _, n_out = w_mod.shape
    tn = 1536
    return pl.pallas_call(
        _mod_body,
        grid=(n_layers, n_out // tn),
        in_specs=[pl.BlockSpec((MOD_ROWS, D_MODEL), lambda l, j: (0, 0)),
                  pl.BlockSpec((1, D_MODEL, tn), lambda l, j: (l, 0, j)),
                  pl.BlockSpec((1, 1, tn), lambda l, j: (l, 0, j))],
        out_specs=pl.BlockSpec((1, MOD_ROWS, tn), lambda l, j: (l, 0, j)),
        out_shape=jax.ShapeDtypeStruct((n_layers, MOD_ROWS, n_out), F32),
        compiler_params=_cp(2),
    )(cond, w_mod, b_mod.reshape(n_layers, 1, n_out))


def _mod_spec(layer, col, row_fn):
    return pl.BlockSpec((1, 1, D_MODEL), lambda i: (layer * MOD_ROWS + row_fn(i), 0, col))


def _inproj_body(x_ref, g_ref, sh_ref, sc_ref, w_ref, ur_ref, uq_ref, uf_ref):
    h = _rms(x_ref[...], g_ref[...]) * (1.0 + sc_ref[0]) + sh_ref[0]
    u = jnp.dot(h.astype(BF16), w_ref[...], preferred_element_type=F32)
    ur_ref[...] = u[:, :N_RWKV_PAD]
    uq_ref[...] = u[:, N_RWKV_PAD:N_RWKV_PAD + N_DIFF_IN]
    uf_ref[...] = u[:, N_RWKV_PAD + N_DIFF_IN:]


def _inproj(x, mod, layer, row_fn, g, w, tm):
    n_tok = x.shape[0]
    row = lambda i: (i, 0)
    const = lambda i: (0, 0)
    return pl.pallas_call(
        _inproj_body,
        grid=(n_tok // tm,),
        in_specs=[pl.BlockSpec((tm, D_MODEL), row),
                  pl.BlockSpec((1, D_MODEL), const),
                  _mod_spec(layer, 0, row_fn),
                  _mod_spec(layer, 1, row_fn),
                  pl.BlockSpec((D_MODEL, D_IN_PAD), const)],
        out_specs=[pl.BlockSpec((tm, N_RWKV_PAD), row),
                   pl.BlockSpec((tm, N_DIFF_IN), row),
                   pl.BlockSpec((tm, D_FNET), row)],
        out_shape=[jax.ShapeDtypeStruct((n_tok, N_RWKV_PAD), F32),
                   jax.ShapeDtypeStruct((n_tok, N_DIFF_IN), F32),
                   jax.ShapeDtypeStruct((n_tok, D_FNET), F32)],
        compiler_params=_cp(1),
    )(x, g.reshape(1, D_MODEL), mod, mod, w)


def _lane_masks():
    lane = lax.broadcasted_iota(jnp.int32, (1, LANES), 1)
    return (lane < HEAD_DIM).astype(F32), (lane >= HEAD_DIM).astype(F32)


def _rwkv_unit(rev, kk, r, v, kd, b, cum, lw, tot, s_prev, m0, m1):
    def bd(x):
        return jnp.concatenate([x * m0, x * m1], axis=0)

    p_in = jnp.exp(cum)
    p_ex = jnp.exp(cum - lw)
    p_inv = jnp.exp(-cum)
    p_rem = jnp.exp(tot - cum)
    ab = -kk * p_ex
    rb = r * p_in
    bb = b * p_inv
    kb = kd * p_inv
    bt = b * p_rem
    kt = kd * p_rem

    row = lax.broadcasted_iota(jnp.int32, (CHUNK, LANES), 0)
    col = lax.broadcasted_iota(jnp.int32, (CHUNK, LANES), 1) & (CHUNK - 1)
    strict = (col > row) if rev else (col < row)
    incl = (col >= row) if rev else (col <= row)
    eye = (col == row).astype(F32)

    bbd = bd(bb)
    kbd = bd(kb)
    m_ab = jnp.where(strict, _bdot_nt(ab, bbd), 0.0)
    m_ak = jnp.where(strict, _bdot_nt(ab, kbd), 0.0)
    m_rb = jnp.where(incl, _bdot_nt(rb, bbd), 0.0)
    m_rk = jnp.where(incl, _bdot_nt(rb, kbd), 0.0)

    n = m_ab
    t = eye + n
    n = _bdot(n, bd(n))
    for j in range(5):
        t = t + _bdot(t, bd(n))
        if j < 4:
            n = _bdot(n, bd(n))

    vbd = bd(v)
    w1 = _bdot(t, bd(ab))
    w2 = _bdot(t, bd(_bdot(m_ak, vbd)))
    u = _bdot_nt(w1, s_prev) + w2
    y = _bdot_nt(rb, s_prev) + _bdot(m_rb, bd(u)) + _bdot(m_rk, vbd)

    uv = jnp.concatenate([u, v], axis=0)
    bk = jnp.concatenate([bt, kt], axis=0)
    r2 = lax.broadcasted_iota(jnp.int32, (LANES, LANES), 0) < HEAD_DIM
    c2 = lax.broadcasted_iota(jnp.int32, (LANES, LANES), 1) < HEAD_DIM
    s_new = s_prev * jnp.exp(tot) + jnp.where(r2 == c2, _bdot(uv.T, bk), 0.0)
    return y, s_new


def _rwkv_body(seq_len, u_ref, s0_ref, mu_ref, wdec_ref, wicl_ref, wg_ref, w0a0_ref, vec_ref,
               bo_ref, tril_ref, triu_ref, y_ref, sfin_ref,
               r_s, v_s, kk_s, g_s, bon_s, kd_s, b_s, lw_s, y_s, st_s):
    n_chunk = seq_len // CHUNK
    blk = 256
    k_k = vec_ref[0:1, :]
    k_a = vec_ref[1:2, :]
    r_k = vec_ref[2:3, :]
    lnx_g = vec_ref[3:4, :]
    lnx_b = vec_ref[4:5, :]
    bo = bo_ref[...]

    def headsum(x):
        return _dot_exact_lhs(x, bo)

    mu0 = mu_ref[0:1, :]
    mu1 = mu_ref[1:2, :]
    rid = lax.broadcasted_iota(jnp.int32, (blk, 1), 0)
    for j in range(seq_len // blk):
        r0 = j * blk
        cur = u_ref[r0:r0 + blk, :]
        first = jnp.zeros((1, N_RWKV_PAD), F32) if j == 0 else u_ref[r0 - 1:r0, :]
        last = (jnp.zeros((1, N_RWKV_PAD), F32) if r0 + blk == seq_len
                else u_ref[r0 + blk:r0 + blk + 1, :])
        prev = jnp.where(rid == 0, first, pltpu.roll(cur, 1, axis=0))
        nxt = jnp.where(rid == blk - 1, last, pltpu.roll(cur, blk - 1, axis=0))
        xs = cur + mu0 * (prev - cur) + mu1 * (nxt - cur)
        r = xs[:, 0:D_RWKV]
        k = xs[:, D_RWKV:2 * D_RWKV]
        v = xs[:, 2 * D_RWKV:3 * D_RWKV]
        lora = xs[:, 3 * D_RWKV:3 * D_RWKV + LANES]
        gd = xs[:, 3 * D_RWKV + LANES:3 * D_RWKV + 2 * LANES]
        dec = _bdot(jnp.tanh(lora), wdec_ref[...])
        icl = _bdot(lora, wicl_ref[...])
        g = _bdot(jax.nn.sigmoid(gd), wg_ref[...])
        logw = -DECAY_SCALE * jax.nn.sigmoid(w0a0_ref[0:1, :] + dec)
        a = jax.nn.sigmoid(w0a0_ref[1:2, :] + icl)
        kk = k * k_k
        kk = kk / jnp.maximum(jnp.sqrt(headsum(kk * kk)), 1e-12)
        a_f = a[:, :D_RWKV]
        a_b = a[:, D_RWKV:]
        kd_f = k * (1.0 + (a_f - 1.0) * k_a)
        kd_b = k * (1.0 + (a_b - 1.0) * k_a)
        rows = slice(r0, r0 + blk)
        r_s[rows, :] = r
        v_s[rows, :] = v
        kk_s[rows, :] = kk
        g_s[rows, :] = g
        bon_s[rows, :] = headsum(r * (kd_f + kd_b) * r_k) * v
        kd_s[0, rows, :] = kd_f
        kd_s[1, rows, :] = kd_b
        b_s[0, rows, :] = kk * a_f
        b_s[1, rows, :] = kk * a_b
        lw_s[0, rows, :] = logw[:, :D_RWKV]
        lw_s[1, rows, :] = logw[:, D_RWKV:]

    st_s[...] = s0_ref[0]
    y_s[...] = jnp.zeros_like(y_s)
    m0, m1 = _lane_masks()

    def chunk_step(i, carry):
        for d in range(2):
            c = i if d == 0 else n_chunk - 1 - i
            rows = pl.ds(pl.multiple_of(c * CHUNK, CHUNK), CHUNK)
            lw = lw_s[d, rows, :]
            tri = tril_ref[...] if d == 0 else triu_ref[...]
            cum = _dot_exact_rhs(tri, lw)
            tot = jnp.sum(lw, axis=0, keepdims=True)
            kk = kk_s[rows, :]
            r = r_s[rows, :]
            v = v_s[rows, :]
            kd = kd_s[d, rows, :]
            b = b_s[d, rows, :]
            ys = []
            for p in range(N_PAIR):
                sl = slice(p * LANES, (p + 1) * LANES)
                y, s_new = _rwkv_unit(d == 1, kk[:, sl], r[:, sl], v[:, sl], kd[:, sl], b[:, sl],
                                      cum[:, sl], lw[:, sl], tot[:, sl], st_s[d, p], m0, m1)
                st_s[d, p] = s_new
                ys.append(y)
            y_s[rows, :] = y_s[rows, :] + jnp.concatenate(ys, axis=1)
        return carry

    lax.fori_loop(0, n_chunk, chunk_step, 0)
    sfin_ref[0] = st_s[...]

    for j in range(seq_len // blk):
        rows = slice(j * blk, (j + 1) * blk)
        y = y_s[rows, :]
        mean = headsum(y) * (1.0 / HEAD_DIM)
        yc = y - mean
        var = headsum(yc * yc) * (1.0 / HEAD_DIM)
        yn = yc * lax.rsqrt(var + GN_EPS) * lnx_g + lnx_b
        y_ref[rows, :] = (yn + bon_s[rows, :]) * g_s[rows, :]


def _rwkv(u_r, s0, seq_len, wts):
    n_seq = u_r.shape[0] // seq_len
    const2 = lambda b: (0, 0)
    st_spec = pl.BlockSpec((1, 2, N_PAIR, LANES, LANES), lambda b: (b, 0, 0, 0, 0))
    tok = pltpu.VMEM((seq_len, D_RWKV), F32)
    tok2 = pltpu.VMEM((2, seq_len, D_RWKV), F32)
    in_specs = [pl.BlockSpec((seq_len, N_RWKV_PAD), lambda b: (b, 0)), st_spec]
    in_specs += [pl.BlockSpec(w.shape, const2) for w in wts]
    return pl.pallas_call(
        functools.partial(_rwkv_body, seq_len),
        grid=(n_seq,),
        in_specs=in_specs,
        out_specs=[pl.BlockSpec((seq_len, D_RWKV), lambda b: (b, 0)), st_spec],
        out_shape=[jax.ShapeDtypeStruct((n_seq * seq_len, D_RWKV), F32),
                   jax.ShapeDtypeStruct((n_seq, 2, N_PAIR, LANES, LANES), F32)],
        scratch_shapes=[tok] * 5 + [tok2] * 3 + [tok,
                        pltpu.VMEM((2, N_PAIR, LANES, LANES), F32)],
        compiler_params=_cp(1),
    )(u_r, s0, *wts)


def _rwkv_weights(p, l):
    z = functools.partial(jnp.zeros, dtype=F32)
    mu = jnp.concatenate([p['shift_mu'][l], z((2, N_RWKV_PAD - N_RWKV_IN))], axis=1)
    wdec = z((LANES, 2 * D_RWKV))
    wdec = wdec.at[0:LORA_W, :D_RWKV].set(p['decay_up'][l, 0])
    wdec = wdec.at[LORA_W:2 * LORA_W, D_RWKV:].set(p['decay_up'][l, 1])
    wicl = z((LANES, 2 * D_RWKV))
    wicl = wicl.at[2 * LORA_W:2 * LORA_W + LORA_A, :D_RWKV].set(p['iclr_up'][l, 0])
    wicl = wicl.at[2 * LORA_W + LORA_A:2 * LORA_W + 2 * LORA_A, D_RWKV:].set(p['iclr_up'][l, 1])
    wg = z((LANES, D_RWKV)).at[0:LORA_G].set(p['gate_up'][l])
    w0a0 = jnp.stack([p['decay_w0'][l].reshape(-1), p['iclr_a0'][l].reshape(-1)])
    vec = jnp.stack([p['k_k'][l], p['k_a'][l], p['r_k'][l].reshape(-1), p['lnx_g'][l], p['lnx_b'][l],
                     z((D_RWKV,)), z((D_RWKV,)), z((D_RWKV,))])
    head = np.arange(D_RWKV) // HEAD_DIM
    bo = jnp.asarray(head[:, None] == head[None, :], BF16)
    idx = np.arange(CHUNK)
    tril = jnp.asarray(idx[None, :] <= idx[:, None], BF16)
    triu = jnp.asarray(idx[None, :] >= idx[:, None], BF16)
    return [mu, wdec.astype(BF16), wicl.astype(BF16), wg.astype(BF16), w0a0, vec, bo, tril, triu]


def _rope(x, cos, sin):
    lane = lax.broadcasted_iota(jnp.int32, (1, LANES), 1)
    first_half = (lane & ROPE_PAIRS) == 0
    partner = jnp.where(first_half, pltpu.roll(x, LANES - ROPE_PAIRS, axis=1),
                        pltpu.roll(x, ROPE_PAIRS, axis=1))
    return x * cos + partner * sin


def _attn_body(has_ctx, lam_init, *refs):
    if has_ctx:
        (q_ref, k_ref, v_ref, lp_ref, sg_ref, kc_ref, vc_ref, cq_ref, sq_ref, ck_ref, sk_ref,
         o_ref) = refs
    else:
        q_ref, k_ref, v_ref, lp_ref, sg_ref, o_ref = refs
    q = q_ref[...]
    k = k_ref[...]
    v = v_ref[...]
    if has_ctx:
        q = _rope(q, cq_ref[...], sq_ref[...])
        k = _rope(k, ck_ref[...], sk_ref[...])
        k = jnp.concatenate([kc_ref[0], k], axis=0)
        v = jnp.concatenate([vc_ref[0], v], axis=0)
    q = q * (D_QK ** -0.5)
    lp = lp_ref[...]
    lam = (jnp.exp(jnp.sum(lp[0:1] * lp[1:2], axis=-1, keepdims=True))
           - jnp.exp(jnp.sum(lp[2:3] * lp[3:4], axis=-1, keepdims=True)) + lam_init)
    lane = lax.broadcasted_iota(jnp.int32, (1, LANES), 1)
    kb = k.astype(BF16)
    out = jnp.zeros(q.shape, F32)
    for h in range(2):
        head = (lane >= h * HEAD_DIM) & (lane < (h + 1) * HEAD_DIM)
        probs = []
        for m in range(2):
            lo = h * HEAD_DIM + m * D_QK
            sel = (lane >= lo) & (lane < lo + D_QK)
            s = _bdot_nt(jnp.where(sel, q, 0.0), kb)
            e = jnp.exp(s - jnp.max(s, axis=-1, keepdims=True))
            probs.append(e / jnp.sum(e, axis=-1, keepdims=True))
        pr = probs[0] - lam * probs[1]
        o = _bdot(pr, jnp.where(head, v, 0.0))
        ms = jnp.sum(o * o, axis=-1, keepdims=True) * (1.0 / HEAD_DIM)
        out = out + o * lax.rsqrt(ms + SUBLN_EPS)
    o_ref[...] = out * sg_ref[...] * (1.0 - lam_init)


def _attention(u_qkv, seq_len, lam_init, lp, sg, ctx=None):
    n_tok = u_qkv.shape[0]
    n_seq = n_tok // seq_len
    tq = 256
    nq = seq_len // tq
    n_col = D_DIFF // LANES
    in_specs = [pl.BlockSpec((tq, LANES), lambda b, p, i: (b * nq + i, p)),
                pl.BlockSpec((seq_len, LANES), lambda b, p, i: (b, n_col + p)),
                pl.BlockSpec((seq_len, LANES), lambda b, p, i: (b, 2 * n_col + p)),
                pl.BlockSpec(lp.shape, lambda b, p, i: (0, 0)),
                pl.BlockSpec((1, LANES), lambda b, p, i: (0, 0))]
    args = [u_qkv, u_qkv, u_qkv, lp, sg]
    if ctx is not None:
        kc, vc, cos, sin = ctx
        past = kc.shape[1]
        in_specs += [pl.BlockSpec((1, past, LANES), lambda b, p, i: (b, 0, p)),
                     pl.BlockSpec((1, past, LANES), lambda b, p, i: (b, 0, p)),
                     pl.BlockSpec((tq, LANES), lambda b, p, i: (i, 0)),
                     pl.BlockSpec((tq, LANES), lambda b, p, i: (i, 0)),
                     pl.BlockSpec((seq_len, LANES), lambda b, p, i: (0, 0)),
                     pl.BlockSpec((seq_len, LANES), lambda b, p, i: (0, 0))]
        args += [kc, vc, cos, sin, cos, sin]
    return pl.pallas_call(
        functools.partial(_attn_body, ctx is not None, lam_init),
        grid=(n_seq, n_col, nq),
        in_specs=in_specs,
        out_specs=pl.BlockSpec((tq, LANES), lambda b, p, i: (b * nq + i, p)),
        out_shape=jax.ShapeDtypeStruct((n_tok, D_DIFF), F32),
        compiler_params=_cp(3),
    )(*args)


def _rope_tables(seq_len):
    t = jnp.arange(seq_len)
    pos = jnp.stack([(t // GRID_W).astype(F32), (t % GRID_W).astype(F32)], axis=1)
    inv = 1.0 / (ROPE_BASE ** (jnp.arange(ROPE_PAIRS, dtype=F32) / ROPE_PAIRS))
    ang = pos[:, :, None] * inv
    d = np.arange(LANES) % D_QK
    axis = d // (2 * ROPE_PAIRS)
    second = (d % (2 * ROPE_PAIRS)) // ROPE_PAIRS
    idx = d % ROPE_PAIRS
    cos = jnp.cos(ang)[:, axis, idx]
    sin = jnp.sin(ang)[:, axis, idx] * jnp.asarray(np.where(second == 1, 1.0, -1.0), F32)
    return cos, sin


def _fnet_body(x_ref, cth_ref, ctl_ref, sth_ref, stl_ref, cch_ref, ccl_ref, sch_ref, scl_ref, o_ref):
    d = functools.partial(jnp.dot, preferred_element_type=F32)
    x_hi, x_lo = _split2(x_ref[...])
    xc = d(x_hi, cch_ref[...]) + d(x_lo, cch_ref[...]) + d(x_hi, ccl_ref[...])
    xs = d(x_hi, sch_ref[...]) + d(x_lo, sch_ref[...]) + d(x_hi, scl_ref[...])
    c_hi, c_lo = _split2(xc)
    s_hi, s_lo = _split2(xs)
    yc = d(cth_ref[...], c_hi) + d(cth_ref[...], c_lo) + d(ctl_ref[...], c_hi)
    ys = d(sth_ref[...], s_hi) + d(sth_ref[...], s_lo) + d(stl_ref[...], s_hi)
    o_ref[...] = yc - ys


def _dft_consts(n, block=1):
    idx = np.arange(n)
    ang = 2.0 * np.pi * ((idx[:, None] * idx[None, :]) % n) / n
    out = []
    for m in (np.cos(ang) / np.sqrt(n), np.sin(ang) / np.sqrt(n)):
        m = np.kron(np.eye(block), m).astype(np.float32)
        hi = jnp.asarray(m, F32).astype(BF16)
        lo = (jnp.asarray(m, F32) - hi.astype(F32)).astype(BF16)
        out += [hi, lo]
    return out


def _fnet(u_f, seq_len):
    n_tok = u_f.shape[0]
    consts = _dft_consts(seq_len) + _dft_consts(FNET_GROUP_DIM, FNET_GROUPS)
    const = lambda b: (0, 0)
    return pl.pallas_call(
        _fnet_body,
        grid=(n_tok // seq_len,),
        in_specs=[pl.BlockSpec((seq_len, D_FNET), lambda b: (b, 0))]
        + [pl.BlockSpec(c.shape, const) for c in consts],
        out_specs=pl.BlockSpec((seq_len, D_FNET), lambda b: (b, 0)),
        out_shape=jax.ShapeDtypeStruct((n_tok, D_FNET), F32),
        compiler_params=_cp(1),
    )(u_f, *consts)


def _ffn_body(final, yr_ref, yd_ref, yf_ref, x_ref, g1_ref, sh2_ref, sc2_ref, g2_ref, n2_ref, fg_ref,
              wo_ref, wi_ref, wf_ref, o_ref):
    y = (_bdot(yr_ref[...], wo_ref[0:D_RWKV, :])
         + _bdot(yd_ref[...], wo_ref[D_RWKV:D_RWKV + D_DIFF, :])
         + _bdot(yf_ref[...], wo_ref[D_RWKV + D_DIFF:, :]))
    x = x_ref[...] + g1_ref[0] * y
    h = _rms(x, n2_ref[...]) * (1.0 + sc2_ref[0]) + sh2_ref[0]
    z = jnp.dot(h.astype(BF16), wi_ref[...], preferred_element_type=F32)
    gate = z[:, :D_FF]
    act = gate * jax.nn.sigmoid(gate) * z[:, D_FF:]
    x = x + g2_ref[0] * _bdot(act, wf_ref[...])
    o_ref[...] = _rms(x, fg_ref[...]) if final else x


def _ffn(y_r, y_d, y_f, x, mod, layer, row_fn, n2, fg, wo, wi, wf, final, tm):
    n_tok = x.shape[0]
    row = lambda i: (i, 0)
    const = lambda i: (0, 0)
    return pl.pallas_call(
        functools.partial(_ffn_body, final),
        grid=(n_tok // tm,),
        in_specs=[pl.BlockSpec((tm, D_RWKV), row),
                  pl.BlockSpec((tm, D_DIFF), row),
                  pl.BlockSpec((tm, D_FNET), row),
                  pl.BlockSpec((tm, D_MODEL), row),
                  _mod_spec(layer, 2, row_fn),
                  _mod_spec(layer, 3, row_fn),
                  _mod_spec(layer, 4, row_fn),
                  _mod_spec(layer, 5, row_fn),
                  pl.BlockSpec((1, D_MODEL), const),
                  pl.BlockSpec((1, D_MODEL), const),
                  pl.BlockSpec(wo.shape, const),
                  pl.BlockSpec(wi.shape, const),
                  pl.BlockSpec(wf.shape, const)],
        out_specs=pl.BlockSpec((tm, D_MODEL), row),
        out_shape=jax.ShapeDtypeStruct((n_tok, D_MODEL), F32),
        compiler_params=_cp(1),
    )(y_r, y_d, y_f, x, mod, mod, mod, mod, n2.reshape(1, D_MODEL), fg.reshape(1, D_MODEL), wo, wi, wf)


def _block_diag_state(s):
    b = s.shape[0]
    s = s.reshape(b, 2, N_PAIR, 2, HEAD_DIM, HEAD_DIM)
    eye = jnp.eye(2, dtype=s.dtype)
    s = s[:, :, :, :, :, None, :] * eye[None, None, None, :, None, :, None]
    return s.reshape(b, 2, N_PAIR, LANES, LANES)


def _diag_blocks(s):
    b = s.shape[0]
    s = s.reshape(b, 2, N_PAIR, 2, HEAD_DIM, 2, HEAD_DIM)
    s = jnp.stack([s[:, :, :, 0, :, 0, :], s[:, :, :, 1, :, 1, :]], axis=3)
    return s.reshape(b, 2, H_RWKV, HEAD_DIM, HEAD_DIM)


def kernel(x_prompt, x_sample, c, state_rwkv, cache_diff_k, cache_diff_v, c_ctx, norm1_g, norm2_g, final_norm_g, w_mod, b_mod, w_in, w_out, shift_mu, decay_w0, decay_up, iclr_a0, iclr_up, gate_up, k_k, k_a, r_k, lnx_g, lnx_b, diff_lambda, subln_g, w_ffn_in, w_ffn_out):
    p = dict(shift_mu=shift_mu, decay_w0=decay_w0, decay_up=decay_up, iclr_a0=iclr_a0, iclr_up=iclr_up,
             gate_up=gate_up, k_k=k_k, k_a=k_a, r_k=r_k, lnx_g=lnx_g, lnx_b=lnx_b)
    n_ctx, t_ctx, _ = x_prompt.shape
    n_dec, t_dec, _ = x_sample.shape
    past = cache_diff_k.shape[2]

    cond = jnp.concatenate([c_ctx[None, :], c, jnp.zeros((MOD_ROWS - 1 - n_dec, D_MODEL), F32)], axis=0)
    mod = _modulation(cond, w_mod, b_mod).reshape(DEPTH * MOD_ROWS, 1, 6 * D_MODEL)

    tm_in, tm_ffn = 512, 256
    streams = [
        dict(x=x_prompt.reshape(n_ctx * t_ctx, D_MODEL), t=t_ctx, n=n_ctx,
             row_in=lambda i: 0, row_ffn=lambda i: 0),
        dict(x=x_sample.reshape(n_dec * t_dec, D_MODEL), t=t_dec, n=n_dec,
             row_in=lambda i: 1 + i // (t_dec // tm_in), row_ffn=lambda i: 1 + i // (t_dec // tm_ffn)),
    ]
    cos, sin = _rope_tables(t_dec)
    states, ks, vs = [], [], []
    for l in range(DEPTH):
        w_in_l = jnp.concatenate(
            [w_in[l, :, :N_RWKV_IN], jnp.zeros((D_MODEL, N_RWKV_PAD - N_RWKV_IN), F32),
             w_in[l, :, N_RWKV_IN:]], axis=1).astype(BF16)
        wo = w_out[l].astype(BF16)
        wi = w_ffn_in[l].astype(BF16)
        wf = w_ffn_out[l].astype(BF16)
        rw = _rwkv_weights(p, l)
        lam_init = 0.8 - 0.6 * math.exp(-0.3 * l)
        sg = jnp.tile(subln_g[l], 2).reshape(1, LANES)
        for si, st in enumerate(streams):
            u_r, u_qkv, u_f = _inproj(st['x'], mod, l, st['row_in'], norm1_g[l], w_in_l, tm_in)
            if si == 0:
                s0 = jnp.zeros((st['n'], 2, N_PAIR, LANES, LANES), F32)
                attn_ctx = None
            else:
                s0 = _block_diag_state(state_rwkv[:, l].astype(F32))
                attn_ctx = (cache_diff_k[:, l].reshape(n_dec, past, D_DIFF).astype(F32),
                            cache_diff_v[:, l].reshape(n_dec, past, D_DIFF).astype(F32), cos, sin)
            y_r, s_fin = _rwkv(u_r, s0, st['t'], rw)
            y_d = _attention(u_qkv, st['t'], lam_init, diff_lambda[l], sg, attn_ctx)
            y_f = _fnet(u_f, st['t'])
            st['x'] = _ffn(y_r, y_d, y_f, st['x'], mod, l, st['row_ffn'], norm2_g[l], final_norm_g,
                           wo, wi, wf, l == DEPTH - 1, tm_ffn)
            if si == 0:
                states.append(_diag_blocks(s_fin))
                ks.append(u_qkv[:, D_DIFF:2 * D_DIFF].reshape(n_ctx, t_ctx, H_DIFF, 2, D_QK))
                vs.append(u_qkv[:, 2 * D_DIFF:].reshape(n_ctx, t_ctx, H_DIFF, HEAD_DIM))
    y_prompt = streams[0]['x'].reshape(n_ctx, t_ctx, D_MODEL)
    y_sample = streams[1]['x'].reshape(n_dec, t_dec, D_MODEL)
    return (y_prompt, y_sample, jnp.stack(states, axis=1), jnp.stack(ks, axis=1), jnp.stack(vs, axis=1))
```

```python
import functools
import math

import numpy as np
import jax
import jax.numpy as jnp
from jax import lax
from jax.experimental import pallas as pl
from jax.experimental.pallas import tpu as pltpu

F32 = jnp.float32
BF16 = jnp.bfloat16

D_MODEL = 1024
DEPTH = 2
GRID_W = 64
HEAD_DIM = 64
D_RWKV = 384
H_RWKV = D_RWKV // HEAD_DIM
D_DIFF = 384
H_DIFF = D_DIFF // HEAD_DIM
D_QK = HEAD_DIM // 2
D_FNET = D_MODEL - D_RWKV - D_DIFF
FNET_GROUPS = 4
FNET_GROUP_DIM = D_FNET // FNET_GROUPS
LORA_W = 32
LORA_A = 32
LORA_G = 64
N_RWKV_IN = 3 * D_RWKV + 2 * LORA_W + 2 * LORA_A + LORA_G
N_DIFF_IN = 3 * D_DIFF
D_FF = ((8 * D_MODEL + 3 * 256 - 1) // (3 * 256)) * 256
ROPE_PAIRS = D_QK // 4
ROPE_BASE = 10000.0
RMS_EPS = 1e-6
GN_EPS = 64e-5
SUBLN_EPS = 1e-5
DECAY_SCALE = math.exp(-0.5)

LANES = 128
N_RWKV_PAD = 11 * LANES
D_IN_PAD = N_RWKV_PAD + N_DIFF_IN + D_FNET
N_PAIR = H_RWKV // 2
CHUNK = 64
MOD_ROWS = 8
VMEM_LIMIT = 56 * 1024 * 1024


def _cp(n_axes=1):
    return pltpu.CompilerParams(dimension_semantics=("arbitrary",) * n_axes,
                                vmem_limit_bytes=VMEM_LIMIT)


def _bdot(a, b):
    return jnp.dot(a.astype(BF16), b.astype(BF16), preferred_element_type=F32)


def _bdot_nt(a, b):
    return lax.dot_general(a.astype(BF16), b.astype(BF16), (((1,), (1,)), ((), ())),
                           preferred_element_type=F32)


def _split2(x):
    hi = x.astype(BF16)
    lo = (x - hi.astype(F32)).astype(BF16)
    return hi, lo


def _split3(x):
    hi = x.astype(BF16)
    r1 = x - hi.astype(F32)
    mid = r1.astype(BF16)
    lo = (r1 - mid.astype(F32)).astype(BF16)
    return hi, mid, lo


def _dot_x3(a, b):
    a_hi, a_lo = _split2(a)
    b_hi, b_lo = _split2(b)
    d = functools.partial(jnp.dot, preferred_element_type=F32)
    return d(a_hi, b_hi) + d(a_lo, b_hi) + d(a_hi, b_lo)


def _dot_exact_rhs(c_bf16, x):
    d = functools.partial(jnp.dot, preferred_element_type=F32)
    hi, mid, lo = _split3(x)
    return d(c_bf16, hi) + d(c_bf16, mid) + d(c_bf16, lo)


def _dot_exact_lhs(x, c_bf16):
    d = functools.partial(jnp.dot, preferred_element_type=F32)
    hi, mid, lo = _split3(x)
    return d(hi, c_bf16) + d(mid, c_bf16) + d(lo, c_bf16)


def _rms(x, g):
    return x * lax.rsqrt(jnp.mean(x * x, axis=-1, keepdims=True) + RMS_EPS) * g


def _mod_body(c_ref, w_ref, b_ref, o_ref):
    c = c_ref[...]
    a = c * jax.nn.sigmoid(c)
    o_ref[0] = _dot_x3(a, w_ref[0]) + b_ref[0]


def _modulation(cond, w_mod, b_mod):
    n_layers, _, n_out = w_mod.shape
    tn = 1536
    return pl.pallas_call(
        _mod_body,
        grid=(n_layers, n_out // tn),
        in_specs=[pl.BlockSpec((MOD_ROWS, D_MODEL), lambda l, j: (0, 0)),
                  pl.BlockSpec((1, D_MODEL, tn), lambda l, j: (l, 0, j)),
                  pl.BlockSpec((1, 1, tn), lambda l, j: (l, 0, j))],
        out_specs=pl.BlockSpec((1, MOD_ROWS, tn), lambda l, j: (l, 0, j)),
        out_shape=jax.ShapeDtypeStruct((n_layers, MOD_ROWS, n_out), F32),
        compiler_params=_cp(2),
    )(cond, w_mod, b_mod.reshape(n_layers, 1, n_out))


def _mod_spec(layer, col, row_fn):
    return pl.BlockSpec((1, 1, D_MODEL), lambda i: (layer * MOD_ROWS + row_fn(i), 0, col))


def _inproj_body(x_ref, g_ref, sh_ref, sc_ref, w_ref, ur_ref, uq_ref, uf_ref):
    h = _rms(x_ref[...], g_ref[...]) * (1.0 + sc_ref[0]) + sh_ref[0]
    u = jnp.dot(h.astype(BF16), w_ref[...], preferred_element_type=F32)
    ur_ref[...] = u[:, :N_RWKV_PAD]
    uq_ref[...] = u[:, N_RWKV_PAD:N_RWKV_PAD + N_DIFF_IN]
    uf_ref[...] = u[:, N_RWKV_PAD + N_DIFF_IN:]


def _inproj(x, mod, layer, row_fn, g, w, tm):
    n_tok = x.shape[0]
    row = lambda i: (i, 0)
    const = lambda i: (0, 0)
    return pl.pallas_call(
        _inproj_body,
        grid=(n_tok // tm,),
        in_specs=[pl.BlockSpec((tm, D_MODEL), row),
                  pl.BlockSpec((1, D_MODEL), const),
                  _mod_spec(layer, 0, row_fn),
                  _mod_spec(layer, 1, row_fn),
                  pl.BlockSpec((D_MODEL, D_IN_PAD), const)],
        out_specs=[pl.BlockSpec((tm, N_RWKV_PAD), row),
                   pl.BlockSpec((tm, N_DIFF_IN), row),
                   pl.BlockSpec((tm, D_FNET), row)],
        out_shape=[jax.ShapeDtypeStruct((n_tok, N_RWKV_PAD), F32),
                   jax.ShapeDtypeStruct((n_tok, N_DIFF_IN), F32),
                   jax.ShapeDtypeStruct((n_tok, D_FNET), F32)],
        compiler_params=_cp(1),
    )(x, g.reshape(1, D_MODEL), mod, mod, w)


def _lane_masks():
    lane = lax.broadcasted_iota(jnp.int32, (1, LANES), 1)
    return (lane < HEAD_DIM).astype(F32), (lane >= HEAD_DIM).astype(F32)


def _rwkv_units(units, m0, m1):
    def bd(x):
        return jnp.concatenate([x * m0, x * m1], axis=0)

    row = lax.broadcasted_iota(jnp.int32, (CHUNK, LANES), 0)
    col = lax.broadcasted_iota(jnp.int32, (CHUNK, LANES), 1) & (CHUNK - 1)
    eye = (col == row).astype(F32)
    r2 = lax.broadcasted_iota(jnp.int32, (LANES, LANES), 0) < HEAD_DIM
    c2 = lax.broadcasted_iota(jnp.int32, (LANES, LANES), 1) < HEAD_DIM
    rng = range(len(units))

    pre = []
    for rev, kk, r, v, kd, b, cum, lw, tot, s_prev in units:
        p_inv = jnp.exp(-cum)
        p_rem = jnp.exp(tot - cum)
        ab = -kk * jnp.exp(cum - lw)
        rb = r * jnp.exp(cum)
        strict = (col > row) if rev else (col < row)
        incl = (col >= row) if rev else (col <= row)
        pre.append(dict(ab=ab, rb=rb, vbd=bd(v), strict=strict, incl=incl,
                        lhs=jnp.concatenate([ab, rb], axis=0),
                        rhs=jnp.concatenate([bd(b * p_inv), bd(kd * p_inv)], axis=0),
                        bk=jnp.concatenate([b * p_rem, kd * p_rem], axis=0)))

    mm = [_bdot_nt(q['lhs'], q['rhs']) for q in pre]
    m_ab = [jnp.where(pre[i]['strict'], mm[i][:CHUNK, :LANES], 0.0) for i in rng]
    m_ak = [jnp.where(pre[i]['strict'], mm[i][:CHUNK, LANES:], 0.0) for i in rng]
    m_r = [jnp.concatenate([jnp.where(pre[i]['incl'], mm[i][CHUNK:, :LANES], 0.0),
                            jnp.where(pre[i]['incl'], mm[i][CHUNK:, LANES:], 0.0)], axis=1) for i in rng]
    mv = [_bdot(m_ak[i], pre[i]['vbd']) for i in rng]

    t = [eye + m_ab[i] for i in rng]
    n = [_bdot(m_ab[i], bd(m_ab[i])) for i in rng]
    for _ in range(4):
        x = [_bdot(jnp.concatenate([t[i], n[i]], axis=0), bd(n[i])) for i in rng]
        t = [t[i] + x[i][:CHUNK] for i in rng]
        n = [x[i][CHUNK:] for i in rng]
    t = [t[i] + _bdot(t[i], bd(n[i])) for i in rng]

    w = [_bdot(t[i], jnp.concatenate([bd(pre[i]['ab']), bd(mv[i])], axis=1)) for i in rng]
    xs = [_bdot_nt(jnp.concatenate([w[i][:, :LANES], pre[i]['rb']], axis=0), units[i][9]) for i in rng]
    u = [xs[i][:CHUNK] + w[i][:, LANES:] for i in rng]
    y = [xs[i][CHUNK:] + _bdot(m_r[i], jnp.concatenate([bd(u[i]), pre[i]['vbd']], axis=0)) for i in rng]
    z = [_bdot(jnp.concatenate([u[i], units[i][3]], axis=0).T, pre[i]['bk']) for i in rng]
    s_new = [units[i][9] * jnp.exp(units[i][8]) + jnp.where(r2 == c2, z[i], 0.0) for i in rng]
    return y, s_new


def _rwkv_body(seq_len, u_ref, s0_ref, mu_ref, wdec_ref, wicl_ref, wg_ref, w0a0_ref, vec_ref,
               bo_ref, tril_ref, triu_ref, y_ref, sfin_ref,
               r_s, v_s, kk_s, g_s, bon_s, kd_s, b_s, lw_s, y_s, st_s):
    n_chunk = seq_len // CHUNK
    blk = 256
    k_k = vec_ref[0:1, :]
    k_a = vec_ref[1:2, :]
    r_k = vec_ref[2:3, :]
    lnx_g = vec_ref[3:4, :]
    lnx_b = vec_ref[4:5, :]
    bo = bo_ref[...]

    def headsum(x):
        return _dot_exact_lhs(x, bo)

    mu0 = mu_ref[0:1, :]
    mu1 = mu_ref[1:2, :]
    rid = lax.broadcasted_iota(jnp.int32, (blk, 1), 0)
    for j in range(seq_len // blk):
        r0 = j * blk
        cur = u_ref[r0:r0 + blk, :]
        first = jnp.zeros((1, N_RWKV_PAD), F32) if j == 0 else u_ref[r0 - 1:r0, :]
        last = (jnp.zeros((1, N_RWKV_PAD), F32) if r0 + blk == seq_len
                else u_ref[r0 + blk:r0 + blk + 1, :])
        prev = jnp.where(rid == 0, first, pltpu.roll(cur, 1, axis=0))
        nxt = jnp.where(rid == blk - 1, last, pltpu.roll(cur, blk - 1, axis=0))
        xs = cur + mu0 * (prev - cur) + mu1 * (nxt - cur)
        r = xs[:, 0:D_RWKV]
        k = xs[:, D_RWKV:2 * D_RWKV]
        v = xs[:, 2 * D_RWKV:3 * D_RWKV]
        lora = xs[:, 3 * D_RWKV:3 * D_RWKV + LANES]
        gd = xs[:, 3 * D_RWKV + LANES:3 * D_RWKV + 2 * LANES]
        dec = _bdot(jnp.tanh(lora), wdec_ref[...])
        icl = _bdot(lora, wicl_ref[...])
        g = _bdot(jax.nn.sigmoid(gd), wg_ref[...])
        logw = -DECAY_SCALE * jax.nn.sigmoid(w0a0_ref[0:1, :] + dec)
        a = jax.nn.sigmoid(w0a0_ref[1:2, :] + icl)
        kk = k * k_k
        kk = kk / jnp.maximum(jnp.sqrt(headsum(kk * kk)), 1e-12)
        a_f = a[:, :D_RWKV]
        a_b = a[:, D_RWKV:]
        kd_f = k * (1.0 + (a_f - 1.0) * k_a)
        kd_b = k * (1.0 + (a_b - 1.0) * k_a)
        rows = slice(r0, r0 + blk)
        r_s[rows, :] = r
        v_s[rows, :] = v
        kk_s[rows, :] = kk
        g_s[rows, :] = g
        bon_s[rows, :] = headsum(r * (kd_f + kd_b) * r_k) * v
        kd_s[0, rows, :] = kd_f
        kd_s[1, rows, :] = kd_b
        b_s[0, rows, :] = kk * a_f
        b_s[1, rows, :] = kk * a_b
        lw_s[0, rows, :] = logw[:, :D_RWKV]
        lw_s[1, rows, :] = logw[:, D_RWKV:]

    st_s[...] = s0_ref[0]
    y_s[...] = jnp.zeros_like(y_s)
    m0, m1 = _lane_masks()

    def chunk_step(i, carry):
        units, rows_d = [], []
        for d in range(2):
            c = i if d == 0 else n_chunk - 1 - i
            rows = pl.ds(pl.multiple_of(c * CHUNK, CHUNK), CHUNK)
            rows_d.append(rows)
            lw = lw_s[d, rows, :]
            tri = tril_ref[...] if d == 0 else triu_ref[...]
            cum = _dot_exact_rhs(tri, lw)
            tot = jnp.sum(lw, axis=0, keepdims=True)
            kk = kk_s[rows, :]
            r = r_s[rows, :]
            v = v_s[rows, :]
            kd = kd_s[d, rows, :]
            b = b_s[d, rows, :]
            for p in range(N_PAIR):
                sl = slice(p * LANES, (p + 1) * LANES)
                units.append((d == 1, kk[:, sl], r[:, sl], v[:, sl], kd[:, sl], b[:, sl],
                              cum[:, sl], lw[:, sl], tot[:, sl], st_s[d, p]))
        ys, s_new = _rwkv_units(units, m0, m1)
        for d in range(2):
            for p in range(N_PAIR):
                st_s[d, p] = s_new[d * N_PAIR + p]
            y_s[rows_d[d], :] = y_s[rows_d[d], :] + jnp.concatenate(
                ys[d * N_PAIR:(d + 1) * N_PAIR], axis=1)
        return carry

    lax.fori_loop(0, n_chunk, chunk_step, 0)
    sfin_ref[0] = st_s[...]

    for j in range(seq_len // blk):
        rows = slice(j * blk, (j + 1) * blk)
        y = y_s[rows, :]
        mean = headsum(y) * (1.0 / HEAD_DIM)
        yc = y - mean
        var = headsum(yc * yc) * (1.0 / HEAD_DIM)
        yn = yc * lax.rsqrt(var + GN_EPS) * lnx_g + lnx_b
        y_ref[rows, :] = (yn + bon_s[rows, :]) * g_s[rows, :]


def _rwkv(u_r, s0, seq_len, wts):
    n_seq = u_r.shape[0] // seq_len
    const2 = lambda b: (0, 0)
    st_spec = pl.BlockSpec((1, 2, N_PAIR, LANES, LANES), lambda b: (b, 0, 0, 0, 0))
    tok = pltpu.VMEM((seq_len, D_RWKV), F32)
    tok2 = pltpu.VMEM((2, seq_len, D_RWKV), F32)
    in_specs = [pl.BlockSpec((seq_len, N_RWKV_PAD), lambda b: (b, 0)), st_spec]
    in_specs += [pl.BlockSpec(w.shape, const2) for w in wts]
    return pl.pallas_call(
        functools.partial(_rwkv_body, seq_len),
        grid=(n_seq,),
        in_specs=in_specs,
        out_specs=[pl.BlockSpec((seq_len, D_RWKV), lambda b: (b, 0)), st_spec],
        out_shape=[jax.ShapeDtypeStruct((n_seq * seq_len, D_RWKV), F32),
                   jax.ShapeDtypeStruct((n_seq, 2, N_PAIR, LANES, LANES), F32)],
        scratch_shapes=[tok] * 5 + [tok2] * 3 + [tok,
                        pltpu.VMEM((2, N_PAIR, LANES, LANES), F32)],
        compiler_params=_cp(1),
    )(u_r, s0, *wts)


def _rwkv_weights(p, l):
    z = functools.partial(jnp.zeros, dtype=F32)
    mu = jnp.concatenate([p['shift_mu'][l], z((2, N_RWKV_PAD - N_RWKV_IN))], axis=1)
    wdec = z((LANES, 2 * D_RWKV))
    wdec = wdec.at[0:LORA_W, :D_RWKV].set(p['decay_up'][l, 0])
    wdec = wdec.at[LORA_W:2 * LORA_W, D_RWKV:].set(p['decay_up'][l, 1])
    wicl = z((LANES, 2 * D_RWKV))
    wicl = wicl.at[2 * LORA_W:2 * LORA_W + LORA_A, :D_RWKV].set(p['iclr_up'][l, 0])
    wicl = wicl.at[2 * LORA_W + LORA_A:2 * LORA_W + 2 * LORA_A, D_RWKV:].set(p['iclr_up'][l, 1])
    wg = z((LANES, D_RWKV)).at[0:LORA_G].set(p['gate_up'][l])
    w0a0 = jnp.stack([p['decay_w0'][l].reshape(-1), p['iclr_a0'][l].reshape(-1)])
    vec = jnp.stack([p['k_k'][l], p['k_a'][l], p['r_k'][l].reshape(-1), p['lnx_g'][l], p['lnx_b'][l],
                     z((D_RWKV,)), z((D_RWKV,)), z((D_RWKV,))])
    head = np.arange(D_RWKV) // HEAD_DIM
    bo = jnp.asarray(head[:, None] == head[None, :], BF16)
    idx = np.arange(CHUNK)
    tril = jnp.asarray(idx[None, :] <= idx[:, None], BF16)
    triu = jnp.asarray(idx[None, :] >= idx[:, None], BF16)
    return [mu, wdec.astype(BF16), wicl.astype(BF16), wg.astype(BF16), w0a0, vec, bo, tril, triu]


def _rope(x, cos, sin):
    lane = lax.broadcasted_iota(jnp.int32, (1, LANES), 1)
    first_half = (lane & ROPE_PAIRS) == 0
    partner = jnp.where(first_half, pltpu.roll(x, LANES - ROPE_PAIRS, axis=1),
                        pltpu.roll(x, ROPE_PAIRS, axis=1))
    return x * cos + partner * sin


def _attn_body(has_ctx, lam_init, *refs):
    if has_ctx:
        (q_ref, k_ref, v_ref, lp_ref, sg_ref, kc_ref, vc_ref, cq_ref, sq_ref, ck_ref, sk_ref,
         o_ref) = refs
    else:
        q_ref, k_ref, v_ref, lp_ref, sg_ref, o_ref = refs
    q = q_ref[...]
    k = k_ref[...]
    v = v_ref[...]
    if has_ctx:
        q = _rope(q, cq_ref[...], sq_ref[...])
        k = _rope(k, ck_ref[...], sk_ref[...])
        k = jnp.concatenate([kc_ref[0], k], axis=0)
        v = jnp.concatenate([vc_ref[0], v], axis=0)
    q = q * (D_QK ** -0.5)
    lp = lp_ref[...]
    lam = (jnp.exp(jnp.sum(lp[0:1] * lp[1:2], axis=-1, keepdims=True))
           - jnp.exp(jnp.sum(lp[2:3] * lp[3:4], axis=-1, keepdims=True)) + lam_init)
    lane = lax.broadcasted_iota(jnp.int32, (1, LANES), 1)
    kb = k.astype(BF16)
    out = jnp.zeros(q.shape, F32)
    for h in range(2):
        head = (lane >= h * HEAD_DIM) & (lane < (h + 1) * HEAD_DIM)
        probs = []
        for m in range(2):
            lo = h * HEAD_DIM + m * D_QK
            sel = (lane >= lo) & (lane < lo + D_QK)
            s = _bdot_nt(jnp.where(sel, q, 0.0), kb)
            e = jnp.exp(s - jnp.max(s, axis=-1, keepdims=True))
            probs.append(e / jnp.sum(e, axis=-1, keepdims=True))
        pr = probs[0] - lam * probs[1]
        o = _bdot(pr, jnp.where(head, v, 0.0))
        ms = jnp.sum(o * o, axis=-1, keepdims=True) * (1.0 / HEAD_DIM)
        out = out + o * lax.rsqrt(ms + SUBLN_EPS)
    o_ref[...] = out * sg_ref[...] * (1.0 - lam_init)


def _attention(u_qkv, seq_len, lam_init, lp, sg, ctx=None):
    n_tok = u_qkv.shape[0]
    n_seq = n_tok // seq_len
    tq = 256
    nq = seq_len // tq
    n_col = D_DIFF // LANES
    in_specs = [pl.BlockSpec((tq, LANES), lambda b, p, i: (b * nq + i, p)),
                pl.BlockSpec((seq_len, LANES), lambda b, p, i: (b, n_col + p)),
                pl.BlockSpec((seq_len, LANES), lambda b, p, i: (b, 2 * n_col + p)),
                pl.BlockSpec(lp.shape, lambda b, p, i: (0, 0)),
                pl.BlockSpec((1, LANES), lambda b, p, i: (0, 0))]
    args = [u_qkv, u_qkv, u_qkv, lp, sg]
    if ctx is not None:
        kc, vc, cos, sin = ctx
        past = kc.shape[1]
        in_specs += [pl.BlockSpec((1, past, LANES), lambda b, p, i: (b, 0, p)),
                     pl.BlockSpec((1, past, LANES), lambda b, p, i: (b, 0, p)),
                     pl.BlockSpec((tq, LANES), lambda b, p, i: (i, 0)),
                     pl.BlockSpec((tq, LANES), lambda b, p, i: (i, 0)),
                     pl.BlockSpec((seq_len, LANES), lambda b, p, i: (0, 0)),
                     pl.BlockSpec((seq_len, LANES), lambda b, p, i: (0, 0))]
        args += [kc, vc, cos, sin, cos, sin]
    return pl.pallas_call(
        functools.partial(_attn_body, ctx is not None, lam_init),
        grid=(n_seq, n_col, nq),
        in_specs=in_specs,
        out_specs=pl.BlockSpec((tq, LANES), lambda b, p, i: (b * nq + i, p)),
        out_shape=jax.ShapeDtypeStruct((n_tok, D_DIFF), F32),
        compiler_params=_cp(3),
    )(*args)


def _rope_tables(seq_len):
    t = jnp.arange(seq_len)
    pos = jnp.stack([(t // GRID_W).astype(F32), (t % GRID_W).astype(F32)], axis=1)
    inv = 1.0 / (ROPE_BASE ** (jnp.arange(ROPE_PAIRS, dtype=F32) / ROPE_PAIRS))
    ang = pos[:, :, None] * inv
    d = np.arange(LANES) % D_QK
    axis = d // (2 * ROPE_PAIRS)
    second = (d % (2 * ROPE_PAIRS)) // ROPE_PAIRS
    idx = d % ROPE_PAIRS
    cos = jnp.cos(ang)[:, axis, idx]
    sin = jnp.sin(ang)[:, axis, idx] * jnp.asarray(np.where(second == 1, 1.0, -1.0), F32)
    return cos, sin


def _fnet_body(x_ref, cth_ref, ctl_ref, sth_ref, stl_ref, cch_ref, ccl_ref, sch_ref, scl_ref, o_ref):
    d = functools.partial(jnp.dot, preferred_element_type=F32)
    x_hi, x_lo = _split2(x_ref[...])
    xc = d(x_hi, cch_ref[...]) + d(x_lo, cch_ref[...]) + d(x_hi, ccl_ref[...])
    xs = d(x_hi, sch_ref[...]) + d(x_lo, sch_ref[...]) + d(x_hi, scl_ref[...])
    c_hi, c_lo = _split2(xc)
    s_hi, s_lo = _split2(xs)
    yc = d(cth_ref[...], c_hi) + d(cth_ref[...], c_lo) + d(ctl_ref[...], c_hi)
    ys = d(sth_ref[...], s_hi) + d(sth_ref[...], s_lo) + d(stl_ref[...], s_hi)
    o_ref[...] = yc - ys


def _dft_consts(n, block=1):
    idx = np.arange(n)
    ang = 2.0 * np.pi * ((idx[:, None] * idx[None, :]) % n) / n
    out = []
    for m in (np.cos(ang) / np.sqrt(n), np.sin(ang) / np.sqrt(n)):
        m = np.kron(np.eye(block), m).astype(np.float32)
        hi = jnp.asarray(m, F32).astype(BF16)
        lo = (jnp.asarray(m, F32) - hi.astype(F32)).astype(BF16)
        out += [hi, lo]
    return out


def _fnet(u_f, seq_len):
    n_tok = u_f.shape[0]
    consts = _dft_consts(seq_len) + _dft_consts(FNET_GROUP_DIM, FNET_GROUPS)
    const = lambda b: (0, 0)
    return pl.pallas_call(
        _fnet_body,
        grid=(n_tok // seq_len,),
        in_specs=[pl.BlockSpec((seq_len, D_FNET), lambda b: (b, 0))]
        + [pl.BlockSpec(c.shape, const) for c in consts],
        out_specs=pl.BlockSpec((seq_len, D_FNET), lambda b: (b, 0)),
        out_shape=jax.ShapeDtypeStruct((n_tok, D_FNET), F32),
        compiler_params=_cp(1),
    )(u_f, *consts)


def _ffn_body(final, yr_ref, yd_ref, yf_ref, x_ref, g1_ref, sh2_ref, sc2_ref, g2_ref, n2_ref, fg_ref,
              wo_ref, wi_ref, wf_ref, o_ref):
    y = (_bdot(yr_ref[...], wo_ref[0:D_RWKV, :])
         + _bdot(yd_ref[...], wo_ref[D_RWKV:D_RWKV + D_DIFF, :])
         + _bdot(yf_ref[...], wo_ref[D_RWKV + D_DIFF:, :]))
    x = x_ref[...] + g1_ref[0] * y
    h = _rms(x, n2_ref[...]) * (1.0 + sc2_ref[0]) + sh2_ref[0]
    z = jnp.dot(h.astype(BF16), wi_ref[...], preferred_element_type=F32)
    gate = z[:, :D_FF]
    act = gate * jax.nn.sigmoid(gate) * z[:, D_FF:]
    x = x + g2_ref[0] * _bdot(act, wf_ref[...])
    o_ref[...] = _rms(x, fg_ref[...]) if final else x


def _ffn(y_r, y_d, y_f, x, mod, layer, row_fn, n2, fg, wo, wi, wf, final, tm):
    n_tok = x.shape[0]
    row = lambda i: (i, 0)
    const = lambda i: (0, 0)
    return pl.pallas_call(
        functools.partial(_ffn_body, final),
        grid=(n_tok // tm,),
        in_specs=[pl.BlockSpec((tm, D_RWKV), row),
                  pl.BlockSpec((tm, D_DIFF), row),
                  pl.BlockSpec((tm, D_FNET), row),
                  pl.BlockSpec((tm, D_MODEL), row),
                  _mod_spec(layer, 2, row_fn),
                  _mod_spec(layer, 3, row_fn),
                  _mod_spec(layer, 4, row_fn),
                  _mod_spec(layer, 5, row_fn),
                  pl.BlockSpec((1, D_MODEL), const),
                  pl.BlockSpec((1, D_MODEL), const),
                  pl.BlockSpec(wo.shape, const),
                  pl.BlockSpec(wi.shape, const),
                  pl.BlockSpec(wf.shape, const)],
        out_specs=pl.BlockSpec((tm, D_MODEL), row),
        out_shape=jax.ShapeDtypeStruct((n_tok, D_MODEL), F32),
        compiler_params=_cp(1),
    )(y_r, y_d, y_f, x, mod, mod, mod, mod, n2.reshape(1, D_MODEL), fg.reshape(1, D_MODEL), wo, wi, wf)


def _block_diag_state(s):
    b = s.shape[0]
    s = s.reshape(b, 2, N_PAIR, 2, HEAD_DIM, HEAD_DIM)
    eye = jnp.eye(2, dtype=s.dtype)
    s = s[:, :, :, :, :, None, :] * eye[None, None, None, :, None, :, None]
    return s.reshape(b, 2, N_PAIR, LANES, LANES)


def _diag_blocks(s):
    b = s.shape[0]
    s = s.reshape(b, 2, N_PAIR, 2, HEAD_DIM, 2, HEAD_DIM)
    s = jnp.stack([s[:, :, :, 0, :, 0, :], s[:, :, :, 1, :, 1, :]], axis=3)
    return s.reshape(b, 2, H_RWKV, HEAD_DIM, HEAD_DIM)


def kernel(x_prompt, x_sample, c, state_rwkv, cache_diff_k, cache_diff_v, c_ctx, norm1_g, norm2_g, final_norm_g, w_mod, b_mod, w_in, w_out, shift_mu, decay_w0, decay_up, iclr_a0, iclr_up, gate_up, k_k, k_a, r_k, lnx_g, lnx_b, diff_lambda, subln_g, w_ffn_in, w_ffn_out):
    p = dict(shift_mu=shift_mu, decay_w0=decay_w0, decay_up=decay_up, iclr_a0=iclr_a0, iclr_up=iclr_up,
             gate_up=gate_up, k_k=k_k, k_a=k_a, r_k=r_k, lnx_g=lnx_g, lnx_b=lnx_b)
    n_ctx, t_ctx, _ = x_prompt.shape
    n_dec, t_dec, _ = x_sample.shape
    past = cache_diff_k.shape[2]

    cond = jnp.concatenate([c_ctx[None, :], c, jnp.zeros((MOD_ROWS - 1 - n_dec, D_MODEL), F32)], axis=0)
    mod = _modulation(cond, w_mod, b_mod).reshape(DEPTH * MOD_ROWS, 1, 6 * D_MODEL)

    tm_in, tm_ffn = 512, 256
    streams = [
        dict(x=x_prompt.reshape(n_ctx * t_ctx, D_MODEL), t=t_ctx, n=n_ctx,
             row_in=lambda i: 0, row_ffn=lambda i: 0),
        dict(x=x_sample.reshape(n_dec * t_dec, D_MODEL), t=t_dec, n=n_dec,
             row_in=lambda i: 1 + i // (t_dec // tm_in), row_ffn=lambda i: 1 + i // (t_dec // tm_ffn)),
    ]
    cos, sin = _rope_tables(t_dec)
    states, ks, vs = [], [], []
    for l in range(DEPTH):
        w_in_l = jnp.concatenate(
            [w_in[l, :, :N_RWKV_IN], jnp.zeros((D_MODEL, N_RWKV_PAD - N_RWKV_IN), F32),
             w_in[l, :, N_RWKV_IN:]], axis=1).astype(BF16)
        wo = w_out[l].astype(BF16)
        wi = w_ffn_in[l].astype(BF16)
        wf = w_ffn_out[l].astype(BF16)
        rw = _rwkv_weights(p, l)
        lam_init = 0.8 - 0.6 * math.exp(-0.3 * l)
        sg = jnp.tile(subln_g[l], 2).reshape(1, LANES)
        for si, st in enumerate(streams):
            u_r, u_qkv, u_f = _inproj(st['x'], mod, l, st['row_in'], norm1_g[l], w_in_l, tm_in)
            if si == 0:
                s0 = jnp.zeros((st['n'], 2, N_PAIR, LANES, LANES), F32)
                attn_ctx = None
            else:
                s0 = _block_diag_state(state_rwkv[:, l].astype(F32))
                attn_ctx = (cache_diff_k[:, l].reshape(n_dec, past, D_DIFF).astype(F32),
                            cache_diff_v[:, l].reshape(n_dec, past, D_DIFF).astype(F32), cos, sin)
            y_r, s_fin = _rwkv(u_r, s0, st['t'], rw)
            y_d = _attention(u_qkv, st['t'], lam_init, diff_lambda[l], sg, attn_ctx)
            y_f = _fnet(u_f, st['t'])
            st['x'] = _ffn(y_r, y_d, y_f, st['x'], mod, l, st['row_ffn'], norm2_g[l], final_norm_g,
                           wo, wi, wf, l == DEPTH - 1, tm_ffn)
            if si == 0:
                states.append(_diag_blocks(s_fin))
                ks.append(u_qkv[:, D_DIFF:2 * D_DIFF].reshape(n_ctx, t_ctx, H_DIFF, 2, D_QK))
                vs.append(u_qkv[:, 2 * D_DIFF:].reshape(n_ctx, t_ctx, H_DIFF, HEAD_DIM))
    y_prompt = streams[0]['x'].reshape(n_ctx, t_ctx, D_MODEL)
    y_sample = streams[1]['x'].reshape(n_dec, t_dec, D_MODEL)
    return (y_prompt, y_sample, jnp.stack(states, axis=1), jnp.stack(ks, axis=1), jnp.stack(vs, axis=1))
```

```python
import functools
import math

import numpy as np
import jax
import jax.numpy as jnp
from jax import lax
from jax.experimental import pallas as pl
from jax.experimental.pallas import tpu as pltpu

F32 = jnp.float32
BF16 = jnp.bfloat16

D_MODEL = 1024
DEPTH = 2
GRID_W = 64
HEAD_DIM = 64
D_RWKV = 384
H_RWKV = D_RWKV // HEAD_DIM
D_DIFF = 384
H_DIFF = D_DIFF // HEAD_DIM
D_QK = HEAD_DIM // 2
D_FNET = D_MODEL - D_RWKV - D_DIFF
FNET_GROUPS = 4
FNET_GROUP_DIM = D_FNET // FNET_GROUPS
LORA_W = 32
LORA_A = 32
LORA_G = 64
N_RWKV_IN = 3 * D_RWKV + 2 * LORA_W + 2 * LORA_A + LORA_G
N_DIFF_IN = 3 * D_DIFF
D_FF = ((8 * D_MODEL + 3 * 256 - 1) // (3 * 256)) * 256
ROPE_PAIRS = D_QK // 4
ROPE_BASE = 10000.0
RMS_EPS = 1e-6
GN_EPS = 64e-5
SUBLN_EPS = 1e-5
DECAY_SCALE = math.exp(-0.5)

LANES = 128
N_RWKV_PAD = 11 * LANES
D_IN_PAD = N_RWKV_PAD + N_DIFF_IN + D_FNET
N_PAIR = H_RWKV // 2
CHUNK = 64
MOD_ROWS = 8
VMEM_LIMIT = 56 * 1024 * 1024


def _cp(n_axes=1):
    return pltpu.CompilerParams(dimension_semantics=("arbitrary",) * n_axes,
                                vmem_limit_bytes=VMEM_LIMIT)


def _bdot(a, b):
    return jnp.dot(a.astype(BF16), b.astype(BF16), preferred_element_type=F32)


def _bdot_nt(a, b):
    return lax.dot_general(a.astype(BF16), b.astype(BF16), (((1,), (1,)), ((), ())),
                           preferred_element_type=F32)


def _split2(x):
    hi = x.astype(BF16)
    lo = (x - hi.astype(F32)).astype(BF16)
    return hi, lo


def _dot_x3(a, b):
    a_hi, a_lo = _split2(a)
    b_hi, b_lo = _split2(b)
    d = functools.partial(jnp.dot, preferred_element_type=F32)
    return d(a_hi, b_hi) + d(a_lo, b_hi) + d(a_hi, b_lo)


def _dot_exact_rhs(c_bf16, x):
    d = functools.partial(jnp.dot, preferred_element_type=F32)
    hi, lo = _split2(x)
    return d(c_bf16, hi) + d(c_bf16, lo)


def _dot_exact_lhs(x, c_bf16):
    d = functools.partial(jnp.dot, preferred_element_type=F32)
    hi, lo = _split2(x)
    return d(hi, c_bf16) + d(lo, c_bf16)


def _rms(x, g):
    return x * lax.rsqrt(jnp.mean(x * x, axis=-1, keepdims=True) + RMS_EPS) * g


def _mod_body(c_ref, w_ref, b_ref, o_ref):
    c = c_ref[...]
    a = c * jax.nn.sigmoid(c)
    o_ref[0] = _dot_x3(a, w_ref[0]) + b_ref[0]


def _modulation(cond, w_mod, b_mod):
    n_layers, _, n_out = w_mod.shape
    tn = 1536
    return pl.pallas_call(
        _mod_body,
        grid=(n_layers, n_out // tn),
        in_specs=[pl.BlockSpec((MOD_ROWS, D_MODEL), lambda l, j: (0, 0)),
                  pl.BlockSpec((1, D_MODEL, tn), lambda l, j: (l, 0, j)),
                  pl.BlockSpec((1, 1, tn), lambda l, j: (l, 0, j))],
        out_specs=pl.BlockSpec((1, MOD_ROWS, tn), lambda l, j: (l, 0, j)),
        out_shape=jax.ShapeDtypeStruct((n_layers, MOD_ROWS, n_out), F32),
        compiler_params=_cp(2),
    )(cond, w_mod, b_mod.reshape(n_layers, 1, n_out))


def _mod_spec(layer, col, row_fn):
    return pl.BlockSpec((1, 1, D_MODEL), lambda i: (layer * MOD_ROWS + row_fn(i), 0, col))


def _inproj_body(x_ref, g_ref, sh_ref, sc_ref, w_ref, ur_ref, uq_ref, uf_ref):
    h = _rms(x_ref[...], g_ref[...]) * (1.0 + sc_ref[0]) + sh_ref[0]
    u = jnp.dot(h.astype(BF16), w_ref[...], preferred_element_type=F32)
    ur_ref[...] = u[:, :N_RWKV_PAD]
    uq_ref[...] = u[:, N_RWKV_PAD:N_RWKV_PAD + N_DIFF_IN]
    uf_ref[...] = u[:, N_RWKV_PAD + N_DIFF_IN:]


def _inproj(x, mod, layer, row_fn, g, w, tm):
    n_tok = x.shape[0]
    row = lambda i: (i, 0)
    const = lambda i: (0, 0)
    return pl.pallas_call(
        _inproj_body,
        grid=(n_tok // tm,),
        in_specs=[pl.BlockSpec((tm, D_MODEL), row),
                  pl.BlockSpec((1, D_MODEL), const),
                  _mod_spec(layer, 0, row_fn),
                  _mod_spec(layer, 1, row_fn),
                  pl.BlockSpec((D_MODEL, D_IN_PAD), const)],
        out_specs=[pl.BlockSpec((tm, N_RWKV_PAD), row),
                   pl.BlockSpec((tm, N_DIFF_IN), row),
                   pl.BlockSpec((tm, D_FNET), row)],
        out_shape=[jax.ShapeDtypeStruct((n_tok, N_RWKV_PAD), F32),
                   jax.ShapeDtypeStruct((n_tok, N_DIFF_IN), F32),
                   jax.ShapeDtypeStruct((n_tok, D_FNET), F32)],
        compiler_params=_cp(1),
    )(x, g.reshape(1, D_MODEL), mod, mod, w)


def _lane_masks():
    lane = lax.broadcasted_iota(jnp.int32, (1, LANES), 1)
    return (lane < HEAD_DIM).astype(F32), (lane >= HEAD_DIM).astype(F32)


def _rwkv_units(units, m0, m1):
    def bd(x):
        return jnp.concatenate([x * m0, x * m1], axis=0)

    row = lax.broadcasted_iota(jnp.int32, (CHUNK, LANES), 0)
    col = lax.broadcasted_iota(jnp.int32, (CHUNK, LANES), 1) & (CHUNK - 1)
    eye = (col == row).astype(F32)
    r2 = lax.broadcasted_iota(jnp.int32, (LANES, LANES), 0) < HEAD_DIM
    c2 = lax.broadcasted_iota(jnp.int32, (LANES, LANES), 1) < HEAD_DIM
    rng = range(len(units))

    pre = []
    for rev, kk, r, v, kd, b, cum, lw, tot, s_prev in units:
        p_inv = jnp.exp(-cum)
        p_rem = jnp.exp(tot - cum)
        ab = -kk * jnp.exp(cum - lw)
        rb = r * jnp.exp(cum)
        strict = (col > row) if rev else (col < row)
        incl = (col >= row) if rev else (col <= row)
        pre.append(dict(ab=ab, rb=rb, vbd=bd(v), strict=strict, incl=incl,
                        lhs=jnp.concatenate([ab, rb], axis=0),
                        rhs=jnp.concatenate([bd(b * p_inv), bd(kd * p_inv)], axis=0),
                        bk=jnp.concatenate([b * p_rem, kd * p_rem], axis=0)))

    mm = [_bdot_nt(q['lhs'], q['rhs']) for q in pre]
    m_ab = [jnp.where(pre[i]['strict'], mm[i][:CHUNK, :LANES], 0.0) for i in rng]
    m_ak = [jnp.where(pre[i]['strict'], mm[i][:CHUNK, LANES:], 0.0) for i in rng]
    m_r = [jnp.concatenate([jnp.where(pre[i]['incl'], mm[i][CHUNK:, :LANES], 0.0),
                            jnp.where(pre[i]['incl'], mm[i][CHUNK:, LANES:], 0.0)], axis=1) for i in rng]
    mv = [_bdot(m_ak[i], pre[i]['vbd']) for i in rng]

    t = [eye + m_ab[i] for i in rng]
    n = [_bdot(m_ab[i], bd(m_ab[i])) for i in rng]
    for _ in range(4):
        x = [_bdot(jnp.concatenate([t[i], n[i]], axis=0), bd(n[i])) for i in rng]
        t = [t[i] + x[i][:CHUNK] for i in rng]
        n = [x[i][CHUNK:] for i in rng]
    t = [t[i] + _bdot(t[i], bd(n[i])) for i in rng]

    w = [_bdot(t[i], jnp.concatenate([bd(pre[i]['ab']), bd(mv[i])], axis=1)) for i in rng]
    xs = [_bdot_nt(jnp.concatenate([w[i][:, :LANES], pre[i]['rb']], axis=0), units[i][9]) for i in rng]
    u = [xs[i][:CHUNK] + w[i][:, LANES:] for i in rng]
    y = [xs[i][CHUNK:] + _bdot(m_r[i], jnp.concatenate([bd(u[i]), pre[i]['vbd']], axis=0)) for i in rng]
    z = [_bdot(jnp.concatenate([u[i], units[i][3]], axis=0).T, pre[i]['bk']) for i in rng]
    s_new = [units[i][9] * jnp.exp(units[i][8]) + jnp.where(r2 == c2, z[i], 0.0) for i in rng]
    return y, s_new


def _rwkv_body(seq_len, n_sub, u_ref, s0_ref, mu_ref, wdec_ref, wicl_ref, wg_ref, w0a0_ref, vec_ref,
               bo_ref, tril_ref, triu_ref, y_ref, sfin_ref,
               r_s, v_s, kk_s, g_s, bon_s, kd_s, b_s, lw_s, y_s, st_s):
    n_chunk = seq_len // CHUNK
    blk = 256
    k_k = vec_ref[0:1, :]
    k_a = vec_ref[1:2, :]
    r_k = vec_ref[2:3, :]
    lnx_g = vec_ref[3:4, :]
    lnx_b = vec_ref[4:5, :]
    bo = bo_ref[...]

    def headsum(x):
        return _dot_exact_lhs(x, bo)

    mu0 = mu_ref[0:1, :]
    mu1 = mu_ref[1:2, :]
    rid = lax.broadcasted_iota(jnp.int32, (blk, 1), 0)
    for j in range(n_sub * seq_len // blk):
        r0 = j * blk
        cur = u_ref[r0:r0 + blk, :]
        first = jnp.zeros((1, N_RWKV_PAD), F32) if r0 % seq_len == 0 else u_ref[r0 - 1:r0, :]
        last = (jnp.zeros((1, N_RWKV_PAD), F32) if (r0 + blk) % seq_len == 0
                else u_ref[r0 + blk:r0 + blk + 1, :])
        prev = jnp.where(rid == 0, first, pltpu.roll(cur, 1, axis=0))
        nxt = jnp.where(rid == blk - 1, last, pltpu.roll(cur, blk - 1, axis=0))
        xs = cur + mu0 * (prev - cur) + mu1 * (nxt - cur)
        r = xs[:, 0:D_RWKV]
        k = xs[:, D_RWKV:2 * D_RWKV]
        v = xs[:, 2 * D_RWKV:3 * D_RWKV]
        lora = xs[:, 3 * D_RWKV:3 * D_RWKV + LANES]
        gd = xs[:, 3 * D_RWKV + LANES:3 * D_RWKV + 2 * LANES]
        dec = _bdot(jnp.tanh(lora), wdec_ref[...])
        icl = _bdot(lora, wicl_ref[...])
        g = _bdot(jax.nn.sigmoid(gd), wg_ref[...])
        logw = -DECAY_SCALE * jax.nn.sigmoid(w0a0_ref[0:1, :] + dec)
        a = jax.nn.sigmoid(w0a0_ref[1:2, :] + icl)
        kk = k * k_k
        kk = kk / jnp.maximum(jnp.sqrt(headsum(kk * kk)), 1e-12)
        a_f = a[:, :D_RWKV]
        a_b = a[:, D_RWKV:]
        kd_f = k * (1.0 + (a_f - 1.0) * k_a)
        kd_b = k * (1.0 + (a_b - 1.0) * k_a)
        rows = slice(r0, r0 + blk)
        r_s[rows, :] = r
        v_s[rows, :] = v
        kk_s[rows, :] = kk
        g_s[rows, :] = g
        bon_s[rows, :] = headsum(r * (kd_f + kd_b) * r_k) * v
        kd_s[0, rows, :] = kd_f
        kd_s[1, rows, :] = kd_b
        b_s[0, rows, :] = kk * a_f
        b_s[1, rows, :] = kk * a_b
        lw_s[0, rows, :] = logw[:, :D_RWKV]
        lw_s[1, rows, :] = logw[:, D_RWKV:]

    st_s[...] = s0_ref[...]
    y_s[...] = jnp.zeros_like(y_s)
    m0, m1 = _lane_masks()

    def chunk_step(i, carry):
        units, rows_sd = [], []
        for s in range(n_sub):
            for d in range(2):
                c = i if d == 0 else n_chunk - 1 - i
                rows = pl.ds(pl.multiple_of(s * seq_len + c * CHUNK, CHUNK), CHUNK)
                rows_sd.append(rows)
                lw = lw_s[d, rows, :]
                tri = tril_ref[...] if d == 0 else triu_ref[...]
                cum = _dot_exact_rhs(tri, lw)
                tot = jnp.sum(lw, axis=0, keepdims=True)
                kk = kk_s[rows, :]
                r = r_s[rows, :]
                v = v_s[rows, :]
                kd = kd_s[d, rows, :]
                b = b_s[d, rows, :]
                for p in range(N_PAIR):
                    sl = slice(p * LANES, (p + 1) * LANES)
                    units.append((d == 1, kk[:, sl], r[:, sl], v[:, sl], kd[:, sl], b[:, sl],
                                  cum[:, sl], lw[:, sl], tot[:, sl], st_s[s, d, p]))
        ys, s_new = _rwkv_units(units, m0, m1)
        for s in range(n_sub):
            for d in range(2):
                base = (s * 2 + d) * N_PAIR
                for p in range(N_PAIR):
                    st_s[s, d, p] = s_new[base + p]
                rows = rows_sd[s * 2 + d]
                y_s[rows, :] = y_s[rows, :] + jnp.concatenate(ys[base:base + N_PAIR], axis=1)
        return carry

    lax.fori_loop(0, n_chunk, chunk_step, 0)
    sfin_ref[...] = st_s[...]

    for j in range(n_sub * seq_len // blk):
        rows = slice(j * blk, (j + 1) * blk)
        y = y_s[rows, :]
        mean = headsum(y) * (1.0 / HEAD_DIM)
        yc = y - mean
        var = headsum(yc * yc) * (1.0 / HEAD_DIM)
        yn = yc * lax.rsqrt(var + GN_EPS) * lnx_g + lnx_b
        y_ref[rows, :] = (yn + bon_s[rows, :]) * g_s[rows, :]


def _rwkv(u_r, s0, seq_len, n_sub, wts):
    n_seq = u_r.shape[0] // seq_len
    rows = n_sub * seq_len
    const2 = lambda b: (0, 0)
    st_spec = pl.BlockSpec((n_sub, 2, N_PAIR, LANES, LANES), lambda b: (b, 0, 0, 0, 0))
    tok = pltpu.VMEM((rows, D_RWKV), F32)
    tok2 = pltpu.VMEM((2, rows, D_RWKV), F32)
    in_specs = [pl.BlockSpec((rows, N_RWKV_PAD), lambda b: (b, 0)), st_spec]
    in_specs += [pl.BlockSpec(w.shape, const2) for w in wts]
    return pl.pallas_call(
        functools.partial(_rwkv_body, seq_len, n_sub),
        grid=(n_seq // n_sub,),
        in_specs=in_specs,
        out_specs=[pl.BlockSpec((rows, D_RWKV), lambda b: (b, 0)), st_spec],
        out_shape=[jax.ShapeDtypeStruct((n_seq * seq_len, D_RWKV), F32),
                   jax.ShapeDtypeStruct((n_seq, 2, N_PAIR, LANES, LANES), F32)],
        scratch_shapes=[tok] * 5 + [tok2] * 3 + [tok,
                        pltpu.VMEM((n_sub, 2, N_PAIR, LANES, LANES), F32)],
        compiler_params=_cp(1),
    )(u_r, s0, *wts)


def _rwkv_weights(p, l):
    z = functools.partial(jnp.zeros, dtype=F32)
    mu = jnp.concatenate([p['shift_mu'][l], z((2, N_RWKV_PAD - N_RWKV_IN))], axis=1)
    wdec = z((LANES, 2 * D_RWKV))
    wdec = wdec.at[0:LORA_W, :D_RWKV].set(p['decay_up'][l, 0])
    wdec = wdec.at[LORA_W:2 * LORA_W, D_RWKV:].set(p['decay_up'][l, 1])
    wicl = z((LANES, 2 * D_RWKV))
    wicl = wicl.at[2 * LORA_W:2 * LORA_W + LORA_A, :D_RWKV].set(p['iclr_up'][l, 0])
    wicl = wicl.at[2 * LORA_W + LORA_A:2 * LORA_W + 2 * LORA_A, D_RWKV:].set(p['iclr_up'][l, 1])
    wg = z((LANES, D_RWKV)).at[0:LORA_G].set(p['gate_up'][l])
    w0a0 = jnp.stack([p['decay_w0'][l].reshape(-1), p['iclr_a0'][l].reshape(-1)])
    vec = jnp.stack([p['k_k'][l], p['k_a'][l], p['r_k'][l].reshape(-1), p['lnx_g'][l], p['lnx_b'][l],
                     z((D_RWKV,)), z((D_RWKV,)), z((D_RWKV,))])
    head = np.arange(D_RWKV) // HEAD_DIM
    bo = jnp.asarray(head[:, None] == head[None, :], BF16)
    idx = np.arange(CHUNK)
    tril = jnp.asarray(idx[None, :] <= idx[:, None], BF16)
    triu = jnp.asarray(idx[None, :] >= idx[:, None], BF16)
    return [mu, wdec.astype(BF16), wicl.astype(BF16), wg.astype(BF16), w0a0, vec, bo, tril, triu]


def _rope(x, cos, sin):
    lane = lax.broadcasted_iota(jnp.int32, (1, LANES), 1)
    first_half = (lane & ROPE_PAIRS) == 0
    partner = jnp.where(first_half, pltpu.roll(x, LANES - ROPE_PAIRS, axis=1),
                        pltpu.roll(x, ROPE_PAIRS, axis=1))
    return x * cos + partner * sin


def _attn_body(has_ctx, lam_init, *refs):
    if has_ctx:
        (q_ref, k_ref, v_ref, lp_ref, sg_ref, kc_ref, vc_ref, cq_ref, sq_ref, ck_ref, sk_ref,
         o_ref) = refs
    else:
        q_ref, k_ref, v_ref, lp_ref, sg_ref, o_ref = refs
    q = q_ref[...]
    k = k_ref[...]
    v = v_ref[...]
    if has_ctx:
        q = _rope(q, cq_ref[...], sq_ref[...])
        k = _rope(k, ck_ref[...], sk_ref[...])
        k = jnp.concatenate([kc_ref[0], k], axis=0)
        v = jnp.concatenate([vc_ref[0], v], axis=0)
    q = q * (D_QK ** -0.5)
    lp = lp_ref[...]
    lam = (jnp.exp(jnp.sum(lp[0:1] * lp[1:2], axis=-1, keepdims=True))
           - jnp.exp(jnp.sum(lp[2:3] * lp[3:4], axis=-1, keepdims=True)) + lam_init)
    lane = lax.broadcasted_iota(jnp.int32, (1, LANES), 1)
    kb = k.astype(BF16)
    out = jnp.zeros(q.shape, F32)
    for h in range(2):
        head = (lane >= h * HEAD_DIM) & (lane < (h + 1) * HEAD_DIM)
        probs = []
        for m in range(2):
            lo = h * HEAD_DIM + m * D_QK
            sel = (lane >= lo) & (lane < lo + D_QK)
            s = _bdot_nt(jnp.where(sel, q, 0.0), kb)
            e = jnp.exp(s - jnp.max(s, axis=-1, keepdims=True))
            probs.append(e / jnp.sum(e, axis=-1, keepdims=True))
        pr = probs[0] - lam * probs[1]
        o = _bdot(pr, jnp.where(head, v, 0.0))
        ms = jnp.sum(o * o, axis=-1, keepdims=True) * (1.0 / HEAD_DIM)
        out = out + o * lax.rsqrt(ms + SUBLN_EPS)
    o_ref[...] = out * sg_ref[...] * (1.0 - lam_init)


def _attention(u_qkv, seq_len, lam_init, lp, sg, ctx=None):
    n_tok = u_qkv.shape[0]
    n_seq = n_tok // seq_len
    tq = 256
    nq = seq_len // tq
    n_col = D_DIFF // LANES
    in_specs = [pl.BlockSpec((tq, LANES), lambda b, p, i: (b * nq + i, p)),
                pl.BlockSpec((seq_len, LANES), lambda b, p, i: (b, n_col + p)),
                pl.BlockSpec((seq_len, LANES), lambda b, p, i: (b, 2 * n_col + p)),
                pl.BlockSpec(lp.shape, lambda b, p, i: (0, 0)),
                pl.BlockSpec((1, LANES), lambda b, p, i: (0, 0))]
    args = [u_qkv, u_qkv, u_qkv, lp, sg]
    if ctx is not None:
        kc, vc, cos, sin = ctx
        past = kc.shape[1]
        in_specs += [pl.BlockSpec((1, past, LANES), lambda b, p, i: (b, 0, p)),
                     pl.BlockSpec((1, past, LANES), lambda b, p, i: (b, 0, p)),
                     pl.BlockSpec((tq, LANES), lambda b, p, i: (i, 0)),
                     pl.BlockSpec((tq, LANES), lambda b, p, i: (i, 0)),
                     pl.BlockSpec((seq_len, LANES), lambda b, p, i: (0, 0)),
                     pl.BlockSpec((seq_len, LANES), lambda b, p, i: (0, 0))]
        args += [kc, vc, cos, sin, cos, sin]
    return pl.pallas_call(
        functools.partial(_attn_body, ctx is not None, lam_init),
        grid=(n_seq, n_col, nq),
        in_specs=in_specs,
        out_specs=pl.BlockSpec((tq, LANES), lambda b, p, i: (b * nq + i, p)),
        out_shape=jax.ShapeDtypeStruct((n_tok, D_DIFF), F32),
        compiler_params=_cp(3),
    )(*args)


def _rope_tables(seq_len):
    t = jnp.arange(seq_len)
    pos = jnp.stack([(t // GRID_W).astype(F32), (t % GRID_W).astype(F32)], axis=1)
    inv = 1.0 / (ROPE_BASE ** (jnp.arange(ROPE_PAIRS, dtype=F32) / ROPE_PAIRS))
    ang = pos[:, :, None] * inv
    d = np.arange(LANES) % D_QK
    axis = d // (2 * ROPE_PAIRS)
    second = (d % (2 * ROPE_PAIRS)) // ROPE_PAIRS
    idx = d % ROPE_PAIRS
    cos = jnp.cos(ang)[:, axis, idx]
    sin = jnp.sin(ang)[:, axis, idx] * jnp.asarray(np.where(second == 1, 1.0, -1.0), F32)
    return cos, sin


def _fnet_body(x_ref, cth_ref, ctl_ref, sth_ref, stl_ref, cch_ref, ccl_ref, sch_ref, scl_ref, o_ref):
    d = functools.partial(jnp.dot, preferred_element_type=F32)
    x_hi, x_lo = _split2(x_ref[...])
    xc = d(x_hi, cch_ref[...]) + d(x_lo, cch_ref[...]) + d(x_hi, ccl_ref[...])
    xs = d(x_hi, sch_ref[...]) + d(x_lo, sch_ref[...]) + d(x_hi, scl_ref[...])
    c_hi, c_lo = _split2(xc)
    s_hi, s_lo = _split2(xs)
    yc = d(cth_ref[...], c_hi) + d(cth_ref[...], c_lo) + d(ctl_ref[...], c_hi)
    ys = d(sth_ref[...], s_hi) + d(sth_ref[...], s_lo) + d(stl_ref[...], s_hi)
    o_ref[...] = yc - ys


def _dft_consts(n, block=1):
    idx = np.arange(n)
    ang = 2.0 * np.pi * ((idx[:, None] * idx[None, :]) % n) / n
    out = []
    for m in (np.cos(ang) / np.sqrt(n), np.sin(ang) / np.sqrt(n)):
        m = np.kron(np.eye(block), m).astype(np.float32)
        hi = jnp.asarray(m, F32).astype(BF16)
        lo = (jnp.asarray(m, F32) - hi.astype(F32)).astype(BF16)
        out += [hi, lo]
    return out


def _fnet(u_f, seq_len):
    n_tok = u_f.shape[0]
    consts = _dft_consts(seq_len) + _dft_consts(FNET_GROUP_DIM, FNET_GROUPS)
    const = lambda b: (0, 0)
    return pl.pallas_call(
        _fnet_body,
        grid=(n_tok // seq_len,),
        in_specs=[pl.BlockSpec((seq_len, D_FNET), lambda b: (b, 0))]
        + [pl.BlockSpec(c.shape, const) for c in consts],
        out_specs=pl.BlockSpec((seq_len, D_FNET), lambda b: (b, 0)),
        out_shape=jax.ShapeDtypeStruct((n_tok, D_FNET), F32),
        compiler_params=_cp(1),
    )(u_f, *consts)


def _ffn_body(final, yr_ref, yd_ref, yf_ref, x_ref, g1_ref, sh2_ref, sc2_ref, g2_ref, n2_ref, fg_ref,
              wo_ref, wi_ref, wf_ref, o_ref):
    y = (_bdot(yr_ref[...], wo_ref[0:D_RWKV, :])
         + _bdot(yd_ref[...], wo_ref[D_RWKV:D_RWKV + D_DIFF, :])
         + _bdot(yf_ref[...], wo_ref[D_RWKV + D_DIFF:, :]))
    x = x_ref[...] + g1_ref[0] * y
    h = _rms(x, n2_ref[...]) * (1.0 + sc2_ref[0]) + sh2_ref[0]
    z = jnp.dot(h.astype(BF16), wi_ref[...], preferred_element_type=F32)
    gate = z[:, :D_FF]
    act = gate * jax.nn.sigmoid(gate) * z[:, D_FF:]
    x = x + g2_ref[0] * _bdot(act, wf_ref[...])
    o_ref[...] = _rms(x, fg_ref[...]) if final else x


def _ffn(y_r, y_d, y_f, x, mod, layer, row_fn, n2, fg, wo, wi, wf, final, tm):
    n_tok = x.shape[0]
    row = lambda i: (i, 0)
    const = lambda i: (0, 0)
    return pl.pallas_call(
        functools.partial(_ffn_body, final),
        grid=(n_tok // tm,),
        in_specs=[pl.BlockSpec((tm, D_RWKV), row),
                  pl.BlockSpec((tm, D_DIFF), row),
                  pl.BlockSpec((tm, D_FNET), row),
                  pl.BlockSpec((tm, D_MODEL), row),
                  _mod_spec(layer, 2, row_fn),
                  _mod_spec(layer, 3, row_fn),
                  _mod_spec(layer, 4, row_fn),
                  _mod_spec(layer, 5, row_fn),
                  pl.BlockSpec((1, D_MODEL), const),
                  pl.BlockSpec((1, D_MODEL), const),
                  pl.BlockSpec(wo.shape, const),
                  pl.BlockSpec(wi.shape, const),
                  pl.BlockSpec(wf.shape, const)],
        out_specs=pl.BlockSpec((tm, D_MODEL), row),
        out_shape=jax.ShapeDtypeStruct((n_tok, D_MODEL), F32),
        compiler_params=_cp(1),
    )(y_r, y_d, y_f, x, mod, mod, mod, mod, n2.reshape(1, D_MODEL), fg.reshape(1, D_MODEL), wo, wi, wf)


def _block_diag_state(s):
    b = s.shape[0]
    s = s.reshape(b, 2, N_PAIR, 2, HEAD_DIM, HEAD_DIM)
    eye = jnp.eye(2, dtype=s.dtype)
    s = s[:, :, :, :, :, None, :] * eye[None, None, None, :, None, :, None]
    return s.reshape(b, 2, N_PAIR, LANES, LANES)


def _diag_blocks(s):
    b = s.shape[0]
    s = s.reshape(b, 2, N_PAIR, 2, HEAD_DIM, 2, HEAD_DIM)
    s = jnp.stack([s[:, :, :, 0, :, 0, :], s[:, :, :, 1, :, 1, :]], axis=3)
    return s.reshape(b, 2, H_RWKV, HEAD_DIM, HEAD_DIM)


def kernel(x_prompt, x_sample, c, state_rwkv, cache_diff_k, cache_diff_v, c_ctx, norm1_g, norm2_g, final_norm_g, w_mod, b_mod, w_in, w_out, shift_mu, decay_w0, decay_up, iclr_a0, iclr_up, gate_up, k_k, k_a, r_k, lnx_g, lnx_b, diff_lambda, subln_g, w_ffn_in, w_ffn_out):
    p = dict(shift_mu=shift_mu, decay_w0=decay_w0, decay_up=decay_up, iclr_a0=iclr_a0, iclr_up=iclr_up,
             gate_up=gate_up, k_k=k_k, k_a=k_a, r_k=r_k, lnx_g=lnx_g, lnx_b=lnx_b)
    n_ctx, t_ctx, _ = x_prompt.shape
    n_dec, t_dec, _ = x_sample.shape
    past = cache_diff_k.shape[2]

    cond = jnp.concatenate([c_ctx[None, :], c, jnp.zeros((MOD_ROWS - 1 - n_dec, D_MODEL), F32)], axis=0)
    mod = _modulation(cond, w_mod, b_mod).reshape(DEPTH * MOD_ROWS, 1, 6 * D_MODEL)

    tm_in, tm_ffn = 512, 256
    streams = [
        dict(x=x_prompt.reshape(n_ctx * t_ctx, D_MODEL), t=t_ctx, n=n_ctx, n_sub=4,
             row_in=lambda i: 0, row_ffn=lambda i: 0),
        dict(x=x_sample.reshape(n_dec * t_dec, D_MODEL), t=t_dec, n=n_dec, n_sub=1,
             row_in=lambda i: 1 + i // (t_dec // tm_in), row_ffn=lambda i: 1 + i // (t_dec // tm_ffn)),
    ]
    cos, sin = _rope_tables(t_dec)
    states, ks, vs = [], [], []
    for l in range(DEPTH):
        w_in_l = jnp.concatenate(
            [w_in[l, :, :N_RWKV_IN], jnp.zeros((D_MODEL, N_RWKV_PAD - N_RWKV_IN), F32),
             w_in[l, :, N_RWKV_IN:]], axis=1).astype(BF16)
        wo = w_out[l].astype(BF16)
        wi = w_ffn_in[l].astype(BF16)
        wf = w_ffn_out[l].astype(BF16)
        rw = _rwkv_weights(p, l)
        lam_init = 0.8 - 0.6 * math.exp(-0.3 * l)
        sg = jnp.tile(subln_g[l], 2).reshape(1, LANES)
        for si, st in enumerate(streams):
            u_r, u_qkv, u_f = _inproj(st['x'], mod, l, st['row_in'], norm1_g[l], w_in_l, tm_in)
            if si == 0:
                s0 = jnp.zeros((st['n'], 2, N_PAIR, LANES, LANES), F32)
                attn_ctx = None
            else:
                s0 = _block_diag_state(state_rwkv[:, l].astype(F32))
                attn_ctx = (cache_diff_k[:, l].reshape(n_dec, past, D_DIFF).astype(F32),
                            cache_diff_v[:, l].reshape(n_dec, past, D_DIFF).astype(F32), cos, sin)
            y_r, s_fin = _rwkv(u_r, s0, st['t'], st['n_sub'], rw)
            y_d = _attention(u_qkv, st['t'], lam_init, diff_lambda[l], sg, attn_ctx)
            y_f = _fnet(u_f, st['t'])
            st['x'] = _ffn(y_r, y_d, y_f, st['x'], mod, l, st['row_ffn'], norm2_g[l], final_norm_g,
                           wo, wi, wf, l == DEPTH - 1, tm_ffn)
            if si == 0:
                states.append(_diag_blocks(s_fin))
                ks.append(u_qkv[:, D_DIFF:2 * D_DIFF].reshape(n_ctx, t_ctx, H_DIFF, 2, D_QK))
                vs.append(u_qkv[:, 2 * D_DIFF:].reshape(n_ctx, t_ctx, H_DIFF, HEAD_DIM))
    y_prompt = streams[0]['x'].reshape(n_ctx, t_ctx, D_MODEL)
    y_sample = streams[1]['x'].reshape(n_dec, t_dec, D_MODEL)
    return (y_prompt, y_sample, jnp.stack(states, axis=1), jnp.stack(ks, axis=1), jnp.stack(vs, axis=1))
```

```python
import functools
import math

import numpy as np
import jax
import jax.numpy as jnp
from jax import lax
from jax.experimental import pallas as pl
from jax.experimental.pallas import tpu as pltpu

F32 = jnp.float32
BF16 = jnp.bfloat16

D_MODEL = 1024
DEPTH = 2
GRID_W = 64
HEAD_DIM = 64
D_RWKV = 384
H_RWKV = D_RWKV // HEAD_DIM
D_DIFF = 384
H_DIFF = D_DIFF // HEAD_DIM
D_QK = HEAD_DIM // 2
D_FNET = D_MODEL - D_RWKV - D_DIFF
FNET_GROUPS = 4
FNET_GROUP_DIM = D_FNET // FNET_GROUPS
LORA_W = 32
LORA_A = 32
LORA_G = 64
N_RWKV_IN = 3 * D_RWKV + 2 * LORA_W + 2 * LORA_A + LORA_G
N_DIFF_IN = 3 * D_DIFF
D_FF = ((8 * D_MODEL + 3 * 256 - 1) // (3 * 256)) * 256
ROPE_PAIRS = D_QK // 4
ROPE_BASE = 10000.0
RMS_EPS = 1e-6
GN_EPS = 64e-5
SUBLN_EPS = 1e-5
DECAY_SCALE = math.exp(-0.5)

LANES = 128
N_RWKV_PAD = 11 * LANES
D_IN_PAD = N_RWKV_PAD + N_DIFF_IN + D_FNET
N_PAIR = H_RWKV // 2
CHUNK = 64
MOD_ROWS = 8
VMEM_LIMIT = 56 * 1024 * 1024


def _cp(n_axes=1):
    return pltpu.CompilerParams(dimension_semantics=("arbitrary",) * n_axes,
                                vmem_limit_bytes=VMEM_LIMIT)


def _bdot(a, b):
    return jnp.dot(a.astype(BF16), b.astype(BF16), preferred_element_type=F32)


def _bdot_nt(a, b):
    return lax.dot_general(a.astype(BF16), b.astype(BF16), (((1,), (1,)), ((), ())),
                           preferred_element_type=F32)


def _split2(x):
    hi = x.astype(BF16)
    lo = (x - hi.astype(F32)).astype(BF16)
    return hi, lo


def _dot_x3(a, b):
    a_hi, a_lo = _split2(a)
    b_hi, b_lo = _split2(b)
    d = functools.partial(jnp.dot, preferred_element_type=F32)
    return d(a_hi, b_hi) + d(a_lo, b_hi) + d(a_hi, b_lo)


def _dot_exact_rhs(c_bf16, x):
    d = functools.partial(jnp.dot, preferred_element_type=F32)
    hi, lo = _split2(x)
    return d(c_bf16, hi) + d(c_bf16, lo)


def _dot_exact_lhs(x, c_bf16):
    d = functools.partial(jnp.dot, preferred_element_type=F32)
    hi, lo = _split2(x)
    return d(hi, c_bf16) + d(lo, c_bf16)


def _rms(x, g):
    return x * lax.rsqrt(jnp.mean(x * x, axis=-1, keepdims=True) + RMS_EPS) * g


def _mod_body(c_ref, w_ref, b_ref, o_ref):
    c = c_ref[...]
    a = c * jax.nn.sigmoid(c)
    o_ref[0] = _dot_x3(a, w_ref[0]) + b_ref[0]


def _modulation(cond, w_mod, b_mod):
    n_layers, _, n_out = w_mod.shape
    tn = 1536
    return pl.pallas_call(
        _mod_body,
        grid=(n_layers, n_out // tn),
        in_specs=[pl.BlockSpec((MOD_ROWS, D_MODEL), lambda l, j: (0, 0)),
                  pl.BlockSpec((1, D_MODEL, tn), lambda l, j: (l, 0, j)),
                  pl.BlockSpec((1, 1, tn), lambda l, j: (l, 0, j))],
        out_specs=pl.BlockSpec((1, MOD_ROWS, tn), lambda l, j: (l, 0, j)),
        out_shape=jax.ShapeDtypeStruct((n_layers, MOD_ROWS, n_out), F32),
        compiler_params=_cp(2),
    )(cond, w_mod, b_mod.reshape(n_layers, 1, n_out))


def _mod_spec(layer, col, row_fn):
    return pl.BlockSpec((1, 1, D_MODEL), lambda i: (layer * MOD_ROWS + row_fn(i), 0, col))


def _inproj_body(x_ref, g_ref, sh_ref, sc_ref, w_ref, ur_ref, uq_ref, uf_ref):
    h = _rms(x_ref[...], g_ref[...]) * (1.0 + sc_ref[0]) + sh_ref[0]
    u = jnp.dot(h.astype(BF16), w_ref[...], preferred_element_type=F32)
    ur_ref[...] = u[:, :N_RWKV_PAD]
    uq_ref[...] = u[:, N_RWKV_PAD:N_RWKV_PAD + N_DIFF_IN]
    uf_ref[...] = u[:, N_RWKV_PAD + N_DIFF_IN:]


def _inproj(x, mod, layer, row_fn, g, w, tm):
    n_tok = x.shape[0]
    row = lambda i: (i, 0)
    const = lambda i: (0, 0)
    return pl.pallas_call(
        _inproj_body,
        grid=(n_tok // tm,),
        in_specs=[pl.BlockSpec((tm, D_MODEL), row),
                  pl.BlockSpec((1, D_MODEL), const),
                  _mod_spec(layer, 0, row_fn),
                  _mod_spec(layer, 1, row_fn),
                  pl.BlockSpec((D_MODEL, D_IN_PAD), const)],
        out_specs=[pl.BlockSpec((tm, N_RWKV_PAD), row),
                   pl.BlockSpec((tm, N_DIFF_IN), row),
                   pl.BlockSpec((tm, D_FNET), row)],
        out_shape=[jax.ShapeDtypeStruct((n_tok, N_RWKV_PAD), F32),
                   jax.ShapeDtypeStruct((n_tok, N_DIFF_IN), F32),
                   jax.ShapeDtypeStruct((n_tok, D_FNET), F32)],
        compiler_params=_cp(1),
    )(x, g.reshape(1, D_MODEL), mod, mod, w)


def _lane_masks():
    lane = lax.broadcasted_iota(jnp.int32, (1, LANES), 1)
    return (lane < HEAD_DIM).astype(F32), (lane >= HEAD_DIM).astype(F32)


def _rwkv_units(units, m0, m1):
    def bd(x):
        return jnp.concatenate([x * m0, x * m1], axis=0)

    row = lax.broadcasted_iota(jnp.int32, (CHUNK, LANES), 0)
    col = lax.broadcasted_iota(jnp.int32, (CHUNK, LANES), 1) & (CHUNK - 1)
    eye = (col == row).astype(F32)
    r2 = lax.broadcasted_iota(jnp.int32, (LANES, LANES), 0) < HEAD_DIM
    c2 = lax.broadcasted_iota(jnp.int32, (LANES, LANES), 1) < HEAD_DIM
    rng = range(len(units))

    pre = []
    for rev, kk, r, v, kd, b, cum, lw, tot, s_prev in units:
        p_inv = jnp.exp(-cum)
        p_rem = jnp.exp(tot - cum)
        ab = -kk * jnp.exp(cum - lw)
        rb = r * jnp.exp(cum)
        strict = (col > row) if rev else (col < row)
        incl = (col >= row) if rev else (col <= row)
        pre.append(dict(ab=ab, rb=rb, vbd=bd(v), strict=strict, incl=incl,
                        lhs=jnp.concatenate([ab, rb], axis=0),
                        rhs=jnp.concatenate([bd(b * p_inv), bd(kd * p_inv)], axis=0),
                        bk=jnp.concatenate([b * p_rem, kd * p_rem], axis=0)))

    mm = [_bdot_nt(q['lhs'], q['rhs']) for q in pre]
    m_ab = [jnp.where(pre[i]['strict'], mm[i][:CHUNK, :LANES], 0.0) for i in rng]
    m_ak = [jnp.where(pre[i]['strict'], mm[i][:CHUNK, LANES:], 0.0) for i in rng]
    m_r = [jnp.concatenate([jnp.where(pre[i]['incl'], mm[i][CHUNK:, :LANES], 0.0),
                            jnp.where(pre[i]['incl'], mm[i][CHUNK:, LANES:], 0.0)], axis=1) for i in rng]
    mv = [_bdot(m_ak[i], pre[i]['vbd']) for i in rng]

    t = [eye + m_ab[i] for i in rng]
    n = [_bdot(m_ab[i], bd(m_ab[i])) for i in rng]
    for _ in range(4):
        x = [_bdot(jnp.concatenate([t[i], n[i]], axis=0), bd(n[i])) for i in rng]
        t = [t[i] + x[i][:CHUNK] for i in rng]
        n = [x[i][CHUNK:] for i in rng]
    t = [t[i] + _bdot(t[i], bd(n[i])) for i in rng]

    w = [_bdot(t[i], jnp.concatenate([bd(pre[i]['ab']), bd(mv[i])], axis=1)) for i in rng]
    xs = [_bdot_nt(jnp.concatenate([w[i][:, :LANES], pre[i]['rb']], axis=0), units[i][9]) for i in rng]
    u = [xs[i][:CHUNK] + w[i][:, LANES:] for i in rng]
    y = [xs[i][CHUNK:] + _bdot(m_r[i], jnp.concatenate([bd(u[i]), pre[i]['vbd']], axis=0)) for i in rng]
    z = [_bdot(jnp.concatenate([u[i], units[i][3]], axis=0).T, pre[i]['bk']) for i in rng]
    s_new = [units[i][9] * jnp.exp(units[i][8]) + jnp.where(r2 == c2, z[i], 0.0) for i in rng]
    return y, s_new


def _rwkv_body(seq_len, n_sub, has_s0, u_ref, *refs):
    s0_ref = refs[0] if has_s0 else None
    (mu_ref, wdec_ref, wicl_ref, wg_ref, w0a0_ref, vec_ref, bo_ref, tril_ref, triu_ref, y_ref, sfin_ref,
     r_s, v_s, kk_s, g_s, bon_s, kd_s, b_s, lw_s, y_s, st_s) = refs[1:] if has_s0 else refs
    n_chunk = seq_len // CHUNK
    blk = 256
    k_k = vec_ref[0:1, :]
    k_a = vec_ref[1:2, :]
    r_k = vec_ref[2:3, :]
    lnx_g = vec_ref[3:4, :]
    lnx_b = vec_ref[4:5, :]
    bo = bo_ref[...]

    def headsum(x):
        return _dot_exact_lhs(x, bo)

    mu0 = mu_ref[0:1, :]
    mu1 = mu_ref[1:2, :]
    rid = lax.broadcasted_iota(jnp.int32, (blk, 1), 0)
    for j in range(n_sub * seq_len // blk):
        r0 = j * blk
        cur = u_ref[r0:r0 + blk, :]
        first = jnp.zeros((1, N_RWKV_PAD), F32) if r0 % seq_len == 0 else u_ref[r0 - 1:r0, :]
        last = (jnp.zeros((1, N_RWKV_PAD), F32) if (r0 + blk) % seq_len == 0
                else u_ref[r0 + blk:r0 + blk + 1, :])
        prev = jnp.where(rid == 0, first, pltpu.roll(cur, 1, axis=0))
        nxt = jnp.where(rid == blk - 1, last, pltpu.roll(cur, blk - 1, axis=0))
        xs = cur + mu0 * (prev - cur) + mu1 * (nxt - cur)
        r = xs[:, 0:D_RWKV]
        k = xs[:, D_RWKV:2 * D_RWKV]
        v = xs[:, 2 * D_RWKV:3 * D_RWKV]
        lora = xs[:, 3 * D_RWKV:3 * D_RWKV + LANES]
        gd = xs[:, 3 * D_RWKV + LANES:3 * D_RWKV + 2 * LANES]
        dec = _bdot(jnp.tanh(lora), wdec_ref[...])
        icl = _bdot(lora, wicl_ref[...])
        g = _bdot(jax.nn.sigmoid(gd), wg_ref[...])
        logw = -DECAY_SCALE * jax.nn.sigmoid(w0a0_ref[0:1, :] + dec)
        a = jax.nn.sigmoid(w0a0_ref[1:2, :] + icl)
        kk = k * k_k
        kk = kk / jnp.maximum(jnp.sqrt(headsum(kk * kk)), 1e-12)
        a_f = a[:, :D_RWKV]
        a_b = a[:, D_RWKV:]
        kd_f = k * (1.0 + (a_f - 1.0) * k_a)
        kd_b = k * (1.0 + (a_b - 1.0) * k_a)
        rows = slice(r0, r0 + blk)
        r_s[rows, :] = r
        v_s[rows, :] = v
        kk_s[rows, :] = kk
        g_s[rows, :] = g
        bon_s[rows, :] = headsum(r * (kd_f + kd_b) * r_k) * v
        kd_s[0, rows, :] = kd_f
        kd_s[1, rows, :] = kd_b
        b_s[0, rows, :] = kk * a_f
        b_s[1, rows, :] = kk * a_b
        lw_s[0, rows, :] = logw[:, :D_RWKV]
        lw_s[1, rows, :] = logw[:, D_RWKV:]

    st_s[...] = s0_ref[...] if has_s0 else jnp.zeros_like(st_s)
    y_s[...] = jnp.zeros_like(y_s)
    m0, m1 = _lane_masks()

    def chunk_step(i, carry):
        units, rows_sd = [], []
        for s in range(n_sub):
            for d in range(2):
                c = i if d == 0 else n_chunk - 1 - i
                rows = pl.ds(pl.multiple_of(s * seq_len + c * CHUNK, CHUNK), CHUNK)
                rows_sd.append(rows)
                lw = lw_s[d, rows, :]
                tri = tril_ref[...] if d == 0 else triu_ref[...]
                cum = _dot_exact_rhs(tri, lw)
                tot = jnp.sum(lw, axis=0, keepdims=True)
                kk = kk_s[rows, :]
                r = r_s[rows, :]
                v = v_s[rows, :]
                kd = kd_s[d, rows, :]
                b = b_s[d, rows, :]
                for p in range(N_PAIR):
                    sl = slice(p * LANES, (p + 1) * LANES)
                    units.append((d == 1, kk[:, sl], r[:, sl], v[:, sl], kd[:, sl], b[:, sl],
                                  cum[:, sl], lw[:, sl], tot[:, sl], st_s[s, d, p]))
        ys, s_new = _rwkv_units(units, m0, m1)
        for s in range(n_sub):
            for d in range(2):
                base = (s * 2 + d) * N_PAIR
                for p in range(N_PAIR):
                    st_s[s, d, p] = s_new[base + p]
                rows = rows_sd[s * 2 + d]
                y_s[rows, :] = y_s[rows, :] + jnp.concatenate(ys[base:base + N_PAIR], axis=1)
        return carry

    lax.fori_loop(0, n_chunk, chunk_step, 0)
    for s in range(n_sub):
        for d in range(2):
            for p in range(N_PAIR):
                st = st_s[s, d, p]
                sfin_ref[s, d, 2 * p] = st[:HEAD_DIM, :HEAD_DIM]
                sfin_ref[s, d, 2 * p + 1] = st[HEAD_DIM:, HEAD_DIM:]

    for j in range(n_sub * seq_len // blk):
        rows = slice(j * blk, (j + 1) * blk)
        y = y_s[rows, :]
        mean = headsum(y) * (1.0 / HEAD_DIM)
        yc = y - mean
        var = headsum(yc * yc) * (1.0 / HEAD_DIM)
        yn = yc * lax.rsqrt(var + GN_EPS) * lnx_g + lnx_b
        y_ref[rows, :] = (yn + bon_s[rows, :]) * g_s[rows, :]


def _rwkv(u_r, s0, seq_len, n_sub, wts):
    n_seq = u_r.shape[0] // seq_len
    rows = n_sub * seq_len
    const2 = lambda b: (0, 0)
    st_shape = (n_sub, 2, N_PAIR, LANES, LANES)
    tok = pltpu.VMEM((rows, D_RWKV), F32)
    tok2 = pltpu.VMEM((2, rows, D_RWKV), F32)
    in_specs = [pl.BlockSpec((rows, N_RWKV_PAD), lambda b: (b, 0))]
    args = [u_r]
    if s0 is not None:
        in_specs.append(pl.BlockSpec(st_shape, lambda b: (b, 0, 0, 0, 0)))
        args.append(_block_diag_state(s0))
    in_specs += [pl.BlockSpec(w.shape, const2) for w in wts]
    return pl.pallas_call(
        functools.partial(_rwkv_body, seq_len, n_sub, s0 is not None),
        grid=(n_seq // n_sub,),
        in_specs=in_specs,
        out_specs=[pl.BlockSpec((rows, D_RWKV), lambda b: (b, 0)),
                   pl.BlockSpec((n_sub, 2, H_RWKV, HEAD_DIM, HEAD_DIM), lambda b: (b, 0, 0, 0, 0))],
        out_shape=[jax.ShapeDtypeStruct((n_seq * seq_len, D_RWKV), F32),
                   jax.ShapeDtypeStruct((n_seq, 2, H_RWKV, HEAD_DIM, HEAD_DIM), F32)],
        scratch_shapes=[tok] * 5 + [tok2] * 3 + [tok, pltpu.VMEM(st_shape, F32)],
        compiler_params=_cp(1),
    )(*args, *wts)


def _rwkv_weights(p, l):
    z = functools.partial(jnp.zeros, dtype=F32)
    mu = jnp.concatenate([p['shift_mu'][l], z((2, N_RWKV_PAD - N_RWKV_IN))], axis=1)
    wdec = z((LANES, 2 * D_RWKV))
    wdec = wdec.at[0:LORA_W, :D_RWKV].set(p['decay_up'][l, 0])
    wdec = wdec.at[LORA_W:2 * LORA_W, D_RWKV:].set(p['decay_up'][l, 1])
    wicl = z((LANES, 2 * D_RWKV))
    wicl = wicl.at[2 * LORA_W:2 * LORA_W + LORA_A, :D_RWKV].set(p['iclr_up'][l, 0])
    wicl = wicl.at[2 * LORA_W + LORA_A:2 * LORA_W + 2 * LORA_A, D_RWKV:].set(p['iclr_up'][l, 1])
    wg = z((LANES, D_RWKV)).at[0:LORA_G].set(p['gate_up'][l])
    w0a0 = jnp.stack([p['decay_w0'][l].reshape(-1), p['iclr_a0'][l].reshape(-1)])
    vec = jnp.stack([p['k_k'][l], p['k_a'][l], p['r_k'][l].reshape(-1), p['lnx_g'][l], p['lnx_b'][l],
                     z((D_RWKV,)), z((D_RWKV,)), z((D_RWKV,))])
    head = np.arange(D_RWKV) // HEAD_DIM
    bo = jnp.asarray(head[:, None] == head[None, :], BF16)
    idx = np.arange(CHUNK)
    tril = jnp.asarray(idx[None, :] <= idx[:, None], BF16)
    triu = jnp.asarray(idx[None, :] >= idx[:, None], BF16)
    return [mu, wdec.astype(BF16), wicl.astype(BF16), wg.astype(BF16), w0a0, vec, bo, tril, triu]


def _rope(x, cos, sin):
    lane = lax.broadcasted_iota(jnp.int32, (1, LANES), 1)
    first_half = (lane & ROPE_PAIRS) == 0
    partner = jnp.where(first_half, pltpu.roll(x, LANES - ROPE_PAIRS, axis=1),
                        pltpu.roll(x, ROPE_PAIRS, axis=1))
    return x * cos + partner * sin


def _attn_body(has_ctx, lam_init, *refs):
    if has_ctx:
        (q_ref, k_ref, v_ref, lp_ref, sg_ref, kc_ref, vc_ref, cq_ref, sq_ref, ck_ref, sk_ref,
         o_ref) = refs
    else:
        q_ref, k_ref, v_ref, lp_ref, sg_ref, o_ref = refs
    n_pair = D_DIFF // LANES
    lp = lp_ref[...]
    lam = (jnp.exp(jnp.sum(lp[0:1] * lp[1:2], axis=-1, keepdims=True))
           - jnp.exp(jnp.sum(lp[2:3] * lp[3:4], axis=-1, keepdims=True)) + lam_init)
    lane = lax.broadcasted_iota(jnp.int32, (1, LANES), 1)
    head0 = lane < HEAD_DIM

    def group(pairs):
        qs, ks, vs = {}, {}, {}
        for p in pairs:
            sl = slice(p * LANES, (p + 1) * LANES)
            q = q_ref[:, sl]
            k = k_ref[:, sl]
            v = v_ref[:, sl]
            if has_ctx:
                q = _rope(q, cq_ref[...], sq_ref[...])
                k = _rope(k, ck_ref[...], sk_ref[...])
                k = jnp.concatenate([kc_ref[0, :, sl], k], axis=0)
                v = jnp.concatenate([vc_ref[0, :, sl], v], axis=0)
            qs[p] = q * (D_QK ** -0.5)
            ks[p] = k.astype(BF16)
            vs[p] = v.astype(BF16)
        chains = [(p, h, m) for p in pairs for h in range(2) for m in range(2)]
        scores = []
        for p, h, m in chains:
            lo = h * HEAD_DIM + m * D_QK
            sel = (lane >= lo) & (lane < lo + D_QK)
            scores.append(_bdot_nt(jnp.where(sel, qs[p], 0.0), ks[p]))
        es = [jnp.exp(s - jnp.max(s, axis=-1, keepdims=True)) for s in scores]
        inv = [1.0 / jnp.sum(e, axis=-1, keepdims=True) for e in es]
        pv = [jnp.dot(es[i].astype(BF16), vs[chains[i][0]], preferred_element_type=F32)
              for i in range(len(chains))]
        for j, p in enumerate(pairs):
            halves = []
            for h in range(2):
                i = (j * 2 + h) * 2
                o = pv[i] * inv[i] - lam * (pv[i + 1] * inv[i + 1])
                mine = head0 if h == 0 else jnp.logical_not(head0)
                ms = jnp.sum(jnp.where(mine, o * o, 0.0), axis=-1, keepdims=True) * (1.0 / HEAD_DIM)
                halves.append(o * lax.rsqrt(ms + SUBLN_EPS))
            o_ref[:, p * LANES:(p + 1) * LANES] = (jnp.where(head0, halves[0], halves[1])
                                                   * sg_ref[...] * (1.0 - lam_init))

    if has_ctx:
        for p in range(n_pair):
            group([p])
    else:
        group(list(range(n_pair)))


def _attention(u_qkv, seq_len, lam_init, lp, sg, ctx=None):
    n_tok = u_qkv.shape[0]
    n_seq = n_tok // seq_len
    tq = 256
    nq = seq_len // tq
    in_specs = [pl.BlockSpec((tq, D_DIFF), lambda b, i: (b * nq + i, 0)),
                pl.BlockSpec((seq_len, D_DIFF), lambda b, i: (b, 1)),
                pl.BlockSpec((seq_len, D_DIFF), lambda b, i: (b, 2)),
                pl.BlockSpec(lp.shape, lambda b, i: (0, 0)),
                pl.BlockSpec((1, LANES), lambda b, i: (0, 0))]
    args = [u_qkv, u_qkv, u_qkv, lp, sg]
    if ctx is not None:
        kc, vc, cos, sin = ctx
        past = kc.shape[1]
        in_specs += [pl.BlockSpec((1, past, D_DIFF), lambda b, i: (b, 0, 0)),
                     pl.BlockSpec((1, past, D_DIFF), lambda b, i: (b, 0, 0)),
                     pl.BlockSpec((tq, LANES), lambda b, i: (i, 0)),
                     pl.BlockSpec((tq, LANES), lambda b, i: (i, 0)),
                     pl.BlockSpec((seq_len, LANES), lambda b, i: (0, 0)),
                     pl.BlockSpec((seq_len, LANES), lambda b, i: (0, 0))]
        args += [kc, vc, cos, sin, cos, sin]
    return pl.pallas_call(
        functools.partial(_attn_body, ctx is not None, lam_init),
        grid=(n_seq, nq),
        in_specs=in_specs,
        out_specs=pl.BlockSpec((tq, D_DIFF), lambda b, i: (b * nq + i, 0)),
        out_shape=jax.ShapeDtypeStruct((n_tok, D_DIFF), F32),
        compiler_params=_cp(2),
    )(*args)


def _rope_tables(seq_len):
    t = jnp.arange(seq_len)
    pos = jnp.stack([(t // GRID_W).astype(F32), (t % GRID_W).astype(F32)], axis=1)
    inv = 1.0 / (ROPE_BASE ** (jnp.arange(ROPE_PAIRS, dtype=F32) / ROPE_PAIRS))
    ang = pos[:, :, None] * inv
    d = np.arange(LANES) % D_QK
    axis = d // (2 * ROPE_PAIRS)
    second = (d % (2 * ROPE_PAIRS)) // ROPE_PAIRS
    idx = d % ROPE_PAIRS
    cos = jnp.cos(ang)[:, axis, idx]
    sin = jnp.sin(ang)[:, axis, idx] * jnp.asarray(np.where(second == 1, 1.0, -1.0), F32)
    return cos, sin


def _fnet_body(x_ref, cth_ref, ctl_ref, sth_ref, stl_ref, cch_ref, ccl_ref, sch_ref, scl_ref, o_ref):
    d = functools.partial(jnp.dot, preferred_element_type=F32)
    x_hi, x_lo = _split2(x_ref[...])
    xc = d(x_hi, cch_ref[...]) + d(x_lo, cch_ref[...]) + d(x_hi, ccl_ref[...])
    xs = d(x_hi, sch_ref[...]) + d(x_lo, sch_ref[...]) + d(x_hi, scl_ref[...])
    c_hi, c_lo = _split2(xc)
    s_hi, s_lo = _split2(xs)
    yc = d(cth_ref[...], c_hi) + d(cth_ref[...], c_lo) + d(ctl_ref[...], c_hi)
    ys = d(sth_ref[...], s_hi) + d(sth_ref[...], s_lo) + d(stl_ref[...], s_hi)
    o_ref[...] = yc - ys


def _dft_consts(n, block=1):
    idx = np.arange(n)
    ang = 2.0 * np.pi * ((idx[:, None] * idx[None, :]) % n) / n
    out = []
    for m in (np.cos(ang) / np.sqrt(n), np.sin(ang) / np.sqrt(n)):
        m = np.kron(np.eye(block), m).astype(np.float32)
        hi = jnp.asarray(m, F32).astype(BF16)
        lo = (jnp.asarray(m, F32) - hi.astype(F32)).astype(BF16)
        out += [hi, lo]
    return out


def _fnet(u_f, seq_len):
    n_tok = u_f.shape[0]
    consts = _dft_consts(seq_len) + _dft_consts(FNET_GROUP_DIM, FNET_GROUPS)
    const = lambda b: (0, 0)
    return pl.pallas_call(
        _fnet_body,
        grid=(n_tok // seq_len,),
        in_specs=[pl.BlockSpec((seq_len, D_FNET), lambda b: (b, 0))]
        + [pl.BlockSpec(c.shape, const) for c in consts],
        out_specs=pl.BlockSpec((seq_len, D_FNET), lambda b: (b, 0)),
        out_shape=jax.ShapeDtypeStruct((n_tok, D_FNET), F32),
        compiler_params=_cp(1),
    )(u_f, *consts)


def _ffn_body(final, yr_ref, yd_ref, yf_ref, x_ref, g1_ref, sh2_ref, sc2_ref, g2_ref, n2_ref, fg_ref,
              wo_ref, wi_ref, wf_ref, o_ref):
    y = (_bdot(yr_ref[...], wo_ref[0:D_RWKV, :])
         + _bdot(yd_ref[...], wo_ref[D_RWKV:D_RWKV + D_DIFF, :])
         + _bdot(yf_ref[...], wo_ref[D_RWKV + D_DIFF:, :]))
    x = x_ref[...] + g1_ref[0] * y
    h = _rms(x, n2_ref[...]) * (1.0 + sc2_ref[0]) + sh2_ref[0]
    z = jnp.dot(h.astype(BF16), wi_ref[...], preferred_element_type=F32)
    gate = z[:, :D_FF]
    act = gate * jax.nn.sigmoid(gate) * z[:, D_FF:]
    x = x + g2_ref[0] * _bdot(act, wf_ref[...])
    o_ref[...] = _rms(x, fg_ref[...]) if final else x


def _ffn(y_r, y_d, y_f, x, mod, layer, row_fn, n2, fg, wo, wi, wf, final, tm):
    n_tok = x.shape[0]
    row = lambda i: (i, 0)
    const = lambda i: (0, 0)
    return pl.pallas_call(
        functools.partial(_ffn_body, final),
        grid=(n_tok // tm,),
        in_specs=[pl.BlockSpec((tm, D_RWKV), row),
                  pl.BlockSpec((tm, D_DIFF), row),
                  pl.BlockSpec((tm, D_FNET), row),
                  pl.BlockSpec((tm, D_MODEL), row),
                  _mod_spec(layer, 2, row_fn),
                  _mod_spec(layer, 3, row_fn),
                  _mod_spec(layer, 4, row_fn),
                  _mod_spec(layer, 5, row_fn),
                  pl.BlockSpec((1, D_MODEL), const),
                  pl.BlockSpec((1, D_MODEL), const),
                  pl.BlockSpec(wo.shape, const),
                  pl.BlockSpec(wi.shape, const),
                  pl.BlockSpec(wf.shape, const)],
        out_specs=pl.BlockSpec((tm, D_MODEL), row),
        out_shape=jax.ShapeDtypeStruct((n_tok, D_MODEL), F32),
        compiler_params=_cp(1),
    )(y_r, y_d, y_f, x, mod, mod, mod, mod, n2.reshape(1, D_MODEL), fg.reshape(1, D_MODEL), wo, wi, wf)


def _block_diag_state(s):
    b = s.shape[0]
    s = s.reshape(b, 2, N_PAIR, 2, HEAD_DIM, HEAD_DIM)
    eye = jnp.eye(2, dtype=s.dtype)
    s = s[:, :, :, :, :, None, :] * eye[None, None, None, :, None, :, None]
    return s.reshape(b, 2, N_PAIR, LANES, LANES)


def kernel(x_prompt, x_sample, c, state_rwkv, cache_diff_k, cache_diff_v, c_ctx, norm1_g, norm2_g, final_norm_g, w_mod, b_mod, w_in, w_out, shift_mu, decay_w0, decay_up, iclr_a0, iclr_up, gate_up, k_k, k_a, r_k, lnx_g, lnx_b, diff_lambda, subln_g, w_ffn_in, w_ffn_out):
    p = dict(shift_mu=shift_mu, decay_w0=decay_w0, decay_up=decay_up, iclr_a0=iclr_a0, iclr_up=iclr_up,
             gate_up=gate_up, k_k=k_k, k_a=k_a, r_k=r_k, lnx_g=lnx_g, lnx_b=lnx_b)
    n_ctx, t_ctx, _ = x_prompt.shape
    n_dec, t_dec, _ = x_sample.shape
    past = cache_diff_k.shape[2]

    cond = jnp.concatenate([c_ctx[None, :], c, jnp.zeros((MOD_ROWS - 1 - n_dec, D_MODEL), F32)], axis=0)
    mod = _modulation(cond, w_mod, b_mod).reshape(DEPTH * MOD_ROWS, 1, 6 * D_MODEL)

    tm_in, tm_ffn = 512, 256
    streams = [
        dict(x=x_prompt.reshape(n_ctx * t_ctx, D_MODEL), t=t_ctx, n=n_ctx, n_sub=4,
             row_in=lambda i: 0, row_ffn=lambda i: 0),
        dict(x=x_sample.reshape(n_dec * t_dec, D_MODEL), t=t_dec, n=n_dec, n_sub=1,
             row_in=lambda i: 1 + i // (t_dec // tm_in), row_ffn=lambda i: 1 + i // (t_dec // tm_ffn)),
    ]
    cos, sin = _rope_tables(t_dec)
    states, ks, vs = [], [], []
    for l in range(DEPTH):
        w_in_l = jnp.concatenate(
            [w_in[l, :, :N_RWKV_IN], jnp.zeros((D_MODEL, N_RWKV_PAD - N_RWKV_IN), F32),
             w_in[l, :, N_RWKV_IN:]], axis=1).astype(BF16)
        wo = w_out[l].astype(BF16)
        wi = w_ffn_in[l].astype(BF16)
        wf = w_ffn_out[l].astype(BF16)
        rw = _rwkv_weights(p, l)
        lam_init = 0.8 - 0.6 * math.exp(-0.3 * l)
        sg = jnp.tile(subln_g[l], 2).reshape(1, LANES)
        for si, st in enumerate(streams):
            u_r, u_qkv, u_f = _inproj(st['x'], mod, l, st['row_in'], norm1_g[l], w_in_l, tm_in)
            if si == 0:
                s0 = None
                attn_ctx = None
            else:
                s0 = state_rwkv[:, l].astype(F32)
                attn_ctx = (cache_diff_k[:, l].reshape(n_dec, past, D_DIFF).astype(F32),
                            cache_diff_v[:, l].reshape(n_dec, past, D_DIFF).astype(F32), cos, sin)
            y_r, s_fin = _rwkv(u_r, s0, st['t'], st['n_sub'], rw)
            y_d = _attention(u_qkv, st['t'], lam_init, diff_lambda[l], sg, attn_ctx)
            y_f = _fnet(u_f, st['t'])
            st['x'] = _ffn(y_r, y_d, y_f, st['x'], mod, l, st['row_ffn'], norm2_g[l], final_norm_g,
                           wo, wi, wf, l == DEPTH - 1, tm_ffn)
            if si == 0:
                states.append(s_fin)
                ks.append(u_qkv[:, D_DIFF:2 * D_DIFF].reshape(n_ctx, t_ctx, H_DIFF, 2, D_QK))
                vs.append(u_qkv[:, 2 * D_DIFF:].reshape(n_ctx, t_ctx, H_DIFF, HEAD_DIM))
    y_prompt = streams[0]['x'].reshape(n_ctx, t_ctx, D_MODEL)
    y_sample = streams[1]['x'].reshape(n_dec, t_dec, D_MODEL)
    return (y_prompt, y_sample, jnp.stack(states, axis=1), jnp.stack(ks, axis=1), jnp.stack(vs, axis=1))
```

```python
import functools
import math

import numpy as np
import jax
import jax.numpy as jnp
from jax import lax
from jax.experimental import pallas as pl
from jax.experimental.pallas import tpu as pltpu

F32 = jnp.float32
BF16 = jnp.bfloat16

D_MODEL = 1024
DEPTH = 2
GRID_W = 64
HEAD_DIM = 64
D_RWKV = 384
H_RWKV = D_RWKV // HEAD_DIM
D_DIFF = 384
H_DIFF = D_DIFF // HEAD_DIM
D_QK = HEAD_DIM // 2
D_FNET = D_MODEL - D_RWKV - D_DIFF
FNET_GROUPS = 4
FNET_GROUP_DIM = D_FNET // FNET_GROUPS
LORA_W = 32
LORA_A = 32
LORA_G = 64
N_RWKV_IN = 3 * D_RWKV + 2 * LORA_W + 2 * LORA_A + LORA_G
N_DIFF_IN = 3 * D_DIFF
D_FF = ((8 * D_MODEL + 3 * 256 - 1) // (3 * 256)) * 256
ROPE_PAIRS = D_QK // 4
ROPE_BASE = 10000.0
RMS_EPS = 1e-6
GN_EPS = 64e-5
SUBLN_EPS = 1e-5
DECAY_SCALE = math.exp(-0.5)

LANES = 128
N_RWKV_PAD = 11 * LANES
D_IN_PAD = N_RWKV_PAD + N_DIFF_IN + D_FNET
N_PAIR = H_RWKV // 2
CHUNK = 64
MOD_ROWS = 8
ONES_ROWS = 16
VMEM_LIMIT = 56 * 1024 * 1024


def _cp(n_axes=1):
    return pltpu.CompilerParams(dimension_semantics=("arbitrary",) * n_axes,
                                vmem_limit_bytes=VMEM_LIMIT)


def _bdot(a, b):
    return jnp.dot(a.astype(BF16), b.astype(BF16), preferred_element_type=F32)


def _bdot_nt(a, b):
    return lax.dot_general(a.astype(BF16), b.astype(BF16), (((1,), (1,)), ((), ())),
                           preferred_element_type=F32)


def _split2(x):
    hi = x.astype(BF16)
    lo = (x - hi.astype(F32)).astype(BF16)
    return hi, lo


def _dot_x3(a, b):
    a_hi, a_lo = _split2(a)
    b_hi, b_lo = _split2(b)
    d = functools.partial(jnp.dot, preferred_element_type=F32)
    return d(a_hi, b_hi) + d(a_lo, b_hi) + d(a_hi, b_lo)


def _dot_exact_rhs(c_bf16, x):
    d = functools.partial(jnp.dot, preferred_element_type=F32)
    hi, lo = _split2(x)
    return d(c_bf16, hi) + d(c_bf16, lo)


def _dot_exact_lhs(x, c_bf16):
    d = functools.partial(jnp.dot, preferred_element_type=F32)
    hi, lo = _split2(x)
    return d(hi, c_bf16) + d(lo, c_bf16)


def _rms(x, g):
    return x * lax.rsqrt(jnp.mean(x * x, axis=-1, keepdims=True) + RMS_EPS) * g


def _mod_body(c_ref, w_ref, b_ref, o_ref):
    c = c_ref[...]
    a = c * jax.nn.sigmoid(c)
    o_ref[0] = _dot_x3(a, w_ref[0]) + b_ref[0]


def _modulation(cond, w_mod, b_mod):
    n_layers, _, n_out = w_mod.shape
    tn = 1536
    return pl.pallas_call(
        _mod_body,
        grid=(n_layers, n_out // tn),
        in_specs=[pl.BlockSpec((MOD_ROWS, D_MODEL), lambda l, j: (0, 0)),
                  pl.BlockSpec((1, D_MODEL, tn), lambda l, j: (l, 0, j)),
                  pl.BlockSpec((1, 1, tn), lambda l, j: (l, 0, j))],
        out_specs=pl.BlockSpec((1, MOD_ROWS, tn), lambda l, j: (l, 0, j)),
        out_shape=jax.ShapeDtypeStruct((n_layers, MOD_ROWS, n_out), F32),
        compiler_params=_cp(2),
    )(cond, w_mod, b_mod.reshape(n_layers, 1, n_out))


def _mod_spec(layer, col, row_fn):
    return pl.BlockSpec((1, 1, D_MODEL), lambda i: (layer * MOD_ROWS + row_fn(i), 0, col))


def _inproj_body(x_ref, g_ref, sh_ref, sc_ref, w_ref, ur_ref, uq_ref, uf_ref):
    h = _rms(x_ref[...], g_ref[...]) * (1.0 + sc_ref[0]) + sh_ref[0]
    u = jnp.dot(h.astype(BF16), w_ref[...], preferred_element_type=F32)
    ur_ref[...] = u[:, :N_RWKV_PAD]
    uq_ref[...] = u[:, N_RWKV_PAD:N_RWKV_PAD + N_DIFF_IN]
    uf_ref[...] = u[:, N_RWKV_PAD + N_DIFF_IN:]


def _inproj(x, mod, layer, row_fn, g, w, tm):
    n_tok = x.shape[0]
    row = lambda i: (i, 0)
    const = lambda i: (0, 0)
    return pl.pallas_call(
        _inproj_body,
        grid=(n_tok // tm,),
        in_specs=[pl.BlockSpec((tm, D_MODEL), row),
                  pl.BlockSpec((1, D_MODEL), const),
                  _mod_spec(layer, 0, row_fn),
                  _mod_spec(layer, 1, row_fn),
                  pl.BlockSpec((None, D_MODEL, D_IN_PAD), lambda i: (layer, 0, 0))],
        out_specs=[pl.BlockSpec((tm, N_RWKV_PAD), row),
                   pl.BlockSpec((tm, N_DIFF_IN), row),
                   pl.BlockSpec((tm, D_FNET), row)],
        out_shape=[jax.ShapeDtypeStruct((n_tok, N_RWKV_PAD), F32),
                   jax.ShapeDtypeStruct((n_tok, N_DIFF_IN), F32),
                   jax.ShapeDtypeStruct((n_tok, D_FNET), F32)],
        compiler_params=_cp(1),
    )(x, g.reshape(1, D_MODEL), mod, mod, w)


def _lane_masks():
    lane = lax.broadcasted_iota(jnp.int32, (1, LANES), 1)
    return (lane < HEAD_DIM).astype(F32), (lane >= HEAD_DIM).astype(F32)


def _rwkv_units(units, m0, m1):
    def bd(x):
        return jnp.concatenate([x * m0, x * m1], axis=0)

    row = lax.broadcasted_iota(jnp.int32, (CHUNK, LANES), 0)
    col = lax.broadcasted_iota(jnp.int32, (CHUNK, LANES), 1) & (CHUNK - 1)
    eye = (col == row).astype(F32)
    r2 = lax.broadcasted_iota(jnp.int32, (LANES, LANES), 0) < HEAD_DIM
    c2 = lax.broadcasted_iota(jnp.int32, (LANES, LANES), 1) < HEAD_DIM
    rng = range(len(units))

    pre = []
    for rev, kk, r, v, kd, b, cum, lw, tot, s_prev in units:
        p_inv = jnp.exp(-cum)
        p_rem = jnp.exp(tot - cum)
        ab = -kk * jnp.exp(cum - lw)
        rb = r * jnp.exp(cum)
        strict = (col > row) if rev else (col < row)
        incl = (col >= row) if rev else (col <= row)
        pre.append(dict(ab=ab, rb=rb, vbd=bd(v), strict=strict, incl=incl,
                        lhs=jnp.concatenate([ab, rb], axis=0),
                        rhs=jnp.concatenate([bd(b * p_inv), bd(kd * p_inv)], axis=0),
                        bk=jnp.concatenate([b * p_rem, kd * p_rem], axis=0)))

    mm = [_bdot_nt(q['lhs'], q['rhs']) for q in pre]
    m_ab = [jnp.where(pre[i]['strict'], mm[i][:CHUNK, :LANES], 0.0) for i in rng]
    m_ak = [jnp.where(pre[i]['strict'], mm[i][:CHUNK, LANES:], 0.0) for i in rng]
    m_r = [jnp.concatenate([jnp.where(pre[i]['incl'], mm[i][CHUNK:, :LANES], 0.0),
                            jnp.where(pre[i]['incl'], mm[i][CHUNK:, LANES:], 0.0)], axis=1) for i in rng]
    mv = [_bdot(m_ak[i], pre[i]['vbd']) for i in rng]

    t = [eye + m_ab[i] for i in rng]
    n = [_bdot(m_ab[i], bd(m_ab[i])) for i in rng]
    for _ in range(4):
        x = [_bdot(jnp.concatenate([t[i], n[i]], axis=0), bd(n[i])) for i in rng]
        t = [t[i] + x[i][:CHUNK] for i in rng]
        n = [x[i][CHUNK:] for i in rng]
    t = [t[i] + _bdot(t[i], bd(n[i])) for i in rng]

    w = [_bdot(t[i], jnp.concatenate([bd(pre[i]['ab']), bd(mv[i])], axis=1)) for i in rng]
    xs = [_bdot_nt(jnp.concatenate([w[i][:, :LANES], pre[i]['rb']], axis=0), units[i][9]) for i in rng]
    u = [xs[i][:CHUNK] + w[i][:, LANES:] for i in rng]
    y = [xs[i][CHUNK:] + _bdot(m_r[i], jnp.concatenate([bd(u[i]), pre[i]['vbd']], axis=0)) for i in rng]
    z = [_bdot(jnp.concatenate([u[i], units[i][3]], axis=0).T, pre[i]['bk']) for i in rng]
    s_new = [units[i][9] * jnp.exp(units[i][8]) + jnp.where(r2 == c2, z[i], 0.0) for i in rng]
    return y, s_new


def _rwkv_body(seq_len, n_sub, has_s0, u_ref, *refs):
    s0_ref = refs[0] if has_s0 else None
    (mu_ref, wdec_ref, wicl_ref, wg_ref, w0a0_ref, vec_ref, bo_ref, tril_ref, triu_ref, y_ref, sfin_ref,
     r_s, v_s, kk_s, g_s, bon_s, kd_s, b_s, lw_s, y_s, st_s) = refs[1:] if has_s0 else refs
    n_chunk = seq_len // CHUNK
    blk = 256
    k_k = vec_ref[0:1, :]
    k_a = vec_ref[1:2, :]
    r_k = vec_ref[2:3, :]
    lnx_g = vec_ref[3:4, :]
    lnx_b = vec_ref[4:5, :]
    bo = bo_ref[...]

    def headsum(x):
        return _dot_exact_lhs(x, bo)

    mu0 = mu_ref[0:1, :]
    mu1 = mu_ref[1:2, :]
    rid = lax.broadcasted_iota(jnp.int32, (blk, 1), 0)
    for j in range(n_sub * seq_len // blk):
        r0 = j * blk
        cur = u_ref[r0:r0 + blk, :]
        first = jnp.zeros((1, N_RWKV_PAD), F32) if r0 % seq_len == 0 else u_ref[r0 - 1:r0, :]
        last = (jnp.zeros((1, N_RWKV_PAD), F32) if (r0 + blk) % seq_len == 0
                else u_ref[r0 + blk:r0 + blk + 1, :])
        prev = jnp.where(rid == 0, first, pltpu.roll(cur, 1, axis=0))
        nxt = jnp.where(rid == blk - 1, last, pltpu.roll(cur, blk - 1, axis=0))
        xs = cur + mu0 * (prev - cur) + mu1 * (nxt - cur)
        r = xs[:, 0:D_RWKV]
        k = xs[:, D_RWKV:2 * D_RWKV]
        v = xs[:, 2 * D_RWKV:3 * D_RWKV]
        lora = xs[:, 3 * D_RWKV:3 * D_RWKV + LANES]
        gd = xs[:, 3 * D_RWKV + LANES:3 * D_RWKV + 2 * LANES]
        dec = _bdot(jnp.tanh(lora), wdec_ref[...])
        icl = _bdot(lora, wicl_ref[...])
        g = _bdot(jax.nn.sigmoid(gd), wg_ref[...])
        logw = -DECAY_SCALE * jax.nn.sigmoid(w0a0_ref[0:1, :] + dec)
        a = jax.nn.sigmoid(w0a0_ref[1:2, :] + icl)
        kk = k * k_k
        kk = kk / jnp.maximum(jnp.sqrt(headsum(kk * kk)), 1e-12)
        a_f = a[:, :D_RWKV]
        a_b = a[:, D_RWKV:]
        kd_f = k * (1.0 + (a_f - 1.0) * k_a)
        kd_b = k * (1.0 + (a_b - 1.0) * k_a)
        rows = slice(r0, r0 + blk)
        r_s[rows, :] = r
        v_s[rows, :] = v
        kk_s[rows, :] = kk
        g_s[rows, :] = g
        bon_s[rows, :] = headsum(r * (kd_f + kd_b) * r_k) * v
        kd_s[0, rows, :] = kd_f
        kd_s[1, rows, :] = kd_b
        b_s[0, rows, :] = kk * a_f
        b_s[1, rows, :] = kk * a_b
        lw_s[0, rows, :] = logw[:, :D_RWKV]
        lw_s[1, rows, :] = logw[:, D_RWKV:]

    st_s[...] = s0_ref[...] if has_s0 else jnp.zeros_like(st_s)
    y_s[...] = jnp.zeros_like(y_s)
    m0, m1 = _lane_masks()

    def chunk_step(i, carry):
        units, rows_sd = [], []
        for s in range(n_sub):
            for d in range(2):
                c = i if d == 0 else n_chunk - 1 - i
                rows = pl.ds(pl.multiple_of(s * seq_len + c * CHUNK, CHUNK), CHUNK)
                rows_sd.append(rows)
                lw = lw_s[d, rows, :]
                tri = tril_ref[...] if d == 0 else triu_ref[...]
                cum = _dot_exact_rhs(tri, lw)
                tot = jnp.sum(lw, axis=0, keepdims=True)
                kk = kk_s[rows, :]
                r = r_s[rows, :]
                v = v_s[rows, :]
                kd = kd_s[d, rows, :]
                b = b_s[d, rows, :]
                for p in range(N_PAIR):
                    sl = slice(p * LANES, (p + 1) * LANES)
                    units.append((d == 1, kk[:, sl], r[:, sl], v[:, sl], kd[:, sl], b[:, sl],
                                  cum[:, sl], lw[:, sl], tot[:, sl], st_s[s, d, p]))
        ys, s_new = _rwkv_units(units, m0, m1)
        for s in range(n_sub):
            for d in range(2):
                base = (s * 2 + d) * N_PAIR
                for p in range(N_PAIR):
                    st_s[s, d, p] = s_new[base + p]
                rows = rows_sd[s * 2 + d]
                y_s[rows, :] = y_s[rows, :] + jnp.concatenate(ys[base:base + N_PAIR], axis=1)
        return carry

    lax.fori_loop(0, n_chunk, chunk_step, 0)
    for s in range(n_sub):
        for d in range(2):
            for p in range(N_PAIR):
                st = st_s[s, d, p]
                sfin_ref[s, d, 2 * p] = st[:HEAD_DIM, :HEAD_DIM]
                sfin_ref[s, d, 2 * p + 1] = st[HEAD_DIM:, HEAD_DIM:]

    for j in range(n_sub * seq_len // blk):
        rows = slice(j * blk, (j + 1) * blk)
        y = y_s[rows, :]
        mean = headsum(y) * (1.0 / HEAD_DIM)
        yc = y - mean
        var = headsum(yc * yc) * (1.0 / HEAD_DIM)
        yn = yc * lax.rsqrt(var + GN_EPS) * lnx_g + lnx_b
        y_ref[rows, :] = (yn + bon_s[rows, :]) * g_s[rows, :]


def _rwkv(u_r, s0, seq_len, n_sub, wts):
    n_seq = u_r.shape[0] // seq_len
    rows = n_sub * seq_len
    const2 = lambda b: (0, 0)
    st_shape = (n_sub, 2, N_PAIR, LANES, LANES)
    tok = pltpu.VMEM((rows, D_RWKV), F32)
    tok2 = pltpu.VMEM((2, rows, D_RWKV), F32)
    in_specs = [pl.BlockSpec((rows, N_RWKV_PAD), lambda b: (b, 0))]
    args = [u_r]
    if s0 is not None:
        in_specs.append(pl.BlockSpec(st_shape, lambda b: (b, 0, 0, 0, 0)))
        args.append(_block_diag_state(s0))
    in_specs += [pl.BlockSpec(w.shape, const2) for w in wts]
    return pl.pallas_call(
        functools.partial(_rwkv_body, seq_len, n_sub, s0 is not None),
        grid=(n_seq // n_sub,),
        in_specs=in_specs,
        out_specs=[pl.BlockSpec((rows, D_RWKV), lambda b: (b, 0)),
                   pl.BlockSpec((n_sub, 2, H_RWKV, HEAD_DIM, HEAD_DIM), lambda b: (b, 0, 0, 0, 0))],
        out_shape=[jax.ShapeDtypeStruct((n_seq * seq_len, D_RWKV), F32),
                   jax.ShapeDtypeStruct((n_seq, 2, H_RWKV, HEAD_DIM, HEAD_DIM), F32)],
        scratch_shapes=[tok] * 5 + [tok2] * 3 + [tok, pltpu.VMEM(st_shape, F32)],
        compiler_params=_cp(1),
    )(*args, *wts)


def _rwkv_weights(p, l):
    z = functools.partial(jnp.zeros, dtype=F32)
    mu = jnp.concatenate([p['shift_mu'][l], z((2, N_RWKV_PAD - N_RWKV_IN))], axis=1)
    wdec = z((LANES, 2 * D_RWKV))
    wdec = wdec.at[0:LORA_W, :D_RWKV].set(p['decay_up'][l, 0])
    wdec = wdec.at[LORA_W:2 * LORA_W, D_RWKV:].set(p['decay_up'][l, 1])
    wicl = z((LANES, 2 * D_RWKV))
    wicl = wicl.at[2 * LORA_W:2 * LORA_W + LORA_A, :D_RWKV].set(p['iclr_up'][l, 0])
    wicl = wicl.at[2 * LORA_W + LORA_A:2 * LORA_W + 2 * LORA_A, D_RWKV:].set(p['iclr_up'][l, 1])
    wg = z((LANES, D_RWKV)).at[0:LORA_G].set(p['gate_up'][l])
    w0a0 = jnp.stack([p['decay_w0'][l].reshape(-1), p['iclr_a0'][l].reshape(-1)])
    vec = jnp.stack([p['k_k'][l], p['k_a'][l], p['r_k'][l].reshape(-1), p['lnx_g'][l], p['lnx_b'][l],
                     z((D_RWKV,)), z((D_RWKV,)), z((D_RWKV,))])
    head = np.arange(D_RWKV) // HEAD_DIM
    bo = jnp.asarray(head[:, None] == head[None, :], BF16)
    idx = np.arange(CHUNK)
    tril = jnp.asarray(idx[None, :] <= idx[:, None], BF16)
    triu = jnp.asarray(idx[None, :] >= idx[:, None], BF16)
    return [mu, wdec.astype(BF16), wicl.astype(BF16), wg.astype(BF16), w0a0, vec, bo, tril, triu]


def _rope(x, cos, sin):
    lane = lax.broadcasted_iota(jnp.int32, (1, LANES), 1)
    first_half = (lane & ROPE_PAIRS) == 0
    partner = jnp.where(first_half, pltpu.roll(x, LANES - ROPE_PAIRS, axis=1),
                        pltpu.roll(x, ROPE_PAIRS, axis=1))
    return x * cos + partner * sin


def _attn_body(has_ctx, lam_init, *refs):
    if has_ctx:
        (q_ref, k_ref, v_ref, lp_ref, sg_ref, kc_ref, vc_ref, cq_ref, sq_ref, ck_ref, sk_ref,
         o_ref) = refs
    else:
        q_ref, k_ref, v_ref, lp_ref, sg_ref, o_ref = refs
    n_pair = D_DIFF // LANES
    lp = lp_ref[...]
    lam = (jnp.exp(jnp.sum(lp[0:1] * lp[1:2], axis=-1, keepdims=True))
           - jnp.exp(jnp.sum(lp[2:3] * lp[3:4], axis=-1, keepdims=True)) + lam_init)
    lane = lax.broadcasted_iota(jnp.int32, (1, LANES), 1)

    def group(pairs):
        qs, ks, vts = {}, {}, {}
        for p in pairs:
            sl = slice(p * LANES, (p + 1) * LANES)
            q = q_ref[:, sl]
            k = k_ref[:, sl]
            v = v_ref[:, sl]
            if has_ctx:
                q = _rope(q, cq_ref[...], sq_ref[...])
                k = _rope(k, ck_ref[...], sk_ref[...])
                k = jnp.concatenate([kc_ref[0, :, sl], k], axis=0)
                v = jnp.concatenate([vc_ref[0, :, sl], v], axis=0)
            qs[p] = q * (D_QK ** -0.5 * math.log2(math.e))
            ks[p] = k.astype(BF16)
            vts[p] = jnp.concatenate([v.T, jnp.ones((ONES_ROWS, v.shape[0]), F32)], axis=0).astype(BF16)
        chains = [(p, h, m) for p in pairs for h in range(2) for m in range(2)]
        scores = []
        for p, h, m in chains:
            lo = h * HEAD_DIM + m * D_QK
            sel = (lane >= lo) & (lane < lo + D_QK)
            scores.append(_bdot_nt(ks[p], jnp.where(sel, qs[p], 0.0)))
        es = [jnp.exp2(s - jnp.max(s, axis=0, keepdims=True)).astype(BF16) for s in scores]
        pv = [jnp.dot(vts[chains[i][0]], es[i], preferred_element_type=F32) for i in range(len(chains))]
        for j, p in enumerate(pairs):
            halves = []
            for h in range(2):
                i = (j * 2 + h) * 2
                rows = slice(h * HEAD_DIM, (h + 1) * HEAD_DIM)
                o = (pv[i][rows] * (1.0 / pv[i][LANES:LANES + 1])
                     - lam * (pv[i + 1][rows] * (1.0 / pv[i + 1][LANES:LANES + 1])))
                ms = jnp.mean(o * o, axis=0, keepdims=True)
                halves.append(o * lax.rsqrt(ms + SUBLN_EPS))
            o_ref[:, p * LANES:(p + 1) * LANES] = (jnp.concatenate(halves, axis=0).T
                                                   * sg_ref[...] * (1.0 - lam_init))

    if has_ctx:
        for p in range(n_pair):
            group([p])
    else:
        group(list(range(n_pair)))


def _attention(u_qkv, seq_len, lam_init, lp, sg, ctx=None):
    n_tok = u_qkv.shape[0]
    n_seq = n_tok // seq_len
    tq = 256
    nq = seq_len // tq
    in_specs = [pl.BlockSpec((tq, D_DIFF), lambda b, i: (b * nq + i, 0)),
                pl.BlockSpec((seq_len, D_DIFF), lambda b, i: (b, 1)),
                pl.BlockSpec((seq_len, D_DIFF), lambda b, i: (b, 2)),
                pl.BlockSpec(lp.shape, lambda b, i: (0, 0)),
                pl.BlockSpec((1, LANES), lambda b, i: (0, 0))]
    args = [u_qkv, u_qkv, u_qkv, lp, sg]
    if ctx is not None:
        kc, vc, cos, sin = ctx
        past = kc.shape[1]
        in_specs += [pl.BlockSpec((1, past, D_DIFF), lambda b, i: (b, 0, 0)),
                     pl.BlockSpec((1, past, D_DIFF), lambda b, i: (b, 0, 0)),
                     pl.BlockSpec((tq, LANES), lambda b, i: (i, 0)),
                     pl.BlockSpec((tq, LANES), lambda b, i: (i, 0)),
                     pl.BlockSpec((seq_len, LANES), lambda b, i: (0, 0)),
                     pl.BlockSpec((seq_len, LANES), lambda b, i: (0, 0))]
        args += [kc, vc, cos, sin, cos, sin]
    return pl.pallas_call(
        functools.partial(_attn_body, ctx is not None, lam_init),
        grid=(n_seq, nq),
        in_specs=in_specs,
        out_specs=pl.BlockSpec((tq, D_DIFF), lambda b, i: (b * nq + i, 0)),
        out_shape=jax.ShapeDtypeStruct((n_tok, D_DIFF), F32),
        compiler_params=_cp(2),
    )(*args)


def _rope_tables(seq_len):
    t = jnp.arange(seq_len)
    pos = jnp.stack([(t // GRID_W).astype(F32), (t % GRID_W).astype(F32)], axis=1)
    inv = 1.0 / (ROPE_BASE ** (jnp.arange(ROPE_PAIRS, dtype=F32) / ROPE_PAIRS))
    ang = pos[:, :, None] * inv
    d = np.arange(LANES) % D_QK
    axis = d // (2 * ROPE_PAIRS)
    second = (d % (2 * ROPE_PAIRS)) // ROPE_PAIRS
    idx = d % ROPE_PAIRS
    cos = jnp.cos(ang)[:, axis, idx]
    sin = jnp.sin(ang)[:, axis, idx] * jnp.asarray(np.where(second == 1, 1.0, -1.0), F32)
    return cos, sin


def _fnet_body(x_ref, cth_ref, ctl_ref, sth_ref, stl_ref, cch_ref, ccl_ref, sch_ref, scl_ref, o_ref):
    d = functools.partial(jnp.dot, preferred_element_type=F32)
    x_hi, x_lo = _split2(x_ref[...])
    xc = d(x_hi, cch_ref[...]) + d(x_lo, cch_ref[...]) + d(x_hi, ccl_ref[...])
    xs = d(x_hi, sch_ref[...]) + d(x_lo, sch_ref[...]) + d(x_hi, scl_ref[...])
    c_hi, c_lo = _split2(xc)
    s_hi, s_lo = _split2(xs)
    yc = d(cth_ref[...], c_hi) + d(cth_ref[...], c_lo) + d(ctl_ref[...], c_hi)
    ys = d(sth_ref[...], s_hi) + d(sth_ref[...], s_lo) + d(stl_ref[...], s_hi)
    o_ref[...] = yc - ys


def _dft_consts(n, block=1):
    idx = np.arange(n)
    ang = 2.0 * np.pi * ((idx[:, None] * idx[None, :]) % n) / n
    out = []
    for m in (np.cos(ang) / np.sqrt(n), np.sin(ang) / np.sqrt(n)):
        m = np.kron(np.eye(block), m).astype(np.float32)
        hi = jnp.asarray(m, F32).astype(BF16)
        lo = (jnp.asarray(m, F32) - hi.astype(F32)).astype(BF16)
        out += [hi, lo]
    return out


def _fnet(u_f, seq_len):
    n_tok = u_f.shape[0]
    consts = _dft_consts(seq_len) + _dft_consts(FNET_GROUP_DIM, FNET_GROUPS)
    const = lambda b: (0, 0)
    return pl.pallas_call(
        _fnet_body,
        grid=(n_tok // seq_len,),
        in_specs=[pl.BlockSpec((seq_len, D_FNET), lambda b: (b, 0))]
        + [pl.BlockSpec(c.shape, const) for c in consts],
        out_specs=pl.BlockSpec((seq_len, D_FNET), lambda b: (b, 0)),
        out_shape=jax.ShapeDtypeStruct((n_tok, D_FNET), F32),
        compiler_params=_cp(1),
    )(u_f, *consts)


def _ffn_body(final, yr_ref, yd_ref, yf_ref, x_ref, g1_ref, sh2_ref, sc2_ref, g2_ref, n2_ref, fg_ref,
              wo_ref, wi_ref, wf_ref, o_ref):
    y = (_bdot(yr_ref[...], wo_ref[0:D_RWKV, :])
         + _bdot(yd_ref[...], wo_ref[D_RWKV:D_RWKV + D_DIFF, :])
         + _bdot(yf_ref[...], wo_ref[D_RWKV + D_DIFF:, :]))
    x = x_ref[...] + g1_ref[0] * y
    h = _rms(x, n2_ref[...]) * (1.0 + sc2_ref[0]) + sh2_ref[0]
    z = jnp.dot(h.astype(BF16), wi_ref[...], preferred_element_type=F32)
    gate = z[:, :D_FF]
    act = gate * jax.nn.sigmoid(gate) * z[:, D_FF:]
    x = x + g2_ref[0] * _bdot(act, wf_ref[...])
    o_ref[...] = _rms(x, fg_ref[...]) if final else x


def _ffn(y_r, y_d, y_f, x, mod, layer, row_fn, n2, fg, wo, wi, wf, final, tm):
    n_tok = x.shape[0]
    row = lambda i: (i, 0)
    const = lambda i: (0, 0)
    return pl.pallas_call(
        functools.partial(_ffn_body, final),
        grid=(n_tok // tm,),
        in_specs=[pl.BlockSpec((tm, D_RWKV), row),
                  pl.BlockSpec((tm, D_DIFF), row),
                  pl.BlockSpec((tm, D_FNET), row),
                  pl.BlockSpec((tm, D_MODEL), row),
                  _mod_spec(layer, 2, row_fn),
                  _mod_spec(layer, 3, row_fn),
                  _mod_spec(layer, 4, row_fn),
                  _mod_spec(layer, 5, row_fn),
                  pl.BlockSpec((1, D_MODEL), const),
                  pl.BlockSpec((1, D_MODEL), const),
                  pl.BlockSpec((None,) + wo.shape[1:], lambda i: (layer, 0, 0)),
                  pl.BlockSpec((None,) + wi.shape[1:], lambda i: (layer, 0, 0)),
                  pl.BlockSpec((None,) + wf.shape[1:], lambda i: (layer, 0, 0))],
        out_specs=pl.BlockSpec((tm, D_MODEL), row),
        out_shape=jax.ShapeDtypeStruct((n_tok, D_MODEL), F32),
        compiler_params=_cp(1),
    )(y_r, y_d, y_f, x, mod, mod, mod, mod, n2.reshape(1, D_MODEL), fg.reshape(1, D_MODEL), wo, wi, wf)


def _block_diag_state(s):
    b = s.shape[0]
    s = s.reshape(b, 2, N_PAIR, 2, HEAD_DIM, HEAD_DIM)
    eye = jnp.eye(2, dtype=s.dtype)
    s = s[:, :, :, :, :, None, :] * eye[None, None, None, :, None, :, None]
    return s.reshape(b, 2, N_PAIR, LANES, LANES)


def kernel(x_prompt, x_sample, c, state_rwkv, cache_diff_k, cache_diff_v, c_ctx, norm1_g, norm2_g, final_norm_g, w_mod, b_mod, w_in, w_out, shift_mu, decay_w0, decay_up, iclr_a0, iclr_up, gate_up, k_k, k_a, r_k, lnx_g, lnx_b, diff_lambda, subln_g, w_ffn_in, w_ffn_out):
    p = dict(shift_mu=shift_mu, decay_w0=decay_w0, decay_up=decay_up, iclr_a0=iclr_a0, iclr_up=iclr_up,
             gate_up=gate_up, k_k=k_k, k_a=k_a, r_k=r_k, lnx_g=lnx_g, lnx_b=lnx_b)
    n_ctx, t_ctx, _ = x_prompt.shape
    n_dec, t_dec, _ = x_sample.shape
    past = cache_diff_k.shape[2]

    cond = jnp.concatenate([c_ctx[None, :], c, jnp.zeros((MOD_ROWS - 1 - n_dec, D_MODEL), F32)], axis=0)
    mod = _modulation(cond, w_mod, b_mod).reshape(DEPTH * MOD_ROWS, 1, 6 * D_MODEL)

    tm_in, tm_ffn = 512, 256
    streams = [
        dict(x=x_prompt.reshape(n_ctx * t_ctx, D_MODEL), t=t_ctx, n=n_ctx, n_sub=4,
             row_in=lambda i: 0, row_ffn=lambda i: 0),
        dict(x=x_sample.reshape(n_dec * t_dec, D_MODEL), t=t_dec, n=n_dec, n_sub=1,
             row_in=lambda i: 1 + i // (t_dec // tm_in), row_ffn=lambda i: 1 + i // (t_dec // tm_ffn)),
    ]
    cos, sin = _rope_tables(t_dec)
    states, ks, vs = [], [], []
    w_in_l = jnp.concatenate(
        [w_in[:, :, :N_RWKV_IN].astype(BF16), jnp.zeros((DEPTH, D_MODEL, N_RWKV_PAD - N_RWKV_IN), BF16),
         w_in[:, :, N_RWKV_IN:].astype(BF16)], axis=2)
    wo = w_out.astype(BF16)
    wi = w_ffn_in.astype(BF16)
    wf = w_ffn_out.astype(BF16)
    for l in range(DEPTH):
        rw = _rwkv_weights(p, l)
        lam_init = 0.8 - 0.6 * math.exp(-0.3 * l)
        sg = jnp.tile(subln_g[l], 2).reshape(1, LANES)
        for si, st in enumerate(streams):
            u_r, u_qkv, u_f = _inproj(st['x'], mod, l, st['row_in'], norm1_g[l], w_in_l, tm_in)
            if si == 0:
                s0 = None
                attn_ctx = None
            else:
                s0 = state_rwkv[:, l].astype(F32)
                attn_ctx = (cache_diff_k[:, l].reshape(n_dec, past, D_DIFF).astype(F32),
                            cache_diff_v[:, l].reshape(n_dec, past, D_DIFF).astype(F32), cos, sin)
            y_r, s_fin = _rwkv(u_r, s0, st['t'], st['n_sub'], rw)
            y_d = _attention(u_qkv, st['t'], lam_init, diff_lambda[l], sg, attn_ctx)
            y_f = _fnet(u_f, st['t'])
            st['x'] = _ffn(y_r, y_d, y_f, st['x'], mod, l, st['row_ffn'], norm2_g[l], final_norm_g,
                           wo, wi, wf, l == DEPTH - 1, tm_ffn)
            if si == 0:
                states.append(s_fin)
                ks.append(u_qkv[:, D_DIFF:2 * D_DIFF].reshape(n_ctx, t_ctx, H_DIFF, 2, D_QK))
                vs.append(u_qkv[:, 2 * D_DIFF:].reshape(n_ctx, t_ctx, H_DIFF, HEAD_DIM))
    y_prompt = streams[0]['x'].reshape(n_ctx, t_ctx, D_MODEL)
    y_sample = streams[1]['x'].reshape(n_dec, t_dec, D_MODEL)
    return (y_prompt, y_sample, jnp.stack(states, axis=1), jnp.stack(ks, axis=1), jnp.stack(vs, axis=1))
```

```python
import functools
import math

import numpy as np
import jax
import jax.numpy as jnp
from jax import lax
from jax.experimental import pallas as pl
from jax.experimental.pallas import tpu as pltpu

F32 = jnp.float32
BF16 = jnp.bfloat16

D_MODEL = 1024
DEPTH = 2
GRID_W = 64
HEAD_DIM = 64
D_RWKV = 384
H_RWKV = D_RWKV // HEAD_DIM
D_DIFF = 384
H_DIFF = D_DIFF // HEAD_DIM
D_QK = HEAD_DIM // 2
D_FNET = D_MODEL - D_RWKV - D_DIFF
FNET_GROUPS = 4
FNET_GROUP_DIM = D_FNET // FNET_GROUPS
LORA_W = 32
LORA_A = 32
LORA_G = 64
N_RWKV_IN = 3 * D_RWKV + 2 * LORA_W + 2 * LORA_A + LORA_G
N_DIFF_IN = 3 * D_DIFF
D_FF = ((8 * D_MODEL + 3 * 256 - 1) // (3 * 256)) * 256
ROPE_PAIRS = D_QK // 4
ROPE_BASE = 10000.0
RMS_EPS = 1e-6
GN_EPS = 64e-5
SUBLN_EPS = 1e-5
DECAY_SCALE = math.exp(-0.5)

LANES = 128
N_RWKV_PAD = 11 * LANES
D_IN_PAD = N_RWKV_PAD + N_DIFF_IN + D_FNET
N_PAIR = H_RWKV // 2
CHUNK = 64
MOD_ROWS = 8
ONES_ROWS = 16
VMEM_LIMIT = 56 * 1024 * 1024


def _cp(n_axes=1):
    return pltpu.CompilerParams(dimension_semantics=("arbitrary",) * n_axes,
                                vmem_limit_bytes=VMEM_LIMIT)


def _bdot(a, b):
    return jnp.dot(a.astype(BF16), b.astype(BF16), preferred_element_type=F32)


def _bdot_nt(a, b):
    return lax.dot_general(a.astype(BF16), b.astype(BF16), (((1,), (1,)), ((), ())),
                           preferred_element_type=F32)


def _split2(x):
    hi = x.astype(BF16)
    lo = (x - hi.astype(F32)).astype(BF16)
    return hi, lo


def _dot_x3(a, b):
    a_hi, a_lo = _split2(a)
    b_hi, b_lo = _split2(b)
    d = functools.partial(jnp.dot, preferred_element_type=F32)
    return d(a_hi, b_hi) + d(a_lo, b_hi) + d(a_hi, b_lo)


def _dot_exact_rhs(c_bf16, x):
    d = functools.partial(jnp.dot, preferred_element_type=F32)
    hi, lo = _split2(x)
    return d(c_bf16, hi) + d(c_bf16, lo)


def _rms(x, g):
    return x * lax.rsqrt(jnp.mean(x * x, axis=-1, keepdims=True) + RMS_EPS) * g


def _mod_body(c_ref, w_ref, b_ref, o_ref):
    c = c_ref[...]
    a = c * jax.nn.sigmoid(c)
    o_ref[0] = _dot_x3(a, w_ref[0]) + b_ref[0]


def _modulation(cond, w_mod, b_mod):
    n_layers, _, n_out = w_mod.shape
    tn = 1536
    return pl.pallas_call(
        _mod_body,
        grid=(n_layers, n_out // tn),
        in_specs=[pl.BlockSpec((MOD_ROWS, D_MODEL), lambda l, j: (0, 0)),
                  pl.BlockSpec((1, D_MODEL, tn), lambda l, j: (l, 0, j)),
                  pl.BlockSpec((1, 1, tn), lambda l, j: (l, 0, j))],
        out_specs=pl.BlockSpec((1, MOD_ROWS, tn), lambda l, j: (l, 0, j)),
        out_shape=jax.ShapeDtypeStruct((n_layers, MOD_ROWS, n_out), F32),
        compiler_params=_cp(2),
    )(cond, w_mod, b_mod.reshape(n_layers, 1, n_out))


def _mod_spec(layer, col, row_fn):
    return pl.BlockSpec((1, 1, D_MODEL), lambda i: (layer * MOD_ROWS + row_fn(i), 0, col))


def _inproj_body(seq_len, x_ref, g_ref, sh_ref, sc_ref, w_ref, mu_ref, k_any, v_any,
                 ur_ref, q_ref, k_ref, v_ref, uf_ref):
    del k_any, v_any
    h = _rms(x_ref[...], g_ref[...]) * (1.0 + sc_ref[0]) + sh_ref[0]
    u = jnp.dot(h.astype(BF16), w_ref[...], preferred_element_type=F32)
    ur = u[:, :N_RWKV_PAD]
    tm = ur.shape[0]
    pos = lax.broadcasted_iota(jnp.int32, (tm, 1), 0) & (seq_len - 1)
    prev = jnp.where(pos == 0, 0.0, pltpu.roll(ur, 1, axis=0))
    nxt = jnp.where(pos == seq_len - 1, 0.0, pltpu.roll(ur, tm - 1, axis=0))
    ur_ref[...] = ur + mu_ref[0:1, :] * (prev - ur) + mu_ref[1:2, :] * (nxt - ur)
    q_ref[...] = u[:, N_RWKV_PAD:N_RWKV_PAD + D_DIFF]
    for s in range(tm // seq_len):
        rows = slice(s * seq_len, (s + 1) * seq_len)
        k_ref[s] = u[rows, N_RWKV_PAD + D_DIFF:N_RWKV_PAD + 2 * D_DIFF]
        v_ref[s] = u[rows, N_RWKV_PAD + 2 * D_DIFF:N_RWKV_PAD + 3 * D_DIFF]
    uf_ref[...] = u[:, N_RWKV_PAD + N_DIFF_IN:]


def _inproj(x, mod, layer, row_fn, g, w, mu, k_all, v_all, seq_len, tm):
    n_tok = x.shape[0]
    assert tm % seq_len == 0 and seq_len & (seq_len - 1) == 0
    row = lambda i: (i, 0)
    const = lambda i: (0, 0)
    kv_spec = pl.BlockSpec((tm // seq_len, None, seq_len, D_DIFF), lambda i: (i, layer, 0, 0))
    return pl.pallas_call(
        functools.partial(_inproj_body, seq_len),
        grid=(n_tok // tm,),
        in_specs=[pl.BlockSpec((tm, D_MODEL), row),
                  pl.BlockSpec((1, D_MODEL), const),
                  _mod_spec(layer, 0, row_fn),
                  _mod_spec(layer, 1, row_fn),
                  pl.BlockSpec((None, D_MODEL, D_IN_PAD), lambda i: (layer, 0, 0)),
                  pl.BlockSpec((2, N_RWKV_PAD), const),
                  pl.BlockSpec(memory_space=pl.ANY),
                  pl.BlockSpec(memory_space=pl.ANY)],
        out_specs=[pl.BlockSpec((tm, N_RWKV_PAD), row),
                   pl.BlockSpec((tm, D_DIFF), row),
                   kv_spec, kv_spec,
                   pl.BlockSpec((tm, D_FNET), row)],
        out_shape=[jax.ShapeDtypeStruct((n_tok, N_RWKV_PAD), F32),
                   jax.ShapeDtypeStruct((n_tok, D_DIFF), F32),
                   jax.ShapeDtypeStruct(k_all.shape, F32),
                   jax.ShapeDtypeStruct(v_all.shape, F32),
                   jax.ShapeDtypeStruct((n_tok, D_FNET), F32)],
        input_output_aliases={6: 2, 7: 3},
        compiler_params=_cp(1),
    )(x, g.reshape(1, D_MODEL), mod, mod, w, mu, k_all, v_all)


def _lane_masks():
    lane = lax.broadcasted_iota(jnp.int32, (1, LANES), 1)
    return lane < HEAD_DIM, lane >= HEAD_DIM


def _rwkv_units(units, m0, m1):
    def bd(x):
        xb = x.astype(BF16)
        zero = jnp.zeros_like(xb)
        return jnp.concatenate([jnp.where(m0, xb, zero), jnp.where(m1, xb, zero)], axis=0)

    row = lax.broadcasted_iota(jnp.int32, (CHUNK, LANES), 0)
    col = lax.broadcasted_iota(jnp.int32, (CHUNK, LANES), 1) & (CHUNK - 1)
    eye = (col == row).astype(F32)
    r2 = lax.broadcasted_iota(jnp.int32, (LANES, LANES), 0) < HEAD_DIM
    c2 = lax.broadcasted_iota(jnp.int32, (LANES, LANES), 1) < HEAD_DIM
    rng = range(len(units))

    pre = []
    for rev, kk, r, v, kd, b, cum, lw, tot, s_prev in units:
        p_inv = jnp.exp(-cum)
        p_rem = jnp.exp(tot - cum)
        ab = -kk * jnp.exp(cum - lw)
        rb = r * jnp.exp(cum)
        strict = (col > row) if rev else (col < row)
        incl = (col >= row) if rev else (col <= row)
        pre.append(dict(ab=ab, rb=rb, vbd=bd(v), strict=strict, incl=incl,
                        lhs=jnp.concatenate([ab, rb], axis=0),
                        rhs=jnp.concatenate([bd(b * p_inv), bd(kd * p_inv)], axis=0),
                        bk=jnp.concatenate([b * p_rem, kd * p_rem], axis=0)))

    mm = [_bdot_nt(q['lhs'], q['rhs']) for q in pre]
    m_ab = [jnp.where(pre[i]['strict'], mm[i][:CHUNK, :LANES], 0.0) for i in rng]
    m_ak = [jnp.where(pre[i]['strict'], mm[i][:CHUNK, LANES:], 0.0) for i in rng]
    m_r = [jnp.concatenate([jnp.where(pre[i]['incl'], mm[i][CHUNK:, :LANES], 0.0),
                            jnp.where(pre[i]['incl'], mm[i][CHUNK:, LANES:], 0.0)], axis=1) for i in rng]
    mv = [_bdot(m_ak[i], pre[i]['vbd']) for i in rng]

    t = [eye + m_ab[i] for i in rng]
    n = [_bdot(m_ab[i], bd(m_ab[i])) for i in rng]
    for _ in range(4):
        x = [_bdot(jnp.concatenate([t[i], n[i]], axis=0), bd(n[i])) for i in rng]
        t = [t[i] + x[i][:CHUNK] for i in rng]
        n = [x[i][CHUNK:] for i in rng]
    t = [t[i] + _bdot(t[i], bd(n[i])) for i in rng]

    w = [_bdot(t[i], jnp.concatenate([bd(pre[i]['ab']), bd(mv[i])], axis=1)) for i in rng]
    xs = [_bdot_nt(jnp.concatenate([w[i][:, :LANES], pre[i]['rb']], axis=0), units[i][9]) for i in rng]
    u = [xs[i][:CHUNK] + w[i][:, LANES:] for i in rng]
    y = [xs[i][CHUNK:] + _bdot(m_r[i], jnp.concatenate([bd(u[i]), pre[i]['vbd']], axis=0)) for i in rng]
    z = [_bdot(jnp.concatenate([u[i], units[i][3]], axis=0).T, pre[i]['bk']) for i in rng]
    s_new = [units[i][9] * jnp.exp(units[i][8]) + jnp.where(r2 == c2, z[i], 0.0) for i in rng]
    return y, s_new


def _rwkv_body(seq_len, n_sub, has_s0, u_ref, *refs):
    s0_ref = refs[0] if has_s0 else None
    (wdec_ref, wicl_ref, wg_ref, w0a0_ref, vec_ref, bo_ref, tril_ref, triu_ref, y_ref, sfin_ref,
     r_s, v_s, kk_s, g_s, bon_s, kd_s, b_s, lw_s, y_s, st_s) = refs[1:] if has_s0 else refs
    n_chunk = seq_len // CHUNK
    blk = 256
    k_k = vec_ref[0:1, :]
    k_a = vec_ref[1:2, :]
    r_k = vec_ref[2:3, :]
    lnx_g = vec_ref[3:4, :]
    lnx_b = vec_ref[4:5, :]
    bo = bo_ref[...]

    def headsum(x, terms=1):
        out = []
        for p in range(N_PAIR):
            xp = x[:, p * LANES:(p + 1) * LANES]
            hi = xp.astype(BF16)
            acc = jnp.dot(hi, bo, preferred_element_type=F32)
            if terms == 2:
                lo = (xp - hi.astype(F32)).astype(BF16)
                acc = acc + jnp.dot(lo, bo, preferred_element_type=F32)
            out.append(acc)
        return jnp.concatenate(out, axis=1)

    for j in range(n_sub * seq_len // blk):
        r0 = j * blk
        xs = u_ref[r0:r0 + blk, :]
        r = xs[:, 0:D_RWKV]
        k = xs[:, D_RWKV:2 * D_RWKV]
        v = xs[:, 2 * D_RWKV:3 * D_RWKV]
        lora = xs[:, 3 * D_RWKV:3 * D_RWKV + LANES]
        gd = xs[:, 3 * D_RWKV + LANES:3 * D_RWKV + 2 * LANES]
        dec = _bdot(jnp.tanh(lora), wdec_ref[...])
        icl = _bdot(lora, wicl_ref[...])
        g = _bdot(jax.nn.sigmoid(gd), wg_ref[...])
        logw = -DECAY_SCALE * jax.nn.sigmoid(w0a0_ref[0:1, :] + dec)
        a = jax.nn.sigmoid(w0a0_ref[1:2, :] + icl)
        kk = k * k_k
        kk = kk / jnp.maximum(jnp.sqrt(headsum(kk * kk)), 1e-12)
        a_f = a[:, :D_RWKV]
        a_b = a[:, D_RWKV:]
        kd_f = k * (1.0 + (a_f - 1.0) * k_a)
        kd_b = k * (1.0 + (a_b - 1.0) * k_a)
        rows = slice(r0, r0 + blk)
        r_s[rows, :] = r
        v_s[rows, :] = v
        kk_s[rows, :] = kk
        g_s[rows, :] = g
        bon_s[rows, :] = headsum(r * (kd_f + kd_b) * r_k, terms=2) * v
        kd_s[0, rows, :] = kd_f
        kd_s[1, rows, :] = kd_b
        b_s[0, rows, :] = kk * a_f
        b_s[1, rows, :] = kk * a_b
        lw_s[0, rows, :] = logw[:, :D_RWKV]
        lw_s[1, rows, :] = logw[:, D_RWKV:]

    st_s[...] = s0_ref[...] if has_s0 else jnp.zeros_like(st_s)
    y_s[...] = jnp.zeros_like(y_s)
    m0, m1 = _lane_masks()

    def chunk_step(i, carry):
        units, rows_sd = [], []
        for s in range(n_sub):
            for d in range(2):
                c = i if d == 0 else n_chunk - 1 - i
                rows = pl.ds(pl.multiple_of(s * seq_len + c * CHUNK, CHUNK), CHUNK)
                rows_sd.append(rows)
                lw = lw_s[d, rows, :]
                tri = tril_ref[...] if d == 0 else triu_ref[...]
                cum = _dot_exact_rhs(tri, lw)
                tot = jnp.sum(lw, axis=0, keepdims=True)
                kk = kk_s[rows, :]
                r = r_s[rows, :]
                v = v_s[rows, :]
                kd = kd_s[d, rows, :]
                b = b_s[d, rows, :]
                for p in range(N_PAIR):
                    sl = slice(p * LANES, (p + 1) * LANES)
                    units.append((d == 1, kk[:, sl], r[:, sl], v[:, sl], kd[:, sl], b[:, sl],
                                  cum[:, sl], lw[:, sl], tot[:, sl], st_s[s, d, p]))
        ys, s_new = _rwkv_units(units, m0, m1)
        for s in range(n_sub):
            for d in range(2):
                base = (s * 2 + d) * N_PAIR
                for p in range(N_PAIR):
                    st_s[s, d, p] = s_new[base + p]
                rows = rows_sd[s * 2 + d]
                y_s[rows, :] = y_s[rows, :] + jnp.concatenate(ys[base:base + N_PAIR], axis=1)
        return carry

    lax.fori_loop(0, n_chunk, chunk_step, 0)
    for s in range(n_sub):
        for d in range(2):
            for p in range(N_PAIR):
                st = st_s[s, d, p]
                sfin_ref[s, d, 2 * p] = st[:HEAD_DIM, :HEAD_DIM]
                sfin_ref[s, d, 2 * p + 1] = st[HEAD_DIM:, HEAD_DIM:]

    for j in range(n_sub * seq_len // blk):
        rows = slice(j * blk, (j + 1) * blk)
        y = y_s[rows, :]
        mean = headsum(y) * (1.0 / HEAD_DIM)
        yc = y - mean
        var = headsum(yc * yc) * (1.0 / HEAD_DIM)
        yn = yc * lax.rsqrt(var + GN_EPS) * lnx_g + lnx_b
        y_ref[rows, :] = (yn + bon_s[rows, :]) * g_s[rows, :]


def _rwkv(u_r, s0, seq_len, n_sub, wts):
    n_seq = u_r.shape[0] // seq_len
    rows = n_sub * seq_len
    const2 = lambda b: (0, 0)
    st_shape = (n_sub, 2, N_PAIR, LANES, LANES)
    tok = pltpu.VMEM((rows, D_RWKV), F32)
    tok2 = pltpu.VMEM((2, rows, D_RWKV), F32)
    in_specs = [pl.BlockSpec((rows, N_RWKV_PAD), lambda b: (b, 0))]
    args = [u_r]
    if s0 is not None:
        in_specs.append(pl.BlockSpec(st_shape, lambda b: (b, 0, 0, 0, 0)))
        args.append(_block_diag_state(s0))
    in_specs += [pl.BlockSpec(w.shape, const2) for w in wts]
    return pl.pallas_call(
        functools.partial(_rwkv_body, seq_len, n_sub, s0 is not None),
        grid=(n_seq // n_sub,),
        in_specs=in_specs,
        out_specs=[pl.BlockSpec((rows, D_RWKV), lambda b: (b, 0)),
                   pl.BlockSpec((n_sub, 2, H_RWKV, HEAD_DIM, HEAD_DIM), lambda b: (b, 0, 0, 0, 0))],
        out_shape=[jax.ShapeDtypeStruct((n_seq * seq_len, D_RWKV), F32),
                   jax.ShapeDtypeStruct((n_seq, 2, H_RWKV, HEAD_DIM, HEAD_DIM), F32)],
        scratch_shapes=[tok] * 5 + [tok2] * 3 + [tok, pltpu.VMEM(st_shape, F32)],
        compiler_params=_cp(1),
    )(*args, *wts)


def _rwkv_weights(p, l):
    z = functools.partial(jnp.zeros, dtype=F32)
    wdec = z((LANES, 2 * D_RWKV))
    wdec = wdec.at[0:LORA_W, :D_RWKV].set(p['decay_up'][l, 0])
    wdec = wdec.at[LORA_W:2 * LORA_W, D_RWKV:].set(p['decay_up'][l, 1])
    wicl = z((LANES, 2 * D_RWKV))
    wicl = wicl.at[2 * LORA_W:2 * LORA_W + LORA_A, :D_RWKV].set(p['iclr_up'][l, 0])
    wicl = wicl.at[2 * LORA_W + LORA_A:2 * LORA_W + 2 * LORA_A, D_RWKV:].set(p['iclr_up'][l, 1])
    wg = z((LANES, D_RWKV)).at[0:LORA_G].set(p['gate_up'][l])
    w0a0 = jnp.stack([p['decay_w0'][l].reshape(-1), p['iclr_a0'][l].reshape(-1)])
    vec = jnp.stack([p['k_k'][l], p['k_a'][l], p['r_k'][l].reshape(-1), p['lnx_g'][l], p['lnx_b'][l],
                     z((D_RWKV,)), z((D_RWKV,)), z((D_RWKV,))])
    head = np.arange(LANES) // HEAD_DIM
    bo = jnp.asarray(head[:, None] == head[None, :], BF16)
    idx = np.arange(CHUNK)
    tril = jnp.asarray(idx[None, :] <= idx[:, None], BF16)
    triu = jnp.asarray(idx[None, :] >= idx[:, None], BF16)
    return [wdec.astype(BF16), wicl.astype(BF16), wg.astype(BF16), w0a0, vec, bo, tril, triu]


def _rope(x, cos, sin):
    lane = lax.broadcasted_iota(jnp.int32, (1, LANES), 1)
    first_half = (lane & ROPE_PAIRS) == 0
    partner = jnp.where(first_half, pltpu.roll(x, LANES - ROPE_PAIRS, axis=1),
                        pltpu.roll(x, ROPE_PAIRS, axis=1))
    return x * cos + partner * sin


def _attn_body(has_ctx, lam_init, *refs):
    if has_ctx:
        (q_ref, k_ref, v_ref, lp_ref, sg_ref, kc_ref, vc_ref, cq_ref, sq_ref, ck_ref, sk_ref,
         o_ref) = refs
    else:
        q_ref, k_ref, v_ref, lp_ref, sg_ref, o_ref = refs
    n_pair = D_DIFF // LANES
    lp = lp_ref[...]
    lam = (jnp.exp(jnp.sum(lp[0:1] * lp[1:2], axis=-1, keepdims=True))
           - jnp.exp(jnp.sum(lp[2:3] * lp[3:4], axis=-1, keepdims=True)) + lam_init)
    lane = lax.broadcasted_iota(jnp.int32, (1, LANES), 1)

    def group(pairs):
        qs, ks, vts = {}, {}, {}
        for p in pairs:
            sl = slice(p * LANES, (p + 1) * LANES)
            q = q_ref[:, sl]
            k = k_ref[:, sl]
            v = v_ref[:, sl]
            if has_ctx:
                q = _rope(q, cq_ref[...], sq_ref[...])
                k = _rope(k, ck_ref[...], sk_ref[...])
                k = jnp.concatenate([kc_ref[0, :, sl], k], axis=0)
                v = jnp.concatenate([vc_ref[0, :, sl], v], axis=0)
            qs[p] = q * (D_QK ** -0.5 * math.log2(math.e))
            ks[p] = k.astype(BF16)
            vts[p] = jnp.concatenate([v.T, jnp.ones((ONES_ROWS, v.shape[0]), F32)], axis=0).astype(BF16)
        chains = [(p, h, m) for p in pairs for h in range(2) for m in range(2)]
        scores = []
        for p, h, m in chains:
            lo = h * HEAD_DIM + m * D_QK
            sel = (lane >= lo) & (lane < lo + D_QK)
            scores.append(_bdot_nt(ks[p], jnp.where(sel, qs[p], 0.0)))
        es = [jnp.exp2(s - jnp.max(s, axis=0, keepdims=True)).astype(BF16) for s in scores]
        pv = [jnp.dot(vts[chains[i][0]], es[i], preferred_element_type=F32) for i in range(len(chains))]
        for j, p in enumerate(pairs):
            halves = []
            for h in range(2):
                i = (j * 2 + h) * 2
                rows = slice(h * HEAD_DIM, (h + 1) * HEAD_DIM)
                o = (pv[i][rows] * (1.0 / pv[i][LANES:LANES + 1])
                     - lam * (pv[i + 1][rows] * (1.0 / pv[i + 1][LANES:LANES + 1])))
                ms = jnp.mean(o * o, axis=0, keepdims=True)
                halves.append(o * lax.rsqrt(ms + SUBLN_EPS))
            o_ref[:, p * LANES:(p + 1) * LANES] = (jnp.concatenate(halves, axis=0).T
                                                   * sg_ref[...] * (1.0 - lam_init))

    if has_ctx:
        for p in range(n_pair):
            group([p])
    else:
        group(list(range(n_pair)))


def _attention(q, k_all, v_all, layer, lam_init, lp, sg, ctx=None):
    n_tok = q.shape[0]
    n_seq, _, seq_len, _ = k_all.shape
    tq = 256
    nq = seq_len // tq
    kv_spec = pl.BlockSpec((None, None, seq_len, D_DIFF), lambda b, i: (b, layer, 0, 0))
    in_specs = [pl.BlockSpec((tq, D_DIFF), lambda b, i: (b * nq + i, 0)),
                kv_spec, kv_spec,
                pl.BlockSpec(lp.shape, lambda b, i: (0, 0)),
                pl.BlockSpec((1, LANES), lambda b, i: (0, 0))]
    args = [q, k_all, v_all, lp, sg]
    if ctx is not None:
        kc, vc, cos, sin = ctx
        past = kc.shape[1]
        in_specs += [pl.BlockSpec((1, past, D_DIFF), lambda b, i: (b, 0, 0)),
                     pl.BlockSpec((1, past, D_DIFF), lambda b, i: (b, 0, 0)),
                     pl.BlockSpec((tq, LANES), lambda b, i: (i, 0)),
                     pl.BlockSpec((tq, LANES), lambda b, i: (i, 0)),
                     pl.BlockSpec((seq_len, LANES), lambda b, i: (0, 0)),
                     pl.BlockSpec((seq_len, LANES), lambda b, i: (0, 0))]
        args += [kc, vc, cos, sin, cos, sin]
    return pl.pallas_call(
        functools.partial(_attn_body, ctx is not None, lam_init),
        grid=(n_seq, nq),
        in_specs=in_specs,
        out_specs=pl.BlockSpec((tq, D_DIFF), lambda b, i: (b * nq + i, 0)),
        out_shape=jax.ShapeDtypeStruct((n_tok, D_DIFF), F32),
        compiler_params=_cp(2),
    )(*args)


def _rope_tables(seq_len):
    t = jnp.arange(seq_len)
    pos = jnp.stack([(t // GRID_W).astype(F32), (t % GRID_W).astype(F32)], axis=1)
    inv = 1.0 / (ROPE_BASE ** (jnp.arange(ROPE_PAIRS, dtype=F32) / ROPE_PAIRS))
    ang = pos[:, :, None] * inv
    d = np.arange(LANES) % D_QK
    axis = d // (2 * ROPE_PAIRS)
    second = (d % (2 * ROPE_PAIRS)) // ROPE_PAIRS
    idx = d % ROPE_PAIRS
    cos = jnp.cos(ang)[:, axis, idx]
    sin = jnp.sin(ang)[:, axis, idx] * jnp.asarray(np.where(second == 1, 1.0, -1.0), F32)
    return cos, sin


def _fnet_body(x_ref, ct_ref, st_ref, cc_ref, sc_ref, o_ref):
    x = x_ref[...].astype(BF16)
    xc = jnp.dot(x, cc_ref[...], preferred_element_type=F32)
    xs = jnp.dot(x, sc_ref[...], preferred_element_type=F32)
    o_ref[...] = _bdot(ct_ref[...], xc) - _bdot(st_ref[...], xs)


def _dft_consts(n, block=1):
    idx = np.arange(n)
    ang = 2.0 * np.pi * ((idx[:, None] * idx[None, :]) % n) / n
    return [jnp.asarray(np.kron(np.eye(block), m).astype(np.float32)).astype(BF16)
            for m in (np.cos(ang) / np.sqrt(n), np.sin(ang) / np.sqrt(n))]


def _fnet(u_f, seq_len):
    n_tok = u_f.shape[0]
    consts = _dft_consts(seq_len) + _dft_consts(FNET_GROUP_DIM, FNET_GROUPS)
    const = lambda b: (0, 0)
    return pl.pallas_call(
        _fnet_body,
        grid=(n_tok // seq_len,),
        in_specs=[pl.BlockSpec((seq_len, D_FNET), lambda b: (b, 0))]
        + [pl.BlockSpec(c.shape, const) for c in consts],
        out_specs=pl.BlockSpec((seq_len, D_FNET), lambda b: (b, 0)),
        out_shape=jax.ShapeDtypeStruct((n_tok, D_FNET), F32),
        compiler_params=_cp(1),
    )(u_f, *consts)


def _ffn_body(final, yr_ref, yd_ref, yf_ref, x_ref, g1_ref, sh2_ref, sc2_ref, g2_ref, n2_ref, fg_ref,
              wo_ref, wi_ref, wf_ref, o_ref):
    y = (_bdot(yr_ref[...], wo_ref[0:D_RWKV, :])
         + _bdot(yd_ref[...], wo_ref[D_RWKV:D_RWKV + D_DIFF, :])
         + _bdot(yf_ref[...], wo_ref[D_RWKV + D_DIFF:, :]))
    x = x_ref[...] + g1_ref[0] * y
    h = _rms(x, n2_ref[...]) * (1.0 + sc2_ref[0]) + sh2_ref[0]
    z = jnp.dot(h.astype(BF16), wi_ref[...], preferred_element_type=F32)
    gate = z[:, :D_FF]
    act = gate * jax.nn.sigmoid(gate) * z[:, D_FF:]
    x = x + g2_ref[0] * _bdot(act, wf_ref[...])
    o_ref[...] = _rms(x, fg_ref[...]) if final else x


def _ffn(y_r, y_d, y_f, x, mod, layer, row_fn, n2, fg, wo, wi, wf, final, tm):
    n_tok = x.shape[0]
    row = lambda i: (i, 0)
    const = lambda i: (0, 0)
    return pl.pallas_call(
        functools.partial(_ffn_body, final),
        grid=(n_tok // tm,),
        in_specs=[pl.BlockSpec((tm, D_RWKV), row),
                  pl.BlockSpec((tm, D_DIFF), row),
                  pl.BlockSpec((tm, D_FNET), row),
                  pl.BlockSpec((tm, D_MODEL), row),
                  _mod_spec(layer, 2, row_fn),
                  _mod_spec(layer, 3, row_fn),
                  _mod_spec(layer, 4, row_fn),
                  _mod_spec(layer, 5, row_fn),
                  pl.BlockSpec((1, D_MODEL), const),
                  pl.BlockSpec((1, D_MODEL), const),
                  pl.BlockSpec((None,) + wo.shape[1:], lambda i: (layer, 0, 0)),
                  pl.BlockSpec((None,) + wi.shape[1:], lambda i: (layer, 0, 0)),
                  pl.BlockSpec((None,) + wf.shape[1:], lambda i: (layer, 0, 0))],
        out_specs=pl.BlockSpec((tm, D_MODEL), row),
        out_shape=jax.ShapeDtypeStruct((n_tok, D_MODEL), F32),
        compiler_params=_cp(1),
    )(y_r, y_d, y_f, x, mod, mod, mod, mod, n2.reshape(1, D_MODEL), fg.reshape(1, D_MODEL), wo, wi, wf)


def _block_diag_state(s):
    b = s.shape[0]
    s = s.reshape(b, 2, N_PAIR, 2, HEAD_DIM, HEAD_DIM)
    eye = jnp.eye(2, dtype=s.dtype)
    s = s[:, :, :, :, :, None, :] * eye[None, None, None, :, None, :, None]
    return s.reshape(b, 2, N_PAIR, LANES, LANES)


def kernel(x_prompt, x_sample, c, state_rwkv, cache_diff_k, cache_diff_v, c_ctx, norm1_g, norm2_g, final_norm_g, w_mod, b_mod, w_in, w_out, shift_mu, decay_w0, decay_up, iclr_a0, iclr_up, gate_up, k_k, k_a, r_k, lnx_g, lnx_b, diff_lambda, subln_g, w_ffn_in, w_ffn_out):
    p = dict(shift_mu=shift_mu, decay_w0=decay_w0, decay_up=decay_up, iclr_a0=iclr_a0, iclr_up=iclr_up,
             gate_up=gate_up, k_k=k_k, k_a=k_a, r_k=r_k, lnx_g=lnx_g, lnx_b=lnx_b)
    n_ctx, t_ctx, _ = x_prompt.shape
    n_dec, t_dec, _ = x_sample.shape
    past = cache_diff_k.shape[2]

    cond = jnp.concatenate([c_ctx[None, :], c, jnp.zeros((MOD_ROWS - 1 - n_dec, D_MODEL), F32)], axis=0)
    mod = _modulation(cond, w_mod, b_mod).reshape(DEPTH * MOD_ROWS, 1, 6 * D_MODEL)

    tm_ffn = 256
    streams = [
        dict(x=x_prompt.reshape(n_ctx * t_ctx, D_MODEL), t=t_ctx, n=n_ctx, n_sub=4, tm_in=2 * t_ctx,
             row_in=lambda i: 0, row_ffn=lambda i: 0),
        dict(x=x_sample.reshape(n_dec * t_dec, D_MODEL), t=t_dec, n=n_dec, n_sub=1, tm_in=t_dec,
             row_in=lambda i: 1 + i, row_ffn=lambda i: 1 + i // (t_dec // tm_ffn)),
    ]
    cos, sin = _rope_tables(t_dec)
    states = []
    for st in streams:
        st['k'] = jnp.zeros((st['n'], DEPTH, st['t'], D_DIFF), F32)
        st['v'] = jnp.zeros((st['n'], DEPTH, st['t'], D_DIFF), F32)
    w_in_l = jnp.concatenate(
        [w_in[:, :, :N_RWKV_IN].astype(BF16), jnp.zeros((DEPTH, D_MODEL, N_RWKV_PAD - N_RWKV_IN), BF16),
         w_in[:, :, N_RWKV_IN:].astype(BF16)], axis=2)
    wo = w_out.astype(BF16)
    wi = w_ffn_in.astype(BF16)
    wf = w_ffn_out.astype(BF16)
    for l in range(DEPTH):
        rw = _rwkv_weights(p, l)
        mu = jnp.concatenate([shift_mu[l], jnp.zeros((2, N_RWKV_PAD - N_RWKV_IN), F32)], axis=1)
        lam_init = 0.8 - 0.6 * math.exp(-0.3 * l)
        sg = jnp.tile(subln_g[l], 2).reshape(1, LANES)
        for si, st in enumerate(streams):
            u_r, q, st['k'], st['v'], u_f = _inproj(st['x'], mod, l, st['row_in'], norm1_g[l], w_in_l, mu,
                                                     st['k'], st['v'], st['t'], st['tm_in'])
            if si == 0:
                s0 = None
                attn_ctx = None
            else:
                s0 = state_rwkv[:, l].astype(F32)
                attn_ctx = (cache_diff_k[:, l].reshape(n_dec, past, D_DIFF).astype(F32),
                            cache_diff_v[:, l].reshape(n_dec, past, D_DIFF).astype(F32), cos, sin)
            y_r, s_fin = _rwkv(u_r, s0, st['t'], st['n_sub'], rw)
            y_d = _attention(q, st['k'], st['v'], l, lam_init, diff_lambda[l], sg, attn_ctx)
            y_f = _fnet(u_f, st['t'])
            st['x'] = _ffn(y_r, y_d, y_f, st['x'], mod, l, st['row_ffn'], norm2_g[l], final_norm_g,
                           wo, wi, wf, l == DEPTH - 1, tm_ffn)
            if si == 0:
                states.append(s_fin)
    y_prompt = streams[0]['x'].reshape(n_ctx, t_ctx, D_MODEL)
    y_sample = streams[1]['x'].reshape(n_dec, t_dec, D_MODEL)
    new_k = streams[0]['k'].reshape(n_ctx, DEPTH, t_ctx, H_DIFF, 2, D_QK)
    new_v = streams[0]['v'].reshape(n_ctx, DEPTH, t_ctx, H_DIFF, HEAD_DIM)
    return (y_prompt, y_sample, jnp.stack(states, axis=1), new_k, new_v)
```

```python
import functools
import math

import numpy as np
import jax
import jax.numpy as jnp
from jax import lax
from jax.experimental import pallas as pl
from jax.experimental.pallas import tpu as pltpu

F32 = jnp.float32
BF16 = jnp.bfloat16

D_MODEL = 1024
DEPTH = 2
GRID_W = 64
HEAD_DIM = 64
D_RWKV = 384
H_RWKV = D_RWKV // HEAD_DIM
D_DIFF = 384
H_DIFF = D_DIFF // HEAD_DIM
D_QK = HEAD_DIM // 2
D_FNET = D_MODEL - D_RWKV - D_DIFF
FNET_GROUPS = 4
FNET_GROUP_DIM = D_FNET // FNET_GROUPS
LORA_W = 32
LORA_A = 32
LORA_G = 64
N_RWKV_IN = 3 * D_RWKV + 2 * LORA_W + 2 * LORA_A + LORA_G
N_DIFF_IN = 3 * D_DIFF
D_FF = ((8 * D_MODEL + 3 * 256 - 1) // (3 * 256)) * 256
ROPE_PAIRS = D_QK // 4
ROPE_BASE = 10000.0
RMS_EPS = 1e-6
GN_EPS = 64e-5
SUBLN_EPS = 1e-5
DECAY_SCALE = math.exp(-0.5)

LANES = 128
N_RWKV_PAD = 11 * LANES
D_IN_PAD = N_RWKV_PAD + N_DIFF_IN + D_FNET
N_PAIR = H_RWKV // 2
CHUNK = 64
MOD_ROWS = 8
ONES_ROWS = 16
VMEM_LIMIT = 56 * 1024 * 1024


def _cp(n_axes=1):
    return pltpu.CompilerParams(dimension_semantics=("arbitrary",) * n_axes,
                                vmem_limit_bytes=VMEM_LIMIT)


def _bdot(a, b):
    return jnp.dot(a.astype(BF16), b.astype(BF16), preferred_element_type=F32)


def _bdot_nt(a, b):
    return lax.dot_general(a.astype(BF16), b.astype(BF16), (((1,), (1,)), ((), ())),
                           preferred_element_type=F32)


def _split2(x):
    hi = x.astype(BF16)
    lo = (x - hi.astype(F32)).astype(BF16)
    return hi, lo


def _dot_x3(a, b):
    a_hi, a_lo = _split2(a)
    b_hi, b_lo = _split2(b)
    d = functools.partial(jnp.dot, preferred_element_type=F32)
    return d(a_hi, b_hi) + d(a_lo, b_hi) + d(a_hi, b_lo)


def _dot_exact_rhs(c_bf16, x):
    d = functools.partial(jnp.dot, preferred_element_type=F32)
    hi, lo = _split2(x)
    return d(c_bf16, hi) + d(c_bf16, lo)


def _rms(x, g):
    return x * lax.rsqrt(jnp.mean(x * x, axis=-1, keepdims=True) + RMS_EPS) * g


def _mod_body(c_ref, w_ref, b_ref, o_ref):
    c = c_ref[...]
    a = c * jax.nn.sigmoid(c)
    o_ref[0] = _dot_x3(a, w_ref[0]) + b_ref[0]


def _modulation(cond, w_mod, b_mod):
    n_layers, _, n_out = w_mod.shape
    tn = 1536
    return pl.pallas_call(
        _mod_body,
        grid=(n_layers, n_out // tn),
        in_specs=[pl.BlockSpec((MOD_ROWS, D_MODEL), lambda l, j: (0, 0)),
                  pl.BlockSpec((1, D_MODEL, tn), lambda l, j: (l, 0, j)),
                  pl.BlockSpec((1, 1, tn), lambda l, j: (l, 0, j))],
        out_specs=pl.BlockSpec((1, MOD_ROWS, tn), lambda l, j: (l, 0, j)),
        out_shape=jax.ShapeDtypeStruct((n_layers, MOD_ROWS, n_out), F32),
        compiler_params=_cp(2),
    )(cond, w_mod, b_mod.reshape(n_layers, 1, n_out))


def _mod_spec(layer, col, row_fn):
    return pl.BlockSpec((1, 1, D_MODEL), lambda i: (layer * MOD_ROWS + row_fn(i), 0, col))


def _put_layer(ref, idx, layer, aliased, val):
    if aliased:
        ref[idx] = val
    else:
        for other in range(DEPTH):
            ref[idx + (other,)] = val if other == layer else jnp.zeros_like(val)


def _inproj_body(seq_len, layer, aliased, x_ref, g_ref, sh_ref, sc_ref, w_ref, mu_ref, *refs):
    ur_ref, q_ref, k_ref, v_ref, uf_ref = refs[2:] if aliased else refs
    h = _rms(x_ref[...], g_ref[...]) * (1.0 + sc_ref[0]) + sh_ref[0]
    u = jnp.dot(h.astype(BF16), w_ref[...], preferred_element_type=F32)
    ur = u[:, :N_RWKV_PAD]
    tm = ur.shape[0]
    pos = lax.broadcasted_iota(jnp.int32, (tm, 1), 0) & (seq_len - 1)
    prev = jnp.where(pos == 0, 0.0, pltpu.roll(ur, 1, axis=0))
    nxt = jnp.where(pos == seq_len - 1, 0.0, pltpu.roll(ur, tm - 1, axis=0))
    ur_ref[...] = ur + mu_ref[0:1, :] * (prev - ur) + mu_ref[1:2, :] * (nxt - ur)
    q_ref[...] = u[:, N_RWKV_PAD:N_RWKV_PAD + D_DIFF]
    for s in range(tm // seq_len):
        rows = slice(s * seq_len, (s + 1) * seq_len)
        _put_layer(k_ref, (s,), layer, aliased, u[rows, N_RWKV_PAD + D_DIFF:N_RWKV_PAD + 2 * D_DIFF])
        _put_layer(v_ref, (s,), layer, aliased, u[rows, N_RWKV_PAD + 2 * D_DIFF:N_RWKV_PAD + 3 * D_DIFF])
    uf_ref[...] = u[:, N_RWKV_PAD + N_DIFF_IN:]


def _inproj(x, mod, layer, row_fn, g, w, mu, kv, seq_len, tm):
    n_tok = x.shape[0]
    kv_shape = jax.ShapeDtypeStruct((n_tok // seq_len, DEPTH, seq_len, D_DIFF), F32)
    assert tm % seq_len == 0 and seq_len & (seq_len - 1) == 0
    row = lambda i: (i, 0)
    const = lambda i: (0, 0)
    if kv is None:
        kv_spec = pl.BlockSpec((tm // seq_len, DEPTH, seq_len, D_DIFF), lambda i: (i, 0, 0, 0))
    else:
        kv_spec = pl.BlockSpec((tm // seq_len, None, seq_len, D_DIFF), lambda i: (i, layer, 0, 0))
    return pl.pallas_call(
        functools.partial(_inproj_body, seq_len, layer, kv is not None),
        grid=(n_tok // tm,),
        in_specs=[pl.BlockSpec((tm, D_MODEL), row),
                  pl.BlockSpec((1, D_MODEL), const),
                  _mod_spec(layer, 0, row_fn),
                  _mod_spec(layer, 1, row_fn),
                  pl.BlockSpec((None, D_MODEL, D_IN_PAD), lambda i: (layer, 0, 0)),
                  pl.BlockSpec((2, N_RWKV_PAD), const)]
        + [pl.BlockSpec(memory_space=pl.ANY)] * (0 if kv is None else 2),
        out_specs=[pl.BlockSpec((tm, N_RWKV_PAD), row),
                   pl.BlockSpec((tm, D_DIFF), row),
                   kv_spec, kv_spec,
                   pl.BlockSpec((tm, D_FNET), row)],
        out_shape=[jax.ShapeDtypeStruct((n_tok, N_RWKV_PAD), F32),
                   jax.ShapeDtypeStruct((n_tok, D_DIFF), F32),
                   kv_shape, kv_shape,
                   jax.ShapeDtypeStruct((n_tok, D_FNET), F32)],
        input_output_aliases={} if kv is None else {6: 2, 7: 3},
        compiler_params=_cp(1),
    )(x, g.reshape(1, D_MODEL), mod, mod, w, mu, *(kv or ()))


def _lane_masks():
    lane = lax.broadcasted_iota(jnp.int32, (1, LANES), 1)
    return lane < HEAD_DIM, lane >= HEAD_DIM


def _rwkv_units(units, m0, m1):
    def bd(x):
        xb = x.astype(BF16)
        zero = jnp.zeros_like(xb)
        return jnp.concatenate([jnp.where(m0, xb, zero), jnp.where(m1, xb, zero)], axis=0)

    row = lax.broadcasted_iota(jnp.int32, (CHUNK, LANES), 0)
    col = lax.broadcasted_iota(jnp.int32, (CHUNK, LANES), 1) & (CHUNK - 1)
    eye = (col == row).astype(F32)
    r2 = lax.broadcasted_iota(jnp.int32, (LANES, LANES), 0) < HEAD_DIM
    c2 = lax.broadcasted_iota(jnp.int32, (LANES, LANES), 1) < HEAD_DIM
    rng = range(len(units))

    pre = []
    for rev, kk, r, v, kd, b, cum, lw, tot, s_prev in units:
        p_inv = jnp.exp(-cum)
        p_rem = jnp.exp(tot - cum)
        ab = -kk * jnp.exp(cum - lw)
        rb = r * jnp.exp(cum)
        strict = (col > row) if rev else (col < row)
        incl = (col >= row) if rev else (col <= row)
        pre.append(dict(ab=ab, rb=rb, vbd=bd(v), strict=strict, incl=incl,
                        lhs=jnp.concatenate([ab, rb], axis=0),
                        rhs=jnp.concatenate([bd(b * p_inv), bd(kd * p_inv)], axis=0),
                        bk=jnp.concatenate([b * p_rem, kd * p_rem], axis=0)))

    mm = [_bdot_nt(q['lhs'], q['rhs']) for q in pre]
    m_ab = [jnp.where(pre[i]['strict'], mm[i][:CHUNK, :LANES], 0.0) for i in rng]
    m_ak = [jnp.where(pre[i]['strict'], mm[i][:CHUNK, LANES:], 0.0) for i in rng]
    m_r = [jnp.concatenate([jnp.where(pre[i]['incl'], mm[i][CHUNK:, :LANES], 0.0),
                            jnp.where(pre[i]['incl'], mm[i][CHUNK:, LANES:], 0.0)], axis=1) for i in rng]
    mv = [_bdot(m_ak[i], pre[i]['vbd']) for i in rng]

    t = [eye + m_ab[i] for i in rng]
    n = [_bdot(m_ab[i], bd(m_ab[i])) for i in rng]
    for _ in range(4):
        x = [_bdot(jnp.concatenate([t[i], n[i]], axis=0), bd(n[i])) for i in rng]
        t = [t[i] + x[i][:CHUNK] for i in rng]
        n = [x[i][CHUNK:] for i in rng]
    t = [t[i] + _bdot(t[i], bd(n[i])) for i in rng]

    w = [_bdot(t[i], jnp.concatenate([bd(pre[i]['ab']), bd(mv[i])], axis=1)) for i in rng]
    xs = [_bdot_nt(jnp.concatenate([w[i][:, :LANES], pre[i]['rb']], axis=0), units[i][9]) for i in rng]
    u = [xs[i][:CHUNK] + w[i][:, LANES:] for i in rng]
    y = [xs[i][CHUNK:] + _bdot(m_r[i], jnp.concatenate([bd(u[i]), pre[i]['vbd']], axis=0)) for i in rng]
    z = [_bdot(jnp.concatenate([u[i], units[i][3]], axis=0).T, pre[i]['bk']) for i in rng]
    s_new = [units[i][9] * jnp.exp(units[i][8]) + jnp.where(r2 == c2, z[i], 0.0) for i in rng]
    return y, s_new


def _rwkv_body(seq_len, n_sub, layer, has_s0, aliased, u_ref, *refs):
    s0_ref = refs[0] if has_s0 else None
    refs = refs[1:] if has_s0 else refs
    wdec_ref, wicl_ref, wg_ref, w0a0_ref, vec_ref, bo_ref, tril_ref, triu_ref = refs[:8]
    (y_ref, sfin_ref, r_s, v_s, kk_s, g_s, bon_s, kd_s, b_s, lw_s, y_s, st_s) = refs[9 if aliased else 8:]
    n_chunk = seq_len // CHUNK
    blk = 256
    k_k = vec_ref[0:1, :]
    k_a = vec_ref[1:2, :]
    r_k = vec_ref[2:3, :]
    lnx_g = vec_ref[3:4, :]
    lnx_b = vec_ref[4:5, :]
    bo = bo_ref[...]

    def headsum(x, terms=1):
        out = []
        for p in range(N_PAIR):
            xp = x[:, p * LANES:(p + 1) * LANES]
            hi = xp.astype(BF16)
            acc = jnp.dot(hi, bo, preferred_element_type=F32)
            if terms == 2:
                lo = (xp - hi.astype(F32)).astype(BF16)
                acc = acc + jnp.dot(lo, bo, preferred_element_type=F32)
            out.append(acc)
        return jnp.concatenate(out, axis=1)

    for j in range(n_sub * seq_len // blk):
        r0 = j * blk
        xs = u_ref[r0:r0 + blk, :]
        r = xs[:, 0:D_RWKV]
        k = xs[:, D_RWKV:2 * D_RWKV]
        v = xs[:, 2 * D_RWKV:3 * D_RWKV]
        lora = xs[:, 3 * D_RWKV:3 * D_RWKV + LANES]
        gd = xs[:, 3 * D_RWKV + LANES:3 * D_RWKV + 2 * LANES]
        dec = _bdot(jnp.tanh(lora), wdec_ref[...])
        icl = _bdot(lora, wicl_ref[...])
        g = _bdot(jax.nn.sigmoid(gd), wg_ref[...])
        logw = -DECAY_SCALE * jax.nn.sigmoid(w0a0_ref[0:1, :] + dec)
        a = jax.nn.sigmoid(w0a0_ref[1:2, :] + icl)
        kk = k * k_k
        kk = kk / jnp.maximum(jnp.sqrt(headsum(kk * kk)), 1e-12)
        a_f = a[:, :D_RWKV]
        a_b = a[:, D_RWKV:]
        kd_f = k * (1.0 + (a_f - 1.0) * k_a)
        kd_b = k * (1.0 + (a_b - 1.0) * k_a)
        rows = slice(r0, r0 + blk)
        r_s[rows, :] = r
        v_s[rows, :] = v
        kk_s[rows, :] = kk
        g_s[rows, :] = g
        bon_s[rows, :] = headsum(r * (kd_f + kd_b) * r_k, terms=2) * v
        kd_s[0, rows, :] = kd_f
        kd_s[1, rows, :] = kd_b
        b_s[0, rows, :] = kk * a_f
        b_s[1, rows, :] = kk * a_b
        lw_s[0, rows, :] = logw[:, :D_RWKV]
        lw_s[1, rows, :] = logw[:, D_RWKV:]

    st_s[...] = s0_ref[...] if has_s0 else jnp.zeros_like(st_s)
    y_s[...] = jnp.zeros_like(y_s)
    m0, m1 = _lane_masks()

    def chunk_step(i, carry):
        units, rows_sd = [], []
        for s in range(n_sub):
            for d in range(2):
                c = i if d == 0 else n_chunk - 1 - i
                rows = pl.ds(pl.multiple_of(s * seq_len + c * CHUNK, CHUNK), CHUNK)
                rows_sd.append(rows)
                lw = lw_s[d, rows, :]
                tri = tril_ref[...] if d == 0 else triu_ref[...]
                cum = _dot_exact_rhs(tri, lw)
                tot = jnp.sum(lw, axis=0, keepdims=True)
                kk = kk_s[rows, :]
                r = r_s[rows, :]
                v = v_s[rows, :]
                kd = kd_s[d, rows, :]
                b = b_s[d, rows, :]
                for p in range(N_PAIR):
                    sl = slice(p * LANES, (p + 1) * LANES)
                    units.append((d == 1, kk[:, sl], r[:, sl], v[:, sl], kd[:, sl], b[:, sl],
                                  cum[:, sl], lw[:, sl], tot[:, sl], st_s[s, d, p]))
        ys, s_new = _rwkv_units(units, m0, m1)
        for s in range(n_sub):
            for d in range(2):
                base = (s * 2 + d) * N_PAIR
                for p in range(N_PAIR):
                    st_s[s, d, p] = s_new[base + p]
                rows = rows_sd[s * 2 + d]
                y_s[rows, :] = y_s[rows, :] + jnp.concatenate(ys[base:base + N_PAIR], axis=1)
        return carry

    lax.fori_loop(0, n_chunk, chunk_step, 0)
    for s in range(n_sub):
        for d in range(2):
            for p in range(N_PAIR):
                st = st_s[s, d, p]
                for h in range(2):
                    rows = slice(h * HEAD_DIM, (h + 1) * HEAD_DIM)
                    val = st[rows, rows]
                    if aliased:
                        sfin_ref[s, d, 2 * p + h] = val
                    else:
                        for other in range(DEPTH):
                            sfin_ref[s, other, d, 2 * p + h] = val if other == layer else jnp.zeros_like(val)

    for j in range(n_sub * seq_len // blk):
        rows = slice(j * blk, (j + 1) * blk)
        y = y_s[rows, :]
        mean = headsum(y) * (1.0 / HEAD_DIM)
        yc = y - mean
        var = headsum(yc * yc) * (1.0 / HEAD_DIM)
        yn = yc * lax.rsqrt(var + GN_EPS) * lnx_g + lnx_b
        y_ref[rows, :] = (yn + bon_s[rows, :]) * g_s[rows, :]


def _rwkv(u_r, s0, seq_len, n_sub, wts, states, layer):
    n_seq = u_r.shape[0] // seq_len
    rows = n_sub * seq_len
    const2 = lambda b: (0, 0)
    st_shape = (n_sub, 2, N_PAIR, LANES, LANES)
    tok = pltpu.VMEM((rows, D_RWKV), F32)
    tok2 = pltpu.VMEM((2, rows, D_RWKV), F32)
    in_specs = [pl.BlockSpec((rows, N_RWKV_PAD), lambda b: (b, 0))]
    args = [u_r]
    if s0 is not None:
        in_specs.append(pl.BlockSpec(st_shape, lambda b: (b, 0, 0, 0, 0)))
        args.append(_block_diag_state(s0))
    in_specs += [pl.BlockSpec(w.shape, const2) for w in wts]
    args += list(wts)
    if states is not None:
        in_specs.append(pl.BlockSpec(memory_space=pl.ANY))
        args.append(states)
    return pl.pallas_call(
        functools.partial(_rwkv_body, seq_len, n_sub, layer, s0 is not None, states is not None),
        grid=(n_seq // n_sub,),
        in_specs=in_specs,
        out_specs=[pl.BlockSpec((rows, D_RWKV), lambda b: (b, 0)),
                   pl.BlockSpec((n_sub, DEPTH, 2, H_RWKV, HEAD_DIM, HEAD_DIM), lambda b: (b, 0, 0, 0, 0, 0))
                   if states is None else
                   pl.BlockSpec((n_sub, None, 2, H_RWKV, HEAD_DIM, HEAD_DIM),
                                lambda b: (b, layer, 0, 0, 0, 0))],
        out_shape=[jax.ShapeDtypeStruct((n_seq * seq_len, D_RWKV), F32),
                   jax.ShapeDtypeStruct((n_seq, DEPTH, 2, H_RWKV, HEAD_DIM, HEAD_DIM), F32)],
        scratch_shapes=[tok] * 5 + [tok2] * 3 + [tok, pltpu.VMEM(st_shape, F32)],
        input_output_aliases={} if states is None else {len(args) - 1: 1},
        compiler_params=_cp(1),
    )(*args)


def _rwkv_weights(p, l):
    z = functools.partial(jnp.zeros, dtype=F32)
    wdec = z((LANES, 2 * D_RWKV))
    wdec = wdec.at[0:LORA_W, :D_RWKV].set(p['decay_up'][l, 0])
    wdec = wdec.at[LORA_W:2 * LORA_W, D_RWKV:].set(p['decay_up'][l, 1])
    wicl = z((LANES, 2 * D_RWKV))
    wicl = wicl.at[2 * LORA_W:2 * LORA_W + LORA_A, :D_RWKV].set(p['iclr_up'][l, 0])
    wicl = wicl.at[2 * LORA_W + LORA_A:2 * LORA_W + 2 * LORA_A, D_RWKV:].set(p['iclr_up'][l, 1])
    wg = z((LANES, D_RWKV)).at[0:LORA_G].set(p['gate_up'][l])
    w0a0 = jnp.stack([p['decay_w0'][l].reshape(-1), p['iclr_a0'][l].reshape(-1)])
    vec = jnp.stack([p['k_k'][l], p['k_a'][l], p['r_k'][l].reshape(-1), p['lnx_g'][l], p['lnx_b'][l],
                     z((D_RWKV,)), z((D_RWKV,)), z((D_RWKV,))])
    head = np.arange(LANES) // HEAD_DIM
    bo = jnp.asarray(head[:, None] == head[None, :], BF16)
    idx = np.arange(CHUNK)
    tril = jnp.asarray(idx[None, :] <= idx[:, None], BF16)
    triu = jnp.asarray(idx[None, :] >= idx[:, None], BF16)
    return [wdec.astype(BF16), wicl.astype(BF16), wg.astype(BF16), w0a0, vec, bo, tril, triu]


def _rope(x, cos, sin):
    lane = lax.broadcasted_iota(jnp.int32, (1, LANES), 1)
    first_half = (lane & ROPE_PAIRS) == 0
    partner = jnp.where(first_half, pltpu.roll(x, LANES - ROPE_PAIRS, axis=1),
                        pltpu.roll(x, ROPE_PAIRS, axis=1))
    return x * cos + partner * sin


def _attn_body(has_ctx, lam_init, *refs):
    if has_ctx:
        (q_ref, k_ref, v_ref, lp_ref, sg_ref, kc_ref, vc_ref, cq_ref, sq_ref, ck_ref, sk_ref,
         o_ref) = refs
    else:
        q_ref, k_ref, v_ref, lp_ref, sg_ref, o_ref = refs
    n_pair = D_DIFF // LANES
    lp = lp_ref[...]
    lam = (jnp.exp(jnp.sum(lp[0:1] * lp[1:2], axis=-1, keepdims=True))
           - jnp.exp(jnp.sum(lp[2:3] * lp[3:4], axis=-1, keepdims=True)) + lam_init)
    lane = lax.broadcasted_iota(jnp.int32, (1, LANES), 1)

    def group(pairs):
        qs, ks, vts = {}, {}, {}
        for p in pairs:
            sl = slice(p * LANES, (p + 1) * LANES)
            q = q_ref[:, sl]
            k = k_ref[:, sl]
            v = v_ref[:, sl]
            if has_ctx:
                q = _rope(q, cq_ref[...], sq_ref[...])
                k = _rope(k, ck_ref[...], sk_ref[...])
                k = jnp.concatenate([kc_ref[0, :, sl], k], axis=0)
                v = jnp.concatenate([vc_ref[0, :, sl], v], axis=0)
            qs[p] = q * (D_QK ** -0.5 * math.log2(math.e))
            ks[p] = k.astype(BF16)
            vt = v.T
            ones = jnp.ones((ONES_ROWS, v.shape[0]), F32)
            vts[p] = [jnp.concatenate([vt[h * HEAD_DIM:(h + 1) * HEAD_DIM], ones], axis=0).astype(BF16)
                      for h in range(2)]
        chains = [(p, h, m) for p in pairs for h in range(2) for m in range(2)]
        scores = []
        for p, h, m in chains:
            lo = h * HEAD_DIM + m * D_QK
            sel = (lane >= lo) & (lane < lo + D_QK)
            scores.append(_bdot_nt(ks[p], jnp.where(sel, qs[p], 0.0)))
        es = [jnp.exp2(s - jnp.max(s, axis=0, keepdims=True)).astype(BF16) for s in scores]
        pv = [jnp.dot(vts[p][h], es[i], preferred_element_type=F32) for i, (p, h, m) in enumerate(chains)]
        for j, p in enumerate(pairs):
            halves = []
            for h in range(2):
                i = (j * 2 + h) * 2
                rows = slice(0, HEAD_DIM)
                o = (pv[i][rows] * (1.0 / pv[i][HEAD_DIM:HEAD_DIM + 1])
                     - lam * (pv[i + 1][rows] * (1.0 / pv[i + 1][HEAD_DIM:HEAD_DIM + 1])))
                ms = jnp.mean(o * o, axis=0, keepdims=True)
                halves.append(o * lax.rsqrt(ms + SUBLN_EPS))
            o_ref[:, p * LANES:(p + 1) * LANES] = (jnp.concatenate(halves, axis=0).T
                                                   * sg_ref[...] * (1.0 - lam_init))

    if has_ctx:
        for p in range(n_pair):
            group([p])
    else:
        group(list(range(n_pair)))


def _attention(q, k_all, v_all, layer, lam_init, lp, sg, ctx=None):
    n_tok = q.shape[0]
    n_seq, _, seq_len, _ = k_all.shape
    tq = 256
    nq = seq_len // tq
    kv_spec = pl.BlockSpec((None, None, seq_len, D_DIFF), lambda b, i: (b, layer, 0, 0))
    in_specs = [pl.BlockSpec((tq, D_DIFF), lambda b, i: (b * nq + i, 0)),
                kv_spec, kv_spec,
                pl.BlockSpec(lp.shape, lambda b, i: (0, 0)),
                pl.BlockSpec((1, LANES), lambda b, i: (0, 0))]
    args = [q, k_all, v_all, lp, sg]
    if ctx is not None:
        kc, vc, cos, sin = ctx
        past = kc.shape[1]
        in_specs += [pl.BlockSpec((1, past, D_DIFF), lambda b, i: (b, 0, 0)),
                     pl.BlockSpec((1, past, D_DIFF), lambda b, i: (b, 0, 0)),
                     pl.BlockSpec((tq, LANES), lambda b, i: (i, 0)),
                     pl.BlockSpec((tq, LANES), lambda b, i: (i, 0)),
                     pl.BlockSpec((seq_len, LANES), lambda b, i: (0, 0)),
                     pl.BlockSpec((seq_len, LANES), lambda b, i: (0, 0))]
        args += [kc, vc, cos, sin, cos, sin]
    return pl.pallas_call(
        functools.partial(_attn_body, ctx is not None, lam_init),
        grid=(n_seq, nq),
        in_specs=in_specs,
        out_specs=pl.BlockSpec((tq, D_DIFF), lambda b, i: (b * nq + i, 0)),
        out_shape=jax.ShapeDtypeStruct((n_tok, D_DIFF), F32),
        compiler_params=_cp(2),
    )(*args)


def _rope_tables(seq_len):
    t = jnp.arange(seq_len)
    pos = jnp.stack([(t // GRID_W).astype(F32), (t % GRID_W).astype(F32)], axis=1)
    inv = 1.0 / (ROPE_BASE ** (jnp.arange(ROPE_PAIRS, dtype=F32) / ROPE_PAIRS))
    ang = pos[:, :, None] * inv
    d = np.arange(LANES) % D_QK
    axis = d // (2 * ROPE_PAIRS)
    second = (d % (2 * ROPE_PAIRS)) // ROPE_PAIRS
    idx = d % ROPE_PAIRS
    cos = jnp.cos(ang)[:, axis, idx]
    sin = jnp.sin(ang)[:, axis, idx] * jnp.asarray(np.where(second == 1, 1.0, -1.0), F32)
    return cos, sin


def _fnet_body(seq_len, x_ref, ct_ref, st_ref, cc_ref, sc_ref, o_ref):
    n_sub = x_ref.shape[0] // seq_len
    x = x_ref[...].astype(BF16)
    xc = jnp.dot(x, cc_ref[...], preferred_element_type=F32)
    xs = jnp.dot(x, sc_ref[...], preferred_element_type=F32)
    wide = lambda a: jnp.concatenate([a[s * seq_len:(s + 1) * seq_len] for s in range(n_sub)], axis=1)
    y = _bdot(ct_ref[...], wide(xc)) - _bdot(st_ref[...], wide(xs))
    for s in range(n_sub):
        o_ref[s * seq_len:(s + 1) * seq_len, :] = y[:, s * D_FNET:(s + 1) * D_FNET]


def _dft_consts(n, block=1):
    idx = np.arange(n)
    ang = 2.0 * np.pi * ((idx[:, None] * idx[None, :]) % n) / n
    return [jnp.asarray(np.kron(np.eye(block), m).astype(np.float32)).astype(BF16)
            for m in (np.cos(ang) / np.sqrt(n), np.sin(ang) / np.sqrt(n))]


def _fnet(u_f, seq_len, n_sub):
    n_tok = u_f.shape[0]
    rows = n_sub * seq_len
    consts = _dft_consts(seq_len) + _dft_consts(FNET_GROUP_DIM, FNET_GROUPS)
    const = lambda b: (0, 0)
    return pl.pallas_call(
        functools.partial(_fnet_body, seq_len),
        grid=(n_tok // rows,),
        in_specs=[pl.BlockSpec((rows, D_FNET), lambda b: (b, 0))]
        + [pl.BlockSpec(c.shape, const) for c in consts],
        out_specs=pl.BlockSpec((rows, D_FNET), lambda b: (b, 0)),
        out_shape=jax.ShapeDtypeStruct((n_tok, D_FNET), F32),
        compiler_params=_cp(1),
    )(u_f, *consts)


def _ffn_body(final, yr_ref, yd_ref, yf_ref, x_ref, g1_ref, sh2_ref, sc2_ref, g2_ref, n2_ref, fg_ref,
              wo_ref, wi_ref, wf_ref, o_ref):
    y = (_bdot(yr_ref[...], wo_ref[0:D_RWKV, :])
         + _bdot(yd_ref[...], wo_ref[D_RWKV:D_RWKV + D_DIFF, :])
         + _bdot(yf_ref[...], wo_ref[D_RWKV + D_DIFF:, :]))
    x = x_ref[...] + g1_ref[0] * y
    h = _rms(x, n2_ref[...]) * (1.0 + sc2_ref[0]) + sh2_ref[0]
    z = jnp.dot(h.astype(BF16), wi_ref[...], preferred_element_type=F32)
    gate = z[:, :D_FF]
    act = gate * jax.nn.sigmoid(gate) * z[:, D_FF:]
    x = x + g2_ref[0] * _bdot(act, wf_ref[...])
    o_ref[...] = _rms(x, fg_ref[...]) if final else x


def _ffn(y_r, y_d, y_f, x, mod, layer, row_fn, n2, fg, wo, wi, wf, final, tm):
    n_tok = x.shape[0]
    row = lambda i: (i, 0)
    const = lambda i: (0, 0)
    return pl.pallas_call(
        functools.partial(_ffn_body, final),
        grid=(n_tok // tm,),
        in_specs=[pl.BlockSpec((tm, D_RWKV), row),
                  pl.BlockSpec((tm, D_DIFF), row),
                  pl.BlockSpec((tm, D_FNET), row),
                  pl.BlockSpec((tm, D_MODEL), row),
                  _mod_spec(layer, 2, row_fn),
                  _mod_spec(layer, 3, row_fn),
                  _mod_spec(layer, 4, row_fn),
                  _mod_spec(layer, 5, row_fn),
                  pl.BlockSpec((1, D_MODEL), const),
                  pl.BlockSpec((1, D_MODEL), const),
                  pl.BlockSpec((None,) + wo.shape[1:], lambda i: (layer, 0, 0), pipeline_mode=pl.Buffered(1)),
                  pl.BlockSpec((None,) + wi.shape[1:], lambda i: (layer, 0, 0), pipeline_mode=pl.Buffered(1)),
                  pl.BlockSpec((None,) + wf.shape[1:], lambda i: (layer, 0, 0), pipeline_mode=pl.Buffered(1))],
        out_specs=pl.BlockSpec((tm, D_MODEL), row),
        out_shape=jax.ShapeDtypeStruct((n_tok, D_MODEL), F32),
        compiler_params=_cp(1),
    )(y_r, y_d, y_f, x, mod, mod, mod, mod, n2.reshape(1, D_MODEL), fg.reshape(1, D_MODEL), wo, wi, wf)


def _block_diag_state(s):
    b = s.shape[0]
    s = s.reshape(b, 2, N_PAIR, 2, HEAD_DIM, HEAD_DIM)
    eye = jnp.eye(2, dtype=s.dtype)
    s = s[:, :, :, :, :, None, :] * eye[None, None, None, :, None, :, None]
    return s.reshape(b, 2, N_PAIR, LANES, LANES)


def kernel(x_prompt, x_sample, c, state_rwkv, cache_diff_k, cache_diff_v, c_ctx, norm1_g, norm2_g, final_norm_g, w_mod, b_mod, w_in, w_out, shift_mu, decay_w0, decay_up, iclr_a0, iclr_up, gate_up, k_k, k_a, r_k, lnx_g, lnx_b, diff_lambda, subln_g, w_ffn_in, w_ffn_out):
    p = dict(shift_mu=shift_mu, decay_w0=decay_w0, decay_up=decay_up, iclr_a0=iclr_a0, iclr_up=iclr_up,
             gate_up=gate_up, k_k=k_k, k_a=k_a, r_k=r_k, lnx_g=lnx_g, lnx_b=lnx_b)
    n_ctx, t_ctx, _ = x_prompt.shape
    n_dec, t_dec, _ = x_sample.shape
    past = cache_diff_k.shape[2]

    cond = jnp.concatenate([c_ctx[None, :], c, jnp.zeros((MOD_ROWS - 1 - n_dec, D_MODEL), F32)], axis=0)
    mod = _modulation(cond, w_mod, b_mod).reshape(DEPTH * MOD_ROWS, 1, 6 * D_MODEL)

    tm_ffn = 512
    streams = [
        dict(x=x_prompt.reshape(n_ctx * t_ctx, D_MODEL), t=t_ctx, n=n_ctx, n_sub=4, tm_in=2 * t_ctx,
             row_in=lambda i: 0, row_ffn=lambda i: 0),
        dict(x=x_sample.reshape(n_dec * t_dec, D_MODEL), t=t_dec, n=n_dec, n_sub=1, tm_in=t_dec,
             row_in=lambda i: 1 + i, row_ffn=lambda i: 1 + i // (t_dec // tm_ffn)),
    ]
    cos, sin = _rope_tables(t_dec)
    for st in streams:
        st['kv'] = st['states'] = None
    w_in_l = jnp.concatenate(
        [w_in[:, :, :N_RWKV_IN].astype(BF16), jnp.zeros((DEPTH, D_MODEL, N_RWKV_PAD - N_RWKV_IN), BF16),
         w_in[:, :, N_RWKV_IN:].astype(BF16)], axis=2)
    wo = w_out.astype(BF16)
    wi = w_ffn_in.astype(BF16)
    wf = w_ffn_out.astype(BF16)
    for l in range(DEPTH):
        rw = _rwkv_weights(p, l)
        mu = jnp.concatenate([shift_mu[l], jnp.zeros((2, N_RWKV_PAD - N_RWKV_IN), F32)], axis=1)
        lam_init = 0.8 - 0.6 * math.exp(-0.3 * l)
        sg = jnp.tile(subln_g[l], 2).reshape(1, LANES)
        for si, st in enumerate(streams):
            u_r, q, k_all, v_all, u_f = _inproj(st['x'], mod, l, st['row_in'], norm1_g[l], w_in_l, mu,
                                                 st['kv'], st['t'], st['tm_in'])
            st['kv'] = (k_all, v_all)
            if si == 0:
                s0 = None
                attn_ctx = None
            else:
                s0 = state_rwkv[:, l].astype(F32)
                attn_ctx = (cache_diff_k[:, l].reshape(n_dec, past, D_DIFF).astype(F32),
                            cache_diff_v[:, l].reshape(n_dec, past, D_DIFF).astype(F32), cos, sin)
            y_r, st['states'] = _rwkv(u_r, s0, st['t'], st['n_sub'], rw, st['states'], l)
            y_d = _attention(q, k_all, v_all, l, lam_init, diff_lambda[l], sg, attn_ctx)
            y_f = _fnet(u_f, st['t'], st['n_sub'])
            st['x'] = _ffn(y_r, y_d, y_f, st['x'], mod, l, st['row_ffn'], norm2_g[l], final_norm_g,
                           wo, wi, wf, l == DEPTH - 1, tm_ffn)
    y_prompt = streams[0]['x'].reshape(n_ctx, t_ctx, D_MODEL)
    y_sample = streams[1]['x'].reshape(n_dec, t_dec, D_MODEL)
    new_k = streams[0]['kv'][0].reshape(n_ctx, DEPTH, t_ctx, H_DIFF, 2, D_QK)
    new_v = streams[0]['kv'][1].reshape(n_ctx, DEPTH, t_ctx, H_DIFF, HEAD_DIM)
    return (y_prompt, y_sample, streams[0]['states'], new_k, new_v)
```

```python
import functools
import math

import numpy as np
import jax
import jax.numpy as jnp
from jax import lax
from jax.experimental import pallas as pl
from jax.experimental.pallas import tpu as pltpu

F32 = jnp.float32
BF16 = jnp.bfloat16

D_MODEL = 1024
DEPTH = 2
GRID_W = 64
HEAD_DIM = 64
D_RWKV = 384
H_RWKV = D_RWKV // HEAD_DIM
D_DIFF = 384
H_DIFF = D_DIFF // HEAD_DIM
D_QK = HEAD_DIM // 2
D_FNET = D_MODEL - D_RWKV - D_DIFF
FNET_GROUPS = 4
FNET_GROUP_DIM = D_FNET // FNET_GROUPS
LORA_W = 32
LORA_A = 32
LORA_G = 64
N_RWKV_IN = 3 * D_RWKV + 2 * LORA_W + 2 * LORA_A + LORA_G
N_DIFF_IN = 3 * D_DIFF
D_FF = ((8 * D_MODEL + 3 * 256 - 1) // (3 * 256)) * 256
ROPE_PAIRS = D_QK // 4
ROPE_BASE = 10000.0
RMS_EPS = 1e-6
GN_EPS = 64e-5
SUBLN_EPS = 1e-5
DECAY_SCALE = math.exp(-0.5)

LANES = 128
N_RWKV_PAD = 11 * LANES
D_IN_PAD = N_RWKV_PAD + N_DIFF_IN + D_FNET
N_PAIR = H_RWKV // 2
CHUNK = 64
N_PARTS = 2
RWKV_BLK = 256
MOD_ROWS = 8
ONES_ROWS = 16
VMEM_LIMIT = 56 * 1024 * 1024


def _cp(n_axes=1):
    return pltpu.CompilerParams(dimension_semantics=("arbitrary",) * n_axes,
                                vmem_limit_bytes=VMEM_LIMIT)


def _bdot(a, b):
    return jnp.dot(a.astype(BF16), b.astype(BF16), preferred_element_type=F32)


def _bdot_nt(a, b):
    return lax.dot_general(a.astype(BF16), b.astype(BF16), (((1,), (1,)), ((), ())),
                           preferred_element_type=F32)


def _split2(x):
    hi = x.astype(BF16)
    lo = (x - hi.astype(F32)).astype(BF16)
    return hi, lo


def _dot_x3(a, b):
    a_hi, a_lo = _split2(a)
    b_hi, b_lo = _split2(b)
    d = functools.partial(jnp.dot, preferred_element_type=F32)
    return d(a_hi, b_hi) + d(a_lo, b_hi) + d(a_hi, b_lo)


def _dot_exact_rhs(c_bf16, x):
    d = functools.partial(jnp.dot, preferred_element_type=F32)
    hi, lo = _split2(x)
    return d(c_bf16, hi) + d(c_bf16, lo)


def _rms(x, g):
    return x * lax.rsqrt(jnp.mean(x * x, axis=-1, keepdims=True) + RMS_EPS) * g


def _mod_body(c_ref, w_ref, b_ref, o_ref):
    c = c_ref[...]
    a = c * jax.nn.sigmoid(c)
    o_ref[0] = _dot_x3(a, w_ref[0]) + b_ref[0]


def _modulation(cond, w_mod, b_mod):
    n_layers, _, n_out = w_mod.shape
    tn = 1536
    return pl.pallas_call(
        _mod_body,
        grid=(n_layers, n_out // tn),
        in_specs=[pl.BlockSpec((MOD_ROWS, D_MODEL), lambda l, j: (0, 0)),
                  pl.BlockSpec((1, D_MODEL, tn), lambda l, j: (l, 0, j)),
                  pl.BlockSpec((1, 1, tn), lambda l, j: (l, 0, j))],
        out_specs=pl.BlockSpec((1, MOD_ROWS, tn), lambda l, j: (l, 0, j)),
        out_shape=jax.ShapeDtypeStruct((n_layers, MOD_ROWS, n_out), F32),
        compiler_params=_cp(2),
    )(cond, w_mod, b_mod.reshape(n_layers, 1, n_out))


def _mod_spec(layer, col, row_fn):
    return pl.BlockSpec((1, 1, D_MODEL), lambda i: (layer * MOD_ROWS + row_fn(i), 0, col))


def _put_layer(ref, idx, layer, aliased, val):
    if aliased:
        ref[idx] = val
    else:
        for other in range(DEPTH):
            ref[idx + (other,)] = val if other == layer else jnp.zeros_like(val)


def _inproj_body(seq_len, layer, aliased, x_ref, g_ref, sh_ref, sc_ref, w_ref, mu_ref, *refs):
    ur_ref, q_ref, k_ref, v_ref, uf_ref = refs[2:] if aliased else refs
    part = x_ref.shape[0] // N_PARTS
    parts = [slice(j * part, (j + 1) * part) for j in range(N_PARTS)]
    h = [(_rms(x_ref[r, :], g_ref[...]) * (1.0 + sc_ref[0]) + sh_ref[0]).astype(BF16) for r in parts]
    u = jnp.concatenate([jnp.dot(hj, w_ref[...], preferred_element_type=F32) for hj in h], axis=0)
    ur = u[:, :N_RWKV_PAD]
    tm = ur.shape[0]
    pos = lax.broadcasted_iota(jnp.int32, (tm, 1), 0) & (seq_len - 1)
    prev = jnp.where(pos == 0, 0.0, pltpu.roll(ur, 1, axis=0))
    nxt = jnp.where(pos == seq_len - 1, 0.0, pltpu.roll(ur, tm - 1, axis=0))
    ur_ref[...] = ur + mu_ref[0:1, :] * (prev - ur) + mu_ref[1:2, :] * (nxt - ur)
    q_ref[...] = u[:, N_RWKV_PAD:N_RWKV_PAD + D_DIFF]
    for s in range(tm // seq_len):
        rows = slice(s * seq_len, (s + 1) * seq_len)
        _put_layer(k_ref, (s,), layer, aliased, u[rows, N_RWKV_PAD + D_DIFF:N_RWKV_PAD + 2 * D_DIFF])
        _put_layer(v_ref, (s,), layer, aliased, u[rows, N_RWKV_PAD + 2 * D_DIFF:N_RWKV_PAD + 3 * D_DIFF])
    uf_ref[...] = u[:, N_RWKV_PAD + N_DIFF_IN:]


def _inproj(x, mod, layer, row_fn, g, w, mu, kv, seq_len, tm):
    n_tok = x.shape[0]
    kv_shape = jax.ShapeDtypeStruct((n_tok // seq_len, DEPTH, seq_len, D_DIFF), F32)
    assert tm % seq_len == 0 and seq_len & (seq_len - 1) == 0
    row = lambda i: (i, 0)
    const = lambda i: (0, 0)
    if kv is None:
        kv_spec = pl.BlockSpec((tm // seq_len, DEPTH, seq_len, D_DIFF), lambda i: (i, 0, 0, 0))
    else:
        kv_spec = pl.BlockSpec((tm // seq_len, None, seq_len, D_DIFF), lambda i: (i, layer, 0, 0))
    return pl.pallas_call(
        functools.partial(_inproj_body, seq_len, layer, kv is not None),
        grid=(n_tok // tm,),
        in_specs=[pl.BlockSpec((tm, D_MODEL), row),
                  pl.BlockSpec((1, D_MODEL), const),
                  _mod_spec(layer, 0, row_fn),
                  _mod_spec(layer, 1, row_fn),
                  pl.BlockSpec((None, D_MODEL, D_IN_PAD), lambda i: (layer, 0, 0)),
                  pl.BlockSpec((2, N_RWKV_PAD), const)]
        + [pl.BlockSpec(memory_space=pl.ANY)] * (0 if kv is None else 2),
        out_specs=[pl.BlockSpec((tm, N_RWKV_PAD), row),
                   pl.BlockSpec((tm, D_DIFF), row),
                   kv_spec, kv_spec,
                   pl.BlockSpec((tm, D_FNET), row)],
        out_shape=[jax.ShapeDtypeStruct((n_tok, N_RWKV_PAD), F32),
                   jax.ShapeDtypeStruct((n_tok, D_DIFF), F32),
                   kv_shape, kv_shape,
                   jax.ShapeDtypeStruct((n_tok, D_FNET), F32)],
        input_output_aliases={} if kv is None else {6: 2, 7: 3},
        compiler_params=_cp(1),
    )(x, g.reshape(1, D_MODEL), mod, mod, w, mu, *(kv or ()))


def _lane_masks():
    lane = lax.broadcasted_iota(jnp.int32, (1, LANES), 1)
    return lane < HEAD_DIM, lane >= HEAD_DIM


def _rwkv_units(units, m0, m1):
    def bd(x):
        xb = x.astype(BF16)
        zero = jnp.zeros_like(xb)
        return jnp.concatenate([jnp.where(m0, xb, zero), jnp.where(m1, xb, zero)], axis=0)

    row = lax.broadcasted_iota(jnp.int32, (CHUNK, LANES), 0)
    col = lax.broadcasted_iota(jnp.int32, (CHUNK, LANES), 1) & (CHUNK - 1)
    eye = (col == row).astype(F32)
    r2 = lax.broadcasted_iota(jnp.int32, (LANES, LANES), 0) < HEAD_DIM
    c2 = lax.broadcasted_iota(jnp.int32, (LANES, LANES), 1) < HEAD_DIM
    rng = range(len(units))

    pre = []
    for rev, kk, r, v, kd, b, cum, cex, tot, s_prev in units:
        p_inv = jnp.exp(-cum)
        p_rem = jnp.exp(tot - cum)
        ab = -kk * jnp.exp(cex)
        rb = r * jnp.exp(cum)
        strict = (col > row) if rev else (col < row)
        incl = (col >= row) if rev else (col <= row)
        pre.append(dict(ab=ab, rb=rb, vbd=bd(v), strict=strict, incl=incl,
                        lhs=jnp.concatenate([ab, rb], axis=0),
                        rhs=jnp.concatenate([bd(b * p_inv), bd(kd * p_inv)], axis=0),
                        bk=jnp.concatenate([b * p_rem, kd * p_rem], axis=0)))

    mm = [_bdot_nt(q['lhs'], q['rhs']) for q in pre]
    m_ab = [jnp.where(pre[i]['strict'], mm[i][:CHUNK, :LANES], 0.0) for i in rng]
    m_ak = [jnp.where(pre[i]['strict'], mm[i][:CHUNK, LANES:], 0.0) for i in rng]
    m_r = [jnp.concatenate([jnp.where(pre[i]['incl'], mm[i][CHUNK:, :LANES], 0.0),
                            jnp.where(pre[i]['incl'], mm[i][CHUNK:, LANES:], 0.0)], axis=1) for i in rng]
    mv = [_bdot(m_ak[i], pre[i]['vbd']) for i in rng]

    t = [eye + m_ab[i] for i in rng]
    n = [_bdot(m_ab[i], bd(m_ab[i])) for i in rng]
    for _ in range(4):
        x = [_bdot(jnp.concatenate([t[i], n[i]], axis=0), bd(n[i])) for i in rng]
        t = [t[i] + x[i][:CHUNK] for i in rng]
        n = [x[i][CHUNK:] for i in rng]
    t = [t[i] + _bdot(t[i], bd(n[i])) for i in rng]

    w = [_bdot(t[i], jnp.concatenate([bd(pre[i]['ab']), bd(mv[i])], axis=1)) for i in rng]
    xs = [_bdot_nt(jnp.concatenate([w[i][:, :LANES], pre[i]['rb']], axis=0), units[i][9]) for i in rng]
    u = [xs[i][:CHUNK] + w[i][:, LANES:] for i in rng]
    y = [xs[i][CHUNK:] + _bdot(m_r[i], jnp.concatenate([bd(u[i]), pre[i]['vbd']], axis=0)) for i in rng]
    z = [_bdot(jnp.concatenate([u[i], units[i][3]], axis=0).T, pre[i]['bk']) for i in rng]
    s_new = [units[i][9] * jnp.exp(units[i][8]) + jnp.where(r2 == c2, z[i], 0.0) for i in rng]
    return y, s_new


def _rwkv_body(seq_len, n_sub, layer, has_s0, aliased, u_ref, *refs):
    s0_ref = refs[0] if has_s0 else None
    refs = refs[1:] if has_s0 else refs
    wdec_ref, wicl_ref, wg_ref, w0a0_ref, vec_ref, bo_ref, tril_ref, triu_ref = refs[:8]
    (y_ref, sfin_ref, r_s, v_s, kk_s, g_s, bon_s, kd_s, b_s, ci_s, ce_s, y_s, st_s) = refs[9 if aliased else 8:]
    n_chunk = seq_len // CHUNK
    blk = RWKV_BLK
    k_k = vec_ref[0:1, :]
    k_a = vec_ref[1:2, :]
    r_k = vec_ref[2:3, :]
    lnx_g = vec_ref[3:4, :]
    lnx_b = vec_ref[4:5, :]
    bo = bo_ref[...]

    def headsum(x, terms=1):
        out = []
        for p in range(N_PAIR):
            xp = x[:, p * LANES:(p + 1) * LANES]
            hi = xp.astype(BF16)
            acc = jnp.dot(hi, bo, preferred_element_type=F32)
            if terms == 2:
                lo = (xp - hi.astype(F32)).astype(BF16)
                acc = acc + jnp.dot(lo, bo, preferred_element_type=F32)
            out.append(acc)
        return jnp.concatenate(out, axis=1)

    for j in range(n_sub * seq_len // blk):
        r0 = j * blk
        xs = u_ref[r0:r0 + blk, :]
        r = xs[:, 0:D_RWKV]
        k = xs[:, D_RWKV:2 * D_RWKV]
        v = xs[:, 2 * D_RWKV:3 * D_RWKV]
        lora = xs[:, 3 * D_RWKV:3 * D_RWKV + LANES]
        gd = xs[:, 3 * D_RWKV + LANES:3 * D_RWKV + 2 * LANES]
        dec = _bdot(jnp.tanh(lora), wdec_ref[...])
        icl = _bdot(lora, wicl_ref[...])
        g = _bdot(jax.nn.sigmoid(gd), wg_ref[...])
        logw = -DECAY_SCALE * jax.nn.sigmoid(w0a0_ref[0:1, :] + dec)
        a = jax.nn.sigmoid(w0a0_ref[1:2, :] + icl)
        kk = k * k_k
        kk = kk / jnp.maximum(jnp.sqrt(headsum(kk * kk)), 1e-12)
        a_f = a[:, :D_RWKV]
        a_b = a[:, D_RWKV:]
        kd_f = k * (1.0 + (a_f - 1.0) * k_a)
        kd_b = k * (1.0 + (a_b - 1.0) * k_a)
        rows = slice(r0, r0 + blk)
        r_s[rows, :] = r
        v_s[rows, :] = v
        kk_s[rows, :] = kk
        g_s[rows, :] = g
        bon_s[rows, :] = headsum(r * (kd_f + kd_b) * r_k, terms=2) * v
        kd_s[0, rows, :] = kd_f
        kd_s[1, rows, :] = kd_b
        b_s[0, rows, :] = kk * a_f
        b_s[1, rows, :] = kk * a_b
        for d, tri_ref in enumerate((tril_ref, triu_ref)):
            lw = logw[:, d * D_RWKV:(d + 1) * D_RWKV]
            cum = _dot_exact_rhs(tri_ref[...], lw)
            ci_s[d, rows, :] = cum
            ce_s[d, rows, :] = cum - lw

    st_s[...] = s0_ref[...] if has_s0 else jnp.zeros_like(st_s)
    y_s[...] = jnp.zeros_like(y_s)
    m0, m1 = _lane_masks()

    def chunk_step(i, carry):
        units, rows_sd = [], []
        for s in range(n_sub):
            for d in range(2):
                c = i if d == 0 else n_chunk - 1 - i
                rows = pl.ds(pl.multiple_of(s * seq_len + c * CHUNK, CHUNK), CHUNK)
                rows_sd.append(rows)
                cum = ci_s[d, rows, :]
                cex = ce_s[d, rows, :]
                tot = cum[CHUNK - 1:CHUNK] if d == 0 else cum[0:1]
                kk = kk_s[rows, :]
                r = r_s[rows, :]
                v = v_s[rows, :]
                kd = kd_s[d, rows, :]
                b = b_s[d, rows, :]
                for p in range(N_PAIR):
                    sl = slice(p * LANES, (p + 1) * LANES)
                    units.append((d == 1, kk[:, sl], r[:, sl], v[:, sl], kd[:, sl], b[:, sl],
                                  cum[:, sl], cex[:, sl], tot[:, sl], st_s[s, d, p]))
        ys, s_new = _rwkv_units(units, m0, m1)
        for s in range(n_sub):
            for d in range(2):
                base = (s * 2 + d) * N_PAIR
                for p in range(N_PAIR):
                    st_s[s, d, p] = s_new[base + p]
                rows = rows_sd[s * 2 + d]
                y_s[rows, :] = y_s[rows, :] + jnp.concatenate(ys[base:base + N_PAIR], axis=1)
        return carry

    lax.fori_loop(0, n_chunk, chunk_step, 0)
    for s in range(n_sub):
        for d in range(2):
            for p in range(N_PAIR):
                st = st_s[s, d, p]
                for h in range(2):
                    rows = slice(h * HEAD_DIM, (h + 1) * HEAD_DIM)
                    val = st[rows, rows]
                    if aliased:
                        sfin_ref[s, d, 2 * p + h] = val
                    else:
                        for other in range(DEPTH):
                            sfin_ref[s, other, d, 2 * p + h] = val if other == layer else jnp.zeros_like(val)

    for j in range(n_sub * seq_len // blk):
        rows = slice(j * blk, (j + 1) * blk)
        y = y_s[rows, :]
        mean = headsum(y) * (1.0 / HEAD_DIM)
        yc = y - mean
        var = headsum(yc * yc) * (1.0 / HEAD_DIM)
        yn = yc * lax.rsqrt(var + GN_EPS) * lnx_g + lnx_b
        y_ref[rows, :] = (yn + bon_s[rows, :]) * g_s[rows, :]


def _rwkv(u_r, s0, seq_len, n_sub, wts, states, layer):
    n_seq = u_r.shape[0] // seq_len
    rows = n_sub * seq_len
    const2 = lambda b: (0, 0)
    st_shape = (n_sub, 2, N_PAIR, LANES, LANES)
    tok = pltpu.VMEM((rows, D_RWKV), F32)
    tok2 = pltpu.VMEM((2, rows, D_RWKV), F32)
    in_specs = [pl.BlockSpec((rows, N_RWKV_PAD), lambda b: (b, 0))]
    args = [u_r]
    if s0 is not None:
        in_specs.append(pl.BlockSpec(st_shape, lambda b: (b, 0, 0, 0, 0)))
        args.append(_block_diag_state(s0))
    in_specs += [pl.BlockSpec(w.shape, const2) for w in wts]
    args += list(wts)
    if states is not None:
        in_specs.append(pl.BlockSpec(memory_space=pl.ANY))
        args.append(states)
    return pl.pallas_call(
        functools.partial(_rwkv_body, seq_len, n_sub, layer, s0 is not None, states is not None),
        grid=(n_seq // n_sub,),
        in_specs=in_specs,
        out_specs=[pl.BlockSpec((rows, D_RWKV), lambda b: (b, 0)),
                   pl.BlockSpec((n_sub, DEPTH, 2, H_RWKV, HEAD_DIM, HEAD_DIM), lambda b: (b, 0, 0, 0, 0, 0))
                   if states is None else
                   pl.BlockSpec((n_sub, None, 2, H_RWKV, HEAD_DIM, HEAD_DIM),
                                lambda b: (b, layer, 0, 0, 0, 0))],
        out_shape=[jax.ShapeDtypeStruct((n_seq * seq_len, D_RWKV), F32),
                   jax.ShapeDtypeStruct((n_seq, DEPTH, 2, H_RWKV, HEAD_DIM, HEAD_DIM), F32)],
        scratch_shapes=[tok] * 5 + [tok2] * 4 + [tok, pltpu.VMEM(st_shape, F32)],
        input_output_aliases={} if states is None else {len(args) - 1: 1},
        compiler_params=_cp(1),
    )(*args)


def _rwkv_weights(p, l):
    z = functools.partial(jnp.zeros, dtype=F32)
    wdec = z((LANES, 2 * D_RWKV))
    wdec = wdec.at[0:LORA_W, :D_RWKV].set(p['decay_up'][l, 0])
    wdec = wdec.at[LORA_W:2 * LORA_W, D_RWKV:].set(p['decay_up'][l, 1])
    wicl = z((LANES, 2 * D_RWKV))
    wicl = wicl.at[2 * LORA_W:2 * LORA_W + LORA_A, :D_RWKV].set(p['iclr_up'][l, 0])
    wicl = wicl.at[2 * LORA_W + LORA_A:2 * LORA_W + 2 * LORA_A, D_RWKV:].set(p['iclr_up'][l, 1])
    wg = z((LANES, D_RWKV)).at[0:LORA_G].set(p['gate_up'][l])
    w0a0 = jnp.stack([p['decay_w0'][l].reshape(-1), p['iclr_a0'][l].reshape(-1)])
    vec = jnp.stack([p['k_k'][l], p['k_a'][l], p['r_k'][l].reshape(-1), p['lnx_g'][l], p['lnx_b'][l],
                     z((D_RWKV,)), z((D_RWKV,)), z((D_RWKV,))])
    head = np.arange(LANES) // HEAD_DIM
    bo = jnp.asarray(head[:, None] == head[None, :], BF16)
    idx = np.arange(RWKV_BLK)
    same = idx[None, :] // CHUNK == idx[:, None] // CHUNK
    tril = jnp.asarray(same & (idx[None, :] <= idx[:, None]), BF16)
    triu = jnp.asarray(same & (idx[None, :] >= idx[:, None]), BF16)
    return [wdec.astype(BF16), wicl.astype(BF16), wg.astype(BF16), w0a0, vec, bo, tril, triu]


def _rope(x, cos, sin):
    lane = lax.broadcasted_iota(jnp.int32, (1, LANES), 1)
    first_half = (lane & ROPE_PAIRS) == 0
    partner = jnp.where(first_half, pltpu.roll(x, LANES - ROPE_PAIRS, axis=1),
                        pltpu.roll(x, ROPE_PAIRS, axis=1))
    return x * cos + partner * sin


def _attn_body(has_ctx, lam_init, *refs):
    if has_ctx:
        (q_ref, k_ref, v_ref, lp_ref, sg_ref, kc_ref, vc_ref, cq_ref, sq_ref, ck_ref, sk_ref,
         o_ref) = refs
    else:
        q_ref, k_ref, v_ref, lp_ref, sg_ref, o_ref = refs
    n_pair = D_DIFF // LANES
    lp = lp_ref[...]
    lam = (jnp.exp(jnp.sum(lp[0:1] * lp[1:2], axis=-1, keepdims=True))
           - jnp.exp(jnp.sum(lp[2:3] * lp[3:4], axis=-1, keepdims=True)) + lam_init)
    lane = lax.broadcasted_iota(jnp.int32, (1, LANES), 1)

    def group(pairs):
        qs, ks, vts = {}, {}, {}
        for p in pairs:
            sl = slice(p * LANES, (p + 1) * LANES)
            q = q_ref[:, sl]
            k = k_ref[:, sl]
            v = v_ref[:, sl]
            if has_ctx:
                q = _rope(q, cq_ref[...], sq_ref[...])
                k = _rope(k, ck_ref[...], sk_ref[...])
                k = jnp.concatenate([kc_ref[0, :, sl], k], axis=0)
                v = jnp.concatenate([vc_ref[0, :, sl], v], axis=0)
            qs[p] = q * (D_QK ** -0.5 * math.log2(math.e))
            ks[p] = k.astype(BF16)
            vt = v.T
            ones = jnp.ones((ONES_ROWS, v.shape[0]), F32)
            vts[p] = [jnp.concatenate([vt[h * HEAD_DIM:(h + 1) * HEAD_DIM], ones], axis=0).astype(BF16)
                      for h in range(2)]
        chains = [(p, h, m) for p in pairs for h in range(2) for m in range(2)]
        scores = []
        for p, h, m in chains:
            lo = h * HEAD_DIM + m * D_QK
            sel = (lane >= lo) & (lane < lo + D_QK)
            scores.append(_bdot_nt(ks[p], jnp.where(sel, qs[p], 0.0)))
        es = [jnp.exp2(s - jnp.max(s, axis=0, keepdims=True)).astype(BF16) for s in scores]
        pv = [jnp.dot(vts[p][h], es[i], preferred_element_type=F32) for i, (p, h, m) in enumerate(chains)]
        for j, p in enumerate(pairs):
            halves = []
            for h in range(2):
                i = (j * 2 + h) * 2
                rows = slice(0, HEAD_DIM)
                o = (pv[i][rows] * (1.0 / pv[i][HEAD_DIM:HEAD_DIM + 1])
                     - lam * (pv[i + 1][rows] * (1.0 / pv[i + 1][HEAD_DIM:HEAD_DIM + 1])))
                ms = jnp.mean(o * o, axis=0, keepdims=True)
                halves.append(o * lax.rsqrt(ms + SUBLN_EPS))
            o_ref[:, p * LANES:(p + 1) * LANES] = (jnp.concatenate(halves, axis=0).T
                                                   * sg_ref[...] * (1.0 - lam_init))

    if has_ctx:
        for p in range(n_pair):
            group([p])
    else:
        group(list(range(n_pair)))


def _attention(q, k_all, v_all, layer, lam_init, lp, sg, ctx=None):
    n_tok = q.shape[0]
    n_seq, _, seq_len, _ = k_all.shape
    tq = 256
    nq = seq_len // tq
    kv_spec = pl.BlockSpec((None, None, seq_len, D_DIFF), lambda b, i: (b, layer, 0, 0))
    in_specs = [pl.BlockSpec((tq, D_DIFF), lambda b, i: (b * nq + i, 0)),
                kv_spec, kv_spec,
                pl.BlockSpec(lp.shape, lambda b, i: (0, 0)),
                pl.BlockSpec((1, LANES), lambda b, i: (0, 0))]
    args = [q, k_all, v_all, lp, sg]
    if ctx is not None:
        kc, vc, cos, sin = ctx
        past = kc.shape[1]
        in_specs += [pl.BlockSpec((1, past, D_DIFF), lambda b, i: (b, 0, 0)),
                     pl.BlockSpec((1, past, D_DIFF), lambda b, i: (b, 0, 0)),
                     pl.BlockSpec((tq, LANES), lambda b, i: (i, 0)),
                     pl.BlockSpec((tq, LANES), lambda b, i: (i, 0)),
                     pl.BlockSpec((seq_len, LANES), lambda b, i: (0, 0)),
                     pl.BlockSpec((seq_len, LANES), lambda b, i: (0, 0))]
        args += [kc, vc, cos, sin, cos, sin]
    return pl.pallas_call(
        functools.partial(_attn_body, ctx is not None, lam_init),
        grid=(n_seq, nq),
        in_specs=in_specs,
        out_specs=pl.BlockSpec((tq, D_DIFF), lambda b, i: (b * nq + i, 0)),
        out_shape=jax.ShapeDtypeStruct((n_tok, D_DIFF), F32),
        compiler_params=_cp(2),
    )(*args)


def _rope_tables(seq_len):
    t = jnp.arange(seq_len)
    pos = jnp.stack([(t // GRID_W).astype(F32), (t % GRID_W).astype(F32)], axis=1)
    inv = 1.0 / (ROPE_BASE ** (jnp.arange(ROPE_PAIRS, dtype=F32) / ROPE_PAIRS))
    ang = pos[:, :, None] * inv
    d = np.arange(LANES) % D_QK
    axis = d // (2 * ROPE_PAIRS)
    second = (d % (2 * ROPE_PAIRS)) // ROPE_PAIRS
    idx = d % ROPE_PAIRS
    cos = jnp.cos(ang)[:, axis, idx]
    sin = jnp.sin(ang)[:, axis, idx] * jnp.asarray(np.where(second == 1, 1.0, -1.0), F32)
    return cos, sin


def _fnet_body(seq_len, x_ref, ct_ref, st_ref, cc_ref, sc_ref, o_ref):
    n_sub = x_ref.shape[0] // seq_len
    x = x_ref[...].astype(BF16)
    xc = jnp.dot(x, cc_ref[...], preferred_element_type=F32)
    xs = jnp.dot(x, sc_ref[...], preferred_element_type=F32)
    wide = lambda a: jnp.concatenate([a[s * seq_len:(s + 1) * seq_len] for s in range(n_sub)], axis=1)
    y = _bdot(ct_ref[...], wide(xc)) - _bdot(st_ref[...], wide(xs))
    for s in range(n_sub):
        o_ref[s * seq_len:(s + 1) * seq_len, :] = y[:, s * D_FNET:(s + 1) * D_FNET]


def _dft_consts(n, block=1):
    idx = np.arange(n)
    ang = 2.0 * np.pi * ((idx[:, None] * idx[None, :]) % n) / n
    return [jnp.asarray(np.kron(np.eye(block), m).astype(np.float32)).astype(BF16)
            for m in (np.cos(ang) / np.sqrt(n), np.sin(ang) / np.sqrt(n))]


def _fnet(u_f, seq_len, n_sub):
    n_tok = u_f.shape[0]
    rows = n_sub * seq_len
    consts = _dft_consts(seq_len) + _dft_consts(FNET_GROUP_DIM, FNET_GROUPS)
    const = lambda b: (0, 0)
    return pl.pallas_call(
        functools.partial(_fnet_body, seq_len),
        grid=(n_tok // rows,),
        in_specs=[pl.BlockSpec((rows, D_FNET), lambda b: (b, 0))]
        + [pl.BlockSpec(c.shape, const) for c in consts],
        out_specs=pl.BlockSpec((rows, D_FNET), lambda b: (b, 0)),
        out_shape=jax.ShapeDtypeStruct((n_tok, D_FNET), F32),
        compiler_params=_cp(1),
    )(u_f, *consts)


def _ffn_body(final, yr_ref, yd_ref, yf_ref, x_ref, g1_ref, sh2_ref, sc2_ref, g2_ref, n2_ref, fg_ref,
              wo_ref, wi_ref, wf_ref, o_ref):
    part = x_ref.shape[0] // N_PARTS
    parts = [slice(j * part, (j + 1) * part) for j in range(N_PARTS)]
    y = [_bdot(yr_ref[r, :], wo_ref[0:D_RWKV, :])
         + _bdot(yd_ref[r, :], wo_ref[D_RWKV:D_RWKV + D_DIFF, :])
         + _bdot(yf_ref[r, :], wo_ref[D_RWKV + D_DIFF:, :]) for r in parts]
    x = [x_ref[r, :] + g1_ref[0] * y[j] for j, r in enumerate(parts)]
    h = [(_rms(xj, n2_ref[...]) * (1.0 + sc2_ref[0]) + sh2_ref[0]).astype(BF16) for xj in x]
    z = [jnp.dot(hj, wi_ref[...], preferred_element_type=F32) for hj in h]
    act = [zj[:, :D_FF] * jax.nn.sigmoid(zj[:, :D_FF]) * zj[:, D_FF:] for zj in z]
    f = [_bdot(aj, wf_ref[...]) for aj in act]
    for j, r in enumerate(parts):
        xj = x[j] + g2_ref[0] * f[j]
        o_ref[r, :] = _rms(xj, fg_ref[...]) if final else xj


def _ffn(y_r, y_d, y_f, x, mod, layer, row_fn, n2, fg, wo, wi, wf, final, tm):
    n_tok = x.shape[0]
    row = lambda i: (i, 0)
    const = lambda i: (0, 0)
    return pl.pallas_call(
        functools.partial(_ffn_body, final),
        grid=(n_tok // tm,),
        in_specs=[pl.BlockSpec((tm, D_RWKV), row),
                  pl.BlockSpec((tm, D_DIFF), row),
                  pl.BlockSpec((tm, D_FNET), row),
                  pl.BlockSpec((tm, D_MODEL), row),
                  _mod_spec(layer, 2, row_fn),
                  _mod_spec(layer, 3, row_fn),
                  _mod_spec(layer, 4, row_fn),
                  _mod_spec(layer, 5, row_fn),
                  pl.BlockSpec((1, D_MODEL), const),
                  pl.BlockSpec((1, D_MODEL), const),
                  pl.BlockSpec((None,) + wo.shape[1:], lambda i: (layer, 0, 0), pipeline_mode=pl.Buffered(1)),
                  pl.BlockSpec((None,) + wi.shape[1:], lambda i: (layer, 0, 0), pipeline_mode=pl.Buffered(1)),
                  pl.BlockSpec((None,) + wf.shape[1:], lambda i: (layer, 0, 0), pipeline_mode=pl.Buffered(1))],
        out_specs=pl.BlockSpec((tm, D_MODEL), row),
        out_shape=jax.ShapeDtypeStruct((n_tok, D_MODEL), F32),
        compiler_params=_cp(1),
    )(y_r, y_d, y_f, x, mod, mod, mod, mod, n2.reshape(1, D_MODEL), fg.reshape(1, D_MODEL), wo, wi, wf)


def _block_diag_state(s):
    b = s.shape[0]
    s = s.reshape(b, 2, N_PAIR, 2, HEAD_DIM, HEAD_DIM)
    eye = jnp.eye(2, dtype=s.dtype)
    s = s[:, :, :, :, :, None, :] * eye[None, None, None, :, None, :, None]
    return s.reshape(b, 2, N_PAIR, LANES, LANES)


def kernel(x_prompt, x_sample, c, state_rwkv, cache_diff_k, cache_diff_v, c_ctx, norm1_g, norm2_g, final_norm_g, w_mod, b_mod, w_in, w_out, shift_mu, decay_w0, decay_up, iclr_a0, iclr_up, gate_up, k_k, k_a, r_k, lnx_g, lnx_b, diff_lambda, subln_g, w_ffn_in, w_ffn_out):
    p = dict(shift_mu=shift_mu, decay_w0=decay_w0, decay_up=decay_up, iclr_a0=iclr_a0, iclr_up=iclr_up,
             gate_up=gate_up, k_k=k_k, k_a=k_a, r_k=r_k, lnx_g=lnx_g, lnx_b=lnx_b)
    n_ctx, t_ctx, _ = x_prompt.shape
    n_dec, t_dec, _ = x_sample.shape
    past = cache_diff_k.shape[2]

    cond = jnp.concatenate([c_ctx[None, :], c, jnp.zeros((MOD_ROWS - 1 - n_dec, D_MODEL), F32)], axis=0)
    mod = _modulation(cond, w_mod, b_mod).reshape(DEPTH * MOD_ROWS, 1, 6 * D_MODEL)

    tm_ffn = 512
    streams = [
        dict(x=x_prompt.reshape(n_ctx * t_ctx, D_MODEL), t=t_ctx, n=n_ctx, n_sub=4, tm_in=2 * t_ctx,
             row_in=lambda i: 0, row_ffn=lambda i: 0),
        dict(x=x_sample.reshape(n_dec * t_dec, D_MODEL), t=t_dec, n=n_dec, n_sub=1, tm_in=t_dec,
             row_in=lambda i: 1 + i, row_ffn=lambda i: 1 + i // (t_dec // tm_ffn)),
    ]
    cos, sin = _rope_tables(t_dec)
    for st in streams:
        st['kv'] = st['states'] = None
    w_in_l = jnp.concatenate(
        [w_in[:, :, :N_RWKV_IN].astype(BF16), jnp.zeros((DEPTH, D_MODEL, N_RWKV_PAD - N_RWKV_IN), BF16),
         w_in[:, :, N_RWKV_IN:].astype(BF16)], axis=2)
    wo = w_out.astype(BF16)
    wi = w_ffn_in.astype(BF16)
    wf = w_ffn_out.astype(BF16)
    for l in range(DEPTH):
        rw = _rwkv_weights(p, l)
        mu = jnp.concatenate([shift_mu[l], jnp.zeros((2, N_RWKV_PAD - N_RWKV_IN), F32)], axis=1)
        lam_init = 0.8 - 0.6 * math.exp(-0.3 * l)
        sg = jnp.tile(subln_g[l], 2).reshape(1, LANES)
        for si, st in enumerate(streams):
            u_r, q, k_all, v_all, u_f = _inproj(st['x'], mod, l, st['row_in'], norm1_g[l], w_in_l, mu,
                                                 st['kv'], st['t'], st['tm_in'])
            st['kv'] = (k_all, v_all)
            if si == 0:
                s0 = None
                attn_ctx = None
            else:
                s0 = state_rwkv[:, l].astype(F32)
                attn_ctx = (cache_diff_k[:, l].reshape(n_dec, past, D_DIFF).astype(F32),
                            cache_diff_v[:, l].reshape(n_dec, past, D_DIFF).astype(F32), cos, sin)
            y_r, st['states'] = _rwkv(u_r, s0, st['t'], st['n_sub'], rw, st['states'], l)
            y_d = _attention(q, k_all, v_all, l, lam_init, diff_lambda[l], sg, attn_ctx)
            y_f = _fnet(u_f, st['t'], st['n_sub'])
            st['x'] = _ffn(y_r, y_d, y_f, st['x'], mod, l, st['row_ffn'], norm2_g[l], final_norm_g,
                           wo, wi, wf, l == DEPTH - 1, tm_ffn)
    y_prompt = streams[0]['x'].reshape(n_ctx, t_ctx, D_MODEL)
    y_sample = streams[1]['x'].reshape(n_dec, t_dec, D_MODEL)
    new_k = streams[0]['kv'][0].reshape(n_ctx, DEPTH, t_ctx, H_DIFF, 2, D_QK)
    new_v = streams[0]['kv'][1].reshape(n_ctx, DEPTH, t_ctx, H_DIFF, HEAD_DIM)
    return (y_prompt, y_sample, streams[0]['states'], new_k, new_v)
```

```python
import functools
import math

import numpy as np
import jax
import jax.numpy as jnp
from jax import lax
from jax.experimental import pallas as pl
from jax.experimental.pallas import tpu as pltpu

F32 = jnp.float32
BF16 = jnp.bfloat16

D_MODEL = 1024
DEPTH = 2
GRID_W = 64
HEAD_DIM = 64
D_RWKV = 384
H_RWKV = D_RWKV // HEAD_DIM
D_DIFF = 384
H_DIFF = D_DIFF // HEAD_DIM
D_QK = HEAD_DIM // 2
D_FNET = D_MODEL - D_RWKV - D_DIFF
FNET_GROUPS = 4
FNET_GROUP_DIM = D_FNET // FNET_GROUPS
LORA_W = 32
LORA_A = 32
LORA_G = 64
N_RWKV_IN = 3 * D_RWKV + 2 * LORA_W + 2 * LORA_A + LORA_G
N_DIFF_IN = 3 * D_DIFF
D_FF = ((8 * D_MODEL + 3 * 256 - 1) // (3 * 256)) * 256
ROPE_PAIRS = D_QK // 4
ROPE_BASE = 10000.0
RMS_EPS = 1e-6
GN_EPS = 64e-5
SUBLN_EPS = 1e-5
DECAY_SCALE = math.exp(-0.5)

LANES = 128
N_RWKV_PAD = 11 * LANES
D_IN_PAD = N_RWKV_PAD + N_DIFF_IN + D_FNET
N_PAIR = H_RWKV // 2
CHUNK = 64
N_PARTS = 2
MOD_ROWS = 8
ONES_ROWS = 16
VMEM_LIMIT = 56 * 1024 * 1024


def _cp(n_axes=1):
    return pltpu.CompilerParams(dimension_semantics=("arbitrary",) * n_axes,
                                vmem_limit_bytes=VMEM_LIMIT)


def _bdot(a, b):
    return jnp.dot(a.astype(BF16), b.astype(BF16), preferred_element_type=F32)


def _bdot_nt(a, b):
    return lax.dot_general(a.astype(BF16), b.astype(BF16), (((1,), (1,)), ((), ())),
                           preferred_element_type=F32)


def _split2(x):
    hi = x.astype(BF16)
    lo = (x - hi.astype(F32)).astype(BF16)
    return hi, lo


def _dot_x3(a, b):
    a_hi, a_lo = _split2(a)
    b_hi, b_lo = _split2(b)
    d = functools.partial(jnp.dot, preferred_element_type=F32)
    return d(a_hi, b_hi) + d(a_lo, b_hi) + d(a_hi, b_lo)


def _dot_exact_rhs(c_bf16, x):
    d = functools.partial(jnp.dot, preferred_element_type=F32)
    hi, lo = _split2(x)
    return d(c_bf16, hi) + d(c_bf16, lo)


def _rms(x, g):
    return x * lax.rsqrt(jnp.mean(x * x, axis=-1, keepdims=True) + RMS_EPS) * g


def _mod_body(c_ref, w_ref, b_ref, o_ref):
    c = c_ref[...]
    a = c * jax.nn.sigmoid(c)
    o_ref[0] = _dot_x3(a, w_ref[0]) + b_ref[0]


def _modulation(cond, w_mod, b_mod):
    n_layers, _, n_out = w_mod.shape
    tn = 1536
    return pl.pallas_call(
        _mod_body,
        grid=(n_layers, n_out // tn),
        in_specs=[pl.BlockSpec((MOD_ROWS, D_MODEL), lambda l, j: (0, 0)),
                  pl.BlockSpec((1, D_MODEL, tn), lambda l, j: (l, 0, j)),
                  pl.BlockSpec((1, 1, tn), lambda l, j: (l, 0, j))],
        out_specs=pl.BlockSpec((1, MOD_ROWS, tn), lambda l, j: (l, 0, j)),
        out_shape=jax.ShapeDtypeStruct((n_layers, MOD_ROWS, n_out), F32),
        compiler_params=_cp(2),
    )(cond, w_mod, b_mod.reshape(n_layers, 1, n_out))


def _mod_spec(layer, col, row_fn):
    return pl.BlockSpec((1, 1, D_MODEL), lambda i: (layer * MOD_ROWS + row_fn(i), 0, col))


def _put_layer(ref, idx, layer, aliased, val):
    if aliased:
        ref[idx] = val
    else:
        for other in range(DEPTH):
            ref[idx + (other,)] = val if other == layer else jnp.zeros_like(val)


def _inproj_body(seq_len, layer, aliased, x_ref, g_ref, sh_ref, sc_ref, w_ref, mu_ref, *refs):
    ur_ref, q_ref, k_ref, v_ref, uf_ref = refs[2:] if aliased else refs
    part = x_ref.shape[0] // N_PARTS
    parts = [slice(j * part, (j + 1) * part) for j in range(N_PARTS)]
    h = [(_rms(x_ref[r, :], g_ref[...]) * (1.0 + sc_ref[0]) + sh_ref[0]).astype(BF16) for r in parts]
    u = jnp.concatenate([jnp.dot(hj, w_ref[...], preferred_element_type=F32) for hj in h], axis=0)
    ur = u[:, :N_RWKV_PAD]
    tm = ur.shape[0]
    pos = lax.broadcasted_iota(jnp.int32, (tm, 1), 0) & (seq_len - 1)
    prev = jnp.where(pos == 0, 0.0, pltpu.roll(ur, 1, axis=0))
    nxt = jnp.where(pos == seq_len - 1, 0.0, pltpu.roll(ur, tm - 1, axis=0))
    ur_ref[...] = ur + mu_ref[0:1, :] * (prev - ur) + mu_ref[1:2, :] * (nxt - ur)
    q_ref[...] = u[:, N_RWKV_PAD:N_RWKV_PAD + D_DIFF]
    for s in range(tm // seq_len):
        rows = slice(s * seq_len, (s + 1) * seq_len)
        _put_layer(k_ref, (s,), layer, aliased, u[rows, N_RWKV_PAD + D_DIFF:N_RWKV_PAD + 2 * D_DIFF])
        _put_layer(v_ref, (s,), layer, aliased, u[rows, N_RWKV_PAD + 2 * D_DIFF:N_RWKV_PAD + 3 * D_DIFF])
    uf_ref[...] = u[:, N_RWKV_PAD + N_DIFF_IN:]


def _inproj(x, mod, layer, row_fn, g, w, mu, kv, seq_len, tm):
    n_tok = x.shape[0]
    kv_shape = jax.ShapeDtypeStruct((n_tok // seq_len, DEPTH, seq_len, D_DIFF), F32)
    assert tm % seq_len == 0 and seq_len & (seq_len - 1) == 0
    row = lambda i: (i, 0)
    const = lambda i: (0, 0)
    if kv is None:
        kv_spec = pl.BlockSpec((tm // seq_len, DEPTH, seq_len, D_DIFF), lambda i: (i, 0, 0, 0))
    else:
        kv_spec = pl.BlockSpec((tm // seq_len, None, seq_len, D_DIFF), lambda i: (i, layer, 0, 0))
    return pl.pallas_call(
        functools.partial(_inproj_body, seq_len, layer, kv is not None),
        grid=(n_tok // tm,),
        in_specs=[pl.BlockSpec((tm, D_MODEL), row),
                  pl.BlockSpec((1, D_MODEL), const),
                  _mod_spec(layer, 0, row_fn),
                  _mod_spec(layer, 1, row_fn),
                  pl.BlockSpec((None, D_MODEL, D_IN_PAD), lambda i: (layer, 0, 0)),
                  pl.BlockSpec((2, N_RWKV_PAD), const)]
        + [pl.BlockSpec(memory_space=pl.ANY)] * (0 if kv is None else 2),
        out_specs=[pl.BlockSpec((tm, N_RWKV_PAD), row),
                   pl.BlockSpec((tm, D_DIFF), row),
                   kv_spec, kv_spec,
                   pl.BlockSpec((tm, D_FNET), row)],
        out_shape=[jax.ShapeDtypeStruct((n_tok, N_RWKV_PAD), F32),
                   jax.ShapeDtypeStruct((n_tok, D_DIFF), F32),
                   kv_shape, kv_shape,
                   jax.ShapeDtypeStruct((n_tok, D_FNET), F32)],
        input_output_aliases={} if kv is None else {6: 2, 7: 3},
        compiler_params=_cp(1),
    )(x, g.reshape(1, D_MODEL), mod, mod, w, mu, *(kv or ()))


def _lane_masks():
    lane = lax.broadcasted_iota(jnp.int32, (1, LANES), 1)
    return lane < HEAD_DIM, lane >= HEAD_DIM


def _rwkv_units(units, m0, m1, side=()):
    side = list(side)

    def run_side(drain=False):
        for gen in list(side):
            for _ in gen:
                if not drain:
                    break
            else:
                side.remove(gen)

    def bd(x):
        xb = x.astype(BF16)
        zero = jnp.zeros_like(xb)
        return jnp.concatenate([jnp.where(m0, xb, zero), jnp.where(m1, xb, zero)], axis=0)

    row = lax.broadcasted_iota(jnp.int32, (CHUNK, LANES), 0)
    col = lax.broadcasted_iota(jnp.int32, (CHUNK, LANES), 1) & (CHUNK - 1)
    eye = (col == row).astype(F32)
    r2 = lax.broadcasted_iota(jnp.int32, (LANES, LANES), 0) < HEAD_DIM
    c2 = lax.broadcasted_iota(jnp.int32, (LANES, LANES), 1) < HEAD_DIM
    rng = range(len(units))

    pre = []
    for rev, kk, r, v, kd, b, cum, cex, tot, s_prev in units:
        p_inv = jnp.exp(-cum)
        p_rem = jnp.exp(tot - cum)
        ab = -kk * jnp.exp(cex)
        rb = r * jnp.exp(cum)
        strict = (col > row) if rev else (col < row)
        incl = (col >= row) if rev else (col <= row)
        pre.append(dict(ab=ab, rb=rb, vbd=bd(v), strict=strict, incl=incl,
                        lhs=jnp.concatenate([ab, rb], axis=0),
                        rhs=jnp.concatenate([bd(b * p_inv), bd(kd * p_inv)], axis=0),
                        bk=jnp.concatenate([b * p_rem, kd * p_rem], axis=0)))

    run_side()
    mm = [_bdot_nt(q['lhs'], q['rhs']) for q in pre]
    run_side()
    m_ab = [jnp.where(pre[i]['strict'], mm[i][:CHUNK, :LANES], 0.0) for i in rng]
    m_ak = [jnp.where(pre[i]['strict'], mm[i][:CHUNK, LANES:], 0.0) for i in rng]
    m_r = [jnp.concatenate([jnp.where(pre[i]['incl'], mm[i][CHUNK:, :LANES], 0.0),
                            jnp.where(pre[i]['incl'], mm[i][CHUNK:, LANES:], 0.0)], axis=1) for i in rng]
    mv = [_bdot(m_ak[i], pre[i]['vbd']) for i in rng]
    run_side()

    t = [eye + m_ab[i] for i in rng]
    n = [_bdot(m_ab[i], bd(m_ab[i])) for i in rng]
    for _ in range(4):
        x = [_bdot(jnp.concatenate([t[i], n[i]], axis=0), bd(n[i])) for i in rng]
        t = [t[i] + x[i][:CHUNK] for i in rng]
        n = [x[i][CHUNK:] for i in rng]
        run_side()
    t = [t[i] + _bdot(t[i], bd(n[i])) for i in rng]

    w = [_bdot(t[i], jnp.concatenate([bd(pre[i]['ab']), bd(mv[i])], axis=1)) for i in rng]
    xs = [_bdot_nt(jnp.concatenate([w[i][:, :LANES], pre[i]['rb']], axis=0), units[i][9]) for i in rng]
    u = [xs[i][:CHUNK] + w[i][:, LANES:] for i in rng]
    run_side(drain=True)
    y = [xs[i][CHUNK:] + _bdot(m_r[i], jnp.concatenate([bd(u[i]), pre[i]['vbd']], axis=0)) for i in rng]
    z = [_bdot(jnp.concatenate([u[i], units[i][3]], axis=0).T, pre[i]['bk']) for i in rng]
    s_new = [units[i][9] * jnp.exp(units[i][8]) + jnp.where(r2 == c2, z[i], 0.0) for i in rng]
    return y, s_new


def _rwkv_body(seq_len, n_sub, layer, has_s0, aliased, u_ref, *refs):
    s0_ref = refs[0] if has_s0 else None
    refs = refs[1:] if has_s0 else refs
    wdec_ref, wicl_ref, wg_ref, w0a0_ref, vec_ref, bo_ref, tril_ref, triu_ref = refs[:8]
    (y_ref, sfin_ref, r_s, v_s, kk_s, g_s, bon_s, kd_s, b_s, ci_s, ce_s, y_s, st_s) = refs[9 if aliased else 8:]
    n_chunk = seq_len // CHUNK
    assert n_chunk % 2 == 0 and n_chunk >= 4
    k_k = vec_ref[0:1, :]
    k_a = vec_ref[1:2, :]
    r_k = vec_ref[2:3, :]
    lnx_g = vec_ref[3:4, :]
    lnx_b = vec_ref[4:5, :]
    bo = bo_ref[...]

    def headsum(xb):
        return jnp.concatenate([jnp.dot(xb[:, p * LANES:(p + 1) * LANES], bo, preferred_element_type=F32)
                                for p in range(N_PAIR)], axis=1)

    def prep(rows):
        xs = u_ref[rows, :]
        r = xs[:, 0:D_RWKV]
        k = xs[:, D_RWKV:2 * D_RWKV]
        v = xs[:, 2 * D_RWKV:3 * D_RWKV]
        lora = xs[:, 3 * D_RWKV:3 * D_RWKV + LANES]
        t_lora = jnp.tanh(lora).astype(BF16)
        s_gd = jax.nn.sigmoid(xs[:, 3 * D_RWKV + LANES:3 * D_RWKV + 2 * LANES]).astype(BF16)
        lora = lora.astype(BF16)
        kk = k * k_k
        kk2 = (kk * kk).astype(BF16)
        yield
        dec = _bdot(t_lora, wdec_ref[...])
        icl = _bdot(lora, wicl_ref[...])
        g = _bdot(s_gd, wg_ref[...])
        ss = headsum(kk2)
        yield
        logw = -DECAY_SCALE * jax.nn.sigmoid(w0a0_ref[0:1, :] + dec)
        a = jax.nn.sigmoid(w0a0_ref[1:2, :] + icl)
        kk = kk / jnp.maximum(jnp.sqrt(ss), 1e-12)
        a_f = a[:, :D_RWKV]
        a_b = a[:, D_RWKV:]
        kd_f = k * (1.0 + (a_f - 1.0) * k_a)
        kd_b = k * (1.0 + (a_b - 1.0) * k_a)
        bon = _split2(r * (kd_f + kd_b) * r_k)
        lws = [_split2(logw[:, d * D_RWKV:(d + 1) * D_RWKV]) for d in range(2)]
        r_s[rows, :] = r
        v_s[rows, :] = v
        kk_s[rows, :] = kk
        g_s[rows, :] = g
        kd_s[0, rows, :] = kd_f
        kd_s[1, rows, :] = kd_b
        b_s[0, rows, :] = kk * a_f
        b_s[1, rows, :] = kk * a_b
        yield
        bsum = headsum(bon[0]) + headsum(bon[1])
        dd = functools.partial(jnp.dot, preferred_element_type=F32)
        cums = [dd(tri_ref[...], lws[d][0]) + dd(tri_ref[...], lws[d][1])
                for d, tri_ref in enumerate((tril_ref, triu_ref))]
        yield
        bon_s[rows, :] = bsum * v
        for d in range(2):
            ci_s[d, rows, :] = cums[d]
            ce_s[d, rows, :] = cums[d] - logw[:, d * D_RWKV:(d + 1) * D_RWKV]

    def post(rows):
        y = y_s[rows, :]
        yb = y.astype(BF16)
        yield
        mean = headsum(yb) * (1.0 / HEAD_DIM)
        yield
        yc = y - mean
        yc2 = (yc * yc).astype(BF16)
        yield
        var = headsum(yc2) * (1.0 / HEAD_DIM)
        yield
        yn = yc * lax.rsqrt(var + GN_EPS) * lnx_g + lnx_b
        y_ref[rows, :] = (yn + bon_s[rows, :]) * g_s[rows, :]

    def run_all(gens):
        gens = list(gens)
        while gens:
            gens = [gen for gen in gens if next(gen, gens) is not gens]

    def chunk_rows(i):
        return [pl.ds(pl.multiple_of(s * seq_len + (i if d == 0 else n_chunk - 1 - i) * CHUNK, CHUNK), CHUNK)
                for s in range(n_sub) for d in range(2)]

    st_s[...] = s0_ref[...] if has_s0 else jnp.zeros_like(st_s)
    y_s[...] = jnp.zeros_like(y_s)
    m0, m1 = _lane_masks()

    def scan_step(i, prep_next, post_prev):
        rows_sd = chunk_rows(i)
        side = []
        if prep_next:
            side = [prep(rows) for rows in chunk_rows(i + 1)]
        if post_prev:
            side = [post(rows) for rows in chunk_rows(i - 1)]
        units = []
        for s in range(n_sub):
            for d in range(2):
                rows = rows_sd[s * 2 + d]
                cum = ci_s[d, rows, :]
                cex = ce_s[d, rows, :]
                tot = cum[CHUNK - 1:CHUNK] if d == 0 else cum[0:1]
                kk = kk_s[rows, :]
                r = r_s[rows, :]
                v = v_s[rows, :]
                kd = kd_s[d, rows, :]
                b = b_s[d, rows, :]
                for p in range(N_PAIR):
                    sl = slice(p * LANES, (p + 1) * LANES)
                    units.append((d == 1, kk[:, sl], r[:, sl], v[:, sl], kd[:, sl], b[:, sl],
                                  cum[:, sl], cex[:, sl], tot[:, sl], st_s[s, d, p]))
        ys, s_new = _rwkv_units(units, m0, m1, side)
        for s in range(n_sub):
            for d in range(2):
                base = (s * 2 + d) * N_PAIR
                for p in range(N_PAIR):
                    st_s[s, d, p] = s_new[base + p]
                rows = rows_sd[s * 2 + d]
                y_s[rows, :] = y_s[rows, :] + jnp.concatenate(ys[base:base + N_PAIR], axis=1)

    def loop(lo, hi, **kw):
        lax.fori_loop(lo, hi, lambda i, c: (scan_step(i, **kw), c)[1], 0)

    half = n_chunk // 2
    run_all(prep(rows) for rows in chunk_rows(0))
    loop(0, half - 1, prep_next=True, post_prev=False)
    loop(half - 1, half + 1, prep_next=False, post_prev=False)
    loop(half + 1, n_chunk, prep_next=False, post_prev=True)
    run_all(post(rows) for rows in chunk_rows(n_chunk - 1))

    for s in range(n_sub):
        for d in range(2):
            for p in range(N_PAIR):
                st = st_s[s, d, p]
                for h in range(2):
                    rows = slice(h * HEAD_DIM, (h + 1) * HEAD_DIM)
                    val = st[rows, rows]
                    if aliased:
                        sfin_ref[s, d, 2 * p + h] = val
                    else:
                        for other in range(DEPTH):
                            sfin_ref[s, other, d, 2 * p + h] = val if other == layer else jnp.zeros_like(val)


def _rwkv(u_r, s0, seq_len, n_sub, wts, states, layer):
    n_seq = u_r.shape[0] // seq_len
    rows = n_sub * seq_len
    const2 = lambda b: (0, 0)
    st_shape = (n_sub, 2, N_PAIR, LANES, LANES)
    tok = pltpu.VMEM((rows, D_RWKV), F32)
    tok2 = pltpu.VMEM((2, rows, D_RWKV), F32)
    in_specs = [pl.BlockSpec((rows, N_RWKV_PAD), lambda b: (b, 0))]
    args = [u_r]
    if s0 is not None:
        in_specs.append(pl.BlockSpec(st_shape, lambda b: (b, 0, 0, 0, 0)))
        args.append(_block_diag_state(s0))
    in_specs += [pl.BlockSpec(w.shape, const2) for w in wts]
    args += list(wts)
    if states is not None:
        in_specs.append(pl.BlockSpec(memory_space=pl.ANY))
        args.append(states)
    return pl.pallas_call(
        functools.partial(_rwkv_body, seq_len, n_sub, layer, s0 is not None, states is not None),
        grid=(n_seq // n_sub,),
        in_specs=in_specs,
        out_specs=[pl.BlockSpec((rows, D_RWKV), lambda b: (b, 0)),
                   pl.BlockSpec((n_sub, DEPTH, 2, H_RWKV, HEAD_DIM, HEAD_DIM), lambda b: (b, 0, 0, 0, 0, 0))
                   if states is None else
                   pl.BlockSpec((n_sub, None, 2, H_RWKV, HEAD_DIM, HEAD_DIM),
                                lambda b: (b, layer, 0, 0, 0, 0))],
        out_shape=[jax.ShapeDtypeStruct((n_seq * seq_len, D_RWKV), F32),
                   jax.ShapeDtypeStruct((n_seq, DEPTH, 2, H_RWKV, HEAD_DIM, HEAD_DIM), F32)],
        scratch_shapes=[tok] * 5 + [tok2] * 4 + [tok, pltpu.VMEM(st_shape, F32)],
        input_output_aliases={} if states is None else {len(args) - 1: 1},
        compiler_params=_cp(1),
    )(*args)


def _rwkv_weights(p, l):
    z = functools.partial(jnp.zeros, dtype=F32)
    wdec = z((LANES, 2 * D_RWKV))
    wdec = wdec.at[0:LORA_W, :D_RWKV].set(p['decay_up'][l, 0])
    wdec = wdec.at[LORA_W:2 * LORA_W, D_RWKV:].set(p['decay_up'][l, 1])
    wicl = z((LANES, 2 * D_RWKV))
    wicl = wicl.at[2 * LORA_W:2 * LORA_W + LORA_A, :D_RWKV].set(p['iclr_up'][l, 0])
    wicl = wicl.at[2 * LORA_W + LORA_A:2 * LORA_W + 2 * LORA_A, D_RWKV:].set(p['iclr_up'][l, 1])
    wg = z((LANES, D_RWKV)).at[0:LORA_G].set(p['gate_up'][l])
    w0a0 = jnp.stack([p['decay_w0'][l].reshape(-1), p['iclr_a0'][l].reshape(-1)])
    vec = jnp.stack([p['k_k'][l], p['k_a'][l], p['r_k'][l].reshape(-1), p['lnx_g'][l], p['lnx_b'][l],
                     z((D_RWKV,)), z((D_RWKV,)), z((D_RWKV,))])
    head = np.arange(LANES) // HEAD_DIM
    bo = jnp.asarray(head[:, None] == head[None, :], BF16)
    idx = np.arange(CHUNK)
    tril = jnp.asarray(idx[None, :] <= idx[:, None], BF16)
    triu = jnp.asarray(idx[None, :] >= idx[:, None], BF16)
    return [wdec.astype(BF16), wicl.astype(BF16), wg.astype(BF16), w0a0, vec, bo, tril, triu]


def _rope(x, cos, sin):
    lane = lax.broadcasted_iota(jnp.int32, (1, LANES), 1)
    first_half = (lane & ROPE_PAIRS) == 0
    partner = jnp.where(first_half, pltpu.roll(x, LANES - ROPE_PAIRS, axis=1),
                        pltpu.roll(x, ROPE_PAIRS, axis=1))
    return x * cos + partner * sin


def _attn_body(has_ctx, lam_init, *refs):
    if has_ctx:
        (q_ref, k_ref, v_ref, lp_ref, sg_ref, kc_ref, vc_ref, cq_ref, sq_ref, ck_ref, sk_ref,
         o_ref) = refs
    else:
        q_ref, k_ref, v_ref, lp_ref, sg_ref, o_ref = refs
    n_pair = D_DIFF // LANES
    lp = lp_ref[...]
    lam = (jnp.exp(jnp.sum(lp[0:1] * lp[1:2], axis=-1, keepdims=True))
           - jnp.exp(jnp.sum(lp[2:3] * lp[3:4], axis=-1, keepdims=True)) + lam_init)
    lane = lax.broadcasted_iota(jnp.int32, (1, LANES), 1)

    def front(pairs):
        qs, ks, vts = {}, {}, {}
        for p in pairs:
            sl = slice(p * LANES, (p + 1) * LANES)
            q = q_ref[:, sl]
            k = k_ref[:, sl]
            v = v_ref[:, sl]
            if has_ctx:
                q = _rope(q, cq_ref[...], sq_ref[...])
                k = _rope(k, ck_ref[...], sk_ref[...])
                k = jnp.concatenate([kc_ref[0, :, sl], k], axis=0)
                v = jnp.concatenate([vc_ref[0, :, sl], v], axis=0)
            qs[p] = q * (D_QK ** -0.5 * math.log2(math.e))
            ks[p] = k.astype(BF16)
            vt = v.T
            ones = jnp.ones((ONES_ROWS, v.shape[0]), F32)
            vts[p] = [jnp.concatenate([vt[h * HEAD_DIM:(h + 1) * HEAD_DIM], ones], axis=0).astype(BF16)
                      for h in range(2)]
        chains = [(p, h, m) for p in pairs for h in range(2) for m in range(2)]
        scores = []
        for p, h, m in chains:
            lo = h * HEAD_DIM + m * D_QK
            sel = (lane >= lo) & (lane < lo + D_QK)
            scores.append(_bdot_nt(ks[p], jnp.where(sel, qs[p], 0.0)))
        return pairs, chains, scores, vts

    def back(pairs, chains, scores, vts):
        es = [jnp.exp2(s - jnp.max(s, axis=0, keepdims=True)).astype(BF16) for s in scores]
        pv = [jnp.dot(vts[p][h], es[i], preferred_element_type=F32) for i, (p, h, m) in enumerate(chains)]
        for j, p in enumerate(pairs):
            halves = []
            for h in range(2):
                i = (j * 2 + h) * 2
                rows = slice(0, HEAD_DIM)
                o = (pv[i][rows] * (1.0 / pv[i][HEAD_DIM:HEAD_DIM + 1])
                     - lam * (pv[i + 1][rows] * (1.0 / pv[i + 1][HEAD_DIM:HEAD_DIM + 1])))
                ms = jnp.mean(o * o, axis=0, keepdims=True)
                halves.append(o * lax.rsqrt(ms + SUBLN_EPS))
            o_ref[:, p * LANES:(p + 1) * LANES] = (jnp.concatenate(halves, axis=0).T
                                                   * sg_ref[...] * (1.0 - lam_init))

    groups = [[p] for p in range(n_pair)] if has_ctx else [list(range(n_pair))]
    pending = front(groups[0])
    for nxt in groups[1:]:
        ahead = front(nxt)
        back(*pending)
        pending = ahead
    back(*pending)


def _attention(q, k_all, v_all, layer, lam_init, lp, sg, ctx=None):
    n_tok = q.shape[0]
    n_seq, _, seq_len, _ = k_all.shape
    tq = 256
    nq = seq_len // tq
    kv_spec = pl.BlockSpec((None, None, seq_len, D_DIFF), lambda b, i: (b, layer, 0, 0))
    in_specs = [pl.BlockSpec((tq, D_DIFF), lambda b, i: (b * nq + i, 0)),
                kv_spec, kv_spec,
                pl.BlockSpec(lp.shape, lambda b, i: (0, 0)),
                pl.BlockSpec((1, LANES), lambda b, i: (0, 0))]
    args = [q, k_all, v_all, lp, sg]
    if ctx is not None:
        kc, vc, cos, sin = ctx
        past = kc.shape[1]
        in_specs += [pl.BlockSpec((1, past, D_DIFF), lambda b, i: (b, 0, 0)),
                     pl.BlockSpec((1, past, D_DIFF), lambda b, i: (b, 0, 0)),
                     pl.BlockSpec((tq, LANES), lambda b, i: (i, 0)),
                     pl.BlockSpec((tq, LANES), lambda b, i: (i, 0)),
                     pl.BlockSpec((seq_len, LANES), lambda b, i: (0, 0)),
                     pl.BlockSpec((seq_len, LANES), lambda b, i: (0, 0))]
        args += [kc, vc, cos, sin, cos, sin]
    return pl.pallas_call(
        functools.partial(_attn_body, ctx is not None, lam_init),
        grid=(n_seq, nq),
        in_specs=in_specs,
        out_specs=pl.BlockSpec((tq, D_DIFF), lambda b, i: (b * nq + i, 0)),
        out_shape=jax.ShapeDtypeStruct((n_tok, D_DIFF), F32),
        compiler_params=_cp(2),
    )(*args)


def _rope_tables(seq_len):
    t = jnp.arange(seq_len)
    pos = jnp.stack([(t // GRID_W).astype(F32), (t % GRID_W).astype(F32)], axis=1)
    inv = 1.0 / (ROPE_BASE ** (jnp.arange(ROPE_PAIRS, dtype=F32) / ROPE_PAIRS))
    ang = pos[:, :, None] * inv
    d = np.arange(LANES) % D_QK
    axis = d // (2 * ROPE_PAIRS)
    second = (d % (2 * ROPE_PAIRS)) // ROPE_PAIRS
    idx = d % ROPE_PAIRS
    cos = jnp.cos(ang)[:, axis, idx]
    sin = jnp.sin(ang)[:, axis, idx] * jnp.asarray(np.where(second == 1, 1.0, -1.0), F32)
    return cos, sin


def _fnet_body(seq_len, x_ref, ct_ref, st_ref, cc_ref, sc_ref, o_ref):
    n_sub = x_ref.shape[0] // seq_len
    x = x_ref[...].astype(BF16)
    xc = jnp.dot(x, cc_ref[...], preferred_element_type=F32)
    xs = jnp.dot(x, sc_ref[...], preferred_element_type=F32)
    wide = lambda a: jnp.concatenate([a[s * seq_len:(s + 1) * seq_len] for s in range(n_sub)], axis=1)
    y = _bdot(ct_ref[...], wide(xc)) - _bdot(st_ref[...], wide(xs))
    for s in range(n_sub):
        o_ref[s * seq_len:(s + 1) * seq_len, :] = y[:, s * D_FNET:(s + 1) * D_FNET]


def _dft_consts(n, block=1):
    idx = np.arange(n)
    ang = 2.0 * np.pi * ((idx[:, None] * idx[None, :]) % n) / n
    return [jnp.asarray(np.kron(np.eye(block), m).astype(np.float32)).astype(BF16)
            for m in (np.cos(ang) / np.sqrt(n), np.sin(ang) / np.sqrt(n))]


def _fnet(u_f, seq_len, n_sub):
    n_tok = u_f.shape[0]
    rows = n_sub * seq_len
    consts = _dft_consts(seq_len) + _dft_consts(FNET_GROUP_DIM, FNET_GROUPS)
    const = lambda b: (0, 0)
    return pl.pallas_call(
        functools.partial(_fnet_body, seq_len),
        grid=(n_tok // rows,),
        in_specs=[pl.BlockSpec((rows, D_FNET), lambda b: (b, 0))]
        + [pl.BlockSpec(c.shape, const) for c in consts],
        out_specs=pl.BlockSpec((rows, D_FNET), lambda b: (b, 0)),
        out_shape=jax.ShapeDtypeStruct((n_tok, D_FNET), F32),
        compiler_params=_cp(1),
    )(u_f, *consts)


def _ffn_body(final, yr_ref, yd_ref, yf_ref, x_ref, g1_ref, sh2_ref, sc2_ref, g2_ref, n2_ref, fg_ref,
              wo_ref, wi_ref, wf_ref, o_ref):
    part = x_ref.shape[0] // N_PARTS
    parts = [slice(j * part, (j + 1) * part) for j in range(N_PARTS)]
    y = [_bdot(yr_ref[r, :], wo_ref[0:D_RWKV, :])
         + _bdot(yd_ref[r, :], wo_ref[D_RWKV:D_RWKV + D_DIFF, :])
         + _bdot(yf_ref[r, :], wo_ref[D_RWKV + D_DIFF:, :]) for r in parts]
    x = [x_ref[r, :] + g1_ref[0] * y[j] for j, r in enumerate(parts)]
    h = [(_rms(xj, n2_ref[...]) * (1.0 + sc2_ref[0]) + sh2_ref[0]).astype(BF16) for xj in x]
    z = [jnp.dot(hj, wi_ref[...], preferred_element_type=F32) for hj in h]
    act = [zj[:, :D_FF] * jax.nn.sigmoid(zj[:, :D_FF]) * zj[:, D_FF:] for zj in z]
    f = [_bdot(aj, wf_ref[...]) for aj in act]
    for j, r in enumerate(parts):
        xj = x[j] + g2_ref[0] * f[j]
        o_ref[r, :] = _rms(xj, fg_ref[...]) if final else xj


def _ffn(y_r, y_d, y_f, x, mod, layer, row_fn, n2, fg, wo, wi, wf, final, tm):
    n_tok = x.shape[0]
    row = lambda i: (i, 0)
    const = lambda i: (0, 0)
    return pl.pallas_call(
        functools.partial(_ffn_body, final),
        grid=(n_tok // tm,),
        in_specs=[pl.BlockSpec((tm, D_RWKV), row),
                  pl.BlockSpec((tm, D_DIFF), row),
                  pl.BlockSpec((tm, D_FNET), row),
                  pl.BlockSpec((tm, D_MODEL), row),
                  _mod_spec(layer, 2, row_fn),
                  _mod_spec(layer, 3, row_fn),
                  _mod_spec(layer, 4, row_fn),
                  _mod_spec(layer, 5, row_fn),
                  pl.BlockSpec((1, D_MODEL), const),
                  pl.BlockSpec((1, D_MODEL), const),
                  pl.BlockSpec((None,) + wo.shape[1:], lambda i: (layer, 0, 0), pipeline_mode=pl.Buffered(1)),
                  pl.BlockSpec((None,) + wi.shape[1:], lambda i: (layer, 0, 0), pipeline_mode=pl.Buffered(1)),
                  pl.BlockSpec((None,) + wf.shape[1:], lambda i: (layer, 0, 0), pipeline_mode=pl.Buffered(1))],
        out_specs=pl.BlockSpec((tm, D_MODEL), row),
        out_shape=jax.ShapeDtypeStruct((n_tok, D_MODEL), F32),
        compiler_params=_cp(1),
    )(y_r, y_d, y_f, x, mod, mod, mod, mod, n2.reshape(1, D_MODEL), fg.reshape(1, D_MODEL), wo, wi, wf)


def _block_diag_state(s):
    b = s.shape[0]
    s = s.reshape(b, 2, N_PAIR, 2, HEAD_DIM, HEAD_DIM)
    eye = jnp.eye(2, dtype=s.dtype)
    s = s[:, :, :, :, :, None, :] * eye[None, None, None, :, None, :, None]
    return s.reshape(b, 2, N_PAIR, LANES, LANES)


def kernel(x_prompt, x_sample, c, state_rwkv, cache_diff_k, cache_diff_v, c_ctx, norm1_g, norm2_g, final_norm_g, w_mod, b_mod, w_in, w_out, shift_mu, decay_w0, decay_up, iclr_a0, iclr_up, gate_up, k_k, k_a, r_k, lnx_g, lnx_b, diff_lambda, subln_g, w_ffn_in, w_ffn_out):
    p = dict(shift_mu=shift_mu, decay_w0=decay_w0, decay_up=decay_up, iclr_a0=iclr_a0, iclr_up=iclr_up,
             gate_up=gate_up, k_k=k_k, k_a=k_a, r_k=r_k, lnx_g=lnx_g, lnx_b=lnx_b)
    n_ctx, t_ctx, _ = x_prompt.shape
    n_dec, t_dec, _ = x_sample.shape
    past = cache_diff_k.shape[2]

    cond = jnp.concatenate([c_ctx[None, :], c, jnp.zeros((MOD_ROWS - 1 - n_dec, D_MODEL), F32)], axis=0)
    mod = _modulation(cond, w_mod, b_mod).reshape(DEPTH * MOD_ROWS, 1, 6 * D_MODEL)

    tm_ffn = 512
    streams = [
        dict(x=x_prompt.reshape(n_ctx * t_ctx, D_MODEL), t=t_ctx, n=n_ctx, n_sub=4, tm_in=2 * t_ctx,
             row_in=lambda i: 0, row_ffn=lambda i: 0),
        dict(x=x_sample.reshape(n_dec * t_dec, D_MODEL), t=t_dec, n=n_dec, n_sub=1, tm_in=t_dec,
             row_in=lambda i: 1 + i, row_ffn=lambda i: 1 + i // (t_dec // tm_ffn)),
    ]
    cos, sin = _rope_tables(t_dec)
    for st in streams:
        st['kv'] = st['states'] = None
    w_in_l = jnp.concatenate(
        [w_in[:, :, :N_RWKV_IN].astype(BF16), jnp.zeros((DEPTH, D_MODEL, N_RWKV_PAD - N_RWKV_IN), BF16),
         w_in[:, :, N_RWKV_IN:].astype(BF16)], axis=2)
    wo = w_out.astype(BF16)
    wi = w_ffn_in.astype(BF16)
    wf = w_ffn_out.astype(BF16)
    for l in range(DEPTH):
        rw = _rwkv_weights(p, l)
        mu = jnp.concatenate([shift_mu[l], jnp.zeros((2, N_RWKV_PAD - N_RWKV_IN), F32)], axis=1)
        lam_init = 0.8 - 0.6 * math.exp(-0.3 * l)
        sg = jnp.tile(subln_g[l], 2).reshape(1, LANES)
        for si, st in enumerate(streams):
            u_r, q, k_all, v_all, u_f = _inproj(st['x'], mod, l, st['row_in'], norm1_g[l], w_in_l, mu,
                                                 st['kv'], st['t'], st['tm_in'])
            st['kv'] = (k_all, v_all)
            if si == 0:
                s0 = None
                attn_ctx = None
            else:
                s0 = state_rwkv[:, l].astype(F32)
                attn_ctx = (cache_diff_k[:, l].reshape(n_dec, past, D_DIFF).astype(F32),
                            cache_diff_v[:, l].reshape(n_dec, past, D_DIFF).astype(F32), cos, sin)
            y_r, st['states'] = _rwkv(u_r, s0, st['t'], st['n_sub'], rw, st['states'], l)
            y_d = _attention(q, k_all, v_all, l, lam_init, diff_lambda[l], sg, attn_ctx)
            y_f = _fnet(u_f, st['t'], st['n_sub'])
            st['x'] = _ffn(y_r, y_d, y_f, st['x'], mod, l, st['row_ffn'], norm2_g[l], final_norm_g,
                           wo, wi, wf, l == DEPTH - 1, tm_ffn)
    y_prompt = streams[0]['x'].reshape(n_ctx, t_ctx, D_MODEL)
    y_sample = streams[1]['x'].reshape(n_dec, t_dec, D_MODEL)
    new_k = streams[0]['kv'][0].reshape(n_ctx, DEPTH, t_ctx, H_DIFF, 2, D_QK)
    new_v = streams[0]['kv'][1].reshape(n_ctx, DEPTH, t_ctx, H_DIFF, HEAD_DIM)
    return (y_prompt, y_sample, streams[0]['states'], new_k, new_v)
```

```python
import functools
import math

import numpy as np
import jax
import jax.numpy as jnp
from jax import lax
from jax.experimental import pallas as pl
from jax.experimental.pallas import tpu as pltpu

F32 = jnp.float32
BF16 = jnp.bfloat16

D_MODEL = 1024
DEPTH = 2
GRID_W = 64
HEAD_DIM = 64
D_RWKV = 384
H_RWKV = D_RWKV // HEAD_DIM
D_DIFF = 384
H_DIFF = D_DIFF // HEAD_DIM
D_QK = HEAD_DIM // 2
D_FNET = D_MODEL - D_RWKV - D_DIFF
FNET_GROUPS = 4
FNET_GROUP_DIM = D_FNET // FNET_GROUPS
LORA_W = 32
LORA_A = 32
LORA_G = 64
N_RWKV_IN = 3 * D_RWKV + 2 * LORA_W + 2 * LORA_A + LORA_G
N_DIFF_IN = 3 * D_DIFF
D_FF = ((8 * D_MODEL + 3 * 256 - 1) // (3 * 256)) * 256
ROPE_PAIRS = D_QK // 4
ROPE_BASE = 10000.0
RMS_EPS = 1e-6
GN_EPS = 64e-5
SUBLN_EPS = 1e-5
DECAY_SCALE = math.exp(-0.5)

LANES = 128
N_RWKV_PAD = 11 * LANES
D_IN_PAD = N_RWKV_PAD + N_DIFF_IN + D_FNET
N_PAIR = H_RWKV // 2
CHUNK = 64
N_PARTS = 2
SIDE_WORK_MAX_UNITS = 12
MOD_ROWS = 8
ONES_ROWS = 16
VMEM_LIMIT = 56 * 1024 * 1024


def _cp(n_axes=1):
    return pltpu.CompilerParams(dimension_semantics=("arbitrary",) * n_axes,
                                vmem_limit_bytes=VMEM_LIMIT)


def _bdot(a, b):
    return jnp.dot(a.astype(BF16), b.astype(BF16), preferred_element_type=F32)


def _bdot_nt(a, b):
    return lax.dot_general(a.astype(BF16), b.astype(BF16), (((1,), (1,)), ((), ())),
                           preferred_element_type=F32)


def _split2(x):
    hi = x.astype(BF16)
    lo = (x - hi.astype(F32)).astype(BF16)
    return hi, lo


def _dot_x3(a, b):
    a_hi, a_lo = _split2(a)
    b_hi, b_lo = _split2(b)
    d = functools.partial(jnp.dot, preferred_element_type=F32)
    return d(a_hi, b_hi) + d(a_lo, b_hi) + d(a_hi, b_lo)


def _dot_exact_rhs(c_bf16, x):
    d = functools.partial(jnp.dot, preferred_element_type=F32)
    hi, lo = _split2(x)
    return d(c_bf16, hi) + d(c_bf16, lo)


def _rms(x, g):
    return x * lax.rsqrt(jnp.mean(x * x, axis=-1, keepdims=True) + RMS_EPS) * g


def _mod_body(c_ref, w_ref, b_ref, o_ref):
    c = c_ref[...]
    a = c * jax.nn.sigmoid(c)
    o_ref[0] = _dot_x3(a, w_ref[0]) + b_ref[0]


def _modulation(cond, w_mod, b_mod):
    n_layers, _, n_out = w_mod.shape
    tn = 1536
    return pl.pallas_call(
        _mod_body,
        grid=(n_layers, n_out // tn),
        in_specs=[pl.BlockSpec((MOD_ROWS, D_MODEL), lambda l, j: (0, 0)),
                  pl.BlockSpec((1, D_MODEL, tn), lambda l, j: (l, 0, j)),
                  pl.BlockSpec((1, 1, tn), lambda l, j: (l, 0, j))],
        out_specs=pl.BlockSpec((1, MOD_ROWS, tn), lambda l, j: (l, 0, j)),
        out_shape=jax.ShapeDtypeStruct((n_layers, MOD_ROWS, n_out), F32),
        compiler_params=_cp(2),
    )(cond, w_mod, b_mod.reshape(n_layers, 1, n_out))


def _mod_spec(layer, col, row_fn):
    return pl.BlockSpec((1, 1, D_MODEL), lambda i: (layer * MOD_ROWS + row_fn(i), 0, col))


def _put_layer(ref, idx, layer, aliased, val):
    if aliased:
        ref[idx] = val
    else:
        for other in range(DEPTH):
            ref[idx + (other,)] = val if other == layer else jnp.zeros_like(val)


def _inproj_body(seq_len, layer, aliased, x_ref, g_ref, sh_ref, sc_ref, w_ref, mu_ref, *refs):
    ur_ref, q_ref, k_ref, v_ref, uf_ref = refs[2:] if aliased else refs
    part = x_ref.shape[0] // N_PARTS
    parts = [slice(j * part, (j + 1) * part) for j in range(N_PARTS)]
    h = [(_rms(x_ref[r, :], g_ref[...]) * (1.0 + sc_ref[0]) + sh_ref[0]).astype(BF16) for r in parts]
    u = jnp.concatenate([jnp.dot(hj, w_ref[...], preferred_element_type=F32) for hj in h], axis=0)
    ur = u[:, :N_RWKV_PAD]
    tm = ur.shape[0]
    pos = lax.broadcasted_iota(jnp.int32, (tm, 1), 0) & (seq_len - 1)
    prev = jnp.where(pos == 0, 0.0, pltpu.roll(ur, 1, axis=0))
    nxt = jnp.where(pos == seq_len - 1, 0.0, pltpu.roll(ur, tm - 1, axis=0))
    ur_ref[...] = ur + mu_ref[0:1, :] * (prev - ur) + mu_ref[1:2, :] * (nxt - ur)
    q_ref[...] = u[:, N_RWKV_PAD:N_RWKV_PAD + D_DIFF]
    for s in range(tm // seq_len):
        rows = slice(s * seq_len, (s + 1) * seq_len)
        _put_layer(k_ref, (s,), layer, aliased, u[rows, N_RWKV_PAD + D_DIFF:N_RWKV_PAD + 2 * D_DIFF])
        _put_layer(v_ref, (s,), layer, aliased, u[rows, N_RWKV_PAD + 2 * D_DIFF:N_RWKV_PAD + 3 * D_DIFF])
    uf_ref[...] = u[:, N_RWKV_PAD + N_DIFF_IN:]


def _inproj(x, mod, layer, row_fn, g, w, mu, kv, seq_len, tm):
    n_tok = x.shape[0]
    kv_shape = jax.ShapeDtypeStruct((n_tok // seq_len, DEPTH, seq_len, D_DIFF), F32)
    assert tm % seq_len == 0 and seq_len & (seq_len - 1) == 0
    row = lambda i: (i, 0)
    const = lambda i: (0, 0)
    if kv is None:
        kv_spec = pl.BlockSpec((tm // seq_len, DEPTH, seq_len, D_DIFF), lambda i: (i, 0, 0, 0))
    else:
        kv_spec = pl.BlockSpec((tm // seq_len, None, seq_len, D_DIFF), lambda i: (i, layer, 0, 0))
    return pl.pallas_call(
        functools.partial(_inproj_body, seq_len, layer, kv is not None),
        grid=(n_tok // tm,),
        in_specs=[pl.BlockSpec((tm, D_MODEL), row),
                  pl.BlockSpec((1, D_MODEL), const),
                  _mod_spec(layer, 0, row_fn),
                  _mod_spec(layer, 1, row_fn),
                  pl.BlockSpec((None, D_MODEL, D_IN_PAD), lambda i: (layer, 0, 0)),
                  pl.BlockSpec((2, N_RWKV_PAD), const)]
        + [pl.BlockSpec(memory_space=pl.ANY)] * (0 if kv is None else 2),
        out_specs=[pl.BlockSpec((tm, N_RWKV_PAD), row),
                   pl.BlockSpec((tm, D_DIFF), row),
                   kv_spec, kv_spec,
                   pl.BlockSpec((tm, D_FNET), row)],
        out_shape=[jax.ShapeDtypeStruct((n_tok, N_RWKV_PAD), F32),
                   jax.ShapeDtypeStruct((n_tok, D_DIFF), F32),
                   kv_shape, kv_shape,
                   jax.ShapeDtypeStruct((n_tok, D_FNET), F32)],
        input_output_aliases={} if kv is None else {6: 2, 7: 3},
        compiler_params=_cp(1),
    )(x, g.reshape(1, D_MODEL), mod, mod, w, mu, *(kv or ()))


def _lane_masks():
    lane = lax.broadcasted_iota(jnp.int32, (1, LANES), 1)
    return lane < HEAD_DIM, lane >= HEAD_DIM


def _rwkv_units(units, m0, m1, side=()):
    side = list(side)

    def run_side(drain=False):
        for gen in list(side):
            for _ in gen:
                if not drain:
                    break
            else:
                side.remove(gen)

    def bd(x):
        xb = x.astype(BF16)
        zero = jnp.zeros_like(xb)
        return jnp.concatenate([jnp.where(m0, xb, zero), jnp.where(m1, xb, zero)], axis=0)

    row = lax.broadcasted_iota(jnp.int32, (CHUNK, LANES), 0)
    col = lax.broadcasted_iota(jnp.int32, (CHUNK, LANES), 1) & (CHUNK - 1)
    eye = (col == row).astype(F32)
    r2 = lax.broadcasted_iota(jnp.int32, (LANES, LANES), 0) < HEAD_DIM
    c2 = lax.broadcasted_iota(jnp.int32, (LANES, LANES), 1) < HEAD_DIM
    rng = range(len(units))

    pre = []
    for rev, kk, r, v, kd, b, cum, cex, tot, s_prev in units:
        p_inv = jnp.exp(-cum)
        p_rem = jnp.exp(tot - cum)
        ab = -kk * jnp.exp(cex)
        rb = r * jnp.exp(cum)
        strict = (col > row) if rev else (col < row)
        incl = (col >= row) if rev else (col <= row)
        pre.append(dict(ab=ab, rb=rb, vbd=bd(v), strict=strict, incl=incl,
                        lhs=jnp.concatenate([ab, rb], axis=0),
                        rhs=jnp.concatenate([bd(b * p_inv), bd(kd * p_inv)], axis=0),
                        bk=jnp.concatenate([b * p_rem, kd * p_rem], axis=0)))

    run_side()
    mm = [_bdot_nt(q['lhs'], q['rhs']) for q in pre]
    run_side()
    m_ab = [jnp.where(pre[i]['strict'], mm[i][:CHUNK, :LANES], 0.0) for i in rng]
    m_ak = [jnp.where(pre[i]['strict'], mm[i][:CHUNK, LANES:], 0.0) for i in rng]
    m_r = [jnp.concatenate([jnp.where(pre[i]['incl'], mm[i][CHUNK:, :LANES], 0.0),
                            jnp.where(pre[i]['incl'], mm[i][CHUNK:, LANES:], 0.0)], axis=1) for i in rng]
    mv = [_bdot(m_ak[i], pre[i]['vbd']) for i in rng]
    run_side()

    t = [eye + m_ab[i] for i in rng]
    n = [_bdot(m_ab[i], bd(m_ab[i])) for i in rng]
    for _ in range(4):
        x = [_bdot(jnp.concatenate([t[i], n[i]], axis=0), bd(n[i])) for i in rng]
        t = [t[i] + x[i][:CHUNK] for i in rng]
        n = [x[i][CHUNK:] for i in rng]
        run_side()
    t = [t[i] + _bdot(t[i], bd(n[i])) for i in rng]

    w = [_bdot(t[i], jnp.concatenate([bd(pre[i]['ab']), bd(mv[i])], axis=1)) for i in rng]
    xs = [_bdot_nt(jnp.concatenate([w[i][:, :LANES], pre[i]['rb']], axis=0), units[i][9]) for i in rng]
    u = [xs[i][:CHUNK] + w[i][:, LANES:] for i in rng]
    run_side(drain=True)
    y = [xs[i][CHUNK:] + _bdot(m_r[i], jnp.concatenate([bd(u[i]), pre[i]['vbd']], axis=0)) for i in rng]
    z = [_bdot(jnp.concatenate([u[i], units[i][3]], axis=0).T, pre[i]['bk']) for i in rng]
    s_new = [units[i][9] * jnp.exp(units[i][8]) + jnp.where(r2 == c2, z[i], 0.0) for i in rng]
    return y, s_new


def _rwkv_body(seq_len, n_sub, layer, has_s0, aliased, u_ref, *refs):
    s0_ref = refs[0] if has_s0 else None
    refs = refs[1:] if has_s0 else refs
    wdec_ref, wicl_ref, wg_ref, w0a0_ref, vec_ref, bo_ref, tril_ref, triu_ref = refs[:8]
    (y_ref, sfin_ref, r_s, v_s, kk_s, g_s, bon_s, kd_s, b_s, ci_s, ce_s, y_s, st_s) = refs[9 if aliased else 8:]
    n_chunk = seq_len // CHUNK
    assert n_chunk % 2 == 0 and n_chunk >= 4
    k_k = vec_ref[0:1, :]
    k_a = vec_ref[1:2, :]
    r_k = vec_ref[2:3, :]
    lnx_g = vec_ref[3:4, :]
    lnx_b = vec_ref[4:5, :]
    bo = bo_ref[...]

    def headsum(xb):
        return jnp.concatenate([jnp.dot(xb[:, p * LANES:(p + 1) * LANES], bo, preferred_element_type=F32)
                                for p in range(N_PAIR)], axis=1)

    def prep(rows):
        xs = u_ref[rows, :]
        r = xs[:, 0:D_RWKV]
        k = xs[:, D_RWKV:2 * D_RWKV]
        v = xs[:, 2 * D_RWKV:3 * D_RWKV]
        lora = xs[:, 3 * D_RWKV:3 * D_RWKV + LANES]
        t_lora = jnp.tanh(lora).astype(BF16)
        s_gd = jax.nn.sigmoid(xs[:, 3 * D_RWKV + LANES:3 * D_RWKV + 2 * LANES]).astype(BF16)
        lora = lora.astype(BF16)
        kk = k * k_k
        kk2 = (kk * kk).astype(BF16)
        yield
        dec = _bdot(t_lora, wdec_ref[...])
        icl = _bdot(lora, wicl_ref[...])
        g = _bdot(s_gd, wg_ref[...])
        ss = headsum(kk2)
        yield
        logw = -DECAY_SCALE * jax.nn.sigmoid(w0a0_ref[0:1, :] + dec)
        a = jax.nn.sigmoid(w0a0_ref[1:2, :] + icl)
        kk = kk / jnp.maximum(jnp.sqrt(ss), 1e-12)
        a_f = a[:, :D_RWKV]
        a_b = a[:, D_RWKV:]
        kd_f = k * (1.0 + (a_f - 1.0) * k_a)
        kd_b = k * (1.0 + (a_b - 1.0) * k_a)
        bon = _split2(r * (kd_f + kd_b) * r_k)
        lws = [_split2(logw[:, d * D_RWKV:(d + 1) * D_RWKV]) for d in range(2)]
        r_s[rows, :] = r
        v_s[rows, :] = v
        kk_s[rows, :] = kk
        g_s[rows, :] = g
        kd_s[0, rows, :] = kd_f
        kd_s[1, rows, :] = kd_b
        b_s[0, rows, :] = kk * a_f
        b_s[1, rows, :] = kk * a_b
        yield
        bsum = headsum(bon[0]) + headsum(bon[1])
        dd = functools.partial(jnp.dot, preferred_element_type=F32)
        cums = [dd(tri_ref[...], lws[d][0]) + dd(tri_ref[...], lws[d][1])
                for d, tri_ref in enumerate((tril_ref, triu_ref))]
        yield
        bon_s[rows, :] = bsum * v
        for d in range(2):
            ci_s[d, rows, :] = cums[d]
            ce_s[d, rows, :] = cums[d] - logw[:, d * D_RWKV:(d + 1) * D_RWKV]

    def post(rows):
        y = y_s[rows, :]
        yb = y.astype(BF16)
        yield
        mean = headsum(yb) * (1.0 / HEAD_DIM)
        yield
        yc = y - mean
        yc2 = (yc * yc).astype(BF16)
        yield
        var = headsum(yc2) * (1.0 / HEAD_DIM)
        yield
        yn = yc * lax.rsqrt(var + GN_EPS) * lnx_g + lnx_b
        y_ref[rows, :] = (yn + bon_s[rows, :]) * g_s[rows, :]

    def run_all(gens):
        gens = list(gens)
        while gens:
            gens = [gen for gen in gens if next(gen, gens) is not gens]

    def chunk_rows(i):
        return [pl.ds(pl.multiple_of(s * seq_len + (i if d == 0 else n_chunk - 1 - i) * CHUNK, CHUNK), CHUNK)
                for s in range(n_sub) for d in range(2)]

    st_s[...] = s0_ref[...] if has_s0 else jnp.zeros_like(st_s)
    y_s[...] = jnp.zeros_like(y_s)
    m0, m1 = _lane_masks()

    def scan_step(i, prep_next, post_prev):
        rows_sd = chunk_rows(i)
        side = []
        if prep_next:
            side = [prep(rows) for rows in chunk_rows(i + 1)]
        if post_prev:
            side = [post(rows) for rows in chunk_rows(i - 1)]
        units = []
        for s in range(n_sub):
            for d in range(2):
                rows = rows_sd[s * 2 + d]
                cum = ci_s[d, rows, :]
                cex = ce_s[d, rows, :]
                tot = cum[CHUNK - 1:CHUNK] if d == 0 else cum[0:1]
                kk = kk_s[rows, :]
                r = r_s[rows, :]
                v = v_s[rows, :]
                kd = kd_s[d, rows, :]
                b = b_s[d, rows, :]
                for p in range(N_PAIR):
                    sl = slice(p * LANES, (p + 1) * LANES)
                    units.append((d == 1, kk[:, sl], r[:, sl], v[:, sl], kd[:, sl], b[:, sl],
                                  cum[:, sl], cex[:, sl], tot[:, sl], st_s[s, d, p]))
        ys, s_new = _rwkv_units(units, m0, m1, side)
        for s in range(n_sub):
            for d in range(2):
                base = (s * 2 + d) * N_PAIR
                for p in range(N_PAIR):
                    st_s[s, d, p] = s_new[base + p]
                rows = rows_sd[s * 2 + d]
                y_s[rows, :] = y_s[rows, :] + jnp.concatenate(ys[base:base + N_PAIR], axis=1)

    def loop(lo, hi, **kw):
        lax.fori_loop(lo, hi, lambda i, c: (scan_step(i, **kw), c)[1], 0)

    half = n_chunk // 2
    if 2 * N_PAIR * n_sub <= SIDE_WORK_MAX_UNITS:
        run_all(prep(rows) for rows in chunk_rows(0))
        loop(0, half - 1, prep_next=True, post_prev=False)
        loop(half - 1, half + 1, prep_next=False, post_prev=False)
        loop(half + 1, n_chunk, prep_next=False, post_prev=True)
        run_all(post(rows) for rows in chunk_rows(n_chunk - 1))
    else:
        for i in range(half):
            run_all(prep(rows) for rows in chunk_rows(i))
        loop(0, n_chunk, prep_next=False, post_prev=False)
        for i in range(half):
            run_all(post(rows) for rows in chunk_rows(i))

    for s in range(n_sub):
        for d in range(2):
            for p in range(N_PAIR):
                st = st_s[s, d, p]
                for h in range(2):
                    rows = slice(h * HEAD_DIM, (h + 1) * HEAD_DIM)
                    val = st[rows, rows]
                    if aliased:
                        sfin_ref[s, d, 2 * p + h] = val
                    else:
                        for other in range(DEPTH):
                            sfin_ref[s, other, d, 2 * p + h] = val if other == layer else jnp.zeros_like(val)


def _rwkv(u_r, s0, seq_len, n_sub, wts, states, layer):
    n_seq = u_r.shape[0] // seq_len
    rows = n_sub * seq_len
    const2 = lambda b: (0, 0)
    st_shape = (n_sub, 2, N_PAIR, LANES, LANES)
    tok = pltpu.VMEM((rows, D_RWKV), F32)
    tok2 = pltpu.VMEM((2, rows, D_RWKV), F32)
    in_specs = [pl.BlockSpec((rows, N_RWKV_PAD), lambda b: (b, 0))]
    args = [u_r]
    if s0 is not None:
        in_specs.append(pl.BlockSpec(st_shape, lambda b: (b, 0, 0, 0, 0)))
        args.append(_block_diag_state(s0))
    in_specs += [pl.BlockSpec(w.shape, const2) for w in wts]
    args += list(wts)
    if states is not None:
        in_specs.append(pl.BlockSpec(memory_space=pl.ANY))
        args.append(states)
    return pl.pallas_call(
        functools.partial(_rwkv_body, seq_len, n_sub, layer, s0 is not None, states is not None),
        grid=(n_seq // n_sub,),
        in_specs=in_specs,
        out_specs=[pl.BlockSpec((rows, D_RWKV), lambda b: (b, 0)),
                   pl.BlockSpec((n_sub, DEPTH, 2, H_RWKV, HEAD_DIM, HEAD_DIM), lambda b: (b, 0, 0, 0, 0, 0))
                   if states is None else
                   pl.BlockSpec((n_sub, None, 2, H_RWKV, HEAD_DIM, HEAD_DIM),
                                lambda b: (b, layer, 0, 0, 0, 0))],
        out_shape=[jax.ShapeDtypeStruct((n_seq * seq_len, D_RWKV), F32),
                   jax.ShapeDtypeStruct((n_seq, DEPTH, 2, H_RWKV, HEAD_DIM, HEAD_DIM), F32)],
        scratch_shapes=[tok] * 5 + [tok2] * 4 + [tok, pltpu.VMEM(st_shape, F32)],
        input_output_aliases={} if states is None else {len(args) - 1: 1},
        compiler_params=_cp(1),
    )(*args)


def _rwkv_weights(p, l):
    z = functools.partial(jnp.zeros, dtype=F32)
    wdec = z((LANES, 2 * D_RWKV))
    wdec = wdec.at[0:LORA_W, :D_RWKV].set(p['decay_up'][l, 0])
    wdec = wdec.at[LORA_W:2 * LORA_W, D_RWKV:].set(p['decay_up'][l, 1])
    wicl = z((LANES, 2 * D_RWKV))
    wicl = wicl.at[2 * LORA_W:2 * LORA_W + LORA_A, :D_RWKV].set(p['iclr_up'][l, 0])
    wicl = wicl.at[2 * LORA_W + LORA_A:2 * LORA_W + 2 * LORA_A, D_RWKV:].set(p['iclr_up'][l, 1])
    wg = z((LANES, D_RWKV)).at[0:LORA_G].set(p['gate_up'][l])
    w0a0 = jnp.stack([p['decay_w0'][l].reshape(-1), p['iclr_a0'][l].reshape(-1)])
    vec = jnp.stack([p['k_k'][l], p['k_a'][l], p['r_k'][l].reshape(-1), p['lnx_g'][l], p['lnx_b'][l],
                     z((D_RWKV,)), z((D_RWKV,)), z((D_RWKV,))])
    head = np.arange(LANES) // HEAD_DIM
    bo = jnp.asarray(head[:, None] == head[None, :], BF16)
    idx = np.arange(CHUNK)
    tril = jnp.asarray(idx[None, :] <= idx[:, None], BF16)
    triu = jnp.asarray(idx[None, :] >= idx[:, None], BF16)
    return [wdec.astype(BF16), wicl.astype(BF16), wg.astype(BF16), w0a0, vec, bo, tril, triu]


def _rope(x, cos, sin):
    lane = lax.broadcasted_iota(jnp.int32, (1, LANES), 1)
    first_half = (lane & ROPE_PAIRS) == 0
    partner = jnp.where(first_half, pltpu.roll(x, LANES - ROPE_PAIRS, axis=1),
                        pltpu.roll(x, ROPE_PAIRS, axis=1))
    return x * cos + partner * sin


def _attn_body(has_ctx, lam_init, *refs):
    if has_ctx:
        (q_ref, k_ref, v_ref, lp_ref, sg_ref, kc_ref, vc_ref, cq_ref, sq_ref, ck_ref, sk_ref,
         o_ref) = refs
    else:
        q_ref, k_ref, v_ref, lp_ref, sg_ref, o_ref = refs
    n_pair = D_DIFF // LANES
    lp = lp_ref[...]
    lam = (jnp.exp(jnp.sum(lp[0:1] * lp[1:2], axis=-1, keepdims=True))
           - jnp.exp(jnp.sum(lp[2:3] * lp[3:4], axis=-1, keepdims=True)) + lam_init)
    lane = lax.broadcasted_iota(jnp.int32, (1, LANES), 1)

    tq = q_ref.shape[0] // k_ref.shape[0]

    def front(seq, pairs):
        qs, ks, vts = {}, {}, {}
        for p in pairs:
            sl = slice(p * LANES, (p + 1) * LANES)
            q = q_ref[seq * tq:(seq + 1) * tq, sl]
            k = k_ref[seq, :, sl]
            v = v_ref[seq, :, sl]
            if has_ctx:
                q = _rope(q, cq_ref[...], sq_ref[...])
                k = _rope(k, ck_ref[...], sk_ref[...])
                k = jnp.concatenate([kc_ref[0, :, sl], k], axis=0)
                v = jnp.concatenate([vc_ref[0, :, sl], v], axis=0)
            qs[p] = q * (D_QK ** -0.5 * math.log2(math.e))
            ks[p] = k.astype(BF16)
            vt = v.T
            ones = jnp.ones((ONES_ROWS, v.shape[0]), F32)
            vts[p] = [jnp.concatenate([vt[h * HEAD_DIM:(h + 1) * HEAD_DIM], ones], axis=0).astype(BF16)
                      for h in range(2)]
        chains = [(p, h, m) for p in pairs for h in range(2) for m in range(2)]
        scores = []
        for p, h, m in chains:
            lo = h * HEAD_DIM + m * D_QK
            sel = (lane >= lo) & (lane < lo + D_QK)
            scores.append(_bdot_nt(ks[p], jnp.where(sel, qs[p], 0.0)))
        return seq, pairs, chains, scores, vts

    def back(seq, pairs, chains, scores, vts):
        es = [jnp.exp2(s - jnp.max(s, axis=0, keepdims=True)).astype(BF16) for s in scores]
        pv = [jnp.dot(vts[p][h], es[i], preferred_element_type=F32) for i, (p, h, m) in enumerate(chains)]
        for j, p in enumerate(pairs):
            halves = []
            for h in range(2):
                i = (j * 2 + h) * 2
                rows = slice(0, HEAD_DIM)
                o = (pv[i][rows] * (1.0 / pv[i][HEAD_DIM:HEAD_DIM + 1])
                     - lam * (pv[i + 1][rows] * (1.0 / pv[i + 1][HEAD_DIM:HEAD_DIM + 1])))
                ms = jnp.mean(o * o, axis=0, keepdims=True)
                halves.append(o * lax.rsqrt(ms + SUBLN_EPS))
            o_ref[seq * tq:(seq + 1) * tq, p * LANES:(p + 1) * LANES] = (
                jnp.concatenate(halves, axis=0).T * sg_ref[...] * (1.0 - lam_init))

    pair_groups = [[p] for p in range(n_pair)] if has_ctx else [list(range(n_pair))]
    groups = [(seq, pairs) for seq in range(k_ref.shape[0]) for pairs in pair_groups]
    pending = front(*groups[0])
    for nxt in groups[1:]:
        ahead = front(*nxt)
        back(*pending)
        pending = ahead
    back(*pending)


def _attention(q, k_all, v_all, layer, n_sub, lam_init, lp, sg, ctx=None):
    n_tok = q.shape[0]
    n_seq, _, seq_len, _ = k_all.shape
    tq = 256
    nq = seq_len // tq
    assert n_sub == 1 or nq == 1
    kv_spec = pl.BlockSpec((n_sub, None, seq_len, D_DIFF), lambda b, i: (b, layer, 0, 0))
    in_specs = [pl.BlockSpec((n_sub * tq, D_DIFF), lambda b, i: (b * nq + i, 0)),
                kv_spec, kv_spec,
                pl.BlockSpec(lp.shape, lambda b, i: (0, 0)),
                pl.BlockSpec((1, LANES), lambda b, i: (0, 0))]
    args = [q, k_all, v_all, lp, sg]
    if ctx is not None:
        kc, vc, cos, sin = ctx
        past = kc.shape[1]
        in_specs += [pl.BlockSpec((1, past, D_DIFF), lambda b, i: (b, 0, 0)),
                     pl.BlockSpec((1, past, D_DIFF), lambda b, i: (b, 0, 0)),
                     pl.BlockSpec((tq, LANES), lambda b, i: (i, 0)),
                     pl.BlockSpec((tq, LANES), lambda b, i: (i, 0)),
                     pl.BlockSpec((seq_len, LANES), lambda b, i: (0, 0)),
                     pl.BlockSpec((seq_len, LANES), lambda b, i: (0, 0))]
        args += [kc, vc, cos, sin, cos, sin]
    return pl.pallas_call(
        functools.partial(_attn_body, ctx is not None, lam_init),
        grid=(n_seq // n_sub, nq),
        in_specs=in_specs,
        out_specs=pl.BlockSpec((n_sub * tq, D_DIFF), lambda b, i: (b * nq + i, 0)),
        out_shape=jax.ShapeDtypeStruct((n_tok, D_DIFF), F32),
        compiler_params=_cp(2),
    )(*args)


def _rope_tables(seq_len):
    t = jnp.arange(seq_len)
    pos = jnp.stack([(t // GRID_W).astype(F32), (t % GRID_W).astype(F32)], axis=1)
    inv = 1.0 / (ROPE_BASE ** (jnp.arange(ROPE_PAIRS, dtype=F32) / ROPE_PAIRS))
    ang = pos[:, :, None] * inv
    d = np.arange(LANES) % D_QK
    axis = d // (2 * ROPE_PAIRS)
    second = (d % (2 * ROPE_PAIRS)) // ROPE_PAIRS
    idx = d % ROPE_PAIRS
    cos = jnp.cos(ang)[:, axis, idx]
    sin = jnp.sin(ang)[:, axis, idx] * jnp.asarray(np.where(second == 1, 1.0, -1.0), F32)
    return cos, sin


def _fnet_body(seq_len, x_ref, ct_ref, st_ref, cc_ref, sc_ref, o_ref):
    n_sub = x_ref.shape[0] // seq_len
    x = x_ref[...].astype(BF16)
    xc = jnp.dot(x, cc_ref[...], preferred_element_type=F32)
    xs = jnp.dot(x, sc_ref[...], preferred_element_type=F32)
    wide = lambda a: jnp.concatenate([a[s * seq_len:(s + 1) * seq_len] for s in range(n_sub)], axis=1)
    y = _bdot(ct_ref[...], wide(xc)) - _bdot(st_ref[...], wide(xs))
    for s in range(n_sub):
        o_ref[s * seq_len:(s + 1) * seq_len, :] = y[:, s * D_FNET:(s + 1) * D_FNET]


def _dft_consts(n, block=1):
    idx = np.arange(n)
    ang = 2.0 * np.pi * ((idx[:, None] * idx[None, :]) % n) / n
    return [jnp.asarray(np.kron(np.eye(block), m).astype(np.float32)).astype(BF16)
            for m in (np.cos(ang) / np.sqrt(n), np.sin(ang) / np.sqrt(n))]


def _fnet(u_f, seq_len, n_sub):
    n_tok = u_f.shape[0]
    rows = n_sub * seq_len
    consts = _dft_consts(seq_len) + _dft_consts(FNET_GROUP_DIM, FNET_GROUPS)
    const = lambda b: (0, 0)
    return pl.pallas_call(
        functools.partial(_fnet_body, seq_len),
        grid=(n_tok // rows,),
        in_specs=[pl.BlockSpec((rows, D_FNET), lambda b: (b, 0))]
        + [pl.BlockSpec(c.shape, const) for c in consts],
        out_specs=pl.BlockSpec((rows, D_FNET), lambda b: (b, 0)),
        out_shape=jax.ShapeDtypeStruct((n_tok, D_FNET), F32),
        compiler_params=_cp(1),
    )(u_f, *consts)


def _ffn_body(final, yr_ref, yd_ref, yf_ref, x_ref, g1_ref, sh2_ref, sc2_ref, g2_ref, n2_ref, fg_ref,
              wo_ref, wi_ref, wf_ref, o_ref):
    part = x_ref.shape[0] // N_PARTS
    parts = [slice(j * part, (j + 1) * part) for j in range(N_PARTS)]
    y = [_bdot(yr_ref[r, :], wo_ref[0:D_RWKV, :])
         + _bdot(yd_ref[r, :], wo_ref[D_RWKV:D_RWKV + D_DIFF, :])
         + _bdot(yf_ref[r, :], wo_ref[D_RWKV + D_DIFF:, :]) for r in parts]
    x = [x_ref[r, :] + g1_ref[0] * y[j] for j, r in enumerate(parts)]
    h = [(_rms(xj, n2_ref[...]) * (1.0 + sc2_ref[0]) + sh2_ref[0]).astype(BF16) for xj in x]
    z = [jnp.dot(hj, wi_ref[...], preferred_element_type=F32) for hj in h]
    act = [zj[:, :D_FF] * jax.nn.sigmoid(zj[:, :D_FF]) * zj[:, D_FF:] for zj in z]
    f = [_bdot(aj, wf_ref[...]) for aj in act]
    for j, r in enumerate(parts):
        xj = x[j] + g2_ref[0] * f[j]
        o_ref[r, :] = _rms(xj, fg_ref[...]) if final else xj


def _ffn(y_r, y_d, y_f, x, mod, layer, row_fn, n2, fg, wo, wi, wf, final, tm):
    n_tok = x.shape[0]
    row = lambda i: (i, 0)
    const = lambda i: (0, 0)
    return pl.pallas_call(
        functools.partial(_ffn_body, final),
        grid=(n_tok // tm,),
        in_specs=[pl.BlockSpec((tm, D_RWKV), row),
                  pl.BlockSpec((tm, D_DIFF), row),
                  pl.BlockSpec((tm, D_FNET), row),
                  pl.BlockSpec((tm, D_MODEL), row),
                  _mod_spec(layer, 2, row_fn),
                  _mod_spec(layer, 3, row_fn),
                  _mod_spec(layer, 4, row_fn),
                  _mod_spec(layer, 5, row_fn),
                  pl.BlockSpec((1, D_MODEL), const),
                  pl.BlockSpec((1, D_MODEL), const),
                  pl.BlockSpec((None,) + wo.shape[1:], lambda i: (layer, 0, 0), pipeline_mode=pl.Buffered(1)),
                  pl.BlockSpec((None,) + wi.shape[1:], lambda i: (layer, 0, 0), pipeline_mode=pl.Buffered(1)),
                  pl.BlockSpec((None,) + wf.shape[1:], lambda i: (layer, 0, 0), pipeline_mode=pl.Buffered(1))],
        out_specs=pl.BlockSpec((tm, D_MODEL), row),
        out_shape=jax.ShapeDtypeStruct((n_tok, D_MODEL), F32),
        compiler_params=_cp(1),
    )(y_r, y_d, y_f, x, mod, mod, mod, mod, n2.reshape(1, D_MODEL), fg.reshape(1, D_MODEL), wo, wi, wf)


def _block_diag_state(s):
    b = s.shape[0]
    s = s.reshape(b, 2, N_PAIR, 2, HEAD_DIM, HEAD_DIM)
    eye = jnp.eye(2, dtype=s.dtype)
    s = s[:, :, :, :, :, None, :] * eye[None, None, None, :, None, :, None]
    return s.reshape(b, 2, N_PAIR, LANES, LANES)


def kernel(x_prompt, x_sample, c, state_rwkv, cache_diff_k, cache_diff_v, c_ctx, norm1_g, norm2_g, final_norm_g, w_mod, b_mod, w_in, w_out, shift_mu, decay_w0, decay_up, iclr_a0, iclr_up, gate_up, k_k, k_a, r_k, lnx_g, lnx_b, diff_lambda, subln_g, w_ffn_in, w_ffn_out):
    p = dict(shift_mu=shift_mu, decay_w0=decay_w0, decay_up=decay_up, iclr_a0=iclr_a0, iclr_up=iclr_up,
             gate_up=gate_up, k_k=k_k, k_a=k_a, r_k=r_k, lnx_g=lnx_g, lnx_b=lnx_b)
    n_ctx, t_ctx, _ = x_prompt.shape
    n_dec, t_dec, _ = x_sample.shape
    past = cache_diff_k.shape[2]

    cond = jnp.concatenate([c_ctx[None, :], c, jnp.zeros((MOD_ROWS - 1 - n_dec, D_MODEL), F32)], axis=0)
    mod = _modulation(cond, w_mod, b_mod).reshape(DEPTH * MOD_ROWS, 1, 6 * D_MODEL)

    tm_ffn = 512
    streams = [
        dict(x=x_prompt.reshape(n_ctx * t_ctx, D_MODEL), t=t_ctx, n=n_ctx, n_sub=4, n_attn=2, tm_in=2 * t_ctx,
             row_in=lambda i: 0, row_ffn=lambda i: 0),
        dict(x=x_sample.reshape(n_dec * t_dec, D_MODEL), t=t_dec, n=n_dec, n_sub=1, n_attn=1, tm_in=t_dec,
             row_in=lambda i: 1 + i, row_ffn=lambda i: 1 + i // (t_dec // tm_ffn)),
    ]
    cos, sin = _rope_tables(t_dec)
    for st in streams:
        st['kv'] = st['states'] = None
    w_in_l = jnp.concatenate(
        [w_in[:, :, :N_RWKV_IN].astype(BF16), jnp.zeros((DEPTH, D_MODEL, N_RWKV_PAD - N_RWKV_IN), BF16),
         w_in[:, :, N_RWKV_IN:].astype(BF16)], axis=2)
    wo = w_out.astype(BF16)
    wi = w_ffn_in.astype(BF16)
    wf = w_ffn_out.astype(BF16)
    for l in range(DEPTH):
        rw = _rwkv_weights(p, l)
        mu = jnp.concatenate([shift_mu[l], jnp.zeros((2, N_RWKV_PAD - N_RWKV_IN), F32)], axis=1)
        lam_init = 0.8 - 0.6 * math.exp(-0.3 * l)
        sg = jnp.tile(subln_g[l], 2).reshape(1, LANES)
        for si, st in enumerate(streams):
            u_r, q, k_all, v_all, u_f = _inproj(st['x'], mod, l, st['row_in'], norm1_g[l], w_in_l, mu,
                                                 st['kv'], st['t'], st['tm_in'])
            st['kv'] = (k_all, v_all)
            if si == 0:
                s0 = None
                attn_ctx = None
            else:
                s0 = state_rwkv[:, l].astype(F32)
                attn_ctx = (cache_diff_k[:, l].reshape(n_dec, past, D_DIFF).astype(F32),
                            cache_diff_v[:, l].reshape(n_dec, past, D_DIFF).astype(F32), cos, sin)
            y_r, st['states'] = _rwkv(u_r, s0, st['t'], st['n_sub'], rw, st['states'], l)
            y_d = _attention(q, k_all, v_all, l, st['n_attn'], lam_init, diff_lambda[l], sg, attn_ctx)
            y_f = _fnet(u_f, st['t'], st['n_sub'])
            st['x'] = _ffn(y_r, y_d, y_f, st['x'], mod, l, st['row_ffn'], norm2_g[l], final_norm_g,
                           wo, wi, wf, l == DEPTH - 1, tm_ffn)
    y_prompt = streams[0]['x'].reshape(n_ctx, t_ctx, D_MODEL)
    y_sample = streams[1]['x'].reshape(n_dec, t_dec, D_MODEL)
    new_k = streams[0]['kv'][0].reshape(n_ctx, DEPTH, t_ctx, H_DIFF, 2, D_QK)
    new_v = streams[0]['kv'][1].reshape(n_ctx, DEPTH, t_ctx, H_DIFF, HEAD_DIM)
    return (y_prompt, y_sample, streams[0]['states'], new_k, new_v)
```

```python
import functools
import math

import numpy as np
import jax
import jax.numpy as jnp
from jax import lax
from jax.experimental import pallas as pl
from jax.experimental.pallas import tpu as pltpu

F32 = jnp.float32
BF16 = jnp.bfloat16

D_MODEL = 1024
DEPTH = 2
GRID_W = 64
HEAD_DIM = 64
D_RWKV = 384
H_RWKV = D_RWKV // HEAD_DIM
D_DIFF = 384
H_DIFF = D_DIFF // HEAD_DIM
D_QK = HEAD_DIM // 2
D_FNET = D_MODEL - D_RWKV - D_DIFF
FNET_GROUPS = 4
FNET_GROUP_DIM = D_FNET // FNET_GROUPS
LORA_W = 32
LORA_A = 32
LORA_G = 64
N_RWKV_IN = 3 * D_RWKV + 2 * LORA_W + 2 * LORA_A + LORA_G
N_DIFF_IN = 3 * D_DIFF
D_FF = ((8 * D_MODEL + 3 * 256 - 1) // (3 * 256)) * 256
ROPE_PAIRS = D_QK // 4
ROPE_BASE = 10000.0
RMS_EPS = 1e-6
GN_EPS = 64e-5
SUBLN_EPS = 1e-5
DECAY_SCALE = math.exp(-0.5)

LANES = 128
N_RWKV_PAD = 11 * LANES
D_IN_PAD = N_RWKV_PAD + N_DIFF_IN + D_FNET
N_PAIR = H_RWKV // 2
CHUNK = 64
N_PARTS = 2
SIDE_WORK_MAX_UNITS = 12
MOD_ROWS = 8
ONES_ROWS = 16
VMEM_LIMIT = 60 * 1024 * 1024


def _cp(n_axes=1):
    return pltpu.CompilerParams(dimension_semantics=("arbitrary",) * n_axes,
                                vmem_limit_bytes=VMEM_LIMIT)


def _bdot(a, b):
    return jnp.dot(a.astype(BF16), b.astype(BF16), preferred_element_type=F32)


def _bdot_nt(a, b):
    return lax.dot_general(a.astype(BF16), b.astype(BF16), (((1,), (1,)), ((), ())),
                           preferred_element_type=F32)


def _split2(x):
    hi = x.astype(BF16)
    lo = (x - hi.astype(F32)).astype(BF16)
    return hi, lo


def _dot_x3(a, b):
    a_hi, a_lo = _split2(a)
    b_hi, b_lo = _split2(b)
    d = functools.partial(jnp.dot, preferred_element_type=F32)
    return d(a_hi, b_hi) + d(a_lo, b_hi) + d(a_hi, b_lo)


def _dot_exact_rhs(c_bf16, x):
    d = functools.partial(jnp.dot, preferred_element_type=F32)
    hi, lo = _split2(x)
    return d(c_bf16, hi) + d(c_bf16, lo)


def _rms(x, g):
    return x * lax.rsqrt(jnp.mean(x * x, axis=-1, keepdims=True) + RMS_EPS) * g


def _mod_body(c_ref, w_ref, b_ref, o_ref):
    c = c_ref[...]
    a = c * jax.nn.sigmoid(c)
    o_ref[0] = _dot_x3(a, w_ref[0]) + b_ref[0]


def _modulation(cond, w_mod, b_mod):
    n_layers, _, n_out = w_mod.shape
    tn = 1536
    return pl.pallas_call(
        _mod_body,
        grid=(n_layers, n_out // tn),
        in_specs=[pl.BlockSpec((MOD_ROWS, D_MODEL), lambda l, j: (0, 0)),
                  pl.BlockSpec((1, D_MODEL, tn), lambda l, j: (l, 0, j)),
                  pl.BlockSpec((1, 1, tn), lambda l, j: (l, 0, j))],
        out_specs=pl.BlockSpec((1, MOD_ROWS, tn), lambda l, j: (l, 0, j)),
        out_shape=jax.ShapeDtypeStruct((n_layers, MOD_ROWS, n_out), F32),
        compiler_params=_cp(2),
    )(cond, w_mod, b_mod.reshape(n_layers, 1, n_out))


def _mod_spec(layer, col, row_fn):
    return pl.BlockSpec((1, 1, D_MODEL), lambda i: (layer * MOD_ROWS + row_fn(i), 0, col))


def _put_layer(ref, idx, layer, aliased, val):
    if aliased:
        ref[idx] = val
    else:
        for other in range(DEPTH):
            ref[idx + (other,)] = val if other == layer else jnp.zeros_like(val)


def _inproj_body(seq_len, layer, aliased, x_ref, g_ref, sh_ref, sc_ref, w_ref, mu_ref, *refs):
    ur_ref, q_ref, k_ref, v_ref, uf_ref = refs[2:] if aliased else refs
    part = x_ref.shape[0] // N_PARTS
    parts = [slice(j * part, (j + 1) * part) for j in range(N_PARTS)]
    h = [(_rms(x_ref[r, :], g_ref[...]) * (1.0 + sc_ref[0]) + sh_ref[0]).astype(BF16) for r in parts]
    u = jnp.concatenate([jnp.dot(hj, w_ref[...], preferred_element_type=F32) for hj in h], axis=0)
    ur = u[:, :N_RWKV_PAD]
    tm = ur.shape[0]
    pos = lax.broadcasted_iota(jnp.int32, (tm, 1), 0) & (seq_len - 1)
    prev = jnp.where(pos == 0, 0.0, pltpu.roll(ur, 1, axis=0))
    nxt = jnp.where(pos == seq_len - 1, 0.0, pltpu.roll(ur, tm - 1, axis=0))
    ur_ref[...] = ur + mu_ref[0:1, :] * (prev - ur) + mu_ref[1:2, :] * (nxt - ur)
    q_ref[...] = u[:, N_RWKV_PAD:N_RWKV_PAD + D_DIFF]
    for s in range(tm // seq_len):
        rows = slice(s * seq_len, (s + 1) * seq_len)
        _put_layer(k_ref, (s,), layer, aliased, u[rows, N_RWKV_PAD + D_DIFF:N_RWKV_PAD + 2 * D_DIFF])
        _put_layer(v_ref, (s,), layer, aliased, u[rows, N_RWKV_PAD + 2 * D_DIFF:N_RWKV_PAD + 3 * D_DIFF])
    uf_ref[...] = u[:, N_RWKV_PAD + N_DIFF_IN:]


def _inproj(x, mod, layer, row_fn, g, w, mu, kv, seq_len, tm):
    n_tok = x.shape[0]
    kv_shape = jax.ShapeDtypeStruct((n_tok // seq_len, DEPTH, seq_len, D_DIFF), F32)
    assert tm % seq_len == 0 and seq_len & (seq_len - 1) == 0
    row = lambda i: (i, 0)
    const = lambda i: (0, 0)
    if kv is None:
        kv_spec = pl.BlockSpec((tm // seq_len, DEPTH, seq_len, D_DIFF), lambda i: (i, 0, 0, 0))
    else:
        kv_spec = pl.BlockSpec((tm // seq_len, None, seq_len, D_DIFF), lambda i: (i, layer, 0, 0))
    return pl.pallas_call(
        functools.partial(_inproj_body, seq_len, layer, kv is not None),
        grid=(n_tok // tm,),
        in_specs=[pl.BlockSpec((tm, D_MODEL), row),
                  pl.BlockSpec((1, D_MODEL), const),
                  _mod_spec(layer, 0, row_fn),
                  _mod_spec(layer, 1, row_fn),
                  pl.BlockSpec((None, D_MODEL, D_IN_PAD), lambda i: (layer, 0, 0)),
                  pl.BlockSpec((2, N_RWKV_PAD), const)]
        + [pl.BlockSpec(memory_space=pl.ANY)] * (0 if kv is None else 2),
        out_specs=[pl.BlockSpec((tm, N_RWKV_PAD), row),
                   pl.BlockSpec((tm, D_DIFF), row),
                   kv_spec, kv_spec,
                   pl.BlockSpec((tm, D_FNET), row)],
        out_shape=[jax.ShapeDtypeStruct((n_tok, N_RWKV_PAD), F32),
                   jax.ShapeDtypeStruct((n_tok, D_DIFF), F32),
                   kv_shape, kv_shape,
                   jax.ShapeDtypeStruct((n_tok, D_FNET), F32)],
        input_output_aliases={} if kv is None else {6: 2, 7: 3},
        compiler_params=_cp(1),
    )(x, g.reshape(1, D_MODEL), mod, mod, w, mu, *(kv or ()))


def _lane_masks():
    lane = lax.broadcasted_iota(jnp.int32, (1, LANES), 1)
    return lane < HEAD_DIM, lane >= HEAD_DIM


def _rwkv_units(units, m0, m1, side=()):
    side = list(side)

    def run_side(drain=False):
        for gen in list(side):
            for _ in gen:
                if not drain:
                    break
            else:
                side.remove(gen)

    def bd(x):
        xb = x.astype(BF16)
        zero = jnp.zeros_like(xb)
        return jnp.concatenate([jnp.where(m0, xb, zero), jnp.where(m1, xb, zero)], axis=0)

    row = lax.broadcasted_iota(jnp.int32, (CHUNK, LANES), 0)
    col = lax.broadcasted_iota(jnp.int32, (CHUNK, LANES), 1) & (CHUNK - 1)
    eye = (col == row).astype(F32)
    r2 = lax.broadcasted_iota(jnp.int32, (LANES, LANES), 0) < HEAD_DIM
    c2 = lax.broadcasted_iota(jnp.int32, (LANES, LANES), 1) < HEAD_DIM
    rng = range(len(units))

    pre = []
    for rev, kk, r, v, kd, b, cum, cex, tot, s_prev in units:
        p_inv = jnp.exp(-cum)
        p_rem = jnp.exp(tot - cum)
        ab = -kk * jnp.exp(cex)
        rb = r * jnp.exp(cum)
        strict = (col > row) if rev else (col < row)
        incl = (col >= row) if rev else (col <= row)
        pre.append(dict(ab=ab, rb=rb, vbd=bd(v), strict=strict, incl=incl,
                        lhs=jnp.concatenate([ab, rb], axis=0),
                        rhs=jnp.concatenate([bd(b * p_inv), bd(kd * p_inv)], axis=0),
                        bk=jnp.concatenate([b * p_rem, kd * p_rem], axis=0)))

    run_side()
    mm = [_bdot_nt(q['lhs'], q['rhs']) for q in pre]
    run_side()
    m_ab = [jnp.where(pre[i]['strict'], mm[i][:CHUNK, :LANES], 0.0) for i in rng]
    m_ak = [jnp.where(pre[i]['strict'], mm[i][:CHUNK, LANES:], 0.0) for i in rng]
    m_r = [jnp.concatenate([jnp.where(pre[i]['incl'], mm[i][CHUNK:, :LANES], 0.0),
                            jnp.where(pre[i]['incl'], mm[i][CHUNK:, LANES:], 0.0)], axis=1) for i in rng]
    mv = [_bdot(m_ak[i], pre[i]['vbd']) for i in rng]
    run_side()

    t = [eye + m_ab[i] for i in rng]
    n = [_bdot(m_ab[i], bd(m_ab[i])) for i in rng]
    for _ in range(4):
        x = [_bdot(jnp.concatenate([t[i], n[i]], axis=0), bd(n[i])) for i in rng]
        t = [t[i] + x[i][:CHUNK] for i in rng]
        n = [x[i][CHUNK:] for i in rng]
        run_side()
    t = [t[i] + _bdot(t[i], bd(n[i])) for i in rng]

    w = [_bdot(t[i], jnp.concatenate([bd(pre[i]['ab']), bd(mv[i])], axis=1)) for i in rng]
    xs = [_bdot_nt(jnp.concatenate([w[i][:, :LANES], pre[i]['rb']], axis=0), units[i][9]) for i in rng]
    u = [xs[i][:CHUNK] + w[i][:, LANES:] for i in rng]
    run_side(drain=True)
    y = [xs[i][CHUNK:] + _bdot(m_r[i], jnp.concatenate([bd(u[i]), pre[i]['vbd']], axis=0)) for i in rng]
    z = [_bdot(jnp.concatenate([u[i], units[i][3]], axis=0).T, pre[i]['bk']) for i in rng]
    s_new = [units[i][9] * jnp.exp(units[i][8]) + jnp.where(r2 == c2, z[i], 0.0) for i in rng]
    return y, s_new


def _rwkv_body(seq_len, n_sub, layer, has_s0, aliased, u_ref, *refs):
    s0_ref = refs[0] if has_s0 else None
    refs = refs[1:] if has_s0 else refs
    wdec_ref, wicl_ref, wg_ref, w0a0_ref, vec_ref, bo_ref, tril_ref, triu_ref = refs[:8]
    (y_ref, sfin_ref, r_s, v_s, kk_s, kd_s, b_s, ci_s, ce_s, y_s, st_s) = refs[9 if aliased else 8:]
    n_chunk = seq_len // CHUNK
    assert n_chunk % 2 == 0 and n_chunk >= 4
    k_k = vec_ref[0:1, :]
    k_a = vec_ref[1:2, :]
    r_k = vec_ref[2:3, :]
    lnx_g = vec_ref[3:4, :]
    lnx_b = vec_ref[4:5, :]
    bo = bo_ref[...]

    def headsum(xb):
        return jnp.concatenate([jnp.dot(xb[:, p * LANES:(p + 1) * LANES], bo, preferred_element_type=F32)
                                for p in range(N_PAIR)], axis=1)

    def prep(rows):
        xs = u_ref[rows, :]
        r = xs[:, 0:D_RWKV]
        k = xs[:, D_RWKV:2 * D_RWKV]
        v = xs[:, 2 * D_RWKV:3 * D_RWKV]
        lora = xs[:, 3 * D_RWKV:3 * D_RWKV + LANES]
        t_lora = jnp.tanh(lora).astype(BF16)
        lora = lora.astype(BF16)
        kk = k * k_k
        kk2 = (kk * kk).astype(BF16)
        yield
        dec = _bdot(t_lora, wdec_ref[...])
        icl = _bdot(lora, wicl_ref[...])
        ss = headsum(kk2)
        yield
        logw = -DECAY_SCALE * jax.nn.sigmoid(w0a0_ref[0:1, :] + dec)
        a = jax.nn.sigmoid(w0a0_ref[1:2, :] + icl)
        kk = kk / jnp.maximum(jnp.sqrt(ss), 1e-12)
        a_f = a[:, :D_RWKV]
        a_b = a[:, D_RWKV:]
        kd_f = k * (1.0 + (a_f - 1.0) * k_a)
        kd_b = k * (1.0 + (a_b - 1.0) * k_a)
        lws = [_split2(logw[:, d * D_RWKV:(d + 1) * D_RWKV]) for d in range(2)]
        r_s[rows, :] = r
        v_s[rows, :] = v
        kk_s[rows, :] = kk
        kd_s[0, rows, :] = kd_f
        kd_s[1, rows, :] = kd_b
        b_s[0, rows, :] = kk * a_f
        b_s[1, rows, :] = kk * a_b
        yield
        dd = functools.partial(jnp.dot, preferred_element_type=F32)
        cums = [dd(tri_ref[...], lws[d][0]) + dd(tri_ref[...], lws[d][1])
                for d, tri_ref in enumerate((tril_ref, triu_ref))]
        yield
        for d in range(2):
            ci_s[d, rows, :] = cums[d]
            ce_s[d, rows, :] = cums[d] - logw[:, d * D_RWKV:(d + 1) * D_RWKV]

    def post(rows):
        y = y_s[rows, :]
        yb = y.astype(BF16)
        v = v_s[rows, :]
        bon = _split2(r_s[rows, :] * (kd_s[0, rows, :] + kd_s[1, rows, :]) * r_k)
        s_gd = jax.nn.sigmoid(u_ref[rows, 3 * D_RWKV + LANES:3 * D_RWKV + 2 * LANES]).astype(BF16)
        yield
        mean = headsum(yb) * (1.0 / HEAD_DIM)
        bonus = (headsum(bon[0]) + headsum(bon[1])) * v
        g = _bdot(s_gd, wg_ref[...])
        yield
        yc = y - mean
        yc2 = (yc * yc).astype(BF16)
        yield
        var = headsum(yc2) * (1.0 / HEAD_DIM)
        yield
        yn = yc * lax.rsqrt(var + GN_EPS) * lnx_g + lnx_b
        y_ref[rows, :] = (yn + bonus) * g

    def run_all(gens):
        gens = list(gens)
        while gens:
            gens = [gen for gen in gens if next(gen, gens) is not gens]

    def chunk_rows(i):
        return [pl.ds(pl.multiple_of(s * seq_len + (i if d == 0 else n_chunk - 1 - i) * CHUNK, CHUNK), CHUNK)
                for s in range(n_sub) for d in range(2)]

    st_s[...] = s0_ref[...] if has_s0 else jnp.zeros_like(st_s)
    y_s[...] = jnp.zeros_like(y_s)
    m0, m1 = _lane_masks()

    def scan_step(i, prep_next, post_prev):
        rows_sd = chunk_rows(i)
        side = []
        if prep_next:
            side = [prep(rows) for rows in chunk_rows(i + 1)]
        if post_prev:
            side = [post(rows) for rows in chunk_rows(i - 1)]
        units = []
        for s in range(n_sub):
            for d in range(2):
                rows = rows_sd[s * 2 + d]
                cum = ci_s[d, rows, :]
                cex = ce_s[d, rows, :]
                tot = cum[CHUNK - 1:CHUNK] if d == 0 else cum[0:1]
                kk = kk_s[rows, :]
                r = r_s[rows, :]
                v = v_s[rows, :]
                kd = kd_s[d, rows, :]
                b = b_s[d, rows, :]
                for p in range(N_PAIR):
                    sl = slice(p * LANES, (p + 1) * LANES)
                    units.append((d == 1, kk[:, sl], r[:, sl], v[:, sl], kd[:, sl], b[:, sl],
                                  cum[:, sl], cex[:, sl], tot[:, sl], st_s[s, d, p]))
        ys, s_new = _rwkv_units(units, m0, m1, side)
        for s in range(n_sub):
            for d in range(2):
                base = (s * 2 + d) * N_PAIR
                for p in range(N_PAIR):
                    st_s[s, d, p] = s_new[base + p]
                rows = rows_sd[s * 2 + d]
                y_s[rows, :] = y_s[rows, :] + jnp.concatenate(ys[base:base + N_PAIR], axis=1)

    def loop(lo, hi, **kw):
        lax.fori_loop(lo, hi, lambda i, c: (scan_step(i, **kw), c)[1], 0)

    half = n_chunk // 2
    if 2 * N_PAIR * n_sub <= SIDE_WORK_MAX_UNITS:
        run_all(prep(rows) for rows in chunk_rows(0))
        loop(0, half - 1, prep_next=True, post_prev=False)
        loop(half - 1, half + 1, prep_next=False, post_prev=False)
        loop(half + 1, n_chunk, prep_next=False, post_prev=True)
        run_all(post(rows) for rows in chunk_rows(n_chunk - 1))
    else:
        for i in range(half):
            run_all(prep(rows) for rows in chunk_rows(i))
        loop(0, n_chunk, prep_next=False, post_prev=False)
        for i in range(half):
            run_all(post(rows) for rows in chunk_rows(i))

    for s in range(n_sub):
        for d in range(2):
            for p in range(N_PAIR):
                st = st_s[s, d, p]
                for h in range(2):
                    rows = slice(h * HEAD_DIM, (h + 1) * HEAD_DIM)
                    val = st[rows, rows]
                    if aliased:
                        sfin_ref[s, d, 2 * p + h] = val
                    else:
                        for other in range(DEPTH):
                            sfin_ref[s, other, d, 2 * p + h] = val if other == layer else jnp.zeros_like(val)


def _rwkv(u_r, s0, seq_len, n_sub, wts, states, layer):
    n_seq = u_r.shape[0] // seq_len
    rows = n_sub * seq_len
    const2 = lambda b: (0, 0)
    st_shape = (n_sub, 2, N_PAIR, LANES, LANES)
    tok = pltpu.VMEM((rows, D_RWKV), F32)
    tok2 = pltpu.VMEM((2, rows, D_RWKV), F32)
    single = n_seq == n_sub
    in_specs = [pl.BlockSpec((rows, N_RWKV_PAD), lambda b: (b, 0),
                             pipeline_mode=pl.Buffered(1) if single else None)]
    args = [u_r]
    if s0 is not None:
        in_specs.append(pl.BlockSpec(st_shape, lambda b: (b, 0, 0, 0, 0)))
        args.append(_block_diag_state(s0))
    in_specs += [pl.BlockSpec(w.shape, const2) for w in wts]
    args += list(wts)
    if states is not None:
        in_specs.append(pl.BlockSpec(memory_space=pl.ANY))
        args.append(states)
    return pl.pallas_call(
        functools.partial(_rwkv_body, seq_len, n_sub, layer, s0 is not None, states is not None),
        grid=(n_seq // n_sub,),
        in_specs=in_specs,
        out_specs=[pl.BlockSpec((rows, D_RWKV), lambda b: (b, 0),
                                pipeline_mode=pl.Buffered(1) if single else None),
                   pl.BlockSpec((n_sub, DEPTH, 2, H_RWKV, HEAD_DIM, HEAD_DIM), lambda b: (b, 0, 0, 0, 0, 0))
                   if states is None else
                   pl.BlockSpec((n_sub, None, 2, H_RWKV, HEAD_DIM, HEAD_DIM),
                                lambda b: (b, layer, 0, 0, 0, 0))],
        out_shape=[jax.ShapeDtypeStruct((n_seq * seq_len, D_RWKV), F32),
                   jax.ShapeDtypeStruct((n_seq, DEPTH, 2, H_RWKV, HEAD_DIM, HEAD_DIM), F32)],
        scratch_shapes=[tok] * 3 + [tok2] * 4 + [tok, pltpu.VMEM(st_shape, F32)],
        input_output_aliases={} if states is None else {len(args) - 1: 1},
        compiler_params=_cp(1),
    )(*args)


def _rwkv_weights(p, l):
    z = functools.partial(jnp.zeros, dtype=F32)
    wdec = z((LANES, 2 * D_RWKV))
    wdec = wdec.at[0:LORA_W, :D_RWKV].set(p['decay_up'][l, 0])
    wdec = wdec.at[LORA_W:2 * LORA_W, D_RWKV:].set(p['decay_up'][l, 1])
    wicl = z((LANES, 2 * D_RWKV))
    wicl = wicl.at[2 * LORA_W:2 * LORA_W + LORA_A, :D_RWKV].set(p['iclr_up'][l, 0])
    wicl = wicl.at[2 * LORA_W + LORA_A:2 * LORA_W + 2 * LORA_A, D_RWKV:].set(p['iclr_up'][l, 1])
    wg = z((LANES, D_RWKV)).at[0:LORA_G].set(p['gate_up'][l])
    w0a0 = jnp.stack([p['decay_w0'][l].reshape(-1), p['iclr_a0'][l].reshape(-1)])
    vec = jnp.stack([p['k_k'][l], p['k_a'][l], p['r_k'][l].reshape(-1), p['lnx_g'][l], p['lnx_b'][l],
                     z((D_RWKV,)), z((D_RWKV,)), z((D_RWKV,))])
    head = np.arange(LANES) // HEAD_DIM
    bo = jnp.asarray(head[:, None] == head[None, :], BF16)
    idx = np.arange(CHUNK)
    tril = jnp.asarray(idx[None, :] <= idx[:, None], BF16)
    triu = jnp.asarray(idx[None, :] >= idx[:, None], BF16)
    return [wdec.astype(BF16), wicl.astype(BF16), wg.astype(BF16), w0a0, vec, bo, tril, triu]


def _rope(x, cos, sin):
    lane = lax.broadcasted_iota(jnp.int32, (1, LANES), 1)
    first_half = (lane & ROPE_PAIRS) == 0
    partner = jnp.where(first_half, pltpu.roll(x, LANES - ROPE_PAIRS, axis=1),
                        pltpu.roll(x, ROPE_PAIRS, axis=1))
    return x * cos + partner * sin


def _attn_body(has_ctx, lam_init, *refs):
    if has_ctx:
        (q_ref, k_ref, v_ref, lp_ref, sg_ref, kc_ref, vc_ref, cq_ref, sq_ref, ck_ref, sk_ref,
         o_ref) = refs
    else:
        q_ref, k_ref, v_ref, lp_ref, sg_ref, o_ref = refs
    n_pair = D_DIFF // LANES
    lp = lp_ref[...]
    lam = (jnp.exp(jnp.sum(lp[0:1] * lp[1:2], axis=-1, keepdims=True))
           - jnp.exp(jnp.sum(lp[2:3] * lp[3:4], axis=-1, keepdims=True)) + lam_init)
    lane = lax.broadcasted_iota(jnp.int32, (1, LANES), 1)

    tq = q_ref.shape[0] // k_ref.shape[0]

    def front(seq, pairs):
        qs, ks, vts = {}, {}, {}
        for p in pairs:
            sl = slice(p * LANES, (p + 1) * LANES)
            q = q_ref[seq * tq:(seq + 1) * tq, sl]
            k = k_ref[seq, :, sl]
            v = v_ref[seq, :, sl]
            if has_ctx:
                q = _rope(q, cq_ref[...], sq_ref[...])
                k = _rope(k, ck_ref[...], sk_ref[...])
                k = jnp.concatenate([kc_ref[0, :, sl], k], axis=0)
                v = jnp.concatenate([vc_ref[0, :, sl], v], axis=0)
            qs[p] = q * (D_QK ** -0.5 * math.log2(math.e))
            ks[p] = k.astype(BF16)
            vt = v.T
            ones = jnp.ones((ONES_ROWS, v.shape[0]), F32)
            vts[p] = [jnp.concatenate([vt[h * HEAD_DIM:(h + 1) * HEAD_DIM], ones], axis=0).astype(BF16)
                      for h in range(2)]
        chains = [(p, h, m) for p in pairs for h in range(2) for m in range(2)]
        scores = []
        for p, h, m in chains:
            lo = h * HEAD_DIM + m * D_QK
            sel = (lane >= lo) & (lane < lo + D_QK)
            scores.append(_bdot_nt(ks[p], jnp.where(sel, qs[p], 0.0)))
        return seq, pairs, chains, scores, vts

    def back(seq, pairs, chains, scores, vts):
        es = [jnp.exp2(s - jnp.max(s, axis=0, keepdims=True)).astype(BF16) for s in scores]
        pv = [jnp.dot(vts[p][h], es[i], preferred_element_type=F32) for i, (p, h, m) in enumerate(chains)]
        for j, p in enumerate(pairs):
            halves = []
            for h in range(2):
                i = (j * 2 + h) * 2
                rows = slice(0, HEAD_DIM)
                o = (pv[i][rows] * (1.0 / pv[i][HEAD_DIM:HEAD_DIM + 1])
                     - lam * (pv[i + 1][rows] * (1.0 / pv[i + 1][HEAD_DIM:HEAD_DIM + 1])))
                ms = jnp.mean(o * o, axis=0, keepdims=True)
                halves.append(o * lax.rsqrt(ms + SUBLN_EPS))
            o_ref[seq * tq:(seq + 1) * tq, p * LANES:(p + 1) * LANES] = (
                jnp.concatenate(halves, axis=0).T * sg_ref[...] * (1.0 - lam_init))

    pair_groups = [[p] for p in range(n_pair)] if has_ctx else [list(range(n_pair))]
    groups = [(seq, pairs) for seq in range(k_ref.shape[0]) for pairs in pair_groups]
    pending = front(*groups[0])
    for nxt in groups[1:]:
        ahead = front(*nxt)
        back(*pending)
        pending = ahead
    back(*pending)


def _attention(q, k_all, v_all, layer, n_sub, lam_init, lp, sg, ctx=None):
    n_tok = q.shape[0]
    n_seq, _, seq_len, _ = k_all.shape
    tq = 256
    nq = seq_len // tq
    assert n_sub == 1 or nq == 1
    kv_spec = pl.BlockSpec((n_sub, None, seq_len, D_DIFF), lambda b, i: (b, layer, 0, 0))
    in_specs = [pl.BlockSpec((n_sub * tq, D_DIFF), lambda b, i: (b * nq + i, 0)),
                kv_spec, kv_spec,
                pl.BlockSpec(lp.shape, lambda b, i: (0, 0)),
                pl.BlockSpec((1, LANES), lambda b, i: (0, 0))]
    args = [q, k_all, v_all, lp, sg]
    if ctx is not None:
        kc, vc, cos, sin = ctx
        past = kc.shape[1]
        in_specs += [pl.BlockSpec((1, past, D_DIFF), lambda b, i: (b, 0, 0)),
                     pl.BlockSpec((1, past, D_DIFF), lambda b, i: (b, 0, 0)),
                     pl.BlockSpec((tq, LANES), lambda b, i: (i, 0)),
                     pl.BlockSpec((tq, LANES), lambda b, i: (i, 0)),
                     pl.BlockSpec((seq_len, LANES), lambda b, i: (0, 0)),
                     pl.BlockSpec((seq_len, LANES), lambda b, i: (0, 0))]
        args += [kc, vc, cos, sin, cos, sin]
    return pl.pallas_call(
        functools.partial(_attn_body, ctx is not None, lam_init),
        grid=(n_seq // n_sub, nq),
        in_specs=in_specs,
        out_specs=pl.BlockSpec((n_sub * tq, D_DIFF), lambda b, i: (b * nq + i, 0)),
        out_shape=jax.ShapeDtypeStruct((n_tok, D_DIFF), F32),
        compiler_params=_cp(2),
    )(*args)


def _rope_tables(seq_len):
    t = jnp.arange(seq_len)
    pos = jnp.stack([(t // GRID_W).astype(F32), (t % GRID_W).astype(F32)], axis=1)
    inv = 1.0 / (ROPE_BASE ** (jnp.arange(ROPE_PAIRS, dtype=F32) / ROPE_PAIRS))
    ang = pos[:, :, None] * inv
    d = np.arange(LANES) % D_QK
    axis = d // (2 * ROPE_PAIRS)
    second = (d % (2 * ROPE_PAIRS)) // ROPE_PAIRS
    idx = d % ROPE_PAIRS
    cos = jnp.cos(ang)[:, axis, idx]
    sin = jnp.sin(ang)[:, axis, idx] * jnp.asarray(np.where(second == 1, 1.0, -1.0), F32)
    return cos, sin


def _fnet_body(seq_len, x_ref, ct_ref, st_ref, cc_ref, sc_ref, o_ref):
    n_sub = x_ref.shape[0] // seq_len
    x = x_ref[...].astype(BF16)
    xc = jnp.dot(x, cc_ref[...], preferred_element_type=F32)
    xs = jnp.dot(x, sc_ref[...], preferred_element_type=F32)
    wide = lambda a: jnp.concatenate([a[s * seq_len:(s + 1) * seq_len] for s in range(n_sub)], axis=1)
    y = _bdot(ct_ref[...], wide(xc)) - _bdot(st_ref[...], wide(xs))
    for s in range(n_sub):
        o_ref[s * seq_len:(s + 1) * seq_len, :] = y[:, s * D_FNET:(s + 1) * D_FNET]


def _dft_consts(n, block=1):
    idx = np.arange(n)
    ang = 2.0 * np.pi * ((idx[:, None] * idx[None, :]) % n) / n
    return [jnp.asarray(np.kron(np.eye(block), m).astype(np.float32)).astype(BF16)
            for m in (np.cos(ang) / np.sqrt(n), np.sin(ang) / np.sqrt(n))]


def _fnet(u_f, seq_len, n_sub):
    n_tok = u_f.shape[0]
    rows = n_sub * seq_len
    consts = _dft_consts(seq_len) + _dft_consts(FNET_GROUP_DIM, FNET_GROUPS)
    const = lambda b: (0, 0)
    return pl.pallas_call(
        functools.partial(_fnet_body, seq_len),
        grid=(n_tok // rows,),
        in_specs=[pl.BlockSpec((rows, D_FNET), lambda b: (b, 0))]
        + [pl.BlockSpec(c.shape, const) for c in consts],
        out_specs=pl.BlockSpec((rows, D_FNET), lambda b: (b, 0)),
        out_shape=jax.ShapeDtypeStruct((n_tok, D_FNET), F32),
        compiler_params=_cp(1),
    )(u_f, *consts)


def _ffn_body(final, yr_ref, yd_ref, yf_ref, x_ref, g1_ref, sh2_ref, sc2_ref, g2_ref, n2_ref, fg_ref,
              wo_ref, wi_ref, wf_ref, o_ref):
    part = x_ref.shape[0] // N_PARTS
    parts = [slice(j * part, (j + 1) * part) for j in range(N_PARTS)]
    y = [_bdot(yr_ref[r, :], wo_ref[0:D_RWKV, :])
         + _bdot(yd_ref[r, :], wo_ref[D_RWKV:D_RWKV + D_DIFF, :])
         + _bdot(yf_ref[r, :], wo_ref[D_RWKV + D_DIFF:, :]) for r in parts]
    x = [x_ref[r, :] + g1_ref[0] * y[j] for j, r in enumerate(parts)]
    h = [(_rms(xj, n2_ref[...]) * (1.0 + sc2_ref[0]) + sh2_ref[0]).astype(BF16) for xj in x]
    z = [jnp.dot(hj, wi_ref[...], preferred_element_type=F32) for hj in h]
    act = [zj[:, :D_FF] * jax.nn.sigmoid(zj[:, :D_FF]) * zj[:, D_FF:] for zj in z]
    f = [_bdot(aj, wf_ref[...]) for aj in act]
    for j, r in enumerate(parts):
        xj = x[j] + g2_ref[0] * f[j]
        o_ref[r, :] = _rms(xj, fg_ref[...]) if final else xj


def _ffn(y_r, y_d, y_f, x, mod, layer, row_fn, n2, fg, wo, wi, wf, final, tm):
    n_tok = x.shape[0]
    row = lambda i: (i, 0)
    const = lambda i: (0, 0)
    return pl.pallas_call(
        functools.partial(_ffn_body, final),
        grid=(n_tok // tm,),
        in_specs=[pl.BlockSpec((tm, D_RWKV), row),
                  pl.BlockSpec((tm, D_DIFF), row),
                  pl.BlockSpec((tm, D_FNET), row),
                  pl.BlockSpec((tm, D_MODEL), row),
                  _mod_spec(layer, 2, row_fn),
                  _mod_spec(layer, 3, row_fn),
                  _mod_spec(layer, 4, row_fn),
                  _mod_spec(layer, 5, row_fn),
                  pl.BlockSpec((1, D_MODEL), const),
                  pl.BlockSpec((1, D_MODEL), const),
                  pl.BlockSpec((None,) + wo.shape[1:], lambda i: (layer, 0, 0), pipeline_mode=pl.Buffered(1)),
                  pl.BlockSpec((None,) + wi.shape[1:], lambda i: (layer, 0, 0), pipeline_mode=pl.Buffered(1)),
                  pl.BlockSpec((None,) + wf.shape[1:], lambda i: (layer, 0, 0), pipeline_mode=pl.Buffered(1))],
        out_specs=pl.BlockSpec((tm, D_MODEL), row),
        out_shape=jax.ShapeDtypeStruct((n_tok, D_MODEL), F32),
        compiler_params=_cp(1),
    )(y_r, y_d, y_f, x, mod, mod, mod, mod, n2.reshape(1, D_MODEL), fg.reshape(1, D_MODEL), wo, wi, wf)


def _block_diag_state(s):
    b = s.shape[0]
    s = s.reshape(b, 2, N_PAIR, 2, HEAD_DIM, HEAD_DIM)
    eye = jnp.eye(2, dtype=s.dtype)
    s = s[:, :, :, :, :, None, :] * eye[None, None, None, :, None, :, None]
    return s.reshape(b, 2, N_PAIR, LANES, LANES)


def kernel(x_prompt, x_sample, c, state_rwkv, cache_diff_k, cache_diff_v, c_ctx, norm1_g, norm2_g, final_norm_g, w_mod, b_mod, w_in, w_out, shift_mu, decay_w0, decay_up, iclr_a0, iclr_up, gate_up, k_k, k_a, r_k, lnx_g, lnx_b, diff_lambda, subln_g, w_ffn_in, w_ffn_out):
    p = dict(shift_mu=shift_mu, decay_w0=decay_w0, decay_up=decay_up, iclr_a0=iclr_a0, iclr_up=iclr_up,
             gate_up=gate_up, k_k=k_k, k_a=k_a, r_k=r_k, lnx_g=lnx_g, lnx_b=lnx_b)
    n_ctx, t_ctx, _ = x_prompt.shape
    n_dec, t_dec, _ = x_sample.shape
    past = cache_diff_k.shape[2]

    cond = jnp.concatenate([c_ctx[None, :], c, jnp.zeros((MOD_ROWS - 1 - n_dec, D_MODEL), F32)], axis=0)
    mod = _modulation(cond, w_mod, b_mod).reshape(DEPTH * MOD_ROWS, 1, 6 * D_MODEL)

    tm_ffn = 512
    streams = [
        dict(x=x_prompt.reshape(n_ctx * t_ctx, D_MODEL), t=t_ctx, n=n_ctx, n_sub=4, n_attn=2, n_fnet=4, tm_in=2 * t_ctx,
             row_in=lambda i: 0, row_ffn=lambda i: 0),
        dict(x=x_sample.reshape(n_dec * t_dec, D_MODEL), t=t_dec, n=n_dec, n_sub=2, n_attn=1, n_fnet=1, tm_in=t_dec,
             row_in=lambda i: 1 + i, row_ffn=lambda i: 1 + i // (t_dec // tm_ffn)),
    ]
    cos, sin = _rope_tables(t_dec)
    for st in streams:
        st['kv'] = st['states'] = None
    w_in_l = jnp.concatenate(
        [w_in[:, :, :N_RWKV_IN].astype(BF16), jnp.zeros((DEPTH, D_MODEL, N_RWKV_PAD - N_RWKV_IN), BF16),
         w_in[:, :, N_RWKV_IN:].astype(BF16)], axis=2)
    wo = w_out.astype(BF16)
    wi = w_ffn_in.astype(BF16)
    wf = w_ffn_out.astype(BF16)
    for l in range(DEPTH):
        rw = _rwkv_weights(p, l)
        mu = jnp.concatenate([shift_mu[l], jnp.zeros((2, N_RWKV_PAD - N_RWKV_IN), F32)], axis=1)
        lam_init = 0.8 - 0.6 * math.exp(-0.3 * l)
        sg = jnp.tile(subln_g[l], 2).reshape(1, LANES)
        for si, st in enumerate(streams):
            u_r, q, k_all, v_all, u_f = _inproj(st['x'], mod, l, st['row_in'], norm1_g[l], w_in_l, mu,
                                                 st['kv'], st['t'], st['tm_in'])
            st['kv'] = (k_all, v_all)
            if si == 0:
                s0 = None
                attn_ctx = None
            else:
                s0 = state_rwkv[:, l].astype(F32)
                attn_ctx = (cache_diff_k[:, l].reshape(n_dec, past, D_DIFF).astype(F32),
                            cache_diff_v[:, l].reshape(n_dec, past, D_DIFF).astype(F32), cos, sin)
            y_r, st['states'] = _rwkv(u_r, s0, st['t'], st['n_sub'], rw, st['states'], l)
            y_d = _attention(q, k_all, v_all, l, st['n_attn'], lam_init, diff_lambda[l], sg, attn_ctx)
            y_f = _fnet(u_f, st['t'], st['n_fnet'])
            st['x'] = _ffn(y_r, y_d, y_f, st['x'], mod, l, st['row_ffn'], norm2_g[l], final_norm_g,
                           wo, wi, wf, l == DEPTH - 1, tm_ffn)
    y_prompt = streams[0]['x'].reshape(n_ctx, t_ctx, D_MODEL)
    y_sample = streams[1]['x'].reshape(n_dec, t_dec, D_MODEL)
    new_k = streams[0]['kv'][0].reshape(n_ctx, DEPTH, t_ctx, H_DIFF, 2, D_QK)
    new_v = streams[0]['kv'][1].reshape(n_ctx, DEPTH, t_ctx, H_DIFF, HEAD_DIM)
    return (y_prompt, y_sample, streams[0]['states'], new_k, new_v)
```

```python
import functools
import math

import numpy as np
import jax
import jax.numpy as jnp
from jax import lax
from jax.experimental import pallas as pl
from jax.experimental.pallas import tpu as pltpu

F32 = jnp.float32
BF16 = jnp.bfloat16

D_MODEL = 1024
DEPTH = 2
GRID_W = 64
HEAD_DIM = 64
D_RWKV = 384
H_RWKV = D_RWKV // HEAD_DIM
D_DIFF = 384
H_DIFF = D_DIFF // HEAD_DIM
D_QK = HEAD_DIM // 2
D_FNET = D_MODEL - D_RWKV - D_DIFF
FNET_GROUPS = 4
FNET_GROUP_DIM = D_FNET // FNET_GROUPS
LORA_W = 32
LORA_A = 32
LORA_G = 64
N_RWKV_IN = 3 * D_RWKV + 2 * LORA_W + 2 * LORA_A + LORA_G
N_DIFF_IN = 3 * D_DIFF
D_FF = ((8 * D_MODEL + 3 * 256 - 1) // (3 * 256)) * 256
ROPE_PAIRS = D_QK // 4
ROPE_BASE = 10000.0
RMS_EPS = 1e-6
GN_EPS = 64e-5
SUBLN_EPS = 1e-5
DECAY_SCALE = math.exp(-0.5)

LANES = 128
N_RWKV_PAD = 11 * LANES
D_IN_PAD = N_RWKV_PAD + N_DIFF_IN + D_FNET
N_PAIR = H_RWKV // 2
CHUNK = 64
N_PARTS = 2
SIDE_WORK_MAX_UNITS = 12
MOD_ROWS = 8
ONES_ROWS = 16
VMEM_LIMIT = 60 * 1024 * 1024

MOD_TN = 1536
INPROJ_TM = 512
FFN_TM = 512
ATTN_TQ = 256
ATTN_STEP_ROWS = 512
KEY_BLOCK = 512
FNET_STEP_ROWS = 1024
RWKV_STEP_ROWS = 1024
RWKV_ONE_STEP_ROWS = 2048


def _stream_plan(n_seq, seq_len):
    one_step = n_seq * seq_len <= RWKV_ONE_STEP_ROWS
    return dict(n_sub=n_seq if one_step else max(1, RWKV_STEP_ROWS // seq_len),
                n_attn=max(1, ATTN_STEP_ROWS // seq_len),
                n_fnet=max(1, FNET_STEP_ROWS // seq_len),
                tm_in=max(seq_len, INPROJ_TM))


def _cp(n_axes=1):
    return pltpu.CompilerParams(dimension_semantics=("arbitrary",) * n_axes,
                                vmem_limit_bytes=VMEM_LIMIT)


def _bdot(a, b):
    return jnp.dot(a.astype(BF16), b.astype(BF16), preferred_element_type=F32)


def _bdot_nt(a, b):
    return lax.dot_general(a.astype(BF16), b.astype(BF16), (((1,), (1,)), ((), ())),
                           preferred_element_type=F32)


def _split2(x):
    hi = x.astype(BF16)
    lo = (x - hi.astype(F32)).astype(BF16)
    return hi, lo


def _dot_x3(a, b):
    a_hi, a_lo = _split2(a)
    b_hi, b_lo = _split2(b)
    d = functools.partial(jnp.dot, preferred_element_type=F32)
    return d(a_hi, b_hi) + d(a_lo, b_hi) + d(a_hi, b_lo)


def _dot_exact_rhs(c_bf16, x):
    d = functools.partial(jnp.dot, preferred_element_type=F32)
    hi, lo = _split2(x)
    return d(c_bf16, hi) + d(c_bf16, lo)


def _rms(x, g):
    return x * lax.rsqrt(jnp.mean(x * x, axis=-1, keepdims=True) + RMS_EPS) * g


def _mod_body(c_ref, w_ref, b_ref, o_ref):
    c = c_ref[...]
    a = c * jax.nn.sigmoid(c)
    o_ref[0] = _dot_x3(a, w_ref[0]) + b_ref[0]


def _modulation(cond, w_mod, b_mod):
    n_layers, _, n_out = w_mod.shape
    tn = MOD_TN
    return pl.pallas_call(
        _mod_body,
        grid=(n_layers, n_out // tn),
        in_specs=[pl.BlockSpec((MOD_ROWS, D_MODEL), lambda l, j: (0, 0)),
                  pl.BlockSpec((1, D_MODEL, tn), lambda l, j: (l, 0, j)),
                  pl.BlockSpec((1, 1, tn), lambda l, j: (l, 0, j))],
        out_specs=pl.BlockSpec((1, MOD_ROWS, tn), lambda l, j: (l, 0, j)),
        out_shape=jax.ShapeDtypeStruct((n_layers, MOD_ROWS, n_out), F32),
        compiler_params=_cp(2),
    )(cond, w_mod, b_mod.reshape(n_layers, 1, n_out))


def _mod_spec(layer, col, row_fn):
    return pl.BlockSpec((1, 1, D_MODEL), lambda i: (layer * MOD_ROWS + row_fn(i), 0, col))


def _put_layer(ref, idx, layer, aliased, val):
    if aliased:
        ref[idx] = val
    else:
        for other in range(DEPTH):
            ref[idx + (other,)] = val if other == layer else jnp.zeros_like(val)


def _inproj_body(seq_len, layer, aliased, x_ref, g_ref, sh_ref, sc_ref, w_ref, mu_ref, *refs):
    ur_ref, q_ref, k_ref, v_ref, uf_ref = refs[2:] if aliased else refs
    part = x_ref.shape[0] // N_PARTS
    parts = [slice(j * part, (j + 1) * part) for j in range(N_PARTS)]
    h = [(_rms(x_ref[r, :], g_ref[...]) * (1.0 + sc_ref[0]) + sh_ref[0]).astype(BF16) for r in parts]
    u = jnp.concatenate([jnp.dot(hj, w_ref[...], preferred_element_type=F32) for hj in h], axis=0)
    ur = u[:, :N_RWKV_PAD]
    tm = ur.shape[0]
    pos = lax.broadcasted_iota(jnp.int32, (tm, 1), 0) & (seq_len - 1)
    prev = jnp.where(pos == 0, 0.0, pltpu.roll(ur, 1, axis=0))
    nxt = jnp.where(pos == seq_len - 1, 0.0, pltpu.roll(ur, tm - 1, axis=0))
    ur_ref[...] = ur + mu_ref[0:1, :] * (prev - ur) + mu_ref[1:2, :] * (nxt - ur)
    q_ref[...] = u[:, N_RWKV_PAD:N_RWKV_PAD + D_DIFF]
    for s in range(tm // seq_len):
        rows = slice(s * seq_len, (s + 1) * seq_len)
        _put_layer(k_ref, (s,), layer, aliased, u[rows, N_RWKV_PAD + D_DIFF:N_RWKV_PAD + 2 * D_DIFF])
        _put_layer(v_ref, (s,), layer, aliased, u[rows, N_RWKV_PAD + 2 * D_DIFF:N_RWKV_PAD + 3 * D_DIFF])
    uf_ref[...] = u[:, N_RWKV_PAD + N_DIFF_IN:]


def _inproj(x, mod, layer, row_fn, g, w, mu, kv, seq_len, tm):
    n_tok = x.shape[0]
    kv_shape = jax.ShapeDtypeStruct((n_tok // seq_len, DEPTH, seq_len, D_DIFF), F32)
    assert tm % seq_len == 0 and seq_len & (seq_len - 1) == 0
    row = lambda i: (i, 0)
    const = lambda i: (0, 0)
    if kv is None:
        kv_spec = pl.BlockSpec((tm // seq_len, DEPTH, seq_len, D_DIFF), lambda i: (i, 0, 0, 0))
    else:
        kv_spec = pl.BlockSpec((tm // seq_len, None, seq_len, D_DIFF), lambda i: (i, layer, 0, 0))
    return pl.pallas_call(
        functools.partial(_inproj_body, seq_len, layer, kv is not None),
        grid=(n_tok // tm,),
        in_specs=[pl.BlockSpec((tm, D_MODEL), row),
                  pl.BlockSpec((1, D_MODEL), const),
                  _mod_spec(layer, 0, row_fn),
                  _mod_spec(layer, 1, row_fn),
                  pl.BlockSpec((None, D_MODEL, D_IN_PAD), lambda i: (layer, 0, 0)),
                  pl.BlockSpec((2, N_RWKV_PAD), const)]
        + [pl.BlockSpec(memory_space=pl.ANY)] * (0 if kv is None else 2),
        out_specs=[pl.BlockSpec((tm, N_RWKV_PAD), row),
                   pl.BlockSpec((tm, D_DIFF), row),
                   kv_spec, kv_spec,
                   pl.BlockSpec((tm, D_FNET), row)],
        out_shape=[jax.ShapeDtypeStruct((n_tok, N_RWKV_PAD), F32),
                   jax.ShapeDtypeStruct((n_tok, D_DIFF), F32),
                   kv_shape, kv_shape,
                   jax.ShapeDtypeStruct((n_tok, D_FNET), F32)],
        input_output_aliases={} if kv is None else {6: 2, 7: 3},
        compiler_params=_cp(1),
    )(x, g.reshape(1, D_MODEL), mod, mod, w, mu, *(kv or ()))


def _lane_masks():
    lane = lax.broadcasted_iota(jnp.int32, (1, LANES), 1)
    return lane < HEAD_DIM, lane >= HEAD_DIM


def _rwkv_units(units, m0, m1, side=()):
    side = list(side)

    def run_side(drain=False):
        for gen in list(side):
            for _ in gen:
                if not drain:
                    break
            else:
                side.remove(gen)

    def bd(x):
        xb = x.astype(BF16)
        zero = jnp.zeros_like(xb)
        return jnp.concatenate([jnp.where(m0, xb, zero), jnp.where(m1, xb, zero)], axis=0)

    row = lax.broadcasted_iota(jnp.int32, (CHUNK, LANES), 0)
    col = lax.broadcasted_iota(jnp.int32, (CHUNK, LANES), 1) & (CHUNK - 1)
    eye = (col == row).astype(F32)
    r2 = lax.broadcasted_iota(jnp.int32, (LANES, LANES), 0) < HEAD_DIM
    c2 = lax.broadcasted_iota(jnp.int32, (LANES, LANES), 1) < HEAD_DIM
    rng = range(len(units))

    pre = []
    for rev, kk, r, v, kd, b, cum, cex, tot, s_prev in units:
        p_inv = jnp.exp(-cum)
        p_rem = jnp.exp(tot - cum)
        ab = -kk * jnp.exp(cex)
        rb = r * jnp.exp(cum)
        strict = (col > row) if rev else (col < row)
        incl = (col >= row) if rev else (col <= row)
        pre.append(dict(ab=ab, rb=rb, vbd=bd(v), strict=strict, incl=incl,
                        lhs=jnp.concatenate([ab, rb], axis=0),
                        rhs=jnp.concatenate([bd(b * p_inv), bd(kd * p_inv)], axis=0),
                        bk=jnp.concatenate([b * p_rem, kd * p_rem], axis=0)))

    run_side()
    mm = [_bdot_nt(q['lhs'], q['rhs']) for q in pre]
    run_side()
    m_ab = [jnp.where(pre[i]['strict'], mm[i][:CHUNK, :LANES], 0.0) for i in rng]
    m_ak = [jnp.where(pre[i]['strict'], mm[i][:CHUNK, LANES:], 0.0) for i in rng]
    m_r = [jnp.concatenate([jnp.where(pre[i]['incl'], mm[i][CHUNK:, :LANES], 0.0),
                            jnp.where(pre[i]['incl'], mm[i][CHUNK:, LANES:], 0.0)], axis=1) for i in rng]
    mv = [_bdot(m_ak[i], pre[i]['vbd']) for i in rng]
    run_side()

    t = [eye + m_ab[i] for i in rng]
    n = [_bdot(m_ab[i], bd(m_ab[i])) for i in rng]
    for _ in range(4):
        x = [_bdot(jnp.concatenate([t[i], n[i]], axis=0), bd(n[i])) for i in rng]
        t = [t[i] + x[i][:CHUNK] for i in rng]
        n = [x[i][CHUNK:] for i in rng]
        run_side()
    t = [t[i] + _bdot(t[i], bd(n[i])) for i in rng]

    w = [_bdot(t[i], jnp.concatenate([bd(pre[i]['ab']), bd(mv[i])], axis=1)) for i in rng]
    xs = [_bdot_nt(jnp.concatenate([w[i][:, :LANES], pre[i]['rb']], axis=0), units[i][9]) for i in rng]
    u = [xs[i][:CHUNK] + w[i][:, LANES:] for i in rng]
    run_side(drain=True)
    y = [xs[i][CHUNK:] + _bdot(m_r[i], jnp.concatenate([bd(u[i]), pre[i]['vbd']], axis=0)) for i in rng]
    z = [_bdot(jnp.concatenate([u[i], units[i][3]], axis=0).T, pre[i]['bk']) for i in rng]
    s_new = [units[i][9] * jnp.exp(units[i][8]) + jnp.where(r2 == c2, z[i], 0.0) for i in rng]
    return y, s_new


def _rwkv_body(seq_len, n_sub, layer, has_s0, aliased, u_ref, *refs):
    s0_ref = refs[0] if has_s0 else None
    refs = refs[1:] if has_s0 else refs
    wdec_ref, wicl_ref, wg_ref, w0a0_ref, vec_ref, bo_ref, tril_ref, triu_ref = refs[:8]
    (y_ref, sfin_ref, r_s, v_s, kk_s, kd_s, b_s, ci_s, ce_s, y_s, st_s) = refs[9 if aliased else 8:]
    n_chunk = seq_len // CHUNK
    assert n_chunk % 2 == 0 and n_chunk >= 4
    k_k = vec_ref[0:1, :]
    k_a = vec_ref[1:2, :]
    r_k = vec_ref[2:3, :]
    lnx_g = vec_ref[3:4, :]
    lnx_b = vec_ref[4:5, :]
    bo = bo_ref[...]

    def headsum(xb):
        return jnp.concatenate([jnp.dot(xb[:, p * LANES:(p + 1) * LANES], bo, preferred_element_type=F32)
                                for p in range(N_PAIR)], axis=1)

    def prep(rows):
        xs = u_ref[rows, :]
        r = xs[:, 0:D_RWKV]
        k = xs[:, D_RWKV:2 * D_RWKV]
        v = xs[:, 2 * D_RWKV:3 * D_RWKV]
        lora = xs[:, 3 * D_RWKV:3 * D_RWKV + LANES]
        t_lora = jnp.tanh(lora).astype(BF16)
        lora = lora.astype(BF16)
        kk = k * k_k
        kk2 = (kk * kk).astype(BF16)
        yield
        dec = _bdot(t_lora, wdec_ref[...])
        icl = _bdot(lora, wicl_ref[...])
        ss = headsum(kk2)
        yield
        logw = -DECAY_SCALE * jax.nn.sigmoid(w0a0_ref[0:1, :] + dec)
        a = jax.nn.sigmoid(w0a0_ref[1:2, :] + icl)
        kk = kk / jnp.maximum(jnp.sqrt(ss), 1e-12)
        a_f = a[:, :D_RWKV]
        a_b = a[:, D_RWKV:]
        kd_f = k * (1.0 + (a_f - 1.0) * k_a)
        kd_b = k * (1.0 + (a_b - 1.0) * k_a)
        lws = [_split2(logw[:, d * D_RWKV:(d + 1) * D_RWKV]) for d in range(2)]
        r_s[rows, :] = r
        v_s[rows, :] = v
        kk_s[rows, :] = kk
        kd_s[0, rows, :] = kd_f
        kd_s[1, rows, :] = kd_b
        b_s[0, rows, :] = kk * a_f
        b_s[1, rows, :] = kk * a_b
        yield
        dd = functools.partial(jnp.dot, preferred_element_type=F32)
        cums = [dd(tri_ref[...], lws[d][0]) + dd(tri_ref[...], lws[d][1])
                for d, tri_ref in enumerate((tril_ref, triu_ref))]
        yield
        for d in range(2):
            ci_s[d, rows, :] = cums[d]
            ce_s[d, rows, :] = cums[d] - logw[:, d * D_RWKV:(d + 1) * D_RWKV]

    def post(rows):
        y = y_s[rows, :]
        yb = y.astype(BF16)
        v = v_s[rows, :]
        bon = _split2(r_s[rows, :] * (kd_s[0, rows, :] + kd_s[1, rows, :]) * r_k)
        s_gd = jax.nn.sigmoid(u_ref[rows, 3 * D_RWKV + LANES:3 * D_RWKV + 2 * LANES]).astype(BF16)
        yield
        mean = headsum(yb) * (1.0 / HEAD_DIM)
        bonus = (headsum(bon[0]) + headsum(bon[1])) * v
        g = _bdot(s_gd, wg_ref[...])
        yield
        yc = y - mean
        yc2 = (yc * yc).astype(BF16)
        yield
        var = headsum(yc2) * (1.0 / HEAD_DIM)
        yield
        yn = yc * lax.rsqrt(var + GN_EPS) * lnx_g + lnx_b
        y_ref[rows, :] = (yn + bonus) * g

    def run_all(gens):
        gens = list(gens)
        while gens:
            gens = [gen for gen in gens if next(gen, gens) is not gens]

    def chunk_rows(i):
        return [pl.ds(pl.multiple_of(s * seq_len + (i if d == 0 else n_chunk - 1 - i) * CHUNK, CHUNK), CHUNK)
                for s in range(n_sub) for d in range(2)]

    st_s[...] = s0_ref[...] if has_s0 else jnp.zeros_like(st_s)
    y_s[...] = jnp.zeros_like(y_s)
    m0, m1 = _lane_masks()

    def scan_step(i, prep_next, post_prev):
        rows_sd = chunk_rows(i)
        side = []
        if prep_next:
            side = [prep(rows) for rows in chunk_rows(i + 1)]
        if post_prev:
            side = [post(rows) for rows in chunk_rows(i - 1)]
        units = []
        for s in range(n_sub):
            for d in range(2):
                rows = rows_sd[s * 2 + d]
                cum = ci_s[d, rows, :]
                cex = ce_s[d, rows, :]
                tot = cum[CHUNK - 1:CHUNK] if d == 0 else cum[0:1]
                kk = kk_s[rows, :]
                r = r_s[rows, :]
                v = v_s[rows, :]
                kd = kd_s[d, rows, :]
                b = b_s[d, rows, :]
                for p in range(N_PAIR):
                    sl = slice(p * LANES, (p + 1) * LANES)
                    units.append((d == 1, kk[:, sl], r[:, sl], v[:, sl], kd[:, sl], b[:, sl],
                                  cum[:, sl], cex[:, sl], tot[:, sl], st_s[s, d, p]))
        ys, s_new = _rwkv_units(units, m0, m1, side)
        for s in range(n_sub):
            for d in range(2):
                base = (s * 2 + d) * N_PAIR
                for p in range(N_PAIR):
                    st_s[s, d, p] = s_new[base + p]
                rows = rows_sd[s * 2 + d]
                y_s[rows, :] = y_s[rows, :] + jnp.concatenate(ys[base:base + N_PAIR], axis=1)

    def loop(lo, hi, **kw):
        lax.fori_loop(lo, hi, lambda i, c: (scan_step(i, **kw), c)[1], 0)

    half = n_chunk // 2
    if 2 * N_PAIR * n_sub <= SIDE_WORK_MAX_UNITS:
        run_all(prep(rows) for rows in chunk_rows(0))
        loop(0, half - 1, prep_next=True, post_prev=False)
        loop(half - 1, half + 1, prep_next=False, post_prev=False)
        loop(half + 1, n_chunk, prep_next=False, post_prev=True)
        run_all(post(rows) for rows in chunk_rows(n_chunk - 1))
    else:
        for i in range(half):
            run_all(prep(rows) for rows in chunk_rows(i))
        loop(0, n_chunk, prep_next=False, post_prev=False)
        for i in range(half):
            run_all(post(rows) for rows in chunk_rows(i))

    for s in range(n_sub):
        for d in range(2):
            for p in range(N_PAIR):
                st = st_s[s, d, p]
                for h in range(2):
                    rows = slice(h * HEAD_DIM, (h + 1) * HEAD_DIM)
                    val = st[rows, rows]
                    if aliased:
                        sfin_ref[s, d, 2 * p + h] = val
                    else:
                        for other in range(DEPTH):
                            sfin_ref[s, other, d, 2 * p + h] = val if other == layer else jnp.zeros_like(val)


def _rwkv(u_r, s0, seq_len, n_sub, wts, states, layer):
    n_seq = u_r.shape[0] // seq_len
    rows = n_sub * seq_len
    const2 = lambda b: (0, 0)
    st_shape = (n_sub, 2, N_PAIR, LANES, LANES)
    tok = pltpu.VMEM((rows, D_RWKV), F32)
    tok2 = pltpu.VMEM((2, rows, D_RWKV), F32)
    single = n_seq == n_sub
    in_specs = [pl.BlockSpec((rows, N_RWKV_PAD), lambda b: (b, 0),
                             pipeline_mode=pl.Buffered(1) if single else None)]
    args = [u_r]
    if s0 is not None:
        in_specs.append(pl.BlockSpec(st_shape, lambda b: (b, 0, 0, 0, 0)))
        args.append(_block_diag_state(s0))
    in_specs += [pl.BlockSpec(w.shape, const2) for w in wts]
    args += list(wts)
    if states is not None:
        in_specs.append(pl.BlockSpec(memory_space=pl.ANY))
        args.append(states)
    return pl.pallas_call(
        functools.partial(_rwkv_body, seq_len, n_sub, layer, s0 is not None, states is not None),
        grid=(n_seq // n_sub,),
        in_specs=in_specs,
        out_specs=[pl.BlockSpec((rows, D_RWKV), lambda b: (b, 0),
                                pipeline_mode=pl.Buffered(1) if single else None),
                   pl.BlockSpec((n_sub, DEPTH, 2, H_RWKV, HEAD_DIM, HEAD_DIM), lambda b: (b, 0, 0, 0, 0, 0))
                   if states is None else
                   pl.BlockSpec((n_sub, None, 2, H_RWKV, HEAD_DIM, HEAD_DIM),
                                lambda b: (b, layer, 0, 0, 0, 0))],
        out_shape=[jax.ShapeDtypeStruct((n_seq * seq_len, D_RWKV), F32),
                   jax.ShapeDtypeStruct((n_seq, DEPTH, 2, H_RWKV, HEAD_DIM, HEAD_DIM), F32)],
        scratch_shapes=[tok] * 3 + [tok2] * 4 + [tok, pltpu.VMEM(st_shape, F32)],
        input_output_aliases={} if states is None else {len(args) - 1: 1},
        compiler_params=_cp(1),
    )(*args)


def _rwkv_weights(p, l):
    z = functools.partial(jnp.zeros, dtype=F32)
    wdec = z((LANES, 2 * D_RWKV))
    wdec = wdec.at[0:LORA_W, :D_RWKV].set(p['decay_up'][l, 0])
    wdec = wdec.at[LORA_W:2 * LORA_W, D_RWKV:].set(p['decay_up'][l, 1])
    wicl = z((LANES, 2 * D_RWKV))
    wicl = wicl.at[2 * LORA_W:2 * LORA_W + LORA_A, :D_RWKV].set(p['iclr_up'][l, 0])
    wicl = wicl.at[2 * LORA_W + LORA_A:2 * LORA_W + 2 * LORA_A, D_RWKV:].set(p['iclr_up'][l, 1])
    wg = z((LANES, D_RWKV)).at[0:LORA_G].set(p['gate_up'][l])
    w0a0 = jnp.stack([p['decay_w0'][l].reshape(-1), p['iclr_a0'][l].reshape(-1)])
    vec = jnp.stack([p['k_k'][l], p['k_a'][l], p['r_k'][l].reshape(-1), p['lnx_g'][l], p['lnx_b'][l],
                     z((D_RWKV,)), z((D_RWKV,)), z((D_RWKV,))])
    head = np.arange(LANES) // HEAD_DIM
    bo = jnp.asarray(head[:, None] == head[None, :], BF16)
    idx = np.arange(CHUNK)
    tril = jnp.asarray(idx[None, :] <= idx[:, None], BF16)
    triu = jnp.asarray(idx[None, :] >= idx[:, None], BF16)
    return [wdec.astype(BF16), wicl.astype(BF16), wg.astype(BF16), w0a0, vec, bo, tril, triu]


def _rope(x, cos, sin):
    lane = lax.broadcasted_iota(jnp.int32, (1, LANES), 1)
    first_half = (lane & ROPE_PAIRS) == 0
    partner = jnp.where(first_half, pltpu.roll(x, LANES - ROPE_PAIRS, axis=1),
                        pltpu.roll(x, ROPE_PAIRS, axis=1))
    return x * cos + partner * sin


def _attn_body(has_ctx, lam_init, *refs):
    if has_ctx:
        (q_ref, k_ref, v_ref, lp_ref, sg_ref, kc_ref, vc_ref, cq_ref, sq_ref, ck_ref, sk_ref,
         o_ref) = refs
    else:
        q_ref, k_ref, v_ref, lp_ref, sg_ref, o_ref = refs
    n_pair = D_DIFF // LANES
    lp = lp_ref[...]
    lam = (jnp.exp(jnp.sum(lp[0:1] * lp[1:2], axis=-1, keepdims=True))
           - jnp.exp(jnp.sum(lp[2:3] * lp[3:4], axis=-1, keepdims=True)) + lam_init)
    lane = lax.broadcasted_iota(jnp.int32, (1, LANES), 1)

    tq = q_ref.shape[0] // k_ref.shape[0]

    def front(seq, pairs):
        qs, ks, vts = {}, {}, {}
        for p in pairs:
            sl = slice(p * LANES, (p + 1) * LANES)
            q = q_ref[seq * tq:(seq + 1) * tq, sl]
            k = k_ref[seq, :, sl]
            v = v_ref[seq, :, sl]
            if has_ctx:
                q = _rope(q, cq_ref[...], sq_ref[...])
                k = _rope(k, ck_ref[...], sk_ref[...])
                k = jnp.concatenate([kc_ref[0, :, sl], k], axis=0)
                v = jnp.concatenate([vc_ref[0, :, sl], v], axis=0)
            qs[p] = q * (D_QK ** -0.5 * math.log2(math.e))
            ks[p] = k.astype(BF16)
            vt = v.T
            ones = jnp.ones((ONES_ROWS, v.shape[0]), F32)
            vts[p] = [jnp.concatenate([vt[h * HEAD_DIM:(h + 1) * HEAD_DIM], ones], axis=0).astype(BF16)
                      for h in range(2)]
        chains = [(p, h, m) for p in pairs for h in range(2) for m in range(2)]
        scores = []
        for p, h, m in chains:
            lo = h * HEAD_DIM + m * D_QK
            sel = (lane >= lo) & (lane < lo + D_QK)
            qm = jnp.where(sel, qs[p], 0.0).astype(BF16)
            n_keys = ks[p].shape[0]
            scores.append([_bdot_nt(ks[p][j:j + KEY_BLOCK], qm) for j in range(0, n_keys, KEY_BLOCK)])
        return seq, pairs, chains, scores, vts

    def back(seq, pairs, chains, scores, vts):
        mx = [functools.reduce(jnp.maximum, [jnp.max(b, axis=0, keepdims=True) for b in s]) for s in scores]
        es = [[jnp.exp2(b - mx[i]).astype(BF16) for b in s] for i, s in enumerate(scores)]
        pv = [sum(jnp.dot(vts[p][h][:, j * KEY_BLOCK:(j + 1) * KEY_BLOCK], e, preferred_element_type=F32)
                  for j, e in enumerate(es[i])) for i, (p, h, m) in enumerate(chains)]
        for j, p in enumerate(pairs):
            halves = []
            for h in range(2):
                i = (j * 2 + h) * 2
                rows = slice(0, HEAD_DIM)
                o = (pv[i][rows] * (1.0 / pv[i][HEAD_DIM:HEAD_DIM + 1])
                     - lam * (pv[i + 1][rows] * (1.0 / pv[i + 1][HEAD_DIM:HEAD_DIM + 1])))
                ms = jnp.mean(o * o, axis=0, keepdims=True)
                halves.append(o * lax.rsqrt(ms + SUBLN_EPS))
            o_ref[seq * tq:(seq + 1) * tq, p * LANES:(p + 1) * LANES] = (
                jnp.concatenate(halves, axis=0).T * sg_ref[...] * (1.0 - lam_init))

    pair_groups = [[p] for p in range(n_pair)] if has_ctx else [list(range(n_pair))]
    groups = [(seq, pairs) for seq in range(k_ref.shape[0]) for pairs in pair_groups]
    pending = front(*groups[0])
    for nxt in groups[1:]:
        ahead = front(*nxt)
        back(*pending)
        pending = ahead
    back(*pending)


def _attention(q, k_all, v_all, layer, n_sub, lam_init, lp, sg, ctx=None):
    n_tok = q.shape[0]
    n_seq, _, seq_len, _ = k_all.shape
    tq = ATTN_TQ
    nq = seq_len // tq
    assert n_sub == 1 or nq == 1
    kv_spec = pl.BlockSpec((n_sub, None, seq_len, D_DIFF), lambda b, i: (b, layer, 0, 0))
    in_specs = [pl.BlockSpec((n_sub * tq, D_DIFF), lambda b, i: (b * nq + i, 0)),
                kv_spec, kv_spec,
                pl.BlockSpec(lp.shape, lambda b, i: (0, 0)),
                pl.BlockSpec((1, LANES), lambda b, i: (0, 0))]
    args = [q, k_all, v_all, lp, sg]
    if ctx is not None:
        kc, vc, cos, sin = ctx
        past = kc.shape[1]
        in_specs += [pl.BlockSpec((1, past, D_DIFF), lambda b, i: (b, 0, 0)),
                     pl.BlockSpec((1, past, D_DIFF), lambda b, i: (b, 0, 0)),
                     pl.BlockSpec((tq, LANES), lambda b, i: (i, 0)),
                     pl.BlockSpec((tq, LANES), lambda b, i: (i, 0)),
                     pl.BlockSpec((seq_len, LANES), lambda b, i: (0, 0)),
                     pl.BlockSpec((seq_len, LANES), lambda b, i: (0, 0))]
        args += [kc, vc, cos, sin, cos, sin]
    return pl.pallas_call(
        functools.partial(_attn_body, ctx is not None, lam_init),
        grid=(n_seq // n_sub, nq),
        in_specs=in_specs,
        out_specs=pl.BlockSpec((n_sub * tq, D_DIFF), lambda b, i: (b * nq + i, 0)),
        out_shape=jax.ShapeDtypeStruct((n_tok, D_DIFF), F32),
        compiler_params=_cp(2),
    )(*args)


def _rope_tables(seq_len):
    t = jnp.arange(seq_len)
    pos = jnp.stack([(t // GRID_W).astype(F32), (t % GRID_W).astype(F32)], axis=1)
    inv = 1.0 / (ROPE_BASE ** (jnp.arange(ROPE_PAIRS, dtype=F32) / ROPE_PAIRS))
    ang = pos[:, :, None] * inv
    d = np.arange(LANES) % D_QK
    axis = d // (2 * ROPE_PAIRS)
    second = (d % (2 * ROPE_PAIRS)) // ROPE_PAIRS
    idx = d % ROPE_PAIRS
    cos = jnp.cos(ang)[:, axis, idx]
    sin = jnp.sin(ang)[:, axis, idx] * jnp.asarray(np.where(second == 1, 1.0, -1.0), F32)
    return cos, sin


def _fnet_body(seq_len, x_ref, ct_ref, st_ref, cc_ref, sc_ref, o_ref):
    n_sub = x_ref.shape[0] // seq_len
    x = x_ref[...].astype(BF16)
    xc = jnp.dot(x, cc_ref[...], preferred_element_type=F32)
    xs = jnp.dot(x, sc_ref[...], preferred_element_type=F32)
    wide = lambda a: jnp.concatenate([a[s * seq_len:(s + 1) * seq_len] for s in range(n_sub)], axis=1)
    y = _bdot(ct_ref[...], wide(xc)) - _bdot(st_ref[...], wide(xs))
    for s in range(n_sub):
        o_ref[s * seq_len:(s + 1) * seq_len, :] = y[:, s * D_FNET:(s + 1) * D_FNET]


def _dft_consts(n, block=1):
    idx = np.arange(n)
    ang = 2.0 * np.pi * ((idx[:, None] * idx[None, :]) % n) / n
    return [jnp.asarray(np.kron(np.eye(block), m).astype(np.float32)).astype(BF16)
            for m in (np.cos(ang) / np.sqrt(n), np.sin(ang) / np.sqrt(n))]


def _fnet(u_f, seq_len, n_sub):
    n_tok = u_f.shape[0]
    rows = n_sub * seq_len
    consts = _dft_consts(seq_len) + _dft_consts(FNET_GROUP_DIM, FNET_GROUPS)
    const = lambda b: (0, 0)
    return pl.pallas_call(
        functools.partial(_fnet_body, seq_len),
        grid=(n_tok // rows,),
        in_specs=[pl.BlockSpec((rows, D_FNET), lambda b: (b, 0))]
        + [pl.BlockSpec(c.shape, const) for c in consts],
        out_specs=pl.BlockSpec((rows, D_FNET), lambda b: (b, 0)),
        out_shape=jax.ShapeDtypeStruct((n_tok, D_FNET), F32),
        compiler_params=_cp(1),
    )(u_f, *consts)


def _ffn_body(final, yr_ref, yd_ref, yf_ref, x_ref, g1_ref, sh2_ref, sc2_ref, g2_ref, n2_ref, fg_ref,
              wo_ref, wi_ref, wf_ref, o_ref):
    part = x_ref.shape[0] // N_PARTS
    parts = [slice(j * part, (j + 1) * part) for j in range(N_PARTS)]
    y = [_bdot(yr_ref[r, :], wo_ref[0:D_RWKV, :])
         + _bdot(yd_ref[r, :], wo_ref[D_RWKV:D_RWKV + D_DIFF, :])
         + _bdot(yf_ref[r, :], wo_ref[D_RWKV + D_DIFF:, :]) for r in parts]
    x = [x_ref[r, :] + g1_ref[0] * y[j] for j, r in enumerate(parts)]
    h = [(_rms(xj, n2_ref[...]) * (1.0 + sc2_ref[0]) + sh2_ref[0]).astype(BF16) for xj in x]
    z = [jnp.dot(hj, wi_ref[...], preferred_element_type=F32) for hj in h]
    act = [zj[:, :D_FF] * jax.nn.sigmoid(zj[:, :D_FF]) * zj[:, D_FF:] for zj in z]
    f = [_bdot(aj, wf_ref[...]) for aj in act]
    for j, r in enumerate(parts):
        xj = x[j] + g2_ref[0] * f[j]
        o_ref[r, :] = _rms(xj, fg_ref[...]) if final else xj


def _ffn(y_r, y_d, y_f, x, mod, layer, row_fn, n2, fg, wo, wi, wf, final, tm):
    n_tok = x.shape[0]
    row = lambda i: (i, 0)
    const = lambda i: (0, 0)
    return pl.pallas_call(
        functools.partial(_ffn_body, final),
        grid=(n_tok // tm,),
        in_specs=[pl.BlockSpec((tm, D_RWKV), row),
                  pl.BlockSpec((tm, D_DIFF), row),
                  pl.BlockSpec((tm, D_FNET), row),
                  pl.BlockSpec((tm, D_MODEL), row),
                  _mod_spec(layer, 2, row_fn),
                  _mod_spec(layer, 3, row_fn),
                  _mod_spec(layer, 4, row_fn),
                  _mod_spec(layer, 5, row_fn),
                  pl.BlockSpec((1, D_MODEL), const),
                  pl.BlockSpec((1, D_MODEL), const),
                  pl.BlockSpec((None,) + wo.shape[1:], lambda i: (layer, 0, 0), pipeline_mode=pl.Buffered(1)),
                  pl.BlockSpec((None,) + wi.shape[1:], lambda i: (layer, 0, 0), pipeline_mode=pl.Buffered(1)),
                  pl.BlockSpec((None,) + wf.shape[1:], lambda i: (layer, 0, 0), pipeline_mode=pl.Buffered(1))],
        out_specs=pl.BlockSpec((tm, D_MODEL), row),
        out_shape=jax.ShapeDtypeStruct((n_tok, D_MODEL), F32),
        compiler_params=_cp(1),
    )(y_r, y_d, y_f, x, mod, mod, mod, mod, n2.reshape(1, D_MODEL), fg.reshape(1, D_MODEL), wo, wi, wf)


def _block_diag_state(s):
    b = s.shape[0]
    s = s.reshape(b, 2, N_PAIR, 2, HEAD_DIM, HEAD_DIM)
    eye = jnp.eye(2, dtype=s.dtype)
    s = s[:, :, :, :, :, None, :] * eye[None, None, None, :, None, :, None]
    return s.reshape(b, 2, N_PAIR, LANES, LANES)


def kernel(x_prompt, x_sample, c, state_rwkv, cache_diff_k, cache_diff_v, c_ctx, norm1_g, norm2_g, final_norm_g, w_mod, b_mod, w_in, w_out, shift_mu, decay_w0, decay_up, iclr_a0, iclr_up, gate_up, k_k, k_a, r_k, lnx_g, lnx_b, diff_lambda, subln_g, w_ffn_in, w_ffn_out):
    p = dict(shift_mu=shift_mu, decay_w0=decay_w0, decay_up=decay_up, iclr_a0=iclr_a0, iclr_up=iclr_up,
             gate_up=gate_up, k_k=k_k, k_a=k_a, r_k=r_k, lnx_g=lnx_g, lnx_b=lnx_b)
    n_ctx, t_ctx, _ = x_prompt.shape
    n_dec, t_dec, _ = x_sample.shape
    past = cache_diff_k.shape[2]

    cond = jnp.concatenate([c_ctx[None, :], c, jnp.zeros((MOD_ROWS - 1 - n_dec, D_MODEL), F32)], axis=0)
    mod = _modulation(cond, w_mod, b_mod).reshape(DEPTH * MOD_ROWS, 1, 6 * D_MODEL)

    tm_ffn = FFN_TM
    ctx_plan = _stream_plan(n_ctx, t_ctx)
    dec_plan = _stream_plan(n_dec, t_dec)
    streams = [
        dict(x=x_prompt.reshape(n_ctx * t_ctx, D_MODEL), t=t_ctx, n=n_ctx, **ctx_plan,
             row_in=lambda i: 0, row_ffn=lambda i: 0),
        dict(x=x_sample.reshape(n_dec * t_dec, D_MODEL), t=t_dec, n=n_dec, **dec_plan,
             row_in=lambda i: 1 + i // (t_dec // dec_plan['tm_in']), row_ffn=lambda i: 1 + i // (t_dec // tm_ffn)),
    ]
    cos, sin = _rope_tables(t_dec)
    for st in streams:
        st['kv'] = st['states'] = None
    w_in_l = jnp.concatenate(
        [w_in[:, :, :N_RWKV_IN].astype(BF16), jnp.zeros((DEPTH, D_MODEL, N_RWKV_PAD - N_RWKV_IN), BF16),
         w_in[:, :, N_RWKV_IN:].astype(BF16)], axis=2)
    wo = w_out.astype(BF16)
    wi = w_ffn_in.astype(BF16)
    wf = w_ffn_out.astype(BF16)
    for l in range(DEPTH):
        rw = _rwkv_weights(p, l)
        mu = jnp.concatenate([shift_mu[l], jnp.zeros((2, N_RWKV_PAD - N_RWKV_IN), F32)], axis=1)
        lam_init = 0.8 - 0.6 * math.exp(-0.3 * l)
        sg = jnp.tile(subln_g[l], 2).reshape(1, LANES)
        for si, st in enumerate(streams):
            u_r, q, k_all, v_all, u_f = _inproj(st['x'], mod, l, st['row_in'], norm1_g[l], w_in_l, mu,
                                                 st['kv'], st['t'], st['tm_in'])
            st['kv'] = (k_all, v_all)
            if si == 0:
                s0 = None
                attn_ctx = None
            else:
                s0 = state_rwkv[:, l].astype(F32)
                attn_ctx = (cache_diff_k[:, l].reshape(n_dec, past, D_DIFF).astype(F32),
                            cache_diff_v[:, l].reshape(n_dec, past, D_DIFF).astype(F32), cos, sin)
            y_r, st['states'] = _rwkv(u_r, s0, st['t'], st['n_sub'], rw, st['states'], l)
            y_d = _attention(q, k_all, v_all, l, st['n_attn'], lam_init, diff_lambda[l], sg, attn_ctx)
            y_f = _fnet(u_f, st['t'], st['n_fnet'])
            st['x'] = _ffn(y_r, y_d, y_f, st['x'], mod, l, st['row_ffn'], norm2_g[l], final_norm_g,
                           wo, wi, wf, l == DEPTH - 1, tm_ffn)
    y_prompt = streams[0]['x'].reshape(n_ctx, t_ctx, D_MODEL)
    y_sample = streams[1]['x'].reshape(n_dec, t_dec, D_MODEL)
    new_k = streams[0]['kv'][0].reshape(n_ctx, DEPTH, t_ctx, H_DIFF, 2, D_QK)
    new_v = streams[0]['kv'][1].reshape(n_ctx, DEPTH, t_ctx, H_DIFF, HEAD_DIM)
    return (y_prompt, y_sample, streams[0]['states'], new_k, new_v)
```

```python
import functools
import math

import numpy as np
import jax
import jax.numpy as jnp
from jax import lax
from jax.experimental import pallas as pl
from jax.experimental.pallas import tpu as pltpu

F32 = jnp.float32
BF16 = jnp.bfloat16

D_MODEL = 1024
DEPTH = 2
GRID_W = 64
HEAD_DIM = 64
D_RWKV = 384
H_RWKV = D_RWKV // HEAD_DIM
D_DIFF = 384
H_DIFF = D_DIFF // HEAD_DIM
D_QK = HEAD_DIM // 2
D_FNET = D_MODEL - D_RWKV - D_DIFF
FNET_GROUPS = 4
FNET_GROUP_DIM = D_FNET // FNET_GROUPS
LORA_W = 32
LORA_A = 32
LORA_G = 64
N_RWKV_IN = 3 * D_RWKV + 2 * LORA_W + 2 * LORA_A + LORA_G
N_DIFF_IN = 3 * D_DIFF
D_FF = ((8 * D_MODEL + 3 * 256 - 1) // (3 * 256)) * 256
ROPE_PAIRS = D_QK // 4
ROPE_BASE = 10000.0
RMS_EPS = 1e-6
GN_EPS = 64e-5
SUBLN_EPS = 1e-5
DECAY_SCALE = math.exp(-0.5)

LANES = 128
N_RWKV_PAD = 11 * LANES
D_IN_PAD = N_RWKV_PAD + N_DIFF_IN + D_FNET
N_PAIR = H_RWKV // 2
CHUNK = 64
N_PARTS = 2
SIDE_WORK_MAX_UNITS = 12
MOD_ROWS = 8
ONES_ROWS = 16
VMEM_LIMIT = 60 * 1024 * 1024

MOD_TN = 1536
INPROJ_TM = 512
FFN_TM = 512
ATTN_TQ = 256
ATTN_STEP_ROWS = 1024
KEY_BLOCK = 512
FNET_STEP_ROWS = 1024
RWKV_STEP_ROWS = 1024
RWKV_ONE_STEP_ROWS = 2048


def _stream_plan(n_seq, seq_len):
    one_step = n_seq * seq_len <= RWKV_ONE_STEP_ROWS
    return dict(n_sub=n_seq if one_step else max(1, RWKV_STEP_ROWS // seq_len),
                n_attn=max(1, ATTN_STEP_ROWS // seq_len),
                n_fnet=max(1, FNET_STEP_ROWS // seq_len),
                tm_in=max(seq_len, INPROJ_TM))


def _cp(n_axes=1):
    return pltpu.CompilerParams(dimension_semantics=("arbitrary",) * n_axes,
                                vmem_limit_bytes=VMEM_LIMIT)


def _bdot(a, b):
    return jnp.dot(a.astype(BF16), b.astype(BF16), preferred_element_type=F32)


def _bdot_nt(a, b):
    return lax.dot_general(a.astype(BF16), b.astype(BF16), (((1,), (1,)), ((), ())),
                           preferred_element_type=F32)


def _split2(x):
    hi = x.astype(BF16)
    lo = (x - hi.astype(F32)).astype(BF16)
    return hi, lo


def _dot_x3(a, b):
    a_hi, a_lo = _split2(a)
    b_hi, b_lo = _split2(b)
    d = functools.partial(jnp.dot, preferred_element_type=F32)
    return d(a_hi, b_hi) + d(a_lo, b_hi) + d(a_hi, b_lo)


def _rms(x, g):
    return x * lax.rsqrt(jnp.mean(x * x, axis=-1, keepdims=True) + RMS_EPS) * g


def _mod_body(c_ref, w_ref, b_ref, o_ref):
    c = c_ref[...]
    a = c * jax.nn.sigmoid(c)
    o_ref[0] = _dot_x3(a, w_ref[0]) + b_ref[0]


def _modulation(cond, w_mod, b_mod):
    n_layers, _, n_out = w_mod.shape
    tn = MOD_TN
    return pl.pallas_call(
        _mod_body,
        grid=(n_layers, n_out // tn),
        in_specs=[pl.BlockSpec((MOD_ROWS, D_MODEL), lambda l, j: (0, 0)),
                  pl.BlockSpec((1, D_MODEL, tn), lambda l, j: (l, 0, j)),
                  pl.BlockSpec((1, 1, tn), lambda l, j: (l, 0, j))],
        out_specs=pl.BlockSpec((1, MOD_ROWS, tn), lambda l, j: (l, 0, j)),
        out_shape=jax.ShapeDtypeStruct((n_layers, MOD_ROWS, n_out), F32),
        compiler_params=_cp(2),
    )(cond, w_mod, b_mod.reshape(n_layers, 1, n_out))


def _mod_spec(layer, col, row_fn):
    return pl.BlockSpec((1, 1, D_MODEL), lambda i: (layer * MOD_ROWS + row_fn(i), 0, col))


def _put_layer(ref, idx, layer, aliased, val):
    if aliased:
        ref[idx] = val
    else:
        for other in range(DEPTH):
            ref[idx + (other,)] = val if other == layer else jnp.zeros_like(val)


def _inproj_body(seq_len, layer, aliased, x_ref, g_ref, sh_ref, sc_ref, w_ref, mu_ref, *refs):
    ur_ref, q_ref, k_ref, v_ref, uf_ref = refs[2:] if aliased else refs
    part = x_ref.shape[0] // N_PARTS
    parts = [slice(j * part, (j + 1) * part) for j in range(N_PARTS)]
    h = [(_rms(x_ref[r, :], g_ref[...]) * (1.0 + sc_ref[0]) + sh_ref[0]).astype(BF16) for r in parts]
    u = jnp.concatenate([jnp.dot(hj, w_ref[...], preferred_element_type=F32) for hj in h], axis=0)
    ur = u[:, :N_RWKV_PAD]
    tm = ur.shape[0]
    pos = lax.broadcasted_iota(jnp.int32, (tm, 1), 0) & (seq_len - 1)
    prev = jnp.where(pos == 0, 0.0, pltpu.roll(ur, 1, axis=0))
    nxt = jnp.where(pos == seq_len - 1, 0.0, pltpu.roll(ur, tm - 1, axis=0))
    ur_ref[...] = ur + mu_ref[0:1, :] * (prev - ur) + mu_ref[1:2, :] * (nxt - ur)
    q_ref[...] = u[:, N_RWKV_PAD:N_RWKV_PAD + D_DIFF]
    for s in range(tm // seq_len):
        rows = slice(s * seq_len, (s + 1) * seq_len)
        _put_layer(k_ref, (s,), layer, aliased, u[rows, N_RWKV_PAD + D_DIFF:N_RWKV_PAD + 2 * D_DIFF])
        _put_layer(v_ref, (s,), layer, aliased, u[rows, N_RWKV_PAD + 2 * D_DIFF:N_RWKV_PAD + 3 * D_DIFF])
    uf_ref[...] = u[:, N_RWKV_PAD + N_DIFF_IN:]


def _inproj(x, mod, layer, row_fn, g, w, mu, kv, seq_len, tm):
    n_tok = x.shape[0]
    kv_shape = jax.ShapeDtypeStruct((n_tok // seq_len, DEPTH, seq_len, D_DIFF), F32)
    assert tm % seq_len == 0 and seq_len & (seq_len - 1) == 0
    row = lambda i: (i, 0)
    const = lambda i: (0, 0)
    if kv is None:
        kv_spec = pl.BlockSpec((tm // seq_len, DEPTH, seq_len, D_DIFF), lambda i: (i, 0, 0, 0))
    else:
        kv_spec = pl.BlockSpec((tm // seq_len, None, seq_len, D_DIFF), lambda i: (i, layer, 0, 0))
    return pl.pallas_call(
        functools.partial(_inproj_body, seq_len, layer, kv is not None),
        grid=(n_tok // tm,),
        in_specs=[pl.BlockSpec((tm, D_MODEL), row),
                  pl.BlockSpec((1, D_MODEL), const),
                  _mod_spec(layer, 0, row_fn),
                  _mod_spec(layer, 1, row_fn),
                  pl.BlockSpec((None, D_MODEL, D_IN_PAD), lambda i: (layer, 0, 0)),
                  pl.BlockSpec((2, N_RWKV_PAD), const)]
        + [pl.BlockSpec(memory_space=pl.ANY)] * (0 if kv is None else 2),
        out_specs=[pl.BlockSpec((tm, N_RWKV_PAD), row),
                   pl.BlockSpec((tm, D_DIFF), row),
                   kv_spec, kv_spec,
                   pl.BlockSpec((tm, D_FNET), row)],
        out_shape=[jax.ShapeDtypeStruct((n_tok, N_RWKV_PAD), F32),
                   jax.ShapeDtypeStruct((n_tok, D_DIFF), F32),
                   kv_shape, kv_shape,
                   jax.ShapeDtypeStruct((n_tok, D_FNET), F32)],
        input_output_aliases={} if kv is None else {6: 2, 7: 3},
        compiler_params=_cp(1),
    )(x, g.reshape(1, D_MODEL), mod, mod, w, mu, *(kv or ()))


def _lane_masks():
    lane = lax.broadcasted_iota(jnp.int32, (1, LANES), 1)
    return lane < HEAD_DIM, lane >= HEAD_DIM


def _rwkv_units(units, m0, m1, side=()):
    side = list(side)

    def run_side(drain=False):
        for gen in list(side):
            for _ in gen:
                if not drain:
                    break
            else:
                side.remove(gen)

    def bd(x):
        xb = x.astype(BF16)
        zero = jnp.zeros_like(xb)
        return jnp.concatenate([jnp.where(m0, xb, zero), jnp.where(m1, xb, zero)], axis=0)

    row = lax.broadcasted_iota(jnp.int32, (CHUNK, LANES), 0)
    col = lax.broadcasted_iota(jnp.int32, (CHUNK, LANES), 1) & (CHUNK - 1)
    eye = (col == row).astype(F32)
    r2 = lax.broadcasted_iota(jnp.int32, (LANES, LANES), 0) < HEAD_DIM
    c2 = lax.broadcasted_iota(jnp.int32, (LANES, LANES), 1) < HEAD_DIM
    rng = range(len(units))

    pre = []
    for rev, kk, r, v, kd, b, cum, cex, tot, s_prev in units:
        p_inv = jnp.exp(-cum)
        p_rem = jnp.exp(tot - cum)
        ab = -kk * jnp.exp(cex)
        rb = r * jnp.exp(cum)
        strict = (col > row) if rev else (col < row)
        incl = (col >= row) if rev else (col <= row)
        pre.append(dict(ab=ab, rb=rb, vbd=bd(v), strict=strict, incl=incl,
                        lhs=jnp.concatenate([ab, rb], axis=0),
                        rhs=jnp.concatenate([bd(b * p_inv), bd(kd * p_inv)], axis=0),
                        bk=jnp.concatenate([b * p_rem, kd * p_rem], axis=0)))

    run_side()
    mm = [_bdot_nt(q['lhs'], q['rhs']) for q in pre]
    run_side()
    m_ab = [jnp.where(pre[i]['strict'], mm[i][:CHUNK, :LANES], 0.0) for i in rng]
    m_ak = [jnp.where(pre[i]['strict'], mm[i][:CHUNK, LANES:], 0.0) for i in rng]
    m_r = [jnp.concatenate([jnp.where(pre[i]['incl'], mm[i][CHUNK:, :LANES], 0.0),
                            jnp.where(pre[i]['incl'], mm[i][CHUNK:, LANES:], 0.0)], axis=1) for i in rng]
    mv = [_bdot(m_ak[i], pre[i]['vbd']) for i in rng]
    run_side()

    t = [eye + m_ab[i] for i in rng]
    n = [_bdot(m_ab[i], bd(m_ab[i])) for i in rng]
    for _ in range(4):
        x = [_bdot(jnp.concatenate([t[i], n[i]], axis=0), bd(n[i])) for i in rng]
        t = [t[i] + x[i][:CHUNK] for i in rng]
        n = [x[i][CHUNK:] for i in rng]
        run_side()
    t = [t[i] + _bdot(t[i], bd(n[i])) for i in rng]

    w = [_bdot(t[i], jnp.concatenate([bd(pre[i]['ab']), bd(mv[i])], axis=1)) for i in rng]
    xs = [_bdot_nt(jnp.concatenate([w[i][:, :LANES], pre[i]['rb']], axis=0), units[i][9]) for i in rng]
    u = [xs[i][:CHUNK] + w[i][:, LANES:] for i in rng]
    run_side(drain=True)
    y = [xs[i][CHUNK:] + _bdot(m_r[i], jnp.concatenate([bd(u[i]), pre[i]['vbd']], axis=0)) for i in rng]
    z = [_bdot(jnp.concatenate([u[i], units[i][3]], axis=0).T, pre[i]['bk']) for i in rng]
    s_new = [units[i][9] * jnp.exp(units[i][8]) + jnp.where(r2 == c2, z[i], 0.0) for i in rng]
    return y, s_new


def _rwkv_body(seq_len, n_sub, layer, has_s0, aliased, u_ref, *refs):
    s0_ref = refs[0] if has_s0 else None
    refs = refs[1:] if has_s0 else refs
    wdec_ref, wicl_ref, wg_ref, w0a0_ref, vec_ref, bo_ref, tril_ref, triu_ref = refs[:8]
    (y_ref, sfin_ref, r_s, v_s, kk_s, kd_s, b_s, ci_s, ce_s, y_s, st_s) = refs[9 if aliased else 8:]
    n_chunk = seq_len // CHUNK
    assert n_chunk % 2 == 0 and n_chunk >= 4
    k_k = vec_ref[0:1, :]
    k_a = vec_ref[1:2, :]
    r_k = vec_ref[2:3, :]
    lnx_g = vec_ref[3:4, :]
    lnx_b = vec_ref[4:5, :]
    bo = bo_ref[...]

    def headsum(xb):
        return jnp.concatenate([jnp.dot(xb[:, p * LANES:(p + 1) * LANES], bo, preferred_element_type=F32)
                                for p in range(N_PAIR)], axis=1)

    def prep(rows):
        xs = u_ref[rows, :]
        r = xs[:, 0:D_RWKV]
        k = xs[:, D_RWKV:2 * D_RWKV]
        v = xs[:, 2 * D_RWKV:3 * D_RWKV]
        lora = xs[:, 3 * D_RWKV:3 * D_RWKV + LANES]
        t_lora = jnp.tanh(lora).astype(BF16)
        lora = lora.astype(BF16)
        kk = k * k_k
        kk2 = (kk * kk).astype(BF16)
        yield
        dec = _bdot(t_lora, wdec_ref[...])
        icl = _bdot(lora, wicl_ref[...])
        ss = headsum(kk2)
        yield
        logw = -DECAY_SCALE * jax.nn.sigmoid(w0a0_ref[0:1, :] + dec)
        a = jax.nn.sigmoid(w0a0_ref[1:2, :] + icl)
        kk = kk / jnp.maximum(jnp.sqrt(ss), 1e-12)
        a_f = a[:, :D_RWKV]
        a_b = a[:, D_RWKV:]
        kd_f = k * (1.0 + (a_f - 1.0) * k_a)
        kd_b = k * (1.0 + (a_b - 1.0) * k_a)
        lws = [_split2(logw[:, d * D_RWKV:(d + 1) * D_RWKV]) for d in range(2)]
        r_s[rows, :] = r
        v_s[rows, :] = v
        kk_s[rows, :] = kk
        kd_s[0, rows, :] = kd_f
        kd_s[1, rows, :] = kd_b
        b_s[0, rows, :] = kk * a_f
        b_s[1, rows, :] = kk * a_b
        yield
        dd = functools.partial(jnp.dot, preferred_element_type=F32)
        cums = [dd(tri_ref[...], lws[d][0]) + dd(tri_ref[...], lws[d][1])
                for d, tri_ref in enumerate((tril_ref, triu_ref))]
        yield
        for d in range(2):
            ci_s[d, rows, :] = cums[d]
            ce_s[d, rows, :] = cums[d] - logw[:, d * D_RWKV:(d + 1) * D_RWKV]

    def post(rows):
        y = y_s[rows, :]
        yb = y.astype(BF16)
        v = v_s[rows, :]
        bon = _split2(r_s[rows, :] * (kd_s[0, rows, :] + kd_s[1, rows, :]) * r_k)
        s_gd = jax.nn.sigmoid(u_ref[rows, 3 * D_RWKV + LANES:3 * D_RWKV + 2 * LANES]).astype(BF16)
        yield
        mean = headsum(yb) * (1.0 / HEAD_DIM)
        bonus = (headsum(bon[0]) + headsum(bon[1])) * v
        g = _bdot(s_gd, wg_ref[...])
        yield
        yc = y - mean
        yc2 = (yc * yc).astype(BF16)
        yield
        var = headsum(yc2) * (1.0 / HEAD_DIM)
        yield
        yn = yc * lax.rsqrt(var + GN_EPS) * lnx_g + lnx_b
        y_ref[rows, :] = (yn + bonus) * g

    def run_all(gens):
        gens = list(gens)
        while gens:
            gens = [gen for gen in gens if next(gen, gens) is not gens]

    def chunk_rows(i):
        return [pl.ds(pl.multiple_of(s * seq_len + (i if d == 0 else n_chunk - 1 - i) * CHUNK, CHUNK), CHUNK)
                for s in range(n_sub) for d in range(2)]

    st_s[...] = s0_ref[...] if has_s0 else jnp.zeros_like(st_s)
    y_s[...] = jnp.zeros_like(y_s)
    m0, m1 = _lane_masks()

    def scan_step(i, prep_next, post_prev):
        rows_sd = chunk_rows(i)
        side = []
        if prep_next:
            side = [prep(rows) for rows in chunk_rows(i + 1)]
        if post_prev:
            side = [post(rows) for rows in chunk_rows(i - 1)]
        units = []
        for s in range(n_sub):
            for d in range(2):
                rows = rows_sd[s * 2 + d]
                cum = ci_s[d, rows, :]
                cex = ce_s[d, rows, :]
                tot = cum[CHUNK - 1:CHUNK] if d == 0 else cum[0:1]
                kk = kk_s[rows, :]
                r = r_s[rows, :]
                v = v_s[rows, :]
                kd = kd_s[d, rows, :]
                b = b_s[d, rows, :]
                for p in range(N_PAIR):
                    sl = slice(p * LANES, (p + 1) * LANES)
                    units.append((d == 1, kk[:, sl], r[:, sl], v[:, sl], kd[:, sl], b[:, sl],
                                  cum[:, sl], cex[:, sl], tot[:, sl], st_s[s, d, p]))
        ys, s_new = _rwkv_units(units, m0, m1, side)
        for s in range(n_sub):
            for d in range(2):
                base = (s * 2 + d) * N_PAIR
                for p in range(N_PAIR):
                    st_s[s, d, p] = s_new[base + p]
                rows = rows_sd[s * 2 + d]
                y_s[rows, :] = y_s[rows, :] + jnp.concatenate(ys[base:base + N_PAIR], axis=1)

    def loop(lo, hi, **kw):
        lax.fori_loop(lo, hi, lambda i, c: (scan_step(i, **kw), c)[1], 0)

    half = n_chunk // 2
    if 2 * N_PAIR * n_sub <= SIDE_WORK_MAX_UNITS:
        run_all(prep(rows) for rows in chunk_rows(0))
        loop(0, half - 1, prep_next=True, post_prev=False)
        loop(half - 1, half + 1, prep_next=False, post_prev=False)
        loop(half + 1, n_chunk, prep_next=False, post_prev=True)
        run_all(post(rows) for rows in chunk_rows(n_chunk - 1))
    else:
        for i in range(half):
            run_all(prep(rows) for rows in chunk_rows(i))
        loop(0, n_chunk, prep_next=False, post_prev=False)
        for i in range(half):
            run_all(post(rows) for rows in chunk_rows(i))

    for s in range(n_sub):
        for d in range(2):
            for p in range(N_PAIR):
                st = st_s[s, d, p]
                for h in range(2):
                    rows = slice(h * HEAD_DIM, (h + 1) * HEAD_DIM)
                    val = st[rows, rows]
                    if aliased:
                        sfin_ref[s, d, 2 * p + h] = val
                    else:
                        for other in range(DEPTH):
                            sfin_ref[s, other, d, 2 * p + h] = val if other == layer else jnp.zeros_like(val)


def _rwkv(u_r, s0, seq_len, n_sub, wts, states, layer):
    n_seq = u_r.shape[0] // seq_len
    rows = n_sub * seq_len
    const2 = lambda b: (0, 0)
    st_shape = (n_sub, 2, N_PAIR, LANES, LANES)
    tok = pltpu.VMEM((rows, D_RWKV), F32)
    tok2 = pltpu.VMEM((2, rows, D_RWKV), F32)
    single = n_seq == n_sub
    in_specs = [pl.BlockSpec((rows, N_RWKV_PAD), lambda b: (b, 0),
                             pipeline_mode=pl.Buffered(1) if single else None)]
    args = [u_r]
    if s0 is not None:
        in_specs.append(pl.BlockSpec(st_shape, lambda b: (b, 0, 0, 0, 0)))
        args.append(_block_diag_state(s0))
    in_specs += [pl.BlockSpec(w.shape, const2) for w in wts]
    args += list(wts)
    if states is not None:
        in_specs.append(pl.BlockSpec(memory_space=pl.ANY))
        args.append(states)
    return pl.pallas_call(
        functools.partial(_rwkv_body, seq_len, n_sub, layer, s0 is not None, states is not None),
        grid=(n_seq // n_sub,),
        in_specs=in_specs,
        out_specs=[pl.BlockSpec((rows, D_RWKV), lambda b: (b, 0),
                                pipeline_mode=pl.Buffered(1) if single else None),
                   pl.BlockSpec((n_sub, DEPTH, 2, H_RWKV, HEAD_DIM, HEAD_DIM), lambda b: (b, 0, 0, 0, 0, 0))
                   if states is None else
                   pl.BlockSpec((n_sub, None, 2, H_RWKV, HEAD_DIM, HEAD_DIM),
                                lambda b: (b, layer, 0, 0, 0, 0))],
        out_shape=[jax.ShapeDtypeStruct((n_seq * seq_len, D_RWKV), F32),
                   jax.ShapeDtypeStruct((n_seq, DEPTH, 2, H_RWKV, HEAD_DIM, HEAD_DIM), F32)],
        scratch_shapes=[tok] * 3 + [tok2] * 4 + [tok, pltpu.VMEM(st_shape, F32)],
        input_output_aliases={} if states is None else {len(args) - 1: 1},
        compiler_params=_cp(1),
    )(*args)


def _rwkv_weights(p, l):
    z = functools.partial(jnp.zeros, dtype=F32)
    wdec = z((LANES, 2 * D_RWKV))
    wdec = wdec.at[0:LORA_W, :D_RWKV].set(p['decay_up'][l, 0])
    wdec = wdec.at[LORA_W:2 * LORA_W, D_RWKV:].set(p['decay_up'][l, 1])
    wicl = z((LANES, 2 * D_RWKV))
    wicl = wicl.at[2 * LORA_W:2 * LORA_W + LORA_A, :D_RWKV].set(p['iclr_up'][l, 0])
    wicl = wicl.at[2 * LORA_W + LORA_A:2 * LORA_W + 2 * LORA_A, D_RWKV:].set(p['iclr_up'][l, 1])
    wg = z((LANES, D_RWKV)).at[0:LORA_G].set(p['gate_up'][l])
    w0a0 = jnp.stack([p['decay_w0'][l].reshape(-1), p['iclr_a0'][l].reshape(-1)])
    vec = jnp.stack([p['k_k'][l], p['k_a'][l], p['r_k'][l].reshape(-1), p['lnx_g'][l], p['lnx_b'][l],
                     z((D_RWKV,)), z((D_RWKV,)), z((D_RWKV,))])
    head = np.arange(LANES) // HEAD_DIM
    bo = jnp.asarray(head[:, None] == head[None, :], BF16)
    idx = np.arange(CHUNK)
    tril = jnp.asarray(idx[None, :] <= idx[:, None], BF16)
    triu = jnp.asarray(idx[None, :] >= idx[:, None], BF16)
    return [wdec.astype(BF16), wicl.astype(BF16), wg.astype(BF16), w0a0, vec, bo, tril, triu]


def _rope(x, cos, sin):
    lane = lax.broadcasted_iota(jnp.int32, (1, LANES), 1)
    first_half = (lane & ROPE_PAIRS) == 0
    partner = jnp.where(first_half, pltpu.roll(x, LANES - ROPE_PAIRS, axis=1),
                        pltpu.roll(x, ROPE_PAIRS, axis=1))
    return x * cos + partner * sin


def _attn_body(has_ctx, lam_init, *refs):
    if has_ctx:
        (q_ref, k_ref, v_ref, lp_ref, sg_ref, kc_ref, vc_ref, cq_ref, sq_ref, ck_ref, sk_ref,
         o_ref) = refs
    else:
        q_ref, k_ref, v_ref, lp_ref, sg_ref, o_ref = refs
    n_pair = D_DIFF // LANES
    lp = lp_ref[...]
    lam = (jnp.exp(jnp.sum(lp[0:1] * lp[1:2], axis=-1, keepdims=True))
           - jnp.exp(jnp.sum(lp[2:3] * lp[3:4], axis=-1, keepdims=True)) + lam_init)
    lane = lax.broadcasted_iota(jnp.int32, (1, LANES), 1)

    tq = q_ref.shape[0] // k_ref.shape[0]

    def front(seq, pairs):
        qs, ks, vts = {}, {}, {}
        for p in pairs:
            sl = slice(p * LANES, (p + 1) * LANES)
            q = q_ref[seq * tq:(seq + 1) * tq, sl]
            k = k_ref[seq, :, sl]
            v = v_ref[seq, :, sl]
            if has_ctx:
                q = _rope(q, cq_ref[...], sq_ref[...])
                k = _rope(k, ck_ref[...], sk_ref[...])
                k = jnp.concatenate([kc_ref[0, :, sl], k], axis=0)
                v = jnp.concatenate([vc_ref[0, :, sl], v], axis=0)
            qs[p] = q * (D_QK ** -0.5 * math.log2(math.e))
            ks[p] = k.astype(BF16)
            vt = v.T
            ones = jnp.ones((ONES_ROWS, v.shape[0]), F32)
            vts[p] = [jnp.concatenate([vt[h * HEAD_DIM:(h + 1) * HEAD_DIM], ones], axis=0).astype(BF16)
                      for h in range(2)]
        chains = [(p, h, m) for p in pairs for h in range(2) for m in range(2)]
        scores = []
        for p, h, m in chains:
            lo = h * HEAD_DIM + m * D_QK
            sel = (lane >= lo) & (lane < lo + D_QK)
            qm = jnp.where(sel, qs[p], 0.0).astype(BF16)
            n_keys = ks[p].shape[0]
            scores.append([_bdot_nt(ks[p][j:j + KEY_BLOCK], qm) for j in range(0, n_keys, KEY_BLOCK)])
        return seq, pairs, chains, scores, vts

    def back(seq, pairs, chains, scores, vts):
        mx = [functools.reduce(jnp.maximum, [jnp.max(b, axis=0, keepdims=True) for b in s]) for s in scores]
        es = [[jnp.exp2(b - mx[i]).astype(BF16) for b in s] for i, s in enumerate(scores)]
        pv = [sum(jnp.dot(vts[p][h][:, j * KEY_BLOCK:(j + 1) * KEY_BLOCK], e, preferred_element_type=F32)
                  for j, e in enumerate(es[i])) for i, (p, h, m) in enumerate(chains)]
        for j, p in enumerate(pairs):
            halves = []
            for h in range(2):
                i = (j * 2 + h) * 2
                rows = slice(0, HEAD_DIM)
                o = (pv[i][rows] * (1.0 / pv[i][HEAD_DIM:HEAD_DIM + 1])
                     - lam * (pv[i + 1][rows] * (1.0 / pv[i + 1][HEAD_DIM:HEAD_DIM + 1])))
                ms = jnp.mean(o * o, axis=0, keepdims=True)
                halves.append(o * lax.rsqrt(ms + SUBLN_EPS))
            o_ref[seq * tq:(seq + 1) * tq, p * LANES:(p + 1) * LANES] = (
                jnp.concatenate(halves, axis=0).T * sg_ref[...] * (1.0 - lam_init))

    pair_groups = [[p] for p in range(n_pair)] if has_ctx else [list(range(n_pair))]
    groups = [(seq, pairs) for seq in range(k_ref.shape[0]) for pairs in pair_groups]
    pending = front(*groups[0])
    for nxt in groups[1:]:
        ahead = front(*nxt)
        back(*pending)
        pending = ahead
    back(*pending)


def _attention(q, k_all, v_all, layer, n_sub, lam_init, lp, sg, ctx=None):
    n_tok = q.shape[0]
    n_seq, _, seq_len, _ = k_all.shape
    tq = ATTN_TQ
    nq = seq_len // tq
    assert n_sub == 1 or nq == 1
    kv_spec = pl.BlockSpec((n_sub, None, seq_len, D_DIFF), lambda b, i: (b, layer, 0, 0))
    in_specs = [pl.BlockSpec((n_sub * tq, D_DIFF), lambda b, i: (b * nq + i, 0)),
                kv_spec, kv_spec,
                pl.BlockSpec(lp.shape, lambda b, i: (0, 0)),
                pl.BlockSpec((1, LANES), lambda b, i: (0, 0))]
    args = [q, k_all, v_all, lp, sg]
    if ctx is not None:
        kc, vc, cos, sin = ctx
        past = kc.shape[1]
        in_specs += [pl.BlockSpec((1, past, D_DIFF), lambda b, i: (b, 0, 0)),
                     pl.BlockSpec((1, past, D_DIFF), lambda b, i: (b, 0, 0)),
                     pl.BlockSpec((tq, LANES), lambda b, i: (i, 0)),
                     pl.BlockSpec((tq, LANES), lambda b, i: (i, 0)),
                     pl.BlockSpec((seq_len, LANES), lambda b, i: (0, 0)),
                     pl.BlockSpec((seq_len, LANES), lambda b, i: (0, 0))]
        args += [kc, vc, cos, sin, cos, sin]
    return pl.pallas_call(
        functools.partial(_attn_body, ctx is not None, lam_init),
        grid=(n_seq // n_sub, nq),
        in_specs=in_specs,
        out_specs=pl.BlockSpec((n_sub * tq, D_DIFF), lambda b, i: (b * nq + i, 0)),
        out_shape=jax.ShapeDtypeStruct((n_tok, D_DIFF), F32),
        compiler_params=_cp(2),
    )(*args)


def _rope_tables(seq_len):
    t = jnp.arange(seq_len)
    pos = jnp.stack([(t // GRID_W).astype(F32), (t % GRID_W).astype(F32)], axis=1)
    inv = 1.0 / (ROPE_BASE ** (jnp.arange(ROPE_PAIRS, dtype=F32) / ROPE_PAIRS))
    ang = pos[:, :, None] * inv
    d = np.arange(LANES) % D_QK
    axis = d // (2 * ROPE_PAIRS)
    second = (d % (2 * ROPE_PAIRS)) // ROPE_PAIRS
    idx = d % ROPE_PAIRS
    cos = jnp.cos(ang)[:, axis, idx]
    sin = jnp.sin(ang)[:, axis, idx] * jnp.asarray(np.where(second == 1, 1.0, -1.0), F32)
    return cos, sin


def _fnet_body(seq_len, x_ref, ct_ref, st_ref, cc_ref, sc_ref, o_ref):
    n_sub = x_ref.shape[0] // seq_len
    x = x_ref[...].astype(BF16)
    xc = jnp.dot(x, cc_ref[...], preferred_element_type=F32)
    xs = jnp.dot(x, sc_ref[...], preferred_element_type=F32)
    wide = lambda a: jnp.concatenate([a[s * seq_len:(s + 1) * seq_len] for s in range(n_sub)], axis=1)
    y = _bdot(ct_ref[...], wide(xc)) - _bdot(st_ref[...], wide(xs))
    for s in range(n_sub):
        o_ref[s * seq_len:(s + 1) * seq_len, :] = y[:, s * D_FNET:(s + 1) * D_FNET]


def _dft_consts(n, block=1):
    idx = np.arange(n)
    ang = 2.0 * np.pi * ((idx[:, None] * idx[None, :]) % n) / n
    return [jnp.asarray(np.kron(np.eye(block), m).astype(np.float32)).astype(BF16)
            for m in (np.cos(ang) / np.sqrt(n), np.sin(ang) / np.sqrt(n))]


def _fnet(u_f, seq_len, n_sub):
    n_tok = u_f.shape[0]
    rows = n_sub * seq_len
    consts = _dft_consts(seq_len) + _dft_consts(FNET_GROUP_DIM, FNET_GROUPS)
    const = lambda b: (0, 0)
    return pl.pallas_call(
        functools.partial(_fnet_body, seq_len),
        grid=(n_tok // rows,),
        in_specs=[pl.BlockSpec((rows, D_FNET), lambda b: (b, 0))]
        + [pl.BlockSpec(c.shape, const) for c in consts],
        out_specs=pl.BlockSpec((rows, D_FNET), lambda b: (b, 0)),
        out_shape=jax.ShapeDtypeStruct((n_tok, D_FNET), F32),
        compiler_params=_cp(1),
    )(u_f, *consts)


def _ffn_body(final, yr_ref, yd_ref, yf_ref, x_ref, g1_ref, sh2_ref, sc2_ref, g2_ref, n2_ref, fg_ref,
              wo_ref, wi_ref, wf_ref, o_ref):
    part = x_ref.shape[0] // N_PARTS
    parts = [slice(j * part, (j + 1) * part) for j in range(N_PARTS)]
    y = [_bdot(yr_ref[r, :], wo_ref[0:D_RWKV, :])
         + _bdot(yd_ref[r, :], wo_ref[D_RWKV:D_RWKV + D_DIFF, :])
         + _bdot(yf_ref[r, :], wo_ref[D_RWKV + D_DIFF:, :]) for r in parts]
    x = [x_ref[r, :] + g1_ref[0] * y[j] for j, r in enumerate(parts)]
    h = [(_rms(xj, n2_ref[...]) * (1.0 + sc2_ref[0]) + sh2_ref[0]).astype(BF16) for xj in x]
    z = [jnp.dot(hj, wi_ref[...], preferred_element_type=F32) for hj in h]
    act = [zj[:, :D_FF] * jax.nn.sigmoid(zj[:, :D_FF]) * zj[:, D_FF:] for zj in z]
    f = [_bdot(aj, wf_ref[...]) for aj in act]
    for j, r in enumerate(parts):
        xj = x[j] + g2_ref[0] * f[j]
        o_ref[r, :] = _rms(xj, fg_ref[...]) if final else xj


def _ffn(y_r, y_d, y_f, x, mod, layer, row_fn, n2, fg, wo, wi, wf, final, tm):
    n_tok = x.shape[0]
    row = lambda i: (i, 0)
    const = lambda i: (0, 0)
    return pl.pallas_call(
        functools.partial(_ffn_body, final),
        grid=(n_tok // tm,),
        in_specs=[pl.BlockSpec((tm, D_RWKV), row),
                  pl.BlockSpec((tm, D_DIFF), row),
                  pl.BlockSpec((tm, D_FNET), row),
                  pl.BlockSpec((tm, D_MODEL), row),
                  _mod_spec(layer, 2, row_fn),
                  _mod_spec(layer, 3, row_fn),
                  _mod_spec(layer, 4, row_fn),
                  _mod_spec(layer, 5, row_fn),
                  pl.BlockSpec((1, D_MODEL), const),
                  pl.BlockSpec((1, D_MODEL), const),
                  pl.BlockSpec((None,) + wo.shape[1:], lambda i: (layer, 0, 0), pipeline_mode=pl.Buffered(1)),
                  pl.BlockSpec((None,) + wi.shape[1:], lambda i: (layer, 0, 0), pipeline_mode=pl.Buffered(1)),
                  pl.BlockSpec((None,) + wf.shape[1:], lambda i: (layer, 0, 0), pipeline_mode=pl.Buffered(1))],
        out_specs=pl.BlockSpec((tm, D_MODEL), row),
        out_shape=jax.ShapeDtypeStruct((n_tok, D_MODEL), F32),
        compiler_params=_cp(1),
    )(y_r, y_d, y_f, x, mod, mod, mod, mod, n2.reshape(1, D_MODEL), fg.reshape(1, D_MODEL), wo, wi, wf)


def _block_diag_state(s):
    b = s.shape[0]
    s = s.reshape(b, 2, N_PAIR, 2, HEAD_DIM, HEAD_DIM)
    eye = jnp.eye(2, dtype=s.dtype)
    s = s[:, :, :, :, :, None, :] * eye[None, None, None, :, None, :, None]
    return s.reshape(b, 2, N_PAIR, LANES, LANES)


def kernel(x_prompt, x_sample, c, state_rwkv, cache_diff_k, cache_diff_v, c_ctx, norm1_g, norm2_g, final_norm_g, w_mod, b_mod, w_in, w_out, shift_mu, decay_w0, decay_up, iclr_a0, iclr_up, gate_up, k_k, k_a, r_k, lnx_g, lnx_b, diff_lambda, subln_g, w_ffn_in, w_ffn_out):
    p = dict(shift_mu=shift_mu, decay_w0=decay_w0, decay_up=decay_up, iclr_a0=iclr_a0, iclr_up=iclr_up,
             gate_up=gate_up, k_k=k_k, k_a=k_a, r_k=r_k, lnx_g=lnx_g, lnx_b=lnx_b)
    n_ctx, t_ctx, _ = x_prompt.shape
    n_dec, t_dec, _ = x_sample.shape
    past = cache_diff_k.shape[2]

    cond = jnp.concatenate([c_ctx[None, :], c, jnp.zeros((MOD_ROWS - 1 - n_dec, D_MODEL), F32)], axis=0)
    mod = _modulation(cond, w_mod, b_mod).reshape(DEPTH * MOD_ROWS, 1, 6 * D_MODEL)

    tm_ffn = FFN_TM
    ctx_plan = _stream_plan(n_ctx, t_ctx)
    dec_plan = _stream_plan(n_dec, t_dec)
    streams = [
        dict(x=x_prompt.reshape(n_ctx * t_ctx, D_MODEL), t=t_ctx, n=n_ctx, **ctx_plan,
             row_in=lambda i: 0, row_ffn=lambda i: 0),
        dict(x=x_sample.reshape(n_dec * t_dec, D_MODEL), t=t_dec, n=n_dec, **dec_plan,
             row_in=lambda i: 1 + i // (t_dec // dec_plan['tm_in']), row_ffn=lambda i: 1 + i // (t_dec // tm_ffn)),
    ]
    cos, sin = _rope_tables(t_dec)
    for st in streams:
        st['kv'] = st['states'] = None
    w_in_l = jnp.concatenate(
        [w_in[:, :, :N_RWKV_IN].astype(BF16), jnp.zeros((DEPTH, D_MODEL, N_RWKV_PAD - N_RWKV_IN), BF16),
         w_in[:, :, N_RWKV_IN:].astype(BF16)], axis=2)
    wo = w_out.astype(BF16)
    wi = w_ffn_in.astype(BF16)
    wf = w_ffn_out.astype(BF16)
    for l in range(DEPTH):
        rw = _rwkv_weights(p, l)
        mu = jnp.concatenate([shift_mu[l], jnp.zeros((2, N_RWKV_PAD - N_RWKV_IN), F32)], axis=1)
        lam_init = 0.8 - 0.6 * math.exp(-0.3 * l)
        sg = jnp.tile(subln_g[l], 2).reshape(1, LANES)
        for si, st in enumerate(streams):
            u_r, q, k_all, v_all, u_f = _inproj(st['x'], mod, l, st['row_in'], norm1_g[l], w_in_l, mu,
                                                 st['kv'], st['t'], st['tm_in'])
            st['kv'] = (k_all, v_all)
            if si == 0:
                s0 = None
                attn_ctx = None
            else:
                s0 = state_rwkv[:, l].astype(F32)
                attn_ctx = (cache_diff_k[:, l].reshape(n_dec, past, D_DIFF).astype(F32),
                            cache_diff_v[:, l].reshape(n_dec, past, D_DIFF).astype(F32), cos, sin)
            y_r, st['states'] = _rwkv(u_r, s0, st['t'], st['n_sub'], rw, st['states'], l)
            y_d = _attention(q, k_all, v_all, l, st['n_attn'], lam_init, diff_lambda[l], sg, attn_ctx)
            y_f = _fnet(u_f, st['t'], st['n_fnet'])
            st['x'] = _ffn(y_r, y_d, y_f, st['x'], mod, l, st['row_ffn'], norm2_g[l], final_norm_g,
                           wo, wi, wf, l == DEPTH - 1, tm_ffn)
    y_prompt = streams[0]['x'].reshape(n_ctx, t_ctx, D_MODEL)
    y_sample = streams[1]['x'].reshape(n_dec, t_dec, D_MODEL)
    new_k = streams[0]['kv'][0].reshape(n_ctx, DEPTH, t_ctx, H_DIFF, 2, D_QK)
    new_v = streams[0]['kv'][1].reshape(n_ctx, DEPTH, t_ctx, H_DIFF, HEAD_DIM)
    return (y_prompt, y_sample, streams[0]['states'], new_k, new_v)
```

```python
import functools
import math

import numpy as np
import jax
import jax.numpy as jnp
from jax import lax
from jax.experimental import pallas as pl
from jax.experimental.pallas import tpu as pltpu

F32 = jnp.float32
BF16 = jnp.bfloat16

D_MODEL = 1024
DEPTH = 2
GRID_W = 64
HEAD_DIM = 64
D_RWKV = 384
H_RWKV = D_RWKV // HEAD_DIM
D_DIFF = 384
H_DIFF = D_DIFF // HEAD_DIM
D_QK = HEAD_DIM // 2
D_FNET = D_MODEL - D_RWKV - D_DIFF
FNET_GROUPS = 4
FNET_GROUP_DIM = D_FNET // FNET_GROUPS
LORA_W = 32
LORA_A = 32
LORA_G = 64
N_RWKV_IN = 3 * D_RWKV + 2 * LORA_W + 2 * LORA_A + LORA_G
N_DIFF_IN = 3 * D_DIFF
D_FF = ((8 * D_MODEL + 3 * 256 - 1) // (3 * 256)) * 256
ROPE_PAIRS = D_QK // 4
ROPE_BASE = 10000.0
RMS_EPS = 1e-6
GN_EPS = 64e-5
SUBLN_EPS = 1e-5
DECAY_SCALE = math.exp(-0.5)

LANES = 128
N_RWKV_PAD = 11 * LANES
D_IN_PAD = N_RWKV_PAD + N_DIFF_IN + D_FNET
N_PAIR = H_RWKV // 2
CHUNK = 64
N_PARTS = 2
SIDE_WORK_MAX_UNITS = 12
MOD_ROWS = 8
ONES_ROWS = 16
VMEM_LIMIT = 60 * 1024 * 1024

MOD_TN = 1536
CAST_TK = 256
INPROJ_TM = 512
FFN_TM = 512
ATTN_TQ = 256
ATTN_STEP_ROWS = 1024
KEY_BLOCK = 512
FNET_STEP_ROWS = 1024
RWKV_STEP_ROWS = 1024
RWKV_ONE_STEP_ROWS = 2048


def _stream_plan(n_seq, seq_len):
    one_step = n_seq * seq_len <= RWKV_ONE_STEP_ROWS
    return dict(n_sub=n_seq if one_step else max(1, RWKV_STEP_ROWS // seq_len),
                n_attn=max(1, ATTN_STEP_ROWS // seq_len),
                n_fnet=max(1, FNET_STEP_ROWS // seq_len),
                tm_in=max(seq_len, INPROJ_TM))


def _cp(n_axes=1):
    return pltpu.CompilerParams(dimension_semantics=("arbitrary",) * n_axes,
                                vmem_limit_bytes=VMEM_LIMIT)


def _bdot(a, b):
    return jnp.dot(a.astype(BF16), b.astype(BF16), preferred_element_type=F32)


def _bdot_nt(a, b):
    return lax.dot_general(a.astype(BF16), b.astype(BF16), (((1,), (1,)), ((), ())),
                           preferred_element_type=F32)


def _split2(x):
    hi = x.astype(BF16)
    lo = (x - hi.astype(F32)).astype(BF16)
    return hi, lo


def _dot_x3(a, b):
    a_hi, a_lo = _split2(a)
    b_hi, b_lo = _split2(b)
    d = functools.partial(jnp.dot, preferred_element_type=F32)
    return d(a_hi, b_hi) + d(a_lo, b_hi) + d(a_hi, b_lo)


def _rms(x, g):
    return x * lax.rsqrt(jnp.mean(x * x, axis=-1, keepdims=True) + RMS_EPS) * g


def _cast_body(gap_at, x_ref, o_ref):
    n_in = x_ref.shape[-1]
    if gap_at is None:
        o_ref[...] = x_ref[...].astype(BF16)
    else:
        gap = o_ref.shape[-1] - n_in
        o_ref[:, :gap_at] = x_ref[:, :gap_at].astype(BF16)
        o_ref[:, gap_at:gap_at + gap] = jnp.zeros((o_ref.shape[0], gap), BF16)
        o_ref[:, gap_at + gap:] = x_ref[:, gap_at:].astype(BF16)


def _to_bf16(w, gap_at=None, gap=0):
    n_layers, k, n = w.shape
    tk = CAST_TK
    return pl.pallas_call(
        functools.partial(_cast_body, gap_at),
        grid=(n_layers, k // tk),
        in_specs=[pl.BlockSpec((None, tk, n), lambda l, i: (l, i, 0))],
        out_specs=pl.BlockSpec((None, tk, n + gap), lambda l, i: (l, i, 0)),
        out_shape=jax.ShapeDtypeStruct((n_layers, k, n + gap), BF16),
        compiler_params=_cp(2),
    )(w)


def _mod_body(c_ref, w_ref, b_ref, o_ref):
    c = c_ref[...]
    a = c * jax.nn.sigmoid(c)
    o_ref[0] = _dot_x3(a, w_ref[0]) + b_ref[0]


def _modulation(cond, w_mod, b_mod):
    n_layers, _, n_out = w_mod.shape
    tn = MOD_TN
    return pl.pallas_call(
        _mod_body,
        grid=(n_layers, n_out // tn),
        in_specs=[pl.BlockSpec((MOD_ROWS, D_MODEL), lambda l, j: (0, 0)),
                  pl.BlockSpec((1, D_MODEL, tn), lambda l, j: (l, 0, j)),
                  pl.BlockSpec((1, 1, tn), lambda l, j: (l, 0, j))],
        out_specs=pl.BlockSpec((1, MOD_ROWS, tn), lambda l, j: (l, 0, j)),
        out_shape=jax.ShapeDtypeStruct((n_layers, MOD_ROWS, n_out), F32),
        compiler_params=_cp(2),
    )(cond, w_mod, b_mod.reshape(n_layers, 1, n_out))


def _mod_spec(layer, col, row_fn):
    return pl.BlockSpec((1, 1, D_MODEL), lambda i: (layer * MOD_ROWS + row_fn(i), 0, col))


def _put_layer(ref, idx, layer, aliased, val):
    if aliased:
        ref[idx] = val
    else:
        for other in range(DEPTH):
            ref[idx + (other,)] = val if other == layer else jnp.zeros_like(val)


def _inproj_body(seq_len, layer, aliased, x_ref, g_ref, sh_ref, sc_ref, w_ref, mu_ref, *refs):
    ur_ref, q_ref, k_ref, v_ref, uf_ref = refs[2:] if aliased else refs
    part = x_ref.shape[0] // N_PARTS
    parts = [slice(j * part, (j + 1) * part) for j in range(N_PARTS)]
    h = [(_rms(x_ref[r, :], g_ref[...]) * (1.0 + sc_ref[0]) + sh_ref[0]).astype(BF16) for r in parts]
    u = jnp.concatenate([jnp.dot(hj, w_ref[...], preferred_element_type=F32) for hj in h], axis=0)
    ur = u[:, :N_RWKV_PAD]
    tm = ur.shape[0]
    pos = lax.broadcasted_iota(jnp.int32, (tm, 1), 0) & (seq_len - 1)
    prev = jnp.where(pos == 0, 0.0, pltpu.roll(ur, 1, axis=0))
    nxt = jnp.where(pos == seq_len - 1, 0.0, pltpu.roll(ur, tm - 1, axis=0))
    ur_ref[...] = ur + mu_ref[0:1, :] * (prev - ur) + mu_ref[1:2, :] * (nxt - ur)
    q_ref[...] = u[:, N_RWKV_PAD:N_RWKV_PAD + D_DIFF]
    for s in range(tm // seq_len):
        rows = slice(s * seq_len, (s + 1) * seq_len)
        _put_layer(k_ref, (s,), layer, aliased, u[rows, N_RWKV_PAD + D_DIFF:N_RWKV_PAD + 2 * D_DIFF])
        _put_layer(v_ref, (s,), layer, aliased, u[rows, N_RWKV_PAD + 2 * D_DIFF:N_RWKV_PAD + 3 * D_DIFF])
    uf_ref[...] = u[:, N_RWKV_PAD + N_DIFF_IN:]


def _inproj(x, mod, layer, row_fn, g, w, mu, kv, seq_len, tm):
    n_tok = x.shape[0]
    kv_shape = jax.ShapeDtypeStruct((n_tok // seq_len, DEPTH, seq_len, D_DIFF), F32)
    assert tm % seq_len == 0 and seq_len & (seq_len - 1) == 0
    row = lambda i: (i, 0)
    const = lambda i: (0, 0)
    if kv is None:
        kv_spec = pl.BlockSpec((tm // seq_len, DEPTH, seq_len, D_DIFF), lambda i: (i, 0, 0, 0))
    else:
        kv_spec = pl.BlockSpec((tm // seq_len, None, seq_len, D_DIFF), lambda i: (i, layer, 0, 0))
    return pl.pallas_call(
        functools.partial(_inproj_body, seq_len, layer, kv is not None),
        grid=(n_tok // tm,),
        in_specs=[pl.BlockSpec((tm, D_MODEL), row),
                  pl.BlockSpec((1, D_MODEL), const),
                  _mod_spec(layer, 0, row_fn),
                  _mod_spec(layer, 1, row_fn),
                  pl.BlockSpec((None, D_MODEL, D_IN_PAD), lambda i: (layer, 0, 0)),
                  pl.BlockSpec((2, N_RWKV_PAD), const)]
        + [pl.BlockSpec(memory_space=pl.ANY)] * (0 if kv is None else 2),
        out_specs=[pl.BlockSpec((tm, N_RWKV_PAD), row),
                   pl.BlockSpec((tm, D_DIFF), row),
                   kv_spec, kv_spec,
                   pl.BlockSpec((tm, D_FNET), row)],
        out_shape=[jax.ShapeDtypeStruct((n_tok, N_RWKV_PAD), F32),
                   jax.ShapeDtypeStruct((n_tok, D_DIFF), F32),
                   kv_shape, kv_shape,
                   jax.ShapeDtypeStruct((n_tok, D_FNET), F32)],
        input_output_aliases={} if kv is None else {6: 2, 7: 3},
        compiler_params=_cp(1),
    )(x, g.reshape(1, D_MODEL), mod, mod, w, mu, *(kv or ()))


def _lane_masks():
    lane = lax.broadcasted_iota(jnp.int32, (1, LANES), 1)
    return lane < HEAD_DIM, lane >= HEAD_DIM


def _rwkv_units(units, m0, m1, side=()):
    side = list(side)

    def run_side(drain=False):
        for gen in list(side):
            for _ in gen:
                if not drain:
                    break
            else:
                side.remove(gen)

    def bd(x):
        xb = x.astype(BF16)
        zero = jnp.zeros_like(xb)
        return jnp.concatenate([jnp.where(m0, xb, zero), jnp.where(m1, xb, zero)], axis=0)

    row = lax.broadcasted_iota(jnp.int32, (CHUNK, LANES), 0)
    col = lax.broadcasted_iota(jnp.int32, (CHUNK, LANES), 1) & (CHUNK - 1)
    eye = (col == row).astype(F32)
    r2 = lax.broadcasted_iota(jnp.int32, (LANES, LANES), 0) < HEAD_DIM
    c2 = lax.broadcasted_iota(jnp.int32, (LANES, LANES), 1) < HEAD_DIM
    rng = range(len(units))

    pre = []
    for rev, kk, r, v, kd, b, cum, cex, tot, s_prev in units:
        p_inv = jnp.exp(-cum)
        p_rem = jnp.exp(tot - cum)
        ab = -kk * jnp.exp(cex)
        rb = r * jnp.exp(cum)
        strict = (col > row) if rev else (col < row)
        incl = (col >= row) if rev else (col <= row)
        pre.append(dict(ab=ab, rb=rb, vbd=bd(v), strict=strict, incl=incl,
                        lhs=jnp.concatenate([ab, rb], axis=0),
                        rhs=jnp.concatenate([bd(b * p_inv), bd(kd * p_inv)], axis=0),
                        bk=jnp.concatenate([b * p_rem, kd * p_rem], axis=0)))

    run_side()
    mm = [_bdot_nt(q['lhs'], q['rhs']) for q in pre]
    run_side()
    m_ab = [jnp.where(pre[i]['strict'], mm[i][:CHUNK, :LANES], 0.0) for i in rng]
    m_ak = [jnp.where(pre[i]['strict'], mm[i][:CHUNK, LANES:], 0.0) for i in rng]
    m_r = [jnp.concatenate([jnp.where(pre[i]['incl'], mm[i][CHUNK:, :LANES], 0.0),
                            jnp.where(pre[i]['incl'], mm[i][CHUNK:, LANES:], 0.0)], axis=1) for i in rng]
    mv = [_bdot(m_ak[i], pre[i]['vbd']) for i in rng]
    run_side()

    t = [eye + m_ab[i] for i in rng]
    n = [_bdot(m_ab[i], bd(m_ab[i])) for i in rng]
    for _ in range(4):
        x = [_bdot(jnp.concatenate([t[i], n[i]], axis=0), bd(n[i])) for i in rng]
        t = [t[i] + x[i][:CHUNK] for i in rng]
        n = [x[i][CHUNK:] for i in rng]
        run_side()
    t = [t[i] + _bdot(t[i], bd(n[i])) for i in rng]

    w = [_bdot(t[i], jnp.concatenate([bd(pre[i]['ab']), bd(mv[i])], axis=1)) for i in rng]
    xs = [_bdot_nt(jnp.concatenate([w[i][:, :LANES], pre[i]['rb']], axis=0), units[i][9]) for i in rng]
    u = [xs[i][:CHUNK] + w[i][:, LANES:] for i in rng]
    run_side(drain=True)
    y = [xs[i][CHUNK:] + _bdot(m_r[i], jnp.concatenate([bd(u[i]), pre[i]['vbd']], axis=0)) for i in rng]
    z = [_bdot(jnp.concatenate([u[i], units[i][3]], axis=0).T, pre[i]['bk']) for i in rng]
    s_new = [units[i][9] * jnp.exp(units[i][8]) + jnp.where(r2 == c2, z[i], 0.0) for i in rng]
    return y, s_new


def _rwkv_body(seq_len, n_sub, layer, has_s0, aliased, u_ref, *refs):
    s0_ref = refs[0] if has_s0 else None
    refs = refs[1:] if has_s0 else refs
    wdec_ref, wicl_ref, wg_ref, w0a0_ref, vec_ref, bo_ref, tril_ref, triu_ref = refs[:8]
    (y_ref, sfin_ref, r_s, v_s, kk_s, kd_s, b_s, ci_s, ce_s, y_s, st_s) = refs[9 if aliased else 8:]
    n_chunk = seq_len // CHUNK
    assert n_chunk % 2 == 0 and n_chunk >= 4
    k_k = vec_ref[0:1, :]
    k_a = vec_ref[1:2, :]
    r_k = vec_ref[2:3, :]
    lnx_g = vec_ref[3:4, :]
    lnx_b = vec_ref[4:5, :]
    bo = bo_ref[...]

    def headsum(xb):
        return jnp.concatenate([jnp.dot(xb[:, p * LANES:(p + 1) * LANES], bo, preferred_element_type=F32)
                                for p in range(N_PAIR)], axis=1)

    def prep(rows):
        xs = u_ref[rows, :]
        r = xs[:, 0:D_RWKV]
        k = xs[:, D_RWKV:2 * D_RWKV]
        v = xs[:, 2 * D_RWKV:3 * D_RWKV]
        lora = xs[:, 3 * D_RWKV:3 * D_RWKV + LANES]
        t_lora = jnp.tanh(lora).astype(BF16)
        lora = lora.astype(BF16)
        kk = k * k_k
        kk2 = (kk * kk).astype(BF16)
        yield
        dec = _bdot(t_lora, wdec_ref[...])
        icl = _bdot(lora, wicl_ref[...])
        ss = headsum(kk2)
        yield
        logw = -DECAY_SCALE * jax.nn.sigmoid(w0a0_ref[0:1, :] + dec)
        a = jax.nn.sigmoid(w0a0_ref[1:2, :] + icl)
        kk = kk / jnp.maximum(jnp.sqrt(ss), 1e-12)
        a_f = a[:, :D_RWKV]
        a_b = a[:, D_RWKV:]
        kd_f = k * (1.0 + (a_f - 1.0) * k_a)
        kd_b = k * (1.0 + (a_b - 1.0) * k_a)
        lws = [_split2(logw[:, d * D_RWKV:(d + 1) * D_RWKV]) for d in range(2)]
        r_s[rows, :] = r
        v_s[rows, :] = v
        kk_s[rows, :] = kk
        kd_s[0, rows, :] = kd_f
        kd_s[1, rows, :] = kd_b
        b_s[0, rows, :] = kk * a_f
        b_s[1, rows, :] = kk * a_b
        yield
        dd = functools.partial(jnp.dot, preferred_element_type=F32)
        cums = [dd(tri_ref[...], lws[d][0]) + dd(tri_ref[...], lws[d][1])
                for d, tri_ref in enumerate((tril_ref, triu_ref))]
        yield
        for d in range(2):
            ci_s[d, rows, :] = cums[d]
            ce_s[d, rows, :] = cums[d] - logw[:, d * D_RWKV:(d + 1) * D_RWKV]

    def post(rows):
        y = y_s[rows, :]
        yb = y.astype(BF16)
        v = v_s[rows, :]
        bon = _split2(r_s[rows, :] * (kd_s[0, rows, :] + kd_s[1, rows, :]) * r_k)
        s_gd = jax.nn.sigmoid(u_ref[rows, 3 * D_RWKV + LANES:3 * D_RWKV + 2 * LANES]).astype(BF16)
        yield
        mean = headsum(yb) * (1.0 / HEAD_DIM)
        bonus = (headsum(bon[0]) + headsum(bon[1])) * v
        g = _bdot(s_gd, wg_ref[...])
        yield
        yc = y - mean
        yc2 = (yc * yc).astype(BF16)
        yield
        var = headsum(yc2) * (1.0 / HEAD_DIM)
        yield
        yn = yc * lax.rsqrt(var + GN_EPS) * lnx_g + lnx_b
        y_ref[rows, :] = (yn + bonus) * g

    def run_all(gens):
        gens = list(gens)
        while gens:
            gens = [gen for gen in gens if next(gen, gens) is not gens]

    def chunk_rows(i):
        return [pl.ds(pl.multiple_of(s * seq_len + (i if d == 0 else n_chunk - 1 - i) * CHUNK, CHUNK), CHUNK)
                for s in range(n_sub) for d in range(2)]

    st_s[...] = s0_ref[...] if has_s0 else jnp.zeros_like(st_s)
    y_s[...] = jnp.zeros_like(y_s)
    m0, m1 = _lane_masks()

    def scan_step(i, prep_next, post_prev):
        rows_sd = chunk_rows(i)
        side = []
        if prep_next:
            side = [prep(rows) for rows in chunk_rows(i + 1)]
        if post_prev:
            side = [post(rows) for rows in chunk_rows(i - 1)]
        units = []
        for s in range(n_sub):
            for d in range(2):
                rows = rows_sd[s * 2 + d]
                cum = ci_s[d, rows, :]
                cex = ce_s[d, rows, :]
                tot = cum[CHUNK - 1:CHUNK] if d == 0 else cum[0:1]
                kk = kk_s[rows, :]
                r = r_s[rows, :]
                v = v_s[rows, :]
                kd = kd_s[d, rows, :]
                b = b_s[d, rows, :]
                for p in range(N_PAIR):
                    sl = slice(p * LANES, (p + 1) * LANES)
                    units.append((d == 1, kk[:, sl], r[:, sl], v[:, sl], kd[:, sl], b[:, sl],
                                  cum[:, sl], cex[:, sl], tot[:, sl], st_s[s, d, p]))
        ys, s_new = _rwkv_units(units, m0, m1, side)
        for s in range(n_sub):
            for d in range(2):
                base = (s * 2 + d) * N_PAIR
                for p in range(N_PAIR):
                    st_s[s, d, p] = s_new[base + p]
                rows = rows_sd[s * 2 + d]
                y_s[rows, :] = y_s[rows, :] + jnp.concatenate(ys[base:base + N_PAIR], axis=1)

    def loop(lo, hi, **kw):
        lax.fori_loop(lo, hi, lambda i, c: (scan_step(i, **kw), c)[1], 0)

    half = n_chunk // 2
    if 2 * N_PAIR * n_sub <= SIDE_WORK_MAX_UNITS:
        run_all(prep(rows) for rows in chunk_rows(0))
        loop(0, half - 1, prep_next=True, post_prev=False)
        loop(half - 1, half + 1, prep_next=False, post_prev=False)
        loop(half + 1, n_chunk, prep_next=False, post_prev=True)
        run_all(post(rows) for rows in chunk_rows(n_chunk - 1))
    else:
        for i in range(half):
            run_all(prep(rows) for rows in chunk_rows(i))
        loop(0, n_chunk, prep_next=False, post_prev=False)
        for i in range(half):
            run_all(post(rows) for rows in chunk_rows(i))

    for s in range(n_sub):
        for d in range(2):
            for p in range(N_PAIR):
                st = st_s[s, d, p]
                for h in range(2):
                    rows = slice(h * HEAD_DIM, (h + 1) * HEAD_DIM)
                    val = st[rows, rows]
                    if aliased:
                        sfin_ref[s, d, 2 * p + h] = val
                    else:
                        for other in range(DEPTH):
                            sfin_ref[s, other, d, 2 * p + h] = val if other == layer else jnp.zeros_like(val)


def _rwkv(u_r, s0, seq_len, n_sub, wts, states, layer):
    n_seq = u_r.shape[0] // seq_len
    rows = n_sub * seq_len
    const2 = lambda b: (0, 0)
    st_shape = (n_sub, 2, N_PAIR, LANES, LANES)
    tok = pltpu.VMEM((rows, D_RWKV), F32)
    tok2 = pltpu.VMEM((2, rows, D_RWKV), F32)
    single = n_seq == n_sub
    in_specs = [pl.BlockSpec((rows, N_RWKV_PAD), lambda b: (b, 0),
                             pipeline_mode=pl.Buffered(1) if single else None)]
    args = [u_r]
    if s0 is not None:
        in_specs.append(pl.BlockSpec(st_shape, lambda b: (b, 0, 0, 0, 0)))
        args.append(_block_diag_state(s0))
    in_specs += [pl.BlockSpec(w.shape, const2) for w in wts]
    args += list(wts)
    if states is not None:
        in_specs.append(pl.BlockSpec(memory_space=pl.ANY))
        args.append(states)
    return pl.pallas_call(
        functools.partial(_rwkv_body, seq_len, n_sub, layer, s0 is not None, states is not None),
        grid=(n_seq // n_sub,),
        in_specs=in_specs,
        out_specs=[pl.BlockSpec((rows, D_RWKV), lambda b: (b, 0),
                                pipeline_mode=pl.Buffered(1) if single else None),
                   pl.BlockSpec((n_sub, DEPTH, 2, H_RWKV, HEAD_DIM, HEAD_DIM), lambda b: (b, 0, 0, 0, 0, 0))
                   if states is None else
                   pl.BlockSpec((n_sub, None, 2, H_RWKV, HEAD_DIM, HEAD_DIM),
                                lambda b: (b, layer, 0, 0, 0, 0))],
        out_shape=[jax.ShapeDtypeStruct((n_seq * seq_len, D_RWKV), F32),
                   jax.ShapeDtypeStruct((n_seq, DEPTH, 2, H_RWKV, HEAD_DIM, HEAD_DIM), F32)],
        scratch_shapes=[tok] * 3 + [tok2] * 4 + [tok, pltpu.VMEM(st_shape, F32)],
        input_output_aliases={} if states is None else {len(args) - 1: 1},
        compiler_params=_cp(1),
    )(*args)


def _rwkv_weights(p, l):
    z = functools.partial(jnp.zeros, dtype=F32)
    wdec = z((LANES, 2 * D_RWKV))
    wdec = wdec.at[0:LORA_W, :D_RWKV].set(p['decay_up'][l, 0])
    wdec = wdec.at[LORA_W:2 * LORA_W, D_RWKV:].set(p['decay_up'][l, 1])
    wicl = z((LANES, 2 * D_RWKV))
    wicl = wicl.at[2 * LORA_W:2 * LORA_W + LORA_A, :D_RWKV].set(p['iclr_up'][l, 0])
    wicl = wicl.at[2 * LORA_W + LORA_A:2 * LORA_W + 2 * LORA_A, D_RWKV:].set(p['iclr_up'][l, 1])
    wg = z((LANES, D_RWKV)).at[0:LORA_G].set(p['gate_up'][l])
    w0a0 = jnp.stack([p['decay_w0'][l].reshape(-1), p['iclr_a0'][l].reshape(-1)])
    vec = jnp.stack([p['k_k'][l], p['k_a'][l], p['r_k'][l].reshape(-1), p['lnx_g'][l], p['lnx_b'][l],
                     z((D_RWKV,)), z((D_RWKV,)), z((D_RWKV,))])
    head = np.arange(LANES) // HEAD_DIM
    bo = jnp.asarray(head[:, None] == head[None, :], BF16)
    idx = np.arange(CHUNK)
    tril = jnp.asarray(idx[None, :] <= idx[:, None], BF16)
    triu = jnp.asarray(idx[None, :] >= idx[:, None], BF16)
    return [wdec.astype(BF16), wicl.astype(BF16), wg.astype(BF16), w0a0, vec, bo, tril, triu]


def _rope(x, cos, sin):
    lane = lax.broadcasted_iota(jnp.int32, (1, LANES), 1)
    first_half = (lane & ROPE_PAIRS) == 0
    partner = jnp.where(first_half, pltpu.roll(x, LANES - ROPE_PAIRS, axis=1),
                        pltpu.roll(x, ROPE_PAIRS, axis=1))
    return x * cos + partner * sin


def _attn_body(has_ctx, lam_init, *refs):
    if has_ctx:
        (q_ref, k_ref, v_ref, lp_ref, sg_ref, kc_ref, vc_ref, cq_ref, sq_ref, ck_ref, sk_ref,
         o_ref) = refs
    else:
        q_ref, k_ref, v_ref, lp_ref, sg_ref, o_ref = refs
    n_pair = D_DIFF // LANES
    lp = lp_ref[...]
    lam = (jnp.exp(jnp.sum(lp[0:1] * lp[1:2], axis=-1, keepdims=True))
           - jnp.exp(jnp.sum(lp[2:3] * lp[3:4], axis=-1, keepdims=True)) + lam_init)
    lane = lax.broadcasted_iota(jnp.int32, (1, LANES), 1)

    tq = q_ref.shape[0] // k_ref.shape[0]

    def front(seq, pairs):
        qs, ks, vts = {}, {}, {}
        for p in pairs:
            sl = slice(p * LANES, (p + 1) * LANES)
            q = q_ref[seq * tq:(seq + 1) * tq, sl]
            k = k_ref[seq, :, sl]
            v = v_ref[seq, :, sl]
            if has_ctx:
                q = _rope(q, cq_ref[...], sq_ref[...])
                k = _rope(k, ck_ref[...], sk_ref[...])
                k = jnp.concatenate([kc_ref[0, :, sl], k], axis=0)
                v = jnp.concatenate([vc_ref[0, :, sl], v], axis=0)
            qs[p] = q * (D_QK ** -0.5 * math.log2(math.e))
            ks[p] = k.astype(BF16)
            vt = v.T
            ones = jnp.ones((ONES_ROWS, v.shape[0]), F32)
            vts[p] = [jnp.concatenate([vt[h * HEAD_DIM:(h + 1) * HEAD_DIM], ones], axis=0).astype(BF16)
                      for h in range(2)]
        chains = [(p, h, m) for p in pairs for h in range(2) for m in range(2)]
        scores = []
        for p, h, m in chains:
            lo = h * HEAD_DIM + m * D_QK
            sel = (lane >= lo) & (lane < lo + D_QK)
            qm = jnp.where(sel, qs[p], 0.0).astype(BF16)
            n_keys = ks[p].shape[0]
            scores.append([_bdot_nt(ks[p][j:j + KEY_BLOCK], qm) for j in range(0, n_keys, KEY_BLOCK)])
        return seq, pairs, chains, scores, vts

    def back(seq, pairs, chains, scores, vts):
        mx = [functools.reduce(jnp.maximum, [jnp.max(b, axis=0, keepdims=True) for b in s]) for s in scores]
        es = [[jnp.exp2(b - mx[i]).astype(BF16) for b in s] for i, s in enumerate(scores)]
        pv = [sum(jnp.dot(vts[p][h][:, j * KEY_BLOCK:(j + 1) * KEY_BLOCK], e, preferred_element_type=F32)
                  for j, e in enumerate(es[i])) for i, (p, h, m) in enumerate(chains)]
        for j, p in enumerate(pairs):
            halves = []
            for h in range(2):
                i = (j * 2 + h) * 2
                rows = slice(0, HEAD_DIM)
                o = (pv[i][rows] * (1.0 / pv[i][HEAD_DIM:HEAD_DIM + 1])
                     - lam * (pv[i + 1][rows] * (1.0 / pv[i + 1][HEAD_DIM:HEAD_DIM + 1])))
                ms = jnp.mean(o * o, axis=0, keepdims=True)
                halves.append(o * lax.rsqrt(ms + SUBLN_EPS))
            o_ref[seq * tq:(seq + 1) * tq, p * LANES:(p + 1) * LANES] = (
                jnp.concatenate(halves, axis=0).T * sg_ref[...] * (1.0 - lam_init))

    pair_groups = [[p] for p in range(n_pair)] if has_ctx else [list(range(n_pair))]
    groups = [(seq, pairs) for seq in range(k_ref.shape[0]) for pairs in pair_groups]
    pending = front(*groups[0])
    for nxt in groups[1:]:
        ahead = front(*nxt)
        back(*pending)
        pending = ahead
    back(*pending)


def _attention(q, k_all, v_all, layer, n_sub, lam_init, lp, sg, ctx=None):
    n_tok = q.shape[0]
    n_seq, _, seq_len, _ = k_all.shape
    tq = ATTN_TQ
    nq = seq_len // tq
    assert n_sub == 1 or nq == 1
    kv_spec = pl.BlockSpec((n_sub, None, seq_len, D_DIFF), lambda b, i: (b, layer, 0, 0))
    in_specs = [pl.BlockSpec((n_sub * tq, D_DIFF), lambda b, i: (b * nq + i, 0)),
                kv_spec, kv_spec,
                pl.BlockSpec(lp.shape, lambda b, i: (0, 0)),
                pl.BlockSpec((1, LANES), lambda b, i: (0, 0))]
    args = [q, k_all, v_all, lp, sg]
    if ctx is not None:
        kc, vc, cos, sin = ctx
        past = kc.shape[1]
        in_specs += [pl.BlockSpec((1, past, D_DIFF), lambda b, i: (b, 0, 0)),
                     pl.BlockSpec((1, past, D_DIFF), lambda b, i: (b, 0, 0)),
                     pl.BlockSpec((tq, LANES), lambda b, i: (i, 0)),
                     pl.BlockSpec((tq, LANES), lambda b, i: (i, 0)),
                     pl.BlockSpec((seq_len, LANES), lambda b, i: (0, 0)),
                     pl.BlockSpec((seq_len, LANES), lambda b, i: (0, 0))]
        args += [kc, vc, cos, sin, cos, sin]
    return pl.pallas_call(
        functools.partial(_attn_body, ctx is not None, lam_init),
        grid=(n_seq // n_sub, nq),
        in_specs=in_specs,
        out_specs=pl.BlockSpec((n_sub * tq, D_DIFF), lambda b, i: (b * nq + i, 0)),
        out_shape=jax.ShapeDtypeStruct((n_tok, D_DIFF), F32),
        compiler_params=_cp(2),
    )(*args)


def _rope_tables(seq_len):
    t = jnp.arange(seq_len)
    pos = jnp.stack([(t // GRID_W).astype(F32), (t % GRID_W).astype(F32)], axis=1)
    inv = 1.0 / (ROPE_BASE ** (jnp.arange(ROPE_PAIRS, dtype=F32) / ROPE_PAIRS))
    ang = pos[:, :, None] * inv
    d = np.arange(LANES) % D_QK
    axis = d // (2 * ROPE_PAIRS)
    second = (d % (2 * ROPE_PAIRS)) // ROPE_PAIRS
    idx = d % ROPE_PAIRS
    cos = jnp.cos(ang)[:, axis, idx]
    sin = jnp.sin(ang)[:, axis, idx] * jnp.asarray(np.where(second == 1, 1.0, -1.0), F32)
    return cos, sin


def _fnet_body(seq_len, x_ref, ct_ref, st_ref, cc_ref, sc_ref, o_ref):
    n_sub = x_ref.shape[0] // seq_len
    x = x_ref[...].astype(BF16)
    xc = jnp.dot(x, cc_ref[...], preferred_element_type=F32)
    xs = jnp.dot(x, sc_ref[...], preferred_element_type=F32)
    wide = lambda a: jnp.concatenate([a[s * seq_len:(s + 1) * seq_len] for s in range(n_sub)], axis=1)
    y = _bdot(ct_ref[...], wide(xc)) - _bdot(st_ref[...], wide(xs))
    for s in range(n_sub):
        o_ref[s * seq_len:(s + 1) * seq_len, :] = y[:, s * D_FNET:(s + 1) * D_FNET]


def _dft_consts(n, block=1):
    idx = np.arange(n)
    ang = 2.0 * np.pi * ((idx[:, None] * idx[None, :]) % n) / n
    return [jnp.asarray(np.kron(np.eye(block), m).astype(np.float32)).astype(BF16)
            for m in (np.cos(ang) / np.sqrt(n), np.sin(ang) / np.sqrt(n))]


def _fnet(u_f, seq_len, n_sub):
    n_tok = u_f.shape[0]
    rows = n_sub * seq_len
    consts = _dft_consts(seq_len) + _dft_consts(FNET_GROUP_DIM, FNET_GROUPS)
    const = lambda b: (0, 0)
    return pl.pallas_call(
        functools.partial(_fnet_body, seq_len),
        grid=(n_tok // rows,),
        in_specs=[pl.BlockSpec((rows, D_FNET), lambda b: (b, 0))]
        + [pl.BlockSpec(c.shape, const) for c in consts],
        out_specs=pl.BlockSpec((rows, D_FNET), lambda b: (b, 0)),
        out_shape=jax.ShapeDtypeStruct((n_tok, D_FNET), F32),
        compiler_params=_cp(1),
    )(u_f, *consts)


def _ffn_body(final, yr_ref, yd_ref, yf_ref, x_ref, g1_ref, sh2_ref, sc2_ref, g2_ref, n2_ref, fg_ref,
              wo_ref, wi_ref, wf_ref, o_ref):
    part = x_ref.shape[0] // N_PARTS
    parts = [slice(j * part, (j + 1) * part) for j in range(N_PARTS)]
    y = [_bdot(yr_ref[r, :], wo_ref[0:D_RWKV, :])
         + _bdot(yd_ref[r, :], wo_ref[D_RWKV:D_RWKV + D_DIFF, :])
         + _bdot(yf_ref[r, :], wo_ref[D_RWKV + D_DIFF:, :]) for r in parts]
    x = [x_ref[r, :] + g1_ref[0] * y[j] for j, r in enumerate(parts)]
    h = [(_rms(xj, n2_ref[...]) * (1.0 + sc2_ref[0]) + sh2_ref[0]).astype(BF16) for xj in x]
    z = [jnp.dot(hj, wi_ref[...], preferred_element_type=F32) for hj in h]
    act = [zj[:, :D_FF] * jax.nn.sigmoid(zj[:, :D_FF]) * zj[:, D_FF:] for zj in z]
    f = [_bdot(aj, wf_ref[...]) for aj in act]
    for j, r in enumerate(parts):
        xj = x[j] + g2_ref[0] * f[j]
        o_ref[r, :] = _rms(xj, fg_ref[...]) if final else xj


def _ffn(y_r, y_d, y_f, x, mod, layer, row_fn, n2, fg, wo, wi, wf, final, tm):
    n_tok = x.shape[0]
    row = lambda i: (i, 0)
    const = lambda i: (0, 0)
    return pl.pallas_call(
        functools.partial(_ffn_body, final),
        grid=(n_tok // tm,),
        in_specs=[pl.BlockSpec((tm, D_RWKV), row),
                  pl.BlockSpec((tm, D_DIFF), row),
                  pl.BlockSpec((tm, D_FNET), row),
                  pl.BlockSpec((tm, D_MODEL), row),
                  _mod_spec(layer, 2, row_fn),
                  _mod_spec(layer, 3, row_fn),
                  _mod_spec(layer, 4, row_fn),
                  _mod_spec(layer, 5, row_fn),
                  pl.BlockSpec((1, D_MODEL), const),
                  pl.BlockSpec((1, D_MODEL), const),
                  pl.BlockSpec((None,) + wo.shape[1:], lambda i: (layer, 0, 0), pipeline_mode=pl.Buffered(1)),
                  pl.BlockSpec((None,) + wi.shape[1:], lambda i: (layer, 0, 0), pipeline_mode=pl.Buffered(1)),
                  pl.BlockSpec((None,) + wf.shape[1:], lambda i: (layer, 0, 0), pipeline_mode=pl.Buffered(1))],
        out_specs=pl.BlockSpec((tm, D_MODEL), row),
        out_shape=jax.ShapeDtypeStruct((n_tok, D_MODEL), F32),
        compiler_params=_cp(1),
    )(y_r, y_d, y_f, x, mod, mod, mod, mod, n2.reshape(1, D_MODEL), fg.reshape(1, D_MODEL), wo, wi, wf)


def _block_diag_state(s):
    b = s.shape[0]
    s = s.reshape(b, 2, N_PAIR, 2, HEAD_DIM, HEAD_DIM)
    eye = jnp.eye(2, dtype=s.dtype)
    s = s[:, :, :, :, :, None, :] * eye[None, None, None, :, None, :, None]
    return s.reshape(b, 2, N_PAIR, LANES, LANES)


def kernel(x_prompt, x_sample, c, state_rwkv, cache_diff_k, cache_diff_v, c_ctx, norm1_g, norm2_g, final_norm_g, w_mod, b_mod, w_in, w_out, shift_mu, decay_w0, decay_up, iclr_a0, iclr_up, gate_up, k_k, k_a, r_k, lnx_g, lnx_b, diff_lambda, subln_g, w_ffn_in, w_ffn_out):
    p = dict(shift_mu=shift_mu, decay_w0=decay_w0, decay_up=decay_up, iclr_a0=iclr_a0, iclr_up=iclr_up,
             gate_up=gate_up, k_k=k_k, k_a=k_a, r_k=r_k, lnx_g=lnx_g, lnx_b=lnx_b)
    n_ctx, t_ctx, _ = x_prompt.shape
    n_dec, t_dec, _ = x_sample.shape
    past = cache_diff_k.shape[2]

    cond = jnp.concatenate([c_ctx[None, :], c, jnp.zeros((MOD_ROWS - 1 - n_dec, D_MODEL), F32)], axis=0)
    mod = _modulation(cond, w_mod, b_mod).reshape(DEPTH * MOD_ROWS, 1, 6 * D_MODEL)

    tm_ffn = FFN_TM
    ctx_plan = _stream_plan(n_ctx, t_ctx)
    dec_plan = _stream_plan(n_dec, t_dec)
    streams = [
        dict(x=x_prompt.reshape(n_ctx * t_ctx, D_MODEL), t=t_ctx, n=n_ctx, **ctx_plan,
             row_in=lambda i: 0, row_ffn=lambda i: 0),
        dict(x=x_sample.reshape(n_dec * t_dec, D_MODEL), t=t_dec, n=n_dec, **dec_plan,
             row_in=lambda i: 1 + i // (t_dec // dec_plan['tm_in']), row_ffn=lambda i: 1 + i // (t_dec // tm_ffn)),
    ]
    cos, sin = _rope_tables(t_dec)
    for st in streams:
        st['kv'] = st['states'] = None
    w_in_l = _to_bf16(w_in, N_RWKV_IN, N_RWKV_PAD - N_RWKV_IN)
    wo = _to_bf16(w_out)
    wi = _to_bf16(w_ffn_in)
    wf = _to_bf16(w_ffn_out)
    for l in range(DEPTH):
        rw = _rwkv_weights(p, l)
        mu = jnp.concatenate([shift_mu[l], jnp.zeros((2, N_RWKV_PAD - N_RWKV_IN), F32)], axis=1)
        lam_init = 0.8 - 0.6 * math.exp(-0.3 * l)
        sg = jnp.tile(subln_g[l], 2).reshape(1, LANES)
        for si, st in enumerate(streams):
            u_r, q, k_all, v_all, u_f = _inproj(st['x'], mod, l, st['row_in'], norm1_g[l], w_in_l, mu,
                                                 st['kv'], st['t'], st['tm_in'])
            st['kv'] = (k_all, v_all)
            if si == 0:
                s0 = None
                attn_ctx = None
            else:
                s0 = state_rwkv[:, l].astype(F32)
                attn_ctx = (cache_diff_k[:, l].reshape(n_dec, past, D_DIFF).astype(F32),
                            cache_diff_v[:, l].reshape(n_dec, past, D_DIFF).astype(F32), cos, sin)
            y_r, st['states'] = _rwkv(u_r, s0, st['t'], st['n_sub'], rw, st['states'], l)
            y_d = _attention(q, k_all, v_all, l, st['n_attn'], lam_init, diff_lambda[l], sg, attn_ctx)
            y_f = _fnet(u_f, st['t'], st['n_fnet'])
            st['x'] = _ffn(y_r, y_d, y_f, st['x'], mod, l, st['row_ffn'], norm2_g[l], final_norm_g,
                           wo, wi, wf, l == DEPTH - 1, tm_ffn)
    y_prompt = streams[0]['x'].reshape(n_ctx, t_ctx, D_MODEL)
    y_sample = streams[1]['x'].reshape(n_dec, t_dec, D_MODEL)
    new_k = streams[0]['kv'][0].reshape(n_ctx, DEPTH, t_ctx, H_DIFF, 2, D_QK)
    new_v = streams[0]['kv'][1].reshape(n_ctx, DEPTH, t_ctx, H_DIFF, HEAD_DIM)
    return (y_prompt, y_sample, streams[0]['states'], new_k, new_v)
```

```python
import functools
import math

import numpy as np
import jax
import jax.numpy as jnp
from jax import lax
from jax.experimental import pallas as pl
from jax.experimental.pallas import tpu as pltpu

F32 = jnp.float32
BF16 = jnp.bfloat16

D_MODEL = 1024
DEPTH = 2
GRID_W = 64
HEAD_DIM = 64
D_RWKV = 384
H_RWKV = D_RWKV // HEAD_DIM
D_DIFF = 384
H_DIFF = D_DIFF // HEAD_DIM
D_QK = HEAD_DIM // 2
D_FNET = D_MODEL - D_RWKV - D_DIFF
FNET_GROUPS = 4
FNET_GROUP_DIM = D_FNET // FNET_GROUPS
LORA_W = 32
LORA_A = 32
LORA_G = 64
N_RWKV_IN = 3 * D_RWKV + 2 * LORA_W + 2 * LORA_A + LORA_G
N_DIFF_IN = 3 * D_DIFF
D_FF = ((8 * D_MODEL + 3 * 256 - 1) // (3 * 256)) * 256
ROPE_PAIRS = D_QK // 4
ROPE_BASE = 10000.0
RMS_EPS = 1e-6
GN_EPS = 64e-5
SUBLN_EPS = 1e-5
DECAY_SCALE = math.exp(-0.5)
Q_SCALE = D_QK ** -0.5 * math.log2(math.e)

LANES = 128
N_RWKV_PAD = 11 * LANES
D_IN_PAD = N_RWKV_PAD + N_DIFF_IN + D_FNET
N_PAIR = H_RWKV // 2
CHUNK = 64
N_PARTS = 2
SIDE_WORK_MAX_UNITS = 12
MOD_ROWS = 8
ONES_ROWS = 16
VMEM_LIMIT = 60 * 1024 * 1024

MOD_TN = 1536
INPROJ_TM = 512
FFN_TM = 512
ATTN_TQ = 256
ATTN_STEP_ROWS = 1024
KEY_BLOCK = 512
FNET_STEP_ROWS = 1024
RWKV_STEP_ROWS = 1024
RWKV_ONE_STEP_ROWS = 2048


def _stream_plan(n_seq, seq_len):
    one_step = n_seq * seq_len <= RWKV_ONE_STEP_ROWS
    return dict(n_sub=n_seq if one_step else max(1, RWKV_STEP_ROWS // seq_len),
                n_attn=max(1, ATTN_STEP_ROWS // seq_len),
                n_fnet=max(1, FNET_STEP_ROWS // seq_len),
                tm_in=max(seq_len, INPROJ_TM))


def _cp(n_axes=1):
    return pltpu.CompilerParams(dimension_semantics=("arbitrary",) * n_axes,
                                vmem_limit_bytes=VMEM_LIMIT)


def _bdot(a, b):
    return jnp.dot(a.astype(BF16), b.astype(BF16), preferred_element_type=F32)


def _bdot_nt(a, b):
    return lax.dot_general(a.astype(BF16), b.astype(BF16), (((1,), (1,)), ((), ())),
                           preferred_element_type=F32)


def _split2(x):
    hi = x.astype(BF16)
    lo = (x - hi.astype(F32)).astype(BF16)
    return hi, lo


def _dot_x3(a, b):
    a_hi, a_lo = _split2(a)
    b_hi, b_lo = _split2(b)
    d = functools.partial(jnp.dot, preferred_element_type=F32)
    return d(a_hi, b_hi) + d(a_lo, b_hi) + d(a_hi, b_lo)


def _rms(x, g):
    return x * lax.rsqrt(jnp.mean(x * x, axis=-1, keepdims=True) + RMS_EPS) * g


def _mod_body(c_ref, w_ref, b_ref, o_ref):
    c = c_ref[...]
    a = c * jax.nn.sigmoid(c)
    o_ref[0] = _dot_x3(a, w_ref[0]) + b_ref[0]


def _modulation(cond, w_mod, b_mod):
    n_layers, _, n_out = w_mod.shape
    tn = MOD_TN
    return pl.pallas_call(
        _mod_body,
        grid=(n_layers, n_out // tn),
        in_specs=[pl.BlockSpec((MOD_ROWS, D_MODEL), lambda l, j: (0, 0)),
                  pl.BlockSpec((1, D_MODEL, tn), lambda l, j: (l, 0, j)),
                  pl.BlockSpec((1, 1, tn), lambda l, j: (l, 0, j))],
        out_specs=pl.BlockSpec((1, MOD_ROWS, tn), lambda l, j: (l, 0, j)),
        out_shape=jax.ShapeDtypeStruct((n_layers, MOD_ROWS, n_out), F32),
        compiler_params=_cp(2),
    )(cond, w_mod, b_mod.reshape(n_layers, 1, n_out))


def _mod_spec(layer, col, row_fn):
    return pl.BlockSpec((1, 1, D_MODEL), lambda i: (layer * MOD_ROWS + row_fn(i), 0, col))


def _put_layer(ref, idx, layer, aliased, val):
    if aliased:
        ref[idx] = val
    else:
        for other in range(DEPTH):
            ref[idx + (other,)] = val if other == layer else jnp.zeros_like(val)


def _inproj_body(seq_len, layer, aliased, x_ref, g_ref, sh_ref, sc_ref, w_ref, mu_ref, *refs):
    ur_ref, q_ref, k_ref, v_ref, uf_ref = refs[2:] if aliased else refs
    part = x_ref.shape[0] // N_PARTS
    parts = [slice(j * part, (j + 1) * part) for j in range(N_PARTS)]
    h = [(_rms(x_ref[r, :], g_ref[...]) * (1.0 + sc_ref[0]) + sh_ref[0]).astype(BF16) for r in parts]
    u = jnp.concatenate([jnp.dot(hj, w_ref[...], preferred_element_type=F32) for hj in h], axis=0)
    ur = u[:, :N_RWKV_PAD]
    tm = ur.shape[0]
    pos = lax.broadcasted_iota(jnp.int32, (tm, 1), 0) & (seq_len - 1)
    prev = jnp.where(pos == 0, 0.0, pltpu.roll(ur, 1, axis=0))
    nxt = jnp.where(pos == seq_len - 1, 0.0, pltpu.roll(ur, tm - 1, axis=0))
    ur_ref[...] = (ur + mu_ref[0:1, :] * (prev - ur) + mu_ref[1:2, :] * (nxt - ur)).astype(BF16)
    q_ref[...] = (u[:, N_RWKV_PAD:N_RWKV_PAD + D_DIFF] * Q_SCALE).astype(BF16)
    for s in range(tm // seq_len):
        rows = slice(s * seq_len, (s + 1) * seq_len)
        _put_layer(k_ref, (s,), layer, aliased, u[rows, N_RWKV_PAD + D_DIFF:N_RWKV_PAD + 2 * D_DIFF])
        _put_layer(v_ref, (s,), layer, aliased, u[rows, N_RWKV_PAD + 2 * D_DIFF:N_RWKV_PAD + 3 * D_DIFF])
    uf_ref[...] = u[:, N_RWKV_PAD + N_DIFF_IN:].astype(BF16)


def _inproj(x, mod, layer, row_fn, g, w, mu, kv, seq_len, tm):
    n_tok = x.shape[0]
    kv_shape = jax.ShapeDtypeStruct((n_tok // seq_len, DEPTH, seq_len, D_DIFF), F32)
    assert tm % seq_len == 0 and seq_len & (seq_len - 1) == 0
    row = lambda i: (i, 0)
    const = lambda i: (0, 0)
    if kv is None:
        kv_spec = pl.BlockSpec((tm // seq_len, DEPTH, seq_len, D_DIFF), lambda i: (i, 0, 0, 0))
    else:
        kv_spec = pl.BlockSpec((tm // seq_len, None, seq_len, D_DIFF), lambda i: (i, layer, 0, 0))
    return pl.pallas_call(
        functools.partial(_inproj_body, seq_len, layer, kv is not None),
        grid=(n_tok // tm,),
        in_specs=[pl.BlockSpec((tm, D_MODEL), row),
                  pl.BlockSpec((1, D_MODEL), const),
                  _mod_spec(layer, 0, row_fn),
                  _mod_spec(layer, 1, row_fn),
                  pl.BlockSpec((None, D_MODEL, D_IN_PAD), lambda i: (layer, 0, 0)),
                  pl.BlockSpec((2, N_RWKV_PAD), const)]
        + [pl.BlockSpec(memory_space=pl.ANY)] * (0 if kv is None else 2),
        out_specs=[pl.BlockSpec((tm, N_RWKV_PAD), row),
                   pl.BlockSpec((tm, D_DIFF), row),
                   kv_spec, kv_spec,
                   pl.BlockSpec((tm, D_FNET), row)],
        out_shape=[jax.ShapeDtypeStruct((n_tok, N_RWKV_PAD), BF16),
                   jax.ShapeDtypeStruct((n_tok, D_DIFF), BF16),
                   kv_shape, kv_shape,
                   jax.ShapeDtypeStruct((n_tok, D_FNET), BF16)],
        input_output_aliases={} if kv is None else {6: 2, 7: 3},
        compiler_params=_cp(1),
    )(x, g.reshape(1, D_MODEL), mod, mod, w, mu, *(kv or ()))


def _lane_masks():
    lane = lax.broadcasted_iota(jnp.int32, (1, LANES), 1)
    return lane < HEAD_DIM, lane >= HEAD_DIM


def _rwkv_units(units, m0, m1, side=()):
    side = list(side)

    def run_side(drain=False):
        for gen in list(side):
            for _ in gen:
                if not drain:
                    break
            else:
                side.remove(gen)

    def bd(x):
        xb = x.astype(BF16)
        zero = jnp.zeros_like(xb)
        return jnp.concatenate([jnp.where(m0, xb, zero), jnp.where(m1, xb, zero)], axis=0)

    row = lax.broadcasted_iota(jnp.int32, (CHUNK, LANES), 0)
    col = lax.broadcasted_iota(jnp.int32, (CHUNK, LANES), 1) & (CHUNK - 1)
    eye = (col == row).astype(F32)
    r2 = lax.broadcasted_iota(jnp.int32, (LANES, LANES), 0) < HEAD_DIM
    c2 = lax.broadcasted_iota(jnp.int32, (LANES, LANES), 1) < HEAD_DIM
    rng = range(len(units))

    pre = []
    for rev, kk, r, v, kd, b, cum, cex, tot, s_prev in units:
        p_inv = jnp.exp(-cum)
        p_rem = jnp.exp(tot - cum)
        ab = -kk * jnp.exp(cex)
        rb = r * jnp.exp(cum)
        strict = (col > row) if rev else (col < row)
        incl = (col >= row) if rev else (col <= row)
        pre.append(dict(ab=ab, rb=rb, vbd=bd(v), strict=strict, incl=incl,
                        lhs=jnp.concatenate([ab, rb], axis=0),
                        rhs=jnp.concatenate([bd(b * p_inv), bd(kd * p_inv)], axis=0),
                        bk=jnp.concatenate([b * p_rem, kd * p_rem], axis=0)))

    run_side()
    mm = [_bdot_nt(q['lhs'], q['rhs']) for q in pre]
    run_side()
    m_ab = [jnp.where(pre[i]['strict'], mm[i][:CHUNK, :LANES], 0.0) for i in rng]
    m_ak = [jnp.where(pre[i]['strict'], mm[i][:CHUNK, LANES:], 0.0) for i in rng]
    m_r = [jnp.concatenate([jnp.where(pre[i]['incl'], mm[i][CHUNK:, :LANES], 0.0),
                            jnp.where(pre[i]['incl'], mm[i][CHUNK:, LANES:], 0.0)], axis=1) for i in rng]
    mv = [_bdot(m_ak[i], pre[i]['vbd']) for i in rng]
    run_side()

    t = [eye + m_ab[i] for i in rng]
    n = [_bdot(m_ab[i], bd(m_ab[i])) for i in rng]
    for _ in range(4):
        x = [_bdot(jnp.concatenate([t[i], n[i]], axis=0), bd(n[i])) for i in rng]
        t = [t[i] + x[i][:CHUNK] for i in rng]
        n = [x[i][CHUNK:] for i in rng]
        run_side()
    t = [t[i] + _bdot(t[i], bd(n[i])) for i in rng]

    w = [_bdot(t[i], jnp.concatenate([bd(pre[i]['ab']), bd(mv[i])], axis=1)) for i in rng]
    xs = [_bdot_nt(jnp.concatenate([w[i][:, :LANES], pre[i]['rb']], axis=0), units[i][9]) for i in rng]
    u = [xs[i][:CHUNK] + w[i][:, LANES:] for i in rng]
    run_side(drain=True)
    y = [xs[i][CHUNK:] + _bdot(m_r[i], jnp.concatenate([bd(u[i]), pre[i]['vbd']], axis=0)) for i in rng]
    z = [_bdot(jnp.concatenate([u[i], units[i][3]], axis=0).T, pre[i]['bk']) for i in rng]
    s_new = [units[i][9] * jnp.exp(units[i][8]) + jnp.where(r2 == c2, z[i], 0.0) for i in rng]
    return y, s_new


def _rwkv_body(seq_len, n_sub, layer, has_s0, aliased, u_ref, *refs):
    s0_ref = refs[0] if has_s0 else None
    refs = refs[1:] if has_s0 else refs
    wdec_ref, wicl_ref, wg_ref, w0a0_ref, vec_ref, bo_ref, tril_ref, triu_ref = refs[:8]
    (y_ref, sfin_ref, r_s, v_s, kk_s, kd_s, b_s, ci_s, ce_s, y_s, st_s) = refs[9 if aliased else 8:]
    n_chunk = seq_len // CHUNK
    assert n_chunk % 2 == 0 and n_chunk >= 4
    k_k = vec_ref[0:1, :]
    k_a = vec_ref[1:2, :]
    r_k = vec_ref[2:3, :]
    lnx_g = vec_ref[3:4, :]
    lnx_b = vec_ref[4:5, :]
    bo = bo_ref[...]

    def headsum(xb):
        return jnp.concatenate([jnp.dot(xb[:, p * LANES:(p + 1) * LANES], bo, preferred_element_type=F32)
                                for p in range(N_PAIR)], axis=1)

    def prep(rows):
        xs = u_ref[rows, :].astype(F32)
        r = xs[:, 0:D_RWKV]
        k = xs[:, D_RWKV:2 * D_RWKV]
        v = xs[:, 2 * D_RWKV:3 * D_RWKV]
        lora = xs[:, 3 * D_RWKV:3 * D_RWKV + LANES]
        t_lora = jnp.tanh(lora).astype(BF16)
        lora = lora.astype(BF16)
        kk = k * k_k
        kk2 = (kk * kk).astype(BF16)
        yield
        dec = _bdot(t_lora, wdec_ref[...])
        icl = _bdot(lora, wicl_ref[...])
        ss = headsum(kk2)
        yield
        logw = -DECAY_SCALE * jax.nn.sigmoid(w0a0_ref[0:1, :] + dec)
        a = jax.nn.sigmoid(w0a0_ref[1:2, :] + icl)
        kk = kk / jnp.maximum(jnp.sqrt(ss), 1e-12)
        a_f = a[:, :D_RWKV]
        a_b = a[:, D_RWKV:]
        kd_f = k * (1.0 + (a_f - 1.0) * k_a)
        kd_b = k * (1.0 + (a_b - 1.0) * k_a)
        lws = [_split2(logw[:, d * D_RWKV:(d + 1) * D_RWKV]) for d in range(2)]
        r_s[rows, :] = r
        v_s[rows, :] = v
        kk_s[rows, :] = kk
        kd_s[0, rows, :] = kd_f
        kd_s[1, rows, :] = kd_b
        b_s[0, rows, :] = kk * a_f
        b_s[1, rows, :] = kk * a_b
        yield
        dd = functools.partial(jnp.dot, preferred_element_type=F32)
        cums = [dd(tri_ref[...], lws[d][0]) + dd(tri_ref[...], lws[d][1])
                for d, tri_ref in enumerate((tril_ref, triu_ref))]
        yield
        for d in range(2):
            ci_s[d, rows, :] = cums[d]
            ce_s[d, rows, :] = cums[d] - logw[:, d * D_RWKV:(d + 1) * D_RWKV]

    def post(rows):
        y = y_s[rows, :]
        yb = y.astype(BF16)
        v = v_s[rows, :]
        bon = _split2(r_s[rows, :] * (kd_s[0, rows, :] + kd_s[1, rows, :]) * r_k)
        gd = u_ref[rows, 3 * D_RWKV + LANES:3 * D_RWKV + 2 * LANES].astype(F32)
        s_gd = jax.nn.sigmoid(gd).astype(BF16)
        yield
        mean = headsum(yb) * (1.0 / HEAD_DIM)
        bonus = (headsum(bon[0]) + headsum(bon[1])) * v
        g = _bdot(s_gd, wg_ref[...])
        yield
        yc = y - mean
        yc2 = (yc * yc).astype(BF16)
        yield
        var = headsum(yc2) * (1.0 / HEAD_DIM)
        yield
        yn = yc * lax.rsqrt(var + GN_EPS) * lnx_g + lnx_b
        y_ref[rows, :] = ((yn + bonus) * g).astype(BF16)

    def run_all(gens):
        gens = list(gens)
        while gens:
            gens = [gen for gen in gens if next(gen, gens) is not gens]

    def chunk_rows(i):
        return [pl.ds(pl.multiple_of(s * seq_len + (i if d == 0 else n_chunk - 1 - i) * CHUNK, CHUNK), CHUNK)
                for s in range(n_sub) for d in range(2)]

    st_s[...] = s0_ref[...] if has_s0 else jnp.zeros_like(st_s)
    y_s[...] = jnp.zeros_like(y_s)
    m0, m1 = _lane_masks()

    def scan_step(i, prep_next, post_prev):
        rows_sd = chunk_rows(i)
        side = []
        if prep_next:
            side = [prep(rows) for rows in chunk_rows(i + 1)]
        if post_prev:
            side = [post(rows) for rows in chunk_rows(i - 1)]
        units = []
        for s in range(n_sub):
            for d in range(2):
                rows = rows_sd[s * 2 + d]
                cum = ci_s[d, rows, :]
                cex = ce_s[d, rows, :]
                tot = cum[CHUNK - 1:CHUNK] if d == 0 else cum[0:1]
                kk = kk_s[rows, :]
                r = r_s[rows, :]
                v = v_s[rows, :]
                kd = kd_s[d, rows, :]
                b = b_s[d, rows, :]
                for p in range(N_PAIR):
                    sl = slice(p * LANES, (p + 1) * LANES)
                    units.append((d == 1, kk[:, sl], r[:, sl], v[:, sl], kd[:, sl], b[:, sl],
                                  cum[:, sl], cex[:, sl], tot[:, sl], st_s[s, d, p]))
        ys, s_new = _rwkv_units(units, m0, m1, side)
        for s in range(n_sub):
            for d in range(2):
                base = (s * 2 + d) * N_PAIR
                for p in range(N_PAIR):
                    st_s[s, d, p] = s_new[base + p]
                rows = rows_sd[s * 2 + d]
                y_s[rows, :] = y_s[rows, :] + jnp.concatenate(ys[base:base + N_PAIR], axis=1)

    def loop(lo, hi, **kw):
        lax.fori_loop(lo, hi, lambda i, c: (scan_step(i, **kw), c)[1], 0)

    half = n_chunk // 2
    if 2 * N_PAIR * n_sub <= SIDE_WORK_MAX_UNITS:
        run_all(prep(rows) for rows in chunk_rows(0))
        loop(0, half - 1, prep_next=True, post_prev=False)
        loop(half - 1, half + 1, prep_next=False, post_prev=False)
        loop(half + 1, n_chunk, prep_next=False, post_prev=True)
        run_all(post(rows) for rows in chunk_rows(n_chunk - 1))
    else:
        for i in range(half):
            run_all(prep(rows) for rows in chunk_rows(i))
        loop(0, n_chunk, prep_next=False, post_prev=False)
        for i in range(half):
            run_all(post(rows) for rows in chunk_rows(i))

    for s in range(n_sub):
        for d in range(2):
            for p in range(N_PAIR):
                st = st_s[s, d, p]
                for h in range(2):
                    rows = slice(h * HEAD_DIM, (h + 1) * HEAD_DIM)
                    val = st[rows, rows]
                    if aliased:
                        sfin_ref[s, d, 2 * p + h] = val
                    else:
                        for other in range(DEPTH):
                            sfin_ref[s, other, d, 2 * p + h] = val if other == layer else jnp.zeros_like(val)


def _rwkv(u_r, s0, seq_len, n_sub, wts, states, layer):
    n_seq = u_r.shape[0] // seq_len
    rows = n_sub * seq_len
    const2 = lambda b: (0, 0)
    st_shape = (n_sub, 2, N_PAIR, LANES, LANES)
    tok = pltpu.VMEM((rows, D_RWKV), F32)
    tok2 = pltpu.VMEM((2, rows, D_RWKV), F32)
    single = n_seq == n_sub
    in_specs = [pl.BlockSpec((rows, N_RWKV_PAD), lambda b: (b, 0),
                             pipeline_mode=pl.Buffered(1) if single else None)]
    args = [u_r]
    if s0 is not None:
        in_specs.append(pl.BlockSpec(st_shape, lambda b: (b, 0, 0, 0, 0)))
        args.append(_block_diag_state(s0))
    in_specs += [pl.BlockSpec(w.shape, const2) for w in wts]
    args += list(wts)
    if states is not None:
        in_specs.append(pl.BlockSpec(memory_space=pl.ANY))
        args.append(states)
    return pl.pallas_call(
        functools.partial(_rwkv_body, seq_len, n_sub, layer, s0 is not None, states is not None),
        grid=(n_seq // n_sub,),
        in_specs=in_specs,
        out_specs=[pl.BlockSpec((rows, D_RWKV), lambda b: (b, 0),
                                pipeline_mode=pl.Buffered(1) if single else None),
                   pl.BlockSpec((n_sub, DEPTH, 2, H_RWKV, HEAD_DIM, HEAD_DIM), lambda b: (b, 0, 0, 0, 0, 0))
                   if states is None else
                   pl.BlockSpec((n_sub, None, 2, H_RWKV, HEAD_DIM, HEAD_DIM),
                                lambda b: (b, layer, 0, 0, 0, 0))],
        out_shape=[jax.ShapeDtypeStruct((n_seq * seq_len, D_RWKV), BF16),
                   jax.ShapeDtypeStruct((n_seq, DEPTH, 2, H_RWKV, HEAD_DIM, HEAD_DIM), F32)],
        scratch_shapes=[tok] * 3 + [tok2] * 4 + [tok, pltpu.VMEM(st_shape, F32)],
        input_output_aliases={} if states is None else {len(args) - 1: 1},
        compiler_params=_cp(1),
    )(*args)


def _rwkv_weights(p, l):
    z = functools.partial(jnp.zeros, dtype=F32)
    wdec = z((LANES, 2 * D_RWKV))
    wdec = wdec.at[0:LORA_W, :D_RWKV].set(p['decay_up'][l, 0])
    wdec = wdec.at[LORA_W:2 * LORA_W, D_RWKV:].set(p['decay_up'][l, 1])
    wicl = z((LANES, 2 * D_RWKV))
    wicl = wicl.at[2 * LORA_W:2 * LORA_W + LORA_A, :D_RWKV].set(p['iclr_up'][l, 0])
    wicl = wicl.at[2 * LORA_W + LORA_A:2 * LORA_W + 2 * LORA_A, D_RWKV:].set(p['iclr_up'][l, 1])
    wg = z((LANES, D_RWKV)).at[0:LORA_G].set(p['gate_up'][l])
    w0a0 = jnp.stack([p['decay_w0'][l].reshape(-1), p['iclr_a0'][l].reshape(-1)])
    vec = jnp.stack([p['k_k'][l], p['k_a'][l], p['r_k'][l].reshape(-1), p['lnx_g'][l], p['lnx_b'][l],
                     z((D_RWKV,)), z((D_RWKV,)), z((D_RWKV,))])
    head = np.arange(LANES) // HEAD_DIM
    bo = jnp.asarray(head[:, None] == head[None, :], BF16)
    idx = np.arange(CHUNK)
    tril = jnp.asarray(idx[None, :] <= idx[:, None], BF16)
    triu = jnp.asarray(idx[None, :] >= idx[:, None], BF16)
    return [wdec.astype(BF16), wicl.astype(BF16), wg.astype(BF16), w0a0, vec, bo, tril, triu]


def _rope(x, cos, sin):
    lane = lax.broadcasted_iota(jnp.int32, (1, LANES), 1)
    first_half = (lane & ROPE_PAIRS) == 0
    partner = jnp.where(first_half, pltpu.roll(x, LANES - ROPE_PAIRS, axis=1),
                        pltpu.roll(x, ROPE_PAIRS, axis=1))
    return x * cos + partner * sin


def _attn_body(has_ctx, lam_init, *refs):
    if has_ctx:
        (q_ref, k_ref, v_ref, lp_ref, sg_ref, kc_ref, vc_ref, cq_ref, sq_ref, ck_ref, sk_ref,
         o_ref) = refs
    else:
        q_ref, k_ref, v_ref, lp_ref, sg_ref, o_ref = refs
    n_pair = D_DIFF // LANES
    lp = lp_ref[...]
    lam = (jnp.exp(jnp.sum(lp[0:1] * lp[1:2], axis=-1, keepdims=True))
           - jnp.exp(jnp.sum(lp[2:3] * lp[3:4], axis=-1, keepdims=True)) + lam_init)
    lane = lax.broadcasted_iota(jnp.int32, (1, LANES), 1)

    tq = q_ref.shape[0] // k_ref.shape[0]

    def front(seq, pairs):
        qs, ks, vts = {}, {}, {}
        for p in pairs:
            sl = slice(p * LANES, (p + 1) * LANES)
            q = q_ref[seq * tq:(seq + 1) * tq, sl]
            k = k_ref[seq, :, sl]
            v = v_ref[seq, :, sl]
            if has_ctx:
                q = _rope(q.astype(F32), cq_ref[...], sq_ref[...])
                k = _rope(k, ck_ref[...], sk_ref[...])
                k = jnp.concatenate([kc_ref[0, :, sl], k], axis=0)
                v = jnp.concatenate([vc_ref[0, :, sl], v], axis=0)
            qs[p] = q.astype(BF16)
            ks[p] = k.astype(BF16)
            vt = v.T
            ones = jnp.ones((ONES_ROWS, v.shape[0]), F32)
            vts[p] = [jnp.concatenate([vt[h * HEAD_DIM:(h + 1) * HEAD_DIM], ones], axis=0).astype(BF16)
                      for h in range(2)]
        chains = [(p, h, m) for p in pairs for h in range(2) for m in range(2)]
        scores = []
        for p, h, m in chains:
            lo = h * HEAD_DIM + m * D_QK
            sel = (lane >= lo) & (lane < lo + D_QK)
            qm = jnp.where(sel, qs[p], jnp.zeros_like(qs[p]))
            n_keys = ks[p].shape[0]
            scores.append([_bdot_nt(ks[p][j:j + KEY_BLOCK], qm) for j in range(0, n_keys, KEY_BLOCK)])
        return seq, pairs, chains, scores, vts

    def back(seq, pairs, chains, scores, vts):
        mx = [functools.reduce(jnp.maximum, [jnp.max(b, axis=0, keepdims=True) for b in s]) for s in scores]
        es = [[jnp.exp2(b - mx[i]).astype(BF16) for b in s] for i, s in enumerate(scores)]
        pv = [sum(jnp.dot(vts[p][h][:, j * KEY_BLOCK:(j + 1) * KEY_BLOCK], e, preferred_element_type=F32)
                  for j, e in enumerate(es[i])) for i, (p, h, m) in enumerate(chains)]
        for j, p in enumerate(pairs):
            halves = []
            for h in range(2):
                i = (j * 2 + h) * 2
                rows = slice(0, HEAD_DIM)
                o = (pv[i][rows] * (1.0 / pv[i][HEAD_DIM:HEAD_DIM + 1])
                     - lam * (pv[i + 1][rows] * (1.0 / pv[i + 1][HEAD_DIM:HEAD_DIM + 1])))
                ms = jnp.mean(o * o, axis=0, keepdims=True)
                halves.append(o * lax.rsqrt(ms + SUBLN_EPS))
            o_ref[seq * tq:(seq + 1) * tq, p * LANES:(p + 1) * LANES] = (
                jnp.concatenate(halves, axis=0).T * sg_ref[...] * (1.0 - lam_init)).astype(BF16)

    pair_groups = [[p] for p in range(n_pair)] if has_ctx else [list(range(n_pair))]
    groups = [(seq, pairs) for seq in range(k_ref.shape[0]) for pairs in pair_groups]
    pending = front(*groups[0])
    for nxt in groups[1:]:
        ahead = front(*nxt)
        back(*pending)
        pending = ahead
    back(*pending)


def _attention(q, k_all, v_all, layer, n_sub, lam_init, lp, sg, ctx=None):
    n_tok = q.shape[0]
    n_seq, _, seq_len, _ = k_all.shape
    tq = ATTN_TQ
    nq = seq_len // tq
    assert n_sub == 1 or nq == 1
    kv_spec = pl.BlockSpec((n_sub, None, seq_len, D_DIFF), lambda b, i: (b, layer, 0, 0))
    in_specs = [pl.BlockSpec((n_sub * tq, D_DIFF), lambda b, i: (b * nq + i, 0)),
                kv_spec, kv_spec,
                pl.BlockSpec(lp.shape, lambda b, i: (0, 0)),
                pl.BlockSpec((1, LANES), lambda b, i: (0, 0))]
    args = [q, k_all, v_all, lp, sg]
    if ctx is not None:
        kc, vc, cos, sin = ctx
        past = kc.shape[1]
        in_specs += [pl.BlockSpec((1, past, D_DIFF), lambda b, i: (b, 0, 0)),
                     pl.BlockSpec((1, past, D_DIFF), lambda b, i: (b, 0, 0)),
                     pl.BlockSpec((tq, LANES), lambda b, i: (i, 0)),
                     pl.BlockSpec((tq, LANES), lambda b, i: (i, 0)),
                     pl.BlockSpec((seq_len, LANES), lambda b, i: (0, 0)),
                     pl.BlockSpec((seq_len, LANES), lambda b, i: (0, 0))]
        args += [kc, vc, cos, sin, cos, sin]
    return pl.pallas_call(
        functools.partial(_attn_body, ctx is not None, lam_init),
        grid=(n_seq // n_sub, nq),
        in_specs=in_specs,
        out_specs=pl.BlockSpec((n_sub * tq, D_DIFF), lambda b, i: (b * nq + i, 0)),
        out_shape=jax.ShapeDtypeStruct((n_tok, D_DIFF), BF16),
        compiler_params=_cp(2),
    )(*args)


def _rope_tables(seq_len):
    t = jnp.arange(seq_len)
    pos = jnp.stack([(t // GRID_W).astype(F32), (t % GRID_W).astype(F32)], axis=1)
    inv = 1.0 / (ROPE_BASE ** (jnp.arange(ROPE_PAIRS, dtype=F32) / ROPE_PAIRS))
    ang = pos[:, :, None] * inv
    d = np.arange(LANES) % D_QK
    axis = d // (2 * ROPE_PAIRS)
    second = (d % (2 * ROPE_PAIRS)) // ROPE_PAIRS
    idx = d % ROPE_PAIRS
    cos = jnp.cos(ang)[:, axis, idx]
    sin = jnp.sin(ang)[:, axis, idx] * jnp.asarray(np.where(second == 1, 1.0, -1.0), F32)
    return cos, sin


def _fnet_body(seq_len, x_ref, ct_ref, st_ref, cc_ref, sc_ref, o_ref):
    n_sub = x_ref.shape[0] // seq_len
    x = x_ref[...]
    xc = jnp.dot(x, cc_ref[...], preferred_element_type=F32)
    xs = jnp.dot(x, sc_ref[...], preferred_element_type=F32)
    wide = lambda a: jnp.concatenate([a[s * seq_len:(s + 1) * seq_len] for s in range(n_sub)], axis=1)
    y = _bdot(ct_ref[...], wide(xc)) - _bdot(st_ref[...], wide(xs))
    for s in range(n_sub):
        o_ref[s * seq_len:(s + 1) * seq_len, :] = y[:, s * D_FNET:(s + 1) * D_FNET].astype(BF16)


def _dft_consts(n, block=1):
    idx = np.arange(n)
    ang = 2.0 * np.pi * ((idx[:, None] * idx[None, :]) % n) / n
    return [jnp.asarray(np.kron(np.eye(block), m).astype(np.float32)).astype(BF16)
            for m in (np.cos(ang) / np.sqrt(n), np.sin(ang) / np.sqrt(n))]


def _fnet(u_f, seq_len, n_sub):
    n_tok = u_f.shape[0]
    rows = n_sub * seq_len
    consts = _dft_consts(seq_len) + _dft_consts(FNET_GROUP_DIM, FNET_GROUPS)
    const = lambda b: (0, 0)
    return pl.pallas_call(
        functools.partial(_fnet_body, seq_len),
        grid=(n_tok // rows,),
        in_specs=[pl.BlockSpec((rows, D_FNET), lambda b: (b, 0))]
        + [pl.BlockSpec(c.shape, const) for c in consts],
        out_specs=pl.BlockSpec((rows, D_FNET), lambda b: (b, 0)),
        out_shape=jax.ShapeDtypeStruct((n_tok, D_FNET), BF16),
        compiler_params=_cp(1),
    )(u_f, *consts)


def _ffn_body(final, yr_ref, yd_ref, yf_ref, x_ref, g1_ref, sh2_ref, sc2_ref, g2_ref, n2_ref, fg_ref,
              wo_ref, wi_ref, wf_ref, o_ref):
    part = x_ref.shape[0] // N_PARTS
    parts = [slice(j * part, (j + 1) * part) for j in range(N_PARTS)]
    y = [_bdot(yr_ref[r, :], wo_ref[0:D_RWKV, :])
         + _bdot(yd_ref[r, :], wo_ref[D_RWKV:D_RWKV + D_DIFF, :])
         + _bdot(yf_ref[r, :], wo_ref[D_RWKV + D_DIFF:, :]) for r in parts]
    x = [x_ref[r, :] + g1_ref[0] * y[j] for j, r in enumerate(parts)]
    h = [(_rms(xj, n2_ref[...]) * (1.0 + sc2_ref[0]) + sh2_ref[0]).astype(BF16) for xj in x]
    z = [jnp.dot(hj, wi_ref[...], preferred_element_type=F32) for hj in h]
    act = [zj[:, :D_FF] * jax.nn.sigmoid(zj[:, :D_FF]) * zj[:, D_FF:] for zj in z]
    f = [_bdot(aj, wf_ref[...]) for aj in act]
    for j, r in enumerate(parts):
        xj = x[j] + g2_ref[0] * f[j]
        o_ref[r, :] = _rms(xj, fg_ref[...]) if final else xj


def _ffn(y_r, y_d, y_f, x, mod, layer, row_fn, n2, fg, wo, wi, wf, final, tm):
    n_tok = x.shape[0]
    row = lambda i: (i, 0)
    const = lambda i: (0, 0)
    return pl.pallas_call(
        functools.partial(_ffn_body, final),
        grid=(n_tok // tm,),
        in_specs=[pl.BlockSpec((tm, D_RWKV), row),
                  pl.BlockSpec((tm, D_DIFF), row),
                  pl.BlockSpec((tm, D_FNET), row),
                  pl.BlockSpec((tm, D_MODEL), row),
                  _mod_spec(layer, 2, row_fn),
                  _mod_spec(layer, 3, row_fn),
                  _mod_spec(layer, 4, row_fn),
                  _mod_spec(layer, 5, row_fn),
                  pl.BlockSpec((1, D_MODEL), const),
                  pl.BlockSpec((1, D_MODEL), const),
                  pl.BlockSpec((None,) + wo.shape[1:], lambda i: (layer, 0, 0), pipeline_mode=pl.Buffered(1)),
                  pl.BlockSpec((None,) + wi.shape[1:], lambda i: (layer, 0, 0), pipeline_mode=pl.Buffered(1)),
                  pl.BlockSpec((None,) + wf.shape[1:], lambda i: (layer, 0, 0), pipeline_mode=pl.Buffered(1))],
        out_specs=pl.BlockSpec((tm, D_MODEL), row),
        out_shape=jax.ShapeDtypeStruct((n_tok, D_MODEL), F32),
        compiler_params=_cp(1),
    )(y_r, y_d, y_f, x, mod, mod, mod, mod, n2.reshape(1, D_MODEL), fg.reshape(1, D_MODEL), wo, wi, wf)


def _block_diag_state(s):
    b = s.shape[0]
    s = s.reshape(b, 2, N_PAIR, 2, HEAD_DIM, HEAD_DIM)
    eye = jnp.eye(2, dtype=s.dtype)
    s = s[:, :, :, :, :, None, :] * eye[None, None, None, :, None, :, None]
    return s.reshape(b, 2, N_PAIR, LANES, LANES)


def kernel(x_prompt, x_sample, c, state_rwkv, cache_diff_k, cache_diff_v, c_ctx, norm1_g, norm2_g, final_norm_g, w_mod, b_mod, w_in, w_out, shift_mu, decay_w0, decay_up, iclr_a0, iclr_up, gate_up, k_k, k_a, r_k, lnx_g, lnx_b, diff_lambda, subln_g, w_ffn_in, w_ffn_out):
    p = dict(shift_mu=shift_mu, decay_w0=decay_w0, decay_up=decay_up, iclr_a0=iclr_a0, iclr_up=iclr_up,
             gate_up=gate_up, k_k=k_k, k_a=k_a, r_k=r_k, lnx_g=lnx_g, lnx_b=lnx_b)
    n_ctx, t_ctx, _ = x_prompt.shape
    n_dec, t_dec, _ = x_sample.shape
    past = cache_diff_k.shape[2]

    cond = jnp.concatenate([c_ctx[None, :], c, jnp.zeros((MOD_ROWS - 1 - n_dec, D_MODEL), F32)], axis=0)
    mod = _modulation(cond, w_mod, b_mod).reshape(DEPTH * MOD_ROWS, 1, 6 * D_MODEL)

    tm_ffn = FFN_TM
    ctx_plan = _stream_plan(n_ctx, t_ctx)
    dec_plan = _stream_plan(n_dec, t_dec)
    streams = [
        dict(x=x_prompt.reshape(n_ctx * t_ctx, D_MODEL), t=t_ctx, n=n_ctx, **ctx_plan,
             row_in=lambda i: 0, row_ffn=lambda i: 0),
        dict(x=x_sample.reshape(n_dec * t_dec, D_MODEL), t=t_dec, n=n_dec, **dec_plan,
             row_in=lambda i: 1 + i // (t_dec // dec_plan['tm_in']), row_ffn=lambda i: 1 + i // (t_dec // tm_ffn)),
    ]
    cos, sin = _rope_tables(t_dec)
    for st in streams:
        st['kv'] = st['states'] = None
    w_in_l = jnp.concatenate(
        [w_in[:, :, :N_RWKV_IN].astype(BF16), jnp.zeros((DEPTH, D_MODEL, N_RWKV_PAD - N_RWKV_IN), BF16),
         w_in[:, :, N_RWKV_IN:].astype(BF16)], axis=2)
    wo = w_out.astype(BF16)
    wi = w_ffn_in.astype(BF16)
    wf = w_ffn_out.astype(BF16)
    for l in range(DEPTH):
        rw = _rwkv_weights(p, l)
        mu = jnp.concatenate([shift_mu[l], jnp.zeros((2, N_RWKV_PAD - N_RWKV_IN), F32)], axis=1)
        lam_init = 0.8 - 0.6 * math.exp(-0.3 * l)
        sg = jnp.tile(subln_g[l], 2).reshape(1, LANES)
        for si, st in enumerate(streams):
            u_r, q, k_all, v_all, u_f = _inproj(st['x'], mod, l, st['row_in'], norm1_g[l], w_in_l, mu,
                                                 st['kv'], st['t'], st['tm_in'])
            st['kv'] = (k_all, v_all)
            if si == 0:
                s0 = None
                attn_ctx = None
            else:
                s0 = state_rwkv[:, l].astype(F32)
                attn_ctx = (cache_diff_k[:, l].reshape(n_dec, past, D_DIFF).astype(F32),
                            cache_diff_v[:, l].reshape(n_dec, past, D_DIFF).astype(F32), cos, sin)
            y_r, st['states'] = _rwkv(u_r, s0, st['t'], st['n_sub'], rw, st['states'], l)
            y_d = _attention(q, k_all, v_all, l, st['n_attn'], lam_init, diff_lambda[l], sg, attn_ctx)
            y_f = _fnet(u_f, st['t'], st['n_fnet'])
            st['x'] = _ffn(y_r, y_d, y_f, st['x'], mod, l, st['row_ffn'], norm2_g[l], final_norm_g,
                           wo, wi, wf, l == DEPTH - 1, tm_ffn)
    y_prompt = streams[0]['x'].reshape(n_ctx, t_ctx, D_MODEL)
    y_sample = streams[1]['x'].reshape(n_dec, t_dec, D_MODEL)
    new_k = streams[0]['kv'][0].reshape(n_ctx, DEPTH, t_ctx, H_DIFF, 2, D_QK)
    new_v = streams[0]['kv'][1].reshape(n_ctx, DEPTH, t_ctx, H_DIFF, HEAD_DIM)
    return (y_prompt, y_sample, streams[0]['states'], new_k, new_v)
```

```python
import functools
import math

import numpy as np
import jax
import jax.numpy as jnp
from jax import lax
from jax.experimental import pallas as pl
from jax.experimental.pallas import tpu as pltpu

F32 = jnp.float32
BF16 = jnp.bfloat16

D_MODEL = 1024
DEPTH = 2
GRID_W = 64
HEAD_DIM = 64
D_RWKV = 384
H_RWKV = D_RWKV // HEAD_DIM
D_DIFF = 384
H_DIFF = D_DIFF // HEAD_DIM
D_QK = HEAD_DIM // 2
D_FNET = D_MODEL - D_RWKV - D_DIFF
FNET_GROUPS = 4
FNET_GROUP_DIM = D_FNET // FNET_GROUPS
LORA_W = 32
LORA_A = 32
LORA_G = 64
N_RWKV_IN = 3 * D_RWKV + 2 * LORA_W + 2 * LORA_A + LORA_G
N_DIFF_IN = 3 * D_DIFF
D_FF = ((8 * D_MODEL + 3 * 256 - 1) // (3 * 256)) * 256
ROPE_PAIRS = D_QK // 4
ROPE_BASE = 10000.0
RMS_EPS = 1e-6
GN_EPS = 64e-5
SUBLN_EPS = 1e-5
DECAY_SCALE = math.exp(-0.5)
Q_SCALE = D_QK ** -0.5 * math.log2(math.e)

LANES = 128
N_RWKV_PAD = 11 * LANES
D_IN_PAD = N_RWKV_PAD + N_DIFF_IN + D_FNET
N_PAIR = H_RWKV // 2
CHUNK = 64
N_PARTS = 2
SIDE_WORK_MAX_UNITS = 12
MOD_ROWS = 8
ONES_ROWS = 16
VMEM_LIMIT = 60 * 1024 * 1024

MOD_TN = 1536
INPROJ_TM = 512
FFN_TM = 512
ATTN_TQ = 256
ATTN_STEP_ROWS = 1024
KEY_BLOCK = 512
ATTN_LOOKAHEAD = 8
FNET_STEP_ROWS = 1024
RWKV_STEP_ROWS = 1024
RWKV_ONE_STEP_ROWS = 2048


def _stream_plan(n_seq, seq_len):
    one_step = n_seq * seq_len <= RWKV_ONE_STEP_ROWS
    return dict(n_sub=n_seq if one_step else max(1, RWKV_STEP_ROWS // seq_len),
                n_attn=max(1, ATTN_STEP_ROWS // seq_len),
                n_fnet=max(1, FNET_STEP_ROWS // seq_len),
                tm_in=max(seq_len, INPROJ_TM))


def _cp(n_axes=1):
    return pltpu.CompilerParams(dimension_semantics=("arbitrary",) * n_axes,
                                vmem_limit_bytes=VMEM_LIMIT)


def _bdot(a, b):
    return jnp.dot(a.astype(BF16), b.astype(BF16), preferred_element_type=F32)


def _bdot_nt(a, b):
    return lax.dot_general(a.astype(BF16), b.astype(BF16), (((1,), (1,)), ((), ())),
                           preferred_element_type=F32)


def _split2(x):
    hi = x.astype(BF16)
    lo = (x - hi.astype(F32)).astype(BF16)
    return hi, lo


def _dot_x3(a, b):
    a_hi, a_lo = _split2(a)
    b_hi, b_lo = _split2(b)
    d = functools.partial(jnp.dot, preferred_element_type=F32)
    return d(a_hi, b_hi) + d(a_lo, b_hi) + d(a_hi, b_lo)


def _rms(x, g):
    return x * lax.rsqrt(jnp.mean(x * x, axis=-1, keepdims=True) + RMS_EPS) * g


def _mod_body(c_ref, w_ref, b_ref, o_ref):
    c = c_ref[...]
    a = c * jax.nn.sigmoid(c)
    o_ref[0] = _dot_x3(a, w_ref[0]) + b_ref[0]


def _modulation(cond, w_mod, b_mod):
    n_layers, _, n_out = w_mod.shape
    tn = MOD_TN
    return pl.pallas_call(
        _mod_body,
        grid=(n_layers, n_out // tn),
        in_specs=[pl.BlockSpec((MOD_ROWS, D_MODEL), lambda l, j: (0, 0)),
                  pl.BlockSpec((1, D_MODEL, tn), lambda l, j: (l, 0, j)),
                  pl.BlockSpec((1, 1, tn), lambda l, j: (l, 0, j))],
        out_specs=pl.BlockSpec((1, MOD_ROWS, tn), lambda l, j: (l, 0, j)),
        out_shape=jax.ShapeDtypeStruct((n_layers, MOD_ROWS, n_out), F32),
        compiler_params=_cp(2),
    )(cond, w_mod, b_mod.reshape(n_layers, 1, n_out))


def _mod_spec(layer, col, row_fn):
    return pl.BlockSpec((1, 1, D_MODEL), lambda i: (layer * MOD_ROWS + row_fn(i), 0, col))


def _put_layer(ref, idx, layer, aliased, val):
    if aliased:
        ref[idx] = val
    else:
        for other in range(DEPTH):
            ref[idx + (other,)] = val if other == layer else jnp.zeros_like(val)


def _inproj_body(seq_len, layer, aliased, x_ref, g_ref, sh_ref, sc_ref, w_ref, mu_ref, *refs):
    ur_ref, q_ref, k_ref, v_ref, uf_ref = refs[2:] if aliased else refs
    part = x_ref.shape[0] // N_PARTS
    parts = [slice(j * part, (j + 1) * part) for j in range(N_PARTS)]
    h = [(_rms(x_ref[r, :], g_ref[...]) * (1.0 + sc_ref[0]) + sh_ref[0]).astype(BF16) for r in parts]
    u = jnp.concatenate([jnp.dot(hj, w_ref[...], preferred_element_type=F32) for hj in h], axis=0)
    ur = u[:, :N_RWKV_PAD]
    tm = ur.shape[0]
    pos = lax.broadcasted_iota(jnp.int32, (tm, 1), 0) & (seq_len - 1)
    prev = jnp.where(pos == 0, 0.0, pltpu.roll(ur, 1, axis=0))
    nxt = jnp.where(pos == seq_len - 1, 0.0, pltpu.roll(ur, tm - 1, axis=0))
    ur_ref[...] = (ur + mu_ref[0:1, :] * (prev - ur) + mu_ref[1:2, :] * (nxt - ur)).astype(BF16)
    q_ref[...] = (u[:, N_RWKV_PAD:N_RWKV_PAD + D_DIFF] * Q_SCALE).astype(BF16)
    for s in range(tm // seq_len):
        rows = slice(s * seq_len, (s + 1) * seq_len)
        _put_layer(k_ref, (s,), layer, aliased, u[rows, N_RWKV_PAD + D_DIFF:N_RWKV_PAD + 2 * D_DIFF])
        _put_layer(v_ref, (s,), layer, aliased, u[rows, N_RWKV_PAD + 2 * D_DIFF:N_RWKV_PAD + 3 * D_DIFF])
    uf_ref[...] = u[:, N_RWKV_PAD + N_DIFF_IN:].astype(BF16)


def _inproj(x, mod, layer, row_fn, g, w, mu, kv, seq_len, tm):
    n_tok = x.shape[0]
    kv_shape = jax.ShapeDtypeStruct((n_tok // seq_len, DEPTH, seq_len, D_DIFF), F32)
    assert tm % seq_len == 0 and seq_len & (seq_len - 1) == 0
    row = lambda i: (i, 0)
    const = lambda i: (0, 0)
    if kv is None:
        kv_spec = pl.BlockSpec((tm // seq_len, DEPTH, seq_len, D_DIFF), lambda i: (i, 0, 0, 0))
    else:
        kv_spec = pl.BlockSpec((tm // seq_len, None, seq_len, D_DIFF), lambda i: (i, layer, 0, 0))
    return pl.pallas_call(
        functools.partial(_inproj_body, seq_len, layer, kv is not None),
        grid=(n_tok // tm,),
        in_specs=[pl.BlockSpec((tm, D_MODEL), row),
                  pl.BlockSpec((1, D_MODEL), const),
                  _mod_spec(layer, 0, row_fn),
                  _mod_spec(layer, 1, row_fn),
                  pl.BlockSpec((None, D_MODEL, D_IN_PAD), lambda i: (layer, 0, 0)),
                  pl.BlockSpec((2, N_RWKV_PAD), const)]
        + [pl.BlockSpec(memory_space=pl.ANY)] * (0 if kv is None else 2),
        out_specs=[pl.BlockSpec((tm, N_RWKV_PAD), row),
                   pl.BlockSpec((tm, D_DIFF), row),
                   kv_spec, kv_spec,
                   pl.BlockSpec((tm, D_FNET), row)],
        out_shape=[jax.ShapeDtypeStruct((n_tok, N_RWKV_PAD), BF16),
                   jax.ShapeDtypeStruct((n_tok, D_DIFF), BF16),
                   kv_shape, kv_shape,
                   jax.ShapeDtypeStruct((n_tok, D_FNET), BF16)],
        input_output_aliases={} if kv is None else {6: 2, 7: 3},
        compiler_params=_cp(1),
    )(x, g.reshape(1, D_MODEL), mod, mod, w, mu, *(kv or ()))


def _lane_masks():
    lane = lax.broadcasted_iota(jnp.int32, (1, LANES), 1)
    return lane < HEAD_DIM, lane >= HEAD_DIM


def _rwkv_units(units, m0, m1, side=()):
    side = list(side)

    def run_side(drain=False):
        for gen in list(side):
            for _ in gen:
                if not drain:
                    break
            else:
                side.remove(gen)

    def bd(x):
        xb = x.astype(BF16)
        zero = jnp.zeros_like(xb)
        return jnp.concatenate([jnp.where(m0, xb, zero), jnp.where(m1, xb, zero)], axis=0)

    row = lax.broadcasted_iota(jnp.int32, (CHUNK, LANES), 0)
    col = lax.broadcasted_iota(jnp.int32, (CHUNK, LANES), 1) & (CHUNK - 1)
    eye = (col == row).astype(F32)
    r2 = lax.broadcasted_iota(jnp.int32, (LANES, LANES), 0) < HEAD_DIM
    c2 = lax.broadcasted_iota(jnp.int32, (LANES, LANES), 1) < HEAD_DIM
    rng = range(len(units))

    pre = []
    for rev, kk, r, v, kd, b, cum, cex, tot, s_prev in units:
        p_inv = jnp.exp(-cum)
        p_rem = jnp.exp(tot - cum)
        ab = -kk * jnp.exp(cex)
        rb = r * jnp.exp(cum)
        strict = (col > row) if rev else (col < row)
        incl = (col >= row) if rev else (col <= row)
        pre.append(dict(ab=ab, rb=rb, vbd=bd(v), strict=strict, incl=incl,
                        lhs=jnp.concatenate([ab, rb], axis=0),
                        rhs=jnp.concatenate([bd(b * p_inv), bd(kd * p_inv)], axis=0),
                        bk=jnp.concatenate([b * p_rem, kd * p_rem], axis=0)))

    run_side()
    mm = [_bdot_nt(q['lhs'], q['rhs']) for q in pre]
    run_side()
    m_ab = [jnp.where(pre[i]['strict'], mm[i][:CHUNK, :LANES], 0.0) for i in rng]
    m_ak = [jnp.where(pre[i]['strict'], mm[i][:CHUNK, LANES:], 0.0) for i in rng]
    m_r = [jnp.concatenate([jnp.where(pre[i]['incl'], mm[i][CHUNK:, :LANES], 0.0),
                            jnp.where(pre[i]['incl'], mm[i][CHUNK:, LANES:], 0.0)], axis=1) for i in rng]
    mv = [_bdot(m_ak[i], pre[i]['vbd']) for i in rng]
    run_side()

    t = [eye + m_ab[i] for i in rng]
    n = [_bdot(m_ab[i], bd(m_ab[i])) for i in rng]
    for _ in range(4):
        x = [_bdot(jnp.concatenate([t[i], n[i]], axis=0), bd(n[i])) for i in rng]
        t = [t[i] + x[i][:CHUNK] for i in rng]
        n = [x[i][CHUNK:] for i in rng]
        run_side()
    t = [t[i] + _bdot(t[i], bd(n[i])) for i in rng]

    w = [_bdot(t[i], jnp.concatenate([bd(pre[i]['ab']), bd(mv[i])], axis=1)) for i in rng]
    xs = [_bdot_nt(jnp.concatenate([w[i][:, :LANES], pre[i]['rb']], axis=0), units[i][9]) for i in rng]
    u = [xs[i][:CHUNK] + w[i][:, LANES:] for i in rng]
    run_side(drain=True)
    y = [xs[i][CHUNK:] + _bdot(m_r[i], jnp.concatenate([bd(u[i]), pre[i]['vbd']], axis=0)) for i in rng]
    z = [_bdot(jnp.concatenate([u[i], units[i][3]], axis=0).T, pre[i]['bk']) for i in rng]
    s_new = [units[i][9] * jnp.exp(units[i][8]) + jnp.where(r2 == c2, z[i], 0.0) for i in rng]
    return y, s_new


def _rwkv_body(seq_len, n_sub, layer, has_s0, aliased, u_ref, *refs):
    s0_ref = refs[0] if has_s0 else None
    refs = refs[1:] if has_s0 else refs
    wdec_ref, wicl_ref, wg_ref, w0a0_ref, vec_ref, bo_ref, tril_ref, triu_ref = refs[:8]
    (y_ref, sfin_ref, r_s, v_s, kk_s, kd_s, b_s, ci_s, ce_s, y_s, st_s) = refs[9 if aliased else 8:]
    n_chunk = seq_len // CHUNK
    assert n_chunk % 2 == 0 and n_chunk >= 4
    k_k = vec_ref[0:1, :]
    k_a = vec_ref[1:2, :]
    r_k = vec_ref[2:3, :]
    lnx_g = vec_ref[3:4, :]
    lnx_b = vec_ref[4:5, :]
    bo = bo_ref[...]

    def headsum(xb):
        return jnp.concatenate([jnp.dot(xb[:, p * LANES:(p + 1) * LANES], bo, preferred_element_type=F32)
                                for p in range(N_PAIR)], axis=1)

    def prep(rows):
        xs = u_ref[rows, :].astype(F32)
        r = xs[:, 0:D_RWKV]
        k = xs[:, D_RWKV:2 * D_RWKV]
        v = xs[:, 2 * D_RWKV:3 * D_RWKV]
        lora = xs[:, 3 * D_RWKV:3 * D_RWKV + LANES]
        t_lora = jnp.tanh(lora).astype(BF16)
        lora = lora.astype(BF16)
        kk = k * k_k
        kk2 = (kk * kk).astype(BF16)
        yield
        dec = _bdot(t_lora, wdec_ref[...])
        icl = _bdot(lora, wicl_ref[...])
        ss = headsum(kk2)
        yield
        logw = -DECAY_SCALE * jax.nn.sigmoid(w0a0_ref[0:1, :] + dec)
        a = jax.nn.sigmoid(w0a0_ref[1:2, :] + icl)
        kk = kk / jnp.maximum(jnp.sqrt(ss), 1e-12)
        a_f = a[:, :D_RWKV]
        a_b = a[:, D_RWKV:]
        kd_f = k * (1.0 + (a_f - 1.0) * k_a)
        kd_b = k * (1.0 + (a_b - 1.0) * k_a)
        lws = [_split2(logw[:, d * D_RWKV:(d + 1) * D_RWKV]) for d in range(2)]
        r_s[rows, :] = r
        v_s[rows, :] = v
        kk_s[rows, :] = kk
        kd_s[0, rows, :] = kd_f
        kd_s[1, rows, :] = kd_b
        b_s[0, rows, :] = kk * a_f
        b_s[1, rows, :] = kk * a_b
        yield
        dd = functools.partial(jnp.dot, preferred_element_type=F32)
        cums = [dd(tri_ref[...], lws[d][0]) + dd(tri_ref[...], lws[d][1])
                for d, tri_ref in enumerate((tril_ref, triu_ref))]
        yield
        for d in range(2):
            ci_s[d, rows, :] = cums[d]
            ce_s[d, rows, :] = cums[d] - logw[:, d * D_RWKV:(d + 1) * D_RWKV]

    def post(rows):
        y = y_s[rows, :]
        yb = y.astype(BF16)
        v = v_s[rows, :]
        bon = _split2(r_s[rows, :] * (kd_s[0, rows, :] + kd_s[1, rows, :]) * r_k)
        gd = u_ref[rows, 3 * D_RWKV + LANES:3 * D_RWKV + 2 * LANES].astype(F32)
        s_gd = jax.nn.sigmoid(gd).astype(BF16)
        yield
        mean = headsum(yb) * (1.0 / HEAD_DIM)
        bonus = (headsum(bon[0]) + headsum(bon[1])) * v
        g = _bdot(s_gd, wg_ref[...])
        yield
        yc = y - mean
        yc2 = (yc * yc).astype(BF16)
        yield
        var = headsum(yc2) * (1.0 / HEAD_DIM)
        yield
        yn = yc * lax.rsqrt(var + GN_EPS) * lnx_g + lnx_b
        y_ref[rows, :] = ((yn + bonus) * g).astype(BF16)

    def run_all(gens):
        gens = list(gens)
        while gens:
            gens = [gen for gen in gens if next(gen, gens) is not gens]

    def chunk_rows(i):
        return [pl.ds(pl.multiple_of(s * seq_len + (i if d == 0 else n_chunk - 1 - i) * CHUNK, CHUNK), CHUNK)
                for s in range(n_sub) for d in range(2)]

    st_s[...] = s0_ref[...] if has_s0 else jnp.zeros_like(st_s)
    y_s[...] = jnp.zeros_like(y_s)
    m0, m1 = _lane_masks()

    def scan_step(i, prep_next, post_prev):
        rows_sd = chunk_rows(i)
        side = []
        if prep_next:
            side = [prep(rows) for rows in chunk_rows(i + 1)]
        if post_prev:
            side = [post(rows) for rows in chunk_rows(i - 1)]
        units = []
        for s in range(n_sub):
            for d in range(2):
                rows = rows_sd[s * 2 + d]
                cum = ci_s[d, rows, :]
                cex = ce_s[d, rows, :]
                tot = cum[CHUNK - 1:CHUNK] if d == 0 else cum[0:1]
                kk = kk_s[rows, :]
                r = r_s[rows, :]
                v = v_s[rows, :]
                kd = kd_s[d, rows, :]
                b = b_s[d, rows, :]
                for p in range(N_PAIR):
                    sl = slice(p * LANES, (p + 1) * LANES)
                    units.append((d == 1, kk[:, sl], r[:, sl], v[:, sl], kd[:, sl], b[:, sl],
                                  cum[:, sl], cex[:, sl], tot[:, sl], st_s[s, d, p]))
        ys, s_new = _rwkv_units(units, m0, m1, side)
        for s in range(n_sub):
            for d in range(2):
                base = (s * 2 + d) * N_PAIR
                for p in range(N_PAIR):
                    st_s[s, d, p] = s_new[base + p]
                rows = rows_sd[s * 2 + d]
                y_s[rows, :] = y_s[rows, :] + jnp.concatenate(ys[base:base + N_PAIR], axis=1)

    def loop(lo, hi, **kw):
        lax.fori_loop(lo, hi, lambda i, c: (scan_step(i, **kw), c)[1], 0)

    half = n_chunk // 2
    if 2 * N_PAIR * n_sub <= SIDE_WORK_MAX_UNITS:
        run_all(prep(rows) for rows in chunk_rows(0))
        loop(0, half - 1, prep_next=True, post_prev=False)
        loop(half - 1, half + 1, prep_next=False, post_prev=False)
        loop(half + 1, n_chunk, prep_next=False, post_prev=True)
        run_all(post(rows) for rows in chunk_rows(n_chunk - 1))
    else:
        for i in range(half):
            run_all(prep(rows) for rows in chunk_rows(i))
        loop(0, n_chunk, prep_next=False, post_prev=False)
        for i in range(half):
            run_all(post(rows) for rows in chunk_rows(i))

    for s in range(n_sub):
        for d in range(2):
            for p in range(N_PAIR):
                st = st_s[s, d, p]
                for h in range(2):
                    rows = slice(h * HEAD_DIM, (h + 1) * HEAD_DIM)
                    val = st[rows, rows]
                    if aliased:
                        sfin_ref[s, d, 2 * p + h] = val
                    else:
                        for other in range(DEPTH):
                            sfin_ref[s, other, d, 2 * p + h] = val if other == layer else jnp.zeros_like(val)


def _rwkv(u_r, s0, seq_len, n_sub, wts, states, layer):
    n_seq = u_r.shape[0] // seq_len
    rows = n_sub * seq_len
    const2 = lambda b: (0, 0)
    st_shape = (n_sub, 2, N_PAIR, LANES, LANES)
    tok = pltpu.VMEM((rows, D_RWKV), F32)
    tok2 = pltpu.VMEM((2, rows, D_RWKV), F32)
    single = n_seq == n_sub
    in_specs = [pl.BlockSpec((rows, N_RWKV_PAD), lambda b: (b, 0),
                             pipeline_mode=pl.Buffered(1) if single else None)]
    args = [u_r]
    if s0 is not None:
        in_specs.append(pl.BlockSpec(st_shape, lambda b: (b, 0, 0, 0, 0)))
        args.append(_block_diag_state(s0))
    in_specs += [pl.BlockSpec(w.shape, const2) for w in wts]
    args += list(wts)
    if states is not None:
        in_specs.append(pl.BlockSpec(memory_space=pl.ANY))
        args.append(states)
    return pl.pallas_call(
        functools.partial(_rwkv_body, seq_len, n_sub, layer, s0 is not None, states is not None),
        grid=(n_seq // n_sub,),
        in_specs=in_specs,
        out_specs=[pl.BlockSpec((rows, D_RWKV), lambda b: (b, 0),
                                pipeline_mode=pl.Buffered(1) if single else None),
                   pl.BlockSpec((n_sub, DEPTH, 2, H_RWKV, HEAD_DIM, HEAD_DIM), lambda b: (b, 0, 0, 0, 0, 0))
                   if states is None else
                   pl.BlockSpec((n_sub, None, 2, H_RWKV, HEAD_DIM, HEAD_DIM),
                                lambda b: (b, layer, 0, 0, 0, 0))],
        out_shape=[jax.ShapeDtypeStruct((n_seq * seq_len, D_RWKV), BF16),
                   jax.ShapeDtypeStruct((n_seq, DEPTH, 2, H_RWKV, HEAD_DIM, HEAD_DIM), F32)],
        scratch_shapes=[tok] * 3 + [tok2] * 4 + [tok, pltpu.VMEM(st_shape, F32)],
        input_output_aliases={} if states is None else {len(args) - 1: 1},
        compiler_params=_cp(1),
    )(*args)


def _rwkv_weights(p, l):
    z = functools.partial(jnp.zeros, dtype=F32)
    wdec = z((LANES, 2 * D_RWKV))
    wdec = wdec.at[0:LORA_W, :D_RWKV].set(p['decay_up'][l, 0])
    wdec = wdec.at[LORA_W:2 * LORA_W, D_RWKV:].set(p['decay_up'][l, 1])
    wicl = z((LANES, 2 * D_RWKV))
    wicl = wicl.at[2 * LORA_W:2 * LORA_W + LORA_A, :D_RWKV].set(p['iclr_up'][l, 0])
    wicl = wicl.at[2 * LORA_W + LORA_A:2 * LORA_W + 2 * LORA_A, D_RWKV:].set(p['iclr_up'][l, 1])
    wg = z((LANES, D_RWKV)).at[0:LORA_G].set(p['gate_up'][l])
    w0a0 = jnp.stack([p['decay_w0'][l].reshape(-1), p['iclr_a0'][l].reshape(-1)])
    vec = jnp.stack([p['k_k'][l], p['k_a'][l], p['r_k'][l].reshape(-1), p['lnx_g'][l], p['lnx_b'][l],
                     z((D_RWKV,)), z((D_RWKV,)), z((D_RWKV,))])
    head = np.arange(LANES) // HEAD_DIM
    bo = jnp.asarray(head[:, None] == head[None, :], BF16)
    idx = np.arange(CHUNK)
    tril = jnp.asarray(idx[None, :] <= idx[:, None], BF16)
    triu = jnp.asarray(idx[None, :] >= idx[:, None], BF16)
    return [wdec.astype(BF16), wicl.astype(BF16), wg.astype(BF16), w0a0, vec, bo, tril, triu]


def _rope(x, cos, sin):
    lane = lax.broadcasted_iota(jnp.int32, (1, LANES), 1)
    first_half = (lane & ROPE_PAIRS) == 0
    partner = jnp.where(first_half, pltpu.roll(x, LANES - ROPE_PAIRS, axis=1),
                        pltpu.roll(x, ROPE_PAIRS, axis=1))
    return x * cos + partner * sin


def _attn_body(has_ctx, lam_init, *refs):
    if has_ctx:
        (q_ref, k_ref, v_ref, lp_ref, sg_ref, kc_ref, vc_ref, cq_ref, sq_ref, ck_ref, sk_ref,
         o_ref) = refs
    else:
        q_ref, k_ref, v_ref, lp_ref, sg_ref, o_ref = refs
    n_pair = D_DIFF // LANES
    lp = lp_ref[...]
    lam = (jnp.exp(jnp.sum(lp[0:1] * lp[1:2], axis=-1, keepdims=True))
           - jnp.exp(jnp.sum(lp[2:3] * lp[3:4], axis=-1, keepdims=True)) + lam_init)
    lane = lax.broadcasted_iota(jnp.int32, (1, LANES), 1)

    tq = q_ref.shape[0] // k_ref.shape[0]

    def operands(seq, p):
        sl = slice(p * LANES, (p + 1) * LANES)
        q = q_ref[seq * tq:(seq + 1) * tq, sl]
        k = k_ref[seq, :, sl]
        v = v_ref[seq, :, sl]
        if has_ctx:
            q = _rope(q.astype(F32), cq_ref[...], sq_ref[...])
            k = _rope(k, ck_ref[...], sk_ref[...])
            k = jnp.concatenate([kc_ref[0, :, sl], k], axis=0)
            v = jnp.concatenate([vc_ref[0, :, sl], v], axis=0)
        vt = v.T
        ones = jnp.ones((ONES_ROWS, v.shape[0]), F32)
        vts = [jnp.concatenate([vt[h * HEAD_DIM:(h + 1) * HEAD_DIM], ones], axis=0).astype(BF16)
               for h in range(2)]
        return q.astype(BF16), k.astype(BF16), vts

    n_keys = k_ref.shape[1] + (kc_ref.shape[1] if has_ctx else 0)
    blocks = range(0, n_keys, KEY_BLOCK)
    ops, tiles = {}, []
    for seq in range(k_ref.shape[0]):
        for p in range(n_pair):
            for j in blocks:
                tiles += [(seq, p, h, m, j) for h in range(2) for m in range(2)]

    def score(tile):
        seq, p, h, m, j = tile
        if (seq, p) not in ops:
            ops[seq, p] = operands(seq, p)
        q, k, _ = ops[seq, p]
        lo = h * HEAD_DIM + m * D_QK
        sel = (lane >= lo) & (lane < lo + D_QK)
        return _bdot_nt(k[j:j + KEY_BLOCK], jnp.where(sel, q, jnp.zeros_like(q)))

    run_max, acc = {}, {}

    def consume(tile, s):
        seq, p, h, m, j = tile
        vt = ops[seq, p][2][h][:, j:j + KEY_BLOCK]
        c = (seq, p, h, m)
        mj = jnp.max(s, axis=0, keepdims=True)
        if j == 0:
            run_max[c] = mj
            acc[c] = jnp.dot(vt, jnp.exp2(s - mj).astype(BF16), preferred_element_type=F32)
        else:
            m_new = jnp.maximum(run_max[c], mj)
            acc[c] = (acc[c] * jnp.exp2(run_max[c] - m_new)
                      + jnp.dot(vt, jnp.exp2(s - m_new).astype(BF16), preferred_element_type=F32))
            run_max[c] = m_new
        if j == blocks[-1] and (h, m) == (1, 1):
            finish(seq, p)

    def finish(seq, p):
        halves = []
        for h in range(2):
            a0, a1 = acc.pop((seq, p, h, 0)), acc.pop((seq, p, h, 1))
            o = (a0[:HEAD_DIM] * (1.0 / a0[HEAD_DIM:HEAD_DIM + 1])
                 - lam * (a1[:HEAD_DIM] * (1.0 / a1[HEAD_DIM:HEAD_DIM + 1])))
            ms = jnp.mean(o * o, axis=0, keepdims=True)
            halves.append(o * lax.rsqrt(ms + SUBLN_EPS))
        o_ref[seq * tq:(seq + 1) * tq, p * LANES:(p + 1) * LANES] = (
            jnp.concatenate(halves, axis=0).T * sg_ref[...] * (1.0 - lam_init)).astype(BF16)

    pending = []
    for tile in tiles:
        pending.append((tile, score(tile)))
        if len(pending) > ATTN_LOOKAHEAD:
            consume(*pending.pop(0))
    for item in pending:
        consume(*item)


def _attention(q, k_all, v_all, layer, n_sub, lam_init, lp, sg, ctx=None):
    n_tok = q.shape[0]
    n_seq, _, seq_len, _ = k_all.shape
    tq = ATTN_TQ
    nq = seq_len // tq
    assert n_sub == 1 or nq == 1
    kv_spec = pl.BlockSpec((n_sub, None, seq_len, D_DIFF), lambda b, i: (b, layer, 0, 0))
    in_specs = [pl.BlockSpec((n_sub * tq, D_DIFF), lambda b, i: (b * nq + i, 0)),
                kv_spec, kv_spec,
                pl.BlockSpec(lp.shape, lambda b, i: (0, 0)),
                pl.BlockSpec((1, LANES), lambda b, i: (0, 0))]
    args = [q, k_all, v_all, lp, sg]
    if ctx is not None:
        kc, vc, cos, sin = ctx
        past = kc.shape[1]
        in_specs += [pl.BlockSpec((1, past, D_DIFF), lambda b, i: (b, 0, 0)),
                     pl.BlockSpec((1, past, D_DIFF), lambda b, i: (b, 0, 0)),
                     pl.BlockSpec((tq, LANES), lambda b, i: (i, 0)),
                     pl.BlockSpec((tq, LANES), lambda b, i: (i, 0)),
                     pl.BlockSpec((seq_len, LANES), lambda b, i: (0, 0)),
                     pl.BlockSpec((seq_len, LANES), lambda b, i: (0, 0))]
        args += [kc, vc, cos, sin, cos, sin]
    return pl.pallas_call(
        functools.partial(_attn_body, ctx is not None, lam_init),
        grid=(n_seq // n_sub, nq),
        in_specs=in_specs,
        out_specs=pl.BlockSpec((n_sub * tq, D_DIFF), lambda b, i: (b * nq + i, 0)),
        out_shape=jax.ShapeDtypeStruct((n_tok, D_DIFF), BF16),
        compiler_params=_cp(2),
    )(*args)


def _rope_tables(seq_len):
    t = jnp.arange(seq_len)
    pos = jnp.stack([(t // GRID_W).astype(F32), (t % GRID_W).astype(F32)], axis=1)
    inv = 1.0 / (ROPE_BASE ** (jnp.arange(ROPE_PAIRS, dtype=F32) / ROPE_PAIRS))
    ang = pos[:, :, None] * inv
    d = np.arange(LANES) % D_QK
    axis = d // (2 * ROPE_PAIRS)
    second = (d % (2 * ROPE_PAIRS)) // ROPE_PAIRS
    idx = d % ROPE_PAIRS
    cos = jnp.cos(ang)[:, axis, idx]
    sin = jnp.sin(ang)[:, axis, idx] * jnp.asarray(np.where(second == 1, 1.0, -1.0), F32)
    return cos, sin


def _fnet_body(seq_len, x_ref, ct_ref, st_ref, cc_ref, sc_ref, o_ref):
    n_sub = x_ref.shape[0] // seq_len
    x = x_ref[...]
    xc = jnp.dot(x, cc_ref[...], preferred_element_type=F32)
    xs = jnp.dot(x, sc_ref[...], preferred_element_type=F32)
    wide = lambda a: jnp.concatenate([a[s * seq_len:(s + 1) * seq_len] for s in range(n_sub)], axis=1)
    y = _bdot(ct_ref[...], wide(xc)) - _bdot(st_ref[...], wide(xs))
    for s in range(n_sub):
        o_ref[s * seq_len:(s + 1) * seq_len, :] = y[:, s * D_FNET:(s + 1) * D_FNET].astype(BF16)


def _dft_consts(n, block=1):
    idx = np.arange(n)
    ang = 2.0 * np.pi * ((idx[:, None] * idx[None, :]) % n) / n
    return [jnp.asarray(np.kron(np.eye(block), m).astype(np.float32)).astype(BF16)
            for m in (np.cos(ang) / np.sqrt(n), np.sin(ang) / np.sqrt(n))]


def _fnet(u_f, seq_len, n_sub):
    n_tok = u_f.shape[0]
    rows = n_sub * seq_len
    consts = _dft_consts(seq_len) + _dft_consts(FNET_GROUP_DIM, FNET_GROUPS)
    const = lambda b: (0, 0)
    return pl.pallas_call(
        functools.partial(_fnet_body, seq_len),
        grid=(n_tok // rows,),
        in_specs=[pl.BlockSpec((rows, D_FNET), lambda b: (b, 0))]
        + [pl.BlockSpec(c.shape, const) for c in consts],
        out_specs=pl.BlockSpec((rows, D_FNET), lambda b: (b, 0)),
        out_shape=jax.ShapeDtypeStruct((n_tok, D_FNET), BF16),
        compiler_params=_cp(1),
    )(u_f, *consts)


def _ffn_body(final, yr_ref, yd_ref, yf_ref, x_ref, g1_ref, sh2_ref, sc2_ref, g2_ref, n2_ref, fg_ref,
              wo_ref, wi_ref, wf_ref, o_ref):
    part = x_ref.shape[0] // N_PARTS
    parts = [slice(j * part, (j + 1) * part) for j in range(N_PARTS)]
    y = [_bdot(yr_ref[r, :], wo_ref[0:D_RWKV, :])
         + _bdot(yd_ref[r, :], wo_ref[D_RWKV:D_RWKV + D_DIFF, :])
         + _bdot(yf_ref[r, :], wo_ref[D_RWKV + D_DIFF:, :]) for r in parts]
    x = [x_ref[r, :] + g1_ref[0] * y[j] for j, r in enumerate(parts)]
    h = [(_rms(xj, n2_ref[...]) * (1.0 + sc2_ref[0]) + sh2_ref[0]).astype(BF16) for xj in x]
    z = [jnp.dot(hj, wi_ref[...], preferred_element_type=F32) for hj in h]
    act = [zj[:, :D_FF] * jax.nn.sigmoid(zj[:, :D_FF]) * zj[:, D_FF:] for zj in z]
    f = [_bdot(aj, wf_ref[...]) for aj in act]
    for j, r in enumerate(parts):
        xj = x[j] + g2_ref[0] * f[j]
        o_ref[r, :] = _rms(xj, fg_ref[...]) if final else xj


def _ffn(y_r, y_d, y_f, x, mod, layer, row_fn, n2, fg, wo, wi, wf, final, tm):
    n_tok = x.shape[0]
    row = lambda i: (i, 0)
    const = lambda i: (0, 0)
    return pl.pallas_call(
        functools.partial(_ffn_body, final),
        grid=(n_tok // tm,),
        in_specs=[pl.BlockSpec((tm, D_RWKV), row),
                  pl.BlockSpec((tm, D_DIFF), row),
                  pl.BlockSpec((tm, D_FNET), row),
                  pl.BlockSpec((tm, D_MODEL), row),
                  _mod_spec(layer, 2, row_fn),
                  _mod_spec(layer, 3, row_fn),
                  _mod_spec(layer, 4, row_fn),
                  _mod_spec(layer, 5, row_fn),
                  pl.BlockSpec((1, D_MODEL), const),
                  pl.BlockSpec((1, D_MODEL), const),
                  pl.BlockSpec((None,) + wo.shape[1:], lambda i: (layer, 0, 0), pipeline_mode=pl.Buffered(1)),
                  pl.BlockSpec((None,) + wi.shape[1:], lambda i: (layer, 0, 0), pipeline_mode=pl.Buffered(1)),
                  pl.BlockSpec((None,) + wf.shape[1:], lambda i: (layer, 0, 0), pipeline_mode=pl.Buffered(1))],
        out_specs=pl.BlockSpec((tm, D_MODEL), row),
        out_shape=jax.ShapeDtypeStruct((n_tok, D_MODEL), F32),
        compiler_params=_cp(1),
    )(y_r, y_d, y_f, x, mod, mod, mod, mod, n2.reshape(1, D_MODEL), fg.reshape(1, D_MODEL), wo, wi, wf)


def _block_diag_state(s):
    b = s.shape[0]
    s = s.reshape(b, 2, N_PAIR, 2, HEAD_DIM, HEAD_DIM)
    eye = jnp.eye(2, dtype=s.dtype)
    s = s[:, :, :, :, :, None, :] * eye[None, None, None, :, None, :, None]
    return s.reshape(b, 2, N_PAIR, LANES, LANES)


def kernel(x_prompt, x_sample, c, state_rwkv, cache_diff_k, cache_diff_v, c_ctx, norm1_g, norm2_g, final_norm_g, w_mod, b_mod, w_in, w_out, shift_mu, decay_w0, decay_up, iclr_a0, iclr_up, gate_up, k_k, k_a, r_k, lnx_g, lnx_b, diff_lambda, subln_g, w_ffn_in, w_ffn_out):
    p = dict(shift_mu=shift_mu, decay_w0=decay_w0, decay_up=decay_up, iclr_a0=iclr_a0, iclr_up=iclr_up,
             gate_up=gate_up, k_k=k_k, k_a=k_a, r_k=r_k, lnx_g=lnx_g, lnx_b=lnx_b)
    n_ctx, t_ctx, _ = x_prompt.shape
    n_dec, t_dec, _ = x_sample.shape
    past = cache_diff_k.shape[2]

    cond = jnp.concatenate([c_ctx[None, :], c, jnp.zeros((MOD_ROWS - 1 - n_dec, D_MODEL), F32)], axis=0)
    mod = _modulation(cond, w_mod, b_mod).reshape(DEPTH * MOD_ROWS, 1, 6 * D_MODEL)

    tm_ffn = FFN_TM
    ctx_plan = _stream_plan(n_ctx, t_ctx)
    dec_plan = _stream_plan(n_dec, t_dec)
    streams = [
        dict(x=x_prompt.reshape(n_ctx * t_ctx, D_MODEL), t=t_ctx, n=n_ctx, **ctx_plan,
             row_in=lambda i: 0, row_ffn=lambda i: 0),
        dict(x=x_sample.reshape(n_dec * t_dec, D_MODEL), t=t_dec, n=n_dec, **dec_plan,
             row_in=lambda i: 1 + i // (t_dec // dec_plan['tm_in']), row_ffn=lambda i: 1 + i // (t_dec // tm_ffn)),
    ]
    cos, sin = _rope_tables(t_dec)
    for st in streams:
        st['kv'] = st['states'] = None
    w_in_l = jnp.concatenate(
        [w_in[:, :, :N_RWKV_IN].astype(BF16), jnp.zeros((DEPTH, D_MODEL, N_RWKV_PAD - N_RWKV_IN), BF16),
         w_in[:, :, N_RWKV_IN:].astype(BF16)], axis=2)
    wo = w_out.astype(BF16)
    wi = w_ffn_in.astype(BF16)
    wf = w_ffn_out.astype(BF16)
    for l in range(DEPTH):
        rw = _rwkv_weights(p, l)
        mu = jnp.concatenate([shift_mu[l], jnp.zeros((2, N_RWKV_PAD - N_RWKV_IN), F32)], axis=1)
        lam_init = 0.8 - 0.6 * math.exp(-0.3 * l)
        sg = jnp.tile(subln_g[l], 2).reshape(1, LANES)
        for si, st in enumerate(streams):
            u_r, q, k_all, v_all, u_f = _inproj(st['x'], mod, l, st['row_in'], norm1_g[l], w_in_l, mu,
                                                 st['kv'], st['t'], st['tm_in'])
            st['kv'] = (k_all, v_all)
            if si == 0:
                s0 = None
                attn_ctx = None
            else:
                s0 = state_rwkv[:, l].astype(F32)
                attn_ctx = (cache_diff_k[:, l].reshape(n_dec, past, D_DIFF).astype(F32),
                            cache_diff_v[:, l].reshape(n_dec, past, D_DIFF).astype(F32), cos, sin)
            y_r, st['states'] = _rwkv(u_r, s0, st['t'], st['n_sub'], rw, st['states'], l)
            y_d = _attention(q, k_all, v_all, l, st['n_attn'], lam_init, diff_lambda[l], sg, attn_ctx)
            y_f = _fnet(u_f, st['t'], st['n_fnet'])
            st['x'] = _ffn(y_r, y_d, y_f, st['x'], mod, l, st['row_ffn'], norm2_g[l], final_norm_g,
                           wo, wi, wf, l == DEPTH - 1, tm_ffn)
    y_prompt = streams[0]['x'].reshape(n_ctx, t_ctx, D_MODEL)
    y_sample = streams[1]['x'].reshape(n_dec, t_dec, D_MODEL)
    new_k = streams[0]['kv'][0].reshape(n_ctx, DEPTH, t_ctx, H_DIFF, 2, D_QK)
    new_v = streams[0]['kv'][1].reshape(n_ctx, DEPTH, t_ctx, H_DIFF, HEAD_DIM)
    return (y_prompt, y_sample, streams[0]['states'], new_k, new_v)
```

```python
import functools
import math

import numpy as np
import jax
import jax.numpy as jnp
from jax import lax
from jax.experimental import pallas as pl
from jax.experimental.pallas import tpu as pltpu

F32 = jnp.float32
BF16 = jnp.bfloat16

D_MODEL = 1024
DEPTH = 2
GRID_W = 64
HEAD_DIM = 64
D_RWKV = 384
H_RWKV = D_RWKV // HEAD_DIM
D_DIFF = 384
H_DIFF = D_DIFF // HEAD_DIM
D_QK = HEAD_DIM // 2
D_FNET = D_MODEL - D_RWKV - D_DIFF
FNET_GROUPS = 4
FNET_GROUP_DIM = D_FNET // FNET_GROUPS
LORA_W = 32
LORA_A = 32
LORA_G = 64
N_RWKV_IN = 3 * D_RWKV + 2 * LORA_W + 2 * LORA_A + LORA_G
N_DIFF_IN = 3 * D_DIFF
D_FF = ((8 * D_MODEL + 3 * 256 - 1) // (3 * 256)) * 256
ROPE_PAIRS = D_QK // 4
ROPE_BASE = 10000.0
RMS_EPS = 1e-6
GN_EPS = 64e-5
SUBLN_EPS = 1e-5
DECAY_SCALE = math.exp(-0.5)
Q_SCALE = D_QK ** -0.5 * math.log2(math.e)

LANES = 128
N_RWKV_PAD = 11 * LANES
D_IN_PAD = N_RWKV_PAD + N_DIFF_IN + D_FNET
N_PAIR = H_RWKV // 2
CHUNK = 64
N_PARTS = 2
SIDE_WORK_MAX_UNITS = 12
MOD_ROWS = 8
ONES_ROWS = 16
VMEM_LIMIT = 60 * 1024 * 1024

MOD_TN = 1536
INPROJ_TM = 512
FFN_TM = 512
ATTN_TQ = 256
ATTN_STEP_ROWS = 1024
KEY_BLOCK = 512
ATTN_LOOKAHEAD = 8
FNET_STEP_ROWS = 1024
RWKV_STEP_ROWS = 1024
RWKV_ONE_STEP_ROWS = 2048


def _stream_plan(n_seq, seq_len):
    one_step = n_seq * seq_len <= RWKV_ONE_STEP_ROWS
    return dict(n_sub=n_seq if one_step else max(1, RWKV_STEP_ROWS // seq_len),
                n_attn=max(1, ATTN_STEP_ROWS // seq_len),
                n_fnet=max(1, FNET_STEP_ROWS // seq_len),
                tm_in=max(seq_len, INPROJ_TM))


def _cp(n_axes=1):
    return pltpu.CompilerParams(dimension_semantics=("arbitrary",) * n_axes,
                                vmem_limit_bytes=VMEM_LIMIT)


def _bdot(a, b):
    return jnp.dot(a.astype(BF16), b.astype(BF16), preferred_element_type=F32)


def _bdot_nt(a, b):
    return lax.dot_general(a.astype(BF16), b.astype(BF16), (((1,), (1,)), ((), ())),
                           preferred_element_type=F32)


def _split2(x):
    hi = x.astype(BF16)
    lo = (x - hi.astype(F32)).astype(BF16)
    return hi, lo


def _dot_x3(a, b):
    a_hi, a_lo = _split2(a)
    b_hi, b_lo = _split2(b)
    d = functools.partial(jnp.dot, preferred_element_type=F32)
    return d(a_hi, b_hi) + d(a_lo, b_hi) + d(a_hi, b_lo)


def _rms(x, g):
    return x * lax.rsqrt(jnp.mean(x * x, axis=-1, keepdims=True) + RMS_EPS) * g


def _mod_body(c_ref, w_ref, b_ref, o_ref):
    c = c_ref[...]
    a = c * jax.nn.sigmoid(c)
    o_ref[0] = _dot_x3(a, w_ref[0]) + b_ref[0]


def _modulation(cond, w_mod, b_mod):
    n_layers, _, n_out = w_mod.shape
    tn = MOD_TN
    return pl.pallas_call(
        _mod_body,
        grid=(n_layers, n_out // tn),
        in_specs=[pl.BlockSpec((MOD_ROWS, D_MODEL), lambda l, j: (0, 0)),
                  pl.BlockSpec((1, D_MODEL, tn), lambda l, j: (l, 0, j)),
                  pl.BlockSpec((1, 1, tn), lambda l, j: (l, 0, j))],
        out_specs=pl.BlockSpec((1, MOD_ROWS, tn), lambda l, j: (l, 0, j)),
        out_shape=jax.ShapeDtypeStruct((n_layers, MOD_ROWS, n_out), F32),
        compiler_params=_cp(2),
    )(cond, w_mod, b_mod.reshape(n_layers, 1, n_out))


def _mod_spec(layer, col, row_fn):
    return pl.BlockSpec((1, 1, D_MODEL), lambda i: (layer * MOD_ROWS + row_fn(i), 0, col))


def _put_layer(ref, idx, layer, aliased, val):
    if aliased:
        ref[idx] = val
    else:
        for other in range(DEPTH):
            ref[idx + (other,)] = val if other == layer else jnp.zeros_like(val)


def _inproj_body(seq_len, layer, aliased, x_ref, g_ref, sh_ref, sc_ref, w_ref, mu_ref, *refs):
    ur_ref, q_ref, k_ref, v_ref, uf_ref = refs[2:] if aliased else refs
    part = x_ref.shape[0] // N_PARTS
    parts = [slice(j * part, (j + 1) * part) for j in range(N_PARTS)]
    h = [(_rms(x_ref[r, :], g_ref[...]) * (1.0 + sc_ref[0]) + sh_ref[0]).astype(BF16) for r in parts]
    u = jnp.concatenate([jnp.dot(hj, w_ref[...], preferred_element_type=F32) for hj in h], axis=0)
    ur = u[:, :N_RWKV_PAD]
    tm = ur.shape[0]
    pos = lax.broadcasted_iota(jnp.int32, (tm, 1), 0) & (seq_len - 1)
    prev = jnp.where(pos == 0, 0.0, pltpu.roll(ur, 1, axis=0))
    nxt = jnp.where(pos == seq_len - 1, 0.0, pltpu.roll(ur, tm - 1, axis=0))
    ur_ref[...] = (ur + mu_ref[0:1, :] * (prev - ur) + mu_ref[1:2, :] * (nxt - ur)).astype(BF16)
    q_ref[...] = (u[:, N_RWKV_PAD:N_RWKV_PAD + D_DIFF] * Q_SCALE).astype(BF16)
    for s in range(tm // seq_len):
        rows = slice(s * seq_len, (s + 1) * seq_len)
        _put_layer(k_ref, (s,), layer, aliased, u[rows, N_RWKV_PAD + D_DIFF:N_RWKV_PAD + 2 * D_DIFF])
        _put_layer(v_ref, (s,), layer, aliased, u[rows, N_RWKV_PAD + 2 * D_DIFF:N_RWKV_PAD + 3 * D_DIFF])
    uf_ref[...] = u[:, N_RWKV_PAD + N_DIFF_IN:].astype(BF16)


def _inproj(x, mod, layer, row_fn, g, w, mu, kv, seq_len, tm):
    n_tok = x.shape[0]
    kv_shape = jax.ShapeDtypeStruct((n_tok // seq_len, DEPTH, seq_len, D_DIFF), F32)
    assert tm % seq_len == 0 and seq_len & (seq_len - 1) == 0
    row = lambda i: (i, 0)
    const = lambda i: (0, 0)
    if kv is None:
        kv_spec = pl.BlockSpec((tm // seq_len, DEPTH, seq_len, D_DIFF), lambda i: (i, 0, 0, 0))
    else:
        kv_spec = pl.BlockSpec((tm // seq_len, None, seq_len, D_DIFF), lambda i: (i, layer, 0, 0))
    return pl.pallas_call(
        functools.partial(_inproj_body, seq_len, layer, kv is not None),
        grid=(n_tok // tm,),
        in_specs=[pl.BlockSpec((tm, D_MODEL), row),
                  pl.BlockSpec((1, D_MODEL), const),
                  _mod_spec(layer, 0, row_fn),
                  _mod_spec(layer, 1, row_fn),
                  pl.BlockSpec((None, D_MODEL, D_IN_PAD), lambda i: (layer, 0, 0)),
                  pl.BlockSpec((2, N_RWKV_PAD), const)]
        + [pl.BlockSpec(memory_space=pl.ANY)] * (0 if kv is None else 2),
        out_specs=[pl.BlockSpec((tm, N_RWKV_PAD), row),
                   pl.BlockSpec((tm, D_DIFF), row),
                   kv_spec, kv_spec,
                   pl.BlockSpec((tm, D_FNET), row)],
        out_shape=[jax.ShapeDtypeStruct((n_tok, N_RWKV_PAD), BF16),
                   jax.ShapeDtypeStruct((n_tok, D_DIFF), BF16),
                   kv_shape, kv_shape,
                   jax.ShapeDtypeStruct((n_tok, D_FNET), BF16)],
        input_output_aliases={} if kv is None else {6: 2, 7: 3},
        compiler_params=_cp(1),
    )(x, g.reshape(1, D_MODEL), mod, mod, w, mu, *(kv or ()))


def _lane_masks():
    lane = lax.broadcasted_iota(jnp.int32, (1, LANES), 1)
    return lane < HEAD_DIM, lane >= HEAD_DIM


def _rwkv_units(units, m0, m1, side=()):
    side = list(side)

    def run_side(drain=False):
        for gen in list(side):
            for _ in gen:
                if not drain:
                    break
            else:
                side.remove(gen)

    def bd(x):
        xb = x.astype(BF16)
        zero = jnp.zeros_like(xb)
        return jnp.concatenate([jnp.where(m0, xb, zero), jnp.where(m1, xb, zero)], axis=0)

    row = lax.broadcasted_iota(jnp.int32, (CHUNK, LANES), 0)
    col = lax.broadcasted_iota(jnp.int32, (CHUNK, LANES), 1) & (CHUNK - 1)
    eye = (col == row).astype(F32)
    r2 = lax.broadcasted_iota(jnp.int32, (LANES, LANES), 0) < HEAD_DIM
    c2 = lax.broadcasted_iota(jnp.int32, (LANES, LANES), 1) < HEAD_DIM
    rng = range(len(units))

    pre = []
    for rev, kk, r, v, kd, b, cum, cex, tot, s_prev in units:
        p_inv = jnp.exp(-cum)
        p_rem = jnp.exp(tot - cum)
        ab = -kk * jnp.exp(cex)
        rb = r * jnp.exp(cum)
        strict = (col > row) if rev else (col < row)
        incl = (col >= row) if rev else (col <= row)
        pre.append(dict(ab=ab, rb=rb, vbd=bd(v), strict=strict, incl=incl,
                        lhs=jnp.concatenate([ab, rb], axis=0),
                        rhs=jnp.concatenate([bd(b * p_inv), bd(kd * p_inv)], axis=0),
                        bk=jnp.concatenate([b * p_rem, kd * p_rem], axis=0)))

    run_side()
    mm = [_bdot_nt(q['lhs'], q['rhs']) for q in pre]
    run_side()
    m_ab = [jnp.where(pre[i]['strict'], mm[i][:CHUNK, :LANES], 0.0) for i in rng]
    m_ak = [jnp.where(pre[i]['strict'], mm[i][:CHUNK, LANES:], 0.0) for i in rng]
    m_r = [jnp.concatenate([jnp.where(pre[i]['incl'], mm[i][CHUNK:, :LANES], 0.0),
                            jnp.where(pre[i]['incl'], mm[i][CHUNK:, LANES:], 0.0)], axis=1) for i in rng]
    mv = [_bdot(m_ak[i], pre[i]['vbd']) for i in rng]
    run_side()

    t = [eye + m_ab[i] for i in rng]
    n = [_bdot(m_ab[i], bd(m_ab[i])) for i in rng]
    for _ in range(4):
        x = [_bdot(jnp.concatenate([t[i], n[i]], axis=0), bd(n[i])) for i in rng]
        t = [t[i] + x[i][:CHUNK] for i in rng]
        n = [x[i][CHUNK:] for i in rng]
        run_side()
    t = [t[i] + _bdot(t[i], bd(n[i])) for i in rng]

    w = [_bdot(t[i], jnp.concatenate([bd(pre[i]['ab']), bd(mv[i])], axis=1)) for i in rng]
    xs = [_bdot_nt(jnp.concatenate([w[i][:, :LANES], pre[i]['rb']], axis=0), units[i][9]) for i in rng]
    u = [xs[i][:CHUNK] + w[i][:, LANES:] for i in rng]
    run_side(drain=True)
    y = [xs[i][CHUNK:] + _bdot(m_r[i], jnp.concatenate([bd(u[i]), pre[i]['vbd']], axis=0)) for i in rng]
    z = [_bdot(jnp.concatenate([u[i], units[i][3]], axis=0).T, pre[i]['bk']) for i in rng]
    s_new = [units[i][9] * jnp.exp(units[i][8]) + jnp.where(r2 == c2, z[i], 0.0) for i in rng]
    return y, s_new


def _rwkv_body(seq_len, n_sub, layer, has_s0, aliased, u_ref, *refs):
    s0_ref = refs[0] if has_s0 else None
    refs = refs[1:] if has_s0 else refs
    wdec_ref, wicl_ref, wg_ref, w0a0_ref, vec_ref, bo_ref, tril_ref, triu_ref = refs[:8]
    (y_ref, sfin_ref, r_s, v_s, kk_s, kd_s, b_s, ci_s, ce_s, y_s, st_s) = refs[9 if aliased else 8:]
    n_chunk = seq_len // CHUNK
    assert n_chunk % 2 == 0 and n_chunk >= 4
    k_k = vec_ref[0:1, :]
    k_a = vec_ref[1:2, :]
    r_k = vec_ref[2:3, :]
    lnx_g = vec_ref[3:4, :]
    lnx_b = vec_ref[4:5, :]
    bo = bo_ref[...]

    def headsum(xb):
        return jnp.concatenate([jnp.dot(xb[:, p * LANES:(p + 1) * LANES], bo, preferred_element_type=F32)
                                for p in range(N_PAIR)], axis=1)

    def prep(rows):
        xs = u_ref[rows, :].astype(F32)
        r = xs[:, 0:D_RWKV]
        k = xs[:, D_RWKV:2 * D_RWKV]
        v = xs[:, 2 * D_RWKV:3 * D_RWKV]
        lora = xs[:, 3 * D_RWKV:3 * D_RWKV + LANES]
        t_lora = jnp.tanh(lora).astype(BF16)
        lora = lora.astype(BF16)
        kk = k * k_k
        kk2 = (kk * kk).astype(BF16)
        yield
        dec = _bdot(t_lora, wdec_ref[...])
        icl = _bdot(lora, wicl_ref[...])
        ss = headsum(kk2)
        yield
        logw = -DECAY_SCALE * jax.nn.sigmoid(w0a0_ref[0:1, :] + dec)
        a = jax.nn.sigmoid(w0a0_ref[1:2, :] + icl)
        kk = kk / jnp.maximum(jnp.sqrt(ss), 1e-12)
        a_f = a[:, :D_RWKV]
        a_b = a[:, D_RWKV:]
        kd_f = k * (1.0 + (a_f - 1.0) * k_a)
        kd_b = k * (1.0 + (a_b - 1.0) * k_a)
        lws = [_split2(logw[:, d * D_RWKV:(d + 1) * D_RWKV]) for d in range(2)]
        r_s[rows, :] = r
        v_s[rows, :] = v
        kk_s[rows, :] = kk
        kd_s[0, rows, :] = kd_f
        kd_s[1, rows, :] = kd_b
        b_s[0, rows, :] = kk * a_f
        b_s[1, rows, :] = kk * a_b
        yield
        dd = functools.partial(jnp.dot, preferred_element_type=F32)
        cums = [dd(tri_ref[...], lws[d][0]) + dd(tri_ref[...], lws[d][1])
                for d, tri_ref in enumerate((tril_ref, triu_ref))]
        yield
        for d in range(2):
            ci_s[d, rows, :] = cums[d]
            ce_s[d, rows, :] = cums[d] - logw[:, d * D_RWKV:(d + 1) * D_RWKV]

    def post(rows):
        y = y_s[rows, :]
        yb = y.astype(BF16)
        v = v_s[rows, :]
        bon = _split2(r_s[rows, :] * (kd_s[0, rows, :] + kd_s[1, rows, :]) * r_k)
        gd = u_ref[rows, 3 * D_RWKV + LANES:3 * D_RWKV + 2 * LANES].astype(F32)
        s_gd = jax.nn.sigmoid(gd).astype(BF16)
        yield
        mean = headsum(yb) * (1.0 / HEAD_DIM)
        bonus = (headsum(bon[0]) + headsum(bon[1])) * v
        g = _bdot(s_gd, wg_ref[...])
        yield
        yc = y - mean
        yc2 = (yc * yc).astype(BF16)
        yield
        var = headsum(yc2) * (1.0 / HEAD_DIM)
        yield
        yn = yc * lax.rsqrt(var + GN_EPS) * lnx_g + lnx_b
        y_ref[rows, :] = ((yn + bonus) * g).astype(BF16)

    def run_all(gens):
        gens = list(gens)
        while gens:
            gens = [gen for gen in gens if next(gen, gens) is not gens]

    def chunk_rows(i):
        return [pl.ds(pl.multiple_of(s * seq_len + (i if d == 0 else n_chunk - 1 - i) * CHUNK, CHUNK), CHUNK)
                for s in range(n_sub) for d in range(2)]

    st_s[...] = s0_ref[...] if has_s0 else jnp.zeros_like(st_s)
    y_s[...] = jnp.zeros_like(y_s)
    m0, m1 = _lane_masks()

    def scan_step(i, prep_next, post_prev):
        rows_sd = chunk_rows(i)
        side = []
        if prep_next:
            side = [prep(rows) for rows in chunk_rows(i + 1)]
        if post_prev:
            side = [post(rows) for rows in chunk_rows(i - 1)]
        units = []
        for s in range(n_sub):
            for d in range(2):
                rows = rows_sd[s * 2 + d]
                cum = ci_s[d, rows, :]
                cex = ce_s[d, rows, :]
                tot = cum[CHUNK - 1:CHUNK] if d == 0 else cum[0:1]
                kk = kk_s[rows, :]
                r = r_s[rows, :]
                v = v_s[rows, :]
                kd = kd_s[d, rows, :]
                b = b_s[d, rows, :]
                for p in range(N_PAIR):
                    sl = slice(p * LANES, (p + 1) * LANES)
                    units.append((d == 1, kk[:, sl], r[:, sl], v[:, sl], kd[:, sl], b[:, sl],
                                  cum[:, sl], cex[:, sl], tot[:, sl], st_s[s, d, p]))
        ys, s_new = _rwkv_units(units, m0, m1, side)
        for s in range(n_sub):
            for d in range(2):
                base = (s * 2 + d) * N_PAIR
                for p in range(N_PAIR):
                    st_s[s, d, p] = s_new[base + p]
                rows = rows_sd[s * 2 + d]
                y_s[rows, :] = y_s[rows, :] + jnp.concatenate(ys[base:base + N_PAIR], axis=1)

    def loop(lo, hi, **kw):
        lax.fori_loop(lo, hi, lambda i, c: (scan_step(i, **kw), c)[1], 0)

    half = n_chunk // 2
    if 2 * N_PAIR * n_sub <= SIDE_WORK_MAX_UNITS:
        run_all(prep(rows) for rows in chunk_rows(0))
        loop(0, half - 1, prep_next=True, post_prev=False)
        loop(half - 1, half + 1, prep_next=False, post_prev=False)
        loop(half + 1, n_chunk, prep_next=False, post_prev=True)
        run_all(post(rows) for rows in chunk_rows(n_chunk - 1))
    else:
        for i in range(half):
            run_all(prep(rows) for rows in chunk_rows(i))
        loop(0, n_chunk, prep_next=False, post_prev=False)
        for i in range(half):
            run_all(post(rows) for rows in chunk_rows(i))

    for s in range(n_sub):
        for d in range(2):
            for p in range(N_PAIR):
                st = st_s[s, d, p]
                for h in range(2):
                    rows = slice(h * HEAD_DIM, (h + 1) * HEAD_DIM)
                    val = st[rows, rows]
                    if aliased:
                        sfin_ref[s, d, 2 * p + h] = val
                    else:
                        for other in range(DEPTH):
                            sfin_ref[s, other, d, 2 * p + h] = val if other == layer else jnp.zeros_like(val)


def _rwkv(u_r, s0, seq_len, n_sub, wts, states, layer):
    n_seq = u_r.shape[0] // seq_len
    rows = n_sub * seq_len
    const2 = lambda b: (0, 0)
    st_shape = (n_sub, 2, N_PAIR, LANES, LANES)
    tok = pltpu.VMEM((rows, D_RWKV), F32)
    tok2 = pltpu.VMEM((2, rows, D_RWKV), F32)
    single = n_seq == n_sub
    in_specs = [pl.BlockSpec((rows, N_RWKV_PAD), lambda b: (b, 0),
                             pipeline_mode=pl.Buffered(1) if single else None)]
    args = [u_r]
    if s0 is not None:
        in_specs.append(pl.BlockSpec(st_shape, lambda b: (b, 0, 0, 0, 0)))
        args.append(_block_diag_state(s0))
    in_specs += [pl.BlockSpec(w.shape, const2) for w in wts]
    args += list(wts)
    if states is not None:
        in_specs.append(pl.BlockSpec(memory_space=pl.ANY))
        args.append(states)
    return pl.pallas_call(
        functools.partial(_rwkv_body, seq_len, n_sub, layer, s0 is not None, states is not None),
        grid=(n_seq // n_sub,),
        in_specs=in_specs,
        out_specs=[pl.BlockSpec((rows, D_RWKV), lambda b: (b, 0),
                                pipeline_mode=pl.Buffered(1) if single else None),
                   pl.BlockSpec((n_sub, DEPTH, 2, H_RWKV, HEAD_DIM, HEAD_DIM), lambda b: (b, 0, 0, 0, 0, 0))
                   if states is None else
                   pl.BlockSpec((n_sub, None, 2, H_RWKV, HEAD_DIM, HEAD_DIM),
                                lambda b: (b, layer, 0, 0, 0, 0))],
        out_shape=[jax.ShapeDtypeStruct((n_seq * seq_len, D_RWKV), BF16),
                   jax.ShapeDtypeStruct((n_seq, DEPTH, 2, H_RWKV, HEAD_DIM, HEAD_DIM), F32)],
        scratch_shapes=[tok] * 3 + [tok2] * 4 + [tok, pltpu.VMEM(st_shape, F32)],
        input_output_aliases={} if states is None else {len(args) - 1: 1},
        compiler_params=_cp(1),
    )(*args)


def _rwkv_weights(p, l):
    z = functools.partial(jnp.zeros, dtype=F32)
    wdec = z((LANES, 2 * D_RWKV))
    wdec = wdec.at[0:LORA_W, :D_RWKV].set(p['decay_up'][l, 0])
    wdec = wdec.at[LORA_W:2 * LORA_W, D_RWKV:].set(p['decay_up'][l, 1])
    wicl = z((LANES, 2 * D_RWKV))
    wicl = wicl.at[2 * LORA_W:2 * LORA_W + LORA_A, :D_RWKV].set(p['iclr_up'][l, 0])
    wicl = wicl.at[2 * LORA_W + LORA_A:2 * LORA_W + 2 * LORA_A, D_RWKV:].set(p['iclr_up'][l, 1])
    wg = z((LANES, D_RWKV)).at[0:LORA_G].set(p['gate_up'][l])
    w0a0 = jnp.stack([p['decay_w0'][l].reshape(-1), p['iclr_a0'][l].reshape(-1)])
    vec = jnp.stack([p['k_k'][l], p['k_a'][l], p['r_k'][l].reshape(-1), p['lnx_g'][l], p['lnx_b'][l],
                     z((D_RWKV,)), z((D_RWKV,)), z((D_RWKV,))])
    head = np.arange(LANES) // HEAD_DIM
    bo = jnp.asarray(head[:, None] == head[None, :], BF16)
    idx = np.arange(CHUNK)
    tril = jnp.asarray(idx[None, :] <= idx[:, None], BF16)
    triu = jnp.asarray(idx[None, :] >= idx[:, None], BF16)
    return [wdec.astype(BF16), wicl.astype(BF16), wg.astype(BF16), w0a0, vec, bo, tril, triu]


def _rope(x, cos, sin):
    lane = lax.broadcasted_iota(jnp.int32, (1, LANES), 1)
    first_half = (lane & ROPE_PAIRS) == 0
    partner = jnp.where(first_half, pltpu.roll(x, LANES - ROPE_PAIRS, axis=1),
                        pltpu.roll(x, ROPE_PAIRS, axis=1))
    return x * cos + partner * sin


def _attn_body(has_ctx, lam_init, *refs):
    if has_ctx:
        (q_ref, k_ref, v_ref, lp_ref, sg_ref, kc_ref, vc_ref, cq_ref, sq_ref, ck_ref, sk_ref,
         o_ref) = refs
    else:
        q_ref, k_ref, v_ref, lp_ref, sg_ref, o_ref = refs
    n_pair = D_DIFF // LANES
    lp = lp_ref[...]
    lam = (jnp.exp(jnp.sum(lp[0:1] * lp[1:2], axis=-1, keepdims=True))
           - jnp.exp(jnp.sum(lp[2:3] * lp[3:4], axis=-1, keepdims=True)) + lam_init)
    lane = lax.broadcasted_iota(jnp.int32, (1, LANES), 1)

    tq = q_ref.shape[0] // k_ref.shape[0]

    def operands(seq, p):
        sl = slice(p * LANES, (p + 1) * LANES)
        q = q_ref[seq * tq:(seq + 1) * tq, sl]
        k = k_ref[seq, :, sl]
        v = v_ref[seq, :, sl]
        if has_ctx:
            q = _rope(q.astype(F32), cq_ref[...], sq_ref[...])
            k = _rope(k, ck_ref[...], sk_ref[...])
            k = jnp.concatenate([kc_ref[0, :, sl], k], axis=0)
            v = jnp.concatenate([vc_ref[0, :, sl], v], axis=0)
        vt = v.T
        ones = jnp.ones((ONES_ROWS, v.shape[0]), F32)
        vts = [jnp.concatenate([vt[h * HEAD_DIM:(h + 1) * HEAD_DIM], ones], axis=0).astype(BF16)
               for h in range(2)]
        return q.astype(BF16), k.astype(BF16), vts

    n_keys = k_ref.shape[1] + (kc_ref.shape[1] if has_ctx else 0)
    blocks = range(0, n_keys, KEY_BLOCK)
    ops, tiles = {}, []
    for seq in range(k_ref.shape[0]):
        for p in range(n_pair):
            for j in blocks:
                tiles += [(seq, p, h, m, j) for h in range(2) for m in range(2)]

    def score(tile):
        seq, p, h, m, j = tile
        if (seq, p) not in ops:
            ops[seq, p] = operands(seq, p)
        q, k, _ = ops[seq, p]
        lo = h * HEAD_DIM + m * D_QK
        sel = (lane >= lo) & (lane < lo + D_QK)
        return _bdot_nt(k[j:j + KEY_BLOCK], jnp.where(sel, q, jnp.zeros_like(q)))

    run_max, acc = {}, {}

    def consume(tile, s):
        seq, p, h, m, j = tile
        vt = ops[seq, p][2][h][:, j:j + KEY_BLOCK]
        c = (seq, p, h, m)
        mj = jnp.max(s, axis=0, keepdims=True)
        if j == 0:
            run_max[c] = mj
            acc[c] = jnp.dot(vt, jnp.exp2(s - mj).astype(BF16), preferred_element_type=F32)
        else:
            m_new = jnp.maximum(run_max[c], mj)
            acc[c] = (acc[c] * jnp.exp2(run_max[c] - m_new)
                      + jnp.dot(vt, jnp.exp2(s - m_new).astype(BF16), preferred_element_type=F32))
            run_max[c] = m_new
        if j == blocks[-1] and (h, m) == (1, 1):
            finish(seq, p)

    def finish(seq, p):
        halves = []
        for h in range(2):
            a0, a1 = acc.pop((seq, p, h, 0)), acc.pop((seq, p, h, 1))
            o = (a0[:HEAD_DIM] * (1.0 / a0[HEAD_DIM:HEAD_DIM + 1])
                 - lam * (a1[:HEAD_DIM] * (1.0 / a1[HEAD_DIM:HEAD_DIM + 1])))
            ms = jnp.mean(o * o, axis=0, keepdims=True)
            halves.append(o * lax.rsqrt(ms + SUBLN_EPS))
        o_ref[seq * tq:(seq + 1) * tq, p * LANES:(p + 1) * LANES] = (
            jnp.concatenate(halves, axis=0).T * sg_ref[...] * (1.0 - lam_init)).astype(BF16)

    pending = []
    for tile in tiles:
        pending.append((tile, score(tile)))
        if len(pending) > ATTN_LOOKAHEAD:
            consume(*pending.pop(0))
    for item in pending:
        consume(*item)


def _attention(q, k_all, v_all, layer, n_sub, lam_init, lp, sg, ctx=None):
    n_tok = q.shape[0]
    n_seq, _, seq_len, _ = k_all.shape
    tq = ATTN_TQ
    nq = seq_len // tq
    assert n_sub == 1 or nq == 1
    kv_spec = pl.BlockSpec((n_sub, None, seq_len, D_DIFF), lambda b, i: (b, layer, 0, 0))
    in_specs = [pl.BlockSpec((n_sub * tq, D_DIFF), lambda b, i: (b * nq + i, 0)),
                kv_spec, kv_spec,
                pl.BlockSpec(lp.shape, lambda b, i: (0, 0)),
                pl.BlockSpec((1, LANES), lambda b, i: (0, 0))]
    args = [q, k_all, v_all, lp, sg]
    if ctx is not None:
        kc, vc, cos, sin = ctx
        past = kc.shape[1]
        in_specs += [pl.BlockSpec((1, past, D_DIFF), lambda b, i: (b, 0, 0)),
                     pl.BlockSpec((1, past, D_DIFF), lambda b, i: (b, 0, 0)),
                     pl.BlockSpec((tq, LANES), lambda b, i: (i, 0)),
                     pl.BlockSpec((tq, LANES), lambda b, i: (i, 0)),
                     pl.BlockSpec((seq_len, LANES), lambda b, i: (0, 0)),
                     pl.BlockSpec((seq_len, LANES), lambda b, i: (0, 0))]
        args += [kc, vc, cos, sin, cos, sin]
    return pl.pallas_call(
        functools.partial(_attn_body, ctx is not None, lam_init),
        grid=(n_seq // n_sub, nq),
        in_specs=in_specs,
        out_specs=pl.BlockSpec((n_sub * tq, D_DIFF), lambda b, i: (b * nq + i, 0)),
        out_shape=jax.ShapeDtypeStruct((n_tok, D_DIFF), BF16),
        compiler_params=_cp(2),
    )(*args)


def _rope_tables(seq_len):
    t = jnp.arange(seq_len)
    pos = jnp.stack([(t // GRID_W).astype(F32), (t % GRID_W).astype(F32)], axis=1)
    inv = 1.0 / (ROPE_BASE ** (jnp.arange(ROPE_PAIRS, dtype=F32) / ROPE_PAIRS))
    ang = pos[:, :, None] * inv
    d = np.arange(LANES) % D_QK
    axis = d // (2 * ROPE_PAIRS)
    second = (d % (2 * ROPE_PAIRS)) // ROPE_PAIRS
    idx = d % ROPE_PAIRS
    cos = jnp.cos(ang)[:, axis, idx]
    sin = jnp.sin(ang)[:, axis, idx] * jnp.asarray(np.where(second == 1, 1.0, -1.0), F32)
    return cos, sin


def _fnet_body(seq_len, x_ref, ct_ref, st_ref, cc_ref, sc_ref, o_ref):
    n_sub = x_ref.shape[0] // seq_len
    x = x_ref[...]
    xc = jnp.dot(x, cc_ref[...], preferred_element_type=F32)
    xs = jnp.dot(x, sc_ref[...], preferred_element_type=F32)
    wide = lambda a: jnp.concatenate([a[s * seq_len:(s + 1) * seq_len] for s in range(n_sub)], axis=1)
    y = _bdot(ct_ref[...], wide(xc)) - _bdot(st_ref[...], wide(xs))
    for s in range(n_sub):
        o_ref[s * seq_len:(s + 1) * seq_len, :] = y[:, s * D_FNET:(s + 1) * D_FNET].astype(BF16)


def _dft_consts(n, block=1):
    idx = np.arange(n)
    ang = 2.0 * np.pi * ((idx[:, None] * idx[None, :]) % n) / n
    return [jnp.asarray(np.kron(np.eye(block), m).astype(np.float32)).astype(BF16)
            for m in (np.cos(ang) / np.sqrt(n), np.sin(ang) / np.sqrt(n))]


def _fnet(u_f, seq_len, n_sub):
    n_tok = u_f.shape[0]
    rows = n_sub * seq_len
    consts = _dft_consts(seq_len) + _dft_consts(FNET_GROUP_DIM, FNET_GROUPS)
    const = lambda b: (0, 0)
    return pl.pallas_call(
        functools.partial(_fnet_body, seq_len),
        grid=(n_tok // rows,),
        in_specs=[pl.BlockSpec((rows, D_FNET), lambda b: (b, 0))]
        + [pl.BlockSpec(c.shape, const) for c in consts],
        out_specs=pl.BlockSpec((rows, D_FNET), lambda b: (b, 0)),
        out_shape=jax.ShapeDtypeStruct((n_tok, D_FNET), BF16),
        compiler_params=_cp(1),
    )(u_f, *consts)


def _ffn_body(final, yr_ref, yd_ref, yf_ref, x_ref, g1_ref, sh2_ref, sc2_ref, g2_ref, n2_ref, fg_ref,
              wo_ref, wi_ref, wf_ref, o_ref):
    part = x_ref.shape[0] // N_PARTS
    parts = [slice(j * part, (j + 1) * part) for j in range(N_PARTS)]
    y = [jnp.dot(jnp.concatenate([yr_ref[r, :], yd_ref[r, :], yf_ref[r, :]], axis=1), wo_ref[...],
                 preferred_element_type=F32) for r in parts]
    x = [x_ref[r, :] + g1_ref[0] * y[j] for j, r in enumerate(parts)]
    h = [(_rms(xj, n2_ref[...]) * (1.0 + sc2_ref[0]) + sh2_ref[0]).astype(BF16) for xj in x]
    z = [jnp.dot(hj, wi_ref[...], preferred_element_type=F32) for hj in h]
    act = [zj[:, :D_FF] * jax.nn.sigmoid(zj[:, :D_FF]) * zj[:, D_FF:] for zj in z]
    f = [_bdot(aj, wf_ref[...]) for aj in act]
    for j, r in enumerate(parts):
        xj = x[j] + g2_ref[0] * f[j]
        o_ref[r, :] = _rms(xj, fg_ref[...]) if final else xj


def _ffn(y_r, y_d, y_f, x, mod, layer, row_fn, n2, fg, wo, wi, wf, final, tm):
    n_tok = x.shape[0]
    row = lambda i: (i, 0)
    const = lambda i: (0, 0)
    return pl.pallas_call(
        functools.partial(_ffn_body, final),
        grid=(n_tok // tm,),
        in_specs=[pl.BlockSpec((tm, D_RWKV), row),
                  pl.BlockSpec((tm, D_DIFF), row),
                  pl.BlockSpec((tm, D_FNET), row),
                  pl.BlockSpec((tm, D_MODEL), row),
                  _mod_spec(layer, 2, row_fn),
                  _mod_spec(layer, 3, row_fn),
                  _mod_spec(layer, 4, row_fn),
                  _mod_spec(layer, 5, row_fn),
                  pl.BlockSpec((1, D_MODEL), const),
                  pl.BlockSpec((1, D_MODEL), const),
                  pl.BlockSpec((None,) + wo.shape[1:], lambda i: (layer, 0, 0), pipeline_mode=pl.Buffered(1)),
                  pl.BlockSpec((None,) + wi.shape[1:], lambda i: (layer, 0, 0), pipeline_mode=pl.Buffered(1)),
                  pl.BlockSpec((None,) + wf.shape[1:], lambda i: (layer, 0, 0), pipeline_mode=pl.Buffered(1))],
        out_specs=pl.BlockSpec((tm, D_MODEL), row),
        out_shape=jax.ShapeDtypeStruct((n_tok, D_MODEL), F32),
        compiler_params=_cp(1),
    )(y_r, y_d, y_f, x, mod, mod, mod, mod, n2.reshape(1, D_MODEL), fg.reshape(1, D_MODEL), wo, wi, wf)


def _block_diag_state(s):
    b = s.shape[0]
    s = s.reshape(b, 2, N_PAIR, 2, HEAD_DIM, HEAD_DIM)
    eye = jnp.eye(2, dtype=s.dtype)
    s = s[:, :, :, :, :, None, :] * eye[None, None, None, :, None, :, None]
    return s.reshape(b, 2, N_PAIR, LANES, LANES)


def kernel(x_prompt, x_sample, c, state_rwkv, cache_diff_k, cache_diff_v, c_ctx, norm1_g, norm2_g, final_norm_g, w_mod, b_mod, w_in, w_out, shift_mu, decay_w0, decay_up, iclr_a0, iclr_up, gate_up, k_k, k_a, r_k, lnx_g, lnx_b, diff_lambda, subln_g, w_ffn_in, w_ffn_out):
    p = dict(shift_mu=shift_mu, decay_w0=decay_w0, decay_up=decay_up, iclr_a0=iclr_a0, iclr_up=iclr_up,
             gate_up=gate_up, k_k=k_k, k_a=k_a, r_k=r_k, lnx_g=lnx_g, lnx_b=lnx_b)
    n_ctx, t_ctx, _ = x_prompt.shape
    n_dec, t_dec, _ = x_sample.shape
    past = cache_diff_k.shape[2]

    cond = jnp.concatenate([c_ctx[None, :], c, jnp.zeros((MOD_ROWS - 1 - n_dec, D_MODEL), F32)], axis=0)
    mod = _modulation(cond, w_mod, b_mod).reshape(DEPTH * MOD_ROWS, 1, 6 * D_MODEL)

    tm_ffn = FFN_TM
    ctx_plan = _stream_plan(n_ctx, t_ctx)
    dec_plan = _stream_plan(n_dec, t_dec)
    streams = [
        dict(x=x_prompt.reshape(n_ctx * t_ctx, D_MODEL), t=t_ctx, n=n_ctx, **ctx_plan,
             row_in=lambda i: 0, row_ffn=lambda i: 0),
        dict(x=x_sample.reshape(n_dec * t_dec, D_MODEL), t=t_dec, n=n_dec, **dec_plan,
             row_in=lambda i: 1 + i // (t_dec // dec_plan['tm_in']), row_ffn=lambda i: 1 + i // (t_dec // tm_ffn)),
    ]
    cos, sin = _rope_tables(t_dec)
    for st in streams:
        st['kv'] = st['states'] = None
    w_in_l = jnp.concatenate(
        [w_in[:, :, :N_RWKV_IN].astype(BF16), jnp.zeros((DEPTH, D_MODEL, N_RWKV_PAD - N_RWKV_IN), BF16),
         w_in[:, :, N_RWKV_IN:].astype(BF16)], axis=2)
    wo = w_out.astype(BF16)
    wi = w_ffn_in.astype(BF16)
    wf = w_ffn_out.astype(BF16)
    for l in range(DEPTH):
        rw = _rwkv_weights(p, l)
        mu = jnp.concatenate([shift_mu[l], jnp.zeros((2, N_RWKV_PAD - N_RWKV_IN), F32)], axis=1)
        lam_init = 0.8 - 0.6 * math.exp(-0.3 * l)
        sg = jnp.tile(subln_g[l], 2).reshape(1, LANES)
        for si, st in enumerate(streams):
            u_r, q, k_all, v_all, u_f = _inproj(st['x'], mod, l, st['row_in'], norm1_g[l], w_in_l, mu,
                                                 st['kv'], st['t'], st['tm_in'])
            st['kv'] = (k_all, v_all)
            if si == 0:
                s0 = None
                attn_ctx = None
            else:
                s0 = state_rwkv[:, l].astype(F32)
                attn_ctx = (cache_diff_k[:, l].reshape(n_dec, past, D_DIFF).astype(F32),
                            cache_diff_v[:, l].reshape(n_dec, past, D_DIFF).astype(F32), cos, sin)
            y_r, st['states'] = _rwkv(u_r, s0, st['t'], st['n_sub'], rw, st['states'], l)
            y_d = _attention(q, k_all, v_all, l, st['n_attn'], lam_init, diff_lambda[l], sg, attn_ctx)
            y_f = _fnet(u_f, st['t'], st['n_fnet'])
            st['x'] = _ffn(y_r, y_d, y_f, st['x'], mod, l, st['row_ffn'], norm2_g[l], final_norm_g,
                           wo, wi, wf, l == DEPTH - 1, tm_ffn)
    y_prompt = streams[0]['x'].reshape(n_ctx, t_ctx, D_MODEL)
    y_sample = streams[1]['x'].reshape(n_dec, t_dec, D_MODEL)
    new_k = streams[0]['kv'][0].reshape(n_ctx, DEPTH, t_ctx, H_DIFF, 2, D_QK)
    new_v = streams[0]['kv'][1].reshape(n_ctx, DEPTH, t_ctx, H_DIFF, HEAD_DIM)
    return (y_prompt, y_sample, streams[0]['states'], new_k, new_v)
```

```python
import functools
import math

import numpy as np
import jax
import jax.numpy as jnp
from jax import lax
from jax.experimental import pallas as pl
from jax.experimental.pallas import tpu as pltpu

F32 = jnp.float32
BF16 = jnp.bfloat16

D_MODEL = 1024
DEPTH = 2
GRID_W = 64
HEAD_DIM = 64
D_RWKV = 384
H_RWKV = D_RWKV // HEAD_DIM
D_DIFF = 384
H_DIFF = D_DIFF // HEAD_DIM
D_QK = HEAD_DIM // 2
D_FNET = D_MODEL - D_RWKV - D_DIFF
FNET_GROUPS = 4
FNET_GROUP_DIM = D_FNET // FNET_GROUPS
LORA_W = 32
LORA_A = 32
LORA_G = 64
N_RWKV_IN = 3 * D_RWKV + 2 * LORA_W + 2 * LORA_A + LORA_G
N_DIFF_IN = 3 * D_DIFF
D_FF = ((8 * D_MODEL + 3 * 256 - 1) // (3 * 256)) * 256
ROPE_PAIRS = D_QK // 4
ROPE_BASE = 10000.0
RMS_EPS = 1e-6
GN_EPS = 64e-5
SUBLN_EPS = 1e-5
DECAY_SCALE = math.exp(-0.5)
Q_SCALE = D_QK ** -0.5 * math.log2(math.e)

LANES = 128
N_RWKV_PAD = 11 * LANES
D_IN_PAD = N_RWKV_PAD + N_DIFF_IN + D_FNET
N_PAIR = H_RWKV // 2
CHUNK = 64
N_PARTS = 2
SIDE_WORK_MAX_UNITS = 12
MOD_ROWS = 8
ONES_ROWS = 16
VMEM_LIMIT = 60 * 1024 * 1024

MOD_TN = 1536
INPROJ_TM = 512
FFN_TM = 512
ATTN_TQ = 256
ATTN_STEP_ROWS = 1024
KEY_BLOCK = 512
ATTN_LOOKAHEAD = 8
FNET_STEP_ROWS = 1024
RWKV_STEP_ROWS = 1024
RWKV_ONE_STEP_ROWS = 2048


def _stream_plan(n_seq, seq_len):
    one_step = n_seq * seq_len <= RWKV_ONE_STEP_ROWS
    return dict(n_sub=n_seq if one_step else max(1, RWKV_STEP_ROWS // seq_len),
                n_attn=max(1, ATTN_STEP_ROWS // seq_len),
                n_fnet=max(1, FNET_STEP_ROWS // seq_len),
                tm_in=max(seq_len, INPROJ_TM))


def _cp(n_axes=1):
    return pltpu.CompilerParams(dimension_semantics=("arbitrary",) * n_axes,
                                vmem_limit_bytes=VMEM_LIMIT)


def _bdot(a, b):
    return jnp.dot(a.astype(BF16), b.astype(BF16), preferred_element_type=F32)


def _bdot_nt(a, b):
    return lax.dot_general(a.astype(BF16), b.astype(BF16), (((1,), (1,)), ((), ())),
                           preferred_element_type=F32)


def _split2(x):
    hi = x.astype(BF16)
    lo = (x - hi.astype(F32)).astype(BF16)
    return hi, lo


def _dot_x3(a, b):
    a_hi, a_lo = _split2(a)
    b_hi, b_lo = _split2(b)
    d = functools.partial(jnp.dot, preferred_element_type=F32)
    return d(a_hi, b_hi) + d(a_lo, b_hi) + d(a_hi, b_lo)


def _rms(x, g):
    return x * lax.rsqrt(jnp.mean(x * x, axis=-1, keepdims=True) + RMS_EPS) * g


def _mod_body(c_ref, w_ref, b_ref, o_ref):
    c = c_ref[...]
    a = c * jax.nn.sigmoid(c)
    o_ref[0] = _dot_x3(a, w_ref[0]) + b_ref[0]


def _modulation(cond, w_mod, b_mod):
    n_layers, _, n_out = w_mod.shape
    tn = MOD_TN
    return pl.pallas_call(
        _mod_body,
        grid=(n_layers, n_out // tn),
        in_specs=[pl.BlockSpec((MOD_ROWS, D_MODEL), lambda l, j: (0, 0)),
                  pl.BlockSpec((1, D_MODEL, tn), lambda l, j: (l, 0, j)),
                  pl.BlockSpec((1, 1, tn), lambda l, j: (l, 0, j))],
        out_specs=pl.BlockSpec((1, MOD_ROWS, tn), lambda l, j: (l, 0, j)),
        out_shape=jax.ShapeDtypeStruct((n_layers, MOD_ROWS, n_out), F32),
        compiler_params=_cp(2),
    )(cond, w_mod, b_mod.reshape(n_layers, 1, n_out))


def _mod_spec(layer, col, row_fn):
    return pl.BlockSpec((1, 1, D_MODEL), lambda i: (layer * MOD_ROWS + row_fn(i), 0, col))


def _put_layer(ref, idx, layer, aliased, val):
    if aliased:
        ref[idx] = val
    else:
        for other in range(DEPTH):
            ref[idx + (other,)] = val if other == layer else jnp.zeros_like(val)


def _inproj_body(seq_len, layer, aliased, x_ref, g_ref, sh_ref, sc_ref, w_ref, mu_ref, *refs):
    ur_ref, q_ref, k_ref, v_ref, uf_ref = refs[2:] if aliased else refs
    part = x_ref.shape[0] // N_PARTS
    parts = [slice(j * part, (j + 1) * part) for j in range(N_PARTS)]
    h = [(_rms(x_ref[r, :], g_ref[...]) * (1.0 + sc_ref[0]) + sh_ref[0]).astype(BF16) for r in parts]
    u = jnp.concatenate([_bdot_nt(hj, w_ref[...]) for hj in h], axis=0)
    ur = u[:, :N_RWKV_PAD]
    tm = ur.shape[0]
    pos = lax.broadcasted_iota(jnp.int32, (tm, 1), 0) & (seq_len - 1)
    prev = jnp.where(pos == 0, 0.0, pltpu.roll(ur, 1, axis=0))
    nxt = jnp.where(pos == seq_len - 1, 0.0, pltpu.roll(ur, tm - 1, axis=0))
    ur_ref[...] = (ur + mu_ref[0:1, :] * (prev - ur) + mu_ref[1:2, :] * (nxt - ur)).astype(BF16)
    q_ref[...] = (u[:, N_RWKV_PAD:N_RWKV_PAD + D_DIFF] * Q_SCALE).astype(BF16)
    for s in range(tm // seq_len):
        rows = slice(s * seq_len, (s + 1) * seq_len)
        _put_layer(k_ref, (s,), layer, aliased, u[rows, N_RWKV_PAD + D_DIFF:N_RWKV_PAD + 2 * D_DIFF])
        _put_layer(v_ref, (s,), layer, aliased, u[rows, N_RWKV_PAD + 2 * D_DIFF:N_RWKV_PAD + 3 * D_DIFF])
    uf_ref[...] = u[:, N_RWKV_PAD + N_DIFF_IN:].astype(BF16)


def _inproj(x, mod, layer, row_fn, g, w, mu, kv, seq_len, tm):
    n_tok = x.shape[0]
    kv_shape = jax.ShapeDtypeStruct((n_tok // seq_len, DEPTH, seq_len, D_DIFF), F32)
    assert tm % seq_len == 0 and seq_len & (seq_len - 1) == 0
    row = lambda i: (i, 0)
    const = lambda i: (0, 0)
    if kv is None:
        kv_spec = pl.BlockSpec((tm // seq_len, DEPTH, seq_len, D_DIFF), lambda i: (i, 0, 0, 0))
    else:
        kv_spec = pl.BlockSpec((tm // seq_len, None, seq_len, D_DIFF), lambda i: (i, layer, 0, 0))
    return pl.pallas_call(
        functools.partial(_inproj_body, seq_len, layer, kv is not None),
        grid=(n_tok // tm,),
        in_specs=[pl.BlockSpec((tm, D_MODEL), row),
                  pl.BlockSpec((1, D_MODEL), const),
                  _mod_spec(layer, 0, row_fn),
                  _mod_spec(layer, 1, row_fn),
                  pl.BlockSpec((None, D_IN_PAD, D_MODEL), lambda i: (layer, 0, 0)),
                  pl.BlockSpec((2, N_RWKV_PAD), const)]
        + [pl.BlockSpec(memory_space=pl.ANY)] * (0 if kv is None else 2),
        out_specs=[pl.BlockSpec((tm, N_RWKV_PAD), row),
                   pl.BlockSpec((tm, D_DIFF), row),
                   kv_spec, kv_spec,
                   pl.BlockSpec((tm, D_FNET), row)],
        out_shape=[jax.ShapeDtypeStruct((n_tok, N_RWKV_PAD), BF16),
                   jax.ShapeDtypeStruct((n_tok, D_DIFF), BF16),
                   kv_shape, kv_shape,
                   jax.ShapeDtypeStruct((n_tok, D_FNET), BF16)],
        input_output_aliases={} if kv is None else {6: 2, 7: 3},
        compiler_params=_cp(1),
    )(x, g.reshape(1, D_MODEL), mod, mod, w, mu, *(kv or ()))


def _lane_masks():
    lane = lax.broadcasted_iota(jnp.int32, (1, LANES), 1)
    return lane < HEAD_DIM, lane >= HEAD_DIM


def _rwkv_units(units, m0, m1, side=()):
    side = list(side)

    def run_side(drain=False):
        for gen in list(side):
            for _ in gen:
                if not drain:
                    break
            else:
                side.remove(gen)

    def bd(x):
        xb = x.astype(BF16)
        zero = jnp.zeros_like(xb)
        return jnp.concatenate([jnp.where(m0, xb, zero), jnp.where(m1, xb, zero)], axis=0)

    row = lax.broadcasted_iota(jnp.int32, (CHUNK, LANES), 0)
    col = lax.broadcasted_iota(jnp.int32, (CHUNK, LANES), 1) & (CHUNK - 1)
    eye = (col == row).astype(F32)
    r2 = lax.broadcasted_iota(jnp.int32, (LANES, LANES), 0) < HEAD_DIM
    c2 = lax.broadcasted_iota(jnp.int32, (LANES, LANES), 1) < HEAD_DIM
    rng = range(len(units))

    pre = []
    for rev, kk, r, v, kd, b, cum, cex, tot, s_prev in units:
        p_inv = jnp.exp(-cum)
        p_rem = jnp.exp(tot - cum)
        ab = -kk * jnp.exp(cex)
        rb = r * jnp.exp(cum)
        strict = (col > row) if rev else (col < row)
        incl = (col >= row) if rev else (col <= row)
        pre.append(dict(ab=ab, rb=rb, vbd=bd(v), strict=strict, incl=incl,
                        lhs=jnp.concatenate([ab, rb], axis=0),
                        rhs=jnp.concatenate([bd(b * p_inv), bd(kd * p_inv)], axis=0),
                        bk=jnp.concatenate([b * p_rem, kd * p_rem], axis=0)))

    run_side()
    mm = [_bdot_nt(q['lhs'], q['rhs']) for q in pre]
    run_side()
    m_ab = [jnp.where(pre[i]['strict'], mm[i][:CHUNK, :LANES], 0.0) for i in rng]
    m_ak = [jnp.where(pre[i]['strict'], mm[i][:CHUNK, LANES:], 0.0) for i in rng]
    m_r = [jnp.concatenate([jnp.where(pre[i]['incl'], mm[i][CHUNK:, :LANES], 0.0),
                            jnp.where(pre[i]['incl'], mm[i][CHUNK:, LANES:], 0.0)], axis=1) for i in rng]
    mv = [_bdot(m_ak[i], pre[i]['vbd']) for i in rng]
    run_side()

    t = [eye + m_ab[i] for i in rng]
    n = [_bdot(m_ab[i], bd(m_ab[i])) for i in rng]
    for _ in range(4):
        x = [_bdot(jnp.concatenate([t[i], n[i]], axis=0), bd(n[i])) for i in rng]
        t = [t[i] + x[i][:CHUNK] for i in rng]
        n = [x[i][CHUNK:] for i in rng]
        run_side()
    t = [t[i] + _bdot(t[i], bd(n[i])) for i in rng]

    w = [_bdot(t[i], jnp.concatenate([bd(pre[i]['ab']), bd(mv[i])], axis=1)) for i in rng]
    xs = [_bdot_nt(jnp.concatenate([w[i][:, :LANES], pre[i]['rb']], axis=0), units[i][9]) for i in rng]
    u = [xs[i][:CHUNK] + w[i][:, LANES:] for i in rng]
    run_side(drain=True)
    y = [xs[i][CHUNK:] + _bdot(m_r[i], jnp.concatenate([bd(u[i]), pre[i]['vbd']], axis=0)) for i in rng]
    z = [_bdot(jnp.concatenate([u[i], units[i][3]], axis=0).T, pre[i]['bk']) for i in rng]
    s_new = [units[i][9] * jnp.exp(units[i][8]) + jnp.where(r2 == c2, z[i], 0.0) for i in rng]
    return y, s_new


def _rwkv_body(seq_len, n_sub, layer, has_s0, aliased, u_ref, *refs):
    s0_ref = refs[0] if has_s0 else None
    refs = refs[1:] if has_s0 else refs
    wdec_ref, wicl_ref, wg_ref, w0a0_ref, vec_ref, bo_ref, tril_ref, triu_ref = refs[:8]
    (y_ref, sfin_ref, r_s, v_s, kk_s, kd_s, b_s, ci_s, ce_s, y_s, st_s) = refs[9 if aliased else 8:]
    n_chunk = seq_len // CHUNK
    assert n_chunk % 2 == 0 and n_chunk >= 4
    k_k = vec_ref[0:1, :]
    k_a = vec_ref[1:2, :]
    r_k = vec_ref[2:3, :]
    lnx_g = vec_ref[3:4, :]
    lnx_b = vec_ref[4:5, :]
    bo = bo_ref[...]

    def headsum(xb):
        return jnp.concatenate([jnp.dot(xb[:, p * LANES:(p + 1) * LANES], bo, preferred_element_type=F32)
                                for p in range(N_PAIR)], axis=1)

    def prep(rows):
        xs = u_ref[rows, :].astype(F32)
        r = xs[:, 0:D_RWKV]
        k = xs[:, D_RWKV:2 * D_RWKV]
        v = xs[:, 2 * D_RWKV:3 * D_RWKV]
        lora = xs[:, 3 * D_RWKV:3 * D_RWKV + LANES]
        t_lora = jnp.tanh(lora).astype(BF16)
        lora = lora.astype(BF16)
        kk = k * k_k
        kk2 = (kk * kk).astype(BF16)
        yield
        dec = _bdot(t_lora, wdec_ref[...])
        icl = _bdot(lora, wicl_ref[...])
        ss = headsum(kk2)
        yield
        logw = -DECAY_SCALE * jax.nn.sigmoid(w0a0_ref[0:1, :] + dec)
        a = jax.nn.sigmoid(w0a0_ref[1:2, :] + icl)
        kk = kk / jnp.maximum(jnp.sqrt(ss), 1e-12)
        a_f = a[:, :D_RWKV]
        a_b = a[:, D_RWKV:]
        kd_f = k * (1.0 + (a_f - 1.0) * k_a)
        kd_b = k * (1.0 + (a_b - 1.0) * k_a)
        lws = [_split2(logw[:, d * D_RWKV:(d + 1) * D_RWKV]) for d in range(2)]
        r_s[rows, :] = r
        v_s[rows, :] = v
        kk_s[rows, :] = kk
        kd_s[0, rows, :] = kd_f
        kd_s[1, rows, :] = kd_b
        b_s[0, rows, :] = kk * a_f
        b_s[1, rows, :] = kk * a_b
        yield
        dd = functools.partial(jnp.dot, preferred_element_type=F32)
        cums = [dd(tri_ref[...], lws[d][0]) + dd(tri_ref[...], lws[d][1])
                for d, tri_ref in enumerate((tril_ref, triu_ref))]
        yield
        for d in range(2):
            ci_s[d, rows, :] = cums[d]
            ce_s[d, rows, :] = cums[d] - logw[:, d * D_RWKV:(d + 1) * D_RWKV]

    def post(rows):
        y = y_s[rows, :]
        yb = y.astype(BF16)
        v = v_s[rows, :]
        bon = _split2(r_s[rows, :] * (kd_s[0, rows, :] + kd_s[1, rows, :]) * r_k)
        gd = u_ref[rows, 3 * D_RWKV + LANES:3 * D_RWKV + 2 * LANES].astype(F32)
        s_gd = jax.nn.sigmoid(gd).astype(BF16)
        yield
        mean = headsum(yb) * (1.0 / HEAD_DIM)
        bonus = (headsum(bon[0]) + headsum(bon[1])) * v
        g = _bdot(s_gd, wg_ref[...])
        yield
        yc = y - mean
        yc2 = (yc * yc).astype(BF16)
        yield
        var = headsum(yc2) * (1.0 / HEAD_DIM)
        yield
        yn = yc * lax.rsqrt(var + GN_EPS) * lnx_g + lnx_b
        y_ref[rows, :] = ((yn + bonus) * g).astype(BF16)

    def run_all(gens):
        gens = list(gens)
        while gens:
            gens = [gen for gen in gens if next(gen, gens) is not gens]

    def chunk_rows(i):
        return [pl.ds(pl.multiple_of(s * seq_len + (i if d == 0 else n_chunk - 1 - i) * CHUNK, CHUNK), CHUNK)
                for s in range(n_sub) for d in range(2)]

    st_s[...] = s0_ref[...] if has_s0 else jnp.zeros_like(st_s)
    y_s[...] = jnp.zeros_like(y_s)
    m0, m1 = _lane_masks()

    def scan_step(i, prep_next, post_prev):
        rows_sd = chunk_rows(i)
        side = []
        if prep_next:
            side = [prep(rows) for rows in chunk_rows(i + 1)]
        if post_prev:
            side = [post(rows) for rows in chunk_rows(i - 1)]
        units = []
        for s in range(n_sub):
            for d in range(2):
                rows = rows_sd[s * 2 + d]
                cum = ci_s[d, rows, :]
                cex = ce_s[d, rows, :]
                tot = cum[CHUNK - 1:CHUNK] if d == 0 else cum[0:1]
                kk = kk_s[rows, :]
                r = r_s[rows, :]
                v = v_s[rows, :]
                kd = kd_s[d, rows, :]
                b = b_s[d, rows, :]
                for p in range(N_PAIR):
                    sl = slice(p * LANES, (p + 1) * LANES)
                    units.append((d == 1, kk[:, sl], r[:, sl], v[:, sl], kd[:, sl], b[:, sl],
                                  cum[:, sl], cex[:, sl], tot[:, sl], st_s[s, d, p]))
        ys, s_new = _rwkv_units(units, m0, m1, side)
        for s in range(n_sub):
            for d in range(2):
                base = (s * 2 + d) * N_PAIR
                for p in range(N_PAIR):
                    st_s[s, d, p] = s_new[base + p]
                rows = rows_sd[s * 2 + d]
                y_s[rows, :] = y_s[rows, :] + jnp.concatenate(ys[base:base + N_PAIR], axis=1)

    def loop(lo, hi, **kw):
        lax.fori_loop(lo, hi, lambda i, c: (scan_step(i, **kw), c)[1], 0)

    half = n_chunk // 2
    if 2 * N_PAIR * n_sub <= SIDE_WORK_MAX_UNITS:
        run_all(prep(rows) for rows in chunk_rows(0))
        loop(0, half - 1, prep_next=True, post_prev=False)
        loop(half - 1, half + 1, prep_next=False, post_prev=False)
        loop(half + 1, n_chunk, prep_next=False, post_prev=True)
        run_all(post(rows) for rows in chunk_rows(n_chunk - 1))
    else:
        for i in range(half):
            run_all(prep(rows) for rows in chunk_rows(i))
        loop(0, n_chunk, prep_next=False, post_prev=False)
        for i in range(half):
            run_all(post(rows) for rows in chunk_rows(i))

    for s in range(n_sub):
        for d in range(2):
            for p in range(N_PAIR):
                st = st_s[s, d, p]
                for h in range(2):
                    rows = slice(h * HEAD_DIM, (h + 1) * HEAD_DIM)
                    val = st[rows, rows]
                    if aliased:
                        sfin_ref[s, d, 2 * p + h] = val
                    else:
                        for other in range(DEPTH):
                            sfin_ref[s, other, d, 2 * p + h] = val if other == layer else jnp.zeros_like(val)


def _rwkv(u_r, s0, seq_len, n_sub, wts, states, layer):
    n_seq = u_r.shape[0] // seq_len
    rows = n_sub * seq_len
    const2 = lambda b: (0, 0)
    st_shape = (n_sub, 2, N_PAIR, LANES, LANES)
    tok = pltpu.VMEM((rows, D_RWKV), F32)
    tok2 = pltpu.VMEM((2, rows, D_RWKV), F32)
    single = n_seq == n_sub
    in_specs = [pl.BlockSpec((rows, N_RWKV_PAD), lambda b: (b, 0),
                             pipeline_mode=pl.Buffered(1) if single else None)]
    args = [u_r]
    if s0 is not None:
        in_specs.append(pl.BlockSpec(st_shape, lambda b: (b, 0, 0, 0, 0)))
        args.append(_block_diag_state(s0))
    in_specs += [pl.BlockSpec(w.shape, const2) for w in wts]
    args += list(wts)
    if states is not None:
        in_specs.append(pl.BlockSpec(memory_space=pl.ANY))
        args.append(states)
    return pl.pallas_call(
        functools.partial(_rwkv_body, seq_len, n_sub, layer, s0 is not None, states is not None),
        grid=(n_seq // n_sub,),
        in_specs=in_specs,
        out_specs=[pl.BlockSpec((rows, D_RWKV), lambda b: (b, 0),
                                pipeline_mode=pl.Buffered(1) if single else None),
                   pl.BlockSpec((n_sub, DEPTH, 2, H_RWKV, HEAD_DIM, HEAD_DIM), lambda b: (b, 0, 0, 0, 0, 0))
                   if states is None else
                   pl.BlockSpec((n_sub, None, 2, H_RWKV, HEAD_DIM, HEAD_DIM),
                                lambda b: (b, layer, 0, 0, 0, 0))],
        out_shape=[jax.ShapeDtypeStruct((n_seq * seq_len, D_RWKV), BF16),
                   jax.ShapeDtypeStruct((n_seq, DEPTH, 2, H_RWKV, HEAD_DIM, HEAD_DIM), F32)],
        scratch_shapes=[tok] * 3 + [tok2] * 4 + [tok, pltpu.VMEM(st_shape, F32)],
        input_output_aliases={} if states is None else {len(args) - 1: 1},
        compiler_params=_cp(1),
    )(*args)


def _rwkv_weights(p, l):
    z = functools.partial(jnp.zeros, dtype=F32)
    wdec = z((LANES, 2 * D_RWKV))
    wdec = wdec.at[0:LORA_W, :D_RWKV].set(p['decay_up'][l, 0])
    wdec = wdec.at[LORA_W:2 * LORA_W, D_RWKV:].set(p['decay_up'][l, 1])
    wicl = z((LANES, 2 * D_RWKV))
    wicl = wicl.at[2 * LORA_W:2 * LORA_W + LORA_A, :D_RWKV].set(p['iclr_up'][l, 0])
    wicl = wicl.at[2 * LORA_W + LORA_A:2 * LORA_W + 2 * LORA_A, D_RWKV:].set(p['iclr_up'][l, 1])
    wg = z((LANES, D_RWKV)).at[0:LORA_G].set(p['gate_up'][l])
    w0a0 = jnp.stack([p['decay_w0'][l].reshape(-1), p['iclr_a0'][l].reshape(-1)])
    vec = jnp.stack([p['k_k'][l], p['k_a'][l], p['r_k'][l].reshape(-1), p['lnx_g'][l], p['lnx_b'][l],
                     z((D_RWKV,)), z((D_RWKV,)), z((D_RWKV,))])
    head = np.arange(LANES) // HEAD_DIM
    bo = jnp.asarray(head[:, None] == head[None, :], BF16)
    idx = np.arange(CHUNK)
    tril = jnp.asarray(idx[None, :] <= idx[:, None], BF16)
    triu = jnp.asarray(idx[None, :] >= idx[:, None], BF16)
    return [wdec.astype(BF16), wicl.astype(BF16), wg.astype(BF16), w0a0, vec, bo, tril, triu]


def _rope(x, cos, sin):
    lane = lax.broadcasted_iota(jnp.int32, (1, LANES), 1)
    first_half = (lane & ROPE_PAIRS) == 0
    partner = jnp.where(first_half, pltpu.roll(x, LANES - ROPE_PAIRS, axis=1),
                        pltpu.roll(x, ROPE_PAIRS, axis=1))
    return x * cos + partner * sin


def _attn_body(has_ctx, lam_init, *refs):
    if has_ctx:
        (q_ref, k_ref, v_ref, lp_ref, sg_ref, kc_ref, vc_ref, cq_ref, sq_ref, ck_ref, sk_ref,
         o_ref) = refs
    else:
        q_ref, k_ref, v_ref, lp_ref, sg_ref, o_ref = refs
    n_pair = D_DIFF // LANES
    lp = lp_ref[...]
    lam = (jnp.exp(jnp.sum(lp[0:1] * lp[1:2], axis=-1, keepdims=True))
           - jnp.exp(jnp.sum(lp[2:3] * lp[3:4], axis=-1, keepdims=True)) + lam_init)
    lane = lax.broadcasted_iota(jnp.int32, (1, LANES), 1)

    tq = q_ref.shape[0] // k_ref.shape[0]

    def operands(seq, p):
        sl = slice(p * LANES, (p + 1) * LANES)
        q = q_ref[seq * tq:(seq + 1) * tq, sl]
        k = k_ref[seq, :, sl]
        v = v_ref[seq, :, sl]
        if has_ctx:
            q = _rope(q.astype(F32), cq_ref[...], sq_ref[...])
            k = _rope(k, ck_ref[...], sk_ref[...])
            k = jnp.concatenate([kc_ref[0, :, sl], k], axis=0)
            v = jnp.concatenate([vc_ref[0, :, sl], v], axis=0)
        vt = v.T
        ones = jnp.ones((ONES_ROWS, v.shape[0]), F32)
        vts = [jnp.concatenate([vt[h * HEAD_DIM:(h + 1) * HEAD_DIM], ones], axis=0).astype(BF16)
               for h in range(2)]
        return q.astype(BF16), k.astype(BF16), vts

    n_keys = k_ref.shape[1] + (kc_ref.shape[1] if has_ctx else 0)
    blocks = range(0, n_keys, KEY_BLOCK)
    ops, tiles = {}, []
    for seq in range(k_ref.shape[0]):
        for p in range(n_pair):
            for j in blocks:
                tiles += [(seq, p, h, m, j) for h in range(2) for m in range(2)]

    def score(tile):
        seq, p, h, m, j = tile
        if (seq, p) not in ops:
            ops[seq, p] = operands(seq, p)
        q, k, _ = ops[seq, p]
        lo = h * HEAD_DIM + m * D_QK
        sel = (lane >= lo) & (lane < lo + D_QK)
        return _bdot_nt(k[j:j + KEY_BLOCK], jnp.where(sel, q, jnp.zeros_like(q)))

    run_max, acc = {}, {}

    def consume(tile, s):
        seq, p, h, m, j = tile
        vt = ops[seq, p][2][h][:, j:j + KEY_BLOCK]
        c = (seq, p, h, m)
        mj = jnp.max(s, axis=0, keepdims=True)
        if j == 0:
            run_max[c] = mj
            acc[c] = jnp.dot(vt, jnp.exp2(s - mj).astype(BF16), preferred_element_type=F32)
        else:
            m_new = jnp.maximum(run_max[c], mj)
            acc[c] = (acc[c] * jnp.exp2(run_max[c] - m_new)
                      + jnp.dot(vt, jnp.exp2(s - m_new).astype(BF16), preferred_element_type=F32))
            run_max[c] = m_new
        if j == blocks[-1] and (h, m) == (1, 1):
            finish(seq, p)

    def finish(seq, p):
        halves = []
        for h in range(2):
            a0, a1 = acc.pop((seq, p, h, 0)), acc.pop((seq, p, h, 1))
            o = (a0[:HEAD_DIM] * (1.0 / a0[HEAD_DIM:HEAD_DIM + 1])
                 - lam * (a1[:HEAD_DIM] * (1.0 / a1[HEAD_DIM:HEAD_DIM + 1])))
            ms = jnp.mean(o * o, axis=0, keepdims=True)
            halves.append(o * lax.rsqrt(ms + SUBLN_EPS))
        o_ref[seq * tq:(seq + 1) * tq, p * LANES:(p + 1) * LANES] = (
            jnp.concatenate(halves, axis=0).T * sg_ref[...] * (1.0 - lam_init)).astype(BF16)

    pending = []
    for tile in tiles:
        pending.append((tile, score(tile)))
        if len(pending) > ATTN_LOOKAHEAD:
            consume(*pending.pop(0))
    for item in pending:
        consume(*item)


def _attention(q, k_all, v_all, layer, n_sub, lam_init, lp, sg, ctx=None):
    n_tok = q.shape[0]
    n_seq, _, seq_len, _ = k_all.shape
    tq = ATTN_TQ
    nq = seq_len // tq
    assert n_sub == 1 or nq == 1
    kv_spec = pl.BlockSpec((n_sub, None, seq_len, D_DIFF), lambda b, i: (b, layer, 0, 0))
    in_specs = [pl.BlockSpec((n_sub * tq, D_DIFF), lambda b, i: (b * nq + i, 0)),
                kv_spec, kv_spec,
                pl.BlockSpec(lp.shape, lambda b, i: (0, 0)),
                pl.BlockSpec((1, LANES), lambda b, i: (0, 0))]
    args = [q, k_all, v_all, lp, sg]
    if ctx is not None:
        kc, vc, cos, sin = ctx
        past = kc.shape[1]
        in_specs += [pl.BlockSpec((1, past, D_DIFF), lambda b, i: (b, 0, 0)),
                     pl.BlockSpec((1, past, D_DIFF), lambda b, i: (b, 0, 0)),
                     pl.BlockSpec((tq, LANES), lambda b, i: (i, 0)),
                     pl.BlockSpec((tq, LANES), lambda b, i: (i, 0)),
                     pl.BlockSpec((seq_len, LANES), lambda b, i: (0, 0)),
                     pl.BlockSpec((seq_len, LANES), lambda b, i: (0, 0))]
        args += [kc, vc, cos, sin, cos, sin]
    return pl.pallas_call(
        functools.partial(_attn_body, ctx is not None, lam_init),
        grid=(n_seq // n_sub, nq),
        in_specs=in_specs,
        out_specs=pl.BlockSpec((n_sub * tq, D_DIFF), lambda b, i: (b * nq + i, 0)),
        out_shape=jax.ShapeDtypeStruct((n_tok, D_DIFF), BF16),
        compiler_params=_cp(2),
    )(*args)


def _rope_tables(seq_len):
    t = jnp.arange(seq_len)
    pos = jnp.stack([(t // GRID_W).astype(F32), (t % GRID_W).astype(F32)], axis=1)
    inv = 1.0 / (ROPE_BASE ** (jnp.arange(ROPE_PAIRS, dtype=F32) / ROPE_PAIRS))
    ang = pos[:, :, None] * inv
    d = np.arange(LANES) % D_QK
    axis = d // (2 * ROPE_PAIRS)
    second = (d % (2 * ROPE_PAIRS)) // ROPE_PAIRS
    idx = d % ROPE_PAIRS
    cos = jnp.cos(ang)[:, axis, idx]
    sin = jnp.sin(ang)[:, axis, idx] * jnp.asarray(np.where(second == 1, 1.0, -1.0), F32)
    return cos, sin


def _fnet_body(seq_len, x_ref, ct_ref, st_ref, cc_ref, sc_ref, o_ref):
    n_sub = x_ref.shape[0] // seq_len
    x = x_ref[...]
    xc = jnp.dot(x, cc_ref[...], preferred_element_type=F32)
    xs = jnp.dot(x, sc_ref[...], preferred_element_type=F32)
    wide = lambda a: jnp.concatenate([a[s * seq_len:(s + 1) * seq_len] for s in range(n_sub)], axis=1)
    y = _bdot(ct_ref[...], wide(xc)) - _bdot(st_ref[...], wide(xs))
    for s in range(n_sub):
        o_ref[s * seq_len:(s + 1) * seq_len, :] = y[:, s * D_FNET:(s + 1) * D_FNET].astype(BF16)


def _dft_consts(n, block=1):
    idx = np.arange(n)
    ang = 2.0 * np.pi * ((idx[:, None] * idx[None, :]) % n) / n
    return [jnp.asarray(np.kron(np.eye(block), m).astype(np.float32)).astype(BF16)
            for m in (np.cos(ang) / np.sqrt(n), np.sin(ang) / np.sqrt(n))]


def _fnet(u_f, seq_len, n_sub):
    n_tok = u_f.shape[0]
    rows = n_sub * seq_len
    consts = _dft_consts(seq_len) + _dft_consts(FNET_GROUP_DIM, FNET_GROUPS)
    const = lambda b: (0, 0)
    return pl.pallas_call(
        functools.partial(_fnet_body, seq_len),
        grid=(n_tok // rows,),
        in_specs=[pl.BlockSpec((rows, D_FNET), lambda b: (b, 0))]
        + [pl.BlockSpec(c.shape, const) for c in consts],
        out_specs=pl.BlockSpec((rows, D_FNET), lambda b: (b, 0)),
        out_shape=jax.ShapeDtypeStruct((n_tok, D_FNET), BF16),
        compiler_params=_cp(1),
    )(u_f, *consts)


def _ffn_body(final, yr_ref, yd_ref, yf_ref, x_ref, g1_ref, sh2_ref, sc2_ref, g2_ref, n2_ref, fg_ref,
              wo_ref, wi_ref, wf_ref, o_ref):
    part = x_ref.shape[0] // N_PARTS
    parts = [slice(j * part, (j + 1) * part) for j in range(N_PARTS)]
    y = [jnp.dot(jnp.concatenate([yr_ref[r, :], yd_ref[r, :], yf_ref[r, :]], axis=1), wo_ref[...],
                 preferred_element_type=F32) for r in parts]
    x = [x_ref[r, :] + g1_ref[0] * y[j] for j, r in enumerate(parts)]
    h = [(_rms(xj, n2_ref[...]) * (1.0 + sc2_ref[0]) + sh2_ref[0]).astype(BF16) for xj in x]
    z = [jnp.dot(hj, wi_ref[...], preferred_element_type=F32) for hj in h]
    act = [zj[:, :D_FF] * jax.nn.sigmoid(zj[:, :D_FF]) * zj[:, D_FF:] for zj in z]
    f = [_bdot(aj, wf_ref[...]) for aj in act]
    for j, r in enumerate(parts):
        xj = x[j] + g2_ref[0] * f[j]
        o_ref[r, :] = _rms(xj, fg_ref[...]) if final else xj


def _ffn(y_r, y_d, y_f, x, mod, layer, row_fn, n2, fg, wo, wi, wf, final, tm):
    n_tok = x.shape[0]
    row = lambda i: (i, 0)
    const = lambda i: (0, 0)
    return pl.pallas_call(
        functools.partial(_ffn_body, final),
        grid=(n_tok // tm,),
        in_specs=[pl.BlockSpec((tm, D_RWKV), row),
                  pl.BlockSpec((tm, D_DIFF), row),
                  pl.BlockSpec((tm, D_FNET), row),
                  pl.BlockSpec((tm, D_MODEL), row),
                  _mod_spec(layer, 2, row_fn),
                  _mod_spec(layer, 3, row_fn),
                  _mod_spec(layer, 4, row_fn),
                  _mod_spec(layer, 5, row_fn),
                  pl.BlockSpec((1, D_MODEL), const),
                  pl.BlockSpec((1, D_MODEL), const),
                  pl.BlockSpec((None,) + wo.shape[1:], lambda i: (layer, 0, 0), pipeline_mode=pl.Buffered(1)),
                  pl.BlockSpec((None,) + wi.shape[1:], lambda i: (layer, 0, 0), pipeline_mode=pl.Buffered(1)),
                  pl.BlockSpec((None,) + wf.shape[1:], lambda i: (layer, 0, 0), pipeline_mode=pl.Buffered(1))],
        out_specs=pl.BlockSpec((tm, D_MODEL), row),
        out_shape=jax.ShapeDtypeStruct((n_tok, D_MODEL), F32),
        compiler_params=_cp(1),
    )(y_r, y_d, y_f, x, mod, mod, mod, mod, n2.reshape(1, D_MODEL), fg.reshape(1, D_MODEL), wo, wi, wf)


def _block_diag_state(s):
    b = s.shape[0]
    s = s.reshape(b, 2, N_PAIR, 2, HEAD_DIM, HEAD_DIM)
    eye = jnp.eye(2, dtype=s.dtype)
    s = s[:, :, :, :, :, None, :] * eye[None, None, None, :, None, :, None]
    return s.reshape(b, 2, N_PAIR, LANES, LANES)


def kernel(x_prompt, x_sample, c, state_rwkv, cache_diff_k, cache_diff_v, c_ctx, norm1_g, norm2_g, final_norm_g, w_mod, b_mod, w_in, w_out, shift_mu, decay_w0, decay_up, iclr_a0, iclr_up, gate_up, k_k, k_a, r_k, lnx_g, lnx_b, diff_lambda, subln_g, w_ffn_in, w_ffn_out):
    p = dict(shift_mu=shift_mu, decay_w0=decay_w0, decay_up=decay_up, iclr_a0=iclr_a0, iclr_up=iclr_up,
             gate_up=gate_up, k_k=k_k, k_a=k_a, r_k=r_k, lnx_g=lnx_g, lnx_b=lnx_b)
    n_ctx, t_ctx, _ = x_prompt.shape
    n_dec, t_dec, _ = x_sample.shape
    past = cache_diff_k.shape[2]

    cond = jnp.concatenate([c_ctx[None, :], c, jnp.zeros((MOD_ROWS - 1 - n_dec, D_MODEL), F32)], axis=0)
    mod = _modulation(cond, w_mod, b_mod).reshape(DEPTH * MOD_ROWS, 1, 6 * D_MODEL)

    tm_ffn = FFN_TM
    ctx_plan = _stream_plan(n_ctx, t_ctx)
    dec_plan = _stream_plan(n_dec, t_dec)
    streams = [
        dict(x=x_prompt.reshape(n_ctx * t_ctx, D_MODEL), t=t_ctx, n=n_ctx, **ctx_plan,
             row_in=lambda i: 0, row_ffn=lambda i: 0),
        dict(x=x_sample.reshape(n_dec * t_dec, D_MODEL), t=t_dec, n=n_dec, **dec_plan,
             row_in=lambda i: 1 + i // (t_dec // dec_plan['tm_in']), row_ffn=lambda i: 1 + i // (t_dec // tm_ffn)),
    ]
    cos, sin = _rope_tables(t_dec)
    for st in streams:
        st['kv'] = st['states'] = None
    w_t = jnp.swapaxes(w_in, 1, 2)
    w_in_l = jnp.concatenate(
        [w_t[:, :N_RWKV_IN], jnp.zeros((DEPTH, N_RWKV_PAD - N_RWKV_IN, D_MODEL), F32), w_t[:, N_RWKV_IN:]],
        axis=1).astype(BF16)
    wo = w_out.astype(BF16)
    wi = w_ffn_in.astype(BF16)
    wf = w_ffn_out.astype(BF16)
    for l in range(DEPTH):
        rw = _rwkv_weights(p, l)
        mu = jnp.concatenate([shift_mu[l], jnp.zeros((2, N_RWKV_PAD - N_RWKV_IN), F32)], axis=1)
        lam_init = 0.8 - 0.6 * math.exp(-0.3 * l)
        sg = jnp.tile(subln_g[l], 2).reshape(1, LANES)
        for si, st in enumerate(streams):
            u_r, q, k_all, v_all, u_f = _inproj(st['x'], mod, l, st['row_in'], norm1_g[l], w_in_l, mu,
                                                 st['kv'], st['t'], st['tm_in'])
            st['kv'] = (k_all, v_all)
            if si == 0:
                s0 = None
                attn_ctx = None
            else:
                s0 = state_rwkv[:, l].astype(F32)
                attn_ctx = (cache_diff_k[:, l].reshape(n_dec, past, D_DIFF).astype(F32),
                            cache_diff_v[:, l].reshape(n_dec, past, D_DIFF).astype(F32), cos, sin)
            y_r, st['states'] = _rwkv(u_r, s0, st['t'], st['n_sub'], rw, st['states'], l)
            y_d = _attention(q, k_all, v_all, l, st['n_attn'], lam_init, diff_lambda[l], sg, attn_ctx)
            y_f = _fnet(u_f, st['t'], st['n_fnet'])
            st['x'] = _ffn(y_r, y_d, y_f, st['x'], mod, l, st['row_ffn'], norm2_g[l], final_norm_g,
                           wo, wi, wf, l == DEPTH - 1, tm_ffn)
    y_prompt = streams[0]['x'].reshape(n_ctx, t_ctx, D_MODEL)
    y_sample = streams[1]['x'].reshape(n_dec, t_dec, D_MODEL)
    new_k = streams[0]['kv'][0].reshape(n_ctx, DEPTH, t_ctx, H_DIFF, 2, D_QK)
    new_v = streams[0]['kv'][1].reshape(n_ctx, DEPTH, t_ctx, H_DIFF, HEAD_DIM)
    return (y_prompt, y_sample, streams[0]['states'], new_k, new_v)
```

```python
import functools
import math

import numpy as np
import jax
import jax.numpy as jnp
from jax import lax
from jax.experimental import pallas as pl
from jax.experimental.pallas import tpu as pltpu

F32 = jnp.float32
BF16 = jnp.bfloat16

D_MODEL = 1024
DEPTH = 2
GRID_W = 64
HEAD_DIM = 64
D_RWKV = 384
H_RWKV = D_RWKV // HEAD_DIM
D_DIFF = 384
H_DIFF = D_DIFF // HEAD_DIM
D_QK = HEAD_DIM // 2
D_FNET = D_MODEL - D_RWKV - D_DIFF
FNET_GROUPS = 4
FNET_GROUP_DIM = D_FNET // FNET_GROUPS
LORA_W = 32
LORA_A = 32
LORA_G = 64
N_RWKV_IN = 3 * D_RWKV + 2 * LORA_W + 2 * LORA_A + LORA_G
N_DIFF_IN = 3 * D_DIFF
D_FF = ((8 * D_MODEL + 3 * 256 - 1) // (3 * 256)) * 256
ROPE_PAIRS = D_QK // 4
ROPE_BASE = 10000.0
RMS_EPS = 1e-6
GN_EPS = 64e-5
SUBLN_EPS = 1e-5
DECAY_SCALE = math.exp(-0.5)
Q_SCALE = D_QK ** -0.5 * math.log2(math.e)

LANES = 128
N_RWKV_PAD = 11 * LANES
D_IN_PAD = N_RWKV_PAD + N_DIFF_IN + D_FNET
N_PAIR = H_RWKV // 2
CHUNK = 64
N_PARTS = 2
SIDE_WORK_MAX_UNITS = 12
RWKV_BLK = 256
MOD_ROWS = 8
ONES_ROWS = 16
VMEM_LIMIT = 60 * 1024 * 1024

MOD_TN = 1536
INPROJ_TM = 512
FFN_TM = 512
ATTN_TQ = 256
ATTN_STEP_ROWS = 1024
KEY_BLOCK = 512
ATTN_LOOKAHEAD = 8
FNET_STEP_ROWS = 1024
RWKV_STEP_ROWS = 1024
RWKV_ONE_STEP_ROWS = 2048


def _stream_plan(n_seq, seq_len):
    one_step = n_seq * seq_len <= RWKV_ONE_STEP_ROWS
    return dict(n_sub=n_seq if one_step else max(1, RWKV_STEP_ROWS // seq_len),
                n_attn=max(1, ATTN_STEP_ROWS // seq_len),
                n_fnet=max(1, FNET_STEP_ROWS // seq_len),
                tm_in=max(seq_len, INPROJ_TM))


def _cp(n_axes=1):
    return pltpu.CompilerParams(dimension_semantics=("arbitrary",) * n_axes,
                                vmem_limit_bytes=VMEM_LIMIT)


def _bdot(a, b):
    return jnp.dot(a.astype(BF16), b.astype(BF16), preferred_element_type=F32)


def _bdot_nt(a, b):
    return lax.dot_general(a.astype(BF16), b.astype(BF16), (((1,), (1,)), ((), ())),
                           preferred_element_type=F32)


def _split2(x):
    hi = x.astype(BF16)
    lo = (x - hi.astype(F32)).astype(BF16)
    return hi, lo


def _dot_x3(a, b):
    a_hi, a_lo = _split2(a)
    b_hi, b_lo = _split2(b)
    d = functools.partial(jnp.dot, preferred_element_type=F32)
    return d(a_hi, b_hi) + d(a_lo, b_hi) + d(a_hi, b_lo)


def _rms(x, g):
    return x * lax.rsqrt(jnp.mean(x * x, axis=-1, keepdims=True) + RMS_EPS) * g


def _mod_body(c_ref, w_ref, b_ref, o_ref):
    c = c_ref[...]
    a = c * jax.nn.sigmoid(c)
    o_ref[0] = _dot_x3(a, w_ref[0]) + b_ref[0]


def _modulation(cond, w_mod, b_mod):
    n_layers, _, n_out = w_mod.shape
    tn = MOD_TN
    return pl.pallas_call(
        _mod_body,
        grid=(n_layers, n_out // tn),
        in_specs=[pl.BlockSpec((MOD_ROWS, D_MODEL), lambda l, j: (0, 0)),
                  pl.BlockSpec((1, D_MODEL, tn), lambda l, j: (l, 0, j)),
                  pl.BlockSpec((1, 1, tn), lambda l, j: (l, 0, j))],
        out_specs=pl.BlockSpec((1, MOD_ROWS, tn), lambda l, j: (l, 0, j)),
        out_shape=jax.ShapeDtypeStruct((n_layers, MOD_ROWS, n_out), F32),
        compiler_params=_cp(2),
    )(cond, w_mod, b_mod.reshape(n_layers, 1, n_out))


def _mod_spec(layer, col, row_fn):
    return pl.BlockSpec((1, 1, D_MODEL), lambda i: (layer * MOD_ROWS + row_fn(i), 0, col))


def _put_layer(ref, idx, layer, aliased, val):
    if aliased:
        ref[idx] = val
    else:
        for other in range(DEPTH):
            ref[idx + (other,)] = val if other == layer else jnp.zeros_like(val)


def _inproj_body(seq_len, layer, aliased, x_ref, g_ref, sh_ref, sc_ref, w_ref, mu_ref, *refs):
    ur_ref, q_ref, k_ref, v_ref, uf_ref = refs[2:] if aliased else refs
    part = x_ref.shape[0] // N_PARTS
    parts = [slice(j * part, (j + 1) * part) for j in range(N_PARTS)]
    h = [(_rms(x_ref[r, :], g_ref[...]) * (1.0 + sc_ref[0]) + sh_ref[0]).astype(BF16) for r in parts]
    u = jnp.concatenate([_bdot_nt(hj, w_ref[...]) for hj in h], axis=0)
    ur = u[:, :N_RWKV_PAD]
    tm = ur.shape[0]
    pos = lax.broadcasted_iota(jnp.int32, (tm, 1), 0) & (seq_len - 1)
    prev = jnp.where(pos == 0, 0.0, pltpu.roll(ur, 1, axis=0))
    nxt = jnp.where(pos == seq_len - 1, 0.0, pltpu.roll(ur, tm - 1, axis=0))
    ur_ref[...] = (ur + mu_ref[0:1, :] * (prev - ur) + mu_ref[1:2, :] * (nxt - ur)).astype(BF16)
    q_ref[...] = (u[:, N_RWKV_PAD:N_RWKV_PAD + D_DIFF] * Q_SCALE).astype(BF16)
    for s in range(tm // seq_len):
        rows = slice(s * seq_len, (s + 1) * seq_len)
        _put_layer(k_ref, (s,), layer, aliased, u[rows, N_RWKV_PAD + D_DIFF:N_RWKV_PAD + 2 * D_DIFF])
        _put_layer(v_ref, (s,), layer, aliased, u[rows, N_RWKV_PAD + 2 * D_DIFF:N_RWKV_PAD + 3 * D_DIFF])
    uf_ref[...] = u[:, N_RWKV_PAD + N_DIFF_IN:].astype(BF16)


def _inproj(x, mod, layer, row_fn, g, w, mu, kv, seq_len, tm):
    n_tok = x.shape[0]
    kv_shape = jax.ShapeDtypeStruct((n_tok // seq_len, DEPTH, seq_len, D_DIFF), F32)
    assert tm % seq_len == 0 and seq_len & (seq_len - 1) == 0
    row = lambda i: (i, 0)
    const = lambda i: (0, 0)
    if kv is None:
        kv_spec = pl.BlockSpec((tm // seq_len, DEPTH, seq_len, D_DIFF), lambda i: (i, 0, 0, 0))
    else:
        kv_spec = pl.BlockSpec((tm // seq_len, None, seq_len, D_DIFF), lambda i: (i, layer, 0, 0))
    return pl.pallas_call(
        functools.partial(_inproj_body, seq_len, layer, kv is not None),
        grid=(n_tok // tm,),
        in_specs=[pl.BlockSpec((tm, D_MODEL), row),
                  pl.BlockSpec((1, D_MODEL), const),
                  _mod_spec(layer, 0, row_fn),
                  _mod_spec(layer, 1, row_fn),
                  pl.BlockSpec((None, D_IN_PAD, D_MODEL), lambda i: (layer, 0, 0)),
                  pl.BlockSpec((2, N_RWKV_PAD), const)]
        + [pl.BlockSpec(memory_space=pl.ANY)] * (0 if kv is None else 2),
        out_specs=[pl.BlockSpec((tm, N_RWKV_PAD), row),
                   pl.BlockSpec((tm, D_DIFF), row),
                   kv_spec, kv_spec,
                   pl.BlockSpec((tm, D_FNET), row)],
        out_shape=[jax.ShapeDtypeStruct((n_tok, N_RWKV_PAD), BF16),
                   jax.ShapeDtypeStruct((n_tok, D_DIFF), BF16),
                   kv_shape, kv_shape,
                   jax.ShapeDtypeStruct((n_tok, D_FNET), BF16)],
        input_output_aliases={} if kv is None else {6: 2, 7: 3},
        compiler_params=_cp(1),
    )(x, g.reshape(1, D_MODEL), mod, mod, w, mu, *(kv or ()))


def _lane_masks():
    lane = lax.broadcasted_iota(jnp.int32, (1, LANES), 1)
    return lane < HEAD_DIM, lane >= HEAD_DIM


def _rwkv_units(units, m0, m1, side=()):
    side = list(side)

    def run_side(drain=False):
        for gen in list(side):
            for _ in gen:
                if not drain:
                    break
            else:
                side.remove(gen)

    def bd(x):
        xb = x.astype(BF16)
        zero = jnp.zeros_like(xb)
        return jnp.concatenate([jnp.where(m0, xb, zero), jnp.where(m1, xb, zero)], axis=0)

    row = lax.broadcasted_iota(jnp.int32, (CHUNK, LANES), 0)
    col = lax.broadcasted_iota(jnp.int32, (CHUNK, LANES), 1) & (CHUNK - 1)
    eye = (col == row).astype(F32)
    r2 = lax.broadcasted_iota(jnp.int32, (LANES, LANES), 0) < HEAD_DIM
    c2 = lax.broadcasted_iota(jnp.int32, (LANES, LANES), 1) < HEAD_DIM
    rng = range(len(units))

    pre = []
    for rev, kk, r, v, kd, b, cum, cex, tot, s_prev in units:
        p_inv = jnp.exp(-cum)
        p_rem = jnp.exp(tot - cum)
        ab = -kk * jnp.exp(cex)
        rb = r * jnp.exp(cum)
        strict = (col > row) if rev else (col < row)
        incl = (col >= row) if rev else (col <= row)
        pre.append(dict(ab=ab, rb=rb, vbd=bd(v), strict=strict, incl=incl,
                        lhs=jnp.concatenate([ab, rb], axis=0),
                        rhs=jnp.concatenate([bd(b * p_inv), bd(kd * p_inv)], axis=0),
                        bk=jnp.concatenate([b * p_rem, kd * p_rem], axis=0)))

    run_side()
    mm = [_bdot_nt(q['lhs'], q['rhs']) for q in pre]
    run_side()
    m_ab = [jnp.where(pre[i]['strict'], mm[i][:CHUNK, :LANES], 0.0) for i in rng]
    m_ak = [jnp.where(pre[i]['strict'], mm[i][:CHUNK, LANES:], 0.0) for i in rng]
    m_r = [jnp.concatenate([jnp.where(pre[i]['incl'], mm[i][CHUNK:, :LANES], 0.0),
                            jnp.where(pre[i]['incl'], mm[i][CHUNK:, LANES:], 0.0)], axis=1) for i in rng]
    mv = [_bdot(m_ak[i], pre[i]['vbd']) for i in rng]
    run_side()

    t = [eye + m_ab[i] for i in rng]
    n = [_bdot(m_ab[i], bd(m_ab[i])) for i in rng]
    for _ in range(4):
        x = [_bdot(jnp.concatenate([t[i], n[i]], axis=0), bd(n[i])) for i in rng]
        t = [t[i] + x[i][:CHUNK] for i in rng]
        n = [x[i][CHUNK:] for i in rng]
        run_side()
    t = [t[i] + _bdot(t[i], bd(n[i])) for i in rng]

    w = [_bdot(t[i], jnp.concatenate([bd(pre[i]['ab']), bd(mv[i])], axis=1)) for i in rng]
    xs = [_bdot_nt(jnp.concatenate([w[i][:, :LANES], pre[i]['rb']], axis=0), units[i][9]) for i in rng]
    u = [xs[i][:CHUNK] + w[i][:, LANES:] for i in rng]
    run_side(drain=True)
    y = [xs[i][CHUNK:] + _bdot(m_r[i], jnp.concatenate([bd(u[i]), pre[i]['vbd']], axis=0)) for i in rng]
    z = [_bdot(jnp.concatenate([u[i], units[i][3]], axis=0).T, pre[i]['bk']) for i in rng]
    s_new = [units[i][9] * jnp.exp(units[i][8]) + jnp.where(r2 == c2, z[i], 0.0) for i in rng]
    return y, s_new


def _rwkv_body(seq_len, n_sub, layer, has_s0, aliased, u_ref, *refs):
    s0_ref = refs[0] if has_s0 else None
    refs = refs[1:] if has_s0 else refs
    wdec_ref, wicl_ref, wg_ref, w0a0_ref, vec_ref, bo_ref, tril_ref, triu_ref = refs[:8]
    (y_ref, sfin_ref, r_s, v_s, kk_s, kd_s, b_s, ci_s, ce_s, y_s, st_s) = refs[9 if aliased else 8:]
    n_chunk = seq_len // CHUNK
    assert n_chunk % 2 == 0 and n_chunk >= 4
    k_k = vec_ref[0:1, :]
    k_a = vec_ref[1:2, :]
    r_k = vec_ref[2:3, :]
    lnx_g = vec_ref[3:4, :]
    lnx_b = vec_ref[4:5, :]
    bo = bo_ref[...]

    def headsum(xb):
        return jnp.concatenate([jnp.dot(xb[:, p * LANES:(p + 1) * LANES], bo, preferred_element_type=F32)
                                for p in range(N_PAIR)], axis=1)

    def prep(rows):
        xs = u_ref[rows, :].astype(F32)
        r = xs[:, 0:D_RWKV]
        k = xs[:, D_RWKV:2 * D_RWKV]
        v = xs[:, 2 * D_RWKV:3 * D_RWKV]
        lora = xs[:, 3 * D_RWKV:3 * D_RWKV + LANES]
        t_lora = jnp.tanh(lora).astype(BF16)
        lora = lora.astype(BF16)
        kk = k * k_k
        kk2 = (kk * kk).astype(BF16)
        yield
        dec = _bdot(t_lora, wdec_ref[...])
        icl = _bdot(lora, wicl_ref[...])
        ss = headsum(kk2)
        yield
        logw = -DECAY_SCALE * jax.nn.sigmoid(w0a0_ref[0:1, :] + dec)
        a = jax.nn.sigmoid(w0a0_ref[1:2, :] + icl)
        kk = kk / jnp.maximum(jnp.sqrt(ss), 1e-12)
        a_f = a[:, :D_RWKV]
        a_b = a[:, D_RWKV:]
        kd_f = k * (1.0 + (a_f - 1.0) * k_a)
        kd_b = k * (1.0 + (a_b - 1.0) * k_a)
        lws = [_split2(logw[:, d * D_RWKV:(d + 1) * D_RWKV]) for d in range(2)]
        r_s[rows, :] = r
        v_s[rows, :] = v
        kk_s[rows, :] = kk
        kd_s[0, rows, :] = kd_f
        kd_s[1, rows, :] = kd_b
        b_s[0, rows, :] = kk * a_f
        b_s[1, rows, :] = kk * a_b
        yield
        dd = functools.partial(jnp.dot, preferred_element_type=F32)
        cums = [dd(tri_ref[...], lws[d][0]) + dd(tri_ref[...], lws[d][1])
                for d, tri_ref in enumerate((tril_ref, triu_ref))]
        yield
        for d in range(2):
            ci_s[d, rows, :] = cums[d]
            ce_s[d, rows, :] = cums[d] - logw[:, d * D_RWKV:(d + 1) * D_RWKV]

    def post(rows):
        y = y_s[rows, :]
        yb = y.astype(BF16)
        v = v_s[rows, :]
        bon = _split2(r_s[rows, :] * (kd_s[0, rows, :] + kd_s[1, rows, :]) * r_k)
        gd = u_ref[rows, 3 * D_RWKV + LANES:3 * D_RWKV + 2 * LANES].astype(F32)
        s_gd = jax.nn.sigmoid(gd).astype(BF16)
        yield
        mean = headsum(yb) * (1.0 / HEAD_DIM)
        bonus = (headsum(bon[0]) + headsum(bon[1])) * v
        g = _bdot(s_gd, wg_ref[...])
        yield
        yc = y - mean
        yc2 = (yc * yc).astype(BF16)
        yield
        var = headsum(yc2) * (1.0 / HEAD_DIM)
        yield
        yn = yc * lax.rsqrt(var + GN_EPS) * lnx_g + lnx_b
        y_ref[rows, :] = ((yn + bonus) * g).astype(BF16)

    def run_all(gens):
        gens = list(gens)
        while gens:
            gens = [gen for gen in gens if next(gen, gens) is not gens]

    def chunk_rows(i):
        return [pl.ds(pl.multiple_of(s * seq_len + (i if d == 0 else n_chunk - 1 - i) * CHUNK, CHUNK), CHUNK)
                for s in range(n_sub) for d in range(2)]

    st_s[...] = s0_ref[...] if has_s0 else jnp.zeros_like(st_s)
    y_s[...] = jnp.zeros_like(y_s)
    m0, m1 = _lane_masks()

    def scan_step(i, prep_next, post_prev):
        rows_sd = chunk_rows(i)
        side = []
        if prep_next:
            side = [prep(rows) for rows in chunk_rows(i + 1)]
        if post_prev:
            side = [post(rows) for rows in chunk_rows(i - 1)]
        units = []
        for s in range(n_sub):
            for d in range(2):
                rows = rows_sd[s * 2 + d]
                cum = ci_s[d, rows, :]
                cex = ce_s[d, rows, :]
                tot = cum[CHUNK - 1:CHUNK] if d == 0 else cum[0:1]
                kk = kk_s[rows, :]
                r = r_s[rows, :]
                v = v_s[rows, :]
                kd = kd_s[d, rows, :]
                b = b_s[d, rows, :]
                for p in range(N_PAIR):
                    sl = slice(p * LANES, (p + 1) * LANES)
                    units.append((d == 1, kk[:, sl], r[:, sl], v[:, sl], kd[:, sl], b[:, sl],
                                  cum[:, sl], cex[:, sl], tot[:, sl], st_s[s, d, p]))
        ys, s_new = _rwkv_units(units, m0, m1, side)
        for s in range(n_sub):
            for d in range(2):
                base = (s * 2 + d) * N_PAIR
                for p in range(N_PAIR):
                    st_s[s, d, p] = s_new[base + p]
                rows = rows_sd[s * 2 + d]
                y_s[rows, :] = y_s[rows, :] + jnp.concatenate(ys[base:base + N_PAIR], axis=1)

    def loop(lo, hi, **kw):
        lax.fori_loop(lo, hi, lambda i, c: (scan_step(i, **kw), c)[1], 0)

    half = n_chunk // 2
    blk = tril_ref.shape[0]
    if blk == CHUNK:
        run_all(prep(rows) for rows in chunk_rows(0))
        loop(0, half - 1, prep_next=True, post_prev=False)
        loop(half - 1, half + 1, prep_next=False, post_prev=False)
        loop(half + 1, n_chunk, prep_next=False, post_prev=True)
        run_all(post(rows) for rows in chunk_rows(n_chunk - 1))
    else:
        blocks = [slice(r0, r0 + blk) for r0 in range(0, n_sub * seq_len, blk)]
        run_all(prep(rows) for rows in blocks)
        loop(0, n_chunk, prep_next=False, post_prev=False)
        run_all(post(rows) for rows in blocks)

    for s in range(n_sub):
        for d in range(2):
            for p in range(N_PAIR):
                st = st_s[s, d, p]
                for h in range(2):
                    rows = slice(h * HEAD_DIM, (h + 1) * HEAD_DIM)
                    val = st[rows, rows]
                    if aliased:
                        sfin_ref[s, d, 2 * p + h] = val
                    else:
                        for other in range(DEPTH):
                            sfin_ref[s, other, d, 2 * p + h] = val if other == layer else jnp.zeros_like(val)


def _rwkv(u_r, s0, seq_len, n_sub, wts, states, layer):
    n_seq = u_r.shape[0] // seq_len
    rows = n_sub * seq_len
    const2 = lambda b: (0, 0)
    st_shape = (n_sub, 2, N_PAIR, LANES, LANES)
    tok = pltpu.VMEM((rows, D_RWKV), F32)
    tok2 = pltpu.VMEM((2, rows, D_RWKV), F32)
    side_work = 2 * N_PAIR * n_sub <= SIDE_WORK_MAX_UNITS
    wts = list(wts) + _chunk_triangles(CHUNK if side_work else RWKV_BLK)
    single = n_seq == n_sub
    in_specs = [pl.BlockSpec((rows, N_RWKV_PAD), lambda b: (b, 0),
                             pipeline_mode=pl.Buffered(1) if single else None)]
    args = [u_r]
    if s0 is not None:
        in_specs.append(pl.BlockSpec(st_shape, lambda b: (b, 0, 0, 0, 0)))
        args.append(_block_diag_state(s0))
    in_specs += [pl.BlockSpec(w.shape, const2) for w in wts]
    args += list(wts)
    if states is not None:
        in_specs.append(pl.BlockSpec(memory_space=pl.ANY))
        args.append(states)
    return pl.pallas_call(
        functools.partial(_rwkv_body, seq_len, n_sub, layer, s0 is not None, states is not None),
        grid=(n_seq // n_sub,),
        in_specs=in_specs,
        out_specs=[pl.BlockSpec((rows, D_RWKV), lambda b: (b, 0),
                                pipeline_mode=pl.Buffered(1) if single else None),
                   pl.BlockSpec((n_sub, DEPTH, 2, H_RWKV, HEAD_DIM, HEAD_DIM), lambda b: (b, 0, 0, 0, 0, 0))
                   if states is None else
                   pl.BlockSpec((n_sub, None, 2, H_RWKV, HEAD_DIM, HEAD_DIM),
                                lambda b: (b, layer, 0, 0, 0, 0))],
        out_shape=[jax.ShapeDtypeStruct((n_seq * seq_len, D_RWKV), BF16),
                   jax.ShapeDtypeStruct((n_seq, DEPTH, 2, H_RWKV, HEAD_DIM, HEAD_DIM), F32)],
        scratch_shapes=[tok] * 3 + [tok2] * 4 + [tok, pltpu.VMEM(st_shape, F32)],
        input_output_aliases={} if states is None else {len(args) - 1: 1},
        compiler_params=_cp(1),
    )(*args)


def _rwkv_weights(p, l):
    z = functools.partial(jnp.zeros, dtype=F32)
    wdec = z((LANES, 2 * D_RWKV))
    wdec = wdec.at[0:LORA_W, :D_RWKV].set(p['decay_up'][l, 0])
    wdec = wdec.at[LORA_W:2 * LORA_W, D_RWKV:].set(p['decay_up'][l, 1])
    wicl = z((LANES, 2 * D_RWKV))
    wicl = wicl.at[2 * LORA_W:2 * LORA_W + LORA_A, :D_RWKV].set(p['iclr_up'][l, 0])
    wicl = wicl.at[2 * LORA_W + LORA_A:2 * LORA_W + 2 * LORA_A, D_RWKV:].set(p['iclr_up'][l, 1])
    wg = z((LANES, D_RWKV)).at[0:LORA_G].set(p['gate_up'][l])
    w0a0 = jnp.stack([p['decay_w0'][l].reshape(-1), p['iclr_a0'][l].reshape(-1)])
    vec = jnp.stack([p['k_k'][l], p['k_a'][l], p['r_k'][l].reshape(-1), p['lnx_g'][l], p['lnx_b'][l],
                     z((D_RWKV,)), z((D_RWKV,)), z((D_RWKV,))])
    head = np.arange(LANES) // HEAD_DIM
    bo = jnp.asarray(head[:, None] == head[None, :], BF16)
    return [wdec.astype(BF16), wicl.astype(BF16), wg.astype(BF16), w0a0, vec, bo]


def _chunk_triangles(rows):
    idx = np.arange(rows)
    same = idx[None, :] // CHUNK == idx[:, None] // CHUNK
    return [jnp.asarray(same & (idx[None, :] <= idx[:, None]), BF16),
            jnp.asarray(same & (idx[None, :] >= idx[:, None]), BF16)]


def _rope(x, cos, sin):
    lane = lax.broadcasted_iota(jnp.int32, (1, LANES), 1)
    first_half = (lane & ROPE_PAIRS) == 0
    partner = jnp.where(first_half, pltpu.roll(x, LANES - ROPE_PAIRS, axis=1),
                        pltpu.roll(x, ROPE_PAIRS, axis=1))
    return x * cos + partner * sin


def _attn_body(has_ctx, lam_init, *refs):
    if has_ctx:
        (q_ref, k_ref, v_ref, lp_ref, sg_ref, kc_ref, vc_ref, cq_ref, sq_ref, ck_ref, sk_ref,
         o_ref) = refs
    else:
        q_ref, k_ref, v_ref, lp_ref, sg_ref, o_ref = refs
    n_pair = D_DIFF // LANES
    lp = lp_ref[...]
    lam = (jnp.exp(jnp.sum(lp[0:1] * lp[1:2], axis=-1, keepdims=True))
           - jnp.exp(jnp.sum(lp[2:3] * lp[3:4], axis=-1, keepdims=True)) + lam_init)
    lane = lax.broadcasted_iota(jnp.int32, (1, LANES), 1)

    tq = q_ref.shape[0] // k_ref.shape[0]

    def operands(seq, p):
        sl = slice(p * LANES, (p + 1) * LANES)
        q = q_ref[seq * tq:(seq + 1) * tq, sl]
        k = k_ref[seq, :, sl]
        v = v_ref[seq, :, sl]
        if has_ctx:
            q = _rope(q.astype(F32), cq_ref[...], sq_ref[...])
            k = _rope(k, ck_ref[...], sk_ref[...])
            k = jnp.concatenate([kc_ref[0, :, sl], k], axis=0)
            v = jnp.concatenate([vc_ref[0, :, sl], v], axis=0)
        vt = v.T
        ones = jnp.ones((ONES_ROWS, v.shape[0]), F32)
        vts = [jnp.concatenate([vt[h * HEAD_DIM:(h + 1) * HEAD_DIM], ones], axis=0).astype(BF16)
               for h in range(2)]
        return q.astype(BF16), k.astype(BF16), vts

    n_keys = k_ref.shape[1] + (kc_ref.shape[1] if has_ctx else 0)
    blocks = range(0, n_keys, KEY_BLOCK)
    ops, tiles = {}, []
    for seq in range(k_ref.shape[0]):
        for p in range(n_pair):
            for j in blocks:
                tiles += [(seq, p, h, m, j) for h in range(2) for m in range(2)]

    def score(tile):
        seq, p, h, m, j = tile
        if (seq, p) not in ops:
            ops[seq, p] = operands(seq, p)
        q, k, _ = ops[seq, p]
        lo = h * HEAD_DIM + m * D_QK
        sel = (lane >= lo) & (lane < lo + D_QK)
        return _bdot_nt(k[j:j + KEY_BLOCK], jnp.where(sel, q, jnp.zeros_like(q)))

    run_max, acc = {}, {}

    def consume(tile, s):
        seq, p, h, m, j = tile
        vt = ops[seq, p][2][h][:, j:j + KEY_BLOCK]
        c = (seq, p, h, m)
        mj = jnp.max(s, axis=0, keepdims=True)
        if j == 0:
            run_max[c] = mj
            acc[c] = jnp.dot(vt, jnp.exp2(s - mj).astype(BF16), preferred_element_type=F32)
        else:
            m_new = jnp.maximum(run_max[c], mj)
            acc[c] = (acc[c] * jnp.exp2(run_max[c] - m_new)
                      + jnp.dot(vt, jnp.exp2(s - m_new).astype(BF16), preferred_element_type=F32))
            run_max[c] = m_new
        if j == blocks[-1] and (h, m) == (1, 1):
            finish(seq, p)

    def finish(seq, p):
        halves = []
        for h in range(2):
            a0, a1 = acc.pop((seq, p, h, 0)), acc.pop((seq, p, h, 1))
            o = (a0[:HEAD_DIM] * (1.0 / a0[HEAD_DIM:HEAD_DIM + 1])
                 - lam * (a1[:HEAD_DIM] * (1.0 / a1[HEAD_DIM:HEAD_DIM + 1])))
            ms = jnp.mean(o * o, axis=0, keepdims=True)
            halves.append(o * lax.rsqrt(ms + SUBLN_EPS))
        o_ref[seq * tq:(seq + 1) * tq, p * LANES:(p + 1) * LANES] = (
            jnp.concatenate(halves, axis=0).T * sg_ref[...] * (1.0 - lam_init)).astype(BF16)

    pending = []
    for tile in tiles:
        pending.append((tile, score(tile)))
        if len(pending) > ATTN_LOOKAHEAD:
            consume(*pending.pop(0))
    for item in pending:
        consume(*item)


def _attention(q, k_all, v_all, layer, n_sub, lam_init, lp, sg, ctx=None):
    n_tok = q.shape[0]
    n_seq, _, seq_len, _ = k_all.shape
    tq = ATTN_TQ
    nq = seq_len // tq
    assert n_sub == 1 or nq == 1
    kv_spec = pl.BlockSpec((n_sub, None, seq_len, D_DIFF), lambda b, i: (b, layer, 0, 0))
    in_specs = [pl.BlockSpec((n_sub * tq, D_DIFF), lambda b, i: (b * nq + i, 0)),
                kv_spec, kv_spec,
                pl.BlockSpec(lp.shape, lambda b, i: (0, 0)),
                pl.BlockSpec((1, LANES), lambda b, i: (0, 0))]
    args = [q, k_all, v_all, lp, sg]
    if ctx is not None:
        kc, vc, cos, sin = ctx
        past = kc.shape[1]
        in_specs += [pl.BlockSpec((1, past, D_DIFF), lambda b, i: (b, 0, 0)),
                     pl.BlockSpec((1, past, D_DIFF), lambda b, i: (b, 0, 0)),
                     pl.BlockSpec((tq, LANES), lambda b, i: (i, 0)),
                     pl.BlockSpec((tq, LANES), lambda b, i: (i, 0)),
                     pl.BlockSpec((seq_len, LANES), lambda b, i: (0, 0)),
                     pl.BlockSpec((seq_len, LANES), lambda b, i: (0, 0))]
        args += [kc, vc, cos, sin, cos, sin]
    return pl.pallas_call(
        functools.partial(_attn_body, ctx is not None, lam_init),
        grid=(n_seq // n_sub, nq),
        in_specs=in_specs,
        out_specs=pl.BlockSpec((n_sub * tq, D_DIFF), lambda b, i: (b * nq + i, 0)),
        out_shape=jax.ShapeDtypeStruct((n_tok, D_DIFF), BF16),
        compiler_params=_cp(2),
    )(*args)


def _rope_tables(seq_len):
    t = jnp.arange(seq_len)
    pos = jnp.stack([(t // GRID_W).astype(F32), (t % GRID_W).astype(F32)], axis=1)
    inv = 1.0 / (ROPE_BASE ** (jnp.arange(ROPE_PAIRS, dtype=F32) / ROPE_PAIRS))
    ang = pos[:, :, None] * inv
    d = np.arange(LANES) % D_QK
    axis = d // (2 * ROPE_PAIRS)
    second = (d % (2 * ROPE_PAIRS)) // ROPE_PAIRS
    idx = d % ROPE_PAIRS
    cos = jnp.cos(ang)[:, axis, idx]
    sin = jnp.sin(ang)[:, axis, idx] * jnp.asarray(np.where(second == 1, 1.0, -1.0), F32)
    return cos, sin


def _fnet_body(seq_len, x_ref, ct_ref, st_ref, cc_ref, sc_ref, o_ref):
    n_sub = x_ref.shape[0] // seq_len
    x = x_ref[...]
    xc = jnp.dot(x, cc_ref[...], preferred_element_type=F32)
    xs = jnp.dot(x, sc_ref[...], preferred_element_type=F32)
    wide = lambda a: jnp.concatenate([a[s * seq_len:(s + 1) * seq_len] for s in range(n_sub)], axis=1)
    y = _bdot(ct_ref[...], wide(xc)) - _bdot(st_ref[...], wide(xs))
    for s in range(n_sub):
        o_ref[s * seq_len:(s + 1) * seq_len, :] = y[:, s * D_FNET:(s + 1) * D_FNET].astype(BF16)


def _dft_consts(n, block=1):
    idx = np.arange(n)
    ang = 2.0 * np.pi * ((idx[:, None] * idx[None, :]) % n) / n
    return [jnp.asarray(np.kron(np.eye(block), m).astype(np.float32)).astype(BF16)
            for m in (np.cos(ang) / np.sqrt(n), np.sin(ang) / np.sqrt(n))]


def _fnet(u_f, seq_len, n_sub):
    n_tok = u_f.shape[0]
    rows = n_sub * seq_len
    consts = _dft_consts(seq_len) + _dft_consts(FNET_GROUP_DIM, FNET_GROUPS)
    const = lambda b: (0, 0)
    return pl.pallas_call(
        functools.partial(_fnet_body, seq_len),
        grid=(n_tok // rows,),
        in_specs=[pl.BlockSpec((rows, D_FNET), lambda b: (b, 0))]
        + [pl.BlockSpec(c.shape, const) for c in consts],
        out_specs=pl.BlockSpec((rows, D_FNET), lambda b: (b, 0)),
        out_shape=jax.ShapeDtypeStruct((n_tok, D_FNET), BF16),
        compiler_params=_cp(1),
    )(u_f, *consts)


def _ffn_body(final, yr_ref, yd_ref, yf_ref, x_ref, g1_ref, sh2_ref, sc2_ref, g2_ref, n2_ref, fg_ref,
              wo_ref, wi_ref, wf_ref, o_ref):
    part = x_ref.shape[0] // N_PARTS
    parts = [slice(j * part, (j + 1) * part) for j in range(N_PARTS)]
    y = [jnp.dot(jnp.concatenate([yr_ref[r, :], yd_ref[r, :], yf_ref[r, :]], axis=1), wo_ref[...],
                 preferred_element_type=F32) for r in parts]
    x = [x_ref[r, :] + g1_ref[0] * y[j] for j, r in enumerate(parts)]
    h = [(_rms(xj, n2_ref[...]) * (1.0 + sc2_ref[0]) + sh2_ref[0]).astype(BF16) for xj in x]
    z = [jnp.dot(hj, wi_ref[...], preferred_element_type=F32) for hj in h]
    act = [zj[:, :D_FF] * jax.nn.sigmoid(zj[:, :D_FF]) * zj[:, D_FF:] for zj in z]
    f = [_bdot(aj, wf_ref[...]) for aj in act]
    for j, r in enumerate(parts):
        xj = x[j] + g2_ref[0] * f[j]
        o_ref[r, :] = _rms(xj, fg_ref[...]) if final else xj


def _ffn(y_r, y_d, y_f, x, mod, layer, row_fn, n2, fg, wo, wi, wf, final, tm):
    n_tok = x.shape[0]
    row = lambda i: (i, 0)
    const = lambda i: (0, 0)
    return pl.pallas_call(
        functools.partial(_ffn_body, final),
        grid=(n_tok // tm,),
        in_specs=[pl.BlockSpec((tm, D_RWKV), row),
                  pl.BlockSpec((tm, D_DIFF), row),
                  pl.BlockSpec((tm, D_FNET), row),
                  pl.BlockSpec((tm, D_MODEL), row),
                  _mod_spec(layer, 2, row_fn),
                  _mod_spec(layer, 3, row_fn),
                  _mod_spec(layer, 4, row_fn),
                  _mod_spec(layer, 5, row_fn),
                  pl.BlockSpec((1, D_MODEL), const),
                  pl.BlockSpec((1, D_MODEL), const),
                  pl.BlockSpec((None,) + wo.shape[1:], lambda i: (layer, 0, 0), pipeline_mode=pl.Buffered(1)),
                  pl.BlockSpec((None,) + wi.shape[1:], lambda i: (layer, 0, 0), pipeline_mode=pl.Buffered(1)),
                  pl.BlockSpec((None,) + wf.shape[1:], lambda i: (layer, 0, 0), pipeline_mode=pl.Buffered(1))],
        out_specs=pl.BlockSpec((tm, D_MODEL), row),
        out_shape=jax.ShapeDtypeStruct((n_tok, D_MODEL), F32),
        compiler_params=_cp(1),
    )(y_r, y_d, y_f, x, mod, mod, mod, mod, n2.reshape(1, D_MODEL), fg.reshape(1, D_MODEL), wo, wi, wf)


def _block_diag_state(s):
    b = s.shape[0]
    s = s.reshape(b, 2, N_PAIR, 2, HEAD_DIM, HEAD_DIM)
    eye = jnp.eye(2, dtype=s.dtype)
    s = s[:, :, :, :, :, None, :] * eye[None, None, None, :, None, :, None]
    return s.reshape(b, 2, N_PAIR, LANES, LANES)


def kernel(x_prompt, x_sample, c, state_rwkv, cache_diff_k, cache_diff_v, c_ctx, norm1_g, norm2_g, final_norm_g, w_mod, b_mod, w_in, w_out, shift_mu, decay_w0, decay_up, iclr_a0, iclr_up, gate_up, k_k, k_a, r_k, lnx_g, lnx_b, diff_lambda, subln_g, w_ffn_in, w_ffn_out):
    p = dict(shift_mu=shift_mu, decay_w0=decay_w0, decay_up=decay_up, iclr_a0=iclr_a0, iclr_up=iclr_up,
             gate_up=gate_up, k_k=k_k, k_a=k_a, r_k=r_k, lnx_g=lnx_g, lnx_b=lnx_b)
    n_ctx, t_ctx, _ = x_prompt.shape
    n_dec, t_dec, _ = x_sample.shape
    past = cache_diff_k.shape[2]

    cond = jnp.concatenate([c_ctx[None, :], c, jnp.zeros((MOD_ROWS - 1 - n_dec, D_MODEL), F32)], axis=0)
    mod = _modulation(cond, w_mod, b_mod).reshape(DEPTH * MOD_ROWS, 1, 6 * D_MODEL)

    tm_ffn = FFN_TM
    ctx_plan = _stream_plan(n_ctx, t_ctx)
    dec_plan = _stream_plan(n_dec, t_dec)
    streams = [
        dict(x=x_prompt.reshape(n_ctx * t_ctx, D_MODEL), t=t_ctx, n=n_ctx, **ctx_plan,
             row_in=lambda i: 0, row_ffn=lambda i: 0),
        dict(x=x_sample.reshape(n_dec * t_dec, D_MODEL), t=t_dec, n=n_dec, **dec_plan,
             row_in=lambda i: 1 + i // (t_dec // dec_plan['tm_in']), row_ffn=lambda i: 1 + i // (t_dec // tm_ffn)),
    ]
    cos, sin = _rope_tables(t_dec)
    for st in streams:
        st['kv'] = st['states'] = None
    w_t = jnp.swapaxes(w_in, 1, 2)
    w_in_l = jnp.concatenate(
        [w_t[:, :N_RWKV_IN], jnp.zeros((DEPTH, N_RWKV_PAD - N_RWKV_IN, D_MODEL), F32), w_t[:, N_RWKV_IN:]],
        axis=1).astype(BF16)
    wo = w_out.astype(BF16)
    wi = w_ffn_in.astype(BF16)
    wf = w_ffn_out.astype(BF16)
    for l in range(DEPTH):
        rw = _rwkv_weights(p, l)
        mu = jnp.concatenate([shift_mu[l], jnp.zeros((2, N_RWKV_PAD - N_RWKV_IN), F32)], axis=1)
        lam_init = 0.8 - 0.6 * math.exp(-0.3 * l)
        sg = jnp.tile(subln_g[l], 2).reshape(1, LANES)
        for si, st in enumerate(streams):
            u_r, q, k_all, v_all, u_f = _inproj(st['x'], mod, l, st['row_in'], norm1_g[l], w_in_l, mu,
                                                 st['kv'], st['t'], st['tm_in'])
            st['kv'] = (k_all, v_all)
            if si == 0:
                s0 = None
                attn_ctx = None
            else:
                s0 = state_rwkv[:, l].astype(F32)
                attn_ctx = (cache_diff_k[:, l].reshape(n_dec, past, D_DIFF).astype(F32),
                            cache_diff_v[:, l].reshape(n_dec, past, D_DIFF).astype(F32), cos, sin)
            y_r, st['states'] = _rwkv(u_r, s0, st['t'], st['n_sub'], rw, st['states'], l)
            y_d = _attention(q, k_all, v_all, l, st['n_attn'], lam_init, diff_lambda[l], sg, attn_ctx)
            y_f = _fnet(u_f, st['t'], st['n_fnet'])
            st['x'] = _ffn(y_r, y_d, y_f, st['x'], mod, l, st['row_ffn'], norm2_g[l], final_norm_g,
                           wo, wi, wf, l == DEPTH - 1, tm_ffn)
    y_prompt = streams[0]['x'].reshape(n_ctx, t_ctx, D_MODEL)
    y_sample = streams[1]['x'].reshape(n_dec, t_dec, D_MODEL)
    new_k = streams[0]['kv'][0].reshape(n_ctx, DEPTH, t_ctx, H_DIFF, 2, D_QK)
    new_v = streams[0]['kv'][1].reshape(n_ctx, DEPTH, t_ctx, H_DIFF, HEAD_DIM)
    return (y_prompt, y_sample, streams[0]['states'], new_k, new_v)
```

```python
import functools
import math

import numpy as np
import jax
import jax.numpy as jnp
from jax import lax
from jax.experimental import pallas as pl
from jax.experimental.pallas import tpu as pltpu

F32 = jnp.float32
BF16 = jnp.bfloat16

D_MODEL = 1024
DEPTH = 2
GRID_W = 64
HEAD_DIM = 64
D_RWKV = 384
H_RWKV = D_RWKV // HEAD_DIM
D_DIFF = 384
H_DIFF = D_DIFF // HEAD_DIM
D_QK = HEAD_DIM // 2
D_FNET = D_MODEL - D_RWKV - D_DIFF
FNET_GROUPS = 4
FNET_GROUP_DIM = D_FNET // FNET_GROUPS
LORA_W = 32
LORA_A = 32
LORA_G = 64
N_RWKV_IN = 3 * D_RWKV + 2 * LORA_W + 2 * LORA_A + LORA_G
N_DIFF_IN = 3 * D_DIFF
D_FF = ((8 * D_MODEL + 3 * 256 - 1) // (3 * 256)) * 256
ROPE_PAIRS = D_QK // 4
ROPE_BASE = 10000.0
RMS_EPS = 1e-6
GN_EPS = 64e-5
SUBLN_EPS = 1e-5
DECAY_SCALE = math.exp(-0.5)
Q_SCALE = D_QK ** -0.5 * math.log2(math.e)

LANES = 128
N_RWKV_PAD = 11 * LANES
D_IN_PAD = N_RWKV_PAD + N_DIFF_IN + D_FNET
N_PAIR = H_RWKV // 2
CHUNK = 64
N_PARTS = 2
SIDE_WORK_MAX_UNITS = 12
RWKV_BLK = 128
MOD_ROWS = 8
ONES_ROWS = 16
VMEM_LIMIT = 60 * 1024 * 1024

MOD_TN = 1536
INPROJ_TM = 512
FFN_TM = 512
ATTN_TQ = 256
ATTN_STEP_ROWS = 1024
KEY_BLOCK = 512
ATTN_LOOKAHEAD = 8
FNET_STEP_ROWS = 1024
RWKV_STEP_ROWS = 1024
RWKV_ONE_STEP_ROWS = 2048


def _stream_plan(n_seq, seq_len):
    one_step = n_seq * seq_len <= RWKV_ONE_STEP_ROWS
    return dict(n_sub=n_seq if one_step else max(1, RWKV_STEP_ROWS // seq_len),
                n_attn=max(1, ATTN_STEP_ROWS // seq_len),
                n_fnet=max(1, FNET_STEP_ROWS // seq_len),
                tm_in=max(seq_len, INPROJ_TM))


def _cp(n_axes=1):
    return pltpu.CompilerParams(dimension_semantics=("arbitrary",) * n_axes,
                                vmem_limit_bytes=VMEM_LIMIT)


def _bdot(a, b):
    return jnp.dot(a.astype(BF16), b.astype(BF16), preferred_element_type=F32)


def _bdot_nt(a, b):
    return lax.dot_general(a.astype(BF16), b.astype(BF16), (((1,), (1,)), ((), ())),
                           preferred_element_type=F32)


def _split2(x):
    hi = x.astype(BF16)
    lo = (x - hi.astype(F32)).astype(BF16)
    return hi, lo


def _dot_x3(a, b):
    a_hi, a_lo = _split2(a)
    b_hi, b_lo = _split2(b)
    d = functools.partial(jnp.dot, preferred_element_type=F32)
    return d(a_hi, b_hi) + d(a_lo, b_hi) + d(a_hi, b_lo)


def _rms(x, g):
    return x * lax.rsqrt(jnp.mean(x * x, axis=-1, keepdims=True) + RMS_EPS) * g


def _mod_body(c_ref, w_ref, b_ref, o_ref):
    c = c_ref[...]
    a = c * jax.nn.sigmoid(c)
    o_ref[0] = _dot_x3(a, w_ref[0]) + b_ref[0]


def _modulation(cond, w_mod, b_mod):
    n_layers, _, n_out = w_mod.shape
    tn = MOD_TN
    return pl.pallas_call(
        _mod_body,
        grid=(n_layers, n_out // tn),
        in_specs=[pl.BlockSpec((MOD_ROWS, D_MODEL), lambda l, j: (0, 0)),
                  pl.BlockSpec((1, D_MODEL, tn), lambda l, j: (l, 0, j)),
                  pl.BlockSpec((1, 1, tn), lambda l, j: (l, 0, j))],
        out_specs=pl.BlockSpec((1, MOD_ROWS, tn), lambda l, j: (l, 0, j)),
        out_shape=jax.ShapeDtypeStruct((n_layers, MOD_ROWS, n_out), F32),
        compiler_params=_cp(2),
    )(cond, w_mod, b_mod.reshape(n_layers, 1, n_out))


def _mod_spec(layer, col, row_fn):
    return pl.BlockSpec((1, 1, D_MODEL), lambda i: (layer * MOD_ROWS + row_fn(i), 0, col))


def _put_layer(ref, idx, layer, aliased, val):
    if aliased:
        ref[idx] = val
    else:
        for other in range(DEPTH):
            ref[idx + (other,)] = val if other == layer else jnp.zeros_like(val)


def _inproj_body(seq_len, layer, aliased, x_ref, g_ref, sh_ref, sc_ref, w_ref, mu_ref, *refs):
    ur_ref, q_ref, k_ref, v_ref, uf_ref = refs[2:] if aliased else refs
    part = x_ref.shape[0] // N_PARTS
    parts = [slice(j * part, (j + 1) * part) for j in range(N_PARTS)]
    h = [(_rms(x_ref[r, :], g_ref[...]) * (1.0 + sc_ref[0]) + sh_ref[0]).astype(BF16) for r in parts]
    u = jnp.concatenate([_bdot_nt(hj, w_ref[...]) for hj in h], axis=0)
    ur = u[:, :N_RWKV_PAD]
    tm = ur.shape[0]
    pos = lax.broadcasted_iota(jnp.int32, (tm, 1), 0) & (seq_len - 1)
    prev = jnp.where(pos == 0, 0.0, pltpu.roll(ur, 1, axis=0))
    nxt = jnp.where(pos == seq_len - 1, 0.0, pltpu.roll(ur, tm - 1, axis=0))
    ur_ref[...] = (ur + mu_ref[0:1, :] * (prev - ur) + mu_ref[1:2, :] * (nxt - ur)).astype(BF16)
    q_ref[...] = (u[:, N_RWKV_PAD:N_RWKV_PAD + D_DIFF] * Q_SCALE).astype(BF16)
    for s in range(tm // seq_len):
        rows = slice(s * seq_len, (s + 1) * seq_len)
        _put_layer(k_ref, (s,), layer, aliased, u[rows, N_RWKV_PAD + D_DIFF:N_RWKV_PAD + 2 * D_DIFF])
        _put_layer(v_ref, (s,), layer, aliased, u[rows, N_RWKV_PAD + 2 * D_DIFF:N_RWKV_PAD + 3 * D_DIFF])
    uf_ref[...] = u[:, N_RWKV_PAD + N_DIFF_IN:].astype(BF16)


def _inproj(x, mod, layer, row_fn, g, w, mu, kv, seq_len, tm):
    n_tok = x.shape[0]
    kv_shape = jax.ShapeDtypeStruct((n_tok // seq_len, DEPTH, seq_len, D_DIFF), F32)
    assert tm % seq_len == 0 and seq_len & (seq_len - 1) == 0
    row = lambda i: (i, 0)
    const = lambda i: (0, 0)
    if kv is None:
        kv_spec = pl.BlockSpec((tm // seq_len, DEPTH, seq_len, D_DIFF), lambda i: (i, 0, 0, 0))
    else:
        kv_spec = pl.BlockSpec((tm // seq_len, None, seq_len, D_DIFF), lambda i: (i, layer, 0, 0))
    return pl.pallas_call(
        functools.partial(_inproj_body, seq_len, layer, kv is not None),
        grid=(n_tok // tm,),
        in_specs=[pl.BlockSpec((tm, D_MODEL), row),
                  pl.BlockSpec((1, D_MODEL), const),
                  _mod_spec(layer, 0, row_fn),
                  _mod_spec(layer, 1, row_fn),
                  pl.BlockSpec((None, D_IN_PAD, D_MODEL), lambda i: (layer, 0, 0)),
                  pl.BlockSpec((2, N_RWKV_PAD), const)]
        + [pl.BlockSpec(memory_space=pl.ANY)] * (0 if kv is None else 2),
        out_specs=[pl.BlockSpec((tm, N_RWKV_PAD), row),
                   pl.BlockSpec((tm, D_DIFF), row),
                   kv_spec, kv_spec,
                   pl.BlockSpec((tm, D_FNET), row)],
        out_shape=[jax.ShapeDtypeStruct((n_tok, N_RWKV_PAD), BF16),
                   jax.ShapeDtypeStruct((n_tok, D_DIFF), BF16),
                   kv_shape, kv_shape,
                   jax.ShapeDtypeStruct((n_tok, D_FNET), BF16)],
        input_output_aliases={} if kv is None else {6: 2, 7: 3},
        compiler_params=_cp(1),
    )(x, g.reshape(1, D_MODEL), mod, mod, w, mu, *(kv or ()))


def _lane_masks():
    lane = lax.broadcasted_iota(jnp.int32, (1, LANES), 1)
    return lane < HEAD_DIM, lane >= HEAD_DIM


def _rwkv_units(units, m0, m1, side=()):
    side = list(side)

    def run_side(drain=False):
        for gen in list(side):
            for _ in gen:
                if not drain:
                    break
            else:
                side.remove(gen)

    def bd(x):
        xb = x.astype(BF16)
        zero = jnp.zeros_like(xb)
        return jnp.concatenate([jnp.where(m0, xb, zero), jnp.where(m1, xb, zero)], axis=0)

    row = lax.broadcasted_iota(jnp.int32, (CHUNK, LANES), 0)
    col = lax.broadcasted_iota(jnp.int32, (CHUNK, LANES), 1) & (CHUNK - 1)
    eye = (col == row).astype(F32)
    r2 = lax.broadcasted_iota(jnp.int32, (LANES, LANES), 0) < HEAD_DIM
    c2 = lax.broadcasted_iota(jnp.int32, (LANES, LANES), 1) < HEAD_DIM
    rng = range(len(units))

    pre = []
    for rev, kk, r, v, kd, b, cum, cex, tot, s_prev in units:
        p_inv = jnp.exp(-cum)
        p_rem = jnp.exp(tot - cum)
        ab = -kk * jnp.exp(cex)
        rb = r * jnp.exp(cum)
        strict = (col > row) if rev else (col < row)
        incl = (col >= row) if rev else (col <= row)
        pre.append(dict(ab=ab, rb=rb, vbd=bd(v), strict=strict, incl=incl,
                        lhs=jnp.concatenate([ab, rb], axis=0),
                        rhs=jnp.concatenate([bd(b * p_inv), bd(kd * p_inv)], axis=0),
                        bk=jnp.concatenate([b * p_rem, kd * p_rem], axis=0)))

    run_side()
    mm = [_bdot_nt(q['lhs'], q['rhs']) for q in pre]
    run_side()
    m_ab = [jnp.where(pre[i]['strict'], mm[i][:CHUNK, :LANES], 0.0) for i in rng]
    m_ak = [jnp.where(pre[i]['strict'], mm[i][:CHUNK, LANES:], 0.0) for i in rng]
    m_r = [jnp.concatenate([jnp.where(pre[i]['incl'], mm[i][CHUNK:, :LANES], 0.0),
                            jnp.where(pre[i]['incl'], mm[i][CHUNK:, LANES:], 0.0)], axis=1) for i in rng]
    mv = [_bdot(m_ak[i], pre[i]['vbd']) for i in rng]
    run_side()

    t = [eye + m_ab[i] for i in rng]
    n = [_bdot(m_ab[i], bd(m_ab[i])) for i in rng]
    for _ in range(4):
        x = [_bdot(jnp.concatenate([t[i], n[i]], axis=0), bd(n[i])) for i in rng]
        t = [t[i] + x[i][:CHUNK] for i in rng]
        n = [x[i][CHUNK:] for i in rng]
        run_side()
    t = [t[i] + _bdot(t[i], bd(n[i])) for i in rng]

    w = [_bdot(t[i], jnp.concatenate([bd(pre[i]['ab']), bd(mv[i])], axis=1)) for i in rng]
    xs = [_bdot_nt(jnp.concatenate([w[i][:, :LANES], pre[i]['rb']], axis=0), units[i][9]) for i in rng]
    u = [xs[i][:CHUNK] + w[i][:, LANES:] for i in rng]
    run_side(drain=True)
    y = [xs[i][CHUNK:] + _bdot(m_r[i], jnp.concatenate([bd(u[i]), pre[i]['vbd']], axis=0)) for i in rng]
    z = [_bdot(jnp.concatenate([u[i], units[i][3]], axis=0).T, pre[i]['bk']) for i in rng]
    s_new = [units[i][9] * jnp.exp(units[i][8]) + jnp.where(r2 == c2, z[i], 0.0) for i in rng]
    return y, s_new


def _rwkv_body(seq_len, n_sub, layer, has_s0, aliased, u_ref, *refs):
    s0_ref = refs[0] if has_s0 else None
    refs = refs[1:] if has_s0 else refs
    wdec_ref, wicl_ref, wg_ref, w0a0_ref, vec_ref, bo_ref, tril_ref, triu_ref = refs[:8]
    (y_ref, sfin_ref, r_s, v_s, kk_s, kd_s, b_s, ci_s, ce_s, y_s, st_s) = refs[9 if aliased else 8:]
    n_chunk = seq_len // CHUNK
    assert n_chunk % 2 == 0 and n_chunk >= 4
    k_k = vec_ref[0:1, :]
    k_a = vec_ref[1:2, :]
    r_k = vec_ref[2:3, :]
    lnx_g = vec_ref[3:4, :]
    lnx_b = vec_ref[4:5, :]
    bo = bo_ref[...]

    def headsum(xb):
        return jnp.concatenate([jnp.dot(xb[:, p * LANES:(p + 1) * LANES], bo, preferred_element_type=F32)
                                for p in range(N_PAIR)], axis=1)

    def prep(rows):
        xs = u_ref[rows, :].astype(F32)
        r = xs[:, 0:D_RWKV]
        k = xs[:, D_RWKV:2 * D_RWKV]
        v = xs[:, 2 * D_RWKV:3 * D_RWKV]
        lora = xs[:, 3 * D_RWKV:3 * D_RWKV + LANES]
        t_lora = jnp.tanh(lora).astype(BF16)
        lora = lora.astype(BF16)
        kk = k * k_k
        kk2 = (kk * kk).astype(BF16)
        yield
        dec = _bdot(t_lora, wdec_ref[...])
        icl = _bdot(lora, wicl_ref[...])
        ss = headsum(kk2)
        yield
        logw = -DECAY_SCALE * jax.nn.sigmoid(w0a0_ref[0:1, :] + dec)
        a = jax.nn.sigmoid(w0a0_ref[1:2, :] + icl)
        kk = kk / jnp.maximum(jnp.sqrt(ss), 1e-12)
        a_f = a[:, :D_RWKV]
        a_b = a[:, D_RWKV:]
        kka = k * k_a
        kd_f = (k - kka) + kka * a_f
        kd_b = (k - kka) + kka * a_b
        lws = [_split2(logw[:, d * D_RWKV:(d + 1) * D_RWKV]) for d in range(2)]
        r_s[rows, :] = r
        v_s[rows, :] = v
        kk_s[rows, :] = kk
        kd_s[0, rows, :] = kd_f
        kd_s[1, rows, :] = kd_b
        b_s[0, rows, :] = kk * a_f
        b_s[1, rows, :] = kk * a_b
        yield
        dd = functools.partial(jnp.dot, preferred_element_type=F32)
        cums = [dd(tri_ref[...], lws[d][0]) + dd(tri_ref[...], lws[d][1])
                for d, tri_ref in enumerate((tril_ref, triu_ref))]
        yield
        for d in range(2):
            ci_s[d, rows, :] = cums[d]
            ce_s[d, rows, :] = cums[d] - logw[:, d * D_RWKV:(d + 1) * D_RWKV]

    def post(rows):
        y = y_s[rows, :]
        yb = y.astype(BF16)
        v = v_s[rows, :]
        bon = _split2(r_s[rows, :] * (kd_s[0, rows, :] + kd_s[1, rows, :]) * r_k)
        gd = u_ref[rows, 3 * D_RWKV + LANES:3 * D_RWKV + 2 * LANES].astype(F32)
        s_gd = jax.nn.sigmoid(gd).astype(BF16)
        yield
        mean = headsum(yb) * (1.0 / HEAD_DIM)
        bonus = (headsum(bon[0]) + headsum(bon[1])) * v
        g = _bdot(s_gd, wg_ref[...])
        yield
        yc = y - mean
        yc2 = (yc * yc).astype(BF16)
        yield
        var = headsum(yc2) * (1.0 / HEAD_DIM)
        yield
        yn = yc * lax.rsqrt(var + GN_EPS) * lnx_g + lnx_b
        y_ref[rows, :] = ((yn + bonus) * g).astype(BF16)

    def run_all(gens):
        gens = list(gens)
        while gens:
            gens = [gen for gen in gens if next(gen, gens) is not gens]

    def chunk_rows(i):
        return [pl.ds(pl.multiple_of(s * seq_len + (i if d == 0 else n_chunk - 1 - i) * CHUNK, CHUNK), CHUNK)
                for s in range(n_sub) for d in range(2)]

    st_s[...] = s0_ref[...] if has_s0 else jnp.zeros_like(st_s)
    y_s[...] = jnp.zeros_like(y_s)
    m0, m1 = _lane_masks()

    def scan_step(i, prep_next, post_prev):
        rows_sd = chunk_rows(i)
        side = []
        if prep_next:
            side = [prep(rows) for rows in chunk_rows(i + 1)]
        if post_prev:
            side = [post(rows) for rows in chunk_rows(i - 1)]
        units = []
        for s in range(n_sub):
            for d in range(2):
                rows = rows_sd[s * 2 + d]
                cum = ci_s[d, rows, :]
                cex = ce_s[d, rows, :]
                tot = cum[CHUNK - 1:CHUNK] if d == 0 else cum[0:1]
                kk = kk_s[rows, :]
                r = r_s[rows, :]
                v = v_s[rows, :]
                kd = kd_s[d, rows, :]
                b = b_s[d, rows, :]
                for p in range(N_PAIR):
                    sl = slice(p * LANES, (p + 1) * LANES)
                    units.append((d == 1, kk[:, sl], r[:, sl], v[:, sl], kd[:, sl], b[:, sl],
                                  cum[:, sl], cex[:, sl], tot[:, sl], st_s[s, d, p]))
        ys, s_new = _rwkv_units(units, m0, m1, side)
        for s in range(n_sub):
            for d in range(2):
                base = (s * 2 + d) * N_PAIR
                for p in range(N_PAIR):
                    st_s[s, d, p] = s_new[base + p]
                rows = rows_sd[s * 2 + d]
                y_s[rows, :] = y_s[rows, :] + jnp.concatenate(ys[base:base + N_PAIR], axis=1)

    def loop(lo, hi, **kw):
        lax.fori_loop(lo, hi, lambda i, c: (scan_step(i, **kw), c)[1], 0)

    half = n_chunk // 2
    blk = tril_ref.shape[0]
    if blk == CHUNK:
        run_all(prep(rows) for rows in chunk_rows(0))
        loop(0, half - 1, prep_next=True, post_prev=False)
        loop(half - 1, half + 1, prep_next=False, post_prev=False)
        loop(half + 1, n_chunk, prep_next=False, post_prev=True)
        run_all(post(rows) for rows in chunk_rows(n_chunk - 1))
    else:
        blocks = [slice(r0, r0 + blk) for r0 in range(0, n_sub * seq_len, blk)]
        run_all(prep(rows) for rows in blocks)
        loop(0, n_chunk, prep_next=False, post_prev=False)
        run_all(post(rows) for rows in blocks)

    for s in range(n_sub):
        for d in range(2):
            for p in range(N_PAIR):
                st = st_s[s, d, p]
                for h in range(2):
                    rows = slice(h * HEAD_DIM, (h + 1) * HEAD_DIM)
                    val = st[rows, rows]
                    if aliased:
                        sfin_ref[s, d, 2 * p + h] = val
                    else:
                        for other in range(DEPTH):
                            sfin_ref[s, other, d, 2 * p + h] = val if other == layer else jnp.zeros_like(val)


def _rwkv(u_r, s0, seq_len, n_sub, wts, states, layer):
    n_seq = u_r.shape[0] // seq_len
    rows = n_sub * seq_len
    const2 = lambda b: (0, 0)
    st_shape = (n_sub, 2, N_PAIR, LANES, LANES)
    tok = pltpu.VMEM((rows, D_RWKV), F32)
    tok2 = pltpu.VMEM((2, rows, D_RWKV), F32)
    side_work = 2 * N_PAIR * n_sub <= SIDE_WORK_MAX_UNITS
    wts = list(wts) + _chunk_triangles(CHUNK if side_work else RWKV_BLK)
    single = n_seq == n_sub
    in_specs = [pl.BlockSpec((rows, N_RWKV_PAD), lambda b: (b, 0),
                             pipeline_mode=pl.Buffered(1) if single else None)]
    args = [u_r]
    if s0 is not None:
        in_specs.append(pl.BlockSpec(st_shape, lambda b: (b, 0, 0, 0, 0)))
        args.append(_block_diag_state(s0))
    in_specs += [pl.BlockSpec(w.shape, const2) for w in wts]
    args += list(wts)
    if states is not None:
        in_specs.append(pl.BlockSpec(memory_space=pl.ANY))
        args.append(states)
    return pl.pallas_call(
        functools.partial(_rwkv_body, seq_len, n_sub, layer, s0 is not None, states is not None),
        grid=(n_seq // n_sub,),
        in_specs=in_specs,
        out_specs=[pl.BlockSpec((rows, D_RWKV), lambda b: (b, 0),
                                pipeline_mode=pl.Buffered(1) if single else None),
                   pl.BlockSpec((n_sub, DEPTH, 2, H_RWKV, HEAD_DIM, HEAD_DIM), lambda b: (b, 0, 0, 0, 0, 0))
                   if states is None else
                   pl.BlockSpec((n_sub, None, 2, H_RWKV, HEAD_DIM, HEAD_DIM),
                                lambda b: (b, layer, 0, 0, 0, 0))],
        out_shape=[jax.ShapeDtypeStruct((n_seq * seq_len, D_RWKV), BF16),
                   jax.ShapeDtypeStruct((n_seq, DEPTH, 2, H_RWKV, HEAD_DIM, HEAD_DIM), F32)],
        scratch_shapes=[tok] * 3 + [tok2] * 4 + [tok, pltpu.VMEM(st_shape, F32)],
        input_output_aliases={} if states is None else {len(args) - 1: 1},
        compiler_params=_cp(1),
    )(*args)


def _rwkv_weights(p, l):
    z = functools.partial(jnp.zeros, dtype=F32)
    wdec = z((LANES, 2 * D_RWKV))
    wdec = wdec.at[0:LORA_W, :D_RWKV].set(p['decay_up'][l, 0])
    wdec = wdec.at[LORA_W:2 * LORA_W, D_RWKV:].set(p['decay_up'][l, 1])
    wicl = z((LANES, 2 * D_RWKV))
    wicl = wicl.at[2 * LORA_W:2 * LORA_W + LORA_A, :D_RWKV].set(p['iclr_up'][l, 0])
    wicl = wicl.at[2 * LORA_W + LORA_A:2 * LORA_W + 2 * LORA_A, D_RWKV:].set(p['iclr_up'][l, 1])
    wg = z((LANES, D_RWKV)).at[0:LORA_G].set(p['gate_up'][l])
    w0a0 = jnp.stack([p['decay_w0'][l].reshape(-1), p['iclr_a0'][l].reshape(-1)])
    vec = jnp.stack([p['k_k'][l], p['k_a'][l], p['r_k'][l].reshape(-1), p['lnx_g'][l], p['lnx_b'][l],
                     z((D_RWKV,)), z((D_RWKV,)), z((D_RWKV,))])
    head = np.arange(LANES) // HEAD_DIM
    bo = jnp.asarray(head[:, None] == head[None, :], BF16)
    return [wdec.astype(BF16), wicl.astype(BF16), wg.astype(BF16), w0a0, vec, bo]


def _chunk_triangles(rows):
    idx = np.arange(rows)
    same = idx[None, :] // CHUNK == idx[:, None] // CHUNK
    return [jnp.asarray(same & (idx[None, :] <= idx[:, None]), BF16),
            jnp.asarray(same & (idx[None, :] >= idx[:, None]), BF16)]


def _rope(x, cos, sin):
    lane = lax.broadcasted_iota(jnp.int32, (1, LANES), 1)
    first_half = (lane & ROPE_PAIRS) == 0
    partner = jnp.where(first_half, pltpu.roll(x, LANES - ROPE_PAIRS, axis=1),
                        pltpu.roll(x, ROPE_PAIRS, axis=1))
    return x * cos + partner * sin


def _attn_body(has_ctx, lam_init, *refs):
    if has_ctx:
        (q_ref, k_ref, v_ref, lp_ref, sg_ref, kc_ref, vc_ref, cq_ref, sq_ref, ck_ref, sk_ref,
         o_ref) = refs
    else:
        q_ref, k_ref, v_ref, lp_ref, sg_ref, o_ref = refs
    n_pair = D_DIFF // LANES
    lp = lp_ref[...]
    lam = (jnp.exp(jnp.sum(lp[0:1] * lp[1:2], axis=-1, keepdims=True))
           - jnp.exp(jnp.sum(lp[2:3] * lp[3:4], axis=-1, keepdims=True)) + lam_init)
    lane = lax.broadcasted_iota(jnp.int32, (1, LANES), 1)

    tq = q_ref.shape[0] // k_ref.shape[0]

    def operands(seq, p):
        sl = slice(p * LANES, (p + 1) * LANES)
        q = q_ref[seq * tq:(seq + 1) * tq, sl]
        k = k_ref[seq, :, sl]
        v = v_ref[seq, :, sl]
        if has_ctx:
            q = _rope(q.astype(F32), cq_ref[...], sq_ref[...])
            k = _rope(k, ck_ref[...], sk_ref[...])
            k = jnp.concatenate([kc_ref[0, :, sl], k], axis=0)
            v = jnp.concatenate([vc_ref[0, :, sl], v], axis=0)
        vt = v.T
        ones = jnp.ones((ONES_ROWS, v.shape[0]), F32)
        vts = [jnp.concatenate([vt[h * HEAD_DIM:(h + 1) * HEAD_DIM], ones], axis=0).astype(BF16)
               for h in range(2)]
        return q.astype(BF16), k.astype(BF16), vts

    n_keys = k_ref.shape[1] + (kc_ref.shape[1] if has_ctx else 0)
    blocks = range(0, n_keys, KEY_BLOCK)
    ops, tiles = {}, []
    for seq in range(k_ref.shape[0]):
        for p in range(n_pair):
            for j in blocks:
                tiles += [(seq, p, h, m, j) for h in range(2) for m in range(2)]

    def score(tile):
        seq, p, h, m, j = tile
        if (seq, p) not in ops:
            ops[seq, p] = operands(seq, p)
        q, k, _ = ops[seq, p]
        lo = h * HEAD_DIM + m * D_QK
        sel = (lane >= lo) & (lane < lo + D_QK)
        return _bdot_nt(k[j:j + KEY_BLOCK], jnp.where(sel, q, jnp.zeros_like(q)))

    run_max, acc = {}, {}

    def consume(tile, s):
        seq, p, h, m, j = tile
        vt = ops[seq, p][2][h][:, j:j + KEY_BLOCK]
        c = (seq, p, h, m)
        mj = jnp.max(s, axis=0, keepdims=True)
        if j == 0:
            run_max[c] = mj
            acc[c] = jnp.dot(vt, jnp.exp2(s - mj).astype(BF16), preferred_element_type=F32)
        else:
            m_new = jnp.maximum(run_max[c], mj)
            acc[c] = (acc[c] * jnp.exp2(run_max[c] - m_new)
                      + jnp.dot(vt, jnp.exp2(s - m_new).astype(BF16), preferred_element_type=F32))
            run_max[c] = m_new
        if j == blocks[-1] and (h, m) == (1, 1):
            finish(seq, p)

    def finish(seq, p):
        halves = []
        for h in range(2):
            a0, a1 = acc.pop((seq, p, h, 0)), acc.pop((seq, p, h, 1))
            o = (a0[:HEAD_DIM] * (1.0 / a0[HEAD_DIM:HEAD_DIM + 1])
                 - lam * (a1[:HEAD_DIM] * (1.0 / a1[HEAD_DIM:HEAD_DIM + 1])))
            ms = jnp.mean(o * o, axis=0, keepdims=True)
            halves.append(o * lax.rsqrt(ms + SUBLN_EPS))
        o_ref[seq * tq:(seq + 1) * tq, p * LANES:(p + 1) * LANES] = (
            jnp.concatenate(halves, axis=0).T * sg_ref[...] * (1.0 - lam_init)).astype(BF16)

    pending = []
    for tile in tiles:
        pending.append((tile, score(tile)))
        if len(pending) > ATTN_LOOKAHEAD:
            consume(*pending.pop(0))
    for item in pending:
        consume(*item)


def _attention(q, k_all, v_all, layer, n_sub, lam_init, lp, sg, ctx=None):
    n_tok = q.shape[0]
    n_seq, _, seq_len, _ = k_all.shape
    tq = ATTN_TQ
    nq = seq_len // tq
    assert n_sub == 1 or nq == 1
    kv_spec = pl.BlockSpec((n_sub, None, seq_len, D_DIFF), lambda b, i: (b, layer, 0, 0))
    in_specs = [pl.BlockSpec((n_sub * tq, D_DIFF), lambda b, i: (b * nq + i, 0)),
                kv_spec, kv_spec,
                pl.BlockSpec(lp.shape, lambda b, i: (0, 0)),
                pl.BlockSpec((1, LANES), lambda b, i: (0, 0))]
    args = [q, k_all, v_all, lp, sg]
    if ctx is not None:
        kc, vc, cos, sin = ctx
        past = kc.shape[1]
        in_specs += [pl.BlockSpec((1, past, D_DIFF), lambda b, i: (b, 0, 0)),
                     pl.BlockSpec((1, past, D_DIFF), lambda b, i: (b, 0, 0)),
                     pl.BlockSpec((tq, LANES), lambda b, i: (i, 0)),
                     pl.BlockSpec((tq, LANES), lambda b, i: (i, 0)),
                     pl.BlockSpec((seq_len, LANES), lambda b, i: (0, 0)),
                     pl.BlockSpec((seq_len, LANES), lambda b, i: (0, 0))]
        args += [kc, vc, cos, sin, cos, sin]
    return pl.pallas_call(
        functools.partial(_attn_body, ctx is not None, lam_init),
        grid=(n_seq // n_sub, nq),
        in_specs=in_specs,
        out_specs=pl.BlockSpec((n_sub * tq, D_DIFF), lambda b, i: (b * nq + i, 0)),
        out_shape=jax.ShapeDtypeStruct((n_tok, D_DIFF), BF16),
        compiler_params=_cp(2),
    )(*args)


def _rope_tables(seq_len):
    t = jnp.arange(seq_len)
    pos = jnp.stack([(t // GRID_W).astype(F32), (t % GRID_W).astype(F32)], axis=1)
    inv = 1.0 / (ROPE_BASE ** (jnp.arange(ROPE_PAIRS, dtype=F32) / ROPE_PAIRS))
    ang = pos[:, :, None] * inv
    d = np.arange(LANES) % D_QK
    axis = d // (2 * ROPE_PAIRS)
    second = (d % (2 * ROPE_PAIRS)) // ROPE_PAIRS
    idx = d % ROPE_PAIRS
    cos = jnp.cos(ang)[:, axis, idx]
    sin = jnp.sin(ang)[:, axis, idx] * jnp.asarray(np.where(second == 1, 1.0, -1.0), F32)
    return cos, sin


def _fnet_body(seq_len, x_ref, ct_ref, st_ref, cc_ref, sc_ref, o_ref):
    n_sub = x_ref.shape[0] // seq_len
    x = x_ref[...]
    xc = jnp.dot(x, cc_ref[...], preferred_element_type=F32)
    xs = jnp.dot(x, sc_ref[...], preferred_element_type=F32)
    wide = lambda a: jnp.concatenate([a[s * seq_len:(s + 1) * seq_len] for s in range(n_sub)], axis=1)
    y = _bdot(ct_ref[...], wide(xc)) - _bdot(st_ref[...], wide(xs))
    for s in range(n_sub):
        o_ref[s * seq_len:(s + 1) * seq_len, :] = y[:, s * D_FNET:(s + 1) * D_FNET].astype(BF16)


def _dft_consts(n, block=1):
    idx = np.arange(n)
    ang = 2.0 * np.pi * ((idx[:, None] * idx[None, :]) % n) / n
    return [jnp.asarray(np.kron(np.eye(block), m).astype(np.float32)).astype(BF16)
            for m in (np.cos(ang) / np.sqrt(n), np.sin(ang) / np.sqrt(n))]


def _fnet(u_f, seq_len, n_sub):
    n_tok = u_f.shape[0]
    rows = n_sub * seq_len
    consts = _dft_consts(seq_len) + _dft_consts(FNET_GROUP_DIM, FNET_GROUPS)
    const = lambda b: (0, 0)
    return pl.pallas_call(
        functools.partial(_fnet_body, seq_len),
        grid=(n_tok // rows,),
        in_specs=[pl.BlockSpec((rows, D_FNET), lambda b: (b, 0))]
        + [pl.BlockSpec(c.shape, const) for c in consts],
        out_specs=pl.BlockSpec((rows, D_FNET), lambda b: (b, 0)),
        out_shape=jax.ShapeDtypeStruct((n_tok, D_FNET), BF16),
        compiler_params=_cp(1),
    )(u_f, *consts)


def _ffn_body(final, yr_ref, yd_ref, yf_ref, x_ref, g1_ref, sh2_ref, sc2_ref, g2_ref, n2_ref, fg_ref,
              wo_ref, wi_ref, wf_ref, o_ref):
    part = x_ref.shape[0] // N_PARTS
    parts = [slice(j * part, (j + 1) * part) for j in range(N_PARTS)]
    y = [jnp.dot(jnp.concatenate([yr_ref[r, :], yd_ref[r, :], yf_ref[r, :]], axis=1), wo_ref[...],
                 preferred_element_type=F32) for r in parts]
    x = [x_ref[r, :] + g1_ref[0] * y[j] for j, r in enumerate(parts)]
    h = [(_rms(xj, n2_ref[...]) * (1.0 + sc2_ref[0]) + sh2_ref[0]).astype(BF16) for xj in x]
    z = [jnp.dot(hj, wi_ref[...], preferred_element_type=F32) for hj in h]
    act = [zj[:, :D_FF] * jax.nn.sigmoid(zj[:, :D_FF]) * zj[:, D_FF:] for zj in z]
    f = [_bdot(aj, wf_ref[...]) for aj in act]
    for j, r in enumerate(parts):
        xj = x[j] + g2_ref[0] * f[j]
        o_ref[r, :] = _rms(xj, fg_ref[...]) if final else xj


def _ffn(y_r, y_d, y_f, x, mod, layer, row_fn, n2, fg, wo, wi, wf, final, tm):
    n_tok = x.shape[0]
    row = lambda i: (i, 0)
    const = lambda i: (0, 0)
    return pl.pallas_call(
        functools.partial(_ffn_body, final),
        grid=(n_tok // tm,),
        in_specs=[pl.BlockSpec((tm, D_RWKV), row),
                  pl.BlockSpec((tm, D_DIFF), row),
                  pl.BlockSpec((tm, D_FNET), row),
                  pl.BlockSpec((tm, D_MODEL), row),
                  _mod_spec(layer, 2, row_fn),
                  _mod_spec(layer, 3, row_fn),
                  _mod_spec(layer, 4, row_fn),
                  _mod_spec(layer, 5, row_fn),
                  pl.BlockSpec((1, D_MODEL), const),
                  pl.BlockSpec((1, D_MODEL), const),
                  pl.BlockSpec((None,) + wo.shape[1:], lambda i: (layer, 0, 0), pipeline_mode=pl.Buffered(1)),
                  pl.BlockSpec((None,) + wi.shape[1:], lambda i: (layer, 0, 0), pipeline_mode=pl.Buffered(1)),
                  pl.BlockSpec((None,) + wf.shape[1:], lambda i: (layer, 0, 0), pipeline_mode=pl.Buffered(1))],
        out_specs=pl.BlockSpec((tm, D_MODEL), row),
        out_shape=jax.ShapeDtypeStruct((n_tok, D_MODEL), F32),
        compiler_params=_cp(1),
    )(y_r, y_d, y_f, x, mod, mod, mod, mod, n2.reshape(1, D_MODEL), fg.reshape(1, D_MODEL), wo, wi, wf)


def _block_diag_state(s):
    b = s.shape[0]
    s = s.reshape(b, 2, N_PAIR, 2, HEAD_DIM, HEAD_DIM)
    eye = jnp.eye(2, dtype=s.dtype)
    s = s[:, :, :, :, :, None, :] * eye[None, None, None, :, None, :, None]
    return s.reshape(b, 2, N_PAIR, LANES, LANES)


def kernel(x_prompt, x_sample, c, state_rwkv, cache_diff_k, cache_diff_v, c_ctx, norm1_g, norm2_g, final_norm_g, w_mod, b_mod, w_in, w_out, shift_mu, decay_w0, decay_up, iclr_a0, iclr_up, gate_up, k_k, k_a, r_k, lnx_g, lnx_b, diff_lambda, subln_g, w_ffn_in, w_ffn_out):
    p = dict(shift_mu=shift_mu, decay_w0=decay_w0, decay_up=decay_up, iclr_a0=iclr_a0, iclr_up=iclr_up,
             gate_up=gate_up, k_k=k_k, k_a=k_a, r_k=r_k, lnx_g=lnx_g, lnx_b=lnx_b)
    n_ctx, t_ctx, _ = x_prompt.shape
    n_dec, t_dec, _ = x_sample.shape
    past = cache_diff_k.shape[2]

    cond = jnp.concatenate([c_ctx[None, :], c, jnp.zeros((MOD_ROWS - 1 - n_dec, D_MODEL), F32)], axis=0)
    mod = _modulation(cond, w_mod, b_mod).reshape(DEPTH * MOD_ROWS, 1, 6 * D_MODEL)

    tm_ffn = FFN_TM
    ctx_plan = _stream_plan(n_ctx, t_ctx)
    dec_plan = _stream_plan(n_dec, t_dec)
    streams = [
        dict(x=x_prompt.reshape(n_ctx * t_ctx, D_MODEL), t=t_ctx, n=n_ctx, **ctx_plan,
             row_in=lambda i: 0, row_ffn=lambda i: 0),
        dict(x=x_sample.reshape(n_dec * t_dec, D_MODEL), t=t_dec, n=n_dec, **dec_plan,
             row_in=lambda i: 1 + i // (t_dec // dec_plan['tm_in']), row_ffn=lambda i: 1 + i // (t_dec // tm_ffn)),
    ]
    cos, sin = _rope_tables(t_dec)
    for st in streams:
        st['kv'] = st['states'] = None
    w_t = jnp.swapaxes(w_in, 1, 2)
    w_in_l = jnp.concatenate(
        [w_t[:, :N_RWKV_IN], jnp.zeros((DEPTH, N_RWKV_PAD - N_RWKV_IN, D_MODEL), F32), w_t[:, N_RWKV_IN:]],
        axis=1).astype(BF16)
    wo = w_out.astype(BF16)
    wi = w_ffn_in.astype(BF16)
    wf = w_ffn_out.astype(BF16)
    for l in range(DEPTH):
        rw = _rwkv_weights(p, l)
        mu = jnp.concatenate([shift_mu[l], jnp.zeros((2, N_RWKV_PAD - N_RWKV_IN), F32)], axis=1)
        lam_init = 0.8 - 0.6 * math.exp(-0.3 * l)
        sg = jnp.tile(subln_g[l], 2).reshape(1, LANES)
        for si, st in enumerate(streams):
            u_r, q, k_all, v_all, u_f = _inproj(st['x'], mod, l, st['row_in'], norm1_g[l], w_in_l, mu,
                                                 st['kv'], st['t'], st['tm_in'])
            st['kv'] = (k_all, v_all)
            if si == 0:
                s0 = None
                attn_ctx = None
            else:
                s0 = state_rwkv[:, l].astype(F32)
                attn_ctx = (cache_diff_k[:, l].reshape(n_dec, past, D_DIFF).astype(F32),
                            cache_diff_v[:, l].reshape(n_dec, past, D_DIFF).astype(F32), cos, sin)
            y_r, st['states'] = _rwkv(u_r, s0, st['t'], st['n_sub'], rw, st['states'], l)
            y_d = _attention(q, k_all, v_all, l, st['n_attn'], lam_init, diff_lambda[l], sg, attn_ctx)
            y_f = _fnet(u_f, st['t'], st['n_fnet'])
            st['x'] = _ffn(y_r, y_d, y_f, st['x'], mod, l, st['row_ffn'], norm2_g[l], final_norm_g,
                           wo, wi, wf, l == DEPTH - 1, tm_ffn)
    y_prompt = streams[0]['x'].reshape(n_ctx, t_ctx, D_MODEL)
    y_sample = streams[1]['x'].reshape(n_dec, t_dec, D_MODEL)
    new_k = streams[0]['kv'][0].reshape(n_ctx, DEPTH, t_ctx, H_DIFF, 2, D_QK)
    new_v = streams[0]['kv'][1].reshape(n_ctx, DEPTH, t_ctx, H_DIFF, HEAD_DIM)
    return (y_prompt, y_sample, streams[0]['states'], new_k, new_v)
```

```python
import functools
import math

import numpy as np
import jax
import jax.numpy as jnp
from jax import lax
from jax.experimental import pallas as pl
from jax.experimental.pallas import tpu as pltpu

F32 = jnp.float32
BF16 = jnp.bfloat16

D_MODEL = 1024
DEPTH = 2
GRID_W = 64
HEAD_DIM = 64
D_RWKV = 384
H_RWKV = D_RWKV // HEAD_DIM
D_DIFF = 384
H_DIFF = D_DIFF // HEAD_DIM
D_QK = HEAD_DIM // 2
D_FNET = D_MODEL - D_RWKV - D_DIFF
FNET_GROUPS = 4
FNET_GROUP_DIM = D_FNET // FNET_GROUPS
LORA_W = 32
LORA_A = 32
LORA_G = 64
N_RWKV_IN = 3 * D_RWKV + 2 * LORA_W + 2 * LORA_A + LORA_G
N_DIFF_IN = 3 * D_DIFF
D_FF = ((8 * D_MODEL + 3 * 256 - 1) // (3 * 256)) * 256
ROPE_PAIRS = D_QK // 4
ROPE_BASE = 10000.0
RMS_EPS = 1e-6
GN_EPS = 64e-5
SUBLN_EPS = 1e-5
DECAY_SCALE = math.exp(-0.5)
Q_SCALE = D_QK ** -0.5 * math.log2(math.e)

LANES = 128
N_RWKV_PAD = 11 * LANES
D_IN_PAD = N_RWKV_PAD + N_DIFF_IN + D_FNET
N_PAIR = H_RWKV // 2
CHUNK = 64
N_PARTS = 2
SIDE_WORK_MAX_UNITS = 12
RWKV_BLK = 128
MOD_ROWS = 8
ONES_ROWS = 16
VMEM_LIMIT = 60 * 1024 * 1024

MOD_TN = 1536
INPROJ_TM = 512
FFN_TM = 512
ATTN_TQ = 256
ATTN_STEP_ROWS = 1024
KEY_BLOCK = 512
ATTN_LOOKAHEAD = 8
FNET_STEP_ROWS = 1024
RWKV_STEP_ROWS = 1024
RWKV_ONE_STEP_ROWS = 2048


def _stream_plan(n_seq, seq_len):
    one_step = n_seq * seq_len <= RWKV_ONE_STEP_ROWS
    return dict(n_sub=n_seq if one_step else max(1, RWKV_STEP_ROWS // seq_len),
                n_attn=max(1, ATTN_STEP_ROWS // seq_len),
                n_fnet=max(1, FNET_STEP_ROWS // seq_len),
                tm_in=max(seq_len, INPROJ_TM))


def _cp(n_axes=1):
    return pltpu.CompilerParams(dimension_semantics=("arbitrary",) * n_axes,
                                vmem_limit_bytes=VMEM_LIMIT)


def _bdot(a, b):
    return jnp.dot(a.astype(BF16), b.astype(BF16), preferred_element_type=F32)


def _bdot_nt(a, b):
    return lax.dot_general(a.astype(BF16), b.astype(BF16), (((1,), (1,)), ((), ())),
                           preferred_element_type=F32)


def _split2(x):
    hi = x.astype(BF16)
    lo = (x - hi.astype(F32)).astype(BF16)
    return hi, lo


def _dot_x3(a, b):
    a_hi, a_lo = _split2(a)
    b_hi, b_lo = _split2(b)
    d = functools.partial(jnp.dot, preferred_element_type=F32)
    return d(a_hi, b_hi) + d(a_lo, b_hi) + d(a_hi, b_lo)


def _rms(x, g):
    return x * lax.rsqrt(jnp.mean(x * x, axis=-1, keepdims=True) + RMS_EPS) * g


def _w_in_body(x_ref, o_ref):
    col = lax.broadcasted_iota(jnp.int32, (1, N_RWKV_PAD), 1)
    head = jnp.where(col < N_RWKV_IN, x_ref[0:N_RWKV_PAD, :].T, 0.0)
    tail = x_ref[N_RWKV_IN:, :].T
    o_ref[:, :N_RWKV_PAD] = head.astype(BF16)
    o_ref[:, N_RWKV_PAD:] = tail.astype(BF16)


def _w_in_rowmajor(w_in):
    w_t = jnp.swapaxes(w_in, 1, 2)
    n_layers, n_out, _ = w_t.shape
    return pl.pallas_call(
        _w_in_body,
        grid=(n_layers,),
        in_specs=[pl.BlockSpec((None, n_out, D_MODEL), lambda l: (l, 0, 0))],
        out_specs=pl.BlockSpec((None, D_MODEL, D_IN_PAD), lambda l: (l, 0, 0)),
        out_shape=jax.ShapeDtypeStruct((n_layers, D_MODEL, D_IN_PAD), BF16),
        compiler_params=_cp(1),
    )(w_t)


def _mod_body(c_ref, w_ref, b_ref, o_ref):
    c = c_ref[...]
    a = c * jax.nn.sigmoid(c)
    o_ref[0] = _dot_x3(a, w_ref[0]) + b_ref[0]


def _modulation(cond, w_mod, b_mod):
    n_layers, _, n_out = w_mod.shape
    tn = MOD_TN
    return pl.pallas_call(
        _mod_body,
        grid=(n_layers, n_out // tn),
        in_specs=[pl.BlockSpec((MOD_ROWS, D_MODEL), lambda l, j: (0, 0)),
                  pl.BlockSpec((1, D_MODEL, tn), lambda l, j: (l, 0, j)),
                  pl.BlockSpec((1, 1, tn), lambda l, j: (l, 0, j))],
        out_specs=pl.BlockSpec((1, MOD_ROWS, tn), lambda l, j: (l, 0, j)),
        out_shape=jax.ShapeDtypeStruct((n_layers, MOD_ROWS, n_out), F32),
        compiler_params=_cp(2),
    )(cond, w_mod, b_mod.reshape(n_layers, 1, n_out))


def _mod_spec(layer, col, row_fn):
    return pl.BlockSpec((1, 1, D_MODEL), lambda i: (layer * MOD_ROWS + row_fn(i), 0, col))


def _put_layer(ref, idx, layer, aliased, val):
    if aliased:
        ref[idx] = val
    else:
        for other in range(DEPTH):
            ref[idx + (other,)] = val if other == layer else jnp.zeros_like(val)


def _inproj_body(seq_len, layer, aliased, x_ref, g_ref, sh_ref, sc_ref, w_ref, mu_ref, *refs):
    ur_ref, q_ref, k_ref, v_ref, uf_ref = refs[2:] if aliased else refs
    part = x_ref.shape[0] // N_PARTS
    parts = [slice(j * part, (j + 1) * part) for j in range(N_PARTS)]
    h = [(_rms(x_ref[r, :], g_ref[...]) * (1.0 + sc_ref[0]) + sh_ref[0]).astype(BF16) for r in parts]
    u = jnp.concatenate([jnp.dot(hj, w_ref[...], preferred_element_type=F32) for hj in h], axis=0)
    ur = u[:, :N_RWKV_PAD]
    tm = ur.shape[0]
    pos = lax.broadcasted_iota(jnp.int32, (tm, 1), 0) & (seq_len - 1)
    prev = jnp.where(pos == 0, 0.0, pltpu.roll(ur, 1, axis=0))
    nxt = jnp.where(pos == seq_len - 1, 0.0, pltpu.roll(ur, tm - 1, axis=0))
    ur_ref[...] = (ur + mu_ref[0:1, :] * (prev - ur) + mu_ref[1:2, :] * (nxt - ur)).astype(BF16)
    q_ref[...] = (u[:, N_RWKV_PAD:N_RWKV_PAD + D_DIFF] * Q_SCALE).astype(BF16)
    for s in range(tm // seq_len):
        rows = slice(s * seq_len, (s + 1) * seq_len)
        _put_layer(k_ref, (s,), layer, aliased, u[rows, N_RWKV_PAD + D_DIFF:N_RWKV_PAD + 2 * D_DIFF])
        _put_layer(v_ref, (s,), layer, aliased, u[rows, N_RWKV_PAD + 2 * D_DIFF:N_RWKV_PAD + 3 * D_DIFF])
    uf_ref[...] = u[:, N_RWKV_PAD + N_DIFF_IN:].astype(BF16)


def _inproj(x, mod, layer, row_fn, g, w, mu, kv, seq_len, tm):
    n_tok = x.shape[0]
    kv_shape = jax.ShapeDtypeStruct((n_tok // seq_len, DEPTH, seq_len, D_DIFF), F32)
    assert tm % seq_len == 0 and seq_len & (seq_len - 1) == 0
    row = lambda i: (i, 0)
    const = lambda i: (0, 0)
    if kv is None:
        kv_spec = pl.BlockSpec((tm // seq_len, DEPTH, seq_len, D_DIFF), lambda i: (i, 0, 0, 0))
    else:
        kv_spec = pl.BlockSpec((tm // seq_len, None, seq_len, D_DIFF), lambda i: (i, layer, 0, 0))
    return pl.pallas_call(
        functools.partial(_inproj_body, seq_len, layer, kv is not None),
        grid=(n_tok // tm,),
        in_specs=[pl.BlockSpec((tm, D_MODEL), row),
                  pl.BlockSpec((1, D_MODEL), const),
                  _mod_spec(layer, 0, row_fn),
                  _mod_spec(layer, 1, row_fn),
                  pl.BlockSpec((None, D_MODEL, D_IN_PAD), lambda i: (layer, 0, 0)),
                  pl.BlockSpec((2, N_RWKV_PAD), const)]
        + [pl.BlockSpec(memory_space=pl.ANY)] * (0 if kv is None else 2),
        out_specs=[pl.BlockSpec((tm, N_RWKV_PAD), row),
                   pl.BlockSpec((tm, D_DIFF), row),
                   kv_spec, kv_spec,
                   pl.BlockSpec((tm, D_FNET), row)],
        out_shape=[jax.ShapeDtypeStruct((n_tok, N_RWKV_PAD), BF16),
                   jax.ShapeDtypeStruct((n_tok, D_DIFF), BF16),
                   kv_shape, kv_shape,
                   jax.ShapeDtypeStruct((n_tok, D_FNET), BF16)],
        input_output_aliases={} if kv is None else {6: 2, 7: 3},
        compiler_params=_cp(1),
    )(x, g.reshape(1, D_MODEL), mod, mod, w, mu, *(kv or ()))


def _lane_masks():
    lane = lax.broadcasted_iota(jnp.int32, (1, LANES), 1)
    return lane < HEAD_DIM, lane >= HEAD_DIM


def _rwkv_units(units, m0, m1, side=()):
    side = list(side)

    def run_side(drain=False):
        for gen in list(side):
            for _ in gen:
                if not drain:
                    break
            else:
                side.remove(gen)

    def bd(x):
        xb = x.astype(BF16)
        zero = jnp.zeros_like(xb)
        return jnp.concatenate([jnp.where(m0, xb, zero), jnp.where(m1, xb, zero)], axis=0)

    row = lax.broadcasted_iota(jnp.int32, (CHUNK, LANES), 0)
    col = lax.broadcasted_iota(jnp.int32, (CHUNK, LANES), 1) & (CHUNK - 1)
    eye = (col == row).astype(F32)
    r2 = lax.broadcasted_iota(jnp.int32, (LANES, LANES), 0) < HEAD_DIM
    c2 = lax.broadcasted_iota(jnp.int32, (LANES, LANES), 1) < HEAD_DIM
    rng = range(len(units))

    pre = []
    for rev, kk, r, v, kd, b, cum, cex, tot, s_prev in units:
        p_inv = jnp.exp(-cum)
        p_rem = jnp.exp(tot - cum)
        ab = -kk * jnp.exp(cex)
        rb = r * jnp.exp(cum)
        strict = (col > row) if rev else (col < row)
        incl = (col >= row) if rev else (col <= row)
        pre.append(dict(ab=ab, rb=rb, vbd=bd(v), strict=strict, incl=incl,
                        lhs=jnp.concatenate([ab, rb], axis=0),
                        rhs=jnp.concatenate([bd(b * p_inv), bd(kd * p_inv)], axis=0),
                        bk=jnp.concatenate([b * p_rem, kd * p_rem], axis=0)))

    run_side()
    mm = [_bdot_nt(q['lhs'], q['rhs']) for q in pre]
    run_side()
    m_ab = [jnp.where(pre[i]['strict'], mm[i][:CHUNK, :LANES], 0.0) for i in rng]
    m_ak = [jnp.where(pre[i]['strict'], mm[i][:CHUNK, LANES:], 0.0) for i in rng]
    m_r = [jnp.concatenate([jnp.where(pre[i]['incl'], mm[i][CHUNK:, :LANES], 0.0),
                            jnp.where(pre[i]['incl'], mm[i][CHUNK:, LANES:], 0.0)], axis=1) for i in rng]
    mv = [_bdot(m_ak[i], pre[i]['vbd']) for i in rng]
    run_side()

    t = [eye + m_ab[i] for i in rng]
    n = [_bdot(m_ab[i], bd(m_ab[i])) for i in rng]
    for _ in range(4):
        x = [_bdot(jnp.concatenate([t[i], n[i]], axis=0), bd(n[i])) for i in rng]
        t = [t[i] + x[i][:CHUNK] for i in rng]
        n = [x[i][CHUNK:] for i in rng]
        run_side()
    t = [t[i] + _bdot(t[i], bd(n[i])) for i in rng]

    w = [_bdot(t[i], jnp.concatenate([bd(pre[i]['ab']), bd(mv[i])], axis=1)) for i in rng]
    xs = [_bdot_nt(jnp.concatenate([w[i][:, :LANES], pre[i]['rb']], axis=0), units[i][9]) for i in rng]
    u = [xs[i][:CHUNK] + w[i][:, LANES:] for i in rng]
    run_side(drain=True)
    y = [xs[i][CHUNK:] + _bdot(m_r[i], jnp.concatenate([bd(u[i]), pre[i]['vbd']], axis=0)) for i in rng]
    z = [_bdot(jnp.concatenate([u[i], units[i][3]], axis=0).T, pre[i]['bk']) for i in rng]
    s_new = [units[i][9] * jnp.exp(units[i][8]) + jnp.where(r2 == c2, z[i], 0.0) for i in rng]
    return y, s_new


def _rwkv_body(seq_len, n_sub, layer, has_s0, aliased, u_ref, *refs):
    s0_ref = refs[0] if has_s0 else None
    refs = refs[1:] if has_s0 else refs
    wdec_ref, wicl_ref, wg_ref, w0a0_ref, vec_ref, bo_ref, tril_ref, triu_ref = refs[:8]
    (y_ref, sfin_ref, r_s, v_s, kk_s, kd_s, b_s, ci_s, ce_s, y_s, st_s) = refs[9 if aliased else 8:]
    n_chunk = seq_len // CHUNK
    assert n_chunk % 2 == 0 and n_chunk >= 4
    k_k = vec_ref[0:1, :]
    k_a = vec_ref[1:2, :]
    r_k = vec_ref[2:3, :]
    lnx_g = vec_ref[3:4, :]
    lnx_b = vec_ref[4:5, :]
    bo = bo_ref[...]

    def headsum(xb):
        return jnp.concatenate([jnp.dot(xb[:, p * LANES:(p + 1) * LANES], bo, preferred_element_type=F32)
                                for p in range(N_PAIR)], axis=1)

    def prep(rows):
        xs = u_ref[rows, :].astype(F32)
        r = xs[:, 0:D_RWKV]
        k = xs[:, D_RWKV:2 * D_RWKV]
        v = xs[:, 2 * D_RWKV:3 * D_RWKV]
        lora = xs[:, 3 * D_RWKV:3 * D_RWKV + LANES]
        t_lora = jnp.tanh(lora).astype(BF16)
        lora = lora.astype(BF16)
        kk = k * k_k
        kk2 = (kk * kk).astype(BF16)
        yield
        dec = _bdot(t_lora, wdec_ref[...])
        icl = _bdot(lora, wicl_ref[...])
        ss = headsum(kk2)
        yield
        logw = -DECAY_SCALE * jax.nn.sigmoid(w0a0_ref[0:1, :] + dec)
        a = jax.nn.sigmoid(w0a0_ref[1:2, :] + icl)
        kk = kk / jnp.maximum(jnp.sqrt(ss), 1e-12)
        a_f = a[:, :D_RWKV]
        a_b = a[:, D_RWKV:]
        kka = k * k_a
        kd_f = (k - kka) + kka * a_f
        kd_b = (k - kka) + kka * a_b
        lws = [_split2(logw[:, d * D_RWKV:(d + 1) * D_RWKV]) for d in range(2)]
        r_s[rows, :] = r
        v_s[rows, :] = v
        kk_s[rows, :] = kk
        kd_s[0, rows, :] = kd_f
        kd_s[1, rows, :] = kd_b
        b_s[0, rows, :] = kk * a_f
        b_s[1, rows, :] = kk * a_b
        yield
        dd = functools.partial(jnp.dot, preferred_element_type=F32)
        cums = [dd(tri_ref[...], lws[d][0]) + dd(tri_ref[...], lws[d][1])
                for d, tri_ref in enumerate((tril_ref, triu_ref))]
        yield
        for d in range(2):
            ci_s[d, rows, :] = cums[d]
            ce_s[d, rows, :] = cums[d] - logw[:, d * D_RWKV:(d + 1) * D_RWKV]

    def post(rows):
        y = y_s[rows, :]
        yb = y.astype(BF16)
        v = v_s[rows, :]
        bon = _split2(r_s[rows, :] * (kd_s[0, rows, :] + kd_s[1, rows, :]) * r_k)
        gd = u_ref[rows, 3 * D_RWKV + LANES:3 * D_RWKV + 2 * LANES].astype(F32)
        s_gd = jax.nn.sigmoid(gd).astype(BF16)
        yield
        mean = headsum(yb) * (1.0 / HEAD_DIM)
        bonus = (headsum(bon[0]) + headsum(bon[1])) * v
        g = _bdot(s_gd, wg_ref[...])
        yield
        yc = y - mean
        yc2 = (yc * yc).astype(BF16)
        yield
        var = headsum(yc2) * (1.0 / HEAD_DIM)
        yield
        yn = yc * lax.rsqrt(var + GN_EPS) * lnx_g + lnx_b
        y_ref[rows, :] = ((yn + bonus) * g).astype(BF16)

    def run_all(gens):
        gens = list(gens)
        while gens:
            gens = [gen for gen in gens if next(gen, gens) is not gens]

    def chunk_rows(i):
        return [pl.ds(pl.multiple_of(s * seq_len + (i if d == 0 else n_chunk - 1 - i) * CHUNK, CHUNK), CHUNK)
                for s in range(n_sub) for d in range(2)]

    st_s[...] = s0_ref[...] if has_s0 else jnp.zeros_like(st_s)
    y_s[...] = jnp.zeros_like(y_s)
    m0, m1 = _lane_masks()

    def scan_step(i, prep_next, post_prev):
        rows_sd = chunk_rows(i)
        side = []
        if prep_next:
            side = [prep(rows) for rows in chunk_rows(i + 1)]
        if post_prev:
            side = [post(rows) for rows in chunk_rows(i - 1)]
        units = []
        for s in range(n_sub):
            for d in range(2):
                rows = rows_sd[s * 2 + d]
                cum = ci_s[d, rows, :]
                cex = ce_s[d, rows, :]
                tot = cum[CHUNK - 1:CHUNK] if d == 0 else cum[0:1]
                kk = kk_s[rows, :]
                r = r_s[rows, :]
                v = v_s[rows, :]
                kd = kd_s[d, rows, :]
                b = b_s[d, rows, :]
                for p in range(N_PAIR):
                    sl = slice(p * LANES, (p + 1) * LANES)
                    units.append((d == 1, kk[:, sl], r[:, sl], v[:, sl], kd[:, sl], b[:, sl],
                                  cum[:, sl], cex[:, sl], tot[:, sl], st_s[s, d, p]))
        ys, s_new = _rwkv_units(units, m0, m1, side)
        for s in range(n_sub):
            for d in range(2):
                base = (s * 2 + d) * N_PAIR
                for p in range(N_PAIR):
                    st_s[s, d, p] = s_new[base + p]
                rows = rows_sd[s * 2 + d]
                y_s[rows, :] = y_s[rows, :] + jnp.concatenate(ys[base:base + N_PAIR], axis=1)

    def loop(lo, hi, **kw):
        lax.fori_loop(lo, hi, lambda i, c: (scan_step(i, **kw), c)[1], 0)

    half = n_chunk // 2
    blk = tril_ref.shape[0]
    if blk == CHUNK:
        run_all(prep(rows) for rows in chunk_rows(0))
        loop(0, half - 1, prep_next=True, post_prev=False)
        loop(half - 1, half + 1, prep_next=False, post_prev=False)
        loop(half + 1, n_chunk, prep_next=False, post_prev=True)
        run_all(post(rows) for rows in chunk_rows(n_chunk - 1))
    else:
        blocks = [slice(r0, r0 + blk) for r0 in range(0, n_sub * seq_len, blk)]
        run_all(prep(rows) for rows in blocks)
        loop(0, n_chunk, prep_next=False, post_prev=False)
        run_all(post(rows) for rows in blocks)

    for s in range(n_sub):
        for d in range(2):
            for p in range(N_PAIR):
                st = st_s[s, d, p]
                for h in range(2):
                    rows = slice(h * HEAD_DIM, (h + 1) * HEAD_DIM)
                    val = st[rows, rows]
                    if aliased:
                        sfin_ref[s, d, 2 * p + h] = val
                    else:
                        for other in range(DEPTH):
                            sfin_ref[s, other, d, 2 * p + h] = val if other == layer else jnp.zeros_like(val)


def _rwkv(u_r, s0, seq_len, n_sub, wts, states, layer):
    n_seq = u_r.shape[0] // seq_len
    rows = n_sub * seq_len
    const2 = lambda b: (0, 0)
    st_shape = (n_sub, 2, N_PAIR, LANES, LANES)
    tok = pltpu.VMEM((rows, D_RWKV), F32)
    tok2 = pltpu.VMEM((2, rows, D_RWKV), F32)
    side_work = 2 * N_PAIR * n_sub <= SIDE_WORK_MAX_UNITS
    wts = list(wts) + _chunk_triangles(CHUNK if side_work else RWKV_BLK)
    single = n_seq == n_sub
    in_specs = [pl.BlockSpec((rows, N_RWKV_PAD), lambda b: (b, 0),
                             pipeline_mode=pl.Buffered(1) if single else None)]
    args = [u_r]
    if s0 is not None:
        in_specs.append(pl.BlockSpec(st_shape, lambda b: (b, 0, 0, 0, 0)))
        args.append(_block_diag_state(s0))
    in_specs += [pl.BlockSpec(w.shape, const2) for w in wts]
    args += list(wts)
    if states is not None:
        in_specs.append(pl.BlockSpec(memory_space=pl.ANY))
        args.append(states)
    return pl.pallas_call(
        functools.partial(_rwkv_body, seq_len, n_sub, layer, s0 is not None, states is not None),
        grid=(n_seq // n_sub,),
        in_specs=in_specs,
        out_specs=[pl.BlockSpec((rows, D_RWKV), lambda b: (b, 0),
                                pipeline_mode=pl.Buffered(1) if single else None),
                   pl.BlockSpec((n_sub, DEPTH, 2, H_RWKV, HEAD_DIM, HEAD_DIM), lambda b: (b, 0, 0, 0, 0, 0))
                   if states is None else
                   pl.BlockSpec((n_sub, None, 2, H_RWKV, HEAD_DIM, HEAD_DIM),
                                lambda b: (b, layer, 0, 0, 0, 0))],
        out_shape=[jax.ShapeDtypeStruct((n_seq * seq_len, D_RWKV), BF16),
                   jax.ShapeDtypeStruct((n_seq, DEPTH, 2, H_RWKV, HEAD_DIM, HEAD_DIM), F32)],
        scratch_shapes=[tok] * 3 + [tok2] * 4 + [tok, pltpu.VMEM(st_shape, F32)],
        input_output_aliases={} if states is None else {len(args) - 1: 1},
        compiler_params=_cp(1),
    )(*args)


def _rwkv_weights(p, l):
    z = functools.partial(jnp.zeros, dtype=F32)
    wdec = z((LANES, 2 * D_RWKV))
    wdec = wdec.at[0:LORA_W, :D_RWKV].set(p['decay_up'][l, 0])
    wdec = wdec.at[LORA_W:2 * LORA_W, D_RWKV:].set(p['decay_up'][l, 1])
    wicl = z((LANES, 2 * D_RWKV))
    wicl = wicl.at[2 * LORA_W:2 * LORA_W + LORA_A, :D_RWKV].set(p['iclr_up'][l, 0])
    wicl = wicl.at[2 * LORA_W + LORA_A:2 * LORA_W + 2 * LORA_A, D_RWKV:].set(p['iclr_up'][l, 1])
    wg = z((LANES, D_RWKV)).at[0:LORA_G].set(p['gate_up'][l])
    w0a0 = jnp.stack([p['decay_w0'][l].reshape(-1), p['iclr_a0'][l].reshape(-1)])
    vec = jnp.stack([p['k_k'][l], p['k_a'][l], p['r_k'][l].reshape(-1), p['lnx_g'][l], p['lnx_b'][l],
                     z((D_RWKV,)), z((D_RWKV,)), z((D_RWKV,))])
    head = np.arange(LANES) // HEAD_DIM
    bo = jnp.asarray(head[:, None] == head[None, :], BF16)
    return [wdec.astype(BF16), wicl.astype(BF16), wg.astype(BF16), w0a0, vec, bo]


def _chunk_triangles(rows):
    idx = np.arange(rows)
    same = idx[None, :] // CHUNK == idx[:, None] // CHUNK
    return [jnp.asarray(same & (idx[None, :] <= idx[:, None]), BF16),
            jnp.asarray(same & (idx[None, :] >= idx[:, None]), BF16)]


def _rope(x, cos, sin):
    lane = lax.broadcasted_iota(jnp.int32, (1, LANES), 1)
    first_half = (lane & ROPE_PAIRS) == 0
    partner = jnp.where(first_half, pltpu.roll(x, LANES - ROPE_PAIRS, axis=1),
                        pltpu.roll(x, ROPE_PAIRS, axis=1))
    return x * cos + partner * sin


def _attn_body(has_ctx, lam_init, *refs):
    if has_ctx:
        (q_ref, k_ref, v_ref, lp_ref, sg_ref, kc_ref, vc_ref, cq_ref, sq_ref, ck_ref, sk_ref,
         o_ref) = refs
    else:
        q_ref, k_ref, v_ref, lp_ref, sg_ref, o_ref = refs
    n_pair = D_DIFF // LANES
    lp = lp_ref[...]
    lam = (jnp.exp(jnp.sum(lp[0:1] * lp[1:2], axis=-1, keepdims=True))
           - jnp.exp(jnp.sum(lp[2:3] * lp[3:4], axis=-1, keepdims=True)) + lam_init)
    lane = lax.broadcasted_iota(jnp.int32, (1, LANES), 1)

    tq = q_ref.shape[0] // k_ref.shape[0]

    def operands(seq, p):
        sl = slice(p * LANES, (p + 1) * LANES)
        q = q_ref[seq * tq:(seq + 1) * tq, sl]
        k = k_ref[seq, :, sl]
        v = v_ref[seq, :, sl]
        if has_ctx:
            q = _rope(q.astype(F32), cq_ref[...], sq_ref[...])
            k = _rope(k, ck_ref[...], sk_ref[...])
            k = jnp.concatenate([kc_ref[0, :, sl], k], axis=0)
            v = jnp.concatenate([vc_ref[0, :, sl], v], axis=0)
        vt = v.T
        ones = jnp.ones((ONES_ROWS, v.shape[0]), F32)
        vts = [jnp.concatenate([vt[h * HEAD_DIM:(h + 1) * HEAD_DIM], ones], axis=0).astype(BF16)
               for h in range(2)]
        return q.astype(BF16), k.astype(BF16), vts

    n_keys = k_ref.shape[1] + (kc_ref.shape[1] if has_ctx else 0)
    blocks = range(0, n_keys, KEY_BLOCK)
    ops, tiles = {}, []
    for seq in range(k_ref.shape[0]):
        for p in range(n_pair):
            for j in blocks:
                tiles += [(seq, p, h, m, j) for h in range(2) for m in range(2)]

    def score(tile):
        seq, p, h, m, j = tile
        if (seq, p) not in ops:
            ops[seq, p] = operands(seq, p)
        q, k, _ = ops[seq, p]
        lo = h * HEAD_DIM + m * D_QK
        sel = (lane >= lo) & (lane < lo + D_QK)
        return _bdot_nt(k[j:j + KEY_BLOCK], jnp.where(sel, q, jnp.zeros_like(q)))

    run_max, acc = {}, {}

    def consume(tile, s):
        seq, p, h, m, j = tile
        vt = ops[seq, p][2][h][:, j:j + KEY_BLOCK]
        c = (seq, p, h, m)
        mj = jnp.max(s, axis=0, keepdims=True)
        if j == 0:
            run_max[c] = mj
            acc[c] = jnp.dot(vt, jnp.exp2(s - mj).astype(BF16), preferred_element_type=F32)
        else:
            m_new = jnp.maximum(run_max[c], mj)
            acc[c] = (acc[c] * jnp.exp2(run_max[c] - m_new)
                      + jnp.dot(vt, jnp.exp2(s - m_new).astype(BF16), preferred_element_type=F32))
            run_max[c] = m_new
        if j == blocks[-1] and (h, m) == (1, 1):
            finish(seq, p)

    def finish(seq, p):
        halves = []
        for h in range(2):
            a0, a1 = acc.pop((seq, p, h, 0)), acc.pop((seq, p, h, 1))
            o = (a0[:HEAD_DIM] * (1.0 / a0[HEAD_DIM:HEAD_DIM + 1])
                 - lam * (a1[:HEAD_DIM] * (1.0 / a1[HEAD_DIM:HEAD_DIM + 1])))
            ms = jnp.mean(o * o, axis=0, keepdims=True)
            halves.append(o * lax.rsqrt(ms + SUBLN_EPS))
        o_ref[seq * tq:(seq + 1) * tq, p * LANES:(p + 1) * LANES] = (
            jnp.concatenate(halves, axis=0).T * sg_ref[...] * (1.0 - lam_init)).astype(BF16)

    pending = []
    for tile in tiles:
        pending.append((tile, score(tile)))
        if len(pending) > ATTN_LOOKAHEAD:
            consume(*pending.pop(0))
    for item in pending:
        consume(*item)


def _attention(q, k_all, v_all, layer, n_sub, lam_init, lp, sg, ctx=None):
    n_tok = q.shape[0]
    n_seq, _, seq_len, _ = k_all.shape
    tq = ATTN_TQ
    nq = seq_len // tq
    assert n_sub == 1 or nq == 1
    kv_spec = pl.BlockSpec((n_sub, None, seq_len, D_DIFF), lambda b, i: (b, layer, 0, 0))
    in_specs = [pl.BlockSpec((n_sub * tq, D_DIFF), lambda b, i: (b * nq + i, 0)),
                kv_spec, kv_spec,
                pl.BlockSpec(lp.shape, lambda b, i: (0, 0)),
                pl.BlockSpec((1, LANES), lambda b, i: (0, 0))]
    args = [q, k_all, v_all, lp, sg]
    if ctx is not None:
        kc, vc, cos, sin = ctx
        past = kc.shape[1]
        in_specs += [pl.BlockSpec((1, past, D_DIFF), lambda b, i: (b, 0, 0)),
                     pl.BlockSpec((1, past, D_DIFF), lambda b, i: (b, 0, 0)),
                     pl.BlockSpec((tq, LANES), lambda b, i: (i, 0)),
                     pl.BlockSpec((tq, LANES), lambda b, i: (i, 0)),
                     pl.BlockSpec((seq_len, LANES), lambda b, i: (0, 0)),
                     pl.BlockSpec((seq_len, LANES), lambda b, i: (0, 0))]
        args += [kc, vc, cos, sin, cos, sin]
    return pl.pallas_call(
        functools.partial(_attn_body, ctx is not None, lam_init),
        grid=(n_seq // n_sub, nq),
        in_specs=in_specs,
        out_specs=pl.BlockSpec((n_sub * tq, D_DIFF), lambda b, i: (b * nq + i, 0)),
        out_shape=jax.ShapeDtypeStruct((n_tok, D_DIFF), BF16),
        compiler_params=_cp(2),
    )(*args)


def _rope_tables(seq_len):
    t = jnp.arange(seq_len)
    pos = jnp.stack([(t // GRID_W).astype(F32), (t % GRID_W).astype(F32)], axis=1)
    inv = 1.0 / (ROPE_BASE ** (jnp.arange(ROPE_PAIRS, dtype=F32) / ROPE_PAIRS))
    ang = pos[:, :, None] * inv
    d = np.arange(LANES) % D_QK
    axis = d // (2 * ROPE_PAIRS)
    second = (d % (2 * ROPE_PAIRS)) // ROPE_PAIRS
    idx = d % ROPE_PAIRS
    cos = jnp.cos(ang)[:, axis, idx]
    sin = jnp.sin(ang)[:, axis, idx] * jnp.asarray(np.where(second == 1, 1.0, -1.0), F32)
    return cos, sin


def _fnet_body(seq_len, x_ref, ct_ref, st_ref, cc_ref, sc_ref, o_ref):
    n_sub = x_ref.shape[0] // seq_len
    x = x_ref[...]
    xc = jnp.dot(x, cc_ref[...], preferred_element_type=F32)
    xs = jnp.dot(x, sc_ref[...], preferred_element_type=F32)
    wide = lambda a: jnp.concatenate([a[s * seq_len:(s + 1) * seq_len] for s in range(n_sub)], axis=1)
    y = _bdot(ct_ref[...], wide(xc)) - _bdot(st_ref[...], wide(xs))
    for s in range(n_sub):
        o_ref[s * seq_len:(s + 1) * seq_len, :] = y[:, s * D_FNET:(s + 1) * D_FNET].astype(BF16)


def _dft_consts(n, block=1):
    idx = np.arange(n)
    ang = 2.0 * np.pi * ((idx[:, None] * idx[None, :]) % n) / n
    return [jnp.asarray(np.kron(np.eye(block), m).astype(np.float32)).astype(BF16)
            for m in (np.cos(ang) / np.sqrt(n), np.sin(ang) / np.sqrt(n))]


def _fnet(u_f, seq_len, n_sub):
    n_tok = u_f.shape[0]
    rows = n_sub * seq_len
    consts = _dft_consts(seq_len) + _dft_consts(FNET_GROUP_DIM, FNET_GROUPS)
    const = lambda b: (0, 0)
    return pl.pallas_call(
        functools.partial(_fnet_body, seq_len),
        grid=(n_tok // rows,),
        in_specs=[pl.BlockSpec((rows, D_FNET), lambda b: (b, 0))]
        + [pl.BlockSpec(c.shape, const) for c in consts],
        out_specs=pl.BlockSpec((rows, D_FNET), lambda b: (b, 0)),
        out_shape=jax.ShapeDtypeStruct((n_tok, D_FNET), BF16),
        compiler_params=_cp(1),
    )(u_f, *consts)


def _ffn_body(final, yr_ref, yd_ref, yf_ref, x_ref, g1_ref, sh2_ref, sc2_ref, g2_ref, n2_ref, fg_ref,
              wo_ref, wi_ref, wf_ref, o_ref):
    part = x_ref.shape[0] // N_PARTS
    parts = [slice(j * part, (j + 1) * part) for j in range(N_PARTS)]
    y = [jnp.dot(jnp.concatenate([yr_ref[r, :], yd_ref[r, :], yf_ref[r, :]], axis=1), wo_ref[...],
                 preferred_element_type=F32) for r in parts]
    x = [x_ref[r, :] + g1_ref[0] * y[j] for j, r in enumerate(parts)]
    h = [(_rms(xj, n2_ref[...]) * (1.0 + sc2_ref[0]) + sh2_ref[0]).astype(BF16) for xj in x]
    z = [jnp.dot(hj, wi_ref[...], preferred_element_type=F32) for hj in h]
    act = [zj[:, :D_FF] * jax.nn.sigmoid(zj[:, :D_FF]) * zj[:, D_FF:] for zj in z]
    f = [_bdot(aj, wf_ref[...]) for aj in act]
    for j, r in enumerate(parts):
        xj = x[j] + g2_ref[0] * f[j]
        o_ref[r, :] = _rms(xj, fg_ref[...]) if final else xj


def _ffn(y_r, y_d, y_f, x, mod, layer, row_fn, n2, fg, wo, wi, wf, final, tm):
    n_tok = x.shape[0]
    row = lambda i: (i, 0)
    const = lambda i: (0, 0)
    return pl.pallas_call(
        functools.partial(_ffn_body, final),
        grid=(n_tok // tm,),
        in_specs=[pl.BlockSpec((tm, D_RWKV), row),
                  pl.BlockSpec((tm, D_DIFF), row),
                  pl.BlockSpec((tm, D_FNET), row),
                  pl.BlockSpec((tm, D_MODEL), row),
                  _mod_spec(layer, 2, row_fn),
                  _mod_spec(layer, 3, row_fn),
                  _mod_spec(layer, 4, row_fn),
                  _mod_spec(layer, 5, row_fn),
                  pl.BlockSpec((1, D_MODEL), const),
                  pl.BlockSpec((1, D_MODEL), const),
                  pl.BlockSpec((None,) + wo.shape[1:], lambda i: (layer, 0, 0), pipeline_mode=pl.Buffered(1)),
                  pl.BlockSpec((None,) + wi.shape[1:], lambda i: (layer, 0, 0), pipeline_mode=pl.Buffered(1)),
                  pl.BlockSpec((None,) + wf.shape[1:], lambda i: (layer, 0, 0), pipeline_mode=pl.Buffered(1))],
        out_specs=pl.BlockSpec((tm, D_MODEL), row),
        out_shape=jax.ShapeDtypeStruct((n_tok, D_MODEL), F32),
        compiler_params=_cp(1),
    )(y_r, y_d, y_f, x, mod, mod, mod, mod, n2.reshape(1, D_MODEL), fg.reshape(1, D_MODEL), wo, wi, wf)


def _block_diag_state(s):
    b = s.shape[0]
    s = s.reshape(b, 2, N_PAIR, 2, HEAD_DIM, HEAD_DIM)
    eye = jnp.eye(2, dtype=s.dtype)
    s = s[:, :, :, :, :, None, :] * eye[None, None, None, :, None, :, None]
    return s.reshape(b, 2, N_PAIR, LANES, LANES)


def kernel(x_prompt, x_sample, c, state_rwkv, cache_diff_k, cache_diff_v, c_ctx, norm1_g, norm2_g, final_norm_g, w_mod, b_mod, w_in, w_out, shift_mu, decay_w0, decay_up, iclr_a0, iclr_up, gate_up, k_k, k_a, r_k, lnx_g, lnx_b, diff_lambda, subln_g, w_ffn_in, w_ffn_out):
    p = dict(shift_mu=shift_mu, decay_w0=decay_w0, decay_up=decay_up, iclr_a0=iclr_a0, iclr_up=iclr_up,
             gate_up=gate_up, k_k=k_k, k_a=k_a, r_k=r_k, lnx_g=lnx_g, lnx_b=lnx_b)
    n_ctx, t_ctx, _ = x_prompt.shape
    n_dec, t_dec, _ = x_sample.shape
    past = cache_diff_k.shape[2]

    cond = jnp.concatenate([c_ctx[None, :], c, jnp.zeros((MOD_ROWS - 1 - n_dec, D_MODEL), F32)], axis=0)
    mod = _modulation(cond, w_mod, b_mod).reshape(DEPTH * MOD_ROWS, 1, 6 * D_MODEL)

    tm_ffn = FFN_TM
    ctx_plan = _stream_plan(n_ctx, t_ctx)
    dec_plan = _stream_plan(n_dec, t_dec)
    streams = [
        dict(x=x_prompt.reshape(n_ctx * t_ctx, D_MODEL), t=t_ctx, n=n_ctx, **ctx_plan,
             row_in=lambda i: 0, row_ffn=lambda i: 0),
        dict(x=x_sample.reshape(n_dec * t_dec, D_MODEL), t=t_dec, n=n_dec, **dec_plan,
             row_in=lambda i: 1 + i // (t_dec // dec_plan['tm_in']), row_ffn=lambda i: 1 + i // (t_dec // tm_ffn)),
    ]
    cos, sin = _rope_tables(t_dec)
    for st in streams:
        st['kv'] = st['states'] = None
    w_in_l = _w_in_rowmajor(w_in)
    wo = w_out.astype(BF16)
    wi = w_ffn_in.astype(BF16)
    wf = w_ffn_out.astype(BF16)
    for l in range(DEPTH):
        rw = _rwkv_weights(p, l)
        mu = jnp.concatenate([shift_mu[l], jnp.zeros((2, N_RWKV_PAD - N_RWKV_IN), F32)], axis=1)
        lam_init = 0.8 - 0.6 * math.exp(-0.3 * l)
        sg = jnp.tile(subln_g[l], 2).reshape(1, LANES)
        for si, st in enumerate(streams):
            u_r, q, k_all, v_all, u_f = _inproj(st['x'], mod, l, st['row_in'], norm1_g[l], w_in_l, mu,
                                                 st['kv'], st['t'], st['tm_in'])
            st['kv'] = (k_all, v_all)
            if si == 0:
                s0 = None
                attn_ctx = None
            else:
                s0 = state_rwkv[:, l].astype(F32)
                attn_ctx = (cache_diff_k[:, l].reshape(n_dec, past, D_DIFF).astype(F32),
                            cache_diff_v[:, l].reshape(n_dec, past, D_DIFF).astype(F32), cos, sin)
            y_r, st['states'] = _rwkv(u_r, s0, st['t'], st['n_sub'], rw, st['states'], l)
            y_d = _attention(q, k_all, v_all, l, st['n_attn'], lam_init, diff_lambda[l], sg, attn_ctx)
            y_f = _fnet(u_f, st['t'], st['n_fnet'])
            st['x'] = _ffn(y_r, y_d, y_f, st['x'], mod, l, st['row_ffn'], norm2_g[l], final_norm_g,
                           wo, wi, wf, l == DEPTH - 1, tm_ffn)
    y_prompt = streams[0]['x'].reshape(n_ctx, t_ctx, D_MODEL)
    y_sample = streams[1]['x'].reshape(n_dec, t_dec, D_MODEL)
    new_k = streams[0]['kv'][0].reshape(n_ctx, DEPTH, t_ctx, H_DIFF, 2, D_QK)
    new_v = streams[0]['kv'][1].reshape(n_ctx, DEPTH, t_ctx, H_DIFF, HEAD_DIM)
    return (y_prompt, y_sample, streams[0]['states'], new_k, new_v)
```

```python
import functools
import math

import numpy as np
import jax
import jax.numpy as jnp
from jax import lax
from jax.experimental import pallas as pl
from jax.experimental.pallas import tpu as pltpu

F32 = jnp.float32
BF16 = jnp.bfloat16

D_MODEL = 1024
DEPTH = 2
GRID_W = 64
HEAD_DIM = 64
D_RWKV = 384
H_RWKV = D_RWKV // HEAD_DIM
D_DIFF = 384
H_DIFF = D_DIFF // HEAD_DIM
D_QK = HEAD_DIM // 2
D_FNET = D_MODEL - D_RWKV - D_DIFF
FNET_GROUPS = 4
FNET_GROUP_DIM = D_FNET // FNET_GROUPS
LORA_W = 32
LORA_A = 32
LORA_G = 64
N_RWKV_IN = 3 * D_RWKV + 2 * LORA_W + 2 * LORA_A + LORA_G
N_DIFF_IN = 3 * D_DIFF
D_FF = ((8 * D_MODEL + 3 * 256 - 1) // (3 * 256)) * 256
ROPE_PAIRS = D_QK // 4
ROPE_BASE = 10000.0
RMS_EPS = 1e-6
GN_EPS = 64e-5
SUBLN_EPS = 1e-5
DECAY_SCALE = math.exp(-0.5)
Q_SCALE = D_QK ** -0.5 * math.log2(math.e)

LANES = 128
N_RWKV_PAD = 11 * LANES
D_IN_PAD = N_RWKV_PAD + N_DIFF_IN + D_FNET
N_PAIR = H_RWKV // 2
CHUNK = 64
N_PARTS = 2
SIDE_WORK_MAX_UNITS = 12
RWKV_BLK = 128
MOD_ROWS = 8
ONES_ROWS = 16
VMEM_LIMIT = 60 * 1024 * 1024

MOD_TN = 1536
INPROJ_TM = 512
FFN_TM = 512
ATTN_TQ = 256
ATTN_STEP_ROWS = 1024
KEY_BLOCK = 512
ATTN_LOOKAHEAD = 8
FNET_STEP_ROWS = 1024
RWKV_STEP_ROWS = 1024
RWKV_ONE_STEP_ROWS = 2048


def _stream_plan(n_seq, seq_len):
    one_step = n_seq * seq_len <= RWKV_ONE_STEP_ROWS
    return dict(n_sub=n_seq if one_step else max(1, RWKV_STEP_ROWS // seq_len),
                n_attn=max(1, ATTN_STEP_ROWS // seq_len),
                n_fnet=max(1, FNET_STEP_ROWS // seq_len),
                tm_in=max(seq_len, INPROJ_TM))


def _cp(n_axes=1):
    return pltpu.CompilerParams(dimension_semantics=("arbitrary",) * n_axes,
                                vmem_limit_bytes=VMEM_LIMIT)


def _bdot(a, b):
    return jnp.dot(a.astype(BF16), b.astype(BF16), preferred_element_type=F32)


def _bdot_nt(a, b):
    return lax.dot_general(a.astype(BF16), b.astype(BF16), (((1,), (1,)), ((), ())),
                           preferred_element_type=F32)


def _split2(x):
    hi = x.astype(BF16)
    lo = (x - hi.astype(F32)).astype(BF16)
    return hi, lo


def _dot_x3(a, b):
    a_hi, a_lo = _split2(a)
    b_hi, b_lo = _split2(b)
    d = functools.partial(jnp.dot, preferred_element_type=F32)
    return d(a_hi, b_hi) + d(a_lo, b_hi) + d(a_hi, b_lo)


def _rms(x, g):
    return x * lax.rsqrt(jnp.mean(x * x, axis=-1, keepdims=True) + RMS_EPS) * g


def _w_in_body(x_ref, o_ref):
    col = lax.broadcasted_iota(jnp.int32, (1, N_RWKV_PAD), 1)
    head = jnp.where(col < N_RWKV_IN, x_ref[0:N_RWKV_PAD, :].T, 0.0)
    tail = x_ref[N_RWKV_IN:, :].T
    o_ref[:, :N_RWKV_PAD] = head.astype(BF16)
    o_ref[:, N_RWKV_PAD:] = tail.astype(BF16)


def _w_in_rowmajor(w_in):
    w_t = jnp.swapaxes(w_in, 1, 2)
    n_layers, n_out, _ = w_t.shape
    return pl.pallas_call(
        _w_in_body,
        grid=(n_layers,),
        in_specs=[pl.BlockSpec((None, n_out, D_MODEL), lambda l: (l, 0, 0))],
        out_specs=pl.BlockSpec((None, D_MODEL, D_IN_PAD), lambda l: (l, 0, 0)),
        out_shape=jax.ShapeDtypeStruct((n_layers, D_MODEL, D_IN_PAD), BF16),
        compiler_params=_cp(1),
    )(w_t)


def _mod_body(c_ref, w_ref, b_ref, o_ref):
    c = c_ref[...]
    a = c * jax.nn.sigmoid(c)
    o_ref[0] = _dot_x3(a, w_ref[0]) + b_ref[0]


def _modulation(cond, w_mod, b_mod):
    n_layers, _, n_out = w_mod.shape
    tn = MOD_TN
    return pl.pallas_call(
        _mod_body,
        grid=(n_layers, n_out // tn),
        in_specs=[pl.BlockSpec((MOD_ROWS, D_MODEL), lambda l, j: (0, 0)),
                  pl.BlockSpec((1, D_MODEL, tn), lambda l, j: (l, 0, j)),
                  pl.BlockSpec((1, 1, tn), lambda l, j: (l, 0, j))],
        out_specs=pl.BlockSpec((1, MOD_ROWS, tn), lambda l, j: (l, 0, j)),
        out_shape=jax.ShapeDtypeStruct((n_layers, MOD_ROWS, n_out), F32),
        compiler_params=_cp(2),
    )(cond, w_mod, b_mod.reshape(n_layers, 1, n_out))


def _mod_spec(layer, col, row_fn):
    return pl.BlockSpec((1, 1, D_MODEL), lambda i: (layer * MOD_ROWS + row_fn(i), 0, col))


def _put_layer(ref, idx, layer, aliased, val):
    if aliased:
        ref[idx] = val
    else:
        for other in range(DEPTH):
            ref[idx + (other,)] = val if other == layer else jnp.zeros_like(val)


def _inproj_body(seq_len, layer, aliased, x_ref, g_ref, sh_ref, sc_ref, w_ref, mu_ref, *refs):
    ur_ref, q_ref, k_ref, v_ref, uf_ref = refs[2:] if aliased else refs
    part = x_ref.shape[0] // N_PARTS
    parts = [slice(j * part, (j + 1) * part) for j in range(N_PARTS)]
    h = [(_rms(x_ref[r, :], g_ref[...]) * (1.0 + sc_ref[0]) + sh_ref[0]).astype(BF16) for r in parts]
    u = jnp.concatenate([jnp.dot(hj, w_ref[...], preferred_element_type=F32) for hj in h], axis=0)
    ur = u[:, :N_RWKV_PAD]
    tm = ur.shape[0]
    pos = lax.broadcasted_iota(jnp.int32, (tm, 1), 0) & (seq_len - 1)
    prev = jnp.where(pos == 0, 0.0, pltpu.roll(ur, 1, axis=0))
    nxt = jnp.where(pos == seq_len - 1, 0.0, pltpu.roll(ur, tm - 1, axis=0))
    ur_ref[...] = (ur + mu_ref[0:1, :] * (prev - ur) + mu_ref[1:2, :] * (nxt - ur)).astype(BF16)
    q_ref[...] = (u[:, N_RWKV_PAD:N_RWKV_PAD + D_DIFF] * Q_SCALE).astype(BF16)
    for s in range(tm // seq_len):
        rows = slice(s * seq_len, (s + 1) * seq_len)
        _put_layer(k_ref, (s,), layer, aliased, u[rows, N_RWKV_PAD + D_DIFF:N_RWKV_PAD + 2 * D_DIFF])
        _put_layer(v_ref, (s,), layer, aliased, u[rows, N_RWKV_PAD + 2 * D_DIFF:N_RWKV_PAD + 3 * D_DIFF])
    uf_ref[...] = u[:, N_RWKV_PAD + N_DIFF_IN:].astype(BF16)


def _inproj(x, mod, layer, row_fn, g, w, mu, kv, seq_len, tm):
    n_tok = x.shape[0]
    kv_shape = jax.ShapeDtypeStruct((n_tok // seq_len, DEPTH, seq_len, D_DIFF), F32)
    assert tm % seq_len == 0 and seq_len & (seq_len - 1) == 0
    row = lambda i: (i, 0)
    const = lambda i: (0, 0)
    if kv is None:
        kv_spec = pl.BlockSpec((tm // seq_len, DEPTH, seq_len, D_DIFF), lambda i: (i, 0, 0, 0))
    else:
        kv_spec = pl.BlockSpec((tm // seq_len, None, seq_len, D_DIFF), lambda i: (i, layer, 0, 0))
    return pl.pallas_call(
        functools.partial(_inproj_body, seq_len, layer, kv is not None),
        grid=(n_tok // tm,),
        in_specs=[pl.BlockSpec((tm, D_MODEL), row),
                  pl.BlockSpec((1, D_MODEL), const),
                  _mod_spec(layer, 0, row_fn),
                  _mod_spec(layer, 1, row_fn),
                  pl.BlockSpec((None, D_MODEL, D_IN_PAD), lambda i: (layer, 0, 0)),
                  pl.BlockSpec((2, N_RWKV_PAD), const)]
        + [pl.BlockSpec(memory_space=pl.ANY)] * (0 if kv is None else 2),
        out_specs=[pl.BlockSpec((tm, N_RWKV_PAD), row),
                   pl.BlockSpec((tm, D_DIFF), row),
                   kv_spec, kv_spec,
                   pl.BlockSpec((tm, D_FNET), row)],
        out_shape=[jax.ShapeDtypeStruct((n_tok, N_RWKV_PAD), BF16),
                   jax.ShapeDtypeStruct((n_tok, D_DIFF), BF16),
                   kv_shape, kv_shape,
                   jax.ShapeDtypeStruct((n_tok, D_FNET), BF16)],
        input_output_aliases={} if kv is None else {6: 2, 7: 3},
        compiler_params=_cp(1),
    )(x, g.reshape(1, D_MODEL), mod, mod, w, mu, *(kv or ()))


def _lane_masks():
    lane = lax.broadcasted_iota(jnp.int32, (1, LANES), 1)
    return lane < HEAD_DIM, lane >= HEAD_DIM


def _rwkv_units(units, m0, m1, side=()):
    side = list(side)

    def run_side(drain=False):
        for gen in list(side):
            for _ in gen:
                if not drain:
                    break
            else:
                side.remove(gen)

    def bd(x):
        xb = x.astype(BF16)
        zero = jnp.zeros_like(xb)
        return jnp.concatenate([jnp.where(m0, xb, zero), jnp.where(m1, xb, zero)], axis=0)

    row = lax.broadcasted_iota(jnp.int32, (CHUNK, LANES), 0)
    col = lax.broadcasted_iota(jnp.int32, (CHUNK, LANES), 1) & (CHUNK - 1)
    eye = (col == row).astype(F32)
    r2 = lax.broadcasted_iota(jnp.int32, (LANES, LANES), 0) < HEAD_DIM
    c2 = lax.broadcasted_iota(jnp.int32, (LANES, LANES), 1) < HEAD_DIM
    rng = range(len(units))

    pre = []
    for rev, kk, r, v, kd, b, cum, cex, tot, s_prev in units:
        p_inv = jnp.exp(-cum)
        p_rem = jnp.exp(tot - cum)
        ab = -kk * jnp.exp(cex)
        rb = r * jnp.exp(cum)
        strict = (col > row) if rev else (col < row)
        incl = (col >= row) if rev else (col <= row)
        pre.append(dict(ab=ab, rb=rb, vbd=bd(v), strict=strict, incl=incl,
                        lhs=jnp.concatenate([ab, rb], axis=0),
                        rhs=jnp.concatenate([bd(b * p_inv), bd(kd * p_inv)], axis=0),
                        bk=jnp.concatenate([b * p_rem, kd * p_rem], axis=0)))

    run_side()
    mm = [_bdot_nt(q['lhs'], q['rhs']) for q in pre]
    run_side()
    m_ab = [jnp.where(pre[i]['strict'], mm[i][:CHUNK, :LANES], 0.0) for i in rng]
    m_ak = [jnp.where(pre[i]['strict'], mm[i][:CHUNK, LANES:], 0.0) for i in rng]
    m_r = [jnp.concatenate([jnp.where(pre[i]['incl'], mm[i][CHUNK:, :LANES], 0.0),
                            jnp.where(pre[i]['incl'], mm[i][CHUNK:, LANES:], 0.0)], axis=1) for i in rng]
    mv = [_bdot(m_ak[i], pre[i]['vbd']) for i in rng]
    run_side()

    t = [eye + m_ab[i] for i in rng]
    n = [_bdot(m_ab[i], bd(m_ab[i])) for i in rng]
    for _ in range(4):
        x = [_bdot(jnp.concatenate([t[i], n[i]], axis=0), bd(n[i])) for i in rng]
        t = [t[i] + x[i][:CHUNK] for i in rng]
        n = [x[i][CHUNK:] for i in rng]
        run_side()
    t = [t[i] + _bdot(t[i], bd(n[i])) for i in rng]

    w = [_bdot(t[i], jnp.concatenate([bd(pre[i]['ab']), bd(mv[i])], axis=1)) for i in rng]
    xs = [_bdot_nt(jnp.concatenate([w[i][:, :LANES], pre[i]['rb']], axis=0), units[i][9]) for i in rng]
    u = [xs[i][:CHUNK] + w[i][:, LANES:] for i in rng]
    run_side(drain=True)
    y = [xs[i][CHUNK:] + _bdot(m_r[i], jnp.concatenate([bd(u[i]), pre[i]['vbd']], axis=0)) for i in rng]
    z = [_bdot(jnp.concatenate([u[i], units[i][3]], axis=0).T, pre[i]['bk']) for i in rng]
    s_new = [units[i][9] * jnp.exp(units[i][8]) + jnp.where(r2 == c2, z[i], 0.0) for i in rng]
    return y, s_new


def _rwkv_body(seq_len, n_sub, layer, has_s0, aliased, u_ref, *refs):
    s0_ref = refs[0] if has_s0 else None
    refs = refs[1:] if has_s0 else refs
    wdec_ref, wicl_ref, wg_ref, w0a0_ref, vec_ref, bo_ref, tril_ref, triu_ref = refs[:8]
    (y_ref, sfin_ref, r_s, v_s, kk_s, kd_s, b_s, ci_s, ce_s, y_s, st_s) = refs[9 if aliased else 8:]
    n_chunk = seq_len // CHUNK
    assert n_chunk % 2 == 0 and n_chunk >= 4
    k_k = vec_ref[0:1, :]
    k_a = vec_ref[1:2, :]
    r_k = vec_ref[2:3, :]
    lnx_g = vec_ref[3:4, :]
    lnx_b = vec_ref[4:5, :]
    bo = bo_ref[...]

    def headsum(xb):
        return jnp.concatenate([jnp.dot(xb[:, p * LANES:(p + 1) * LANES], bo, preferred_element_type=F32)
                                for p in range(N_PAIR)], axis=1)

    def prep(rows):
        xs = u_ref[rows, :].astype(F32)
        r = xs[:, 0:D_RWKV]
        k = xs[:, D_RWKV:2 * D_RWKV]
        v = xs[:, 2 * D_RWKV:3 * D_RWKV]
        lora = xs[:, 3 * D_RWKV:3 * D_RWKV + LANES]
        t_lora = jnp.tanh(lora).astype(BF16)
        lora = lora.astype(BF16)
        kk = k * k_k
        kk2 = (kk * kk).astype(BF16)
        yield
        dec = _bdot(t_lora, wdec_ref[...])
        icl = _bdot(lora, wicl_ref[...])
        ss = headsum(kk2)
        yield
        logw = -DECAY_SCALE * jax.nn.sigmoid(w0a0_ref[0:1, :] + dec)
        a = jax.nn.sigmoid(w0a0_ref[1:2, :] + icl)
        kk = kk / jnp.maximum(jnp.sqrt(ss), 1e-12)
        a_f = a[:, :D_RWKV]
        a_b = a[:, D_RWKV:]
        kka = k * k_a
        kd_f = (k - kka) + kka * a_f
        kd_b = (k - kka) + kka * a_b
        lws = [_split2(logw[:, d * D_RWKV:(d + 1) * D_RWKV]) for d in range(2)]
        r_s[rows, :] = r
        v_s[rows, :] = v
        kk_s[rows, :] = kk
        kd_s[0, rows, :] = kd_f
        kd_s[1, rows, :] = kd_b
        b_s[0, rows, :] = kk * a_f
        b_s[1, rows, :] = kk * a_b
        yield
        dd = functools.partial(jnp.dot, preferred_element_type=F32)
        cums = [dd(tri_ref[...], lws[d][0]) + dd(tri_ref[...], lws[d][1])
                for d, tri_ref in enumerate((tril_ref, triu_ref))]
        yield
        for d in range(2):
            ci_s[d, rows, :] = cums[d]
            ce_s[d, rows, :] = cums[d] - logw[:, d * D_RWKV:(d + 1) * D_RWKV]

    def post(rows):
        y = y_s[rows, :]
        yb = y.astype(BF16)
        v = v_s[rows, :]
        bon = _split2(r_s[rows, :] * (kd_s[0, rows, :] + kd_s[1, rows, :]) * r_k)
        gd = u_ref[rows, 3 * D_RWKV + LANES:3 * D_RWKV + 2 * LANES].astype(F32)
        s_gd = jax.nn.sigmoid(gd).astype(BF16)
        yield
        mean = headsum(yb) * (1.0 / HEAD_DIM)
        bonus = (headsum(bon[0]) + headsum(bon[1])) * v
        g = _bdot(s_gd, wg_ref[...])
        yield
        yc = y - mean
        yc2 = (yc * yc).astype(BF16)
        yield
        var = headsum(yc2) * (1.0 / HEAD_DIM)
        yield
        yn = yc * lax.rsqrt(var + GN_EPS) * lnx_g + lnx_b
        y_ref[rows, :] = ((yn + bonus) * g).astype(BF16)

    def run_all(gens):
        gens = list(gens)
        while gens:
            gens = [gen for gen in gens if next(gen, gens) is not gens]

    def chunk_rows(i):
        return [pl.ds(pl.multiple_of(s * seq_len + (i if d == 0 else n_chunk - 1 - i) * CHUNK, CHUNK), CHUNK)
                for s in range(n_sub) for d in range(2)]

    st_s[...] = s0_ref[...] if has_s0 else jnp.zeros_like(st_s)
    y_s[...] = jnp.zeros_like(y_s)
    m0, m1 = _lane_masks()

    def scan_step(i, prep_next, post_prev):
        rows_sd = chunk_rows(i)
        side = []
        if prep_next:
            side = [prep(rows) for rows in chunk_rows(i + 1)]
        if post_prev:
            side = [post(rows) for rows in chunk_rows(i - 1)]
        units = []
        for s in range(n_sub):
            for d in range(2):
                rows = rows_sd[s * 2 + d]
                cum = ci_s[d, rows, :]
                cex = ce_s[d, rows, :]
                tot = cum[CHUNK - 1:CHUNK] if d == 0 else cum[0:1]
                kk = kk_s[rows, :]
                r = r_s[rows, :]
                v = v_s[rows, :]
                kd = kd_s[d, rows, :]
                b = b_s[d, rows, :]
                for p in range(N_PAIR):
                    sl = slice(p * LANES, (p + 1) * LANES)
                    units.append((d == 1, kk[:, sl], r[:, sl], v[:, sl], kd[:, sl], b[:, sl],
                                  cum[:, sl], cex[:, sl], tot[:, sl], st_s[s, d, p]))
        ys, s_new = _rwkv_units(units, m0, m1, side)
        for s in range(n_sub):
            for d in range(2):
                base = (s * 2 + d) * N_PAIR
                for p in range(N_PAIR):
                    st_s[s, d, p] = s_new[base + p]
                rows = rows_sd[s * 2 + d]
                y_s[rows, :] = y_s[rows, :] + jnp.concatenate(ys[base:base + N_PAIR], axis=1)

    def loop(lo, hi, **kw):
        lax.fori_loop(lo, hi, lambda i, c: (scan_step(i, **kw), c)[1], 0)

    half = n_chunk // 2
    blk = tril_ref.shape[0]
    if blk == CHUNK:
        run_all(prep(rows) for rows in chunk_rows(0))
        loop(0, half - 1, prep_next=True, post_prev=False)
        loop(half - 1, half + 1, prep_next=False, post_prev=False)
        loop(half + 1, n_chunk, prep_next=False, post_prev=True)
        run_all(post(rows) for rows in chunk_rows(n_chunk - 1))
    else:
        blocks = [slice(r0, r0 + blk) for r0 in range(0, n_sub * seq_len, blk)]
        run_all(prep(rows) for rows in blocks)
        loop(0, n_chunk, prep_next=False, post_prev=False)
        run_all(post(rows) for rows in blocks)

    for s in range(n_sub):
        for d in range(2):
            for p in range(N_PAIR):
                st = st_s[s, d, p]
                for h in range(2):
                    rows = slice(h * HEAD_DIM, (h + 1) * HEAD_DIM)
                    val = st[rows, rows]
                    if aliased:
                        sfin_ref[s, d, 2 * p + h] = val
                    else:
                        for other in range(DEPTH):
                            sfin_ref[s, other, d, 2 * p + h] = val if other == layer else jnp.zeros_like(val)


def _rwkv(u_r, s0, seq_len, n_sub, wts, states, layer):
    n_seq = u_r.shape[0] // seq_len
    rows = n_sub * seq_len
    const2 = lambda b: (0, 0)
    st_shape = (n_sub, 2, N_PAIR, LANES, LANES)
    tok = pltpu.VMEM((rows, D_RWKV), F32)
    tok2 = pltpu.VMEM((2, rows, D_RWKV), F32)
    side_work = 2 * N_PAIR * n_sub <= SIDE_WORK_MAX_UNITS
    wts = list(wts) + _chunk_triangles(CHUNK if side_work else RWKV_BLK)
    single = n_seq == n_sub
    in_specs = [pl.BlockSpec((rows, N_RWKV_PAD), lambda b: (b, 0),
                             pipeline_mode=pl.Buffered(1) if single else None)]
    args = [u_r]
    if s0 is not None:
        in_specs.append(pl.BlockSpec(st_shape, lambda b: (b, 0, 0, 0, 0)))
        args.append(_block_diag_state(s0))
    in_specs += [pl.BlockSpec(w.shape, const2) for w in wts]
    args += list(wts)
    if states is not None:
        in_specs.append(pl.BlockSpec(memory_space=pl.ANY))
        args.append(states)
    return pl.pallas_call(
        functools.partial(_rwkv_body, seq_len, n_sub, layer, s0 is not None, states is not None),
        grid=(n_seq // n_sub,),
        in_specs=in_specs,
        out_specs=[pl.BlockSpec((rows, D_RWKV), lambda b: (b, 0),
                                pipeline_mode=pl.Buffered(1) if single else None),
                   pl.BlockSpec((n_sub, DEPTH, 2, H_RWKV, HEAD_DIM, HEAD_DIM), lambda b: (b, 0, 0, 0, 0, 0))
                   if states is None else
                   pl.BlockSpec((n_sub, None, 2, H_RWKV, HEAD_DIM, HEAD_DIM),
                                lambda b: (b, layer, 0, 0, 0, 0))],
        out_shape=[jax.ShapeDtypeStruct((n_seq * seq_len, D_RWKV), BF16),
                   jax.ShapeDtypeStruct((n_seq, DEPTH, 2, H_RWKV, HEAD_DIM, HEAD_DIM), F32)],
        scratch_shapes=[tok] * 3 + [tok2] * 4 + [tok, pltpu.VMEM(st_shape, F32)],
        input_output_aliases={} if states is None else {len(args) - 1: 1},
        compiler_params=_cp(1),
    )(*args)


def _rwkv_weights(p, l):
    z = functools.partial(jnp.zeros, dtype=F32)
    wdec = z((LANES, 2 * D_RWKV))
    wdec = wdec.at[0:LORA_W, :D_RWKV].set(p['decay_up'][l, 0])
    wdec = wdec.at[LORA_W:2 * LORA_W, D_RWKV:].set(p['decay_up'][l, 1])
    wicl = z((LANES, 2 * D_RWKV))
    wicl = wicl.at[2 * LORA_W:2 * LORA_W + LORA_A, :D_RWKV].set(p['iclr_up'][l, 0])
    wicl = wicl.at[2 * LORA_W + LORA_A:2 * LORA_W + 2 * LORA_A, D_RWKV:].set(p['iclr_up'][l, 1])
    wg = z((LANES, D_RWKV)).at[0:LORA_G].set(p['gate_up'][l])
    w0a0 = jnp.stack([p['decay_w0'][l].reshape(-1), p['iclr_a0'][l].reshape(-1)])
    vec = jnp.stack([p['k_k'][l], p['k_a'][l], p['r_k'][l].reshape(-1), p['lnx_g'][l], p['lnx_b'][l],
                     z((D_RWKV,)), z((D_RWKV,)), z((D_RWKV,))])
    head = np.arange(LANES) // HEAD_DIM
    bo = jnp.asarray(head[:, None] == head[None, :], BF16)
    return [wdec.astype(BF16), wicl.astype(BF16), wg.astype(BF16), w0a0, vec, bo]


def _chunk_triangles(rows):
    idx = np.arange(rows)
    same = idx[None, :] // CHUNK == idx[:, None] // CHUNK
    return [jnp.asarray(same & (idx[None, :] <= idx[:, None]), BF16),
            jnp.asarray(same & (idx[None, :] >= idx[:, None]), BF16)]


def _rope(x, cos, sin):
    lane = lax.broadcasted_iota(jnp.int32, (1, LANES), 1)
    first_half = (lane & ROPE_PAIRS) == 0
    partner = jnp.where(first_half, pltpu.roll(x, LANES - ROPE_PAIRS, axis=1),
                        pltpu.roll(x, ROPE_PAIRS, axis=1))
    return x * cos + partner * sin


def _attn_body(has_ctx, lam_init, *refs):
    if has_ctx:
        (q_ref, k_ref, v_ref, lp_ref, sg_ref, kc_ref, vc_ref, cq_ref, sq_ref, ck_ref, sk_ref,
         o_ref) = refs
    else:
        q_ref, k_ref, v_ref, lp_ref, sg_ref, o_ref = refs
    n_pair = D_DIFF // LANES
    lp = lp_ref[...]
    lam = (jnp.exp(jnp.sum(lp[0:1] * lp[1:2], axis=-1, keepdims=True))
           - jnp.exp(jnp.sum(lp[2:3] * lp[3:4], axis=-1, keepdims=True)) + lam_init)
    lane = lax.broadcasted_iota(jnp.int32, (1, LANES), 1)

    tq = q_ref.shape[0] // k_ref.shape[0]

    def operands(seq, p):
        sl = slice(p * LANES, (p + 1) * LANES)
        q = q_ref[seq * tq:(seq + 1) * tq, sl]
        k = k_ref[seq, :, sl]
        v = v_ref[seq, :, sl]
        if has_ctx:
            q = _rope(q.astype(F32), cq_ref[...], sq_ref[...])
            k = _rope(k, ck_ref[...], sk_ref[...])
            k = jnp.concatenate([kc_ref[0, :, sl], k], axis=0)
            v = jnp.concatenate([vc_ref[0, :, sl], v], axis=0)
        vt = v.T
        ones = jnp.ones((ONES_ROWS, v.shape[0]), F32)
        vts = [jnp.concatenate([vt[h * HEAD_DIM:(h + 1) * HEAD_DIM], ones], axis=0).astype(BF16)
               for h in range(2)]
        return q.astype(BF16), k.astype(BF16), vts

    n_keys = k_ref.shape[1] + (kc_ref.shape[1] if has_ctx else 0)
    blocks = range(0, n_keys, KEY_BLOCK)
    ops, tiles = {}, []
    for seq in range(k_ref.shape[0]):
        for p in range(n_pair):
            for j in blocks:
                tiles += [(seq, p, h, m, j) for h in range(2) for m in range(2)]

    def score(tile):
        seq, p, h, m, j = tile
        if (seq, p) not in ops:
            ops[seq, p] = operands(seq, p)
        q, k, _ = ops[seq, p]
        lo = h * HEAD_DIM + m * D_QK
        sel = (lane >= lo) & (lane < lo + D_QK)
        return _bdot_nt(k[j:j + KEY_BLOCK], jnp.where(sel, q, jnp.zeros_like(q)))

    run_max, acc = {}, {}

    def consume(tile, s):
        seq, p, h, m, j = tile
        vt = ops[seq, p][2][h][:, j:j + KEY_BLOCK]
        c = (seq, p, h, m)
        mj = jnp.max(s, axis=0, keepdims=True)
        if j == 0:
            run_max[c] = mj
            acc[c] = jnp.dot(vt, jnp.exp2(s - mj).astype(BF16), preferred_element_type=F32)
        else:
            m_new = jnp.maximum(run_max[c], mj)
            acc[c] = (acc[c] * jnp.exp2(run_max[c] - m_new)
                      + jnp.dot(vt, jnp.exp2(s - m_new).astype(BF16), preferred_element_type=F32))
            run_max[c] = m_new
        if j == blocks[-1] and (h, m) == (1, 1):
            finish(seq, p)

    def finish(seq, p):
        halves = []
        for h in range(2):
            a0, a1 = acc.pop((seq, p, h, 0)), acc.pop((seq, p, h, 1))
            o = (a0[:HEAD_DIM] * (1.0 / a0[HEAD_DIM:HEAD_DIM + 1])
                 - lam * (a1[:HEAD_DIM] * (1.0 / a1[HEAD_DIM:HEAD_DIM + 1])))
            ms = jnp.mean(o * o, axis=0, keepdims=True)
            halves.append(o * lax.rsqrt(ms + SUBLN_EPS))
        o_ref[seq * tq:(seq + 1) * tq, p * LANES:(p + 1) * LANES] = (
            jnp.concatenate(halves, axis=0).T * sg_ref[...] * (1.0 - lam_init)).astype(BF16)

    pending = []
    for tile in tiles:
        pending.append((tile, score(tile)))
        if len(pending) > ATTN_LOOKAHEAD:
            consume(*pending.pop(0))
    for item in pending:
        consume(*item)


def _attention(q, k_all, v_all, layer, n_sub, lam_init, lp, sg, ctx=None):
    n_tok = q.shape[0]
    n_seq, _, seq_len, _ = k_all.shape
    tq = ATTN_TQ
    nq = seq_len // tq
    assert n_sub == 1 or nq == 1
    kv_spec = pl.BlockSpec((n_sub, None, seq_len, D_DIFF), lambda b, i: (b, layer, 0, 0))
    in_specs = [pl.BlockSpec((n_sub * tq, D_DIFF), lambda b, i: (b * nq + i, 0)),
                kv_spec, kv_spec,
                pl.BlockSpec(lp.shape, lambda b, i: (0, 0)),
                pl.BlockSpec((1, LANES), lambda b, i: (0, 0))]
    args = [q, k_all, v_all, lp, sg]
    if ctx is not None:
        kc, vc, cos, sin = ctx
        past = kc.shape[2]
        in_specs += [pl.BlockSpec((1, None, past, D_DIFF), lambda b, i: (b, layer, 0, 0)),
                     pl.BlockSpec((1, None, past, D_DIFF), lambda b, i: (b, layer, 0, 0)),
                     pl.BlockSpec((tq, LANES), lambda b, i: (i, 0)),
                     pl.BlockSpec((tq, LANES), lambda b, i: (i, 0)),
                     pl.BlockSpec((seq_len, LANES), lambda b, i: (0, 0)),
                     pl.BlockSpec((seq_len, LANES), lambda b, i: (0, 0))]
        args += [kc, vc, cos, sin, cos, sin]
    return pl.pallas_call(
        functools.partial(_attn_body, ctx is not None, lam_init),
        grid=(n_seq // n_sub, nq),
        in_specs=in_specs,
        out_specs=pl.BlockSpec((n_sub * tq, D_DIFF), lambda b, i: (b * nq + i, 0)),
        out_shape=jax.ShapeDtypeStruct((n_tok, D_DIFF), BF16),
        compiler_params=_cp(2),
    )(*args)


def _rope_tables(seq_len):
    f32 = np.float32
    t = np.arange(seq_len)
    pos = np.stack([t // GRID_W, t % GRID_W], axis=1).astype(f32)
    inv = (f32(1.0) / (f32(ROPE_BASE) ** (np.arange(ROPE_PAIRS, dtype=f32) / f32(ROPE_PAIRS)))).astype(f32)
    ang = pos[:, :, None] * inv
    d = np.arange(LANES) % D_QK
    axis = d // (2 * ROPE_PAIRS)
    second = (d % (2 * ROPE_PAIRS)) // ROPE_PAIRS
    idx = d % ROPE_PAIRS
    cos = np.cos(ang)[:, axis, idx]
    sin = np.sin(ang)[:, axis, idx] * np.where(second == 1, 1.0, -1.0).astype(f32)
    return jnp.asarray(cos, F32), jnp.asarray(sin, F32)


def _fnet_body(seq_len, x_ref, ct_ref, st_ref, cc_ref, sc_ref, o_ref):
    n_sub = x_ref.shape[0] // seq_len
    x = x_ref[...]
    xc = jnp.dot(x, cc_ref[...], preferred_element_type=F32)
    xs = jnp.dot(x, sc_ref[...], preferred_element_type=F32)
    wide = lambda a: jnp.concatenate([a[s * seq_len:(s + 1) * seq_len] for s in range(n_sub)], axis=1)
    y = _bdot(ct_ref[...], wide(xc)) - _bdot(st_ref[...], wide(xs))
    for s in range(n_sub):
        o_ref[s * seq_len:(s + 1) * seq_len, :] = y[:, s * D_FNET:(s + 1) * D_FNET].astype(BF16)


def _dft_consts(n, block=1):
    idx = np.arange(n)
    ang = 2.0 * np.pi * ((idx[:, None] * idx[None, :]) % n) / n
    return [jnp.asarray(np.kron(np.eye(block), m).astype(np.float32)).astype(BF16)
            for m in (np.cos(ang) / np.sqrt(n), np.sin(ang) / np.sqrt(n))]


def _fnet(u_f, seq_len, n_sub):
    n_tok = u_f.shape[0]
    rows = n_sub * seq_len
    consts = _dft_consts(seq_len) + _dft_consts(FNET_GROUP_DIM, FNET_GROUPS)
    const = lambda b: (0, 0)
    return pl.pallas_call(
        functools.partial(_fnet_body, seq_len),
        grid=(n_tok // rows,),
        in_specs=[pl.BlockSpec((rows, D_FNET), lambda b: (b, 0))]
        + [pl.BlockSpec(c.shape, const) for c in consts],
        out_specs=pl.BlockSpec((rows, D_FNET), lambda b: (b, 0)),
        out_shape=jax.ShapeDtypeStruct((n_tok, D_FNET), BF16),
        compiler_params=_cp(1),
    )(u_f, *consts)


def _ffn_body(final, layer, yr_ref, yd_ref, yf_ref, x_ref, g1_ref, sh2_ref, sc2_ref, g2_ref, n2_ref, fg_ref,
              wo_ref, wi_hbm, wf_hbm, o_ref, wi_ref, wf_ref, sem):
    part = x_ref.shape[0] // N_PARTS
    parts = [slice(j * part, (j + 1) * part) for j in range(N_PARTS)]

    def tile(before_ffn_in=None, before_ffn_out=None):
        y = [jnp.dot(jnp.concatenate([yr_ref[r, :], yd_ref[r, :], yf_ref[r, :]], axis=1), wo_ref[...],
                     preferred_element_type=F32) for r in parts]
        x = [x_ref[r, :] + g1_ref[0] * y[j] for j, r in enumerate(parts)]
        h = [(_rms(xj, n2_ref[...]) * (1.0 + sc2_ref[0]) + sh2_ref[0]).astype(BF16) for xj in x]
        if before_ffn_in is not None:
            before_ffn_in()
        z = [jnp.dot(hj, wi_ref[...], preferred_element_type=F32) for hj in h]
        act = [zj[:, :D_FF] * jax.nn.sigmoid(zj[:, :D_FF]) * zj[:, D_FF:] for zj in z]
        if before_ffn_out is not None:
            before_ffn_out()
        f = [_bdot(aj, wf_ref[...]) for aj in act]
        for j, r in enumerate(parts):
            xj = x[j] + g2_ref[0] * f[j]
            o_ref[r, :] = _rms(xj, fg_ref[...]) if final else xj

    first = pl.program_id(0) == 0
    wi_copy = pltpu.make_async_copy(wi_hbm.at[layer], wi_ref, sem.at[0])
    wf_copy = pltpu.make_async_copy(wf_hbm.at[layer], wf_ref, sem.at[1])

    @pl.when(first)
    def _():
        wi_copy.start()
        wf_copy.start()
        tile(wi_copy.wait, wf_copy.wait)

    @pl.when(jnp.logical_not(first))
    def _():
        tile()


def _ffn(y_r, y_d, y_f, x, mod, layer, row_fn, n2, fg, wo, wi, wf, final, tm):
    n_tok = x.shape[0]
    row = lambda i: (i, 0)
    const = lambda i: (0, 0)
    return pl.pallas_call(
        functools.partial(_ffn_body, final, layer),
        grid=(n_tok // tm,),
        in_specs=[pl.BlockSpec((tm, D_RWKV), row),
                  pl.BlockSpec((tm, D_DIFF), row),
                  pl.BlockSpec((tm, D_FNET), row),
                  pl.BlockSpec((tm, D_MODEL), row),
                  _mod_spec(layer, 2, row_fn),
                  _mod_spec(layer, 3, row_fn),
                  _mod_spec(layer, 4, row_fn),
                  _mod_spec(layer, 5, row_fn),
                  pl.BlockSpec((1, D_MODEL), const),
                  pl.BlockSpec((1, D_MODEL), const),
                  pl.BlockSpec((None,) + wo.shape[1:], lambda i: (layer, 0, 0), pipeline_mode=pl.Buffered(1)),
                  pl.BlockSpec(memory_space=pl.ANY),
                  pl.BlockSpec(memory_space=pl.ANY)],
        out_specs=pl.BlockSpec((tm, D_MODEL), row),
        out_shape=jax.ShapeDtypeStruct((n_tok, D_MODEL), F32),
        scratch_shapes=[pltpu.VMEM(wi.shape[1:], BF16), pltpu.VMEM(wf.shape[1:], BF16),
                        pltpu.SemaphoreType.DMA((2,))],
        compiler_params=_cp(1),
    )(y_r, y_d, y_f, x, mod, mod, mod, mod, n2.reshape(1, D_MODEL), fg.reshape(1, D_MODEL), wo, wi, wf)


def _block_diag_state(s):
    b = s.shape[0]
    s = s.reshape(b, 2, N_PAIR, 2, HEAD_DIM, HEAD_DIM)
    eye = jnp.eye(2, dtype=s.dtype)
    s = s[:, :, :, :, :, None, :] * eye[None, None, None, :, None, :, None]
    return s.reshape(b, 2, N_PAIR, LANES, LANES)


def kernel(x_prompt, x_sample, c, state_rwkv, cache_diff_k, cache_diff_v, c_ctx, norm1_g, norm2_g, final_norm_g, w_mod, b_mod, w_in, w_out, shift_mu, decay_w0, decay_up, iclr_a0, iclr_up, gate_up, k_k, k_a, r_k, lnx_g, lnx_b, diff_lambda, subln_g, w_ffn_in, w_ffn_out):
    p = dict(shift_mu=shift_mu, decay_w0=decay_w0, decay_up=decay_up, iclr_a0=iclr_a0, iclr_up=iclr_up,
             gate_up=gate_up, k_k=k_k, k_a=k_a, r_k=r_k, lnx_g=lnx_g, lnx_b=lnx_b)
    n_ctx, t_ctx, _ = x_prompt.shape
    n_dec, t_dec, _ = x_sample.shape
    past = cache_diff_k.shape[2]

    cond = jnp.concatenate([c_ctx[None, :], c, jnp.zeros((MOD_ROWS - 1 - n_dec, D_MODEL), F32)], axis=0)
    mod = _modulation(cond, w_mod, b_mod).reshape(DEPTH * MOD_ROWS, 1, 6 * D_MODEL)

    tm_ffn = FFN_TM
    ctx_plan = _stream_plan(n_ctx, t_ctx)
    dec_plan = _stream_plan(n_dec, t_dec)
    streams = [
        dict(x=x_prompt.reshape(n_ctx * t_ctx, D_MODEL), t=t_ctx, n=n_ctx, **ctx_plan,
             row_in=lambda i: 0, row_ffn=lambda i: 0),
        dict(x=x_sample.reshape(n_dec * t_dec, D_MODEL), t=t_dec, n=n_dec, **dec_plan,
             row_in=lambda i: 1 + i // (t_dec // dec_plan['tm_in']), row_ffn=lambda i: 1 + i // (t_dec // tm_ffn)),
    ]
    cos, sin = _rope_tables(t_dec)
    cache_k = cache_diff_k.reshape(n_dec, DEPTH, past, D_DIFF).astype(F32)
    cache_v = cache_diff_v.reshape(n_dec, DEPTH, past, D_DIFF).astype(F32)
    for st in streams:
        st['kv'] = st['states'] = None
    w_in_l = _w_in_rowmajor(w_in)
    wo = w_out.astype(BF16)
    wi = w_ffn_in.astype(BF16)
    wf = w_ffn_out.astype(BF16)
    for l in range(DEPTH):
        rw = _rwkv_weights(p, l)
        mu = jnp.concatenate([shift_mu[l], jnp.zeros((2, N_RWKV_PAD - N_RWKV_IN), F32)], axis=1)
        lam_init = 0.8 - 0.6 * math.exp(-0.3 * l)
        sg = jnp.tile(subln_g[l], 2).reshape(1, LANES)
        for si, st in enumerate(streams):
            u_r, q, k_all, v_all, u_f = _inproj(st['x'], mod, l, st['row_in'], norm1_g[l], w_in_l, mu,
                                                 st['kv'], st['t'], st['tm_in'])
            st['kv'] = (k_all, v_all)
            if si == 0:
                s0 = None
                attn_ctx = None
            else:
                s0 = state_rwkv[:, l].astype(F32)
                attn_ctx = (cache_k, cache_v, cos, sin)
            y_r, st['states'] = _rwkv(u_r, s0, st['t'], st['n_sub'], rw, st['states'], l)
            y_d = _attention(q, k_all, v_all, l, st['n_attn'], lam_init, diff_lambda[l], sg, attn_ctx)
            y_f = _fnet(u_f, st['t'], st['n_fnet'])
            st['x'] = _ffn(y_r, y_d, y_f, st['x'], mod, l, st['row_ffn'], norm2_g[l], final_norm_g,
                           wo, wi, wf, l == DEPTH - 1, tm_ffn)
    y_prompt = streams[0]['x'].reshape(n_ctx, t_ctx, D_MODEL)
    y_sample = streams[1]['x'].reshape(n_dec, t_dec, D_MODEL)
    new_k = streams[0]['kv'][0].reshape(n_ctx, DEPTH, t_ctx, H_DIFF, 2, D_QK)
    new_v = streams[0]['kv'][1].reshape(n_ctx, DEPTH, t_ctx, H_DIFF, HEAD_DIM)
    return (y_prompt, y_sample, streams[0]['states'], new_k, new_v)
```

```python
import functools
import math

import numpy as np
import jax
import jax.numpy as jnp
from jax import lax
from jax.experimental import pallas as pl
from jax.experimental.pallas import tpu as pltpu

F32 = jnp.float32
BF16 = jnp.bfloat16

D_MODEL = 1024
DEPTH = 2
GRID_W = 64
HEAD_DIM = 64
D_RWKV = 384
H_RWKV = D_RWKV // HEAD_DIM
D_DIFF = 384
H_DIFF = D_DIFF // HEAD_DIM
D_QK = HEAD_DIM // 2
D_FNET = D_MODEL - D_RWKV - D_DIFF
FNET_GROUPS = 4
FNET_GROUP_DIM = D_FNET // FNET_GROUPS
LORA_W = 32
LORA_A = 32
LORA_G = 64
N_RWKV_IN = 3 * D_RWKV + 2 * LORA_W + 2 * LORA_A + LORA_G
N_DIFF_IN = 3 * D_DIFF
D_FF = ((8 * D_MODEL + 3 * 256 - 1) // (3 * 256)) * 256
ROPE_PAIRS = D_QK // 4
ROPE_BASE = 10000.0
RMS_EPS = 1e-6
GN_EPS = 64e-5
SUBLN_EPS = 1e-5
DECAY_SCALE = math.exp(-0.5)
Q_SCALE = D_QK ** -0.5 * math.log2(math.e)

LANES = 128
N_RWKV_PAD = 11 * LANES
D_IN_PAD = N_RWKV_PAD + N_DIFF_IN + D_FNET
N_PAIR = H_RWKV // 2
CHUNK = 64
N_PARTS = 2
SIDE_WORK_MAX_UNITS = 12
RWKV_BLK = 128
MOD_ROWS = 8
ONES_ROWS = 16
VMEM_LIMIT = 60 * 1024 * 1024

MOD_TN = 1536
INPROJ_TM = 512
FFN_TM = 512
ATTN_TQ = 256
ATTN_STEP_ROWS = 1024
KEY_BLOCK = 512
ATTN_LOOKAHEAD = 8
FNET_STEP_ROWS = 1024
RWKV_STEP_ROWS = 1024
RWKV_ONE_STEP_ROWS = 2048


def _stream_plan(n_seq, seq_len):
    one_step = n_seq * seq_len <= RWKV_ONE_STEP_ROWS
    return dict(n_sub=n_seq if one_step else max(1, RWKV_STEP_ROWS // seq_len),
                n_attn=max(1, ATTN_STEP_ROWS // seq_len),
                n_fnet=max(1, FNET_STEP_ROWS // seq_len),
                tm_in=max(seq_len, INPROJ_TM))


def _cp(n_axes=1):
    return pltpu.CompilerParams(dimension_semantics=("arbitrary",) * n_axes,
                                vmem_limit_bytes=VMEM_LIMIT)


def _bdot(a, b):
    return jnp.dot(a.astype(BF16), b.astype(BF16), preferred_element_type=F32)


def _bdot_nt(a, b):
    return lax.dot_general(a.astype(BF16), b.astype(BF16), (((1,), (1,)), ((), ())),
                           preferred_element_type=F32)


def _split2(x):
    hi = x.astype(BF16)
    lo = (x - hi.astype(F32)).astype(BF16)
    return hi, lo


def _dot_x3(a, b):
    a_hi, a_lo = _split2(a)
    b_hi, b_lo = _split2(b)
    d = functools.partial(jnp.dot, preferred_element_type=F32)
    return d(a_hi, b_hi) + d(a_lo, b_hi) + d(a_hi, b_lo)


def _rms(x, g):
    return x * lax.rsqrt(jnp.mean(x * x, axis=-1, keepdims=True) + RMS_EPS) * g


def _w_in_body(x_ref, o_ref):
    col = lax.broadcasted_iota(jnp.int32, (1, N_RWKV_PAD), 1)
    head = jnp.where(col < N_RWKV_IN, x_ref[0:N_RWKV_PAD, :].T, 0.0)
    tail = x_ref[N_RWKV_IN:, :].T
    o_ref[:, :N_RWKV_PAD] = head.astype(BF16)
    o_ref[:, N_RWKV_PAD:] = tail.astype(BF16)


def _w_in_rowmajor(w_in):
    w_t = jnp.swapaxes(w_in, 1, 2)
    n_layers, n_out, _ = w_t.shape
    return pl.pallas_call(
        _w_in_body,
        grid=(n_layers,),
        in_specs=[pl.BlockSpec((None, n_out, D_MODEL), lambda l: (l, 0, 0))],
        out_specs=pl.BlockSpec((None, D_MODEL, D_IN_PAD), lambda l: (l, 0, 0)),
        out_shape=jax.ShapeDtypeStruct((n_layers, D_MODEL, D_IN_PAD), BF16),
        compiler_params=_cp(1),
    )(w_t)


def _mod_body(c_ref, w_ref, b_ref, o_ref):
    c = c_ref[...]
    a = c * jax.nn.sigmoid(c)
    o_ref[0] = _dot_x3(a, w_ref[0]) + b_ref[0]


def _modulation(cond, w_mod, b_mod):
    n_layers, _, n_out = w_mod.shape
    tn = MOD_TN
    return pl.pallas_call(
        _mod_body,
        grid=(n_layers, n_out // tn),
        in_specs=[pl.BlockSpec((MOD_ROWS, D_MODEL), lambda l, j: (0, 0)),
                  pl.BlockSpec((1, D_MODEL, tn), lambda l, j: (l, 0, j)),
                  pl.BlockSpec((1, 1, tn), lambda l, j: (l, 0, j))],
        out_specs=pl.BlockSpec((1, MOD_ROWS, tn), lambda l, j: (l, 0, j)),
        out_shape=jax.ShapeDtypeStruct((n_layers, MOD_ROWS, n_out), F32),
        compiler_params=_cp(2),
    )(cond, w_mod, b_mod.reshape(n_layers, 1, n_out))


def _mod_spec(layer, col, row_fn):
    return pl.BlockSpec((1, 1, D_MODEL), lambda i: (layer * MOD_ROWS + row_fn(i), 0, col))


def _put_layer(ref, idx, layer, aliased, val):
    if aliased:
        ref[idx] = val
    else:
        for other in range(DEPTH):
            ref[idx + (other,)] = val if other == layer else jnp.zeros_like(val)


def _inproj_body(seq_len, layer, aliased, x_ref, g_ref, sh_ref, sc_ref, w_ref, mu_ref, *refs):
    ur_ref, q_ref, k_ref, v_ref, uf_ref = refs[2:] if aliased else refs
    part = x_ref.shape[0] // N_PARTS
    parts = [slice(j * part, (j + 1) * part) for j in range(N_PARTS)]
    h = [(_rms(x_ref[r, :], g_ref[...]) * (1.0 + sc_ref[0]) + sh_ref[0]).astype(BF16) for r in parts]
    u = jnp.concatenate([jnp.dot(hj, w_ref[...], preferred_element_type=F32) for hj in h], axis=0)
    ur = u[:, :N_RWKV_PAD]
    tm = ur.shape[0]
    pos = lax.broadcasted_iota(jnp.int32, (tm, 1), 0) & (seq_len - 1)
    prev = jnp.where(pos == 0, 0.0, pltpu.roll(ur, 1, axis=0))
    nxt = jnp.where(pos == seq_len - 1, 0.0, pltpu.roll(ur, tm - 1, axis=0))
    ur_ref[...] = (ur + mu_ref[0:1, :] * (prev - ur) + mu_ref[1:2, :] * (nxt - ur)).astype(BF16)
    q_ref[...] = (u[:, N_RWKV_PAD:N_RWKV_PAD + D_DIFF] * Q_SCALE).astype(BF16)
    for s in range(tm // seq_len):
        rows = slice(s * seq_len, (s + 1) * seq_len)
        _put_layer(k_ref, (s,), layer, aliased, u[rows, N_RWKV_PAD + D_DIFF:N_RWKV_PAD + 2 * D_DIFF])
        _put_layer(v_ref, (s,), layer, aliased, u[rows, N_RWKV_PAD + 2 * D_DIFF:N_RWKV_PAD + 3 * D_DIFF])
    uf_ref[...] = u[:, N_RWKV_PAD + N_DIFF_IN:].astype(BF16)


def _inproj(x, mod, layer, row_fn, g, w, mu, kv, seq_len, tm):
    n_tok = x.shape[0]
    kv_shape = jax.ShapeDtypeStruct((n_tok // seq_len, DEPTH, seq_len, D_DIFF), F32)
    assert tm % seq_len == 0 and seq_len & (seq_len - 1) == 0
    row = lambda i: (i, 0)
    const = lambda i: (0, 0)
    if kv is None:
        kv_spec = pl.BlockSpec((tm // seq_len, DEPTH, seq_len, D_DIFF), lambda i: (i, 0, 0, 0))
    else:
        kv_spec = pl.BlockSpec((tm // seq_len, None, seq_len, D_DIFF), lambda i: (i, layer, 0, 0))
    return pl.pallas_call(
        functools.partial(_inproj_body, seq_len, layer, kv is not None),
        grid=(n_tok // tm,),
        in_specs=[pl.BlockSpec((tm, D_MODEL), row),
                  pl.BlockSpec((1, D_MODEL), const),
                  _mod_spec(layer, 0, row_fn),
                  _mod_spec(layer, 1, row_fn),
                  pl.BlockSpec((None, D_MODEL, D_IN_PAD), lambda i: (layer, 0, 0)),
                  pl.BlockSpec((2, N_RWKV_PAD), const)]
        + [pl.BlockSpec(memory_space=pl.ANY)] * (0 if kv is None else 2),
        out_specs=[pl.BlockSpec((tm, N_RWKV_PAD), row),
                   pl.BlockSpec((tm, D_DIFF), row),
                   kv_spec, kv_spec,
                   pl.BlockSpec((tm, D_FNET), row)],
        out_shape=[jax.ShapeDtypeStruct((n_tok, N_RWKV_PAD), BF16),
                   jax.ShapeDtypeStruct((n_tok, D_DIFF), BF16),
                   kv_shape, kv_shape,
                   jax.ShapeDtypeStruct((n_tok, D_FNET), BF16)],
        input_output_aliases={} if kv is None else {6: 2, 7: 3},
        compiler_params=_cp(1),
    )(x, g.reshape(1, D_MODEL), mod, mod, w, mu, *(kv or ()))


def _lane_masks():
    lane = lax.broadcasted_iota(jnp.int32, (1, LANES), 1)
    return lane < HEAD_DIM, lane >= HEAD_DIM


def _rwkv_units(units, m0, m1, side=()):
    side = list(side)

    def run_side(drain=False):
        for gen in list(side):
            for _ in gen:
                if not drain:
                    break
            else:
                side.remove(gen)

    def bd(x):
        xb = x.astype(BF16)
        zero = jnp.zeros_like(xb)
        return jnp.concatenate([jnp.where(m0, xb, zero), jnp.where(m1, xb, zero)], axis=0)

    row = lax.broadcasted_iota(jnp.int32, (CHUNK, LANES), 0)
    col = lax.broadcasted_iota(jnp.int32, (CHUNK, LANES), 1) & (CHUNK - 1)
    eye = (col == row).astype(F32)
    r2 = lax.broadcasted_iota(jnp.int32, (LANES, LANES), 0) < HEAD_DIM
    c2 = lax.broadcasted_iota(jnp.int32, (LANES, LANES), 1) < HEAD_DIM
    rng = range(len(units))

    pre = []
    for rev, kk, r, v, kd, b, cum, cex, tot, s_prev in units:
        p_inv = jnp.exp(-cum)
        p_rem = jnp.exp(tot - cum)
        ab = -kk * jnp.exp(cex)
        rb = r * jnp.exp(cum)
        strict = (col > row) if rev else (col < row)
        incl = (col >= row) if rev else (col <= row)
        pre.append(dict(ab=ab, rb=rb, vbd=bd(v), strict=strict, incl=incl,
                        lhs=jnp.concatenate([ab, rb], axis=0),
                        rhs=jnp.concatenate([bd(b * p_inv), bd(kd * p_inv)], axis=0),
                        bk=jnp.concatenate([b * p_rem, kd * p_rem], axis=0)))

    run_side()
    mm = [_bdot_nt(q['lhs'], q['rhs']) for q in pre]
    run_side()
    m_ab = [jnp.where(pre[i]['strict'], mm[i][:CHUNK, :LANES], 0.0) for i in rng]
    m_ak = [jnp.where(pre[i]['strict'], mm[i][:CHUNK, LANES:], 0.0) for i in rng]
    m_r = [jnp.concatenate([jnp.where(pre[i]['incl'], mm[i][CHUNK:, :LANES], 0.0),
                            jnp.where(pre[i]['incl'], mm[i][CHUNK:, LANES:], 0.0)], axis=1) for i in rng]
    mv = [_bdot(m_ak[i], pre[i]['vbd']) for i in rng]
    run_side()

    t = [eye + m_ab[i] for i in rng]
    n = [_bdot(m_ab[i], bd(m_ab[i])) for i in rng]
    for _ in range(4):
        x = [_bdot(jnp.concatenate([t[i], n[i]], axis=0), bd(n[i])) for i in rng]
        t = [t[i] + x[i][:CHUNK] for i in rng]
        n = [x[i][CHUNK:] for i in rng]
        run_side()
    t = [t[i] + _bdot(t[i], bd(n[i])) for i in rng]

    w = [_bdot(t[i], jnp.concatenate([bd(pre[i]['ab']), bd(mv[i])], axis=1)) for i in rng]
    xs = [_bdot_nt(jnp.concatenate([w[i][:, :LANES], pre[i]['rb']], axis=0), units[i][9]) for i in rng]
    u = [xs[i][:CHUNK] + w[i][:, LANES:] for i in rng]
    run_side(drain=True)
    y = [xs[i][CHUNK:] + _bdot(m_r[i], jnp.concatenate([bd(u[i]), pre[i]['vbd']], axis=0)) for i in rng]
    z = [_bdot(jnp.concatenate([u[i], units[i][3]], axis=0).T, pre[i]['bk']) for i in rng]
    s_new = [units[i][9] * jnp.exp(units[i][8]) + jnp.where(r2 == c2, z[i], 0.0) for i in rng]
    return y, s_new


def _rwkv_body(seq_len, n_sub, layer, has_s0, aliased, u_ref, *refs):
    s0_ref = refs[0] if has_s0 else None
    refs = refs[1:] if has_s0 else refs
    wdec_ref, wicl_ref, wg_ref, w0a0_ref, vec_ref, bo_ref, tril_ref, triu_ref = refs[:8]
    (y_ref, sfin_ref, r_s, v_s, kk_s, kd_s, b_s, ci_s, ce_s, y_s, st_s) = refs[9 if aliased else 8:]
    n_chunk = seq_len // CHUNK
    assert n_chunk % 2 == 0 and n_chunk >= 4
    k_k = vec_ref[0:1, :]
    k_a = vec_ref[1:2, :]
    r_k = vec_ref[2:3, :]
    lnx_g = vec_ref[3:4, :]
    lnx_b = vec_ref[4:5, :]
    bo = bo_ref[...]

    def headsum(xb):
        return jnp.concatenate([jnp.dot(xb[:, p * LANES:(p + 1) * LANES], bo, preferred_element_type=F32)
                                for p in range(N_PAIR)], axis=1)

    def prep(rows):
        xs = u_ref[rows, :].astype(F32)
        r = xs[:, 0:D_RWKV]
        k = xs[:, D_RWKV:2 * D_RWKV]
        v = xs[:, 2 * D_RWKV:3 * D_RWKV]
        lora = xs[:, 3 * D_RWKV:3 * D_RWKV + LANES]
        t_lora = jnp.tanh(lora).astype(BF16)
        lora = lora.astype(BF16)
        kk = k * k_k
        kk2 = (kk * kk).astype(BF16)
        yield
        dec = _bdot(t_lora, wdec_ref[...])
        icl = _bdot(lora, wicl_ref[...])
        ss = headsum(kk2)
        yield
        logw = -DECAY_SCALE * jax.nn.sigmoid(w0a0_ref[0:1, :] + dec)
        a = jax.nn.sigmoid(w0a0_ref[1:2, :] + icl)
        kk = kk / jnp.maximum(jnp.sqrt(ss), 1e-12)
        a_f = a[:, :D_RWKV]
        a_b = a[:, D_RWKV:]
        kka = k * k_a
        kd_f = (k - kka) + kka * a_f
        kd_b = (k - kka) + kka * a_b
        lws = [_split2(logw[:, d * D_RWKV:(d + 1) * D_RWKV]) for d in range(2)]
        r_s[rows, :] = r
        v_s[rows, :] = v
        kk_s[rows, :] = kk
        kd_s[0, rows, :] = kd_f
        kd_s[1, rows, :] = kd_b
        b_s[0, rows, :] = kk * a_f
        b_s[1, rows, :] = kk * a_b
        yield
        dd = functools.partial(jnp.dot, preferred_element_type=F32)
        cums = [dd(tri_ref[...], lws[d][0]) + dd(tri_ref[...], lws[d][1])
                for d, tri_ref in enumerate((tril_ref, triu_ref))]
        yield
        for d in range(2):
            ci_s[d, rows, :] = cums[d]
            ce_s[d, rows, :] = cums[d] - logw[:, d * D_RWKV:(d + 1) * D_RWKV]

    def post(rows):
        y = y_s[rows, :]
        yb = y.astype(BF16)
        v = v_s[rows, :]
        bon = _split2(r_s[rows, :] * (kd_s[0, rows, :] + kd_s[1, rows, :]) * r_k)
        gd = u_ref[rows, 3 * D_RWKV + LANES:3 * D_RWKV + 2 * LANES].astype(F32)
        s_gd = jax.nn.sigmoid(gd).astype(BF16)
        yield
        mean = headsum(yb) * (1.0 / HEAD_DIM)
        bonus = (headsum(bon[0]) + headsum(bon[1])) * v
        g = _bdot(s_gd, wg_ref[...])
        yield
        yc = y - mean
        yc2 = (yc * yc).astype(BF16)
        yield
        var = headsum(yc2) * (1.0 / HEAD_DIM)
        yield
        yn = yc * lax.rsqrt(var + GN_EPS) * lnx_g + lnx_b
        y_ref[rows, :] = ((yn + bonus) * g).astype(BF16)

    def run_all(gens):
        gens = list(gens)
        while gens:
            gens = [gen for gen in gens if next(gen, gens) is not gens]

    def chunk_rows(i):
        return [pl.ds(pl.multiple_of(s * seq_len + (i if d == 0 else n_chunk - 1 - i) * CHUNK, CHUNK), CHUNK)
                for s in range(n_sub) for d in range(2)]

    st_s[...] = s0_ref[...] if has_s0 else jnp.zeros_like(st_s)
    y_s[...] = jnp.zeros_like(y_s)
    m0, m1 = _lane_masks()

    def scan_step(i, prep_next, post_prev):
        rows_sd = chunk_rows(i)
        side = []
        if prep_next:
            side = [prep(rows) for rows in chunk_rows(i + 1)]
        if post_prev:
            side = [post(rows) for rows in chunk_rows(i - 1)]
        units = []
        for s in range(n_sub):
            for d in range(2):
                rows = rows_sd[s * 2 + d]
                cum = ci_s[d, rows, :]
                cex = ce_s[d, rows, :]
                tot = cum[CHUNK - 1:CHUNK] if d == 0 else cum[0:1]
                kk = kk_s[rows, :]
                r = r_s[rows, :]
                v = v_s[rows, :]
                kd = kd_s[d, rows, :]
                b = b_s[d, rows, :]
                for p in range(N_PAIR):
                    sl = slice(p * LANES, (p + 1) * LANES)
                    units.append((d == 1, kk[:, sl], r[:, sl], v[:, sl], kd[:, sl], b[:, sl],
                                  cum[:, sl], cex[:, sl], tot[:, sl], st_s[s, d, p]))
        ys, s_new = _rwkv_units(units, m0, m1, side)
        for s in range(n_sub):
            for d in range(2):
                base = (s * 2 + d) * N_PAIR
                for p in range(N_PAIR):
                    st_s[s, d, p] = s_new[base + p]
                rows = rows_sd[s * 2 + d]
                y_s[rows, :] = y_s[rows, :] + jnp.concatenate(ys[base:base + N_PAIR], axis=1)

    def loop(lo, hi, **kw):
        lax.fori_loop(lo, hi, lambda i, c: (scan_step(i, **kw), c)[1], 0)

    half = n_chunk // 2
    blk = tril_ref.shape[0]
    if blk == CHUNK:
        run_all(prep(rows) for rows in chunk_rows(0))
        loop(0, half - 1, prep_next=True, post_prev=False)
        loop(half - 1, half + 1, prep_next=False, post_prev=False)
        loop(half + 1, n_chunk, prep_next=False, post_prev=True)
        run_all(post(rows) for rows in chunk_rows(n_chunk - 1))
    else:
        blocks = [slice(r0, r0 + blk) for r0 in range(0, n_sub * seq_len, blk)]
        run_all(prep(rows) for rows in blocks)
        loop(0, n_chunk, prep_next=False, post_prev=False)
        run_all(post(rows) for rows in blocks)

    for s in range(n_sub):
        for d in range(2):
            for p in range(N_PAIR):
                st = st_s[s, d, p]
                for h in range(2):
                    rows = slice(h * HEAD_DIM, (h + 1) * HEAD_DIM)
                    val = st[rows, rows]
                    if aliased:
                        sfin_ref[s, d, 2 * p + h] = val
                    else:
                        for other in range(DEPTH):
                            sfin_ref[s, other, d, 2 * p + h] = val if other == layer else jnp.zeros_like(val)


def _rwkv(u_r, s0, seq_len, n_sub, wts, states, layer):
    n_seq = u_r.shape[0] // seq_len
    rows = n_sub * seq_len
    const2 = lambda b: (0, 0)
    st_shape = (n_sub, 2, N_PAIR, LANES, LANES)
    tok = pltpu.VMEM((rows, D_RWKV), F32)
    tok2 = pltpu.VMEM((2, rows, D_RWKV), F32)
    side_work = 2 * N_PAIR * n_sub <= SIDE_WORK_MAX_UNITS
    wts = list(wts) + _chunk_triangles(CHUNK if side_work else RWKV_BLK)
    single = n_seq == n_sub
    in_specs = [pl.BlockSpec((rows, N_RWKV_PAD), lambda b: (b, 0),
                             pipeline_mode=pl.Buffered(1) if single else None)]
    args = [u_r]
    if s0 is not None:
        in_specs.append(pl.BlockSpec(st_shape, lambda b: (b, 0, 0, 0, 0)))
        args.append(_block_diag_state(s0))
    in_specs += [pl.BlockSpec(w.shape, const2) for w in wts]
    args += list(wts)
    if states is not None:
        in_specs.append(pl.BlockSpec(memory_space=pl.ANY))
        args.append(states)
    return pl.pallas_call(
        functools.partial(_rwkv_body, seq_len, n_sub, layer, s0 is not None, states is not None),
        grid=(n_seq // n_sub,),
        in_specs=in_specs,
        out_specs=[pl.BlockSpec((rows, D_RWKV), lambda b: (b, 0),
                                pipeline_mode=pl.Buffered(1) if single else None),
                   pl.BlockSpec((n_sub, DEPTH, 2, H_RWKV, HEAD_DIM, HEAD_DIM), lambda b: (b, 0, 0, 0, 0, 0))
                   if states is None else
                   pl.BlockSpec((n_sub, None, 2, H_RWKV, HEAD_DIM, HEAD_DIM),
                                lambda b: (b, layer, 0, 0, 0, 0))],
        out_shape=[jax.ShapeDtypeStruct((n_seq * seq_len, D_RWKV), BF16),
                   jax.ShapeDtypeStruct((n_seq, DEPTH, 2, H_RWKV, HEAD_DIM, HEAD_DIM), F32)],
        scratch_shapes=[tok] * 3 + [tok2] * 4 + [tok, pltpu.VMEM(st_shape, F32)],
        input_output_aliases={} if states is None else {len(args) - 1: 1},
        compiler_params=_cp(1),
    )(*args)


def _rwkv_weights(p, l):
    z = functools.partial(jnp.zeros, dtype=F32)
    wdec = z((LANES, 2 * D_RWKV))
    wdec = wdec.at[0:LORA_W, :D_RWKV].set(p['decay_up'][l, 0])
    wdec = wdec.at[LORA_W:2 * LORA_W, D_RWKV:].set(p['decay_up'][l, 1])
    wicl = z((LANES, 2 * D_RWKV))
    wicl = wicl.at[2 * LORA_W:2 * LORA_W + LORA_A, :D_RWKV].set(p['iclr_up'][l, 0])
    wicl = wicl.at[2 * LORA_W + LORA_A:2 * LORA_W + 2 * LORA_A, D_RWKV:].set(p['iclr_up'][l, 1])
    wg = z((LANES, D_RWKV)).at[0:LORA_G].set(p['gate_up'][l])
    w0a0 = jnp.stack([p['decay_w0'][l].reshape(-1), p['iclr_a0'][l].reshape(-1)])
    vec = jnp.stack([p['k_k'][l], p['k_a'][l], p['r_k'][l].reshape(-1), p['lnx_g'][l], p['lnx_b'][l],
                     z((D_RWKV,)), z((D_RWKV,)), z((D_RWKV,))])
    head = np.arange(LANES) // HEAD_DIM
    bo = jnp.asarray(head[:, None] == head[None, :], BF16)
    return [wdec.astype(BF16), wicl.astype(BF16), wg.astype(BF16), w0a0, vec, bo]


def _chunk_triangles(rows):
    idx = np.arange(rows)
    same = idx[None, :] // CHUNK == idx[:, None] // CHUNK
    return [jnp.asarray(same & (idx[None, :] <= idx[:, None]), BF16),
            jnp.asarray(same & (idx[None, :] >= idx[:, None]), BF16)]


def _rope(x, cos, sin):
    lane = lax.broadcasted_iota(jnp.int32, (1, LANES), 1)
    first_half = (lane & ROPE_PAIRS) == 0
    partner = jnp.where(first_half, pltpu.roll(x, LANES - ROPE_PAIRS, axis=1),
                        pltpu.roll(x, ROPE_PAIRS, axis=1))
    return x * cos + partner * sin


def _attn_body(has_ctx, lam_init, *refs):
    if has_ctx:
        (q_ref, k_ref, v_ref, lp_ref, sg_ref, kc_ref, vc_ref, cq_ref, sq_ref, ck_ref, sk_ref,
         o_ref) = refs
    else:
        q_ref, k_ref, v_ref, lp_ref, sg_ref, o_ref = refs
    n_pair = D_DIFF // LANES
    lp = lp_ref[...]
    lam = (jnp.exp(jnp.sum(lp[0:1] * lp[1:2], axis=-1, keepdims=True))
           - jnp.exp(jnp.sum(lp[2:3] * lp[3:4], axis=-1, keepdims=True)) + lam_init)
    lane = lax.broadcasted_iota(jnp.int32, (1, LANES), 1)

    tq = q_ref.shape[0] // k_ref.shape[0]

    def operands(seq, p):
        sl = slice(p * LANES, (p + 1) * LANES)
        q = q_ref[seq * tq:(seq + 1) * tq, sl]
        k = k_ref[seq, :, sl]
        v = v_ref[seq, :, sl]
        if has_ctx:
            q = _rope(q.astype(F32), cq_ref[...], sq_ref[...])
            k = _rope(k, ck_ref[...], sk_ref[...])
            k = jnp.concatenate([kc_ref[0, :, sl], k], axis=0)
            v = jnp.concatenate([vc_ref[0, :, sl], v], axis=0)
        vt = v.T
        ones = jnp.ones((ONES_ROWS, v.shape[0]), F32)
        vts = [jnp.concatenate([vt[h * HEAD_DIM:(h + 1) * HEAD_DIM], ones], axis=0).astype(BF16)
               for h in range(2)]
        return q.astype(BF16), k.astype(BF16), vts

    n_keys = k_ref.shape[1] + (kc_ref.shape[1] if has_ctx else 0)
    blocks = range(0, n_keys, KEY_BLOCK)
    ops, tiles = {}, []
    for seq in range(k_ref.shape[0]):
        for p in range(n_pair):
            for j in blocks:
                tiles += [(seq, p, h, m, j) for h in range(2) for m in range(2)]

    def score(tile):
        seq, p, h, m, j = tile
        if (seq, p) not in ops:
            ops[seq, p] = operands(seq, p)
        q, k, _ = ops[seq, p]
        lo = h * HEAD_DIM + m * D_QK
        sel = (lane >= lo) & (lane < lo + D_QK)
        return _bdot_nt(k[j:j + KEY_BLOCK], jnp.where(sel, q, jnp.zeros_like(q)))

    run_max, acc = {}, {}

    def consume(tile, s):
        seq, p, h, m, j = tile
        vt = ops[seq, p][2][h][:, j:j + KEY_BLOCK]
        c = (seq, p, h, m)
        mj = jnp.max(s, axis=0, keepdims=True)
        if j == 0:
            run_max[c] = mj
            acc[c] = jnp.dot(vt, jnp.exp2(s - mj).astype(BF16), preferred_element_type=F32)
        else:
            m_new = jnp.maximum(run_max[c], mj)
            acc[c] = (acc[c] * jnp.exp2(run_max[c] - m_new)
                      + jnp.dot(vt, jnp.exp2(s - m_new).astype(BF16), preferred_element_type=F32))
            run_max[c] = m_new
        if j == blocks[-1] and (h, m) == (1, 1):
            finish(seq, p)

    def finish(seq, p):
        halves = []
        for h in range(2):
            a0, a1 = acc.pop((seq, p, h, 0)), acc.pop((seq, p, h, 1))
            o = (a0[:HEAD_DIM] * (1.0 / a0[HEAD_DIM:HEAD_DIM + 1])
                 - lam * (a1[:HEAD_DIM] * (1.0 / a1[HEAD_DIM:HEAD_DIM + 1])))
            ms = jnp.mean(o * o, axis=0, keepdims=True)
            halves.append(o * lax.rsqrt(ms + SUBLN_EPS))
        o_ref[seq * tq:(seq + 1) * tq, p * LANES:(p + 1) * LANES] = (
            jnp.concatenate(halves, axis=0).T * sg_ref[...] * (1.0 - lam_init)).astype(BF16)

    pending = []
    for tile in tiles:
        pending.append((tile, score(tile)))
        if len(pending) > ATTN_LOOKAHEAD:
            consume(*pending.pop(0))
    for item in pending:
        consume(*item)


def _attention(q, k_all, v_all, layer, n_sub, lam_init, lp, sg, ctx=None):
    n_tok = q.shape[0]
    n_seq, _, seq_len, _ = k_all.shape
    tq = ATTN_TQ
    nq = seq_len // tq
    assert n_sub == 1 or nq == 1
    kv_spec = pl.BlockSpec((n_sub, None, seq_len, D_DIFF), lambda b, i: (b, layer, 0, 0))
    in_specs = [pl.BlockSpec((n_sub * tq, D_DIFF), lambda b, i: (b * nq + i, 0)),
                kv_spec, kv_spec,
                pl.BlockSpec(lp.shape, lambda b, i: (0, 0)),
                pl.BlockSpec((1, LANES), lambda b, i: (0, 0))]
    args = [q, k_all, v_all, lp, sg]
    if ctx is not None:
        kc, vc, cos, sin = ctx
        past = kc.shape[2]
        in_specs += [pl.BlockSpec((1, None, past, D_DIFF), lambda b, i: (b, layer, 0, 0)),
                     pl.BlockSpec((1, None, past, D_DIFF), lambda b, i: (b, layer, 0, 0)),
                     pl.BlockSpec((tq, LANES), lambda b, i: (i, 0)),
                     pl.BlockSpec((tq, LANES), lambda b, i: (i, 0)),
                     pl.BlockSpec((seq_len, LANES), lambda b, i: (0, 0)),
                     pl.BlockSpec((seq_len, LANES), lambda b, i: (0, 0))]
        args += [kc, vc, cos, sin, cos, sin]
    return pl.pallas_call(
        functools.partial(_attn_body, ctx is not None, lam_init),
        grid=(n_seq // n_sub, nq),
        in_specs=in_specs,
        out_specs=pl.BlockSpec((n_sub * tq, D_DIFF), lambda b, i: (b * nq + i, 0)),
        out_shape=jax.ShapeDtypeStruct((n_tok, D_DIFF), BF16),
        compiler_params=_cp(2),
    )(*args)


def _rope_tables(seq_len):
    f32 = np.float32
    t = np.arange(seq_len)
    pos = np.stack([t // GRID_W, t % GRID_W], axis=1).astype(f32)
    inv = (f32(1.0) / (f32(ROPE_BASE) ** (np.arange(ROPE_PAIRS, dtype=f32) / f32(ROPE_PAIRS)))).astype(f32)
    ang = pos[:, :, None] * inv
    d = np.arange(LANES) % D_QK
    axis = d // (2 * ROPE_PAIRS)
    second = (d % (2 * ROPE_PAIRS)) // ROPE_PAIRS
    idx = d % ROPE_PAIRS
    cos = np.cos(ang)[:, axis, idx]
    sin = np.sin(ang)[:, axis, idx] * np.where(second == 1, 1.0, -1.0).astype(f32)
    return jnp.asarray(cos, F32), jnp.asarray(sin, F32)


def _fnet_body(seq_len, x_ref, ct_ref, st_ref, cc_ref, sc_ref, o_ref):
    n_sub = x_ref.shape[0] // seq_len
    x = x_ref[...]
    xc = jnp.dot(x, cc_ref[...], preferred_element_type=F32)
    xs = jnp.dot(x, sc_ref[...], preferred_element_type=F32)
    wide = lambda a: jnp.concatenate([a[s * seq_len:(s + 1) * seq_len] for s in range(n_sub)], axis=1)
    y = _bdot(ct_ref[...], wide(xc)) - _bdot(st_ref[...], wide(xs))
    for s in range(n_sub):
        o_ref[s * seq_len:(s + 1) * seq_len, :] = y[:, s * D_FNET:(s + 1) * D_FNET].astype(BF16)


def _dft_consts(n, block=1):
    idx = np.arange(n)
    ang = 2.0 * np.pi * ((idx[:, None] * idx[None, :]) % n) / n
    return [jnp.asarray(np.kron(np.eye(block), m).astype(np.float32)).astype(BF16)
            for m in (np.cos(ang) / np.sqrt(n), np.sin(ang) / np.sqrt(n))]


def _fnet(u_f, seq_len, n_sub):
    n_tok = u_f.shape[0]
    rows = n_sub * seq_len
    consts = _dft_consts(seq_len) + _dft_consts(FNET_GROUP_DIM, FNET_GROUPS)
    const = lambda b: (0, 0)
    return pl.pallas_call(
        functools.partial(_fnet_body, seq_len),
        grid=(n_tok // rows,),
        in_specs=[pl.BlockSpec((rows, D_FNET), lambda b: (b, 0))]
        + [pl.BlockSpec(c.shape, const) for c in consts],
        out_specs=pl.BlockSpec((rows, D_FNET), lambda b: (b, 0)),
        out_shape=jax.ShapeDtypeStruct((n_tok, D_FNET), BF16),
        compiler_params=_cp(1),
    )(u_f, *consts)


def _ffn_body(final, layer, yr_ref, yd_ref, yf_ref, x_ref, g1_ref, sh2_ref, sc2_ref, g2_ref, n2_ref, fg_ref,
              wo_ref, wi_hbm, wf_hbm, o_ref, wi_ref, wf_ref, sem):
    part = x_ref.shape[0] // N_PARTS
    parts = [slice(j * part, (j + 1) * part) for j in range(N_PARTS)]

    def tile(before_ffn_in=None, before_ffn_out=None):
        y = [jnp.dot(jnp.concatenate([yr_ref[r, :], yd_ref[r, :], yf_ref[r, :]], axis=1), wo_ref[...],
                     preferred_element_type=F32) for r in parts]
        x = [x_ref[r, :] + g1_ref[0] * y[j] for j, r in enumerate(parts)]
        h = [(_rms(xj, n2_ref[...]) * (1.0 + sc2_ref[0]) + sh2_ref[0]).astype(BF16) for xj in x]
        if before_ffn_in is not None:
            before_ffn_in()
        z = [jnp.dot(hj, wi_ref[...], preferred_element_type=F32) for hj in h]
        act = [zj[:, :D_FF] * jax.nn.sigmoid(zj[:, :D_FF]) * zj[:, D_FF:] for zj in z]
        if before_ffn_out is not None:
            before_ffn_out()
        f = [_bdot(aj, wf_ref[...]) for aj in act]
        for j, r in enumerate(parts):
            xj = x[j] + g2_ref[0] * f[j]
            o_ref[r, :] = _rms(xj, fg_ref[...]) if final else xj

    first = pl.program_id(0) == 0
    wi_copy = pltpu.make_async_copy(wi_hbm.at[layer], wi_ref, sem.at[0])
    wf_copy = pltpu.make_async_copy(wf_hbm.at[layer], wf_ref, sem.at[1])

    @pl.when(first)
    def _():
        wi_copy.start()
        wf_copy.start()
        tile(wi_copy.wait, wf_copy.wait)

    @pl.when(jnp.logical_not(first))
    def _():
        tile()


def _ffn(y_r, y_d, y_f, x, mod, layer, row_fn, n2, fg, wo, wi, wf, final, tm):
    n_tok = x.shape[0]
    row = lambda i: (i, 0)
    const = lambda i: (0, 0)
    return pl.pallas_call(
        functools.partial(_ffn_body, final, layer),
        grid=(n_tok // tm,),
        in_specs=[pl.BlockSpec((tm, D_RWKV), row),
                  pl.BlockSpec((tm, D_DIFF), row),
                  pl.BlockSpec((tm, D_FNET), row),
                  pl.BlockSpec((tm, D_MODEL), row),
                  _mod_spec(layer, 2, row_fn),
                  _mod_spec(layer, 3, row_fn),
                  _mod_spec(layer, 4, row_fn),
                  _mod_spec(layer, 5, row_fn),
                  pl.BlockSpec((1, D_MODEL), const),
                  pl.BlockSpec((1, D_MODEL), const),
                  pl.BlockSpec((None,) + wo.shape[1:], lambda i: (layer, 0, 0), pipeline_mode=pl.Buffered(1)),
                  pl.BlockSpec(memory_space=pl.ANY),
                  pl.BlockSpec(memory_space=pl.ANY)],
        out_specs=pl.BlockSpec((tm, D_MODEL), row),
        out_shape=jax.ShapeDtypeStruct((n_tok, D_MODEL), F32),
        scratch_shapes=[pltpu.VMEM(wi.shape[1:], BF16), pltpu.VMEM(wf.shape[1:], BF16),
                        pltpu.SemaphoreType.DMA((2,))],
        compiler_params=_cp(1),
    )(y_r, y_d, y_f, x, mod, mod, mod, mod, n2.reshape(1, D_MODEL), fg.reshape(1, D_MODEL), wo, wi, wf)


def _block_diag_state(s):
    b = s.shape[0]
    s = s.reshape(b, 2, N_PAIR, 2, HEAD_DIM, HEAD_DIM)
    eye = jnp.eye(2, dtype=s.dtype)
    s = s[:, :, :, :, :, None, :] * eye[None, None, None, :, None, :, None]
    return s.reshape(b, 2, N_PAIR, LANES, LANES)


def kernel(x_prompt, x_sample, c, state_rwkv, cache_diff_k, cache_diff_v, c_ctx, norm1_g, norm2_g, final_norm_g, w_mod, b_mod, w_in, w_out, shift_mu, decay_w0, decay_up, iclr_a0, iclr_up, gate_up, k_k, k_a, r_k, lnx_g, lnx_b, diff_lambda, subln_g, w_ffn_in, w_ffn_out):
    p = dict(shift_mu=shift_mu, decay_w0=decay_w0, decay_up=decay_up, iclr_a0=iclr_a0, iclr_up=iclr_up,
             gate_up=gate_up, k_k=k_k, k_a=k_a, r_k=r_k, lnx_g=lnx_g, lnx_b=lnx_b)
    n_ctx, t_ctx, _ = x_prompt.shape
    n_dec, t_dec, _ = x_sample.shape
    past = cache_diff_k.shape[2]

    cond = jnp.concatenate([c_ctx[None, :], c, jnp.zeros((MOD_ROWS - 1 - n_dec, D_MODEL), F32)], axis=0)
    mod = _modulation(cond, w_mod, b_mod).reshape(DEPTH * MOD_ROWS, 1, 6 * D_MODEL)

    tm_ffn = FFN_TM
    ctx_plan = _stream_plan(n_ctx, t_ctx)
    dec_plan = _stream_plan(n_dec, t_dec)
    streams = [
        dict(x=x_prompt.reshape(n_ctx * t_ctx, D_MODEL), t=t_ctx, n=n_ctx, **ctx_plan,
             row_in=lambda i: 0, row_ffn=lambda i: 0),
        dict(x=x_sample.reshape(n_dec * t_dec, D_MODEL), t=t_dec, n=n_dec, **dec_plan,
             row_in=lambda i: 1 + i // (t_dec // dec_plan['tm_in']), row_ffn=lambda i: 1 + i // (t_dec // tm_ffn)),
    ]
    cos, sin = _rope_tables(t_dec)
    cache_k = cache_diff_k.reshape(n_dec, DEPTH, past, D_DIFF).astype(F32)
    cache_v = cache_diff_v.reshape(n_dec, DEPTH, past, D_DIFF).astype(F32)
    for st in streams:
        st['kv'] = st['states'] = None
    w_in_l = _w_in_rowmajor(w_in)
    wo = w_out.astype(BF16)
    wi = w_ffn_in.astype(BF16)
    wf = w_ffn_out.astype(BF16)
    for l in range(DEPTH):
        rw = _rwkv_weights(p, l)
        mu = jnp.concatenate([shift_mu[l], jnp.zeros((2, N_RWKV_PAD - N_RWKV_IN), F32)], axis=1)
        lam_init = 0.8 - 0.6 * math.exp(-0.3 * l)
        sg = jnp.tile(subln_g[l], 2).reshape(1, LANES)
        for si, st in enumerate(streams):
            u_r, q, k_all, v_all, u_f = _inproj(st['x'], mod, l, st['row_in'], norm1_g[l], w_in_l, mu,
                                                 st['kv'], st['t'], st['tm_in'])
            st['kv'] = (k_all, v_all)
            if si == 0:
                s0 = None
                attn_ctx = None
            else:
                s0 = state_rwkv[:, l].astype(F32)
                attn_ctx = (cache_k, cache_v, cos, sin)
                q, streams[0]['x'] = lax.optimization_barrier((q, streams[0]['x']))
            y_r, st['states'] = _rwkv(u_r, s0, st['t'], st['n_sub'], rw, st['states'], l)
            y_d = _attention(q, k_all, v_all, l, st['n_attn'], lam_init, diff_lambda[l], sg, attn_ctx)
            y_f = _fnet(u_f, st['t'], st['n_fnet'])
            st['x'] = _ffn(y_r, y_d, y_f, st['x'], mod, l, st['row_ffn'], norm2_g[l], final_norm_g,
                           wo, wi, wf, l == DEPTH - 1, tm_ffn)
    y_prompt = streams[0]['x'].reshape(n_ctx, t_ctx, D_MODEL)
    y_sample = streams[1]['x'].reshape(n_dec, t_dec, D_MODEL)
    new_k = streams[0]['kv'][0].reshape(n_ctx, DEPTH, t_ctx, H_DIFF, 2, D_QK)
    new_v = streams[0]['kv'][1].reshape(n_ctx, DEPTH, t_ctx, H_DIFF, HEAD_DIM)
    return (y_prompt, y_sample, streams[0]['states'], new_k, new_v)
```

```python
import functools
import math

import numpy as np
import jax
import jax.numpy as jnp
from jax import lax
from jax.experimental import pallas as pl
from jax.experimental.pallas import tpu as pltpu

F32 = jnp.float32
BF16 = jnp.bfloat16

D_MODEL = 1024
DEPTH = 2
GRID_W = 64
HEAD_DIM = 64
D_RWKV = 384
H_RWKV = D_RWKV // HEAD_DIM
D_DIFF = 384
H_DIFF = D_DIFF // HEAD_DIM
D_QK = HEAD_DIM // 2
D_FNET = D_MODEL - D_RWKV - D_DIFF
FNET_GROUPS = 4
FNET_GROUP_DIM = D_FNET // FNET_GROUPS
LORA_W = 32
LORA_A = 32
LORA_G = 64
N_RWKV_IN = 3 * D_RWKV + 2 * LORA_W + 2 * LORA_A + LORA_G
N_DIFF_IN = 3 * D_DIFF
D_FF = ((8 * D_MODEL + 3 * 256 - 1) // (3 * 256)) * 256
ROPE_PAIRS = D_QK // 4
ROPE_BASE = 10000.0
RMS_EPS = 1e-6
GN_EPS = 64e-5
SUBLN_EPS = 1e-5
DECAY_SCALE = math.exp(-0.5)
Q_SCALE = D_QK ** -0.5 * math.log2(math.e)

LANES = 128
N_RWKV_PAD = 11 * LANES
D_IN_PAD = N_RWKV_PAD + N_DIFF_IN + D_FNET
N_PAIR = H_RWKV // 2
CHUNK = 64
N_PARTS = 2
SIDE_WORK_MAX_UNITS = 12
RWKV_BLK = 128
MOD_ROWS = 8
ONES_ROWS = 16
VMEM_LIMIT = 60 * 1024 * 1024

MOD_TN = 1536
INPROJ_TM = 512
FFN_TM = 512
ATTN_TQ = 256
ATTN_STEP_ROWS = 1024
KEY_BLOCK = 512
ATTN_LOOKAHEAD = 8
FNET_STEP_ROWS = 1024
RWKV_STEP_ROWS = 1024
RWKV_ONE_STEP_ROWS = 2048


def _stream_plan(n_seq, seq_len):
    one_step = n_seq * seq_len <= RWKV_ONE_STEP_ROWS
    return dict(n_sub=n_seq if one_step else max(1, RWKV_STEP_ROWS // seq_len),
                n_attn=max(1, ATTN_STEP_ROWS // seq_len),
                n_fnet=max(1, FNET_STEP_ROWS // seq_len),
                tm_in=max(seq_len, INPROJ_TM))


def _cp(n_axes=1):
    return pltpu.CompilerParams(dimension_semantics=("arbitrary",) * n_axes,
                                vmem_limit_bytes=VMEM_LIMIT)


def _bdot(a, b):
    return jnp.dot(a.astype(BF16), b.astype(BF16), preferred_element_type=F32)


def _bdot_nt(a, b):
    return lax.dot_general(a.astype(BF16), b.astype(BF16), (((1,), (1,)), ((), ())),
                           preferred_element_type=F32)


def _split2(x):
    hi = x.astype(BF16)
    lo = (x - hi.astype(F32)).astype(BF16)
    return hi, lo


def _dot_x3(a, b):
    a_hi, a_lo = _split2(a)
    b_hi, b_lo = _split2(b)
    d = functools.partial(jnp.dot, preferred_element_type=F32)
    return d(a_hi, b_hi) + d(a_lo, b_hi) + d(a_hi, b_lo)


def _rms(x, g):
    return x * lax.rsqrt(jnp.mean(x * x, axis=-1, keepdims=True) + RMS_EPS) * g


def _w_in_body(x_ref, o_ref):
    col = lax.broadcasted_iota(jnp.int32, (1, N_RWKV_PAD), 1)
    head = jnp.where(col < N_RWKV_IN, x_ref[0:N_RWKV_PAD, :].T, 0.0)
    tail = x_ref[N_RWKV_IN:, :].T
    o_ref[:, :N_RWKV_PAD] = head.astype(BF16)
    o_ref[:, N_RWKV_PAD:] = tail.astype(BF16)


def _w_in_rowmajor(w_in):
    w_t = jnp.swapaxes(w_in, 1, 2)
    n_layers, n_out, _ = w_t.shape
    return pl.pallas_call(
        _w_in_body,
        grid=(n_layers,),
        in_specs=[pl.BlockSpec((None, n_out, D_MODEL), lambda l: (l, 0, 0))],
        out_specs=pl.BlockSpec((None, D_MODEL, D_IN_PAD), lambda l: (l, 0, 0)),
        out_shape=jax.ShapeDtypeStruct((n_layers, D_MODEL, D_IN_PAD), BF16),
        compiler_params=_cp(1),
    )(w_t)


def _mod_body(c_ref, w_ref, b_ref, o_ref):
    c = c_ref[...]
    a = c * jax.nn.sigmoid(c)
    o_ref[0] = _dot_x3(a, w_ref[0]) + b_ref[0]


def _modulation(cond, w_mod, b_mod):
    n_layers, _, n_out = w_mod.shape
    tn = MOD_TN
    return pl.pallas_call(
        _mod_body,
        grid=(n_layers, n_out // tn),
        in_specs=[pl.BlockSpec((MOD_ROWS, D_MODEL), lambda l, j: (0, 0)),
                  pl.BlockSpec((1, D_MODEL, tn), lambda l, j: (l, 0, j)),
                  pl.BlockSpec((1, 1, tn), lambda l, j: (l, 0, j))],
        out_specs=pl.BlockSpec((1, MOD_ROWS, tn), lambda l, j: (l, 0, j)),
        out_shape=jax.ShapeDtypeStruct((n_layers, MOD_ROWS, n_out), F32),
        compiler_params=_cp(2),
    )(cond, w_mod, b_mod.reshape(n_layers, 1, n_out))


def _mod_spec(layer, col, row_fn):
    return pl.BlockSpec((1, 1, D_MODEL), lambda i: (layer * MOD_ROWS + row_fn(i), 0, col))


def _put_layer(ref, idx, layer, aliased, val):
    if aliased:
        ref[idx] = val
    else:
        for other in range(DEPTH):
            ref[idx + (other,)] = val if other == layer else jnp.zeros_like(val)


def _inproj_body(seq_len, layer, aliased, x_ref, g_ref, sh_ref, sc_ref, w_ref, mu_ref, *refs):
    ur_ref, q_ref, k_ref, v_ref, uf_ref = refs[2:] if aliased else refs
    part = x_ref.shape[0] // N_PARTS
    parts = [slice(j * part, (j + 1) * part) for j in range(N_PARTS)]
    h = [(_rms(x_ref[r, :], g_ref[...]) * (1.0 + sc_ref[0]) + sh_ref[0]).astype(BF16) for r in parts]
    u = jnp.concatenate([jnp.dot(hj, w_ref[...], preferred_element_type=F32) for hj in h], axis=0)
    ur = u[:, :N_RWKV_PAD]
    tm = ur.shape[0]
    pos = lax.broadcasted_iota(jnp.int32, (tm, 1), 0) & (seq_len - 1)
    prev = jnp.where(pos == 0, 0.0, pltpu.roll(ur, 1, axis=0))
    nxt = jnp.where(pos == seq_len - 1, 0.0, pltpu.roll(ur, tm - 1, axis=0))
    ur_ref[...] = (ur + mu_ref[0:1, :] * (prev - ur) + mu_ref[1:2, :] * (nxt - ur)).astype(BF16)
    q_ref[...] = (u[:, N_RWKV_PAD:N_RWKV_PAD + D_DIFF] * Q_SCALE).astype(BF16)
    for s in range(tm // seq_len):
        rows = slice(s * seq_len, (s + 1) * seq_len)
        _put_layer(k_ref, (s,), layer, aliased, u[rows, N_RWKV_PAD + D_DIFF:N_RWKV_PAD + 2 * D_DIFF])
        _put_layer(v_ref, (s,), layer, aliased, u[rows, N_RWKV_PAD + 2 * D_DIFF:N_RWKV_PAD + 3 * D_DIFF])
    uf_ref[...] = u[:, N_RWKV_PAD + N_DIFF_IN:].astype(BF16)


def _inproj(x, mod, layer, row_fn, g, w, mu, kv, seq_len, tm):
    n_tok = x.shape[0]
    kv_shape = jax.ShapeDtypeStruct((n_tok // seq_len, DEPTH, seq_len, D_DIFF), F32)
    assert tm % seq_len == 0 and seq_len & (seq_len - 1) == 0
    row = lambda i: (i, 0)
    const = lambda i: (0, 0)
    if kv is None:
        kv_spec = pl.BlockSpec((tm // seq_len, DEPTH, seq_len, D_DIFF), lambda i: (i, 0, 0, 0))
    else:
        kv_spec = pl.BlockSpec((tm // seq_len, None, seq_len, D_DIFF), lambda i: (i, layer, 0, 0))
    return pl.pallas_call(
        functools.partial(_inproj_body, seq_len, layer, kv is not None),
        grid=(n_tok // tm,),
        in_specs=[pl.BlockSpec((tm, D_MODEL), row),
                  pl.BlockSpec((1, D_MODEL), const),
                  _mod_spec(layer, 0, row_fn),
                  _mod_spec(layer, 1, row_fn),
                  pl.BlockSpec((None, D_MODEL, D_IN_PAD), lambda i: (layer, 0, 0)),
                  pl.BlockSpec((2, N_RWKV_PAD), const)]
        + [pl.BlockSpec(memory_space=pl.ANY)] * (0 if kv is None else 2),
        out_specs=[pl.BlockSpec((tm, N_RWKV_PAD), row),
                   pl.BlockSpec((tm, D_DIFF), row),
                   kv_spec, kv_spec,
                   pl.BlockSpec((tm, D_FNET), row)],
        out_shape=[jax.ShapeDtypeStruct((n_tok, N_RWKV_PAD), BF16),
                   jax.ShapeDtypeStruct((n_tok, D_DIFF), BF16),
                   kv_shape, kv_shape,
                   jax.ShapeDtypeStruct((n_tok, D_FNET), BF16)],
        input_output_aliases={} if kv is None else {6: 2, 7: 3},
        compiler_params=_cp(1),
    )(x, g.reshape(1, D_MODEL), mod, mod, w, mu, *(kv or ()))


def _lane_masks():
    lane = lax.broadcasted_iota(jnp.int32, (1, LANES), 1)
    return lane < HEAD_DIM, lane >= HEAD_DIM


def _rwkv_units(units, m0, m1, side=()):
    side = list(side)

    def run_side(drain=False):
        for gen in list(side):
            for _ in gen:
                if not drain:
                    break
            else:
                side.remove(gen)

    def bd(x):
        xb = x.astype(BF16)
        zero = jnp.zeros_like(xb)
        return jnp.concatenate([jnp.where(m0, xb, zero), jnp.where(m1, xb, zero)], axis=0)

    row = lax.broadcasted_iota(jnp.int32, (CHUNK, LANES), 0)
    col = lax.broadcasted_iota(jnp.int32, (CHUNK, LANES), 1) & (CHUNK - 1)
    eye = (col == row).astype(F32)
    r2 = lax.broadcasted_iota(jnp.int32, (LANES, LANES), 0) < HEAD_DIM
    c2 = lax.broadcasted_iota(jnp.int32, (LANES, LANES), 1) < HEAD_DIM
    rng = range(len(units))

    pre = []
    for rev, kk, r, v, kd, b, cum, cex, tot, s_prev in units:
        p_inv = jnp.exp(-cum)
        p_rem = jnp.exp(tot - cum)
        ab = -kk * jnp.exp(cex)
        rb = r * jnp.exp(cum)
        strict = (col > row) if rev else (col < row)
        incl = (col >= row) if rev else (col <= row)
        pre.append(dict(ab=ab, rb=rb, vbd=bd(v), strict=strict, incl=incl,
                        lhs=jnp.concatenate([ab, rb], axis=0),
                        rhs=jnp.concatenate([bd(b * p_inv), bd(kd * p_inv)], axis=0),
                        bk=jnp.concatenate([b * p_rem, kd * p_rem], axis=0)))

    run_side()
    mm = [_bdot_nt(q['lhs'], q['rhs']) for q in pre]
    run_side()
    m_ab = [jnp.where(pre[i]['strict'], mm[i][:CHUNK, :LANES], 0.0) for i in rng]
    m_ak = [jnp.where(pre[i]['strict'], mm[i][:CHUNK, LANES:], 0.0) for i in rng]
    m_r = [jnp.concatenate([jnp.where(pre[i]['incl'], mm[i][CHUNK:, :LANES], 0.0),
                            jnp.where(pre[i]['incl'], mm[i][CHUNK:, LANES:], 0.0)], axis=1) for i in rng]
    mv = [_bdot(m_ak[i], pre[i]['vbd']) for i in rng]
    run_side()

    t = [eye + m_ab[i] for i in rng]
    n = [_bdot(m_ab[i], bd(m_ab[i])) for i in rng]
    for _ in range(4):
        x = [_bdot(jnp.concatenate([t[i], n[i]], axis=0), bd(n[i])) for i in rng]
        t = [t[i] + x[i][:CHUNK] for i in rng]
        n = [x[i][CHUNK:] for i in rng]
        run_side()
    t = [t[i] + _bdot(t[i], bd(n[i])) for i in rng]

    w = [_bdot(t[i], jnp.concatenate([bd(pre[i]['ab']), bd(mv[i])], axis=1)) for i in rng]
    xs = [_bdot_nt(jnp.concatenate([w[i][:, :LANES], pre[i]['rb']], axis=0), units[i][9]) for i in rng]
    u = [xs[i][:CHUNK] + w[i][:, LANES:] for i in rng]
    run_side(drain=True)
    y = [xs[i][CHUNK:] + _bdot(m_r[i], jnp.concatenate([bd(u[i]), pre[i]['vbd']], axis=0)) for i in rng]
    z = [_bdot(jnp.concatenate([u[i], units[i][3]], axis=0).T, pre[i]['bk']) for i in rng]
    s_new = [units[i][9] * jnp.exp(units[i][8]) + jnp.where(r2 == c2, z[i], 0.0) for i in rng]
    return y, s_new


def _rwkv_body(seq_len, n_sub, layer, has_s0, aliased, u_ref, *refs):
    s0_ref = refs[0] if has_s0 else None
    refs = refs[1:] if has_s0 else refs
    wdec_ref, wicl_ref, wg_ref, w0a0_ref, vec_ref, bo_ref, tril_ref, triu_ref = refs[:8]
    (y_ref, sfin_ref, r_s, v_s, kk_s, kd_s, b_s, ci_s, ce_s, y_s, st_s) = refs[9 if aliased else 8:]
    n_chunk = seq_len // CHUNK
    assert n_chunk % 2 == 0 and n_chunk >= 4
    k_k = vec_ref[0:1, :]
    k_a = vec_ref[1:2, :]
    r_k = vec_ref[2:3, :]
    lnx_g = vec_ref[3:4, :]
    lnx_b = vec_ref[4:5, :]
    bo = bo_ref[...]

    def headsum(xb):
        return jnp.concatenate([jnp.dot(xb[:, p * LANES:(p + 1) * LANES], bo, preferred_element_type=F32)
                                for p in range(N_PAIR)], axis=1)

    def prep(rows):
        xs = u_ref[rows, :].astype(F32)
        r = xs[:, 0:D_RWKV]
        k = xs[:, D_RWKV:2 * D_RWKV]
        v = xs[:, 2 * D_RWKV:3 * D_RWKV]
        lora = xs[:, 3 * D_RWKV:3 * D_RWKV + LANES]
        t_lora = jnp.tanh(lora).astype(BF16)
        lora = lora.astype(BF16)
        kk = k * k_k
        kk2 = (kk * kk).astype(BF16)
        yield
        dec = _bdot(t_lora, wdec_ref[...])
        icl = _bdot(lora, wicl_ref[...])
        ss = headsum(kk2)
        yield
        logw = -DECAY_SCALE * jax.nn.sigmoid(w0a0_ref[0:1, :] + dec)
        a = jax.nn.sigmoid(w0a0_ref[1:2, :] + icl)
        kk = kk / jnp.maximum(jnp.sqrt(ss), 1e-12)
        a_f = a[:, :D_RWKV]
        a_b = a[:, D_RWKV:]
        kka = k * k_a
        kd_f = (k - kka) + kka * a_f
        kd_b = (k - kka) + kka * a_b
        lws = [_split2(logw[:, d * D_RWKV:(d + 1) * D_RWKV]) for d in range(2)]
        r_s[rows, :] = r
        v_s[rows, :] = v
        kk_s[rows, :] = kk
        kd_s[0, rows, :] = kd_f
        kd_s[1, rows, :] = kd_b
        b_s[0, rows, :] = kk * a_f
        b_s[1, rows, :] = kk * a_b
        yield
        dd = functools.partial(jnp.dot, preferred_element_type=F32)
        cums = [dd(tri_ref[...], lws[d][0]) + dd(tri_ref[...], lws[d][1])
                for d, tri_ref in enumerate((tril_ref, triu_ref))]
        yield
        for d in range(2):
            ci_s[d, rows, :] = cums[d]
            ce_s[d, rows, :] = cums[d] - logw[:, d * D_RWKV:(d + 1) * D_RWKV]

    def post(rows):
        y = y_s[rows, :]
        yb = y.astype(BF16)
        v = v_s[rows, :]
        bon = _split2(r_s[rows, :] * (kd_s[0, rows, :] + kd_s[1, rows, :]) * r_k)
        gd = u_ref[rows, 3 * D_RWKV + LANES:3 * D_RWKV + 2 * LANES].astype(F32)
        s_gd = jax.nn.sigmoid(gd).astype(BF16)
        yield
        mean = headsum(yb) * (1.0 / HEAD_DIM)
        bonus = (headsum(bon[0]) + headsum(bon[1])) * v
        g = _bdot(s_gd, wg_ref[...])
        yield
        yc = y - mean
        yc2 = (yc * yc).astype(BF16)
        yield
        var = headsum(yc2) * (1.0 / HEAD_DIM)
        yield
        yn = yc * lax.rsqrt(var + GN_EPS) * lnx_g + lnx_b
        y_ref[rows, :] = ((yn + bonus) * g).astype(BF16)

    def run_all(gens):
        gens = list(gens)
        while gens:
            gens = [gen for gen in gens if next(gen, gens) is not gens]

    def chunk_rows(i):
        return [pl.ds(pl.multiple_of(s * seq_len + (i if d == 0 else n_chunk - 1 - i) * CHUNK, CHUNK), CHUNK)
                for s in range(n_sub) for d in range(2)]

    st_s[...] = s0_ref[...] if has_s0 else jnp.zeros_like(st_s)
    y_s[...] = jnp.zeros_like(y_s)
    m0, m1 = _lane_masks()

    def scan_step(i, prep_next, post_prev):
        rows_sd = chunk_rows(i)
        side = []
        if prep_next:
            side = [prep(rows) for rows in chunk_rows(i + 1)]
        if post_prev:
            side = [post(rows) for rows in chunk_rows(i - 1)]
        units = []
        for s in range(n_sub):
            for d in range(2):
                rows = rows_sd[s * 2 + d]
                cum = ci_s[d, rows, :]
                cex = ce_s[d, rows, :]
                tot = cum[CHUNK - 1:CHUNK] if d == 0 else cum[0:1]
                kk = kk_s[rows, :]
                r = r_s[rows, :]
                v = v_s[rows, :]
                kd = kd_s[d, rows, :]
                b = b_s[d, rows, :]
                for p in range(N_PAIR):
                    sl = slice(p * LANES, (p + 1) * LANES)
                    units.append((d == 1, kk[:, sl], r[:, sl], v[:, sl], kd[:, sl], b[:, sl],
                                  cum[:, sl], cex[:, sl], tot[:, sl], st_s[s, d, p]))
        ys, s_new = _rwkv_units(units, m0, m1, side)
        for s in range(n_sub):
            for d in range(2):
                base = (s * 2 + d) * N_PAIR
                for p in range(N_PAIR):
                    st_s[s, d, p] = s_new[base + p]
                rows = rows_sd[s * 2 + d]
                y_s[rows, :] = y_s[rows, :] + jnp.concatenate(ys[base:base + N_PAIR], axis=1)

    def loop(lo, hi, **kw):
        lax.fori_loop(lo, hi, lambda i, c: (scan_step(i, **kw), c)[1], 0)

    half = n_chunk // 2
    blk = tril_ref.shape[0]
    if blk == CHUNK:
        run_all(prep(rows) for rows in chunk_rows(0))
        loop(0, half - 1, prep_next=True, post_prev=False)
        loop(half - 1, half + 1, prep_next=False, post_prev=False)
        loop(half + 1, n_chunk, prep_next=False, post_prev=True)
        run_all(post(rows) for rows in chunk_rows(n_chunk - 1))
    else:
        blocks = [slice(r0, r0 + blk) for r0 in range(0, n_sub * seq_len, blk)]
        run_all(prep(rows) for rows in blocks)
        loop(0, n_chunk, prep_next=False, post_prev=False)
        run_all(post(rows) for rows in blocks)

    for s in range(n_sub):
        for d in range(2):
            for p in range(N_PAIR):
                st = st_s[s, d, p]
                for h in range(2):
                    rows = slice(h * HEAD_DIM, (h + 1) * HEAD_DIM)
                    val = st[rows, rows]
                    if aliased:
                        sfin_ref[s, d, 2 * p + h] = val
                    else:
                        for other in range(DEPTH):
                            sfin_ref[s, other, d, 2 * p + h] = val if other == layer else jnp.zeros_like(val)


def _rwkv(u_r, s0, seq_len, n_sub, wts, states, layer):
    n_seq = u_r.shape[0] // seq_len
    rows = n_sub * seq_len
    const2 = lambda b: (0, 0)
    st_shape = (n_sub, 2, N_PAIR, LANES, LANES)
    tok = pltpu.VMEM((rows, D_RWKV), F32)
    tok2 = pltpu.VMEM((2, rows, D_RWKV), F32)
    side_work = 2 * N_PAIR * n_sub <= SIDE_WORK_MAX_UNITS
    wts = list(wts) + _chunk_triangles(CHUNK if side_work else RWKV_BLK)
    single = n_seq == n_sub
    in_specs = [pl.BlockSpec((rows, N_RWKV_PAD), lambda b: (b, 0),
                             pipeline_mode=pl.Buffered(1) if single else None)]
    args = [u_r]
    if s0 is not None:
        in_specs.append(pl.BlockSpec(st_shape, lambda b: (b, 0, 0, 0, 0)))
        args.append(_block_diag_state(s0))
    in_specs += [pl.BlockSpec(w.shape, const2) for w in wts]
    args += list(wts)
    if states is not None:
        in_specs.append(pl.BlockSpec(memory_space=pl.ANY))
        args.append(states)
    return pl.pallas_call(
        functools.partial(_rwkv_body, seq_len, n_sub, layer, s0 is not None, states is not None),
        grid=(n_seq // n_sub,),
        in_specs=in_specs,
        out_specs=[pl.BlockSpec((rows, D_RWKV), lambda b: (b, 0),
                                pipeline_mode=pl.Buffered(1) if single else None),
                   pl.BlockSpec((n_sub, DEPTH, 2, H_RWKV, HEAD_DIM, HEAD_DIM), lambda b: (b, 0, 0, 0, 0, 0))
                   if states is None else
                   pl.BlockSpec((n_sub, None, 2, H_RWKV, HEAD_DIM, HEAD_DIM),
                                lambda b: (b, layer, 0, 0, 0, 0))],
        out_shape=[jax.ShapeDtypeStruct((n_seq * seq_len, D_RWKV), BF16),
                   jax.ShapeDtypeStruct((n_seq, DEPTH, 2, H_RWKV, HEAD_DIM, HEAD_DIM), F32)],
        scratch_shapes=[tok] * 3 + [tok2] * 4 + [tok, pltpu.VMEM(st_shape, F32)],
        input_output_aliases={} if states is None else {len(args) - 1: 1},
        compiler_params=_cp(1),
    )(*args)


def _rwkv_weights(p, l):
    z = functools.partial(jnp.zeros, dtype=F32)
    wdec = z((LANES, 2 * D_RWKV))
    wdec = wdec.at[0:LORA_W, :D_RWKV].set(p['decay_up'][l, 0])
    wdec = wdec.at[LORA_W:2 * LORA_W, D_RWKV:].set(p['decay_up'][l, 1])
    wicl = z((LANES, 2 * D_RWKV))
    wicl = wicl.at[2 * LORA_W:2 * LORA_W + LORA_A, :D_RWKV].set(p['iclr_up'][l, 0])
    wicl = wicl.at[2 * LORA_W + LORA_A:2 * LORA_W + 2 * LORA_A, D_RWKV:].set(p['iclr_up'][l, 1])
    wg = z((LANES, D_RWKV)).at[0:LORA_G].set(p['gate_up'][l])
    w0a0 = jnp.stack([p['decay_w0'][l].reshape(-1), p['iclr_a0'][l].reshape(-1)])
    vec = jnp.stack([p['k_k'][l], p['k_a'][l], p['r_k'][l].reshape(-1), p['lnx_g'][l], p['lnx_b'][l],
                     z((D_RWKV,)), z((D_RWKV,)), z((D_RWKV,))])
    head = np.arange(LANES) // HEAD_DIM
    bo = jnp.asarray(head[:, None] == head[None, :], BF16)
    return [wdec.astype(BF16), wicl.astype(BF16), wg.astype(BF16), w0a0, vec, bo]


def _chunk_triangles(rows):
    idx = np.arange(rows)
    same = idx[None, :] // CHUNK == idx[:, None] // CHUNK
    return [jnp.asarray(same & (idx[None, :] <= idx[:, None]), BF16),
            jnp.asarray(same & (idx[None, :] >= idx[:, None]), BF16)]


def _rope(x, cos, sin):
    lane = lax.broadcasted_iota(jnp.int32, (1, LANES), 1)
    first_half = (lane & ROPE_PAIRS) == 0
    partner = jnp.where(first_half, pltpu.roll(x, LANES - ROPE_PAIRS, axis=1),
                        pltpu.roll(x, ROPE_PAIRS, axis=1))
    return x * cos + partner * sin


def _attn_body(has_ctx, lam_init, *refs):
    if has_ctx:
        (q_ref, k_ref, v_ref, lp_ref, sg_ref, kc_ref, vc_ref, cq_ref, sq_ref, ck_ref, sk_ref,
         o_ref) = refs
    else:
        q_ref, k_ref, v_ref, lp_ref, sg_ref, o_ref = refs
    n_pair = D_DIFF // LANES
    lp = lp_ref[...]
    lam = (jnp.exp(jnp.sum(lp[0:1] * lp[1:2], axis=-1, keepdims=True))
           - jnp.exp(jnp.sum(lp[2:3] * lp[3:4], axis=-1, keepdims=True)) + lam_init)
    lane = lax.broadcasted_iota(jnp.int32, (1, LANES), 1)

    tq = q_ref.shape[0] // k_ref.shape[0]

    def operands(seq, p):
        sl = slice(p * LANES, (p + 1) * LANES)
        q = q_ref[seq * tq:(seq + 1) * tq, sl]
        k = k_ref[seq, :, sl]
        v = v_ref[seq, :, sl]
        if has_ctx:
            q = _rope(q.astype(F32), cq_ref[...], sq_ref[...])
            k = _rope(k, ck_ref[...], sk_ref[...])
            k = jnp.concatenate([kc_ref[0, :, sl], k], axis=0)
            v = jnp.concatenate([vc_ref[0, :, sl], v], axis=0)
        vt = v.T
        ones = jnp.ones((ONES_ROWS, v.shape[0]), F32)
        vts = [jnp.concatenate([vt[h * HEAD_DIM:(h + 1) * HEAD_DIM], ones], axis=0).astype(BF16)
               for h in range(2)]
        return q.astype(BF16), k.astype(BF16), vts

    n_keys = k_ref.shape[1] + (kc_ref.shape[1] if has_ctx else 0)
    blocks = range(0, n_keys, KEY_BLOCK)
    ops, tiles = {}, []
    for seq in range(k_ref.shape[0]):
        for p in range(n_pair):
            for j in blocks:
                tiles += [(seq, p, h, m, j) for h in range(2) for m in range(2)]

    def score(tile):
        seq, p, h, m, j = tile
        if (seq, p) not in ops:
            ops[seq, p] = operands(seq, p)
        q, k, _ = ops[seq, p]
        lo = h * HEAD_DIM + m * D_QK
        sel = (lane >= lo) & (lane < lo + D_QK)
        return _bdot_nt(k[j:j + KEY_BLOCK], jnp.where(sel, q, jnp.zeros_like(q)))

    run_max, acc = {}, {}

    def consume(tile, s):
        seq, p, h, m, j = tile
        vt = ops[seq, p][2][h][:, j:j + KEY_BLOCK]
        c = (seq, p, h, m)
        mj = jnp.max(s, axis=0, keepdims=True)
        if j == 0:
            run_max[c] = mj
            acc[c] = jnp.dot(vt, jnp.exp2(s - mj).astype(BF16), preferred_element_type=F32)
        else:
            m_new = jnp.maximum(run_max[c], mj)
            acc[c] = (acc[c] * jnp.exp2(run_max[c] - m_new)
                      + jnp.dot(vt, jnp.exp2(s - m_new).astype(BF16), preferred_element_type=F32))
            run_max[c] = m_new
        if j == blocks[-1] and (h, m) == (1, 1):
            finish(seq, p)

    def finish(seq, p):
        halves = []
        for h in range(2):
            a0, a1 = acc.pop((seq, p, h, 0)), acc.pop((seq, p, h, 1))
            o = (a0[:HEAD_DIM] * (1.0 / a0[HEAD_DIM:HEAD_DIM + 1])
                 - lam * (a1[:HEAD_DIM] * (1.0 / a1[HEAD_DIM:HEAD_DIM + 1])))
            ms = jnp.mean(o * o, axis=0, keepdims=True)
            halves.append(o * lax.rsqrt(ms + SUBLN_EPS))
        o_ref[seq * tq:(seq + 1) * tq, p * LANES:(p + 1) * LANES] = (
            jnp.concatenate(halves, axis=0).T * sg_ref[...] * (1.0 - lam_init)).astype(BF16)

    pending = []
    for tile in tiles:
        pending.append((tile, score(tile)))
        if len(pending) > ATTN_LOOKAHEAD:
            consume(*pending.pop(0))
    for item in pending:
        consume(*item)


def _attention(q, k_all, v_all, layer, n_sub, lam_init, lp, sg, ctx=None):
    n_tok = q.shape[0]
    n_seq, _, seq_len, _ = k_all.shape
    tq = ATTN_TQ
    nq = seq_len // tq
    assert n_sub == 1 or nq == 1
    kv_spec = pl.BlockSpec((n_sub, None, seq_len, D_DIFF), lambda b, i: (b, layer, 0, 0))
    in_specs = [pl.BlockSpec((n_sub * tq, D_DIFF), lambda b, i: (b * nq + i, 0)),
                kv_spec, kv_spec,
                pl.BlockSpec(lp.shape, lambda b, i: (0, 0)),
                pl.BlockSpec((1, LANES), lambda b, i: (0, 0))]
    args = [q, k_all, v_all, lp, sg]
    if ctx is not None:
        kc, vc, cos, sin = ctx
        past = kc.shape[2]
        in_specs += [pl.BlockSpec((1, None, past, D_DIFF), lambda b, i: (b, layer, 0, 0)),
                     pl.BlockSpec((1, None, past, D_DIFF), lambda b, i: (b, layer, 0, 0)),
                     pl.BlockSpec((tq, LANES), lambda b, i: (i, 0)),
                     pl.BlockSpec((tq, LANES), lambda b, i: (i, 0)),
                     pl.BlockSpec((seq_len, LANES), lambda b, i: (0, 0)),
                     pl.BlockSpec((seq_len, LANES), lambda b, i: (0, 0))]
        args += [kc, vc, cos, sin, cos, sin]
    return pl.pallas_call(
        functools.partial(_attn_body, ctx is not None, lam_init),
        grid=(n_seq // n_sub, nq),
        in_specs=in_specs,
        out_specs=pl.BlockSpec((n_sub * tq, D_DIFF), lambda b, i: (b * nq + i, 0)),
        out_shape=jax.ShapeDtypeStruct((n_tok, D_DIFF), BF16),
        compiler_params=_cp(2),
    )(*args)


def _rope_tables(seq_len):
    f32 = np.float32
    t = np.arange(seq_len)
    pos = np.stack([t // GRID_W, t % GRID_W], axis=1).astype(f32)
    inv = (f32(1.0) / (f32(ROPE_BASE) ** (np.arange(ROPE_PAIRS, dtype=f32) / f32(ROPE_PAIRS)))).astype(f32)
    ang = pos[:, :, None] * inv
    d = np.arange(LANES) % D_QK
    axis = d // (2 * ROPE_PAIRS)
    second = (d % (2 * ROPE_PAIRS)) // ROPE_PAIRS
    idx = d % ROPE_PAIRS
    cos = np.cos(ang)[:, axis, idx]
    sin = np.sin(ang)[:, axis, idx] * np.where(second == 1, 1.0, -1.0).astype(f32)
    return jnp.asarray(cos, F32), jnp.asarray(sin, F32)


def _fnet_body(seq_len, x_ref, ct_ref, st_ref, cc_ref, sc_ref, o_ref):
    n_sub = x_ref.shape[0] // seq_len
    x = x_ref[...]
    xc = jnp.dot(x, cc_ref[...], preferred_element_type=F32)
    xs = jnp.dot(x, sc_ref[...], preferred_element_type=F32)
    wide = lambda a: jnp.concatenate([a[s * seq_len:(s + 1) * seq_len] for s in range(n_sub)], axis=1)
    y = _bdot(ct_ref[...], wide(xc)) - _bdot(st_ref[...], wide(xs))
    for s in range(n_sub):
        o_ref[s * seq_len:(s + 1) * seq_len, :] = y[:, s * D_FNET:(s + 1) * D_FNET].astype(BF16)


def _dft_consts(n, block=1):
    idx = np.arange(n)
    ang = 2.0 * np.pi * ((idx[:, None] * idx[None, :]) % n) / n
    return [jnp.asarray(np.kron(np.eye(block), m).astype(np.float32)).astype(BF16)
            for m in (np.cos(ang) / np.sqrt(n), np.sin(ang) / np.sqrt(n))]


def _fnet(u_f, seq_len, n_sub):
    n_tok = u_f.shape[0]
    rows = n_sub * seq_len
    consts = _dft_consts(seq_len) + _dft_consts(FNET_GROUP_DIM, FNET_GROUPS)
    const = lambda b: (0, 0)
    return pl.pallas_call(
        functools.partial(_fnet_body, seq_len),
        grid=(n_tok // rows,),
        in_specs=[pl.BlockSpec((rows, D_FNET), lambda b: (b, 0))]
        + [pl.BlockSpec(c.shape, const) for c in consts],
        out_specs=pl.BlockSpec((rows, D_FNET), lambda b: (b, 0)),
        out_shape=jax.ShapeDtypeStruct((n_tok, D_FNET), BF16),
        compiler_params=_cp(1),
    )(u_f, *consts)


def _ffn_body(final, layer, yr_ref, yd_ref, yf_ref, x_ref, g1_ref, sh2_ref, sc2_ref, g2_ref, n2_ref, fg_ref,
              wo_ref, wi_hbm, wf_hbm, o_ref, wi_ref, wf_ref, sem):
    part = x_ref.shape[0] // N_PARTS
    parts = [slice(j * part, (j + 1) * part) for j in range(N_PARTS)]

    def tile(before_ffn_in=None, before_ffn_out=None):
        y = [jnp.dot(jnp.concatenate([yr_ref[r, :], yd_ref[r, :], yf_ref[r, :]], axis=1), wo_ref[...],
                     preferred_element_type=F32) for r in parts]
        x = [x_ref[r, :] + g1_ref[0] * y[j] for j, r in enumerate(parts)]
        h = [(_rms(xj, n2_ref[...]) * (1.0 + sc2_ref[0]) + sh2_ref[0]).astype(BF16) for xj in x]
        if before_ffn_in is not None:
            before_ffn_in()
        z = [jnp.dot(hj, wi_ref[...], preferred_element_type=F32) for hj in h]
        act = [zj[:, :D_FF] * jax.nn.sigmoid(zj[:, :D_FF]) * zj[:, D_FF:] for zj in z]
        if before_ffn_out is not None:
            before_ffn_out()
        f = [_bdot(aj, wf_ref[...]) for aj in act]
        for j, r in enumerate(parts):
            xj = x[j] + g2_ref[0] * f[j]
            o_ref[r, :] = _rms(xj, fg_ref[...]) if final else xj

    first = pl.program_id(0) == 0
    wi_copy = pltpu.make_async_copy(wi_hbm.at[layer], wi_ref, sem.at[0])
    wf_copy = pltpu.make_async_copy(wf_hbm.at[layer], wf_ref, sem.at[1])

    @pl.when(first)
    def _():
        wi_copy.start()
        wf_copy.start()
        tile(wi_copy.wait, wf_copy.wait)

    @pl.when(jnp.logical_not(first))
    def _():
        tile()


def _ffn(y_r, y_d, y_f, x, mod, layer, row_fn, n2, fg, wo, wi, wf, final, tm):
    n_tok = x.shape[0]
    row = lambda i: (i, 0)
    const = lambda i: (0, 0)
    return pl.pallas_call(
        functools.partial(_ffn_body, final, layer),
        grid=(n_tok // tm,),
        in_specs=[pl.BlockSpec((tm, D_RWKV), row),
                  pl.BlockSpec((tm, D_DIFF), row),
                  pl.BlockSpec((tm, D_FNET), row),
                  pl.BlockSpec((tm, D_MODEL), row),
                  _mod_spec(layer, 2, row_fn),
                  _mod_spec(layer, 3, row_fn),
                  _mod_spec(layer, 4, row_fn),
                  _mod_spec(layer, 5, row_fn),
                  pl.BlockSpec((1, D_MODEL), const),
                  pl.BlockSpec((1, D_MODEL), const),
                  pl.BlockSpec((None,) + wo.shape[1:], lambda i: (layer, 0, 0), pipeline_mode=pl.Buffered(1)),
                  pl.BlockSpec(memory_space=pl.ANY),
                  pl.BlockSpec(memory_space=pl.ANY)],
        out_specs=pl.BlockSpec((tm, D_MODEL), row),
        out_shape=jax.ShapeDtypeStruct((n_tok, D_MODEL), F32),
        scratch_shapes=[pltpu.VMEM(wi.shape[1:], BF16), pltpu.VMEM(wf.shape[1:], BF16),
                        pltpu.SemaphoreType.DMA((2,))],
        compiler_params=_cp(1),
    )(y_r, y_d, y_f, x, mod, mod, mod, mod, n2.reshape(1, D_MODEL), fg.reshape(1, D_MODEL), wo, wi, wf)


def _block_diag_state(s):
    b = s.shape[0]
    s = s.reshape(b, 2, N_PAIR, 2, HEAD_DIM, HEAD_DIM)
    eye = jnp.eye(2, dtype=s.dtype)
    s = s[:, :, :, :, :, None, :] * eye[None, None, None, :, None, :, None]
    return s.reshape(b, 2, N_PAIR, LANES, LANES)


def kernel(x_prompt, x_sample, c, state_rwkv, cache_diff_k, cache_diff_v, c_ctx, norm1_g, norm2_g, final_norm_g, w_mod, b_mod, w_in, w_out, shift_mu, decay_w0, decay_up, iclr_a0, iclr_up, gate_up, k_k, k_a, r_k, lnx_g, lnx_b, diff_lambda, subln_g, w_ffn_in, w_ffn_out):
    p = dict(shift_mu=shift_mu, decay_w0=decay_w0, decay_up=decay_up, iclr_a0=iclr_a0, iclr_up=iclr_up,
             gate_up=gate_up, k_k=k_k, k_a=k_a, r_k=r_k, lnx_g=lnx_g, lnx_b=lnx_b)
    n_ctx, t_ctx, _ = x_prompt.shape
    n_dec, t_dec, _ = x_sample.shape
    past = cache_diff_k.shape[2]

    cond = jnp.concatenate([c_ctx[None, :], c, jnp.zeros((MOD_ROWS - 1 - n_dec, D_MODEL), F32)], axis=0)
    mod = _modulation(cond, w_mod, b_mod).reshape(DEPTH * MOD_ROWS, 1, 6 * D_MODEL)

    tm_ffn = FFN_TM
    ctx_plan = _stream_plan(n_ctx, t_ctx)
    dec_plan = _stream_plan(n_dec, t_dec)
    streams = [
        dict(x=x_prompt.reshape(n_ctx * t_ctx, D_MODEL), t=t_ctx, n=n_ctx, **ctx_plan,
             row_in=lambda i: 0, row_ffn=lambda i: 0),
        dict(x=x_sample.reshape(n_dec * t_dec, D_MODEL), t=t_dec, n=n_dec, **dec_plan,
             row_in=lambda i: 1 + i // (t_dec // dec_plan['tm_in']), row_ffn=lambda i: 1 + i // (t_dec // tm_ffn)),
    ]
    cos, sin = _rope_tables(t_dec)
    cache_k = cache_diff_k.reshape(n_dec, DEPTH, past, D_DIFF).astype(F32)
    cache_v = cache_diff_v.reshape(n_dec, DEPTH, past, D_DIFF).astype(F32)
    for st in streams:
        st['kv'] = st['states'] = None
    w_in_l = _w_in_rowmajor(w_in)
    wo = w_out.astype(BF16)
    wi = w_ffn_in.astype(BF16)
    wf = w_ffn_out.astype(BF16)
    for l in range(DEPTH):
        rw = _rwkv_weights(p, l)
        mu = jnp.concatenate([shift_mu[l], jnp.zeros((2, N_RWKV_PAD - N_RWKV_IN), F32)], axis=1)
        lam_init = 0.8 - 0.6 * math.exp(-0.3 * l)
        sg = jnp.tile(subln_g[l], 2).reshape(1, LANES)
        for si, st in enumerate(streams):
            u_r, q, k_all, v_all, u_f = _inproj(st['x'], mod, l, st['row_in'], norm1_g[l], w_in_l, mu,
                                                 st['kv'], st['t'], st['tm_in'])
            st['kv'] = (k_all, v_all)
            if si == 0:
                s0 = None
                attn_ctx = None
            else:
                s0 = state_rwkv[:, l].astype(F32)
                attn_ctx = (cache_k, cache_v, cos, sin)
                q, streams[0]['x'] = lax.optimization_barrier((q, streams[0]['x']))
            y_r, st['states'] = _rwkv(u_r, s0, st['t'], st['n_sub'], rw, st['states'], l)
            y_d = _attention(q, k_all, v_all, l, st['n_attn'], lam_init, diff_lambda[l], sg, attn_ctx)
            if si == 0:
                st['kv'], y_d = lax.optimization_barrier((st['kv'], y_d))
            y_f = _fnet(u_f, st['t'], st['n_fnet'])
            st['x'] = _ffn(y_r, y_d, y_f, st['x'], mod, l, st['row_ffn'], norm2_g[l], final_norm_g,
                           wo, wi, wf, l == DEPTH - 1, tm_ffn)
    y_prompt = streams[0]['x'].reshape(n_ctx, t_ctx, D_MODEL)
    y_sample = streams[1]['x'].reshape(n_dec, t_dec, D_MODEL)
    new_k = streams[0]['kv'][0].reshape(n_ctx, DEPTH, t_ctx, H_DIFF, 2, D_QK)
    new_v = streams[0]['kv'][1].reshape(n_ctx, DEPTH, t_ctx, H_DIFF, HEAD_DIM)
    return (y_prompt, y_sample, streams[0]['states'], new_k, new_v)
```

```python
import functools
import math

import numpy as np
import jax
import jax.numpy as jnp
from jax import lax
from jax.experimental import pallas as pl
from jax.experimental.pallas import tpu as pltpu

F32 = jnp.float32
BF16 = jnp.bfloat16

D_MODEL = 1024
DEPTH = 2
GRID_W = 64
HEAD_DIM = 64
D_RWKV = 384
H_RWKV = D_RWKV // HEAD_DIM
D_DIFF = 384
H_DIFF = D_DIFF // HEAD_DIM
D_QK = HEAD_DIM // 2
D_FNET = D_MODEL - D_RWKV - D_DIFF
FNET_GROUPS = 4
FNET_GROUP_DIM = D_FNET // FNET_GROUPS
LORA_W = 32
LORA_A = 32
LORA_G = 64
N_RWKV_IN = 3 * D_RWKV + 2 * LORA_W + 2 * LORA_A + LORA_G
N_DIFF_IN = 3 * D_DIFF
D_FF = ((8 * D_MODEL + 3 * 256 - 1) // (3 * 256)) * 256
ROPE_PAIRS = D_QK // 4
ROPE_BASE = 10000.0
RMS_EPS = 1e-6
GN_EPS = 64e-5
SUBLN_EPS = 1e-5
DECAY_SCALE = math.exp(-0.5)
Q_SCALE = D_QK ** -0.5 * math.log2(math.e)

LANES = 128
N_RWKV_PAD = 11 * LANES
D_IN_PAD = N_RWKV_PAD + N_DIFF_IN + D_FNET
N_PAIR = H_RWKV // 2
CHUNK = 64
N_PARTS = 2
SIDE_WORK_MAX_UNITS = 12
RWKV_BLK = 128
MOD_ROWS = 8
ONES_ROWS = 16
VMEM_LIMIT = 60 * 1024 * 1024

MOD_TN = 1536
INPROJ_TM = 512
FFN_TM = 512
ATTN_TQ = 256
ATTN_STEP_ROWS = 1024
KEY_BLOCK = 512
ATTN_LOOKAHEAD = 8
FNET_STEP_ROWS = 1024
RWKV_STEP_ROWS = 1024
RWKV_ONE_STEP_ROWS = 2048


def _stream_plan(n_seq, seq_len):
    one_step = n_seq * seq_len <= RWKV_ONE_STEP_ROWS
    return dict(n_sub=n_seq if one_step else max(1, RWKV_STEP_ROWS // seq_len),
                n_attn=max(1, ATTN_STEP_ROWS // seq_len),
                n_fnet=max(1, FNET_STEP_ROWS // seq_len),
                tm_in=max(seq_len, INPROJ_TM))


def _cp(n_axes=1):
    return pltpu.CompilerParams(dimension_semantics=("arbitrary",) * n_axes,
                                vmem_limit_bytes=VMEM_LIMIT)


def _bdot(a, b):
    return jnp.dot(a.astype(BF16), b.astype(BF16), preferred_element_type=F32)


def _bdot_nt(a, b):
    return lax.dot_general(a.astype(BF16), b.astype(BF16), (((1,), (1,)), ((), ())),
                           preferred_element_type=F32)


def _split2(x):
    hi = x.astype(BF16)
    lo = (x - hi.astype(F32)).astype(BF16)
    return hi, lo


def _dot_x3(a, b):
    a_hi, a_lo = _split2(a)
    b_hi, b_lo = _split2(b)
    d = functools.partial(jnp.dot, preferred_element_type=F32)
    return d(a_hi, b_hi) + d(a_lo, b_hi) + d(a_hi, b_lo)


def _rms(x, g):
    return x * lax.rsqrt(jnp.mean(x * x, axis=-1, keepdims=True) + RMS_EPS) * g


def _w_in_body(x_ref, o_ref):
    col = lax.broadcasted_iota(jnp.int32, (1, N_RWKV_PAD), 1)
    head = jnp.where(col < N_RWKV_IN, x_ref[0:N_RWKV_PAD, :].T, 0.0)
    tail = x_ref[N_RWKV_IN:, :].T
    o_ref[:, :N_RWKV_PAD] = head.astype(BF16)
    o_ref[:, N_RWKV_PAD:] = tail.astype(BF16)


def _w_in_rowmajor(w_in):
    w_t = jnp.swapaxes(w_in, 1, 2)
    n_layers, n_out, _ = w_t.shape
    return pl.pallas_call(
        _w_in_body,
        grid=(n_layers,),
        in_specs=[pl.BlockSpec((None, n_out, D_MODEL), lambda l: (l, 0, 0))],
        out_specs=pl.BlockSpec((None, D_MODEL, D_IN_PAD), lambda l: (l, 0, 0)),
        out_shape=jax.ShapeDtypeStruct((n_layers, D_MODEL, D_IN_PAD), BF16),
        compiler_params=_cp(1),
    )(w_t)


def _mod_body(c_ref, w_ref, b_ref, o_ref):
    c = c_ref[...]
    a = c * jax.nn.sigmoid(c)
    o_ref[0] = _dot_x3(a, w_ref[0]) + b_ref[0]


def _modulation(cond, w_mod, b_mod):
    n_layers, _, n_out = w_mod.shape
    tn = MOD_TN
    return pl.pallas_call(
        _mod_body,
        grid=(n_layers, n_out // tn),
        in_specs=[pl.BlockSpec((MOD_ROWS, D_MODEL), lambda l, j: (0, 0)),
                  pl.BlockSpec((1, D_MODEL, tn), lambda l, j: (l, 0, j)),
                  pl.BlockSpec((1, 1, tn), lambda l, j: (l, 0, j))],
        out_specs=pl.BlockSpec((1, MOD_ROWS, tn), lambda l, j: (l, 0, j)),
        out_shape=jax.ShapeDtypeStruct((n_layers, MOD_ROWS, n_out), F32),
        compiler_params=_cp(2),
    )(cond, w_mod, b_mod.reshape(n_layers, 1, n_out))


def _mod_spec(layer, col, row_fn):
    return pl.BlockSpec((1, 1, D_MODEL), lambda i: (layer * MOD_ROWS + row_fn(i), 0, col))


def _put_layer(ref, idx, layer, aliased, val):
    if aliased:
        ref[idx] = val
    else:
        for other in range(DEPTH):
            ref[idx + (other,)] = val if other == layer else jnp.zeros_like(val)


def _inproj_body(seq_len, layer, aliased, x_ref, g_ref, sh_ref, sc_ref, w_ref, mu_ref, *refs):
    ur_ref, q_ref, k_ref, v_ref, uf_ref = refs[2:] if aliased else refs
    part = x_ref.shape[0] // N_PARTS
    parts = [slice(j * part, (j + 1) * part) for j in range(N_PARTS)]
    h = [(_rms(x_ref[r, :], g_ref[...]) * (1.0 + sc_ref[0]) + sh_ref[0]).astype(BF16) for r in parts]
    u = jnp.concatenate([jnp.dot(hj, w_ref[...], preferred_element_type=F32) for hj in h], axis=0)
    ur = u[:, :N_RWKV_PAD]
    tm = ur.shape[0]
    pos = lax.broadcasted_iota(jnp.int32, (tm, 1), 0) & (seq_len - 1)
    prev = jnp.where(pos == 0, 0.0, pltpu.roll(ur, 1, axis=0))
    nxt = jnp.where(pos == seq_len - 1, 0.0, pltpu.roll(ur, tm - 1, axis=0))
    ur_ref[...] = (ur + mu_ref[0:1, :] * (prev - ur) + mu_ref[1:2, :] * (nxt - ur)).astype(BF16)
    q_ref[...] = (u[:, N_RWKV_PAD:N_RWKV_PAD + D_DIFF] * Q_SCALE).astype(BF16)
    for s in range(tm // seq_len):
        rows = slice(s * seq_len, (s + 1) * seq_len)
        _put_layer(k_ref, (s,), layer, aliased, u[rows, N_RWKV_PAD + D_DIFF:N_RWKV_PAD + 2 * D_DIFF])
        _put_layer(v_ref, (s,), layer, aliased, u[rows, N_RWKV_PAD + 2 * D_DIFF:N_RWKV_PAD + 3 * D_DIFF])
    uf_ref[...] = u[:, N_RWKV_PAD + N_DIFF_IN:].astype(BF16)


def _inproj(x, mod, layer, row_fn, g, w, mu, kv, seq_len, tm):
    n_tok = x.shape[0]
    kv_shape = jax.ShapeDtypeStruct((n_tok // seq_len, DEPTH, seq_len, D_DIFF), F32)
    assert tm % seq_len == 0 and seq_len & (seq_len - 1) == 0
    row = lambda i: (i, 0)
    const = lambda i: (0, 0)
    if kv is None:
        kv_spec = pl.BlockSpec((tm // seq_len, DEPTH, seq_len, D_DIFF), lambda i: (i, 0, 0, 0))
    else:
        kv_spec = pl.BlockSpec((tm // seq_len, None, seq_len, D_DIFF), lambda i: (i, layer, 0, 0))
    return pl.pallas_call(
        functools.partial(_inproj_body, seq_len, layer, kv is not None),
        grid=(n_tok // tm,),
        in_specs=[pl.BlockSpec((tm, D_MODEL), row),
                  pl.BlockSpec((1, D_MODEL), const),
                  _mod_spec(layer, 0, row_fn),
                  _mod_spec(layer, 1, row_fn),
                  pl.BlockSpec((None, D_MODEL, D_IN_PAD), lambda i: (layer, 0, 0)),
                  pl.BlockSpec((2, N_RWKV_PAD), const)]
        + [pl.BlockSpec(memory_space=pl.ANY)] * (0 if kv is None else 2),
        out_specs=[pl.BlockSpec((tm, N_RWKV_PAD), row),
                   pl.BlockSpec((tm, D_DIFF), row),
                   kv_spec, kv_spec,
                   pl.BlockSpec((tm, D_FNET), row)],
        out_shape=[jax.ShapeDtypeStruct((n_tok, N_RWKV_PAD), BF16),
                   jax.ShapeDtypeStruct((n_tok, D_DIFF), BF16),
                   kv_shape, kv_shape,
                   jax.ShapeDtypeStruct((n_tok, D_FNET), BF16)],
        input_output_aliases={} if kv is None else {6: 2, 7: 3},
        compiler_params=_cp(1),
    )(x, g.reshape(1, D_MODEL), mod, mod, w, mu, *(kv or ()))


def _lane_masks():
    lane = lax.broadcasted_iota(jnp.int32, (1, LANES), 1)
    return lane < HEAD_DIM, lane >= HEAD_DIM


def _rwkv_units(units, m0, m1, side=()):
    side = list(side)

    def run_side(drain=False):
        for gen in list(side):
            for _ in gen:
                if not drain:
                    break
            else:
                side.remove(gen)

    def bd(x):
        xb = x.astype(BF16)
        zero = jnp.zeros_like(xb)
        return jnp.concatenate([jnp.where(m0, xb, zero), jnp.where(m1, xb, zero)], axis=0)

    row = lax.broadcasted_iota(jnp.int32, (CHUNK, LANES), 0)
    col = lax.broadcasted_iota(jnp.int32, (CHUNK, LANES), 1) & (CHUNK - 1)
    eye = (col == row).astype(F32)
    r2 = lax.broadcasted_iota(jnp.int32, (LANES, LANES), 0) < HEAD_DIM
    c2 = lax.broadcasted_iota(jnp.int32, (LANES, LANES), 1) < HEAD_DIM
    rng = range(len(units))

    pre = []
    for rev, kk, r, v, kd, b, cum, cex, tot, s_prev in units:
        p_inv = jnp.exp(-cum)
        p_rem = jnp.exp(tot - cum)
        ab = -kk * jnp.exp(cex)
        rb = r * jnp.exp(cum)
        strict = (col > row) if rev else (col < row)
        incl = (col >= row) if rev else (col <= row)
        pre.append(dict(ab=ab, rb=rb, vbd=bd(v), strict=strict, incl=incl,
                        lhs=jnp.concatenate([ab, rb], axis=0),
                        rhs=jnp.concatenate([bd(b * p_inv), bd(kd * p_inv)], axis=0),
                        bk=jnp.concatenate([b * p_rem, kd * p_rem], axis=0)))

    run_side()
    mm = [_bdot_nt(q['lhs'], q['rhs']) for q in pre]
    run_side()
    m_ab = [jnp.where(pre[i]['strict'], mm[i][:CHUNK, :LANES], 0.0) for i in rng]
    m_ak = [jnp.where(pre[i]['strict'], mm[i][:CHUNK, LANES:], 0.0) for i in rng]
    m_r = [jnp.concatenate([jnp.where(pre[i]['incl'], mm[i][CHUNK:, :LANES], 0.0),
                            jnp.where(pre[i]['incl'], mm[i][CHUNK:, LANES:], 0.0)], axis=1) for i in rng]
    mv = [_bdot(m_ak[i], pre[i]['vbd']) for i in rng]
    run_side()

    t = [eye + m_ab[i] for i in rng]
    n = [_bdot(m_ab[i], bd(m_ab[i])) for i in rng]
    for _ in range(4):
        x = [_bdot(jnp.concatenate([t[i], n[i]], axis=0), bd(n[i])) for i in rng]
        t = [t[i] + x[i][:CHUNK] for i in rng]
        n = [x[i][CHUNK:] for i in rng]
        run_side()
    t = [t[i] + _bdot(t[i], bd(n[i])) for i in rng]

    w = [_bdot(t[i], jnp.concatenate([bd(pre[i]['ab']), bd(mv[i])], axis=1)) for i in rng]
    xs = [_bdot_nt(jnp.concatenate([w[i][:, :LANES], pre[i]['rb']], axis=0), units[i][9]) for i in rng]
    u = [xs[i][:CHUNK] + w[i][:, LANES:] for i in rng]
    run_side(drain=True)
    y = [xs[i][CHUNK:] + _bdot(m_r[i], jnp.concatenate([bd(u[i]), pre[i]['vbd']], axis=0)) for i in rng]
    z = [_bdot(jnp.concatenate([u[i], units[i][3]], axis=0).T, pre[i]['bk']) for i in rng]
    s_new = [units[i][9] * jnp.exp(units[i][8]) + jnp.where(r2 == c2, z[i], 0.0) for i in rng]
    return y, s_new


def _rwkv_body(seq_len, n_sub, layer, has_s0, aliased, u_ref, *refs):
    s0_ref = refs[0] if has_s0 else None
    refs = refs[1:] if has_s0 else refs
    wdec_ref, wicl_ref, wg_ref, w0a0_ref, vec_ref, bo_ref, tril_ref, triu_ref = refs[:8]
    (y_ref, sfin_ref, r_s, v_s, kk_s, kd_s, b_s, ci_s, ce_s, y_s, st_s) = refs[9 if aliased else 8:]
    n_chunk = seq_len // CHUNK
    assert n_chunk % 2 == 0 and n_chunk >= 4
    k_k = vec_ref[0:1, :]
    k_a = vec_ref[1:2, :]
    r_k = vec_ref[2:3, :]
    lnx_g = vec_ref[3:4, :]
    lnx_b = vec_ref[4:5, :]
    bo = bo_ref[...]

    def headsum(xb):
        return jnp.concatenate([jnp.dot(xb[:, p * LANES:(p + 1) * LANES], bo, preferred_element_type=F32)
                                for p in range(N_PAIR)], axis=1)

    def prep(rows):
        xs = u_ref[rows, :].astype(F32)
        r = xs[:, 0:D_RWKV]
        k = xs[:, D_RWKV:2 * D_RWKV]
        v = xs[:, 2 * D_RWKV:3 * D_RWKV]
        lora = xs[:, 3 * D_RWKV:3 * D_RWKV + LANES]
        t_lora = jnp.tanh(lora).astype(BF16)
        lora = lora.astype(BF16)
        kk = k * k_k
        kk2 = (kk * kk).astype(BF16)
        yield
        dec = _bdot(t_lora, wdec_ref[...])
        icl = _bdot(lora, wicl_ref[...])
        ss = headsum(kk2)
        yield
        logw = -DECAY_SCALE * jax.nn.sigmoid(w0a0_ref[0:1, :] + dec)
        a = jax.nn.sigmoid(w0a0_ref[1:2, :] + icl)
        kk = kk / jnp.maximum(jnp.sqrt(ss), 1e-12)
        a_f = a[:, :D_RWKV]
        a_b = a[:, D_RWKV:]
        kka = k * k_a
        kd_f = (k - kka) + kka * a_f
        kd_b = (k - kka) + kka * a_b
        lws = [_split2(logw[:, d * D_RWKV:(d + 1) * D_RWKV]) for d in range(2)]
        r_s[rows, :] = r
        v_s[rows, :] = v
        kk_s[rows, :] = kk
        kd_s[0, rows, :] = kd_f
        kd_s[1, rows, :] = kd_b
        b_s[0, rows, :] = kk * a_f
        b_s[1, rows, :] = kk * a_b
        yield
        dd = functools.partial(jnp.dot, preferred_element_type=F32)
        cums = [dd(tri_ref[...], lws[d][0]) + dd(tri_ref[...], lws[d][1])
                for d, tri_ref in enumerate((tril_ref, triu_ref))]
        yield
        for d in range(2):
            ci_s[d, rows, :] = cums[d]
            ce_s[d, rows, :] = cums[d] - logw[:, d * D_RWKV:(d + 1) * D_RWKV]

    def post(rows):
        y = y_s[rows, :]
        yb = y.astype(BF16)
        v = v_s[rows, :]
        bon = _split2(r_s[rows, :] * (kd_s[0, rows, :] + kd_s[1, rows, :]) * r_k)
        gd = u_ref[rows, 3 * D_RWKV + LANES:3 * D_RWKV + 2 * LANES].astype(F32)
        s_gd = jax.nn.sigmoid(gd).astype(BF16)
        yield
        mean = headsum(yb) * (1.0 / HEAD_DIM)
        bonus = (headsum(bon[0]) + headsum(bon[1])) * v
        g = _bdot(s_gd, wg_ref[...])
        yield
        yc = y - mean
        yc2 = (yc * yc).astype(BF16)
        yield
        var = headsum(yc2) * (1.0 / HEAD_DIM)
        yield
        yn = yc * lax.rsqrt(var + GN_EPS) * lnx_g + lnx_b
        y_ref[rows, :] = ((yn + bonus) * g).astype(BF16)

    def run_all(gens):
        gens = list(gens)
        while gens:
            gens = [gen for gen in gens if next(gen, gens) is not gens]

    def chunk_rows(i):
        return [pl.ds(pl.multiple_of(s * seq_len + (i if d == 0 else n_chunk - 1 - i) * CHUNK, CHUNK), CHUNK)
                for s in range(n_sub) for d in range(2)]

    st_s[...] = s0_ref[...] if has_s0 else jnp.zeros_like(st_s)
    y_s[...] = jnp.zeros_like(y_s)
    m0, m1 = _lane_masks()

    def scan_step(i, prep_next, post_prev):
        rows_sd = chunk_rows(i)
        side = []
        if prep_next:
            side = [prep(rows) for rows in chunk_rows(i + 1)]
        if post_prev:
            side = [post(rows) for rows in chunk_rows(i - 1)]
        units = []
        for s in range(n_sub):
            for d in range(2):
                rows = rows_sd[s * 2 + d]
                cum = ci_s[d, rows, :]
                cex = ce_s[d, rows, :]
                tot = cum[CHUNK - 1:CHUNK] if d == 0 else cum[0:1]
                kk = kk_s[rows, :]
                r = r_s[rows, :]
                v = v_s[rows, :]
                kd = kd_s[d, rows, :]
                b = b_s[d, rows, :]
                for p in range(N_PAIR):
                    sl = slice(p * LANES, (p + 1) * LANES)
                    units.append((d == 1, kk[:, sl], r[:, sl], v[:, sl], kd[:, sl], b[:, sl],
                                  cum[:, sl], cex[:, sl], tot[:, sl], st_s[s, d, p]))
        ys, s_new = _rwkv_units(units, m0, m1, side)
        for s in range(n_sub):
            for d in range(2):
                base = (s * 2 + d) * N_PAIR
                for p in range(N_PAIR):
                    st_s[s, d, p] = s_new[base + p]
                rows = rows_sd[s * 2 + d]
                y_s[rows, :] = y_s[rows, :] + jnp.concatenate(ys[base:base + N_PAIR], axis=1)

    def loop(lo, hi, **kw):
        lax.fori_loop(lo, hi, lambda i, c: (scan_step(i, **kw), c)[1], 0)

    half = n_chunk // 2
    blk = tril_ref.shape[0]
    if blk == CHUNK:
        run_all(prep(rows) for rows in chunk_rows(0))
        loop(0, half - 1, prep_next=True, post_prev=False)
        loop(half - 1, half + 1, prep_next=False, post_prev=False)
        loop(half + 1, n_chunk, prep_next=False, post_prev=True)
        run_all(post(rows) for rows in chunk_rows(n_chunk - 1))
    else:
        blocks = [slice(r0, r0 + blk) for r0 in range(0, n_sub * seq_len, blk)]
        run_all(prep(rows) for rows in blocks)
        loop(0, n_chunk, prep_next=False, post_prev=False)
        run_all(post(rows) for rows in blocks)

    for s in range(n_sub):
        for d in range(2):
            for p in range(N_PAIR):
                st = st_s[s, d, p]
                for h in range(2):
                    rows = slice(h * HEAD_DIM, (h + 1) * HEAD_DIM)
                    val = st[rows, rows]
                    if aliased:
                        sfin_ref[s, d, 2 * p + h] = val
                    else:
                        for other in range(DEPTH):
                            sfin_ref[s, other, d, 2 * p + h] = val if other == layer else jnp.zeros_like(val)


def _rwkv(u_r, s0, seq_len, n_sub, wts, states, layer):
    n_seq = u_r.shape[0] // seq_len
    rows = n_sub * seq_len
    const2 = lambda b: (0, 0)
    st_shape = (n_sub, 2, N_PAIR, LANES, LANES)
    tok = pltpu.VMEM((rows, D_RWKV), F32)
    tok2 = pltpu.VMEM((2, rows, D_RWKV), F32)
    side_work = 2 * N_PAIR * n_sub <= SIDE_WORK_MAX_UNITS
    wts = list(wts) + _chunk_triangles(CHUNK if side_work else RWKV_BLK)
    single = n_seq == n_sub
    in_specs = [pl.BlockSpec((rows, N_RWKV_PAD), lambda b: (b, 0),
                             pipeline_mode=pl.Buffered(1) if single else None)]
    args = [u_r]
    if s0 is not None:
        in_specs.append(pl.BlockSpec(st_shape, lambda b: (b, 0, 0, 0, 0)))
        args.append(_block_diag_state(s0))
    in_specs += [pl.BlockSpec(w.shape, const2) for w in wts]
    args += list(wts)
    if states is not None:
        in_specs.append(pl.BlockSpec(memory_space=pl.ANY))
        args.append(states)
    return pl.pallas_call(
        functools.partial(_rwkv_body, seq_len, n_sub, layer, s0 is not None, states is not None),
        grid=(n_seq // n_sub,),
        in_specs=in_specs,
        out_specs=[pl.BlockSpec((rows, D_RWKV), lambda b: (b, 0),
                                pipeline_mode=pl.Buffered(1) if single else None),
                   pl.BlockSpec((n_sub, DEPTH, 2, H_RWKV, HEAD_DIM, HEAD_DIM), lambda b: (b, 0, 0, 0, 0, 0))
                   if states is None else
                   pl.BlockSpec((n_sub, None, 2, H_RWKV, HEAD_DIM, HEAD_DIM),
                                lambda b: (b, layer, 0, 0, 0, 0))],
        out_shape=[jax.ShapeDtypeStruct((n_seq * seq_len, D_RWKV), BF16),
                   jax.ShapeDtypeStruct((n_seq, DEPTH, 2, H_RWKV, HEAD_DIM, HEAD_DIM), F32)],
        scratch_shapes=[tok] * 3 + [tok2] * 4 + [tok, pltpu.VMEM(st_shape, F32)],
        input_output_aliases={} if states is None else {len(args) - 1: 1},
        compiler_params=_cp(1),
    )(*args)


def _rwkv_weights(p, l):
    z = functools.partial(jnp.zeros, dtype=F32)
    wdec = z((LANES, 2 * D_RWKV))
    wdec = wdec.at[0:LORA_W, :D_RWKV].set(p['decay_up'][l, 0])
    wdec = wdec.at[LORA_W:2 * LORA_W, D_RWKV:].set(p['decay_up'][l, 1])
    wicl = z((LANES, 2 * D_RWKV))
    wicl = wicl.at[2 * LORA_W:2 * LORA_W + LORA_A, :D_RWKV].set(p['iclr_up'][l, 0])
    wicl = wicl.at[2 * LORA_W + LORA_A:2 * LORA_W + 2 * LORA_A, D_RWKV:].set(p['iclr_up'][l, 1])
    wg = z((LANES, D_RWKV)).at[0:LORA_G].set(p['gate_up'][l])
    w0a0 = jnp.stack([p['decay_w0'][l].reshape(-1), p['iclr_a0'][l].reshape(-1)])
    vec = jnp.stack([p['k_k'][l], p['k_a'][l], p['r_k'][l].reshape(-1), p['lnx_g'][l], p['lnx_b'][l],
                     z((D_RWKV,)), z((D_RWKV,)), z((D_RWKV,))])
    head = np.arange(LANES) // HEAD_DIM
    bo = jnp.asarray(head[:, None] == head[None, :], BF16)
    return [wdec.astype(BF16), wicl.astype(BF16), wg.astype(BF16), w0a0, vec, bo]


def _chunk_triangles(rows):
    idx = np.arange(rows)
    same = idx[None, :] // CHUNK == idx[:, None] // CHUNK
    return [jnp.asarray(same & (idx[None, :] <= idx[:, None]), BF16),
            jnp.asarray(same & (idx[None, :] >= idx[:, None]), BF16)]


def _rope(x, cos, sin):
    lane = lax.broadcasted_iota(jnp.int32, (1, LANES), 1)
    first_half = (lane & ROPE_PAIRS) == 0
    partner = jnp.where(first_half, pltpu.roll(x, LANES - ROPE_PAIRS, axis=1),
                        pltpu.roll(x, ROPE_PAIRS, axis=1))
    return x * cos + partner * sin


def _attn_body(has_ctx, lam_init, *refs):
    if has_ctx:
        (q_ref, k_ref, v_ref, lp_ref, sg_ref, kc_ref, vc_ref, cq_ref, sq_ref, ck_ref, sk_ref,
         o_ref) = refs
    else:
        q_ref, k_ref, v_ref, lp_ref, sg_ref, o_ref = refs
    n_pair = D_DIFF // LANES
    lp = lp_ref[...]
    lam = (jnp.exp(jnp.sum(lp[0:1] * lp[1:2], axis=-1, keepdims=True))
           - jnp.exp(jnp.sum(lp[2:3] * lp[3:4], axis=-1, keepdims=True)) + lam_init)
    lane = lax.broadcasted_iota(jnp.int32, (1, LANES), 1)

    tq = q_ref.shape[0] // k_ref.shape[0]

    def operands(seq, p):
        sl = slice(p * LANES, (p + 1) * LANES)
        q = q_ref[seq * tq:(seq + 1) * tq, sl]
        k = k_ref[seq, :, sl]
        v = v_ref[seq, :, sl]
        if has_ctx:
            q = _rope(q.astype(F32), cq_ref[...], sq_ref[...])
            k = _rope(k, ck_ref[...], sk_ref[...])
            k = jnp.concatenate([kc_ref[0, :, sl], k], axis=0)
            v = jnp.concatenate([vc_ref[0, :, sl], v], axis=0)
        vt = v.T
        ones = jnp.ones((ONES_ROWS, v.shape[0]), F32)
        vts = [jnp.concatenate([vt[h * HEAD_DIM:(h + 1) * HEAD_DIM], ones], axis=0).astype(BF16)
               for h in range(2)]
        return q.astype(BF16), k.astype(BF16), vts

    n_keys = k_ref.shape[1] + (kc_ref.shape[1] if has_ctx else 0)
    blocks = range(0, n_keys, KEY_BLOCK)
    ops, tiles = {}, []
    for seq in range(k_ref.shape[0]):
        for p in range(n_pair):
            for j in blocks:
                tiles += [(seq, p, h, m, j) for h in range(2) for m in range(2)]

    def score(tile):
        seq, p, h, m, j = tile
        if (seq, p) not in ops:
            ops[seq, p] = operands(seq, p)
        q, k, _ = ops[seq, p]
        lo = h * HEAD_DIM + m * D_QK
        sel = (lane >= lo) & (lane < lo + D_QK)
        return _bdot_nt(k[j:j + KEY_BLOCK], jnp.where(sel, q, jnp.zeros_like(q)))

    run_max, acc = {}, {}

    def consume(tile, s):
        seq, p, h, m, j = tile
        vt = ops[seq, p][2][h][:, j:j + KEY_BLOCK]
        c = (seq, p, h, m)
        mj = jnp.max(s, axis=0, keepdims=True)
        if j == 0:
            run_max[c] = mj
            acc[c] = jnp.dot(vt, jnp.exp2(s - mj).astype(BF16), preferred_element_type=F32)
        else:
            m_new = jnp.maximum(run_max[c], mj)
            acc[c] = (acc[c] * jnp.exp2(run_max[c] - m_new)
                      + jnp.dot(vt, jnp.exp2(s - m_new).astype(BF16), preferred_element_type=F32))
            run_max[c] = m_new
        if j == blocks[-1] and (h, m) == (1, 1):
            finish(seq, p)

    def finish(seq, p):
        halves = []
        for h in range(2):
            a0, a1 = acc.pop((seq, p, h, 0)), acc.pop((seq, p, h, 1))
            o = (a0[:HEAD_DIM] * (1.0 / a0[HEAD_DIM:HEAD_DIM + 1])
                 - lam * (a1[:HEAD_DIM] * (1.0 / a1[HEAD_DIM:HEAD_DIM + 1])))
            ms = jnp.mean(o * o, axis=0, keepdims=True)
            halves.append(o * lax.rsqrt(ms + SUBLN_EPS))
        o_ref[seq * tq:(seq + 1) * tq, p * LANES:(p + 1) * LANES] = (
            jnp.concatenate(halves, axis=0).T * sg_ref[...] * (1.0 - lam_init)).astype(BF16)

    pending = []
    for tile in tiles:
        pending.append((tile, score(tile)))
        if len(pending) > ATTN_LOOKAHEAD:
            consume(*pending.pop(0))
    for item in pending:
        consume(*item)


def _attention(q, k_all, v_all, layer, n_sub, lam_init, lp, sg, ctx=None):
    n_tok = q.shape[0]
    n_seq, _, seq_len, _ = k_all.shape
    tq = ATTN_TQ
    nq = seq_len // tq
    assert n_sub == 1 or nq == 1
    kv_spec = pl.BlockSpec((n_sub, None, seq_len, D_DIFF), lambda b, i: (b, layer, 0, 0))
    in_specs = [pl.BlockSpec((n_sub * tq, D_DIFF), lambda b, i: (b * nq + i, 0)),
                kv_spec, kv_spec,
                pl.BlockSpec(lp.shape, lambda b, i: (0, 0)),
                pl.BlockSpec((1, LANES), lambda b, i: (0, 0))]
    args = [q, k_all, v_all, lp, sg]
    if ctx is not None:
        kc, vc, cos, sin = ctx
        past = kc.shape[2]
        in_specs += [pl.BlockSpec((1, None, past, D_DIFF), lambda b, i: (b, layer, 0, 0)),
                     pl.BlockSpec((1, None, past, D_DIFF), lambda b, i: (b, layer, 0, 0)),
                     pl.BlockSpec((tq, LANES), lambda b, i: (i, 0)),
                     pl.BlockSpec((tq, LANES), lambda b, i: (i, 0)),
                     pl.BlockSpec((seq_len, LANES), lambda b, i: (0, 0)),
                     pl.BlockSpec((seq_len, LANES), lambda b, i: (0, 0))]
        args += [kc, vc, cos, sin, cos, sin]
    return pl.pallas_call(
        functools.partial(_attn_body, ctx is not None, lam_init),
        grid=(n_seq // n_sub, nq),
        in_specs=in_specs,
        out_specs=pl.BlockSpec((n_sub * tq, D_DIFF), lambda b, i: (b * nq + i, 0)),
        out_shape=jax.ShapeDtypeStruct((n_tok, D_DIFF), BF16),
        compiler_params=_cp(2),
    )(*args)


def _rope_tables(seq_len):
    f32 = np.float32
    t = np.arange(seq_len)
    pos = np.stack([t // GRID_W, t % GRID_W], axis=1).astype(f32)
    inv = (f32(1.0) / (f32(ROPE_BASE) ** (np.arange(ROPE_PAIRS, dtype=f32) / f32(ROPE_PAIRS)))).astype(f32)
    ang = pos[:, :, None] * inv
    d = np.arange(LANES) % D_QK
    axis = d // (2 * ROPE_PAIRS)
    second = (d % (2 * ROPE_PAIRS)) // ROPE_PAIRS
    idx = d % ROPE_PAIRS
    cos = np.cos(ang)[:, axis, idx]
    sin = np.sin(ang)[:, axis, idx] * np.where(second == 1, 1.0, -1.0).astype(f32)
    return jnp.asarray(cos, F32), jnp.asarray(sin, F32)


def _fnet_body(seq_len, x_ref, ct_ref, st_ref, cc_ref, sc_ref, o_ref):
    n_sub = x_ref.shape[0] // seq_len
    x = x_ref[...]
    xc = jnp.dot(x, cc_ref[...], preferred_element_type=F32)
    xs = jnp.dot(x, sc_ref[...], preferred_element_type=F32)
    wide = lambda a: jnp.concatenate([a[s * seq_len:(s + 1) * seq_len] for s in range(n_sub)], axis=1)
    y = _bdot(ct_ref[...], wide(xc)) - _bdot(st_ref[...], wide(xs))
    for s in range(n_sub):
        o_ref[s * seq_len:(s + 1) * seq_len, :] = y[:, s * D_FNET:(s + 1) * D_FNET].astype(BF16)


def _dft_consts(n, block=1):
    idx = np.arange(n)
    ang = 2.0 * np.pi * ((idx[:, None] * idx[None, :]) % n) / n
    return [jnp.asarray(np.kron(np.eye(block), m).astype(np.float32)).astype(BF16)
            for m in (np.cos(ang) / np.sqrt(n), np.sin(ang) / np.sqrt(n))]


def _fnet(u_f, seq_len, n_sub):
    n_tok = u_f.shape[0]
    rows = n_sub * seq_len
    consts = _dft_consts(seq_len) + _dft_consts(FNET_GROUP_DIM, FNET_GROUPS)
    const = lambda b: (0, 0)
    return pl.pallas_call(
        functools.partial(_fnet_body, seq_len),
        grid=(n_tok // rows,),
        in_specs=[pl.BlockSpec((rows, D_FNET), lambda b: (b, 0))]
        + [pl.BlockSpec(c.shape, const) for c in consts],
        out_specs=pl.BlockSpec((rows, D_FNET), lambda b: (b, 0)),
        out_shape=jax.ShapeDtypeStruct((n_tok, D_FNET), BF16),
        compiler_params=_cp(1),
    )(u_f, *consts)


def _ffn_body(final, layer, yr_ref, yd_ref, yf_ref, x_ref, g1_ref, sh2_ref, sc2_ref, g2_ref, n2_ref, fg_ref,
              wo_ref, wi_hbm, wf_hbm, o_ref, wi_ref, wf_ref, sem):
    part = x_ref.shape[0] // N_PARTS
    parts = [slice(j * part, (j + 1) * part) for j in range(N_PARTS)]

    def tile(before_ffn_in=None, before_ffn_out=None):
        y = [jnp.dot(jnp.concatenate([yr_ref[r, :], yd_ref[r, :], yf_ref[r, :]], axis=1), wo_ref[...],
                     preferred_element_type=F32) for r in parts]
        x = [x_ref[r, :] + g1_ref[0] * y[j] for j, r in enumerate(parts)]
        h = [(_rms(xj, n2_ref[...]) * (1.0 + sc2_ref[0]) + sh2_ref[0]).astype(BF16) for xj in x]
        if before_ffn_in is not None:
            before_ffn_in()
        z = [jnp.dot(hj, wi_ref[...], preferred_element_type=F32) for hj in h]
        act = [zj[:, :D_FF] * jax.nn.sigmoid(zj[:, :D_FF]) * zj[:, D_FF:] for zj in z]
        if before_ffn_out is not None:
            before_ffn_out()
        f = [_bdot(aj, wf_ref[...]) for aj in act]
        for j, r in enumerate(parts):
            xj = x[j] + g2_ref[0] * f[j]
            o_ref[r, :] = _rms(xj, fg_ref[...]) if final else xj

    first = pl.program_id(0) == 0
    wi_copy = pltpu.make_async_copy(wi_hbm.at[layer], wi_ref, sem.at[0])
    wf_copy = pltpu.make_async_copy(wf_hbm.at[layer], wf_ref, sem.at[1])

    @pl.when(first)
    def _():
        wi_copy.start()
        wf_copy.start()
        tile(wi_copy.wait, wf_copy.wait)

    @pl.when(jnp.logical_not(first))
    def _():
        tile()


def _ffn(y_r, y_d, y_f, x, mod, layer, row_fn, n2, fg, wo, wi, wf, final, tm):
    n_tok = x.shape[0]
    row = lambda i: (i, 0)
    const = lambda i: (0, 0)
    return pl.pallas_call(
        functools.partial(_ffn_body, final, layer),
        grid=(n_tok // tm,),
        in_specs=[pl.BlockSpec((tm, D_RWKV), row),
                  pl.BlockSpec((tm, D_DIFF), row),
                  pl.BlockSpec((tm, D_FNET), row),
                  pl.BlockSpec((tm, D_MODEL), row),
                  _mod_spec(layer, 2, row_fn),
                  _mod_spec(layer, 3, row_fn),
                  _mod_spec(layer, 4, row_fn),
                  _mod_spec(layer, 5, row_fn),
                  pl.BlockSpec((1, D_MODEL), const),
                  pl.BlockSpec((1, D_MODEL), const),
                  pl.BlockSpec((None,) + wo.shape[1:], lambda i: (layer, 0, 0), pipeline_mode=pl.Buffered(1)),
                  pl.BlockSpec(memory_space=pl.ANY),
                  pl.BlockSpec(memory_space=pl.ANY)],
        out_specs=pl.BlockSpec((tm, D_MODEL), row),
        out_shape=jax.ShapeDtypeStruct((n_tok, D_MODEL), F32),
        scratch_shapes=[pltpu.VMEM(wi.shape[1:], BF16), pltpu.VMEM(wf.shape[1:], BF16),
                        pltpu.SemaphoreType.DMA((2,))],
        compiler_params=_cp(1),
    )(y_r, y_d, y_f, x, mod, mod, mod, mod, n2.reshape(1, D_MODEL), fg.reshape(1, D_MODEL), wo, wi, wf)


def _block_diag_state(s):
    b = s.shape[0]
    s = s.reshape(b, 2, N_PAIR, 2, HEAD_DIM, HEAD_DIM)
    eye = jnp.eye(2, dtype=s.dtype)
    s = s[:, :, :, :, :, None, :] * eye[None, None, None, :, None, :, None]
    return s.reshape(b, 2, N_PAIR, LANES, LANES)


def kernel(x_prompt, x_sample, c, state_rwkv, cache_diff_k, cache_diff_v, c_ctx, norm1_g, norm2_g, final_norm_g, w_mod, b_mod, w_in, w_out, shift_mu, decay_w0, decay_up, iclr_a0, iclr_up, gate_up, k_k, k_a, r_k, lnx_g, lnx_b, diff_lambda, subln_g, w_ffn_in, w_ffn_out):
    p = dict(shift_mu=shift_mu, decay_w0=decay_w0, decay_up=decay_up, iclr_a0=iclr_a0, iclr_up=iclr_up,
             gate_up=gate_up, k_k=k_k, k_a=k_a, r_k=r_k, lnx_g=lnx_g, lnx_b=lnx_b)
    n_ctx, t_ctx, _ = x_prompt.shape
    n_dec, t_dec, _ = x_sample.shape
    past = cache_diff_k.shape[2]

    cond = jnp.concatenate([c_ctx[None, :], c, jnp.zeros((MOD_ROWS - 1 - n_dec, D_MODEL), F32)], axis=0)
    mod = _modulation(cond, w_mod, b_mod).reshape(DEPTH * MOD_ROWS, 1, 6 * D_MODEL)

    tm_ffn = FFN_TM
    ctx_plan = _stream_plan(n_ctx, t_ctx)
    dec_plan = _stream_plan(n_dec, t_dec)
    streams = [
        dict(x=x_prompt.reshape(n_ctx * t_ctx, D_MODEL), t=t_ctx, n=n_ctx, **ctx_plan,
             row_in=lambda i: 0, row_ffn=lambda i: 0),
        dict(x=x_sample.reshape(n_dec * t_dec, D_MODEL), t=t_dec, n=n_dec, **dec_plan,
             row_in=lambda i: 1 + i // (t_dec // dec_plan['tm_in']), row_ffn=lambda i: 1 + i // (t_dec // tm_ffn)),
    ]
    cos, sin = _rope_tables(t_dec)
    cache_k = cache_diff_k.reshape(n_dec, DEPTH, past, D_DIFF).astype(F32)
    cache_v = cache_diff_v.reshape(n_dec, DEPTH, past, D_DIFF).astype(F32)
    for st in streams:
        st['kv'] = st['states'] = None
    w_in_l = _w_in_rowmajor(w_in)
    wo = w_out.astype(BF16)
    wi = w_ffn_in.astype(BF16)
    wf = w_ffn_out.astype(BF16)
    for l in range(DEPTH):
        rw = _rwkv_weights(p, l)
        mu = jnp.concatenate([shift_mu[l], jnp.zeros((2, N_RWKV_PAD - N_RWKV_IN), F32)], axis=1)
        lam_init = 0.8 - 0.6 * math.exp(-0.3 * l)
        sg = jnp.tile(subln_g[l], 2).reshape(1, LANES)
        for si, st in enumerate(streams):
            u_r, q, k_all, v_all, u_f = _inproj(st['x'], mod, l, st['row_in'], norm1_g[l], w_in_l, mu,
                                                 st['kv'], st['t'], st['tm_in'])
            st['kv'] = (k_all, v_all)
            if si == 0:
                s0 = None
                attn_ctx = None
            else:
                s0 = state_rwkv[:, l].astype(F32)
                attn_ctx = (cache_k, cache_v, cos, sin)
                q, streams[0]['x'] = lax.optimization_barrier((q, streams[0]['x']))
            y_r, st['states'] = _rwkv(u_r, s0, st['t'], st['n_sub'], rw, st['states'], l)
            y_d = _attention(q, k_all, v_all, l, st['n_attn'], lam_init, diff_lambda[l], sg, attn_ctx)
            u_f, y_r = lax.optimization_barrier((u_f, y_r))
            y_f = _fnet(u_f, st['t'], st['n_fnet'])
            st['x'] = _ffn(y_r, y_d, y_f, st['x'], mod, l, st['row_ffn'], norm2_g[l], final_norm_g,
                           wo, wi, wf, l == DEPTH - 1, tm_ffn)
    y_prompt = streams[0]['x'].reshape(n_ctx, t_ctx, D_MODEL)
    y_sample = streams[1]['x'].reshape(n_dec, t_dec, D_MODEL)
    new_k = streams[0]['kv'][0].reshape(n_ctx, DEPTH, t_ctx, H_DIFF, 2, D_QK)
    new_v = streams[0]['kv'][1].reshape(n_ctx, DEPTH, t_ctx, H_DIFF, HEAD_DIM)
    return (y_prompt, y_sample, streams[0]['states'], new_k, new_v)
```

```python
import functools
import math

import numpy as np
import jax
import jax.numpy as jnp
from jax import lax
from jax.experimental import pallas as pl
from jax.experimental.pallas import tpu as pltpu

F32 = jnp.float32
BF16 = jnp.bfloat16

D_MODEL = 1024
DEPTH = 2
GRID_W = 64
HEAD_DIM = 64
D_RWKV = 384
H_RWKV = D_RWKV // HEAD_DIM
D_DIFF = 384
H_DIFF = D_DIFF // HEAD_DIM
D_QK = HEAD_DIM // 2
D_FNET = D_MODEL - D_RWKV - D_DIFF
FNET_GROUPS = 4
FNET_GROUP_DIM = D_FNET // FNET_GROUPS
LORA_W = 32
LORA_A = 32
LORA_G = 64
N_RWKV_IN = 3 * D_RWKV + 2 * LORA_W + 2 * LORA_A + LORA_G
N_DIFF_IN = 3 * D_DIFF
D_FF = ((8 * D_MODEL + 3 * 256 - 1) // (3 * 256)) * 256
ROPE_PAIRS = D_QK // 4
ROPE_BASE = 10000.0
RMS_EPS = 1e-6
GN_EPS = 64e-5
SUBLN_EPS = 1e-5
DECAY_SCALE = math.exp(-0.5)
Q_SCALE = D_QK ** -0.5 * math.log2(math.e)

LANES = 128
N_RWKV_PAD = 11 * LANES
D_IN_PAD = N_RWKV_PAD + N_DIFF_IN + D_FNET
N_PAIR = H_RWKV // 2
CHUNK = 64
N_PARTS = 2
SIDE_WORK_MAX_UNITS = 12
RWKV_BLK = 128
MOD_ROWS = 8
ONES_ROWS = 16
VMEM_LIMIT = 60 * 1024 * 1024

MOD_TN = 1536
INPROJ_TM = 512
FFN_TM = 512
ATTN_TQ = 256
ATTN_STEP_ROWS = 1024
KEY_BLOCK = 512
ATTN_LOOKAHEAD = 8
FNET_STEP_ROWS = 1024
RWKV_STEP_ROWS = 1024
RWKV_ONE_STEP_ROWS = 2048


def _stream_plan(n_seq, seq_len):
    one_step = n_seq * seq_len <= RWKV_ONE_STEP_ROWS
    return dict(n_sub=n_seq if one_step else max(1, RWKV_STEP_ROWS // seq_len),
                n_attn=max(1, ATTN_STEP_ROWS // seq_len),
                n_fnet=max(1, FNET_STEP_ROWS // seq_len),
                tm_in=max(seq_len, INPROJ_TM))


def _cp(n_axes=1):
    return pltpu.CompilerParams(dimension_semantics=("arbitrary",) * n_axes,
                                vmem_limit_bytes=VMEM_LIMIT)


def _bdot(a, b):
    return jnp.dot(a.astype(BF16), b.astype(BF16), preferred_element_type=F32)


def _bdot_nt(a, b):
    return lax.dot_general(a.astype(BF16), b.astype(BF16), (((1,), (1,)), ((), ())),
                           preferred_element_type=F32)


def _split2(x):
    hi = x.astype(BF16)
    lo = (x - hi.astype(F32)).astype(BF16)
    return hi, lo


def _dot_x3(a, b):
    a_hi, a_lo = _split2(a)
    b_hi, b_lo = _split2(b)
    d = functools.partial(jnp.dot, preferred_element_type=F32)
    return d(a_hi, b_hi) + d(a_lo, b_hi) + d(a_hi, b_lo)


def _rms(x, g):
    return x * lax.rsqrt(jnp.mean(x * x, axis=-1, keepdims=True) + RMS_EPS) * g


def _w_in_body(x_ref, o_ref):
    col = lax.broadcasted_iota(jnp.int32, (1, N_RWKV_PAD), 1)
    head = jnp.where(col < N_RWKV_IN, x_ref[0:N_RWKV_PAD, :].T, 0.0)
    tail = x_ref[N_RWKV_IN:, :].T
    o_ref[:, :N_RWKV_PAD] = head.astype(BF16)
    o_ref[:, N_RWKV_PAD:] = tail.astype(BF16)


def _w_in_rowmajor(w_in):
    w_t = jnp.swapaxes(w_in, 1, 2)
    n_layers, n_out, _ = w_t.shape
    return pl.pallas_call(
        _w_in_body,
        grid=(n_layers,),
        in_specs=[pl.BlockSpec((None, n_out, D_MODEL), lambda l: (l, 0, 0))],
        out_specs=pl.BlockSpec((None, D_MODEL, D_IN_PAD), lambda l: (l, 0, 0)),
        out_shape=jax.ShapeDtypeStruct((n_layers, D_MODEL, D_IN_PAD), BF16),
        compiler_params=_cp(1),
    )(w_t)


def _mod_body(c_ref, w_ref, b_ref, o_ref):
    c = c_ref[...]
    a = c * jax.nn.sigmoid(c)
    o_ref[0] = _dot_x3(a, w_ref[0]) + b_ref[0]


def _modulation(cond, w_mod, b_mod):
    n_layers, _, n_out = w_mod.shape
    tn = MOD_TN
    return pl.pallas_call(
        _mod_body,
        grid=(n_layers, n_out // tn),
        in_specs=[pl.BlockSpec((MOD_ROWS, D_MODEL), lambda l, j: (0, 0)),
                  pl.BlockSpec((1, D_MODEL, tn), lambda l, j: (l, 0, j)),
                  pl.BlockSpec((1, 1, tn), lambda l, j: (l, 0, j))],
        out_specs=pl.BlockSpec((1, MOD_ROWS, tn), lambda l, j: (l, 0, j)),
        out_shape=jax.ShapeDtypeStruct((n_layers, MOD_ROWS, n_out), F32),
        compiler_params=_cp(2),
    )(cond, w_mod, b_mod.reshape(n_layers, 1, n_out))


def _mod_spec(layer, col, row_fn):
    return pl.BlockSpec((1, 1, D_MODEL), lambda i: (layer * MOD_ROWS + row_fn(i), 0, col))


def _put_layer(ref, idx, layer, aliased, val):
    if aliased:
        ref[idx] = val
    else:
        for other in range(DEPTH):
            ref[idx + (other,)] = val if other == layer else jnp.zeros_like(val)


def _inproj_body(seq_len, layer, aliased, x_ref, g_ref, sh_ref, sc_ref, w_ref, mu_ref, *refs):
    ur_ref, q_ref, k_ref, v_ref, uf_ref = refs[2:] if aliased else refs
    part = x_ref.shape[0] // N_PARTS
    parts = [slice(j * part, (j + 1) * part) for j in range(N_PARTS)]
    h = [(_rms(x_ref[r, :], g_ref[...]) * (1.0 + sc_ref[0]) + sh_ref[0]).astype(BF16) for r in parts]
    u = jnp.concatenate([jnp.dot(hj, w_ref[...], preferred_element_type=F32) for hj in h], axis=0)
    ur = u[:, :N_RWKV_PAD]
    tm = ur.shape[0]
    pos = lax.broadcasted_iota(jnp.int32, (tm, 1), 0) & (seq_len - 1)
    prev = jnp.where(pos == 0, 0.0, pltpu.roll(ur, 1, axis=0))
    nxt = jnp.where(pos == seq_len - 1, 0.0, pltpu.roll(ur, tm - 1, axis=0))
    ur_ref[...] = (ur + mu_ref[0:1, :] * (prev - ur) + mu_ref[1:2, :] * (nxt - ur)).astype(BF16)
    q_ref[...] = (u[:, N_RWKV_PAD:N_RWKV_PAD + D_DIFF] * Q_SCALE).astype(BF16)
    for s in range(tm // seq_len):
        rows = slice(s * seq_len, (s + 1) * seq_len)
        _put_layer(k_ref, (s,), layer, aliased, u[rows, N_RWKV_PAD + D_DIFF:N_RWKV_PAD + 2 * D_DIFF])
        _put_layer(v_ref, (s,), layer, aliased, u[rows, N_RWKV_PAD + 2 * D_DIFF:N_RWKV_PAD + 3 * D_DIFF])
    uf_ref[...] = u[:, N_RWKV_PAD + N_DIFF_IN:].astype(BF16)


def _inproj(x, mod, layer, row_fn, g, w, mu, kv, seq_len, tm):
    n_tok = x.shape[0]
    kv_shape = jax.ShapeDtypeStruct((n_tok // seq_len, DEPTH, seq_len, D_DIFF), F32)
    assert tm % seq_len == 0 and seq_len & (seq_len - 1) == 0
    row = lambda i: (i, 0)
    const = lambda i: (0, 0)
    if kv is None:
        kv_spec = pl.BlockSpec((tm // seq_len, DEPTH, seq_len, D_DIFF), lambda i: (i, 0, 0, 0))
    else:
        kv_spec = pl.BlockSpec((tm // seq_len, None, seq_len, D_DIFF), lambda i: (i, layer, 0, 0))
    return pl.pallas_call(
        functools.partial(_inproj_body, seq_len, layer, kv is not None),
        grid=(n_tok // tm,),
        in_specs=[pl.BlockSpec((tm, D_MODEL), row),
                  pl.BlockSpec((1, D_MODEL), const),
                  _mod_spec(layer, 0, row_fn),
                  _mod_spec(layer, 1, row_fn),
                  pl.BlockSpec((None, D_MODEL, D_IN_PAD), lambda i: (layer, 0, 0)),
                  pl.BlockSpec((2, N_RWKV_PAD), const)]
        + [pl.BlockSpec(memory_space=pl.ANY)] * (0 if kv is None else 2),
        out_specs=[pl.BlockSpec((tm, N_RWKV_PAD), row),
                   pl.BlockSpec((tm, D_DIFF), row),
                   kv_spec, kv_spec,
                   pl.BlockSpec((tm, D_FNET), row)],
        out_shape=[jax.ShapeDtypeStruct((n_tok, N_RWKV_PAD), BF16),
                   jax.ShapeDtypeStruct((n_tok, D_DIFF), BF16),
                   kv_shape, kv_shape,
                   jax.ShapeDtypeStruct((n_tok, D_FNET), BF16)],
        input_output_aliases={} if kv is None else {6: 2, 7: 3},
        compiler_params=_cp(1),
    )(x, g.reshape(1, D_MODEL), mod, mod, w, mu, *(kv or ()))


def _lane_masks():
    lane = lax.broadcasted_iota(jnp.int32, (1, LANES), 1)
    return lane < HEAD_DIM, lane >= HEAD_DIM


def _rwkv_units(units, m0, m1, side=()):
    side = list(side)

    def run_side(drain=False):
        for gen in list(side):
            for _ in gen:
                if not drain:
                    break
            else:
                side.remove(gen)

    def bd(x):
        xb = x.astype(BF16)
        zero = jnp.zeros_like(xb)
        return jnp.concatenate([jnp.where(m0, xb, zero), jnp.where(m1, xb, zero)], axis=0)

    row = lax.broadcasted_iota(jnp.int32, (CHUNK, LANES), 0)
    col = lax.broadcasted_iota(jnp.int32, (CHUNK, LANES), 1) & (CHUNK - 1)
    eye = (col == row).astype(F32)
    r2 = lax.broadcasted_iota(jnp.int32, (LANES, LANES), 0) < HEAD_DIM
    c2 = lax.broadcasted_iota(jnp.int32, (LANES, LANES), 1) < HEAD_DIM
    rng = range(len(units))

    pre = []
    for rev, kk, r, v, kd, b, cum, cex, tot, s_prev in units:
        p_inv = jnp.exp(-cum)
        p_rem = jnp.exp(tot - cum)
        ab = -kk * jnp.exp(cex)
        rb = r * jnp.exp(cum)
        strict = (col > row) if rev else (col < row)
        incl = (col >= row) if rev else (col <= row)
        pre.append(dict(ab=ab, rb=rb, vbd=bd(v), strict=strict, incl=incl,
                        lhs=jnp.concatenate([ab, rb], axis=0),
                        rhs=jnp.concatenate([bd(b * p_inv), bd(kd * p_inv)], axis=0),
                        bk=jnp.concatenate([b * p_rem, kd * p_rem], axis=0)))

    run_side()
    mm = [_bdot_nt(q['lhs'], q['rhs']) for q in pre]
    run_side()
    m_ab = [jnp.where(pre[i]['strict'], mm[i][:CHUNK, :LANES], 0.0) for i in rng]
    m_ak = [jnp.where(pre[i]['strict'], mm[i][:CHUNK, LANES:], 0.0) for i in rng]
    m_r = [jnp.concatenate([jnp.where(pre[i]['incl'], mm[i][CHUNK:, :LANES], 0.0),
                            jnp.where(pre[i]['incl'], mm[i][CHUNK:, LANES:], 0.0)], axis=1) for i in rng]
    mv = [_bdot(m_ak[i], pre[i]['vbd']) for i in rng]
    run_side()

    t = [eye + m_ab[i] for i in rng]
    n = [_bdot(m_ab[i], bd(m_ab[i])) for i in rng]
    for _ in range(4):
        x = [_bdot(jnp.concatenate([t[i], n[i]], axis=0), bd(n[i])) for i in rng]
        t = [t[i] + x[i][:CHUNK] for i in rng]
        n = [x[i][CHUNK:] for i in rng]
        run_side()
    t = [t[i] + _bdot(t[i], bd(n[i])) for i in rng]

    w = [_bdot(t[i], jnp.concatenate([bd(pre[i]['ab']), bd(mv[i])], axis=1)) for i in rng]
    xs = [_bdot_nt(jnp.concatenate([w[i][:, :LANES], pre[i]['rb']], axis=0), units[i][9]) for i in rng]
    u = [xs[i][:CHUNK] + w[i][:, LANES:] for i in rng]
    run_side(drain=True)
    y = [xs[i][CHUNK:] + _bdot(m_r[i], jnp.concatenate([bd(u[i]), pre[i]['vbd']], axis=0)) for i in rng]
    z = [_bdot(jnp.concatenate([u[i], units[i][3]], axis=0).T, pre[i]['bk']) for i in rng]
    s_new = [units[i][9] * jnp.exp(units[i][8]) + jnp.where(r2 == c2, z[i], 0.0) for i in rng]
    return y, s_new


def _rwkv_body(seq_len, n_sub, layer, has_s0, aliased, u_ref, *refs):
    s0_ref = refs[0] if has_s0 else None
    refs = refs[1:] if has_s0 else refs
    wdec_ref, wicl_ref, wg_ref, w0a0_ref, vec_ref, bo_ref, tril_ref, triu_ref = refs[:8]
    (y_ref, sfin_ref, r_s, v_s, kk_s, kd_s, b_s, ci_s, ce_s, y_s, st_s) = refs[9 if aliased else 8:]
    n_chunk = seq_len // CHUNK
    assert n_chunk % 2 == 0 and n_chunk >= 4
    k_k = vec_ref[0:1, :]
    k_a = vec_ref[1:2, :]
    r_k = vec_ref[2:3, :]
    lnx_g = vec_ref[3:4, :]
    lnx_b = vec_ref[4:5, :]
    bo = bo_ref[...]

    def headsum(xb):
        return jnp.concatenate([jnp.dot(xb[:, p * LANES:(p + 1) * LANES], bo, preferred_element_type=F32)
                                for p in range(N_PAIR)], axis=1)

    def prep(rows):
        xs = u_ref[rows, :].astype(F32)
        r = xs[:, 0:D_RWKV]
        k = xs[:, D_RWKV:2 * D_RWKV]
        v = xs[:, 2 * D_RWKV:3 * D_RWKV]
        lora = xs[:, 3 * D_RWKV:3 * D_RWKV + LANES]
        t_lora = jnp.tanh(lora).astype(BF16)
        lora = lora.astype(BF16)
        kk = k * k_k
        kk2 = (kk * kk).astype(BF16)
        yield
        dec = _bdot(t_lora, wdec_ref[...])
        icl = _bdot(lora, wicl_ref[...])
        ss = headsum(kk2)
        yield
        logw = -DECAY_SCALE * jax.nn.sigmoid(w0a0_ref[0:1, :] + dec)
        a = jax.nn.sigmoid(w0a0_ref[1:2, :] + icl)
        kk = kk / jnp.maximum(jnp.sqrt(ss), 1e-12)
        a_f = a[:, :D_RWKV]
        a_b = a[:, D_RWKV:]
        kka = k * k_a
        kd_f = (k - kka) + kka * a_f
        kd_b = (k - kka) + kka * a_b
        lws = [_split2(logw[:, d * D_RWKV:(d + 1) * D_RWKV]) for d in range(2)]
        r_s[rows, :] = r
        v_s[rows, :] = v
        kk_s[rows, :] = kk
        kd_s[0, rows, :] = kd_f
        kd_s[1, rows, :] = kd_b
        b_s[0, rows, :] = kk * a_f
        b_s[1, rows, :] = kk * a_b
        yield
        dd = functools.partial(jnp.dot, preferred_element_type=F32)
        cums = [dd(tri_ref[...], lws[d][0]) + dd(tri_ref[...], lws[d][1])
                for d, tri_ref in enumerate((tril_ref, triu_ref))]
        yield
        for d in range(2):
            ci_s[d, rows, :] = cums[d]
            ce_s[d, rows, :] = cums[d] - logw[:, d * D_RWKV:(d + 1) * D_RWKV]

    def post(rows):
        y = y_s[rows, :]
        yb = y.astype(BF16)
        v = v_s[rows, :]
        bon = _split2(r_s[rows, :] * (kd_s[0, rows, :] + kd_s[1, rows, :]) * r_k)
        gd = u_ref[rows, 3 * D_RWKV + LANES:3 * D_RWKV + 2 * LANES].astype(F32)
        s_gd = jax.nn.sigmoid(gd).astype(BF16)
        yield
        mean = headsum(yb) * (1.0 / HEAD_DIM)
        bonus = (headsum(bon[0]) + headsum(bon[1])) * v
        g = _bdot(s_gd, wg_ref[...])
        yield
        yc = y - mean
        yc2 = (yc * yc).astype(BF16)
        yield
        var = headsum(yc2) * (1.0 / HEAD_DIM)
        yield
        yn = yc * lax.rsqrt(var + GN_EPS) * lnx_g + lnx_b
        y_ref[rows, :] = ((yn + bonus) * g).astype(BF16)

    def run_all(gens):
        gens = list(gens)
        while gens:
            gens = [gen for gen in gens if next(gen, gens) is not gens]

    def chunk_rows(i):
        return [pl.ds(pl.multiple_of(s * seq_len + (i if d == 0 else n_chunk - 1 - i) * CHUNK, CHUNK), CHUNK)
                for s in range(n_sub) for d in range(2)]

    st_s[...] = s0_ref[...] if has_s0 else jnp.zeros_like(st_s)
    y_s[...] = jnp.zeros_like(y_s)
    m0, m1 = _lane_masks()

    def scan_step(i, prep_next, post_prev):
        rows_sd = chunk_rows(i)
        side = []
        if prep_next:
            side = [prep(rows) for rows in chunk_rows(i + 1)]
        if post_prev:
            side = [post(rows) for rows in chunk_rows(i - 1)]
        units = []
        for s in range(n_sub):
            for d in range(2):
                rows = rows_sd[s * 2 + d]
                cum = ci_s[d, rows, :]
                cex = ce_s[d, rows, :]
                tot = cum[CHUNK - 1:CHUNK] if d == 0 else cum[0:1]
                kk = kk_s[rows, :]
                r = r_s[rows, :]
                v = v_s[rows, :]
                kd = kd_s[d, rows, :]
                b = b_s[d, rows, :]
                for p in range(N_PAIR):
                    sl = slice(p * LANES, (p + 1) * LANES)
                    units.append((d == 1, kk[:, sl], r[:, sl], v[:, sl], kd[:, sl], b[:, sl],
                                  cum[:, sl], cex[:, sl], tot[:, sl], st_s[s, d, p]))
        ys, s_new = _rwkv_units(units, m0, m1, side)
        for s in range(n_sub):
            for d in range(2):
                base = (s * 2 + d) * N_PAIR
                for p in range(N_PAIR):
                    st_s[s, d, p] = s_new[base + p]
                rows = rows_sd[s * 2 + d]
                y_s[rows, :] = y_s[rows, :] + jnp.concatenate(ys[base:base + N_PAIR], axis=1)

    def loop(lo, hi, **kw):
        lax.fori_loop(lo, hi, lambda i, c: (scan_step(i, **kw), c)[1], 0)

    half = n_chunk // 2
    blk = tril_ref.shape[0]
    if blk == CHUNK:
        run_all(prep(rows) for rows in chunk_rows(0))
        loop(0, half - 1, prep_next=True, post_prev=False)
        loop(half - 1, half + 1, prep_next=False, post_prev=False)
        loop(half + 1, n_chunk, prep_next=False, post_prev=True)
        run_all(post(rows) for rows in chunk_rows(n_chunk - 1))
    else:
        blocks = [slice(r0, r0 + blk) for r0 in range(0, n_sub * seq_len, blk)]
        run_all(prep(rows) for rows in blocks)
        loop(0, n_chunk, prep_next=False, post_prev=False)
        run_all(post(rows) for rows in blocks)

    for s in range(n_sub):
        for d in range(2):
            for p in range(N_PAIR):
                st = st_s[s, d, p]
                for h in range(2):
                    rows = slice(h * HEAD_DIM, (h + 1) * HEAD_DIM)
                    val = st[rows, rows]
                    if aliased:
                        sfin_ref[s, d, 2 * p + h] = val
                    else:
                        for other in range(DEPTH):
                            sfin_ref[s, other, d, 2 * p + h] = val if other == layer else jnp.zeros_like(val)


def _rwkv(u_r, s0, seq_len, n_sub, wts, states, layer):
    n_seq = u_r.shape[0] // seq_len
    rows = n_sub * seq_len
    const2 = lambda b: (0, 0)
    st_shape = (n_sub, 2, N_PAIR, LANES, LANES)
    tok = pltpu.VMEM((rows, D_RWKV), F32)
    tok2 = pltpu.VMEM((2, rows, D_RWKV), F32)
    side_work = 2 * N_PAIR * n_sub <= SIDE_WORK_MAX_UNITS
    wts = list(wts) + _chunk_triangles(CHUNK if side_work else RWKV_BLK)
    single = n_seq == n_sub
    in_specs = [pl.BlockSpec((rows, N_RWKV_PAD), lambda b: (b, 0),
                             pipeline_mode=pl.Buffered(1) if single else None)]
    args = [u_r]
    if s0 is not None:
        in_specs.append(pl.BlockSpec(st_shape, lambda b: (b, 0, 0, 0, 0)))
        args.append(_block_diag_state(s0))
    in_specs += [pl.BlockSpec(w.shape, const2) for w in wts]
    args += list(wts)
    if states is not None:
        in_specs.append(pl.BlockSpec(memory_space=pl.ANY))
        args.append(states)
    return pl.pallas_call(
        functools.partial(_rwkv_body, seq_len, n_sub, layer, s0 is not None, states is not None),
        grid=(n_seq // n_sub,),
        in_specs=in_specs,
        out_specs=[pl.BlockSpec((rows, D_RWKV), lambda b: (b, 0),
                                pipeline_mode=pl.Buffered(1) if single else None),
                   pl.BlockSpec((n_sub, DEPTH, 2, H_RWKV, HEAD_DIM, HEAD_DIM), lambda b: (b, 0, 0, 0, 0, 0))
                   if states is None else
                   pl.BlockSpec((n_sub, None, 2, H_RWKV, HEAD_DIM, HEAD_DIM),
                                lambda b: (b, layer, 0, 0, 0, 0))],
        out_shape=[jax.ShapeDtypeStruct((n_seq * seq_len, D_RWKV), BF16),
                   jax.ShapeDtypeStruct((n_seq, DEPTH, 2, H_RWKV, HEAD_DIM, HEAD_DIM), F32)],
        scratch_shapes=[tok] * 3 + [tok2] * 4 + [tok, pltpu.VMEM(st_shape, F32)],
        input_output_aliases={} if states is None else {len(args) - 1: 1},
        compiler_params=_cp(1),
    )(*args)


def _rwkv_weights(p, l):
    z = functools.partial(jnp.zeros, dtype=F32)
    wdec = z((LANES, 2 * D_RWKV))
    wdec = wdec.at[0:LORA_W, :D_RWKV].set(p['decay_up'][l, 0])
    wdec = wdec.at[LORA_W:2 * LORA_W, D_RWKV:].set(p['decay_up'][l, 1])
    wicl = z((LANES, 2 * D_RWKV))
    wicl = wicl.at[2 * LORA_W:2 * LORA_W + LORA_A, :D_RWKV].set(p['iclr_up'][l, 0])
    wicl = wicl.at[2 * LORA_W + LORA_A:2 * LORA_W + 2 * LORA_A, D_RWKV:].set(p['iclr_up'][l, 1])
    wg = z((LANES, D_RWKV)).at[0:LORA_G].set(p['gate_up'][l])
    w0a0 = jnp.stack([p['decay_w0'][l].reshape(-1), p['iclr_a0'][l].reshape(-1)])
    vec = jnp.stack([p['k_k'][l], p['k_a'][l], p['r_k'][l].reshape(-1), p['lnx_g'][l], p['lnx_b'][l],
                     z((D_RWKV,)), z((D_RWKV,)), z((D_RWKV,))])
    head = np.arange(LANES) // HEAD_DIM
    bo = jnp.asarray(head[:, None] == head[None, :], BF16)
    return [wdec.astype(BF16), wicl.astype(BF16), wg.astype(BF16), w0a0, vec, bo]


def _chunk_triangles(rows):
    idx = np.arange(rows)
    same = idx[None, :] // CHUNK == idx[:, None] // CHUNK
    return [jnp.asarray(same & (idx[None, :] <= idx[:, None]), BF16),
            jnp.asarray(same & (idx[None, :] >= idx[:, None]), BF16)]


def _rope(x, cos, sin):
    lane = lax.broadcasted_iota(jnp.int32, (1, LANES), 1)
    first_half = (lane & ROPE_PAIRS) == 0
    partner = jnp.where(first_half, pltpu.roll(x, LANES - ROPE_PAIRS, axis=1),
                        pltpu.roll(x, ROPE_PAIRS, axis=1))
    return x * cos + partner * sin


def _attn_body(has_ctx, lam_init, *refs):
    if has_ctx:
        (q_ref, k_ref, v_ref, lp_ref, sg_ref, kc_ref, vc_ref, cq_ref, sq_ref, ck_ref, sk_ref,
         o_ref) = refs
    else:
        q_ref, k_ref, v_ref, lp_ref, sg_ref, o_ref = refs
    n_pair = D_DIFF // LANES
    lp = lp_ref[...]
    lam = (jnp.exp(jnp.sum(lp[0:1] * lp[1:2], axis=-1, keepdims=True))
           - jnp.exp(jnp.sum(lp[2:3] * lp[3:4], axis=-1, keepdims=True)) + lam_init)
    lane = lax.broadcasted_iota(jnp.int32, (1, LANES), 1)

    tq = q_ref.shape[0] // k_ref.shape[0]

    def operands(seq, p):
        sl = slice(p * LANES, (p + 1) * LANES)
        q = q_ref[seq * tq:(seq + 1) * tq, sl]
        k = k_ref[seq, :, sl]
        v = v_ref[seq, :, sl]
        if has_ctx:
            q = _rope(q.astype(F32), cq_ref[...], sq_ref[...])
            k = _rope(k, ck_ref[...], sk_ref[...])
            k = jnp.concatenate([kc_ref[0, :, sl], k], axis=0)
            v = jnp.concatenate([vc_ref[0, :, sl], v], axis=0)
        vt = v.T
        ones = jnp.ones((ONES_ROWS, v.shape[0]), F32)
        vts = [jnp.concatenate([vt[h * HEAD_DIM:(h + 1) * HEAD_DIM], ones], axis=0).astype(BF16)
               for h in range(2)]
        return q.astype(BF16), k.astype(BF16), vts

    n_keys = k_ref.shape[1] + (kc_ref.shape[1] if has_ctx else 0)
    blocks = range(0, n_keys, KEY_BLOCK)
    ops, tiles = {}, []
    for seq in range(k_ref.shape[0]):
        for p in range(n_pair):
            for j in blocks:
                tiles += [(seq, p, h, m, j) for h in range(2) for m in range(2)]

    def score(tile):
        seq, p, h, m, j = tile
        if (seq, p) not in ops:
            ops[seq, p] = operands(seq, p)
        q, k, _ = ops[seq, p]
        lo = h * HEAD_DIM + m * D_QK
        sel = (lane >= lo) & (lane < lo + D_QK)
        return _bdot_nt(k[j:j + KEY_BLOCK], jnp.where(sel, q, jnp.zeros_like(q)))

    run_max, acc = {}, {}

    def consume(tile, s):
        seq, p, h, m, j = tile
        vt = ops[seq, p][2][h][:, j:j + KEY_BLOCK]
        c = (seq, p, h, m)
        mj = jnp.max(s, axis=0, keepdims=True)
        if j == 0:
            run_max[c] = mj
            acc[c] = jnp.dot(vt, jnp.exp2(s - mj).astype(BF16), preferred_element_type=F32)
        else:
            m_new = jnp.maximum(run_max[c], mj)
            acc[c] = (acc[c] * jnp.exp2(run_max[c] - m_new)
                      + jnp.dot(vt, jnp.exp2(s - m_new).astype(BF16), preferred_element_type=F32))
            run_max[c] = m_new
        if j == blocks[-1] and (h, m) == (1, 1):
            finish(seq, p)

    def finish(seq, p):
        halves = []
        for h in range(2):
            a0, a1 = acc.pop((seq, p, h, 0)), acc.pop((seq, p, h, 1))
            o = (a0[:HEAD_DIM] * (1.0 / a0[HEAD_DIM:HEAD_DIM + 1])
                 - lam * (a1[:HEAD_DIM] * (1.0 / a1[HEAD_DIM:HEAD_DIM + 1])))
            ms = jnp.mean(o * o, axis=0, keepdims=True)
            halves.append(o * lax.rsqrt(ms + SUBLN_EPS))
        o_ref[seq * tq:(seq + 1) * tq, p * LANES:(p + 1) * LANES] = (
            jnp.concatenate(halves, axis=0).T * sg_ref[...] * (1.0 - lam_init)).astype(BF16)

    pending = []
    for tile in tiles:
        pending.append((tile, score(tile)))
        if len(pending) > ATTN_LOOKAHEAD:
            consume(*pending.pop(0))
    for item in pending:
        consume(*item)


def _attention(q, k_all, v_all, layer, n_sub, lam_init, lp, sg, ctx=None):
    n_tok = q.shape[0]
    n_seq, _, seq_len, _ = k_all.shape
    tq = ATTN_TQ
    nq = seq_len // tq
    assert n_sub == 1 or nq == 1
    kv_spec = pl.BlockSpec((n_sub, None, seq_len, D_DIFF), lambda b, i: (b, layer, 0, 0))
    in_specs = [pl.BlockSpec((n_sub * tq, D_DIFF), lambda b, i: (b * nq + i, 0)),
                kv_spec, kv_spec,
                pl.BlockSpec(lp.shape, lambda b, i: (0, 0)),
                pl.BlockSpec((1, LANES), lambda b, i: (0, 0))]
    args = [q, k_all, v_all, lp, sg]
    if ctx is not None:
        kc, vc, cos, sin = ctx
        past = kc.shape[2]
        in_specs += [pl.BlockSpec((1, None, past, D_DIFF), lambda b, i: (b, layer, 0, 0)),
                     pl.BlockSpec((1, None, past, D_DIFF), lambda b, i: (b, layer, 0, 0)),
                     pl.BlockSpec((tq, LANES), lambda b, i: (i, 0)),
                     pl.BlockSpec((tq, LANES), lambda b, i: (i, 0)),
                     pl.BlockSpec((seq_len, LANES), lambda b, i: (0, 0)),
                     pl.BlockSpec((seq_len, LANES), lambda b, i: (0, 0))]
        args += [kc, vc, cos, sin, cos, sin]
    return pl.pallas_call(
        functools.partial(_attn_body, ctx is not None, lam_init),
        grid=(n_seq // n_sub, nq),
        in_specs=in_specs,
        out_specs=pl.BlockSpec((n_sub * tq, D_DIFF), lambda b, i: (b * nq + i, 0)),
        out_shape=jax.ShapeDtypeStruct((n_tok, D_DIFF), BF16),
        compiler_params=_cp(2),
    )(*args)


def _rope_tables(seq_len):
    f32 = np.float32
    t = np.arange(seq_len)
    pos = np.stack([t // GRID_W, t % GRID_W], axis=1).astype(f32)
    inv = (f32(1.0) / (f32(ROPE_BASE) ** (np.arange(ROPE_PAIRS, dtype=f32) / f32(ROPE_PAIRS)))).astype(f32)
    ang = pos[:, :, None] * inv
    d = np.arange(LANES) % D_QK
    axis = d // (2 * ROPE_PAIRS)
    second = (d % (2 * ROPE_PAIRS)) // ROPE_PAIRS
    idx = d % ROPE_PAIRS
    cos = np.cos(ang)[:, axis, idx]
    sin = np.sin(ang)[:, axis, idx] * np.where(second == 1, 1.0, -1.0).astype(f32)
    return jnp.asarray(cos, F32), jnp.asarray(sin, F32)


def _fnet_body(seq_len, x_ref, ct_ref, st_ref, cc_ref, sc_ref, o_ref):
    n_sub = x_ref.shape[0] // seq_len
    x = x_ref[...]
    xc = jnp.dot(x, cc_ref[...], preferred_element_type=F32)
    xs = jnp.dot(x, sc_ref[...], preferred_element_type=F32)
    wide = lambda a: jnp.concatenate([a[s * seq_len:(s + 1) * seq_len] for s in range(n_sub)], axis=1)
    y = _bdot(ct_ref[...], wide(xc)) - _bdot(st_ref[...], wide(xs))
    for s in range(n_sub):
        o_ref[s * seq_len:(s + 1) * seq_len, :] = y[:, s * D_FNET:(s + 1) * D_FNET].astype(BF16)


def _dft_consts(n, block=1):
    idx = np.arange(n)
    ang = 2.0 * np.pi * ((idx[:, None] * idx[None, :]) % n) / n
    return [jnp.asarray(np.kron(np.eye(block), m).astype(np.float32)).astype(BF16)
            for m in (np.cos(ang) / np.sqrt(n), np.sin(ang) / np.sqrt(n))]


def _fnet(u_f, seq_len, n_sub):
    n_tok = u_f.shape[0]
    rows = n_sub * seq_len
    consts = _dft_consts(seq_len) + _dft_consts(FNET_GROUP_DIM, FNET_GROUPS)
    const = lambda b: (0, 0)
    return pl.pallas_call(
        functools.partial(_fnet_body, seq_len),
        grid=(n_tok // rows,),
        in_specs=[pl.BlockSpec((rows, D_FNET), lambda b: (b, 0))]
        + [pl.BlockSpec(c.shape, const) for c in consts],
        out_specs=pl.BlockSpec((rows, D_FNET), lambda b: (b, 0)),
        out_shape=jax.ShapeDtypeStruct((n_tok, D_FNET), BF16),
        compiler_params=_cp(1),
    )(u_f, *consts)


def _ffn_body(final, layer, cast_wf, yr_ref, yd_ref, yf_ref, x_ref, g1_ref, sh2_ref, sc2_ref, g2_ref, n2_ref,
              fg_ref, wo_ref, wi_hbm, wf_hbm, o_ref, *rest):
    if cast_wf:
        wf_out, wi_ref, wf_ref, wf32_ref, sem = rest
    else:
        wi_ref, wf_ref, sem = rest
    part = x_ref.shape[0] // N_PARTS
    parts = [slice(j * part, (j + 1) * part) for j in range(N_PARTS)]

    def tile(before_ffn_in=None, before_ffn_out=None):
        y = [jnp.dot(jnp.concatenate([yr_ref[r, :], yd_ref[r, :], yf_ref[r, :]], axis=1), wo_ref[...],
                     preferred_element_type=F32) for r in parts]
        x = [x_ref[r, :] + g1_ref[0] * y[j] for j, r in enumerate(parts)]
        h = [(_rms(xj, n2_ref[...]) * (1.0 + sc2_ref[0]) + sh2_ref[0]).astype(BF16) for xj in x]
        if before_ffn_in is not None:
            before_ffn_in()
        z = [jnp.dot(hj, wi_ref[...], preferred_element_type=F32) for hj in h]
        act = [zj[:, :D_FF] * jax.nn.sigmoid(zj[:, :D_FF]) * zj[:, D_FF:] for zj in z]
        if before_ffn_out is not None:
            before_ffn_out()
        f = [_bdot(aj, wf_ref[...]) for aj in act]
        for j, r in enumerate(parts):
            xj = x[j] + g2_ref[0] * f[j]
            o_ref[r, :] = _rms(xj, fg_ref[...]) if final else xj

    first = pl.program_id(0) == 0
    wi_copy = pltpu.make_async_copy(wi_hbm.at[layer], wi_ref, sem.at[0])
    if cast_wf:
        wf_copy = pltpu.make_async_copy(wf_hbm.at[layer], wf32_ref, sem.at[1])
        wf_store = pltpu.make_async_copy(wf_ref, wf_out, sem.at[2])

        def wf_arrived():
            wf_copy.wait()
            wf_ref[...] = wf32_ref[...].astype(BF16)
            wf_store.start()
    else:
        wf_copy = pltpu.make_async_copy(wf_hbm, wf_ref, sem.at[1])
        wf_arrived = wf_copy.wait

    @pl.when(first)
    def _():
        wi_copy.start()
        wf_copy.start()
        tile(wi_copy.wait, wf_arrived)
        if cast_wf:
            wf_store.wait()

    @pl.when(jnp.logical_not(first))
    def _():
        tile()


def _ffn(y_r, y_d, y_f, x, mod, layer, row_fn, n2, fg, wo, wi, wf, final, tm):
    n_tok = x.shape[0]
    row = lambda i: (i, 0)
    const = lambda i: (0, 0)
    cast_wf = wf.ndim == 3
    wf_shape = wf.shape[-2:]
    x_spec = pl.BlockSpec((tm, D_MODEL), row)
    x_shape = jax.ShapeDtypeStruct((n_tok, D_MODEL), F32)
    out = pl.pallas_call(
        functools.partial(_ffn_body, final, layer, cast_wf),
        grid=(n_tok // tm,),
        in_specs=[pl.BlockSpec((tm, D_RWKV), row),
                  pl.BlockSpec((tm, D_DIFF), row),
                  pl.BlockSpec((tm, D_FNET), row),
                  pl.BlockSpec((tm, D_MODEL), row),
                  _mod_spec(layer, 2, row_fn),
                  _mod_spec(layer, 3, row_fn),
                  _mod_spec(layer, 4, row_fn),
                  _mod_spec(layer, 5, row_fn),
                  pl.BlockSpec((1, D_MODEL), const),
                  pl.BlockSpec((1, D_MODEL), const),
                  pl.BlockSpec((None,) + wo.shape[1:], lambda i: (layer, 0, 0), pipeline_mode=pl.Buffered(1)),
                  pl.BlockSpec(memory_space=pl.ANY),
                  pl.BlockSpec(memory_space=pl.ANY)],
        out_specs=[x_spec, pl.BlockSpec(memory_space=pl.ANY)] if cast_wf else x_spec,
        out_shape=[x_shape, jax.ShapeDtypeStruct(wf_shape, BF16)] if cast_wf else x_shape,
        scratch_shapes=[pltpu.VMEM(wi.shape[1:], BF16), pltpu.VMEM(wf_shape, BF16)]
        + ([pltpu.VMEM(wf_shape, F32)] if cast_wf else [])
        + [pltpu.SemaphoreType.DMA((3 if cast_wf else 2,))],
        compiler_params=_cp(1),
    )(y_r, y_d, y_f, x, mod, mod, mod, mod, n2.reshape(1, D_MODEL), fg.reshape(1, D_MODEL), wo, wi, wf)
    return tuple(out) if cast_wf else (out, wf)


def _block_diag_state(s):
    b = s.shape[0]
    s = s.reshape(b, 2, N_PAIR, 2, HEAD_DIM, HEAD_DIM)
    eye = jnp.eye(2, dtype=s.dtype)
    s = s[:, :, :, :, :, None, :] * eye[None, None, None, :, None, :, None]
    return s.reshape(b, 2, N_PAIR, LANES, LANES)


def kernel(x_prompt, x_sample, c, state_rwkv, cache_diff_k, cache_diff_v, c_ctx, norm1_g, norm2_g, final_norm_g, w_mod, b_mod, w_in, w_out, shift_mu, decay_w0, decay_up, iclr_a0, iclr_up, gate_up, k_k, k_a, r_k, lnx_g, lnx_b, diff_lambda, subln_g, w_ffn_in, w_ffn_out):
    p = dict(shift_mu=shift_mu, decay_w0=decay_w0, decay_up=decay_up, iclr_a0=iclr_a0, iclr_up=iclr_up,
             gate_up=gate_up, k_k=k_k, k_a=k_a, r_k=r_k, lnx_g=lnx_g, lnx_b=lnx_b)
    n_ctx, t_ctx, _ = x_prompt.shape
    n_dec, t_dec, _ = x_sample.shape
    past = cache_diff_k.shape[2]

    cond = jnp.concatenate([c_ctx[None, :], c, jnp.zeros((MOD_ROWS - 1 - n_dec, D_MODEL), F32)], axis=0)
    mod = _modulation(cond, w_mod, b_mod).reshape(DEPTH * MOD_ROWS, 1, 6 * D_MODEL)

    tm_ffn = FFN_TM
    ctx_plan = _stream_plan(n_ctx, t_ctx)
    dec_plan = _stream_plan(n_dec, t_dec)
    streams = [
        dict(x=x_prompt.reshape(n_ctx * t_ctx, D_MODEL), t=t_ctx, n=n_ctx, **ctx_plan,
             row_in=lambda i: 0, row_ffn=lambda i: 0),
        dict(x=x_sample.reshape(n_dec * t_dec, D_MODEL), t=t_dec, n=n_dec, **dec_plan,
             row_in=lambda i: 1 + i // (t_dec // dec_plan['tm_in']), row_ffn=lambda i: 1 + i // (t_dec // tm_ffn)),
    ]
    cos, sin = _rope_tables(t_dec)
    cache_k = cache_diff_k.reshape(n_dec, DEPTH, past, D_DIFF).astype(F32)
    cache_v = cache_diff_v.reshape(n_dec, DEPTH, past, D_DIFF).astype(F32)
    for st in streams:
        st['kv'] = st['states'] = None
    w_in_l = _w_in_rowmajor(w_in)
    wo = w_out.astype(BF16)
    wi = w_ffn_in.astype(BF16)
    for l in range(DEPTH):
        rw = _rwkv_weights(p, l)
        mu = jnp.concatenate([shift_mu[l], jnp.zeros((2, N_RWKV_PAD - N_RWKV_IN), F32)], axis=1)
        lam_init = 0.8 - 0.6 * math.exp(-0.3 * l)
        sg = jnp.tile(subln_g[l], 2).reshape(1, LANES)
        for si, st in enumerate(streams):
            u_r, q, k_all, v_all, u_f = _inproj(st['x'], mod, l, st['row_in'], norm1_g[l], w_in_l, mu,
                                                 st['kv'], st['t'], st['tm_in'])
            st['kv'] = (k_all, v_all)
            if si == 0:
                s0 = None
                attn_ctx = None
            else:
                s0 = state_rwkv[:, l].astype(F32)
                attn_ctx = (cache_k, cache_v, cos, sin)
                q, streams[0]['x'] = lax.optimization_barrier((q, streams[0]['x']))
            y_r, st['states'] = _rwkv(u_r, s0, st['t'], st['n_sub'], rw, st['states'], l)
            y_d = _attention(q, k_all, v_all, l, st['n_attn'], lam_init, diff_lambda[l], sg, attn_ctx)
            u_f, y_r = lax.optimization_barrier((u_f, y_r))
            y_f = _fnet(u_f, st['t'], st['n_fnet'])
            st['x'], wf = _ffn(y_r, y_d, y_f, st['x'], mod, l, st['row_ffn'], norm2_g[l], final_norm_g,
                               wo, wi, w_ffn_out if si == 0 else wf, l == DEPTH - 1, tm_ffn)
    y_prompt = streams[0]['x'].reshape(n_ctx, t_ctx, D_MODEL)
    y_sample = streams[1]['x'].reshape(n_dec, t_dec, D_MODEL)
    new_k = streams[0]['kv'][0].reshape(n_ctx, DEPTH, t_ctx, H_DIFF, 2, D_QK)
    new_v = streams[0]['kv'][1].reshape(n_ctx, DEPTH, t_ctx, H_DIFF, HEAD_DIM)
    return (y_prompt, y_sample, streams[0]['states'], new_k, new_v)
```
